```python
import math
import jax
import jax.numpy as jnp
from jax import lax
import numpy as np

D_MODEL = 1024
BATCH = 8
SEQ = 4096
DEPTH = 1

HEAD_DIM = 64
SWA_Q_HEADS = 8
SWA_KV_HEADS = 2
SWA_GROUP = SWA_Q_HEADS // SWA_KV_HEADS
SB_HEADS = 8
WINDOW = 128
BLOCK = 128
ROPE_THETA = 10000.0
D_FF = 2816
CONV_WIDTH = 3
N_BRANCHES = 2
LN_EPS = 1e-5
DEEPNORM_ALPHA = (2.0 * DEPTH) ** 0.25
DEEPNORM_BETA = (8.0 * DEPTH) ** -0.25

SWA_Q_WIDTH = SWA_Q_HEADS * HEAD_DIM
SWA_KV_WIDTH = SWA_KV_HEADS * HEAD_DIM
SB_WIDTH = SB_HEADS * HEAD_DIM
GATE_WIDTH = N_BRANCHES * D_MODEL
IN_WIDTHS = [SWA_Q_WIDTH, SWA_KV_WIDTH, SWA_KV_WIDTH, SB_WIDTH, SB_WIDTH, SB_WIDTH, GATE_WIDTH]
IN_SPLITS = [int(v) for v in np.cumsum(IN_WIDTHS)[:-1]]
IN_TOTAL = int(sum(IN_WIDTHS))

kernel_name = 'hybrid_swa_sink_stickbreaking_convffn_deepnorm'


def layer_norm(x, g, b):
    xf = x.astype(jnp.float32)
    mu = jnp.mean(xf, axis=-1, keepdims=True)
    xc = xf - mu
    var = jnp.mean(xc * xc, axis=-1, keepdims=True)
    y = xc * lax.rsqrt(var + LN_EPS) * g.astype(jnp.float32) + b.astype(jnp.float32)
    return y.astype(x.dtype)


def rotary_tables(positions):
    inv_freq = 1.0 / (ROPE_THETA ** (jnp.arange(0, HEAD_DIM, 2, dtype=jnp.float32) / HEAD_DIM))
    ang = positions.astype(jnp.float32)[..., None] * inv_freq
    return jnp.cos(ang)[:, :, None, :], jnp.sin(ang)[:, :, None, :]


def apply_rope(t, cos, sin):
    tf = t.astype(jnp.float32)
    t1, t2 = jnp.split(tf, 2, axis=-1)
    out = jnp.concatenate([t1 * cos - t2 * sin, t2 * cos + t1 * sin], axis=-1)
    return out.astype(t.dtype)


def sliding_window_sink_attention(q, k, v, sinks):
    B, T, _, _ = q.shape
    n = T // BLOCK
    qb = q.reshape(B, n, BLOCK, SWA_KV_HEADS, SWA_GROUP, HEAD_DIM)
    pad = ((0, 0), (BLOCK, 0), (0, 0), (0, 0))
    kb = jnp.pad(k, pad).reshape(B, n + 1, BLOCK, SWA_KV_HEADS, HEAD_DIM)
    vb = jnp.pad(v, pad).reshape(B, n + 1, BLOCK, SWA_KV_HEADS, HEAD_DIM)
    kwin = jnp.concatenate([kb[:, :-1], kb[:, 1:]], axis=2)
    vwin = jnp.concatenate([vb[:, :-1], vb[:, 1:]], axis=2)
    scale = HEAD_DIM ** -0.5
    s = jnp.einsum('bnqhgd,bnshd->bnhgqs', qb, kwin).astype(jnp.float32) * scale
    blk = jnp.arange(n)[:, None, None]
    qloc = jnp.arange(BLOCK)[None, :, None] + BLOCK
    kloc = jnp.arange(2 * BLOCK)[None, None, :]
    rel = qloc - kloc
    kglob = blk * BLOCK + kloc - BLOCK
    mask = (rel >= 0) & (rel < WINDOW) & (kglob >= 0)
    s = jnp.where(mask[None, :, None, None], s, -jnp.inf)
    sink = sinks.astype(jnp.float32).reshape(SWA_KV_HEADS, SWA_GROUP)[None, None, :, :, None, None]
    m = jnp.maximum(jnp.max(s, axis=-1, keepdims=True), sink)
    p = jnp.exp(s - m)
    denom = jnp.sum(p, axis=-1, keepdims=True) + jnp.exp(sink - m)
    probs = (p / denom).astype(v.dtype)
    out = jnp.einsum('bnhgqs,bnshd->bnqhgd', probs, vwin)
    return out.reshape(B, T, SWA_Q_WIDTH)


def stick_breaking_attention(q, k, v):
    B, T, H, D = q.shape
    n = T // BLOCK
    qb = q.reshape(B, n, BLOCK, H, D).transpose(1, 0, 2, 3, 4)
    kpos = jnp.arange(T)
    scale = D ** -0.5

    def block(args):
        qi, i = args
        z = jnp.einsum('bqhd,bshd->bhqs', qi, k).astype(jnp.float32) * scale
        qpos = i * BLOCK + jnp.arange(BLOCK)
        mask = kpos[None, :] < qpos[:, None]
        log_beta = jax.nn.log_sigmoid(z)
        log_one_minus = jnp.where(mask, jax.nn.log_sigmoid(-z), 0.0)
        after = lax.cumsum(log_one_minus, axis=3, reverse=True) - log_one_minus
        a = jnp.where(mask, jnp.exp(log_beta + after), 0.0).astype(v.dtype)
        return jnp.einsum('bhqs,bshd->bqhd', a, v)

    out = lax.map(block, (qb, jnp.arange(n)))
    return out.transpose(1, 0, 2, 3, 4).reshape(B, T, H * D)


def causal_depthwise_conv(u, w, b):
    K, C = w.shape
    y = lax.conv_general_dilated(
        u, w[:, None, :].astype(u.dtype), window_strides=(1,), padding=[(K - 1, 0)],
        dimension_numbers=('NWC', 'WIO', 'NWC'), feature_group_count=C)
    return y + b


def conv_ffn(x, w_up, conv_w, conv_b, w_down):
    a = causal_depthwise_conv(x @ w_up, conv_w, conv_b)
    gate, up = jnp.split(a, 2, axis=-1)
    return (jax.nn.silu(gate) * up) @ w_down


def _fwd_setup_inputs(seed: int = 0) -> dict:
    key = jax.random.key(seed)
    ks = jax.random.split(key, 20)
    f32 = jnp.float32

    def dense(k, fan_in, fan_out, scale=1.0):
        return jax.random.normal(k, (DEPTH, fan_in, fan_out), f32) * (fan_in ** -0.5) * scale

    x = jax.random.normal(ks[0], (BATCH, SEQ, D_MODEL), f32)
    offset = jax.random.randint(ks[1], (BATCH, 1), 0, 1024, dtype=jnp.int32)
    positions = (offset + jnp.arange(SEQ, dtype=jnp.int32)[None, :]).astype(jnp.int32)
    w_in = jnp.concatenate([
        dense(ks[2], D_MODEL, SWA_Q_WIDTH),
        dense(ks[3], D_MODEL, SWA_KV_WIDTH),
        dense(ks[4], D_MODEL, SWA_KV_WIDTH, DEEPNORM_BETA),
        dense(ks[5], D_MODEL, SB_WIDTH),
        dense(ks[6], D_MODEL, SB_WIDTH),
        dense(ks[7], D_MODEL, SB_WIDTH, DEEPNORM_BETA),
        dense(ks[8], D_MODEL, GATE_WIDTH),
    ], axis=-1)
    b_gate = 0.02 * jax.random.normal(ks[9], (DEPTH, GATE_WIDTH), f32)
    sinks = 0.5 * jax.random.normal(ks[10], (DEPTH, SWA_Q_HEADS), f32)
    w_branch_a = dense(ks[11], SWA_Q_WIDTH, D_MODEL, DEEPNORM_BETA)
    w_branch_b = dense(ks[12], SB_WIDTH, D_MODEL, DEEPNORM_BETA)
    w_out = dense(ks[13], D_MODEL, D_MODEL, DEEPNORM_BETA)
    ln1_g = 1.0 + 0.02 * jax.random.normal(ks[14], (DEPTH, D_MODEL), f32)
    ln1_b = 0.02 * jax.random.normal(ks[15], (DEPTH, D_MODEL), f32)
    w_up = dense(ks[16], D_MODEL, 2 * D_FF, DEEPNORM_BETA)
    kc1, kc2 = jax.random.split(ks[17])
    conv_w = jax.random.normal(kc1, (DEPTH, CONV_WIDTH, 2 * D_FF), f32) * (CONV_WIDTH ** -0.5)
    conv_b = 0.02 * jax.random.normal(kc2, (DEPTH, 2 * D_FF), f32)
    w_down = dense(ks[18], D_FF, D_MODEL, DEEPNORM_BETA)
    kl1, kl2 = jax.random.split(ks[19])
    ln2_g = 1.0 + 0.02 * jax.random.normal(kl1, (DEPTH, D_MODEL), f32)
    ln2_b = 0.02 * jax.random.normal(kl2, (DEPTH, D_MODEL), f32)
    return {'x': x, 'positions': positions, 'w_in': w_in, 'b_gate': b_gate, 'sinks': sinks,
            'w_branch_a': w_branch_a, 'w_branch_b': w_branch_b, 'w_out': w_out,
            'ln1_g': ln1_g, 'ln1_b': ln1_b, 'w_up': w_up, 'conv_w': conv_w, 'conv_b': conv_b,
            'w_down': w_down, 'ln2_g': ln2_g, 'ln2_b': ln2_b}


def _fwd_reference(x, positions, w_in, b_gate, sinks, w_branch_a, w_branch_b, w_out,
              ln1_g, ln1_b, w_up, conv_w, conv_b, w_down, ln2_g, ln2_b):
    B, T, _ = x.shape
    cos, sin = rotary_tables(positions)
    for l in range(DEPTH):
        proj = x @ w_in[l]
        qa, ka, va, qb, kb, vb, gl = jnp.split(proj, IN_SPLITS, axis=-1)
        qa = apply_rope(qa.reshape(B, T, SWA_Q_HEADS, HEAD_DIM), cos, sin)
        ka = apply_rope(ka.reshape(B, T, SWA_KV_HEADS, HEAD_DIM), cos, sin)
        va = va.reshape(B, T, SWA_KV_HEADS, HEAD_DIM)
        ya = sliding_window_sink_attention(qa, ka, va, sinks[l])
        yb = stick_breaking_attention(
            qb.reshape(B, T, SB_HEADS, HEAD_DIM),
            kb.reshape(B, T, SB_HEADS, HEAD_DIM),
            vb.reshape(B, T, SB_HEADS, HEAD_DIM))
        gates = jax.nn.sigmoid(gl + b_gate[l]).reshape(B, T, N_BRANCHES, D_MODEL)
        h = gates[:, :, 0, :] * (ya @ w_branch_a[l]) + gates[:, :, 1, :] * (yb @ w_branch_b[l])
        x = layer_norm(DEEPNORM_ALPHA * x + h @ w_out[l], ln1_g[l], ln1_b[l])
        f = conv_ffn(x, w_up[l], conv_w[l], conv_b[l], w_down[l])
        x = layer_norm(DEEPNORM_ALPHA * x + f, ln2_g[l], ln2_b[l])
    return x


import jax as _jax
import jax.numpy as _jnp

TWIN_FORMAT = 'train_step'
FWD_PARAMS = ['x', 'positions', 'w_in', 'b_gate', 'sinks', 'w_branch_a', 'w_branch_b', 'w_out', 'ln1_g', 'ln1_b', 'w_up', 'conv_w', 'conv_b', 'w_down', 'ln2_g', 'ln2_b']
TWIN_WEIGHTS = ['w_in', 'b_gate', 'sinks', 'w_branch_a', 'w_branch_b', 'w_out', 'ln1_g', 'ln1_b', 'w_up', 'conv_w', 'conv_b', 'w_down', 'ln2_g', 'ln2_b']
TWIN_DIFF_INPUT = 'x'
TWIN_INPUTS = ['x', 'positions', 'w_in', 'b_gate', 'sinks', 'w_branch_a', 'w_branch_b', 'w_out', 'ln1_g', 'ln1_b', 'w_up', 'conv_w', 'conv_b', 'w_down', 'ln2_g', 'ln2_b', 'loss_target', 'm_w_in', 'm_b_gate', 'm_sinks', 'm_w_branch_a', 'm_w_branch_b', 'm_w_out', 'm_ln1_g', 'm_ln1_b', 'm_w_up', 'm_conv_w', 'm_conv_b', 'm_w_down', 'm_ln2_g', 'm_ln2_b', 'v_w_in', 'v_b_gate', 'v_sinks', 'v_w_branch_a', 'v_w_branch_b', 'v_w_out', 'v_ln1_g', 'v_ln1_b', 'v_w_up', 'v_conv_w', 'v_conv_b', 'v_w_down', 'v_ln2_g', 'v_ln2_b']
TWIN_OUTPUTS = ['loss', 'grad_x', 'grad_w_in', 'grad_b_gate', 'grad_sinks', 'grad_w_branch_a', 'grad_w_branch_b', 'grad_w_out', 'grad_ln1_g', 'grad_ln1_b', 'grad_w_up', 'grad_conv_w', 'grad_conv_b', 'grad_w_down', 'grad_ln2_g', 'grad_ln2_b', 'delta_w_in', 'delta_b_gate', 'delta_sinks', 'delta_w_branch_a', 'delta_w_branch_b', 'delta_w_out', 'delta_ln1_g', 'delta_ln1_b', 'delta_w_up', 'delta_conv_w', 'delta_conv_b', 'delta_w_down', 'delta_ln2_g', 'delta_ln2_b', 'new_m_w_in', 'new_m_b_gate', 'new_m_sinks', 'new_m_w_branch_a', 'new_m_w_branch_b', 'new_m_w_out', 'new_m_ln1_g', 'new_m_ln1_b', 'new_m_w_up', 'new_m_conv_w', 'new_m_conv_b', 'new_m_w_down', 'new_m_ln2_g', 'new_m_ln2_b', 'new_v_w_in', 'new_v_b_gate', 'new_v_sinks', 'new_v_w_branch_a', 'new_v_w_branch_b', 'new_v_w_out', 'new_v_ln1_g', 'new_v_ln1_b', 'new_v_w_up', 'new_v_conv_w', 'new_v_conv_b', 'new_v_w_down', 'new_v_ln2_g', 'new_v_ln2_b']
TWIN_LEAF_KINDS = {'loss': 'loss', 'grad_x': 'grad_x', 'grad_w_in': 'grad_w', 'grad_b_gate': 'grad_w', 'grad_sinks': 'grad_w', 'grad_w_branch_a': 'grad_w', 'grad_w_branch_b': 'grad_w', 'grad_w_out': 'grad_w', 'grad_ln1_g': 'grad_w', 'grad_ln1_b': 'grad_w', 'grad_w_up': 'grad_w', 'grad_conv_w': 'grad_w', 'grad_conv_b': 'grad_w', 'grad_w_down': 'grad_w', 'grad_ln2_g': 'grad_w', 'grad_ln2_b': 'grad_w', 'delta_w_in': 'delta_w', 'delta_b_gate': 'delta_w', 'delta_sinks': 'delta_w', 'delta_w_branch_a': 'delta_w', 'delta_w_branch_b': 'delta_w', 'delta_w_out': 'delta_w', 'delta_ln1_g': 'delta_w', 'delta_ln1_b': 'delta_w', 'delta_w_up': 'delta_w', 'delta_conv_w': 'delta_w', 'delta_conv_b': 'delta_w', 'delta_w_down': 'delta_w', 'delta_ln2_g': 'delta_w', 'delta_ln2_b': 'delta_w', 'new_m_w_in': 'new_m', 'new_m_b_gate': 'new_m', 'new_m_sinks': 'new_m', 'new_m_w_branch_a': 'new_m', 'new_m_w_branch_b': 'new_m', 'new_m_w_out': 'new_m', 'new_m_ln1_g': 'new_m', 'new_m_ln1_b': 'new_m', 'new_m_w_up': 'new_m', 'new_m_conv_w': 'new_m', 'new_m_conv_b': 'new_m', 'new_m_w_down': 'new_m', 'new_m_ln2_g': 'new_m', 'new_m_ln2_b': 'new_m', 'new_v_w_in': 'new_v', 'new_v_b_gate': 'new_v', 'new_v_sinks': 'new_v', 'new_v_w_branch_a': 'new_v', 'new_v_w_branch_b': 'new_v', 'new_v_w_out': 'new_v', 'new_v_ln1_g': 'new_v', 'new_v_ln1_b': 'new_v', 'new_v_w_up': 'new_v', 'new_v_conv_w': 'new_v', 'new_v_conv_b': 'new_v', 'new_v_w_down': 'new_v', 'new_v_ln2_g': 'new_v', 'new_v_ln2_b': 'new_v'}


def _forward(args):
    return _fwd_reference(*[args[k] for k in FWD_PARAMS])


def _output_shape():
    out = _jax.eval_shape(lambda: _forward(_fwd_setup_inputs(0)))
    return out.shape, out.dtype

N_MICROBATCH = 1
ADAM_LR = 0.001
ADAM_B1 = 0.9
ADAM_B2 = 0.999
ADAM_EPS = 1e-08
ADAM_WD = 0.01
ADAM_STEP = 10
PER_EXAMPLE_BATCH_AXIS = {'x': 0, 'positions': 0, 'loss_target': 0}
SHARED_INPUTS = []
_WEIGHT_DTYPES = {'w_in': _jnp.float32, 'b_gate': _jnp.float32, 'sinks': _jnp.float32, 'w_branch_a': _jnp.float32, 'w_branch_b': _jnp.float32, 'w_out': _jnp.float32, 'ln1_g': _jnp.float32, 'ln1_b': _jnp.float32, 'w_up': _jnp.float32, 'conv_w': _jnp.float32, 'conv_b': _jnp.float32, 'w_down': _jnp.float32, 'ln2_g': _jnp.float32, 'ln2_b': _jnp.float32}
MOMENT_SCALE = {'w_in': 1.043298e-02, 'b_gate': 3.205312e-03, 'sinks': 3.029997e-03, 'w_branch_a': 5.900711e-03, 'w_branch_b': 1.840038e-02, 'w_out': 1.929275e-02, 'ln1_g': 1.060631e+00, 'ln1_b': 4.631488e-01, 'w_up': 1.813533e-02, 'conv_w': 1.071118e-02, 'conv_b': 1.845410e-02, 'w_down': 2.975027e-02, 'ln2_g': 3.198403e+01, 'ln2_b': 6.261610e-01}


def _to_microbatches(a, axis):
    t = _jnp.moveaxis(a, axis, 0)
    t = t.reshape((N_MICROBATCH, t.shape[0] // N_MICROBATCH) + t.shape[1:])
    return _jnp.moveaxis(t, 1, axis + 1)


def setup_inputs(seed: int = 0) -> dict:
    inp = _fwd_setup_inputs(seed)
    key = _jax.random.fold_in(_jax.random.key(seed), 7919)
    shape, _ = _output_shape()
    out = dict(inp)
    out["loss_target"] = _jax.random.normal(_jax.random.fold_in(key, 0), shape, _jnp.float32)
    for i, name in enumerate(TWIN_WEIGHTS):
        w = inp[name].astype(_jnp.float32)
        if MOMENT_SCALE is None:
            s = _jnp.sqrt(_jnp.mean(_jnp.square(w)) + 1e-30)
        else:
            s = MOMENT_SCALE[name]
        km, kv = _jax.random.split(_jax.random.fold_in(key, i + 1))
        out[name] = w
        out["m_" + name] = s * _jax.random.normal(km, w.shape, _jnp.float32)
        out["v_" + name] = (s * s) * _jax.random.uniform(kv, w.shape, _jnp.float32, 0.5, 1.5)
    if N_MICROBATCH > 1:
        for name, axis in PER_EXAMPLE_BATCH_AXIS.items():
            out[name] = _to_microbatches(out[name], axis)
    return {'x': out['x'], 'positions': out['positions'], 'w_in': out['w_in'], 'b_gate': out['b_gate'], 'sinks': out['sinks'], 'w_branch_a': out['w_branch_a'], 'w_branch_b': out['w_branch_b'], 'w_out': out['w_out'], 'ln1_g': out['ln1_g'], 'ln1_b': out['ln1_b'], 'w_up': out['w_up'], 'conv_w': out['conv_w'], 'conv_b': out['conv_b'], 'w_down': out['w_down'], 'ln2_g': out['ln2_g'], 'ln2_b': out['ln2_b'], 'loss_target': out['loss_target'], 'm_w_in': out['m_w_in'], 'm_b_gate': out['m_b_gate'], 'm_sinks': out['m_sinks'], 'm_w_branch_a': out['m_w_branch_a'], 'm_w_branch_b': out['m_w_branch_b'], 'm_w_out': out['m_w_out'], 'm_ln1_g': out['m_ln1_g'], 'm_ln1_b': out['m_ln1_b'], 'm_w_up': out['m_w_up'], 'm_conv_w': out['m_conv_w'], 'm_conv_b': out['m_conv_b'], 'm_w_down': out['m_w_down'], 'm_ln2_g': out['m_ln2_g'], 'm_ln2_b': out['m_ln2_b'], 'v_w_in': out['v_w_in'], 'v_b_gate': out['v_b_gate'], 'v_sinks': out['v_sinks'], 'v_w_branch_a': out['v_w_branch_a'], 'v_w_branch_b': out['v_w_branch_b'], 'v_w_out': out['v_w_out'], 'v_ln1_g': out['v_ln1_g'], 'v_ln1_b': out['v_ln1_b'], 'v_w_up': out['v_w_up'], 'v_conv_w': out['v_conv_w'], 'v_conv_b': out['v_conv_b'], 'v_w_down': out['v_w_down'], 'v_ln2_g': out['v_ln2_g'], 'v_ln2_b': out['v_ln2_b']}


def _loss(weights, diff, rest, loss_target):
    with _jax.named_scope("forward"):
        args = {**rest, TWIN_DIFF_INPUT: diff, **{k: w.astype(_WEIGHT_DTYPES[k]) for k, w in weights.items()}}
        y = _forward(args)
    with _jax.named_scope("loss_head"):
        err = _jnp.square(y.astype(_jnp.float32) - loss_target)
        return 0.5 * _jnp.sum(_jnp.mean(err, axis=-1)) if err.ndim else 0.5 * err


def _adamw(w, g, m, v):
    m = ADAM_B1 * m + (1.0 - ADAM_B1) * g
    v = ADAM_B2 * v + (1.0 - ADAM_B2) * _jnp.square(g)
    m_hat = m / (1.0 - ADAM_B1 ** ADAM_STEP)
    v_hat = v / (1.0 - ADAM_B2 ** ADAM_STEP)
    delta = -ADAM_LR * (m_hat / (_jnp.sqrt(v_hat) + ADAM_EPS) + ADAM_WD * w)
    return delta, m, v


def reference(x, positions, w_in, b_gate, sinks, w_branch_a, w_branch_b, w_out, ln1_g, ln1_b, w_up, conv_w, conv_b, w_down, ln2_g, ln2_b, loss_target, m_w_in, m_b_gate, m_sinks, m_w_branch_a, m_w_branch_b, m_w_out, m_ln1_g, m_ln1_b, m_w_up, m_conv_w, m_conv_b, m_w_down, m_ln2_g, m_ln2_b, v_w_in, v_b_gate, v_sinks, v_w_branch_a, v_w_branch_b, v_w_out, v_ln1_g, v_ln1_b, v_w_up, v_conv_w, v_conv_b, v_w_down, v_ln2_g, v_ln2_b):
    given = dict(x=x, positions=positions, w_in=w_in, b_gate=b_gate, sinks=sinks, w_branch_a=w_branch_a, w_branch_b=w_branch_b, w_out=w_out, ln1_g=ln1_g, ln1_b=ln1_b, w_up=w_up, conv_w=conv_w, conv_b=conv_b, w_down=w_down, ln2_g=ln2_g, ln2_b=ln2_b, loss_target=loss_target, m_w_in=m_w_in, m_b_gate=m_b_gate, m_sinks=m_sinks, m_w_branch_a=m_w_branch_a, m_w_branch_b=m_w_branch_b, m_w_out=m_w_out, m_ln1_g=m_ln1_g, m_ln1_b=m_ln1_b, m_w_up=m_w_up, m_conv_w=m_conv_w, m_conv_b=m_conv_b, m_w_down=m_w_down, m_ln2_g=m_ln2_g, m_ln2_b=m_ln2_b, v_w_in=v_w_in, v_b_gate=v_b_gate, v_sinks=v_sinks, v_w_branch_a=v_w_branch_a, v_w_branch_b=v_w_branch_b, v_w_out=v_w_out, v_ln1_g=v_ln1_g, v_ln1_b=v_ln1_b, v_w_up=v_w_up, v_conv_w=v_conv_w, v_conv_b=v_conv_b, v_w_down=v_w_down, v_ln2_g=v_ln2_g, v_ln2_b=v_ln2_b)
    weights = {n: given[n] for n in TWIN_WEIGHTS}
    shared = {n: given[n] for n in SHARED_INPUTS}
    per_example = {n: given[n] for n in ['x', 'positions']}
    grad_fn = _jax.value_and_grad(_loss, argnums=(0, 1))

    def one_microbatch(ex, loss_target):
        ex = dict(ex)
        diff = ex.pop(TWIN_DIFF_INPUT)
        return grad_fn(weights, diff, {**shared, **ex}, loss_target)

    if N_MICROBATCH == 1:
        loss, (grad_w, grad_x) = one_microbatch(per_example, given["loss_target"])
    else:
        def body(carry, xs):
            loss_sum, grad_sum = carry
            l_k, (gw_k, gx_k) = one_microbatch(xs[0], xs[1])
            with _jax.named_scope("update"):
                return (loss_sum + l_k, _jax.tree.map(_jnp.add, grad_sum, gw_k)), gx_k

        init = (_jnp.zeros((), _jnp.float32), _jax.tree.map(_jnp.zeros_like, weights))
        (loss, grad_w), grad_x = _jax.lax.scan(body, init, (per_example, given["loss_target"]))
    with _jax.named_scope("update"):
        delta_w, new_m, new_v = {}, {}, {}
        for n in TWIN_WEIGHTS:
            delta_w[n], new_m[n], new_v[n] = _adamw(weights[n], grad_w[n], given["m_" + n], given["v_" + n])
    return (loss, grad_x, *[grad_w[n] for n in TWIN_WEIGHTS], *[delta_w[n] for n in TWIN_WEIGHTS],
            *[new_m[n] for n in TWIN_WEIGHTS], *[new_v[n] for n in TWIN_WEIGHTS])
```

```python
import functools

import jax
import jax.numpy as jnp
import numpy as np
from jax import lax
from jax.experimental import pallas as pl
from jax.experimental.pallas import tpu as pltpu

D_MODEL = 1024
HEAD_DIM = 64
SWA_Q_HEADS = 8
SWA_KV_HEADS = 2
SB_HEADS = 8
WINDOW = 128
ROPE_THETA = 10000.0
D_FF = 2816
LN_EPS = 1e-5
DEPTH = 1
ALPHA = (2.0 * DEPTH) ** 0.25
SWA_Q_WIDTH = SWA_Q_HEADS * HEAD_DIM
SWA_KV_WIDTH = SWA_KV_HEADS * HEAD_DIM
SB_WIDTH = SB_HEADS * HEAD_DIM
GATE_WIDTH = 2 * D_MODEL
IN_WIDTHS = (SWA_Q_WIDTH, SWA_KV_WIDTH, SWA_KV_WIDTH, SB_WIDTH, SB_WIDTH, SB_WIDTH, GATE_WIDTH)
IN_TOTAL = sum(IN_WIDTHS)
ATTN_SCALE = HEAD_DIM ** -0.5

ADAM_LR = 0.001
ADAM_B1 = 0.9
ADAM_B2 = 0.999
ADAM_EPS = 1e-08
ADAM_WD = 0.01
ADAM_STEP = 10

N_DEV = 8
LANES = 128
SB_BLOCK = 256
ADAM_ROWS = 512
VMEM_LIMIT = 56 * 1024 * 1024

F32 = jnp.float32
BF16 = jnp.bfloat16
MESH = pl.DeviceIdType.MESH


def _params(sem=None):
    return pltpu.CompilerParams(dimension_semantics=sem, vmem_limit_bytes=VMEM_LIMIT)


def _dot(a, b):
    return jnp.dot(a, b, preferred_element_type=F32)


def _dot_nt(a, b):
    return lax.dot_general(a, b, (((1,), (1,)), ((), ())), preferred_element_type=F32)


def _dot_tn(a, b):
    return lax.dot_general(a, b, (((0,), (0,)), ((), ())), preferred_element_type=F32)


def _split_bf16(v):
    hi = v.astype(BF16)
    lo = (v - hi.astype(F32)).astype(BF16)
    return hi, lo


def _matmul(a, b, *, kind, out_shape, grid, a_spec, b_spec, out_spec, name, add=None, add_spec=None, add_scale=1.0):
    dot = {"nn": _dot, "nt": _dot_nt, "tn": _dot_tn}[kind]

    def body(*refs):
        if add is None:
            a_ref, b_ref, o_ref = refs
        else:
            a_ref, b_ref, add_ref, o_ref = refs
        r = dot(a_ref[...].astype(BF16), b_ref[...].astype(BF16))
        if add is not None:
            r = r + add_scale * add_ref[...]
        o_ref[...] = r.astype(o_ref.dtype)

    ins = [a, b] + ([] if add is None else [add])
    specs = [a_spec, b_spec] + ([] if add is None else [add_spec])
    return pl.pallas_call(
        body, name=name, grid=grid, in_specs=specs, out_specs=out_spec, out_shape=out_shape,
        compiler_params=_params(("parallel",) * len(grid)),
    )(*ins)


def _rope_tables(pos_col, inv_freq_lanes):
    T = pos_col.shape[0]
    tm = min(512, T)

    def body(pos_ref, f_ref, cos_ref, sin_ref):
        ang = pos_ref[...].astype(F32) * f_ref[...]
        cos_ref[...] = jnp.cos(ang)
        sin_ref[...] = jnp.sin(ang)

    return pl.pallas_call(
        body, name="rope_tables", grid=(T // tm,),
        in_specs=[pl.BlockSpec((tm, 1), lambda i: (i, 0)), pl.BlockSpec((1, LANES), lambda i: (0, 0))],
        out_specs=[pl.BlockSpec((tm, LANES), lambda i: (i, 0))] * 2,
        out_shape=[jax.ShapeDtypeStruct((T, LANES), F32)] * 2,
        compiler_params=_params(("parallel",)),
    )(pos_col, inv_freq_lanes)


def _lane_iota(shape):
    return lax.broadcasted_iota(jnp.int32, shape, len(shape) - 1)


def _rot_half(t):
    first = (_lane_iota(t.shape) % HEAD_DIM) < (HEAD_DIM // 2)
    return jnp.where(first, -pltpu.roll(t, LANES - HEAD_DIM // 2, axis=1), pltpu.roll(t, HEAD_DIM // 2, axis=1))


def _rope(t, cos, sin):
    return t * cos + _rot_half(t) * sin


def _rope_transpose(d, cos, sin):
    return d * cos - _rot_half(d * sin)


_IN_DTYPES = (F32, F32, BF16, BF16, BF16, BF16, F32)


def _in_proj(x, w_in_b):
    T = x.shape[0]
    tm = min(256, T)
    offs = np.cumsum((0,) + IN_WIDTHS)

    def body(x_ref, w_ref, xb_ref, *outs):
        xb = x_ref[...].astype(BF16)
        xb_ref[...] = xb
        for o_ref, a, b in zip(outs, offs[:-1], offs[1:]):
            o_ref[...] = _dot(xb, w_ref[:, a:b]).astype(o_ref.dtype)

    row = lambda n: pl.BlockSpec((tm, n), lambda i: (i, 0))
    return pl.pallas_call(
        body, name="in_proj", grid=(T // tm,),
        in_specs=[row(D_MODEL), pl.BlockSpec((D_MODEL, IN_TOTAL), lambda i: (0, 0))],
        out_specs=[row(D_MODEL)] + [row(n) for n in IN_WIDTHS],
        out_shape=[jax.ShapeDtypeStruct((T, D_MODEL), BF16)] + [jax.ShapeDtypeStruct((T, n), dt) for n, dt in zip(IN_WIDTHS, _IN_DTYPES)],
        compiler_params=_params(("parallel",)),
    )(x, w_in_b)


def _swa_specs(T):
    blk = WINDOW
    cur = lambda n: pl.BlockSpec((blk, n), lambda i: (i, 0))
    prev = lambda n: pl.BlockSpec((blk, n), lambda i: (jnp.maximum(i - 1, 0), 0))
    return blk, cur, prev


def _swa_window(i, kp, kc, vp, vc, cp, cc, sp, sc):
    kwin = jnp.concatenate([_rope(kp, cp, sp), _rope(kc, cc, sc)], axis=0)
    vwin = jnp.concatenate([vp, vc], axis=0)
    lane = _lane_iota(kwin.shape)
    low = lane < HEAD_DIM
    ks, vs = [], []
    for g in range(SWA_KV_HEADS):
        k0 = jnp.where(low, kwin if g == 0 else pltpu.roll(kwin, HEAD_DIM, axis=1), 0.0)
        v0 = jnp.where(low, vwin if g == 0 else pltpu.roll(vwin, HEAD_DIM, axis=1), 0.0)
        ks.append((k0, pltpu.roll(k0, HEAD_DIM, axis=1)))
        vs.append((v0, pltpu.roll(v0, HEAD_DIM, axis=1)))
    blk = WINDOW
    r = lax.broadcasted_iota(jnp.int32, (blk, 2 * blk), 0)
    c = lax.broadcasted_iota(jnp.int32, (blk, 2 * blk), 1)
    rel = blk + r - c
    valid = (rel >= 0) & (rel < WINDOW) & ((c >= blk) | (i > 0))
    return ks, vs, valid


def _swa_probs(qh, kk, valid, sink):
    s = _dot_nt(qh, kk) * ATTN_SCALE
    s = jnp.where(valid, s, -1e30)
    m = jnp.maximum(jnp.max(s, axis=1, keepdims=True), sink)
    p = jnp.where(valid, jnp.exp(s - m), 0.0)
    es = jnp.exp(sink - m)
    den = jnp.sum(p, axis=1, keepdims=True) + es
    return p / den, es / den


def _swa_fwd(qa, ka, va, cos, sin, sinks):
    T = qa.shape[0]
    blk, cur, prev = _swa_specs(T)

    def body(sink_ref, q_ref, kp_ref, kc_ref, vp_ref, vc_ref, cp_ref, cc_ref, sp_ref, sc_ref, o_ref):
        i = pl.program_id(0)
        cc, sc = cc_ref[...], sc_ref[...]
        ks, vs, valid = _swa_window(i, kp_ref[...], kc_ref[...], vp_ref[...].astype(F32), vc_ref[...].astype(F32),
                                    cp_ref[...], cc, sp_ref[...], sc)
        lane = _lane_iota((blk, LANES))
        for pp in range(SWA_Q_HEADS // 2):
            g = pp // (SWA_Q_HEADS // SWA_KV_HEADS // 2)
            qp = _rope(q_ref[:, pp * LANES:(pp + 1) * LANES], cc, sc)
            out = jnp.zeros((blk, LANES), F32)
            for hh in range(2):
                half = (lane >= hh * HEAD_DIM) & (lane < (hh + 1) * HEAD_DIM)
                qh = jnp.where(half, qp, 0.0).astype(BF16)
                probs, _ = _swa_probs(qh, ks[g][hh].astype(BF16), valid, sink_ref[2 * pp + hh])
                out = out + _dot(probs.astype(BF16), vs[g][hh].astype(BF16))
            o_ref[:, pp * LANES:(pp + 1) * LANES] = out.astype(o_ref.dtype)

    return pl.pallas_call(
        body, name="swa_fwd", grid=(T // blk,),
        in_specs=[pl.BlockSpec(memory_space=pltpu.SMEM), cur(SWA_Q_WIDTH), prev(LANES), cur(LANES), prev(LANES), cur(LANES),
                  prev(LANES), cur(LANES), prev(LANES), cur(LANES)],
        out_specs=cur(SWA_Q_WIDTH),
        out_shape=jax.ShapeDtypeStruct((T, SWA_Q_WIDTH), BF16),
        compiler_params=_params(("parallel",)),
    )(sinks, qa, ka, ka, va, va, cos, cos, sin, sin)


def _swa_bwd(qa, ka, va, cos, sin, sinks, dya):
    T = qa.shape[0]
    blk, cur, prev = _swa_specs(T)
    full = lambda n: pl.BlockSpec((T, n), lambda i: (0, 0))

    def body(sink_ref, q_ref, kp_ref, kc_ref, vp_ref, vc_ref, cp_ref, cc_ref, sp_ref, sc_ref, do_ref,
             dq_ref, dk_ref, dv_ref, dsink_ref):
        i = pl.program_id(0)

        @pl.when(i == 0)
        def _():
            dk_ref[...] = jnp.zeros_like(dk_ref)
            dv_ref[...] = jnp.zeros_like(dv_ref)
            dsink_ref[...] = jnp.zeros_like(dsink_ref)

        cp, cc, sp, sc = cp_ref[...], cc_ref[...], sp_ref[...], sc_ref[...]
        ks, vs, valid = _swa_window(i, kp_ref[...], kc_ref[...], vp_ref[...].astype(F32), vc_ref[...].astype(F32), cp, cc, sp, sc)
        lane = _lane_iota((blk, LANES))
        lane1 = _lane_iota((1, LANES))
        dkw = jnp.zeros((2 * blk, LANES), F32)
        dvw = jnp.zeros((2 * blk, LANES), F32)
        dsink = jnp.zeros((1, LANES), F32)
        for pp in range(SWA_Q_HEADS // 2):
            g = pp // (SWA_Q_HEADS // SWA_KV_HEADS // 2)
            qp = _rope(q_ref[:, pp * LANES:(pp + 1) * LANES], cc, sc)
            dop = do_ref[:, pp * LANES:(pp + 1) * LANES]
            dqp = jnp.zeros((blk, LANES), F32)
            for hh in range(2):
                half = (lane >= hh * HEAD_DIM) & (lane < (hh + 1) * HEAD_DIM)
                qh = jnp.where(half, qp, 0.0).astype(BF16)
                doh = jnp.where(half, dop, 0.0).astype(BF16)
                kk = ks[g][hh].astype(BF16)
                vv = vs[g][hh].astype(BF16)
                probs, psink = _swa_probs(qh, kk, valid, sink_ref[2 * pp + hh])
                dp = _dot_nt(doh, vv)
                dsum = jnp.sum(probs * dp, axis=1, keepdims=True)
                ds = (probs * (dp - dsum) * ATTN_SCALE).astype(BF16)
                dsink = dsink + jnp.where(lane1 == 2 * pp + hh, -jnp.sum(psink * dsum), 0.0)
                dqp = dqp + _dot(ds, kk)
                dk_h = _dot_tn(ds, qh)
                dv_h = _dot_tn(probs.astype(BF16), doh)
                if hh != g:
                    dk_h = pltpu.roll(dk_h, HEAD_DIM, axis=1)
                    dv_h = pltpu.roll(dv_h, HEAD_DIM, axis=1)
                dkw = dkw + dk_h
                dvw = dvw + dv_h
            dq_ref[:, pp * LANES:(pp + 1) * LANES] = _rope_transpose(dqp, cc, sc).astype(dq_ref.dtype)
        dsink_ref[...] += dsink
        ip = jnp.maximum(i - 1, 0)
        rows_p = pl.ds(pl.multiple_of(ip * blk, blk), blk)
        rows_c = pl.ds(pl.multiple_of(i * blk, blk), blk)
        dk_ref[rows_p, :] += _rope_transpose(dkw[:blk], cp, sp)
        dv_ref[rows_p, :] += dvw[:blk]
        dk_ref[rows_c, :] += _rope_transpose(dkw[blk:], cc, sc)
        dv_ref[rows_c, :] += dvw[blk:]

    return pl.pallas_call(
        body, name="swa_bwd", grid=(T // blk,),
        in_specs=[pl.BlockSpec(memory_space=pltpu.SMEM), cur(SWA_Q_WIDTH), prev(LANES), cur(LANES), prev(LANES), cur(LANES),
                  prev(LANES), cur(LANES), prev(LANES), cur(LANES), cur(SWA_Q_WIDTH)],
        out_specs=[cur(SWA_Q_WIDTH), full(LANES), full(LANES), pl.BlockSpec((1, LANES), lambda i: (0, 0))],
        out_shape=[jax.ShapeDtypeStruct((T, SWA_Q_WIDTH), BF16), jax.ShapeDtypeStruct((T, LANES), F32),
                   jax.ShapeDtypeStruct((T, LANES), F32), jax.ShapeDtypeStruct((1, LANES), F32)],
        compiler_params=_params(("arbitrary",)),
    )(sinks, qa, ka, ka, va, va, cos, cos, sin, sin, dya)


def _sb_scores(qm, k, diag):
    B = SB_BLOCK
    z = _dot_nt(qm, k) * ATTN_SCALE
    sp = jnp.maximum(z, 0.0) + jnp.log1p(jnp.exp(-jnp.abs(z)))
    log_one_minus = -sp
    log_beta = z - sp
    valid = None
    if diag:
        r = lax.broadcasted_iota(jnp.int32, (B, B), 0)
        c = lax.broadcasted_iota(jnp.int32, (B, B), 1)
        valid = c < r
        log_one_minus = jnp.where(valid, log_one_minus, 0.0)
    return log_one_minus, log_beta, valid


def _tri(B, cmp):
    r = lax.broadcasted_iota(jnp.int32, (B, B), 0)
    c = lax.broadcasted_iota(jnp.int32, (B, B), 1)
    return cmp(r, c).astype(BF16)


def _sb_fwd(qb, kb, vb):
    T = qb.shape[0]
    B = min(SB_BLOCK, T)
    assert T // B <= HEAD_DIM
    n_pairs = SB_HEADS // 2

    def body(q_ref, k_ref, v_ref, o_ref, carry_ref):
        i = pl.program_id(1)
        q = q_ref[...]
        lane = _lane_iota((1, LANES))
        upper = _tri(B, lambda r, c: r > c)
        acc = jnp.zeros((B, LANES), F32)
        cm = jnp.zeros((B, LANES), F32)
        for hh in range(2):
            half = (lane >= hh * HEAD_DIM) & (lane < (hh + 1) * HEAD_DIM)
            qm = jnp.where(half, q, jnp.zeros_like(q))

            def block(j, c, acc, cm, diag, half=half, qm=qm, hh=hh):
                rows = pl.ds(pl.multiple_of(j * B, B), B)
                k = k_ref[rows, :]
                v = jnp.where(half, v_ref[rows, :], jnp.zeros((), BF16))
                lom, lb, valid = _sb_scores(qm, k, diag)
                cm = jnp.where(lane == hh * HEAD_DIM + j, c, cm)
                hi, lo = _split_bf16(lom)
                after = c + _dot(hi, upper) + _dot(lo, upper)
                a = jnp.exp(lb + after)
                if diag:
                    a = jnp.where(valid, a, 0.0)
                acc = acc + _dot(a.astype(BF16), v)
                c = c + jnp.sum(lom, axis=1, keepdims=True)
                return c, acc, cm

            c, acc, cm = block(i, jnp.zeros((B, 1), F32), acc, cm, True)
            c, acc, cm = lax.fori_loop(0, i, lambda jj, s: block(i - 1 - jj, *s, False), (c, acc, cm))
        o_ref[...] = acc.astype(o_ref.dtype)
        carry_ref[...] = cm

    return pl.pallas_call(
        body, name="sb_fwd", grid=(n_pairs, T // B),
        in_specs=[pl.BlockSpec((B, LANES), lambda p, i: (i, p)), pl.BlockSpec((T, LANES), lambda p, i: (0, p)),
                  pl.BlockSpec((T, LANES), lambda p, i: (0, p))],
        out_specs=[pl.BlockSpec((B, LANES), lambda p, i: (i, p))] * 2,
        out_shape=[jax.ShapeDtypeStruct((T, SB_WIDTH), BF16), jax.ShapeDtypeStruct((T, SB_WIDTH), F32)],
        compiler_params=_params(("parallel", "parallel")),
    )(qb, kb, vb)


def _sb_bwd(qb, kb, vb, carries, dyb):
    T = qb.shape[0]
    B = min(SB_BLOCK, T)
    n_pairs = SB_HEADS // 2

    def body(q_ref, k_ref, v_ref, carry_ref, do_ref, dq_ref, dk_ref, dv_ref):
        i = pl.program_id(1)

        @pl.when(i == 0)
        def _():
            dk_ref[...] = jnp.zeros_like(dk_ref)
            dv_ref[...] = jnp.zeros_like(dv_ref)

        q = q_ref[...]
        do = do_ref[...]
        cm = carry_ref[...]
        lane = _lane_iota((1, LANES))
        upper = _tri(B, lambda r, c: r > c)
        lower = _tri(B, lambda r, c: r < c)
        dq = jnp.zeros((B, LANES), F32)
        for hh in range(2):
            half = (lane >= hh * HEAD_DIM) & (lane < (hh + 1) * HEAD_DIM)
            qm = jnp.where(half, q, jnp.zeros_like(q))
            dom = jnp.where(half, do, jnp.zeros_like(do))

            def block(j, cg, dq, diag, half=half, qm=qm, dom=dom, hh=hh):
                rows = pl.ds(pl.multiple_of(j * B, B), B)
                k = k_ref[rows, :]
                v = v_ref[rows, :]
                km = jnp.where(half, k, jnp.zeros((), BF16))
                lom, lb, valid = _sb_scores(qm, k, diag)
                c = jnp.sum(jnp.where(lane == hh * HEAD_DIM + j, cm, 0.0), axis=1, keepdims=True)
                hi, lo = _split_bf16(lom)
                after = c + _dot(hi, upper) + _dot(lo, upper)
                a = jnp.exp(lb + after)
                if diag:
                    a = jnp.where(valid, a, 0.0)
                g = a * _dot_nt(dom, v)
                ghi, glo = _split_bf16(g)
                gpre = cg + _dot(ghi, lower) + _dot(glo, lower)
                beta = jnp.exp(lb)
                dz = g * (1.0 - beta) - gpre * beta
                if diag:
                    dz = jnp.where(valid, dz, 0.0)
                dzb = (dz * ATTN_SCALE).astype(BF16)
                dq = dq + _dot(dzb, km)
                dk_ref[rows, :] += _dot_tn(dzb, qm)
                dv_ref[rows, :] += _dot_tn(a.astype(BF16), dom)
                cg = cg + jnp.sum(g, axis=1, keepdims=True)
                return cg, dq

            cg, dq = lax.fori_loop(0, i, lambda j, s: block(j, *s, False), (jnp.zeros((B, 1), F32), dq))
            cg, dq = block(i, cg, dq, True)
        dq_ref[...] = dq.astype(dq_ref.dtype)

    blk = pl.BlockSpec((B, LANES), lambda p, i: (i, p))
    full = pl.BlockSpec((T, LANES), lambda p, i: (0, p))
    return pl.pallas_call(
        body, name="sb_bwd", grid=(n_pairs, T // B),
        in_specs=[blk, full, full, blk, blk],
        out_specs=[blk, full, full],
        out_shape=[jax.ShapeDtypeStruct((T, SB_WIDTH), BF16), jax.ShapeDtypeStruct((T, SB_WIDTH), F32),
                   jax.ShapeDtypeStruct((T, SB_WIDTH), F32)],
        compiler_params=_params(("parallel", "arbitrary")),
    )(qb, kb, vb, carries, dyb)


def _ln_stats(u):
    mu = jnp.mean(u, axis=-1, keepdims=True)
    xc = u - mu
    var = jnp.mean(xc * xc, axis=-1, keepdims=True)
    rstd = lax.rsqrt(var + LN_EPS)
    return xc * rstd, rstd


def _ln_bwd(dy, xhat, rstd, g):
    dxh = dy * g
    return rstd * (dxh - jnp.mean(dxh, axis=-1, keepdims=True) - xhat * jnp.mean(dxh * xhat, axis=-1, keepdims=True))


def _gates(gl_ref, bg_ref):
    ga = jax.nn.sigmoid(gl_ref[:, :D_MODEL] + bg_ref[:, :D_MODEL])
    gb = jax.nn.sigmoid(gl_ref[:, D_MODEL:] + bg_ref[:, D_MODEL:])
    return ga, gb


def _mix_fwd(ya, yb, gl, x, wa, wb, wo, b_gate, ln1_g, ln1_b):
    T = x.shape[0]
    tm = min(256, T)

    def body(ya_ref, yb_ref, gl_ref, x_ref, wa_ref, wb_ref, wo_ref, bg_ref, g_ref, b_ref, h_ref, u_ref, x1_ref):
        ga, gb = _gates(gl_ref, bg_ref)
        h = (ga * _dot(ya_ref[...], wa_ref[...]) + gb * _dot(yb_ref[...], wb_ref[...])).astype(BF16)
        h_ref[...] = h
        u = ALPHA * x_ref[...] + _dot(h, wo_ref[...])
        u_ref[...] = u
        xhat, _ = _ln_stats(u)
        x1_ref[...] = (xhat * g_ref[...] + b_ref[...]).astype(BF16)

    row = lambda n: pl.BlockSpec((tm, n), lambda i: (i, 0))
    const = lambda r, n: pl.BlockSpec((r, n), lambda i: (0, 0))
    return pl.pallas_call(
        body, name="mix_fwd", grid=(T // tm,),
        in_specs=[row(SWA_Q_WIDTH), row(SB_WIDTH), row(GATE_WIDTH), row(D_MODEL), const(SWA_Q_WIDTH, D_MODEL), const(SB_WIDTH, D_MODEL),
                  const(D_MODEL, D_MODEL), const(1, GATE_WIDTH), const(1, D_MODEL), const(1, D_MODEL)],
        out_specs=[row(D_MODEL)] * 3,
        out_shape=[jax.ShapeDtypeStruct((T, D_MODEL), BF16), jax.ShapeDtypeStruct((T, D_MODEL), F32), jax.ShapeDtypeStruct((T, D_MODEL), BF16)],
        compiler_params=_params(("parallel",)),
    )(ya, yb, gl, x, wa, wb, wo, b_gate, ln1_g, ln1_b)


def _mix_bwd(du1, ya, yb, gl, wa, wb, wo, b_gate):
    T = du1.shape[0]
    tm = min(256, T)

    def body(du_ref, ya_ref, yb_ref, gl_ref, wa_ref, wb_ref, wo_ref, bg_ref, dya_ref, dyb_ref, dgl_ref, dta_ref, dtb_ref, dbg_ref):
        @pl.when(pl.program_id(0) == 0)
        def _():
            dbg_ref[...] = jnp.zeros_like(dbg_ref)

        dh = _dot_nt(du_ref[...].astype(BF16), wo_ref[...])
        ga, gb = _gates(gl_ref, bg_ref)
        for gate, y_ref, w_ref, dy_ref, dt_ref, lo in ((ga, ya_ref, wa_ref, dya_ref, dta_ref, 0), (gb, yb_ref, wb_ref, dyb_ref, dtb_ref, D_MODEL)):
            t = _dot(y_ref[...], w_ref[...])
            dlogit = dh * t * gate * (1.0 - gate)
            dgl_ref[:, lo:lo + D_MODEL] = dlogit.astype(BF16)
            dbg_ref[:, lo:lo + D_MODEL] += jnp.sum(dlogit, axis=0, keepdims=True)
            dt = (dh * gate).astype(BF16)
            dt_ref[...] = dt
            dy_ref[...] = _dot_nt(dt, w_ref[...]).astype(BF16)

    row = lambda n: pl.BlockSpec((tm, n), lambda i: (i, 0))
    const = lambda r, n: pl.BlockSpec((r, n), lambda i: (0, 0))
    sds = lambda n, dt: jax.ShapeDtypeStruct((T, n), dt)
    return pl.pallas_call(
        body, name="mix_bwd", grid=(T // tm,),
        in_specs=[row(D_MODEL), row(SWA_Q_WIDTH), row(SB_WIDTH), row(GATE_WIDTH), const(SWA_Q_WIDTH, D_MODEL), const(SB_WIDTH, D_MODEL),
                  const(D_MODEL, D_MODEL), const(1, GATE_WIDTH)],
        out_specs=[row(SWA_Q_WIDTH), row(SB_WIDTH), row(GATE_WIDTH), row(D_MODEL), row(D_MODEL), const(1, GATE_WIDTH)],
        out_shape=[sds(SWA_Q_WIDTH, BF16), sds(SB_WIDTH, BF16), sds(GATE_WIDTH, BF16), sds(D_MODEL, BF16), sds(D_MODEL, BF16),
                   jax.ShapeDtypeStruct((1, GATE_WIDTH), F32)],
        compiler_params=_params(("arbitrary",)),
    )(du1, ya, yb, gl, wa, wb, wo, b_gate)


CONV_COLS = LANES


def _shift_down(v, k):
    row = lax.broadcasted_iota(jnp.int32, v.shape, 0)
    return jnp.where(row >= k, pltpu.roll(v, k, axis=0), 0.0)


def _shift_up(v, k):
    n = v.shape[0]
    row = lax.broadcasted_iota(jnp.int32, v.shape, 0)
    return jnp.where(row < n - k, pltpu.roll(v, n - k, axis=0), 0.0)


def _conv(pv, w_ref, b_ref):
    return w_ref[0:1, :] * _shift_down(pv, 2) + w_ref[1:2, :] * _shift_down(pv, 1) + w_ref[2:3, :] * pv + b_ref[...]


def _conv_specs(T):
    nb = D_FF // CONV_COLS
    pair = pl.BlockSpec((2, T, CONV_COLS), lambda j: (0, 0, j))
    gate = lambda r: pl.BlockSpec((r, CONV_COLS), lambda j: (0, j))
    up = lambda r: pl.BlockSpec((r, CONV_COLS), lambda j: (0, j + nb))
    return nb, pair, gate, up


def _conv_glu_fwd(p3, conv_w, conv_b):
    T = p3.shape[1]
    nb, pair, gate, up = _conv_specs(T)

    def body(p_ref, wg_ref, wu_ref, bg_ref, bu_ref, s_ref):
        ag = _conv(p_ref[0], wg_ref, bg_ref)
        au = _conv(p_ref[1], wu_ref, bu_ref)
        s_ref[...] = (ag * jax.nn.sigmoid(ag) * au).astype(BF16)

    return pl.pallas_call(
        body, name="conv_glu_fwd", grid=(nb,),
        in_specs=[pair, gate(3), up(3), gate(1), up(1)],
        out_specs=pl.BlockSpec((T, CONV_COLS), lambda j: (0, j)),
        out_shape=jax.ShapeDtypeStruct((T, D_FF), BF16),
        compiler_params=_params(("parallel",)),
    )(p3, conv_w, conv_w, conv_b, conv_b)


def _conv_glu_bwd(p3, ds, conv_w, conv_b):
    T = p3.shape[1]
    nb, pair, gate, up = _conv_specs(T)

    def body(p_ref, ds_ref, wg_ref, wu_ref, bg_ref, bu_ref, dp_ref, dwg_ref, dwu_ref, dbg_ref, dbu_ref):
        pg, pu = p_ref[0], p_ref[1]
        ag = _conv(pg, wg_ref, bg_ref)
        au = _conv(pu, wu_ref, bu_ref)
        sg = jax.nn.sigmoid(ag)
        d = ds_ref[...]
        dau = d * ag * sg
        dag = d * au * (sg * (1.0 + ag * (1.0 - sg)))
        for half, (da, pv, w_ref, dw_ref, db_ref) in enumerate(((dag, pg, wg_ref, dwg_ref, dbg_ref), (dau, pu, wu_ref, dwu_ref, dbu_ref))):
            db_ref[...] = jnp.sum(da, axis=0, keepdims=True)
            dw_ref[0:1, :] = jnp.sum(da * _shift_down(pv, 2), axis=0, keepdims=True)
            dw_ref[1:2, :] = jnp.sum(da * _shift_down(pv, 1), axis=0, keepdims=True)
            dw_ref[2:3, :] = jnp.sum(da * pv, axis=0, keepdims=True)
            dp = w_ref[2:3, :] * da + w_ref[1:2, :] * _shift_up(da, 1) + w_ref[0:1, :] * _shift_up(da, 2)
            dp_ref[half] = dp.astype(BF16)

    col = lambda r: pl.BlockSpec((r, CONV_COLS), lambda j: (0, j))
    return pl.pallas_call(
        body, name="conv_glu_bwd", grid=(nb,),
        in_specs=[pair, col(T), gate(3), up(3), gate(1), up(1)],
        out_specs=[pair, col(3), col(3), col(1), col(1)],
        out_shape=[jax.ShapeDtypeStruct((2, T, D_FF), BF16), jax.ShapeDtypeStruct((3, D_FF), F32), jax.ShapeDtypeStruct((3, D_FF), F32),
                   jax.ShapeDtypeStruct((1, D_FF), F32), jax.ShapeDtypeStruct((1, D_FF), F32)],
        compiler_params=_params(("parallel",)),
    )(p3, ds, conv_w, conv_w, conv_b, conv_b)


def _ffn_down_loss(s, w_down, u1, ln1_g, ln1_b, ln2_g, ln2_b, target):
    T = u1.shape[0]
    tm = min(256, T)

    def body(s_ref, w_ref, u1_ref, g1_ref, b1_ref, g2_ref, b2_ref, t_ref, du_ref, dub_ref, dg_ref, db_ref, loss_ref):
        @pl.when(pl.program_id(0) == 0)
        def _():
            dg_ref[...] = jnp.zeros_like(dg_ref)
            db_ref[...] = jnp.zeros_like(db_ref)
            loss_ref[...] = jnp.zeros_like(loss_ref)

        xh1, _ = _ln_stats(u1_ref[...])
        x1 = xh1 * g1_ref[...] + b1_ref[...]
        u2 = ALPHA * x1 + _dot(s_ref[...], w_ref[...])
        xh2, rstd2 = _ln_stats(u2)
        err = xh2 * g2_ref[...] + b2_ref[...] - t_ref[...]
        per_token = jnp.mean(err * err, axis=-1, keepdims=True)
        loss_ref[...] += 0.5 * jnp.sum(per_token, axis=0, keepdims=True)
        dy = err * (1.0 / D_MODEL)
        dg_ref[...] += jnp.sum(dy * xh2, axis=0, keepdims=True)
        db_ref[...] += jnp.sum(dy, axis=0, keepdims=True)
        du2 = _ln_bwd(dy, xh2, rstd2, g2_ref[...])
        du_ref[...] = du2
        dub_ref[...] = du2.astype(BF16)

    row = lambda n: pl.BlockSpec((tm, n), lambda i: (i, 0))
    const = lambda r, n: pl.BlockSpec((r, n), lambda i: (0, 0))
    vec = const(1, D_MODEL)
    return pl.pallas_call(
        body, name="ffn_down_loss", grid=(T // tm,),
        in_specs=[row(D_FF), const(D_FF, D_MODEL), row(D_MODEL), vec, vec, vec, vec, row(D_MODEL)],
        out_specs=[row(D_MODEL), row(D_MODEL), vec, vec, const(1, LANES)],
        out_shape=[jax.ShapeDtypeStruct((T, D_MODEL), F32), jax.ShapeDtypeStruct((T, D_MODEL), BF16), jax.ShapeDtypeStruct((1, D_MODEL), F32),
                   jax.ShapeDtypeStruct((1, D_MODEL), F32), jax.ShapeDtypeStruct((1, LANES), F32)],
        compiler_params=_params(("arbitrary",)),
    )(s, w_down, u1, ln1_g, ln1_b, ln2_g, ln2_b, target)


def _ffn_up_bwd_ln1(dp3, w_up, du2, u1, ln1_g):
    T = u1.shape[0]
    tm = min(256, T)

    def body(dp_ref, w_ref, du2_ref, u1_ref, g_ref, du_ref, dub_ref, dg_ref, db_ref):
        @pl.when(pl.program_id(0) == 0)
        def _():
            dg_ref[...] = jnp.zeros_like(dg_ref)
            db_ref[...] = jnp.zeros_like(db_ref)

        dx1 = _dot_nt(dp_ref[0], w_ref[:, :D_FF]) + _dot_nt(dp_ref[1], w_ref[:, D_FF:]) + ALPHA * du2_ref[...]
        xh, rstd = _ln_stats(u1_ref[...])
        dg_ref[...] += jnp.sum(dx1 * xh, axis=0, keepdims=True)
        db_ref[...] += jnp.sum(dx1, axis=0, keepdims=True)
        du1 = _ln_bwd(dx1, xh, rstd, g_ref[...])
        du_ref[...] = du1
        dub_ref[...] = du1.astype(BF16)

    row = lambda n: pl.BlockSpec((tm, n), lambda i: (i, 0))
    const = lambda r, n: pl.BlockSpec((r, n), lambda i: (0, 0))
    vec = const(1, D_MODEL)
    return pl.pallas_call(
        body, name="ffn_up_bwd_ln1", grid=(T // tm,),
        in_specs=[pl.BlockSpec((2, tm, D_FF), lambda i: (0, i, 0)), const(D_MODEL, 2 * D_FF), row(D_MODEL), row(D_MODEL), vec],
        out_specs=[row(D_MODEL), row(D_MODEL), vec, vec],
        out_shape=[jax.ShapeDtypeStruct((T, D_MODEL), F32), jax.ShapeDtypeStruct((T, D_MODEL), BF16), jax.ShapeDtypeStruct((1, D_MODEL), F32),
                   jax.ShapeDtypeStruct((1, D_MODEL), F32)],
        compiler_params=_params(("arbitrary",)),
    )(dp3, w_up, du2, u1, ln1_g)


def _local_step(x, positions, w_in, b_gate, sinks, wa, wb, wo, ln1_g, ln1_b, w_up, conv_w, conv_b, w_down, ln2_g, ln2_b, target):
    T = x.shape[0]
    inv_freq = 1.0 / (ROPE_THETA ** (jnp.arange(0, HEAD_DIM, 2, dtype=F32) / HEAD_DIM))
    cos, sin = _rope_tables(positions.reshape(T, 1), jnp.tile(inv_freq, LANES // (HEAD_DIM // 2)).reshape(1, LANES))

    xb, qa, ka, va, qb, kb, vb, gl = _in_proj(x, w_in)
    ya = _swa_fwd(qa, ka, va, cos, sin, sinks)
    yb, carries = _sb_fwd(qb, kb, vb)
    h, u1, x1 = _mix_fwd(ya, yb, gl, x, wa, wb, wo, b_gate, ln1_g, ln1_b)

    ff_tn = D_FF // 2
    nff = D_FF // ff_tn
    tm = min(512, T)
    p3 = _matmul(x1, w_up, kind="nn", name="ffn_up", grid=(T // tm, 2 * nff),
                 a_spec=pl.BlockSpec((tm, D_MODEL), lambda i, j: (i, 0)), b_spec=pl.BlockSpec((D_MODEL, ff_tn), lambda i, j: (0, j)),
                 out_spec=pl.BlockSpec((None, tm, ff_tn), lambda i, j: (j // nff, i, j % nff)),
                 out_shape=jax.ShapeDtypeStruct((2, T, D_FF), F32))
    s = _conv_glu_fwd(p3, conv_w, conv_b)
    du2, du2b, dln2_g, dln2_b, loss = _ffn_down_loss(s, w_down, u1, ln1_g, ln1_b, ln2_g, ln2_b, target)

    ds = _matmul(du2b, w_down, kind="nt", name="ffn_down_bwd", grid=(T // tm, nff),
                 a_spec=pl.BlockSpec((tm, D_MODEL), lambda i, j: (i, 0)), b_spec=pl.BlockSpec((ff_tn, D_MODEL), lambda i, j: (j, 0)),
                 out_spec=pl.BlockSpec((tm, ff_tn), lambda i, j: (i, j)), out_shape=jax.ShapeDtypeStruct((T, D_FF), F32))
    dp3, dcw_g, dcw_u, dcb_g, dcb_u = _conv_glu_bwd(p3, ds, conv_w, conv_b)
    tk = 256
    dw_down = _matmul(s, du2b, kind="tn", name="dw_down", grid=(D_FF // tk,),
                      a_spec=pl.BlockSpec((T, tk), lambda i: (0, i)), b_spec=pl.BlockSpec((T, D_MODEL), lambda i: (0, 0)),
                      out_spec=pl.BlockSpec((tk, D_MODEL), lambda i: (i, 0)), out_shape=jax.ShapeDtypeStruct((D_FF, D_MODEL), F32))
    dw_up = _matmul(x1, dp3, kind="tn", name="dw_up", grid=(D_MODEL // 512, 2 * nff),
                    a_spec=pl.BlockSpec((T, 512), lambda i, j: (0, i)), b_spec=pl.BlockSpec((None, T, ff_tn), lambda i, j: (j // nff, 0, j % nff)),
                    out_spec=pl.BlockSpec((512, ff_tn), lambda i, j: (i, j)), out_shape=jax.ShapeDtypeStruct((D_MODEL, 2 * D_FF), F32))
    du1, du1b, dln1_g, dln1_b = _ffn_up_bwd_ln1(dp3, w_up, du2, u1, ln1_g)
    dya, dyb, dgl, dta, dtb, db_gate = _mix_bwd(du1, ya, yb, gl, wa, wb, wo, b_gate)

    def dw_small(a, g, name, rows):
        return _matmul(a, g, kind="tn", name=name, grid=(rows // 512, D_MODEL // 512),
                       a_spec=pl.BlockSpec((T, 512), lambda i, j: (0, i)), b_spec=pl.BlockSpec((T, 512), lambda i, j: (0, j)),
                       out_spec=pl.BlockSpec((512, 512), lambda i, j: (i, j)), out_shape=jax.ShapeDtypeStruct((rows, D_MODEL), F32))

    dwa = dw_small(ya, dta, "dw_branch_a", SWA_Q_WIDTH)
    dwb = dw_small(yb, dtb, "dw_branch_b", SB_WIDTH)
    dwo = dw_small(h, du1b, "dw_out", D_MODEL)

    dqb, dkb, dvb = _sb_bwd(qb, kb, vb, carries, dyb)
    dqa, dka, dva, dsinks = _swa_bwd(qa, ka, va, cos, sin, sinks, dya)
    dproj = jnp.concatenate([dqa, dka.astype(BF16), dva.astype(BF16), dqb, dkb.astype(BF16), dvb.astype(BF16), dgl], axis=1)
    tn = 256
    dw_in = _matmul(xb, dproj, kind="tn", name="dw_in", grid=(IN_TOTAL // tn,),
                    a_spec=pl.BlockSpec((T, D_MODEL), lambda j: (0, 0)), b_spec=pl.BlockSpec((T, tn), lambda j: (0, j)),
                    out_spec=pl.BlockSpec((D_MODEL, tn), lambda j: (0, j)), out_shape=jax.ShapeDtypeStruct((D_MODEL, IN_TOTAL), F32))
    tr = min(256, T)
    grad_x = _matmul(dproj, w_in, kind="nt", name="grad_x", grid=(T // tr,),
                     a_spec=pl.BlockSpec((tr, IN_TOTAL), lambda i: (i, 0)), b_spec=pl.BlockSpec((D_MODEL, IN_TOTAL), lambda i: (0, 0)),
                     out_spec=pl.BlockSpec((tr, D_MODEL), lambda i: (i, 0)), out_shape=jax.ShapeDtypeStruct((T, D_MODEL), F32),
                     add=du1, add_spec=pl.BlockSpec((tr, D_MODEL), lambda i: (i, 0)), add_scale=ALPHA)
    grads = dict(
        w_in=dw_in, b_gate=db_gate, sinks=dsinks[:, :SWA_Q_HEADS], w_branch_a=dwa, w_branch_b=dwb, w_out=dwo, ln1_g=dln1_g, ln1_b=dln1_b,
        w_up=dw_up, conv_w=jnp.concatenate([dcw_g, dcw_u], axis=1), conv_b=jnp.concatenate([dcb_g, dcb_u], axis=1), w_down=dw_down,
        ln2_g=dln2_g, ln2_b=dln2_b)
    return loss, grad_x, grads


ANY = pl.BlockSpec(memory_space=pl.ANY)


def _all_gather(slabs, name):
    n = len(slabs)

    def body(*refs):
        ins, outs = refs[:n], refs[n:2 * n]
        send_sems, recv_sems, local_sems = refs[2 * n:]
        x, y, c = lax.axis_index("x"), lax.axis_index("y"), lax.axis_index("c")
        me, sibling = (x, y, c), (x, y, 1 - c)
        chips = [(1 - x, y), (x, 1 - y), (1 - x, 1 - y)]

        def slot(pos):
            return 4 * pos[0] + 2 * pos[1] + pos[2]

        def copy(a, k, block, to, from_input=False):
            return pltpu.make_async_remote_copy(
                src_ref=ins[a] if from_input else outs[a].at[slot(block)], dst_ref=outs[a].at[slot(block)],
                send_sem=send_sems.at[a, k], recv_sem=recv_sems.at[a, k], device_id=to, device_id_type=MESH)

        mine = [pltpu.make_async_copy(ins[a], outs[a].at[slot(me)], local_sems.at[a]) for a in range(n)]
        for cp in mine:
            cp.start()
        first = []
        for a in range(n):
            first.append(copy(a, 0, me, sibling, from_input=True))
            first += [copy(a, 1 + j, me, (*chip, c), from_input=True) for j, chip in enumerate(chips)]
        for cp in first:
            cp.start()
        passed = []
        for j, chip in enumerate(chips):
            for a in range(n):
                copy(a, 1 + j, (*chip, c), me).wait_recv()
                fwd = copy(a, 4 + j, (*chip, c), sibling)
                fwd.start()
                passed.append(fwd)
        for a in range(n):
            copy(a, 0, sibling, me).wait_recv()
            for j, chip in enumerate(chips):
                copy(a, 4 + j, (*chip, 1 - c), me).wait_recv()
        for cp in first + passed:
            cp.wait_send()
        for cp in mine:
            cp.wait()

    return pl.pallas_call(
        body, name=name,
        in_specs=[ANY] * n, out_specs=[ANY] * n,
        out_shape=[jax.ShapeDtypeStruct((N_DEV,) + s.shape, s.dtype) for s in slabs],
        scratch_shapes=[pltpu.SemaphoreType.DMA((n, 7)), pltpu.SemaphoreType.DMA((n, 7)), pltpu.SemaphoreType.DMA((n,))],
    )(*slabs)


def _all_to_all(slabs, name):
    n = len(slabs)

    def body(*refs):
        ins, outs = refs[:n], refs[n:2 * n]
        send_sems, recv_sems, local_sems = refs[2 * n:]
        x, y, c = lax.axis_index("x"), lax.axis_index("y"), lax.axis_index("c")
        my_slot = 4 * x + 2 * y + c
        flips = [(fx, fy, fc) for fx in (0, 1) for fy in (0, 1) for fc in (0, 1) if (fx, fy, fc) != (0, 0, 0)]

        def copy(a, k):
            fx, fy, fc = flips[k]
            peer = (x ^ fx, y ^ fy, c ^ fc)
            peer_slot = 4 * peer[0] + 2 * peer[1] + peer[2]
            send = pltpu.make_async_remote_copy(src_ref=ins[a].at[peer_slot], dst_ref=outs[a].at[my_slot], send_sem=send_sems.at[a, k],
                                                recv_sem=recv_sems.at[a, k], device_id=peer, device_id_type=MESH)
            recv = pltpu.make_async_remote_copy(src_ref=ins[a].at[peer_slot], dst_ref=outs[a].at[peer_slot], send_sem=send_sems.at[a, k],
                                                recv_sem=recv_sems.at[a, k], device_id=peer, device_id_type=MESH)
            return send, recv

        mine = [pltpu.make_async_copy(ins[a].at[my_slot], outs[a].at[my_slot], local_sems.at[a]) for a in range(n)]
        for cp in mine:
            cp.start()
        copies = [copy(a, k) for a in range(n) for k in range(len(flips))]
        for send, _ in copies:
            send.start()
        for send, recv in copies:
            recv.wait_recv()
            send.wait_send()
        for cp in mine:
            cp.wait()

    return pl.pallas_call(
        body, name=name,
        in_specs=[ANY] * n, out_specs=[ANY] * n,
        out_shape=[jax.ShapeDtypeStruct(s.shape, s.dtype) for s in slabs],
        scratch_shapes=[pltpu.SemaphoreType.DMA((n, 7)), pltpu.SemaphoreType.DMA((n, 7)), pltpu.SemaphoreType.DMA((n,))],
    )(*slabs)


def _reduce_adamw(parts, w, m, v, name):
    rows = w.shape[0]
    tr = ADAM_ROWS if rows % ADAM_ROWS == 0 else rows

    def body(p_ref, w_ref, m_ref, v_ref, g_ref, d_ref, mo_ref, vo_ref):
        g = p_ref[0]
        for d in range(1, N_DEV):
            g = g + p_ref[d]
        g_ref[...] = g
        m_new = ADAM_B1 * m_ref[...] + (1.0 - ADAM_B1) * g
        v_new = ADAM_B2 * v_ref[...] + (1.0 - ADAM_B2) * jnp.square(g)
        m_hat = m_new / (1.0 - ADAM_B1 ** ADAM_STEP)
        v_hat = v_new / (1.0 - ADAM_B2 ** ADAM_STEP)
        d_ref[...] = -ADAM_LR * (m_hat / (jnp.sqrt(v_hat) + ADAM_EPS) + ADAM_WD * w_ref[...])
        mo_ref[...] = m_new
        vo_ref[...] = v_new

    row = pl.BlockSpec((tr, LANES), lambda i: (i, 0))
    return pl.pallas_call(
        body, name=name, grid=(rows // tr,),
        in_specs=[pl.BlockSpec((N_DEV, tr, LANES), lambda i: (0, i, 0)), row, row, row],
        out_specs=[row] * 4, out_shape=[jax.ShapeDtypeStruct((rows, LANES), F32)] * 4,
        compiler_params=_params(("parallel",)),
    )(parts, w, m, v)


BIG = ("w_in", "w_branch_a", "w_branch_b", "w_out", "w_up", "w_down")
BIG_SHARD_AXIS = {"w_in": 1, "w_branch_a": 1, "w_branch_b": 1, "w_out": 0, "w_up": 1, "w_down": 0}
FULL_SHAPE = {"w_in": (D_MODEL, IN_TOTAL), "w_branch_a": (SWA_Q_WIDTH, D_MODEL), "w_branch_b": (SB_WIDTH, D_MODEL), "w_out": (D_MODEL, D_MODEL),
              "w_up": (D_MODEL, 2 * D_FF), "w_down": (D_FF, D_MODEL), "conv_w": (3, 2 * D_FF)}
SMALL = ("b_gate", "sinks", "ln1_g", "ln1_b", "conv_b", "ln2_g", "ln2_b")
SMALL_SIZE = {"b_gate": GATE_WIDTH, "sinks": SWA_Q_HEADS, "ln1_g": D_MODEL, "ln1_b": D_MODEL, "conv_b": 2 * D_FF, "ln2_g": D_MODEL, "ln2_b": D_MODEL}
ORDER = ("w_in", "b_gate", "sinks", "w_branch_a", "w_branch_b", "w_out", "ln1_g", "ln1_b", "w_up", "conv_w", "conv_b", "w_down", "ln2_g", "ln2_b")
SHARDED = BIG + ("conv_w",)


def _shard_shape(name):
    shape = list(FULL_SHAPE[name])
    shape[BIG_SHARD_AXIS.get(name, 1)] //= N_DEV
    return tuple(shape)


def _pad_lanes(flat, mult):
    pad = (-flat.shape[-1]) % mult
    return flat if pad == 0 else jnp.pad(flat, [(0, 0)] * (flat.ndim - 1) + [(0, pad)])


def _flatten(arrs, row_mult):
    lead = arrs[0].shape[:-2] if arrs[0].ndim > 2 else ()
    flat = jnp.concatenate([_pad_lanes(a.reshape(lead + (-1,)), LANES) for a in arrs], axis=-1)
    flat = _pad_lanes(flat, LANES * row_mult)
    return flat.reshape(lead + (-1, LANES))


def _unflatten(flat, shapes):
    v = flat.reshape(-1)
    out, off = [], 0
    for shp in shapes:
        n = int(np.prod(shp))
        out.append(v[off:off + n].reshape(shp))
        off += n + (-n) % LANES
    return out


def _split_for_devices(full, name):
    axis = BIG_SHARD_AXIS.get(name, 1)
    shp = _shard_shape(name)
    if axis == 0:
        return full.reshape((N_DEV,) + shp)
    return full.reshape(shp[0], N_DEV, shp[1]).transpose(1, 0, 2)


def _join_from_devices(slabs, name):
    axis = BIG_SHARD_AXIS.get(name, 1)
    full = FULL_SHAPE[name]
    if axis == 0:
        return slabs.reshape(full)
    return slabs.transpose(1, 0, 2).reshape(full)


def kernel(x, positions, w_in, b_gate, sinks, w_branch_a, w_branch_b, w_out, ln1_g, ln1_b, w_up, conv_w, conv_b, w_down, ln2_g, ln2_b, loss_target, m_w_in, m_b_gate, m_sinks, m_w_branch_a, m_w_branch_b, m_w_out, m_ln1_g, m_ln1_b, m_w_up, m_conv_w, m_conv_b, m_w_down, m_ln2_g, m_ln2_b, v_w_in, v_b_gate, v_sinks, v_w_branch_a, v_w_branch_b, v_w_out, v_ln1_g, v_ln1_b, v_w_up, v_conv_w, v_conv_b, v_w_down, v_ln2_g, v_ln2_b):
    args = dict(locals())
    w = {n: args[n][0] for n in ORDER}
    m = {n: args["m_" + n][0] for n in ORDER}
    v = {n: args["v_" + n][0] for n in ORDER}

    big_flat = _flatten([w[n].astype(BF16) for n in BIG], 16)
    cw_flat = _flatten([w["conv_w"]], 8)
    big_all, cw_all = _all_gather([big_flat, cw_flat], "all_gather_weights")
    big_parts = _unflatten_slabs(big_all, [_shard_shape(n) for n in BIG])
    full = {n: _join_from_devices(p, n) for n, p in zip(BIG, big_parts)}
    full["conv_w"] = _join_from_devices(_unflatten_slabs(cw_all, [_shard_shape("conv_w")])[0], "conv_w")

    vec = lambda a: a.reshape(1, -1)
    loss, grad_x, grads = _local_step(
        x[0], positions[0], full["w_in"], vec(w["b_gate"]), w["sinks"], full["w_branch_a"], full["w_branch_b"], full["w_out"],
        vec(w["ln1_g"]), vec(w["ln1_b"]), full["w_up"], full["conv_w"], vec(w["conv_b"]), full["w_down"], vec(w["ln2_g"]), vec(w["ln2_b"]),
        loss_target[0])

    send_big = _flatten([_split_for_devices(grads[n], n) for n in SHARDED], ADAM_ROWS)
    small = _flatten([grads[n].reshape(1, -1) for n in SMALL] + [loss], 8)
    send_small = jnp.broadcast_to(small[None], (N_DEV,) + small.shape)
    recv_big, recv_small = _all_to_all([send_big, send_small], "all_to_all_grads")

    shard_shapes = [_shard_shape(n) for n in SHARDED]
    pack = lambda d: _flatten([d[n] for n in SHARDED], ADAM_ROWS)
    outs_big = _reduce_adamw(recv_big, pack(w), pack(m), pack(v), "reduce_adamw_sharded")
    small_shapes = [(1, SMALL_SIZE[n]) for n in SMALL] + [(1, LANES)]
    zeros = jnp.zeros((1, LANES), F32)
    pack_small = lambda d: _flatten([d[n].reshape(1, -1) for n in SMALL] + [zeros], 8)
    outs_small = _reduce_adamw(recv_small, pack_small(w), pack_small(m), pack_small(v), "reduce_adamw_replicated")

    res = [{}, {}, {}, {}]
    loss_sum = None
    for k in range(4):
        for n, a in zip(SHARDED, _unflatten(outs_big[k], shard_shapes)):
            res[k][n] = a
        pieces = _unflatten(outs_small[k], small_shapes)
        for n, a in zip(SMALL, pieces):
            res[k][n] = a.reshape(-1)
        if k == 0:
            loss_sum = pieces[-1][0, 0]
    out = [loss_sum, grad_x[None]]
    for k in range(4):
        out += [res[k][n][None] for n in ORDER]
    return tuple(out)


def _unflatten_slabs(flat, shapes):
    v = flat.reshape(N_DEV, -1)
    out, off = [], 0
    for shp in shapes:
        n = int(np.prod(shp))
        out.append(v[:, off:off + n].reshape((N_DEV,) + tuple(shp)))
        off += n + (-n) % LANES
    return out
```

```python
import functools

import jax
import jax.numpy as jnp
import numpy as np
from jax import lax
from jax.experimental import pallas as pl
from jax.experimental.pallas import tpu as pltpu

D_MODEL = 1024
HEAD_DIM = 64
SWA_Q_HEADS = 8
SWA_KV_HEADS = 2
SB_HEADS = 8
WINDOW = 128
ROPE_THETA = 10000.0
D_FF = 2816
LN_EPS = 1e-5
DEPTH = 1
ALPHA = (2.0 * DEPTH) ** 0.25
SWA_Q_WIDTH = SWA_Q_HEADS * HEAD_DIM
SWA_KV_WIDTH = SWA_KV_HEADS * HEAD_DIM
SB_WIDTH = SB_HEADS * HEAD_DIM
GATE_WIDTH = 2 * D_MODEL
IN_WIDTHS = (SWA_Q_WIDTH, SWA_KV_WIDTH, SWA_KV_WIDTH, SB_WIDTH, SB_WIDTH, SB_WIDTH, GATE_WIDTH)
IN_TOTAL = sum(IN_WIDTHS)
ATTN_SCALE = HEAD_DIM ** -0.5

ADAM_LR = 0.001
ADAM_B1 = 0.9
ADAM_B2 = 0.999
ADAM_EPS = 1e-08
ADAM_WD = 0.01
ADAM_STEP = 10

N_DEV = 8
LANES = 128
SB_BLOCK = 256
ADAM_ROWS = 512
VMEM_LIMIT = 56 * 1024 * 1024

F32 = jnp.float32
BF16 = jnp.bfloat16
MESH = pl.DeviceIdType.MESH


def _params(sem=None):
    return pltpu.CompilerParams(dimension_semantics=sem, vmem_limit_bytes=VMEM_LIMIT)


def _dot(a, b):
    return jnp.dot(a, b, preferred_element_type=F32)


def _dot_nt(a, b):
    return lax.dot_general(a, b, (((1,), (1,)), ((), ())), preferred_element_type=F32)


def _dot_tn(a, b):
    return lax.dot_general(a, b, (((0,), (0,)), ((), ())), preferred_element_type=F32)


def _split_bf16(v):
    hi = v.astype(BF16)
    lo = (v - hi.astype(F32)).astype(BF16)
    return hi, lo


def _matmul(a, b, *, kind, out_shape, grid, a_spec, b_spec, out_spec, name, add=None, add_spec=None, add_scale=1.0):
    dot = {"nn": _dot, "nt": _dot_nt, "tn": _dot_tn}[kind]

    def body(*refs):
        if add is None:
            a_ref, b_ref, o_ref = refs
        else:
            a_ref, b_ref, add_ref, o_ref = refs
        r = dot(a_ref[...].astype(BF16), b_ref[...].astype(BF16))
        if add is not None:
            r = r + add_scale * add_ref[...]
        o_ref[...] = r.astype(o_ref.dtype)

    ins = [a, b] + ([] if add is None else [add])
    specs = [a_spec, b_spec] + ([] if add is None else [add_spec])
    return pl.pallas_call(
        body, name=name, grid=grid, in_specs=specs, out_specs=out_spec, out_shape=out_shape,
        compiler_params=_params(("parallel",) * len(grid)),
    )(*ins)


def _rope_tables(pos_col, inv_freq_lanes):
    T = pos_col.shape[0]
    tm = min(512, T)

    def body(pos_ref, f_ref, cos_ref, sin_ref):
        ang = pos_ref[...].astype(F32) * f_ref[...]
        cos_ref[...] = jnp.cos(ang)
        sin_ref[...] = jnp.sin(ang)

    return pl.pallas_call(
        body, name="rope_tables", grid=(T // tm,),
        in_specs=[pl.BlockSpec((tm, 1), lambda i: (i, 0)), pl.BlockSpec((1, LANES), lambda i: (0, 0))],
        out_specs=[pl.BlockSpec((tm, LANES), lambda i: (i, 0))] * 2,
        out_shape=[jax.ShapeDtypeStruct((T, LANES), F32)] * 2,
        compiler_params=_params(("parallel",)),
    )(pos_col, inv_freq_lanes)


def _lane_iota(shape):
    return lax.broadcasted_iota(jnp.int32, shape, len(shape) - 1)


def _rot_half(t):
    first = (_lane_iota(t.shape) % HEAD_DIM) < (HEAD_DIM // 2)
    return jnp.where(first, -pltpu.roll(t, LANES - HEAD_DIM // 2, axis=1), pltpu.roll(t, HEAD_DIM // 2, axis=1))


def _rope(t, cos, sin):
    return t * cos + _rot_half(t) * sin


def _rope_transpose(d, cos, sin):
    return d * cos - _rot_half(d * sin)


_IN_DTYPES = (F32, F32, BF16, BF16, BF16, BF16, F32)


def _in_proj(x, w_in_b):
    T = x.shape[0]
    tm = min(256, T)
    offs = np.cumsum((0,) + IN_WIDTHS)

    def body(x_ref, w_ref, xb_ref, *outs):
        xb = x_ref[...].astype(BF16)
        xb_ref[...] = xb
        for o_ref, a, b in zip(outs, offs[:-1], offs[1:]):
            o_ref[...] = _dot(xb, w_ref[:, a:b]).astype(o_ref.dtype)

    row = lambda n: pl.BlockSpec((tm, n), lambda i: (i, 0))
    return pl.pallas_call(
        body, name="in_proj", grid=(T // tm,),
        in_specs=[row(D_MODEL), pl.BlockSpec((D_MODEL, IN_TOTAL), lambda i: (0, 0))],
        out_specs=[row(D_MODEL)] + [row(n) for n in IN_WIDTHS],
        out_shape=[jax.ShapeDtypeStruct((T, D_MODEL), BF16)] + [jax.ShapeDtypeStruct((T, n), dt) for n, dt in zip(IN_WIDTHS, _IN_DTYPES)],
        compiler_params=_params(("parallel",)),
    )(x, w_in_b)


def _swa_specs(T):
    blk = WINDOW
    cur = lambda n: pl.BlockSpec((blk, n), lambda i: (i, 0))
    prev = lambda n: pl.BlockSpec((blk, n), lambda i: (jnp.maximum(i - 1, 0), 0))
    return blk, cur, prev


def _swa_window(i, kp, kc, vp, vc, cp, cc, sp, sc):
    kwin = jnp.concatenate([_rope(kp, cp, sp), _rope(kc, cc, sc)], axis=0)
    vwin = jnp.concatenate([vp, vc], axis=0)
    lane = _lane_iota(kwin.shape)
    low = lane < HEAD_DIM
    ks, vs = [], []
    for g in range(SWA_KV_HEADS):
        k0 = jnp.where(low, kwin if g == 0 else pltpu.roll(kwin, HEAD_DIM, axis=1), 0.0)
        v0 = jnp.where(low, vwin if g == 0 else pltpu.roll(vwin, HEAD_DIM, axis=1), 0.0)
        ks.append((k0, pltpu.roll(k0, HEAD_DIM, axis=1)))
        vs.append((v0, pltpu.roll(v0, HEAD_DIM, axis=1)))
    blk = WINDOW
    r = lax.broadcasted_iota(jnp.int32, (blk, 2 * blk), 0)
    c = lax.broadcasted_iota(jnp.int32, (blk, 2 * blk), 1)
    rel = blk + r - c
    valid = (rel >= 0) & (rel < WINDOW) & ((c >= blk) | (i > 0))
    return ks, vs, valid


def _swa_probs(qh, kk, valid, sink):
    s = _dot_nt(qh, kk) * ATTN_SCALE
    s = jnp.where(valid, s, -1e30)
    m = jnp.maximum(jnp.max(s, axis=1, keepdims=True), sink)
    p = jnp.where(valid, jnp.exp(s - m), 0.0)
    es = jnp.exp(sink - m)
    den = jnp.sum(p, axis=1, keepdims=True) + es
    return p / den, es / den


def _swa_fwd(qa, ka, va, cos, sin, sinks):
    T = qa.shape[0]
    blk, cur, prev = _swa_specs(T)

    def body(sink_ref, q_ref, kp_ref, kc_ref, vp_ref, vc_ref, cp_ref, cc_ref, sp_ref, sc_ref, o_ref):
        i = pl.program_id(0)
        cc, sc = cc_ref[...], sc_ref[...]
        ks, vs, valid = _swa_window(i, kp_ref[...], kc_ref[...], vp_ref[...].astype(F32), vc_ref[...].astype(F32),
                                    cp_ref[...], cc, sp_ref[...], sc)
        lane = _lane_iota((blk, LANES))
        for pp in range(SWA_Q_HEADS // 2):
            g = pp // (SWA_Q_HEADS // SWA_KV_HEADS // 2)
            qp = _rope(q_ref[:, pp * LANES:(pp + 1) * LANES], cc, sc)
            out = jnp.zeros((blk, LANES), F32)
            for hh in range(2):
                half = (lane >= hh * HEAD_DIM) & (lane < (hh + 1) * HEAD_DIM)
                qh = jnp.where(half, qp, 0.0).astype(BF16)
                probs, _ = _swa_probs(qh, ks[g][hh].astype(BF16), valid, sink_ref[2 * pp + hh])
                out = out + _dot(probs.astype(BF16), vs[g][hh].astype(BF16))
            o_ref[:, pp * LANES:(pp + 1) * LANES] = out.astype(o_ref.dtype)

    return pl.pallas_call(
        body, name="swa_fwd", grid=(T // blk,),
        in_specs=[pl.BlockSpec(memory_space=pltpu.SMEM), cur(SWA_Q_WIDTH), prev(LANES), cur(LANES), prev(LANES), cur(LANES),
                  prev(LANES), cur(LANES), prev(LANES), cur(LANES)],
        out_specs=cur(SWA_Q_WIDTH),
        out_shape=jax.ShapeDtypeStruct((T, SWA_Q_WIDTH), BF16),
        compiler_params=_params(("parallel",)),
    )(sinks, qa, ka, ka, va, va, cos, cos, sin, sin)


def _swa_bwd(qa, ka, va, cos, sin, sinks, dya):
    T = qa.shape[0]
    blk, cur, prev = _swa_specs(T)
    full = lambda n: pl.BlockSpec((T, n), lambda i: (0, 0))

    def body(sink_ref, q_ref, kp_ref, kc_ref, vp_ref, vc_ref, cp_ref, cc_ref, sp_ref, sc_ref, do_ref,
             dq_ref, dk_ref, dv_ref, dsink_ref):
        i = pl.program_id(0)

        @pl.when(i == 0)
        def _():
            dk_ref[...] = jnp.zeros_like(dk_ref)
            dv_ref[...] = jnp.zeros_like(dv_ref)
            dsink_ref[...] = jnp.zeros_like(dsink_ref)

        cp, cc, sp, sc = cp_ref[...], cc_ref[...], sp_ref[...], sc_ref[...]
        ks, vs, valid = _swa_window(i, kp_ref[...], kc_ref[...], vp_ref[...].astype(F32), vc_ref[...].astype(F32), cp, cc, sp, sc)
        lane = _lane_iota((blk, LANES))
        lane1 = _lane_iota((1, LANES))
        dkw = jnp.zeros((2 * blk, LANES), F32)
        dvw = jnp.zeros((2 * blk, LANES), F32)
        dsink = jnp.zeros((1, LANES), F32)
        for pp in range(SWA_Q_HEADS // 2):
            g = pp // (SWA_Q_HEADS // SWA_KV_HEADS // 2)
            qp = _rope(q_ref[:, pp * LANES:(pp + 1) * LANES], cc, sc)
            dop = do_ref[:, pp * LANES:(pp + 1) * LANES]
            dqp = jnp.zeros((blk, LANES), F32)
            for hh in range(2):
                half = (lane >= hh * HEAD_DIM) & (lane < (hh + 1) * HEAD_DIM)
                qh = jnp.where(half, qp, 0.0).astype(BF16)
                doh = jnp.where(half, dop, 0.0).astype(BF16)
                kk = ks[g][hh].astype(BF16)
                vv = vs[g][hh].astype(BF16)
                probs, psink = _swa_probs(qh, kk, valid, sink_ref[2 * pp + hh])
                dp = _dot_nt(doh, vv)
                dsum = jnp.sum(probs * dp, axis=1, keepdims=True)
                ds = (probs * (dp - dsum) * ATTN_SCALE).astype(BF16)
                dsink = dsink + jnp.where(lane1 == 2 * pp + hh, -jnp.sum(psink * dsum), 0.0)
                dqp = dqp + _dot(ds, kk)
                dk_h = _dot_tn(ds, qh)
                dv_h = _dot_tn(probs.astype(BF16), doh)
                if hh != g:
                    dk_h = pltpu.roll(dk_h, HEAD_DIM, axis=1)
                    dv_h = pltpu.roll(dv_h, HEAD_DIM, axis=1)
                dkw = dkw + dk_h
                dvw = dvw + dv_h
            dq_ref[:, pp * LANES:(pp + 1) * LANES] = _rope_transpose(dqp, cc, sc).astype(dq_ref.dtype)
        dsink_ref[...] += dsink
        ip = jnp.maximum(i - 1, 0)
        rows_p = pl.ds(pl.multiple_of(ip * blk, blk), blk)
        rows_c = pl.ds(pl.multiple_of(i * blk, blk), blk)
        dk_ref[rows_p, :] += _rope_transpose(dkw[:blk], cp, sp)
        dv_ref[rows_p, :] += dvw[:blk]
        dk_ref[rows_c, :] += _rope_transpose(dkw[blk:], cc, sc)
        dv_ref[rows_c, :] += dvw[blk:]

    return pl.pallas_call(
        body, name="swa_bwd", grid=(T // blk,),
        in_specs=[pl.BlockSpec(memory_space=pltpu.SMEM), cur(SWA_Q_WIDTH), prev(LANES), cur(LANES), prev(LANES), cur(LANES),
                  prev(LANES), cur(LANES), prev(LANES), cur(LANES), cur(SWA_Q_WIDTH)],
        out_specs=[cur(SWA_Q_WIDTH), full(LANES), full(LANES), pl.BlockSpec((1, LANES), lambda i: (0, 0))],
        out_shape=[jax.ShapeDtypeStruct((T, SWA_Q_WIDTH), BF16), jax.ShapeDtypeStruct((T, LANES), F32),
                   jax.ShapeDtypeStruct((T, LANES), F32), jax.ShapeDtypeStruct((1, LANES), F32)],
        compiler_params=_params(("arbitrary",)),
    )(sinks, qa, ka, ka, va, va, cos, cos, sin, sin, dya)


def _sb_scores(qm, k, diag):
    B = SB_BLOCK
    z = _dot_nt(qm, k) * ATTN_SCALE
    sp = jnp.maximum(z, 0.0) + jnp.log1p(jnp.exp(-jnp.abs(z)))
    log_one_minus = -sp
    log_beta = z - sp
    valid = None
    if diag:
        r = lax.broadcasted_iota(jnp.int32, (B, B), 0)
        c = lax.broadcasted_iota(jnp.int32, (B, B), 1)
        valid = c < r
        log_one_minus = jnp.where(valid, log_one_minus, 0.0)
    return log_one_minus, log_beta, valid


def _tri(B, cmp):
    r = lax.broadcasted_iota(jnp.int32, (B, B), 0)
    c = lax.broadcasted_iota(jnp.int32, (B, B), 1)
    return cmp(r, c).astype(BF16)


def _sb_fwd(qb, kb, vb):
    T = qb.shape[0]
    B = min(SB_BLOCK, T)
    assert T // B <= HEAD_DIM
    n_pairs = SB_HEADS // 2

    def body(q_ref, k_ref, v_ref, o_ref, carry_ref):
        i = pl.program_id(1)
        q = q_ref[...]
        lane = _lane_iota((1, LANES))
        upper = _tri(B, lambda r, c: r > c)
        acc = jnp.zeros((B, LANES), F32)
        cm = jnp.zeros((B, LANES), F32)
        for hh in range(2):
            half = (lane >= hh * HEAD_DIM) & (lane < (hh + 1) * HEAD_DIM)
            qm = jnp.where(half, q, jnp.zeros_like(q))

            def block(j, c, acc, cm, diag, half=half, qm=qm, hh=hh):
                rows = pl.ds(pl.multiple_of(j * B, B), B)
                k = k_ref[rows, :]
                v = jnp.where(half, v_ref[rows, :], jnp.zeros((), BF16))
                lom, lb, valid = _sb_scores(qm, k, diag)
                cm = jnp.where(lane == hh * HEAD_DIM + j, c, cm)
                hi, lo = _split_bf16(lom)
                after = c + _dot(hi, upper) + _dot(lo, upper)
                a = jnp.exp(lb + after)
                if diag:
                    a = jnp.where(valid, a, 0.0)
                acc = acc + _dot(a.astype(BF16), v)
                c = c + jnp.sum(lom, axis=1, keepdims=True)
                return c, acc, cm

            c, acc, cm = block(i, jnp.zeros((B, 1), F32), acc, cm, True)
            c, acc, cm = lax.fori_loop(0, i, lambda jj, s: block(i - 1 - jj, *s, False), (c, acc, cm))
        o_ref[...] = acc.astype(o_ref.dtype)
        carry_ref[...] = cm

    return pl.pallas_call(
        body, name="sb_fwd", grid=(n_pairs, T // B),
        in_specs=[pl.BlockSpec((B, LANES), lambda p, i: (i, p)), pl.BlockSpec((T, LANES), lambda p, i: (0, p)),
                  pl.BlockSpec((T, LANES), lambda p, i: (0, p))],
        out_specs=[pl.BlockSpec((B, LANES), lambda p, i: (i, p))] * 2,
        out_shape=[jax.ShapeDtypeStruct((T, SB_WIDTH), BF16), jax.ShapeDtypeStruct((T, SB_WIDTH), F32)],
        compiler_params=_params(("parallel", "parallel")),
    )(qb, kb, vb)


def _sb_bwd(qb, kb, vb, carries, dyb):
    T = qb.shape[0]
    B = min(SB_BLOCK, T)
    n_pairs = SB_HEADS // 2

    def body(q_ref, k_ref, v_ref, carry_ref, do_ref, dq_ref, dk_ref, dv_ref):
        i = pl.program_id(1)

        @pl.when(i == 0)
        def _():
            dk_ref[...] = jnp.zeros_like(dk_ref)
            dv_ref[...] = jnp.zeros_like(dv_ref)

        q = q_ref[...]
        do = do_ref[...]
        cm = carry_ref[...]
        lane = _lane_iota((1, LANES))
        upper = _tri(B, lambda r, c: r > c)
        lower = _tri(B, lambda r, c: r < c)
        dq = jnp.zeros((B, LANES), F32)
        for hh in range(2):
            half = (lane >= hh * HEAD_DIM) & (lane < (hh + 1) * HEAD_DIM)
            qm = jnp.where(half, q, jnp.zeros_like(q))
            dom = jnp.where(half, do, jnp.zeros_like(do))

            def block(j, cg, dq, diag, half=half, qm=qm, dom=dom, hh=hh):
                rows = pl.ds(pl.multiple_of(j * B, B), B)
                k = k_ref[rows, :]
                v = v_ref[rows, :]
                km = jnp.where(half, k, jnp.zeros((), BF16))
                lom, lb, valid = _sb_scores(qm, k, diag)
                c = jnp.sum(jnp.where(lane == hh * HEAD_DIM + j, cm, 0.0), axis=1, keepdims=True)
                hi, lo = _split_bf16(lom)
                after = c + _dot(hi, upper) + _dot(lo, upper)
                a = jnp.exp(lb + after)
                if diag:
                    a = jnp.where(valid, a, 0.0)
                g = a * _dot_nt(dom, v)
                ghi, glo = _split_bf16(g)
                gpre = cg + _dot(ghi, lower) + _dot(glo, lower)
                beta = jnp.exp(lb)
                dz = g * (1.0 - beta) - gpre * beta
                if diag:
                    dz = jnp.where(valid, dz, 0.0)
                dzb = (dz * ATTN_SCALE).astype(BF16)
                dq = dq + _dot(dzb, km)
                dk_ref[rows, :] += _dot_tn(dzb, qm)
                dv_ref[rows, :] += _dot_tn(a.astype(BF16), dom)
                cg = cg + jnp.sum(g, axis=1, keepdims=True)
                return cg, dq

            cg, dq = lax.fori_loop(0, i, lambda j, s: block(j, *s, False), (jnp.zeros((B, 1), F32), dq))
            cg, dq = block(i, cg, dq, True)
        dq_ref[...] = dq.astype(dq_ref.dtype)

    blk = pl.BlockSpec((B, LANES), lambda p, i: (i, p))
    full = pl.BlockSpec((T, LANES), lambda p, i: (0, p))
    return pl.pallas_call(
        body, name="sb_bwd", grid=(n_pairs, T // B),
        in_specs=[blk, full, full, blk, blk],
        out_specs=[blk, full, full],
        out_shape=[jax.ShapeDtypeStruct((T, SB_WIDTH), BF16), jax.ShapeDtypeStruct((T, SB_WIDTH), F32),
                   jax.ShapeDtypeStruct((T, SB_WIDTH), F32)],
        compiler_params=_params(("parallel", "arbitrary")),
    )(qb, kb, vb, carries, dyb)


def _ln_stats(u):
    mu = jnp.mean(u, axis=-1, keepdims=True)
    xc = u - mu
    var = jnp.mean(xc * xc, axis=-1, keepdims=True)
    rstd = lax.rsqrt(var + LN_EPS)
    return xc * rstd, rstd


def _ln_bwd(dy, xhat, rstd, g):
    dxh = dy * g
    return rstd * (dxh - jnp.mean(dxh, axis=-1, keepdims=True) - xhat * jnp.mean(dxh * xhat, axis=-1, keepdims=True))


def _gates(gl_ref, bg_ref):
    ga = jax.nn.sigmoid(gl_ref[:, :D_MODEL] + bg_ref[:, :D_MODEL])
    gb = jax.nn.sigmoid(gl_ref[:, D_MODEL:] + bg_ref[:, D_MODEL:])
    return ga, gb


def _mix_fwd(ya, yb, gl, x, wa, wb, wo, b_gate, ln1_g, ln1_b):
    T = x.shape[0]
    tm = min(256, T)

    def body(ya_ref, yb_ref, gl_ref, x_ref, wa_ref, wb_ref, wo_ref, bg_ref, g_ref, b_ref, h_ref, u_ref, x1_ref):
        ga, gb = _gates(gl_ref, bg_ref)
        h = (ga * _dot(ya_ref[...], wa_ref[...]) + gb * _dot(yb_ref[...], wb_ref[...])).astype(BF16)
        h_ref[...] = h
        u = ALPHA * x_ref[...] + _dot(h, wo_ref[...])
        u_ref[...] = u
        xhat, _ = _ln_stats(u)
        x1_ref[...] = (xhat * g_ref[...] + b_ref[...]).astype(BF16)

    row = lambda n: pl.BlockSpec((tm, n), lambda i: (i, 0))
    const = lambda r, n: pl.BlockSpec((r, n), lambda i: (0, 0))
    return pl.pallas_call(
        body, name="mix_fwd", grid=(T // tm,),
        in_specs=[row(SWA_Q_WIDTH), row(SB_WIDTH), row(GATE_WIDTH), row(D_MODEL), const(SWA_Q_WIDTH, D_MODEL), const(SB_WIDTH, D_MODEL),
                  const(D_MODEL, D_MODEL), const(1, GATE_WIDTH), const(1, D_MODEL), const(1, D_MODEL)],
        out_specs=[row(D_MODEL)] * 3,
        out_shape=[jax.ShapeDtypeStruct((T, D_MODEL), BF16), jax.ShapeDtypeStruct((T, D_MODEL), F32), jax.ShapeDtypeStruct((T, D_MODEL), BF16)],
        compiler_params=_params(("parallel",)),
    )(ya, yb, gl, x, wa, wb, wo, b_gate, ln1_g, ln1_b)


def _mix_bwd(du1, ya, yb, gl, wa, wb, wo, b_gate):
    T = du1.shape[0]
    tm = min(256, T)

    def body(du_ref, ya_ref, yb_ref, gl_ref, wa_ref, wb_ref, wo_ref, bg_ref, dya_ref, dyb_ref, dgl_ref, dta_ref, dtb_ref, dbg_ref):
        @pl.when(pl.program_id(0) == 0)
        def _():
            dbg_ref[...] = jnp.zeros_like(dbg_ref)

        dh = _dot_nt(du_ref[...].astype(BF16), wo_ref[...])
        ga, gb = _gates(gl_ref, bg_ref)
        for gate, y_ref, w_ref, dy_ref, dt_ref, lo in ((ga, ya_ref, wa_ref, dya_ref, dta_ref, 0), (gb, yb_ref, wb_ref, dyb_ref, dtb_ref, D_MODEL)):
            t = _dot(y_ref[...], w_ref[...])
            dlogit = dh * t * gate * (1.0 - gate)
            dgl_ref[:, lo:lo + D_MODEL] = dlogit.astype(BF16)
            dbg_ref[:, lo:lo + D_MODEL] += jnp.sum(dlogit, axis=0, keepdims=True)
            dt = (dh * gate).astype(BF16)
            dt_ref[...] = dt
            dy_ref[...] = _dot_nt(dt, w_ref[...]).astype(BF16)

    row = lambda n: pl.BlockSpec((tm, n), lambda i: (i, 0))
    const = lambda r, n: pl.BlockSpec((r, n), lambda i: (0, 0))
    sds = lambda n, dt: jax.ShapeDtypeStruct((T, n), dt)
    return pl.pallas_call(
        body, name="mix_bwd", grid=(T // tm,),
        in_specs=[row(D_MODEL), row(SWA_Q_WIDTH), row(SB_WIDTH), row(GATE_WIDTH), const(SWA_Q_WIDTH, D_MODEL), const(SB_WIDTH, D_MODEL),
                  const(D_MODEL, D_MODEL), const(1, GATE_WIDTH)],
        out_specs=[row(SWA_Q_WIDTH), row(SB_WIDTH), row(GATE_WIDTH), row(D_MODEL), row(D_MODEL), const(1, GATE_WIDTH)],
        out_shape=[sds(SWA_Q_WIDTH, BF16), sds(SB_WIDTH, BF16), sds(GATE_WIDTH, BF16), sds(D_MODEL, BF16), sds(D_MODEL, BF16),
                   jax.ShapeDtypeStruct((1, GATE_WIDTH), F32)],
        compiler_params=_params(("arbitrary",)),
    )(du1, ya, yb, gl, wa, wb, wo, b_gate)


CONV_COLS = LANES


def _shift_down(v, k):
    row = lax.broadcasted_iota(jnp.int32, v.shape, 0)
    return jnp.where(row >= k, pltpu.roll(v, k, axis=0), 0.0)


def _shift_up(v, k):
    n = v.shape[0]
    row = lax.broadcasted_iota(jnp.int32, v.shape, 0)
    return jnp.where(row < n - k, pltpu.roll(v, n - k, axis=0), 0.0)


def _conv(pv, w_ref, b_ref):
    return w_ref[0:1, :] * _shift_down(pv, 2) + w_ref[1:2, :] * _shift_down(pv, 1) + w_ref[2:3, :] * pv + b_ref[...]


def _conv_specs(T):
    nb = D_FF // CONV_COLS
    pair = pl.BlockSpec((2, T, CONV_COLS), lambda j: (0, 0, j))
    gate = lambda r: pl.BlockSpec((r, CONV_COLS), lambda j: (0, j))
    up = lambda r: pl.BlockSpec((r, CONV_COLS), lambda j: (0, j + nb))
    return nb, pair, gate, up


def _conv_glu_fwd(p3, conv_w, conv_b):
    T = p3.shape[1]
    nb, pair, gate, up = _conv_specs(T)

    def body(p_ref, wg_ref, wu_ref, bg_ref, bu_ref, s_ref):
        ag = _conv(p_ref[0], wg_ref, bg_ref)
        au = _conv(p_ref[1], wu_ref, bu_ref)
        s_ref[...] = (ag * jax.nn.sigmoid(ag) * au).astype(BF16)

    return pl.pallas_call(
        body, name="conv_glu_fwd", grid=(nb,),
        in_specs=[pair, gate(3), up(3), gate(1), up(1)],
        out_specs=pl.BlockSpec((T, CONV_COLS), lambda j: (0, j)),
        out_shape=jax.ShapeDtypeStruct((T, D_FF), BF16),
        compiler_params=_params(("parallel",)),
    )(p3, conv_w, conv_w, conv_b, conv_b)


def _conv_glu_bwd(p3, ds, conv_w, conv_b):
    T = p3.shape[1]
    nb, pair, gate, up = _conv_specs(T)

    def body(p_ref, ds_ref, wg_ref, wu_ref, bg_ref, bu_ref, dp_ref, dwg_ref, dwu_ref, dbg_ref, dbu_ref):
        pg, pu = p_ref[0], p_ref[1]
        ag = _conv(pg, wg_ref, bg_ref)
        au = _conv(pu, wu_ref, bu_ref)
        sg = jax.nn.sigmoid(ag)
        d = ds_ref[...]
        dau = d * ag * sg
        dag = d * au * (sg * (1.0 + ag * (1.0 - sg)))
        for half, (da, pv, w_ref, dw_ref, db_ref) in enumerate(((dag, pg, wg_ref, dwg_ref, dbg_ref), (dau, pu, wu_ref, dwu_ref, dbu_ref))):
            db_ref[...] = jnp.sum(da, axis=0, keepdims=True)
            dw_ref[0:1, :] = jnp.sum(da * _shift_down(pv, 2), axis=0, keepdims=True)
            dw_ref[1:2, :] = jnp.sum(da * _shift_down(pv, 1), axis=0, keepdims=True)
            dw_ref[2:3, :] = jnp.sum(da * pv, axis=0, keepdims=True)
            dp = w_ref[2:3, :] * da + w_ref[1:2, :] * _shift_up(da, 1) + w_ref[0:1, :] * _shift_up(da, 2)
            dp_ref[half] = dp.astype(BF16)

    col = lambda r: pl.BlockSpec((r, CONV_COLS), lambda j: (0, j))
    return pl.pallas_call(
        body, name="conv_glu_bwd", grid=(nb,),
        in_specs=[pair, col(T), gate(3), up(3), gate(1), up(1)],
        out_specs=[pair, col(3), col(3), col(1), col(1)],
        out_shape=[jax.ShapeDtypeStruct((2, T, D_FF), BF16), jax.ShapeDtypeStruct((3, D_FF), F32), jax.ShapeDtypeStruct((3, D_FF), F32),
                   jax.ShapeDtypeStruct((1, D_FF), F32), jax.ShapeDtypeStruct((1, D_FF), F32)],
        compiler_params=_params(("parallel",)),
    )(p3, ds, conv_w, conv_w, conv_b, conv_b)


def _ffn_down_loss(s, w_down, u1, ln1_g, ln1_b, ln2_g, ln2_b, target):
    T = u1.shape[0]
    tm = min(256, T)

    def body(s_ref, w_ref, u1_ref, g1_ref, b1_ref, g2_ref, b2_ref, t_ref, du_ref, dub_ref, dg_ref, db_ref, loss_ref):
        @pl.when(pl.program_id(0) == 0)
        def _():
            dg_ref[...] = jnp.zeros_like(dg_ref)
            db_ref[...] = jnp.zeros_like(db_ref)
            loss_ref[...] = jnp.zeros_like(loss_ref)

        xh1, _ = _ln_stats(u1_ref[...])
        x1 = xh1 * g1_ref[...] + b1_ref[...]
        u2 = ALPHA * x1 + _dot(s_ref[...], w_ref[...])
        xh2, rstd2 = _ln_stats(u2)
        err = xh2 * g2_ref[...] + b2_ref[...] - t_ref[...]
        per_token = jnp.mean(err * err, axis=-1, keepdims=True)
        loss_ref[...] += 0.5 * jnp.sum(per_token, axis=0, keepdims=True)
        dy = err * (1.0 / D_MODEL)
        dg_ref[...] += jnp.sum(dy * xh2, axis=0, keepdims=True)
        db_ref[...] += jnp.sum(dy, axis=0, keepdims=True)
        du2 = _ln_bwd(dy, xh2, rstd2, g2_ref[...])
        du_ref[...] = du2
        dub_ref[...] = du2.astype(BF16)

    row = lambda n: pl.BlockSpec((tm, n), lambda i: (i, 0))
    const = lambda r, n: pl.BlockSpec((r, n), lambda i: (0, 0))
    vec = const(1, D_MODEL)
    return pl.pallas_call(
        body, name="ffn_down_loss", grid=(T // tm,),
        in_specs=[row(D_FF), const(D_FF, D_MODEL), row(D_MODEL), vec, vec, vec, vec, row(D_MODEL)],
        out_specs=[row(D_MODEL), row(D_MODEL), vec, vec, const(1, LANES)],
        out_shape=[jax.ShapeDtypeStruct((T, D_MODEL), F32), jax.ShapeDtypeStruct((T, D_MODEL), BF16), jax.ShapeDtypeStruct((1, D_MODEL), F32),
                   jax.ShapeDtypeStruct((1, D_MODEL), F32), jax.ShapeDtypeStruct((1, LANES), F32)],
        compiler_params=_params(("arbitrary",)),
    )(s, w_down, u1, ln1_g, ln1_b, ln2_g, ln2_b, target)


def _ffn_up_bwd_ln1(dp3, w_up, du2, u1, ln1_g):
    T = u1.shape[0]
    tm = min(256, T)

    def body(dp_ref, w_ref, du2_ref, u1_ref, g_ref, du_ref, dub_ref, dg_ref, db_ref):
        @pl.when(pl.program_id(0) == 0)
        def _():
            dg_ref[...] = jnp.zeros_like(dg_ref)
            db_ref[...] = jnp.zeros_like(db_ref)

        dx1 = _dot_nt(dp_ref[0], w_ref[:, :D_FF]) + _dot_nt(dp_ref[1], w_ref[:, D_FF:]) + ALPHA * du2_ref[...]
        xh, rstd = _ln_stats(u1_ref[...])
        dg_ref[...] += jnp.sum(dx1 * xh, axis=0, keepdims=True)
        db_ref[...] += jnp.sum(dx1, axis=0, keepdims=True)
        du1 = _ln_bwd(dx1, xh, rstd, g_ref[...])
        du_ref[...] = du1
        dub_ref[...] = du1.astype(BF16)

    row = lambda n: pl.BlockSpec((tm, n), lambda i: (i, 0))
    const = lambda r, n: pl.BlockSpec((r, n), lambda i: (0, 0))
    vec = const(1, D_MODEL)
    return pl.pallas_call(
        body, name="ffn_up_bwd_ln1", grid=(T // tm,),
        in_specs=[pl.BlockSpec((2, tm, D_FF), lambda i: (0, i, 0)), const(D_MODEL, 2 * D_FF), row(D_MODEL), row(D_MODEL), vec],
        out_specs=[row(D_MODEL), row(D_MODEL), vec, vec],
        out_shape=[jax.ShapeDtypeStruct((T, D_MODEL), F32), jax.ShapeDtypeStruct((T, D_MODEL), BF16), jax.ShapeDtypeStruct((1, D_MODEL), F32),
                   jax.ShapeDtypeStruct((1, D_MODEL), F32)],
        compiler_params=_params(("arbitrary",)),
    )(dp3, w_up, du2, u1, ln1_g)


def _local_step(x, positions, w_in, b_gate, sinks, wa, wb, wo, ln1_g, ln1_b, w_up, conv_w, conv_b, w_down, ln2_g, ln2_b, target):
    T = x.shape[0]
    inv_freq = 1.0 / (ROPE_THETA ** (jnp.arange(0, HEAD_DIM, 2, dtype=F32) / HEAD_DIM))
    cos, sin = _rope_tables(positions.reshape(T, 1), jnp.tile(inv_freq, LANES // (HEAD_DIM // 2)).reshape(1, LANES))

    xb, qa, ka, va, qb, kb, vb, gl = _in_proj(x, w_in)
    ya = _swa_fwd(qa, ka, va, cos, sin, sinks)
    yb, carries = _sb_fwd(qb, kb, vb)
    h, u1, x1 = _mix_fwd(ya, yb, gl, x, wa, wb, wo, b_gate, ln1_g, ln1_b)

    ff_tn = D_FF // 2
    nff = D_FF // ff_tn
    tm = min(512, T)
    p3 = _matmul(x1, w_up, kind="nn", name="ffn_up", grid=(T // tm, 2 * nff),
                 a_spec=pl.BlockSpec((tm, D_MODEL), lambda i, j: (i, 0)), b_spec=pl.BlockSpec((D_MODEL, ff_tn), lambda i, j: (0, j)),
                 out_spec=pl.BlockSpec((None, tm, ff_tn), lambda i, j: (j // nff, i, j % nff)),
                 out_shape=jax.ShapeDtypeStruct((2, T, D_FF), F32))
    s = _conv_glu_fwd(p3, conv_w, conv_b)
    du2, du2b, dln2_g, dln2_b, loss = _ffn_down_loss(s, w_down, u1, ln1_g, ln1_b, ln2_g, ln2_b, target)

    ds = _matmul(du2b, w_down, kind="nt", name="ffn_down_bwd", grid=(T // tm, nff),
                 a_spec=pl.BlockSpec((tm, D_MODEL), lambda i, j: (i, 0)), b_spec=pl.BlockSpec((ff_tn, D_MODEL), lambda i, j: (j, 0)),
                 out_spec=pl.BlockSpec((tm, ff_tn), lambda i, j: (i, j)), out_shape=jax.ShapeDtypeStruct((T, D_FF), F32))
    dp3, dcw_g, dcw_u, dcb_g, dcb_u = _conv_glu_bwd(p3, ds, conv_w, conv_b)
    tk = 256
    dw_down = _matmul(s, du2b, kind="tn", name="dw_down", grid=(D_FF // tk,),
                      a_spec=pl.BlockSpec((T, tk), lambda i: (0, i)), b_spec=pl.BlockSpec((T, D_MODEL), lambda i: (0, 0)),
                      out_spec=pl.BlockSpec((tk, D_MODEL), lambda i: (i, 0)), out_shape=jax.ShapeDtypeStruct((D_FF, D_MODEL), BF16))
    dw_up = _matmul(x1, dp3, kind="tn", name="dw_up", grid=(D_MODEL // 512, 2 * nff),
                    a_spec=pl.BlockSpec((T, 512), lambda i, j: (0, i)), b_spec=pl.BlockSpec((None, T, ff_tn), lambda i, j: (j // nff, 0, j % nff)),
                    out_spec=pl.BlockSpec((512, ff_tn), lambda i, j: (i, j)), out_shape=jax.ShapeDtypeStruct((D_MODEL, 2 * D_FF), BF16))
    du1, du1b, dln1_g, dln1_b = _ffn_up_bwd_ln1(dp3, w_up, du2, u1, ln1_g)
    dya, dyb, dgl, dta, dtb, db_gate = _mix_bwd(du1, ya, yb, gl, wa, wb, wo, b_gate)

    def dw_tn(a, g, name):
        rows, cols = a.shape[1], g.shape[1]
        tn = min(512, cols)
        return _matmul(a, g, kind="tn", name=name, grid=(rows // 512, cols // tn),
                       a_spec=pl.BlockSpec((T, 512), lambda i, j: (0, i)), b_spec=pl.BlockSpec((T, tn), lambda i, j: (0, j)),
                       out_spec=pl.BlockSpec((512, tn), lambda i, j: (i, j)), out_shape=jax.ShapeDtypeStruct((rows, cols), BF16))

    dwa = dw_tn(ya, dta, "dw_branch_a")
    dwb = dw_tn(yb, dtb, "dw_branch_b")
    dwo = dw_tn(h, du1b, "dw_out")

    dqb, dkb, dvb = _sb_bwd(qb, kb, vb, carries, dyb)
    dqa, dka, dva, dsinks = _swa_bwd(qa, ka, va, cos, sin, sinks, dya)
    dproj = (dqa, dka, dva, dqb, dkb, dvb, dgl)
    dw_in = tuple(dw_tn(xb, piece, f"dw_in_{k}") for k, piece in enumerate(dproj))
    grad_x = _grad_x(dproj, w_in, du1)
    grads = dict(
        w_in=dw_in, b_gate=db_gate, sinks=dsinks, w_branch_a=dwa, w_branch_b=dwb, w_out=dwo, ln1_g=dln1_g, ln1_b=dln1_b,
        w_up=dw_up, conv_w=jnp.concatenate([dcw_g, dcw_u], axis=1), conv_b=(dcb_g, dcb_u), w_down=dw_down, ln2_g=dln2_g, ln2_b=dln2_b)
    return loss, grad_x, grads


def _grad_x(dproj, w_in, du1):
    T = du1.shape[0]
    tm = min(256, T)
    offs = np.cumsum((0,) + IN_WIDTHS)

    def body(*refs):
        pieces, (w_ref, du_ref, o_ref) = refs[:len(IN_WIDTHS)], refs[len(IN_WIDTHS):]
        acc = ALPHA * du_ref[...]
        for p_ref, a, b in zip(pieces, offs[:-1], offs[1:]):
            acc = acc + _dot_nt(p_ref[...].astype(BF16), w_ref[:, a:b])
        o_ref[...] = acc

    row = lambda n: pl.BlockSpec((tm, n), lambda i: (i, 0))
    return pl.pallas_call(
        body, name="grad_x", grid=(T // tm,),
        in_specs=[row(n) for n in IN_WIDTHS] + [pl.BlockSpec((D_MODEL, IN_TOTAL), lambda i: (0, 0)), row(D_MODEL)],
        out_specs=row(D_MODEL), out_shape=jax.ShapeDtypeStruct((T, D_MODEL), F32),
        compiler_params=_params(("parallel",)),
    )(*dproj, w_in, du1)


ANY = pl.BlockSpec(memory_space=pl.ANY)


def _all_gather(slabs, name):
    n = len(slabs)

    def body(*refs):
        ins, outs = refs[:n], refs[n:2 * n]
        send_sems, recv_sems, local_sems = refs[2 * n:]
        x, y, c = lax.axis_index("x"), lax.axis_index("y"), lax.axis_index("c")
        me, sibling = (x, y, c), (x, y, 1 - c)
        chips = [(1 - x, y), (x, 1 - y), (1 - x, 1 - y)]

        def slot(pos):
            return 4 * pos[0] + 2 * pos[1] + pos[2]

        def copy(a, k, block, to, from_input=False):
            return pltpu.make_async_remote_copy(
                src_ref=ins[a] if from_input else outs[a].at[slot(block)], dst_ref=outs[a].at[slot(block)],
                send_sem=send_sems.at[a, k], recv_sem=recv_sems.at[a, k], device_id=to, device_id_type=MESH)

        mine = [pltpu.make_async_copy(ins[a], outs[a].at[slot(me)], local_sems.at[a]) for a in range(n)]
        for cp in mine:
            cp.start()
        first = []
        for a in range(n):
            first.append(copy(a, 0, me, sibling, from_input=True))
            first += [copy(a, 1 + j, me, (*chip, c), from_input=True) for j, chip in enumerate(chips)]
        for cp in first:
            cp.start()
        passed = []
        for j, chip in enumerate(chips):
            for a in range(n):
                copy(a, 1 + j, (*chip, c), me).wait_recv()
                fwd = copy(a, 4 + j, (*chip, c), sibling)
                fwd.start()
                passed.append(fwd)
        for a in range(n):
            copy(a, 0, sibling, me).wait_recv()
            for j, chip in enumerate(chips):
                copy(a, 4 + j, (*chip, 1 - c), me).wait_recv()
        for cp in first + passed:
            cp.wait_send()
        for cp in mine:
            cp.wait()

    return pl.pallas_call(
        body, name=name,
        in_specs=[ANY] * n, out_specs=[ANY] * n,
        out_shape=[jax.ShapeDtypeStruct((N_DEV,) + s.shape, s.dtype) for s in slabs],
        scratch_shapes=[pltpu.SemaphoreType.DMA((n, 7)), pltpu.SemaphoreType.DMA((n, 7)), pltpu.SemaphoreType.DMA((n,))],
    )(*slabs)


def _all_to_all(slabs, name):
    n = len(slabs)

    def body(*refs):
        ins, outs = refs[:n], refs[n:2 * n]
        send_sems, recv_sems, local_sems = refs[2 * n:]
        x, y, c = lax.axis_index("x"), lax.axis_index("y"), lax.axis_index("c")
        my_slot = 4 * x + 2 * y + c
        flips = [(fx, fy, fc) for fx in (0, 1) for fy in (0, 1) for fc in (0, 1) if (fx, fy, fc) != (0, 0, 0)]

        def copy(a, k):
            fx, fy, fc = flips[k]
            peer = (x ^ fx, y ^ fy, c ^ fc)
            peer_slot = 4 * peer[0] + 2 * peer[1] + peer[2]
            send = pltpu.make_async_remote_copy(src_ref=ins[a].at[peer_slot], dst_ref=outs[a].at[my_slot], send_sem=send_sems.at[a, k],
                                                recv_sem=recv_sems.at[a, k], device_id=peer, device_id_type=MESH)
            recv = pltpu.make_async_remote_copy(src_ref=ins[a].at[peer_slot], dst_ref=outs[a].at[peer_slot], send_sem=send_sems.at[a, k],
                                                recv_sem=recv_sems.at[a, k], device_id=peer, device_id_type=MESH)
            return send, recv

        mine = [pltpu.make_async_copy(ins[a].at[my_slot], outs[a].at[my_slot], local_sems.at[a]) for a in range(n)]
        for cp in mine:
            cp.start()
        copies = [copy(a, k) for a in range(n) for k in range(len(flips))]
        for send, _ in copies:
            send.start()
        for send, recv in copies:
            recv.wait_recv()
            send.wait_send()
        for cp in mine:
            cp.wait()

    return pl.pallas_call(
        body, name=name,
        in_specs=[ANY] * n, out_specs=[ANY] * n,
        out_shape=[jax.ShapeDtypeStruct(s.shape, s.dtype) for s in slabs],
        scratch_shapes=[pltpu.SemaphoreType.DMA((n, 7)), pltpu.SemaphoreType.DMA((n, 7)), pltpu.SemaphoreType.DMA((n,))],
    )(*slabs)


def _row_tile(rows):
    for cand in range(256, 7, -8):
        if rows % cand == 0:
            return cand
    return rows


def _window(w):
    wp = max(-(-((w * r) % LANES + w) // LANES) for r in range(N_DEV)) * LANES
    assert all((w * r) // LANES * LANES + wp <= N_DEV * w for r in range(N_DEV))
    return wp


def _join_cols(slabs, name):
    _, R, w = slabs.shape
    tr = _row_tile(R)
    wp = _window(w)

    def body(g_ref, o_ref, pad_ref):
        if w % LANES == 0:
            for r in range(N_DEV):
                o_ref[:, w * r:w * (r + 1)] = g_ref[r]
            return
        o_ref[...] = jnp.zeros_like(o_ref)
        pad_ref[...] = jnp.zeros_like(pad_ref)
        for r in range(N_DEV):
            q, s = divmod(w * r, LANES)
            pad_ref[:, :w] = g_ref[r]
            y = pad_ref[...]
            if s:
                y = pltpu.roll(y, s, axis=1)
            o_ref[:, LANES * q:LANES * q + wp] += y

    return pl.pallas_call(
        body, name=name, grid=(R // tr,),
        in_specs=[pl.BlockSpec((N_DEV, tr, w), lambda i: (0, i, 0))], out_specs=pl.BlockSpec((tr, N_DEV * w), lambda i: (i, 0)),
        out_shape=jax.ShapeDtypeStruct((R, N_DEV * w), slabs.dtype), scratch_shapes=[pltpu.VMEM((tr, wp), slabs.dtype)],
        compiler_params=_params(("parallel",)),
    )(slabs)


def _split_cols(pieces, name):
    R = pieces[0].shape[0]
    widths = [p.shape[1] for p in pieces]
    total = sum(widths)
    w = total // N_DEV
    tr = _row_tile(R)
    wp = _window(w)
    offs = np.cumsum([0] + widths)
    dtype = pieces[0].dtype

    def body(*refs):
        ins, (o_ref, full_ref) = refs[:len(pieces)], refs[len(pieces):]
        for p_ref, a, b in zip(ins, offs[:-1], offs[1:]):
            full_ref[:, a:b] = p_ref[...].astype(dtype)
        for r in range(N_DEV):
            q, s = divmod(w * r, LANES)
            y = full_ref[:, LANES * q:LANES * q + wp]
            if s:
                y = pltpu.roll(y, wp - s, axis=1)
            o_ref[r] = y[:, :w]

    return pl.pallas_call(
        body, name=name, grid=(R // tr,),
        in_specs=[pl.BlockSpec((tr, n), lambda i: (i, 0)) for n in widths], out_specs=pl.BlockSpec((N_DEV, tr, w), lambda i: (0, i, 0)),
        out_shape=jax.ShapeDtypeStruct((N_DEV, R, w), dtype), scratch_shapes=[pltpu.VMEM((tr, total), dtype)],
        compiler_params=_params(("parallel",)),
    )(*pieces)


def _adamw(g, w, m, v):
    m_new = ADAM_B1 * m + (1.0 - ADAM_B1) * g
    v_new = ADAM_B2 * v + (1.0 - ADAM_B2) * jnp.square(g)
    m_hat = m_new / (1.0 - ADAM_B1 ** ADAM_STEP)
    v_hat = v_new / (1.0 - ADAM_B2 ** ADAM_STEP)
    return -ADAM_LR * (m_hat / (jnp.sqrt(v_hat) + ADAM_EPS) + ADAM_WD * w), m_new, v_new


def _sum_parts(p_ref):
    g = p_ref[0].astype(F32)
    for d in range(1, N_DEV):
        g = g + p_ref[d].astype(F32)
    return g


def _reduce_adamw(parts, w, m, v, name):
    R, C = w.shape
    tr = _row_tile(R)

    def body(p_ref, w_ref, m_ref, v_ref, g_ref, d_ref, mo_ref, vo_ref):
        g = _sum_parts(p_ref)
        g_ref[...] = g
        d_ref[...], mo_ref[...], vo_ref[...] = _adamw(g, w_ref[...], m_ref[...], v_ref[...])

    row = pl.BlockSpec((tr, C), lambda i: (i, 0))
    return pl.pallas_call(
        body, name=name, grid=(R // tr,),
        in_specs=[pl.BlockSpec((N_DEV, tr, C), lambda i: (0, i, 0)), row, row, row],
        out_specs=[row] * 4, out_shape=[jax.ShapeDtypeStruct((R, C), F32)] * 4,
        compiler_params=_params(("parallel",)),
    )(parts, w, m, v)


def _reduce_adamw_small(parts, ws, ms, vs):
    sizes = [a.shape[1] for a in ws]
    k = len(sizes)
    offs = np.cumsum([0] + [-(-n // LANES) * LANES for n in sizes])

    def body(*refs):
        p_ref, w_refs, m_refs, v_refs = refs[0], refs[1:1 + k], refs[1 + k:1 + 2 * k], refs[1 + 2 * k:1 + 3 * k]
        outs, loss_ref = refs[1 + 3 * k:-1], refs[-1]
        g_all = _sum_parts(p_ref)
        for j, n in enumerate(sizes):
            g = g_all[:, offs[j]:offs[j] + LANES * (-(-n // LANES))][:, :n]
            outs[4 * j][...] = g
            outs[4 * j + 1][...], outs[4 * j + 2][...], outs[4 * j + 3][...] = _adamw(g, w_refs[j][...], m_refs[j][...], v_refs[j][...])
        loss_ref[...] = g_all[:, offs[k]:offs[k] + LANES]

    vm = pl.BlockSpec(memory_space=pltpu.VMEM)
    out_shape = [jax.ShapeDtypeStruct((1, n), F32) for n in sizes for _ in range(4)] + [jax.ShapeDtypeStruct((1, LANES), F32)]
    res = pl.pallas_call(
        body, name="reduce_adamw_replicated", in_specs=[vm] * (1 + 3 * k), out_specs=[vm] * len(out_shape), out_shape=out_shape,
        compiler_params=_params(),
    )(parts, *ws, *ms, *vs)
    return [res[4 * j:4 * j + 4] for j in range(k)], res[-1]


COL_SHARDED = ("w_in", "w_branch_a", "w_branch_b", "w_up", "conv_w")
ROW_SHARDED = ("w_out", "w_down")
SMALL = ("b_gate", "sinks", "ln1_g", "ln1_b", "conv_b", "ln2_g", "ln2_b")
ORDER = ("w_in", "b_gate", "sinks", "w_branch_a", "w_branch_b", "w_out", "ln1_g", "ln1_b", "w_up", "conv_w", "conv_b", "w_down", "ln2_g", "ln2_b")


def _pad_lanes(a):
    pad = (-a.shape[-1]) % LANES
    return a if pad == 0 else jnp.pad(a, ((0, 0), (0, pad)))


def kernel(x, positions, w_in, b_gate, sinks, w_branch_a, w_branch_b, w_out, ln1_g, ln1_b, w_up, conv_w, conv_b, w_down, ln2_g, ln2_b, loss_target, m_w_in, m_b_gate, m_sinks, m_w_branch_a, m_w_branch_b, m_w_out, m_ln1_g, m_ln1_b, m_w_up, m_conv_w, m_conv_b, m_w_down, m_ln2_g, m_ln2_b, v_w_in, v_b_gate, v_sinks, v_w_branch_a, v_w_branch_b, v_w_out, v_ln1_g, v_ln1_b, v_w_up, v_conv_w, v_conv_b, v_w_down, v_ln2_g, v_ln2_b):
    args = dict(locals())
    sharded = COL_SHARDED + ROW_SHARDED
    w = {n: args[n][0] if n in sharded else args[n] for n in ORDER}
    m = {n: args["m_" + n][0] if n in sharded else args["m_" + n] for n in ORDER}
    v = {n: args["v_" + n][0] if n in sharded else args["v_" + n] for n in ORDER}

    travel = {n: (w[n] if n == "conv_w" else w[n].astype(BF16)) for n in sharded}
    (g_in,) = _all_gather([travel["w_in"]], "all_gather_w_in")
    rest_names = [n for n in sharded if n != "w_in"]
    gathered = dict(zip(rest_names, _all_gather([travel[n] for n in rest_names], "all_gather_rest")))
    gathered["w_in"] = g_in
    full = {n: _join_cols(gathered[n], "join_" + n) for n in COL_SHARDED}
    for n in ROW_SHARDED:
        full[n] = gathered[n].reshape(-1, gathered[n].shape[-1])

    loss, grad_x, grads = _local_step(
        x[0], positions[0], full["w_in"], w["b_gate"], w["sinks"][0], full["w_branch_a"], full["w_branch_b"], full["w_out"],
        w["ln1_g"], w["ln1_b"], full["w_up"], full["conv_w"], w["conv_b"], full["w_down"], w["ln2_g"], w["ln2_b"], loss_target[0])

    send = {n: _split_cols(grads[n] if isinstance(grads[n], tuple) else (grads[n],), "split_d" + n) for n in COL_SHARDED}
    for n in ROW_SHARDED:
        send[n] = grads[n].reshape((N_DEV, -1, grads[n].shape[-1]))
    small_pack = jnp.concatenate(
        [_pad_lanes(p) for n in SMALL for p in (grads[n] if isinstance(grads[n], tuple) else (grads[n],))] + [loss], axis=1)
    recv = dict(zip(sharded, _all_to_all([send[n] for n in sharded], "all_to_all_grads")))
    (small_parts,) = _all_gather([small_pack], "all_gather_small_grads")

    res = {n: _reduce_adamw(recv[n], w[n], m[n], v[n], "reduce_adamw_" + n) for n in sharded}
    small_res, loss_sum = _reduce_adamw_small(small_parts, [w[n] for n in SMALL], [m[n] for n in SMALL], [v[n] for n in SMALL])
    res.update(zip(SMALL, small_res))
    out = [loss_sum[0, 0], grad_x[None]]
    for k in range(4):
        out += [res[n][k][None] if n in sharded else res[n][k] for n in ORDER]
    return tuple(out)
```

```python
import functools

import jax
import jax.numpy as jnp
import numpy as np
from jax import lax
from jax.experimental import pallas as pl
from jax.experimental.pallas import tpu as pltpu

D_MODEL = 1024
HEAD_DIM = 64
SWA_Q_HEADS = 8
SWA_KV_HEADS = 2
SB_HEADS = 8
WINDOW = 128
ROPE_THETA = 10000.0
D_FF = 2816
LN_EPS = 1e-5
DEPTH = 1
ALPHA = (2.0 * DEPTH) ** 0.25
SWA_Q_WIDTH = SWA_Q_HEADS * HEAD_DIM
SWA_KV_WIDTH = SWA_KV_HEADS * HEAD_DIM
SB_WIDTH = SB_HEADS * HEAD_DIM
GATE_WIDTH = 2 * D_MODEL
IN_WIDTHS = (SWA_Q_WIDTH, SWA_KV_WIDTH, SWA_KV_WIDTH, SB_WIDTH, SB_WIDTH, SB_WIDTH, GATE_WIDTH)
IN_TOTAL = sum(IN_WIDTHS)
ATTN_SCALE = HEAD_DIM ** -0.5

ADAM_LR = 0.001
ADAM_B1 = 0.9
ADAM_B2 = 0.999
ADAM_EPS = 1e-08
ADAM_WD = 0.01
ADAM_STEP = 10

N_DEV = 8
LANES = 128
SB_BLOCK = 256
ADAM_ROWS = 512
VMEM_LIMIT = 56 * 1024 * 1024

F32 = jnp.float32
BF16 = jnp.bfloat16
MESH = pl.DeviceIdType.MESH


def _params(sem=None):
    return pltpu.CompilerParams(dimension_semantics=sem, vmem_limit_bytes=VMEM_LIMIT)


def _dot(a, b):
    return jnp.dot(a, b, preferred_element_type=F32)


def _dot_nt(a, b):
    return lax.dot_general(a, b, (((1,), (1,)), ((), ())), preferred_element_type=F32)


def _dot_tn(a, b):
    return lax.dot_general(a, b, (((0,), (0,)), ((), ())), preferred_element_type=F32)


def _split_bf16(v):
    hi = v.astype(BF16)
    lo = (v - hi.astype(F32)).astype(BF16)
    return hi, lo


def _matmul(a, b, *, kind, out_shape, grid, a_spec, b_spec, out_spec, name, add=None, add_spec=None, add_scale=1.0):
    dot = {"nn": _dot, "nt": _dot_nt, "tn": _dot_tn}[kind]

    def body(*refs):
        if add is None:
            a_ref, b_ref, o_ref = refs
        else:
            a_ref, b_ref, add_ref, o_ref = refs
        r = dot(a_ref[...].astype(BF16), b_ref[...].astype(BF16))
        if add is not None:
            r = r + add_scale * add_ref[...]
        o_ref[...] = r.astype(o_ref.dtype)

    ins = [a, b] + ([] if add is None else [add])
    specs = [a_spec, b_spec] + ([] if add is None else [add_spec])
    return pl.pallas_call(
        body, name=name, grid=grid, in_specs=specs, out_specs=out_spec, out_shape=out_shape,
        compiler_params=_params(("parallel",) * len(grid)),
    )(*ins)


def _rope_tables(pos_col, inv_freq_lanes):
    T = pos_col.shape[0]
    tm = min(512, T)

    def body(pos_ref, f_ref, cos_ref, sin_ref):
        ang = pos_ref[...].astype(F32) * f_ref[...]
        cos_ref[...] = jnp.cos(ang)
        sin_ref[...] = jnp.sin(ang)

    return pl.pallas_call(
        body, name="rope_tables", grid=(T // tm,),
        in_specs=[pl.BlockSpec((tm, 1), lambda i: (i, 0)), pl.BlockSpec((1, LANES), lambda i: (0, 0))],
        out_specs=[pl.BlockSpec((tm, LANES), lambda i: (i, 0))] * 2,
        out_shape=[jax.ShapeDtypeStruct((T, LANES), F32)] * 2,
        compiler_params=_params(("parallel",)),
    )(pos_col, inv_freq_lanes)


def _lane_iota(shape):
    return lax.broadcasted_iota(jnp.int32, shape, len(shape) - 1)


def _rot_half(t):
    first = (_lane_iota(t.shape) % HEAD_DIM) < (HEAD_DIM // 2)
    return jnp.where(first, -pltpu.roll(t, LANES - HEAD_DIM // 2, axis=1), pltpu.roll(t, HEAD_DIM // 2, axis=1))


def _rope(t, cos, sin):
    return t * cos + _rot_half(t) * sin


def _rope_transpose(d, cos, sin):
    return d * cos - _rot_half(d * sin)


_IN_DTYPES = (F32, F32, BF16, BF16, BF16, BF16, F32)


def _in_proj(x, w_in_b):
    T = x.shape[0]
    tm = min(256, T)
    offs = np.cumsum((0,) + IN_WIDTHS)

    def body(x_ref, w_ref, xb_ref, *outs):
        xb = x_ref[...].astype(BF16)
        xb_ref[...] = xb
        for o_ref, a, b in zip(outs, offs[:-1], offs[1:]):
            o_ref[...] = _dot(xb, w_ref[:, a:b]).astype(o_ref.dtype)

    row = lambda n: pl.BlockSpec((tm, n), lambda i: (i, 0))
    return pl.pallas_call(
        body, name="in_proj", grid=(T // tm,),
        in_specs=[row(D_MODEL), pl.BlockSpec((D_MODEL, IN_TOTAL), lambda i: (0, 0))],
        out_specs=[row(D_MODEL)] + [row(n) for n in IN_WIDTHS],
        out_shape=[jax.ShapeDtypeStruct((T, D_MODEL), BF16)] + [jax.ShapeDtypeStruct((T, n), dt) for n, dt in zip(IN_WIDTHS, _IN_DTYPES)],
        compiler_params=_params(("parallel",)),
    )(x, w_in_b)


def _swa_specs(T):
    blk = WINDOW
    cur = lambda n: pl.BlockSpec((blk, n), lambda i: (i, 0))
    prev = lambda n: pl.BlockSpec((blk, n), lambda i: (jnp.maximum(i - 1, 0), 0))
    return blk, cur, prev


def _swa_window(i, kp, kc, vp, vc, cp, cc, sp, sc):
    kwin = jnp.concatenate([_rope(kp, cp, sp), _rope(kc, cc, sc)], axis=0)
    vwin = jnp.concatenate([vp, vc], axis=0)
    lane = _lane_iota(kwin.shape)
    low = lane < HEAD_DIM
    ks, vs = [], []
    for g in range(SWA_KV_HEADS):
        k0 = jnp.where(low, kwin if g == 0 else pltpu.roll(kwin, HEAD_DIM, axis=1), 0.0)
        v0 = jnp.where(low, vwin if g == 0 else pltpu.roll(vwin, HEAD_DIM, axis=1), 0.0)
        ks.append((k0, pltpu.roll(k0, HEAD_DIM, axis=1)))
        vs.append((v0, pltpu.roll(v0, HEAD_DIM, axis=1)))
    blk = WINDOW
    r = lax.broadcasted_iota(jnp.int32, (blk, 2 * blk), 0)
    c = lax.broadcasted_iota(jnp.int32, (blk, 2 * blk), 1)
    rel = blk + r - c
    valid = (rel >= 0) & (rel < WINDOW) & ((c >= blk) | (i > 0))
    return ks, vs, valid


def _swa_probs(qh, kk, valid, sink):
    s = _dot_nt(qh, kk) * ATTN_SCALE
    s = jnp.where(valid, s, -1e30)
    m = jnp.maximum(jnp.max(s, axis=1, keepdims=True), sink)
    p = jnp.where(valid, jnp.exp(s - m), 0.0)
    es = jnp.exp(sink - m)
    den = jnp.sum(p, axis=1, keepdims=True) + es
    return p / den, es / den


def _swa_fwd(qa, ka, va, cos, sin, sinks):
    T = qa.shape[0]
    blk, cur, prev = _swa_specs(T)

    def body(sink_ref, q_ref, kp_ref, kc_ref, vp_ref, vc_ref, cp_ref, cc_ref, sp_ref, sc_ref, o_ref):
        i = pl.program_id(0)
        cc, sc = cc_ref[...], sc_ref[...]
        ks, vs, valid = _swa_window(i, kp_ref[...], kc_ref[...], vp_ref[...].astype(F32), vc_ref[...].astype(F32),
                                    cp_ref[...], cc, sp_ref[...], sc)
        lane = _lane_iota((blk, LANES))
        for pp in range(SWA_Q_HEADS // 2):
            g = pp // (SWA_Q_HEADS // SWA_KV_HEADS // 2)
            qp = _rope(q_ref[:, pp * LANES:(pp + 1) * LANES], cc, sc)
            out = jnp.zeros((blk, LANES), F32)
            for hh in range(2):
                half = (lane >= hh * HEAD_DIM) & (lane < (hh + 1) * HEAD_DIM)
                qh = jnp.where(half, qp, 0.0).astype(BF16)
                probs, _ = _swa_probs(qh, ks[g][hh].astype(BF16), valid, sink_ref[2 * pp + hh])
                out = out + _dot(probs.astype(BF16), vs[g][hh].astype(BF16))
            o_ref[:, pp * LANES:(pp + 1) * LANES] = out.astype(o_ref.dtype)

    return pl.pallas_call(
        body, name="swa_fwd", grid=(T // blk,),
        in_specs=[pl.BlockSpec(memory_space=pltpu.SMEM), cur(SWA_Q_WIDTH), prev(LANES), cur(LANES), prev(LANES), cur(LANES),
                  prev(LANES), cur(LANES), prev(LANES), cur(LANES)],
        out_specs=cur(SWA_Q_WIDTH),
        out_shape=jax.ShapeDtypeStruct((T, SWA_Q_WIDTH), BF16),
        compiler_params=_params(("parallel",)),
    )(sinks, qa, ka, ka, va, va, cos, cos, sin, sin)


def _swa_bwd(qa, ka, va, cos, sin, sinks, dya):
    T = qa.shape[0]
    blk, cur, prev = _swa_specs(T)
    full = lambda n: pl.BlockSpec((T, n), lambda i: (0, 0))

    def body(sink_ref, q_ref, kp_ref, kc_ref, vp_ref, vc_ref, cp_ref, cc_ref, sp_ref, sc_ref, do_ref,
             dq_ref, dk_ref, dv_ref, dsink_ref):
        i = pl.program_id(0)

        @pl.when(i == 0)
        def _():
            dk_ref[...] = jnp.zeros_like(dk_ref)
            dv_ref[...] = jnp.zeros_like(dv_ref)
            dsink_ref[...] = jnp.zeros_like(dsink_ref)

        cp, cc, sp, sc = cp_ref[...], cc_ref[...], sp_ref[...], sc_ref[...]
        ks, vs, valid = _swa_window(i, kp_ref[...], kc_ref[...], vp_ref[...].astype(F32), vc_ref[...].astype(F32), cp, cc, sp, sc)
        lane = _lane_iota((blk, LANES))
        lane1 = _lane_iota((1, LANES))
        dkw = jnp.zeros((2 * blk, LANES), F32)
        dvw = jnp.zeros((2 * blk, LANES), F32)
        dsink = jnp.zeros((1, LANES), F32)
        for pp in range(SWA_Q_HEADS // 2):
            g = pp // (SWA_Q_HEADS // SWA_KV_HEADS // 2)
            qp = _rope(q_ref[:, pp * LANES:(pp + 1) * LANES], cc, sc)
            dop = do_ref[:, pp * LANES:(pp + 1) * LANES]
            dqp = jnp.zeros((blk, LANES), F32)
            for hh in range(2):
                half = (lane >= hh * HEAD_DIM) & (lane < (hh + 1) * HEAD_DIM)
                qh = jnp.where(half, qp, 0.0).astype(BF16)
                doh = jnp.where(half, dop, 0.0).astype(BF16)
                kk = ks[g][hh].astype(BF16)
                vv = vs[g][hh].astype(BF16)
                probs, psink = _swa_probs(qh, kk, valid, sink_ref[2 * pp + hh])
                dp = _dot_nt(doh, vv)
                dsum = jnp.sum(probs * dp, axis=1, keepdims=True)
                ds = (probs * (dp - dsum) * ATTN_SCALE).astype(BF16)
                dsink = dsink + jnp.where(lane1 == 2 * pp + hh, -jnp.sum(psink * dsum), 0.0)
                dqp = dqp + _dot(ds, kk)
                dk_h = _dot_tn(ds, qh)
                dv_h = _dot_tn(probs.astype(BF16), doh)
                if hh != g:
                    dk_h = pltpu.roll(dk_h, HEAD_DIM, axis=1)
                    dv_h = pltpu.roll(dv_h, HEAD_DIM, axis=1)
                dkw = dkw + dk_h
                dvw = dvw + dv_h
            dq_ref[:, pp * LANES:(pp + 1) * LANES] = _rope_transpose(dqp, cc, sc).astype(dq_ref.dtype)
        dsink_ref[...] += dsink
        ip = jnp.maximum(i - 1, 0)
        rows_p = pl.ds(pl.multiple_of(ip * blk, blk), blk)
        rows_c = pl.ds(pl.multiple_of(i * blk, blk), blk)
        dk_ref[rows_p, :] += _rope_transpose(dkw[:blk], cp, sp)
        dv_ref[rows_p, :] += dvw[:blk]
        dk_ref[rows_c, :] += _rope_transpose(dkw[blk:], cc, sc)
        dv_ref[rows_c, :] += dvw[blk:]

    return pl.pallas_call(
        body, name="swa_bwd", grid=(T // blk,),
        in_specs=[pl.BlockSpec(memory_space=pltpu.SMEM), cur(SWA_Q_WIDTH), prev(LANES), cur(LANES), prev(LANES), cur(LANES),
                  prev(LANES), cur(LANES), prev(LANES), cur(LANES), cur(SWA_Q_WIDTH)],
        out_specs=[cur(SWA_Q_WIDTH), full(LANES), full(LANES), pl.BlockSpec((1, LANES), lambda i: (0, 0))],
        out_shape=[jax.ShapeDtypeStruct((T, SWA_Q_WIDTH), BF16), jax.ShapeDtypeStruct((T, LANES), F32),
                   jax.ShapeDtypeStruct((T, LANES), F32), jax.ShapeDtypeStruct((1, LANES), F32)],
        compiler_params=_params(("arbitrary",)),
    )(sinks, qa, ka, ka, va, va, cos, cos, sin, sin, dya)


def _sb_scores(qm, k, valid):
    z = _dot_nt(qm, k)
    sp = jnp.maximum(z, 0.0) + jnp.log(1.0 + jnp.exp(-jnp.abs(z)))
    log_one_minus = -sp
    if valid is not None:
        log_one_minus = jnp.where(valid, log_one_minus, 0.0)
    return log_one_minus, z - sp


def _tri2(B, cmp):
    r = lax.broadcasted_iota(jnp.int32, (2 * B, B), 0) % B
    c = lax.broadcasted_iota(jnp.int32, (2 * B, B), 1)
    return cmp(r, c).astype(BF16)


def _tri_sum(v, tri2):
    hi, lo = _split_bf16(v)
    return _dot(jnp.concatenate([hi, lo], axis=1), tri2)


def _head_masks(x):
    low = _lane_iota(x.shape) < HEAD_DIM
    zero = jnp.zeros((), x.dtype)
    return jnp.where(low, x, zero), jnp.where(low, zero, x)


def _strictly_below(B):
    r = lax.broadcasted_iota(jnp.int32, (B, B), 0)
    c = lax.broadcasted_iota(jnp.int32, (B, B), 1)
    return c < r


def _sb_fwd(qb, kb, vb):
    T = qb.shape[0]
    B = min(SB_BLOCK, T)
    assert T // B <= HEAD_DIM
    n_pairs = SB_HEADS // 2

    def body(q_ref, k_ref, v_ref, o_ref, carry_ref):
        i = pl.program_id(1)
        qms = _head_masks(q_ref[...] * ATTN_SCALE)
        lane = _lane_iota((1, LANES))
        upper2 = _tri2(B, lambda r, c: r > c)

        def block(j, state, valid):
            cs, acc, cm = state
            rows = pl.ds(pl.multiple_of(j * B, B), B)
            k = k_ref[rows, :]
            vms = _head_masks(v_ref[rows, :])
            probs, new_cs = [], []
            for hh in range(2):
                lom, lb = _sb_scores(qms[hh], k, valid)
                cm = jnp.where(lane == hh * HEAD_DIM + j, cs[hh], cm)
                a = jnp.exp(lb + (cs[hh] + _tri_sum(lom, upper2)))
                if valid is not None:
                    a = jnp.where(valid, a, 0.0)
                probs.append(a.astype(BF16))
                new_cs.append(cs[hh] + jnp.sum(lom, axis=1, keepdims=True))
            acc = acc + _dot(jnp.concatenate(probs, axis=1), jnp.concatenate(vms, axis=0))
            return tuple(new_cs), acc, cm

        zero = jnp.zeros((B, 1), F32)
        state = block(i, ((zero, zero), jnp.zeros((B, LANES), F32), jnp.zeros((B, LANES), F32)), _strictly_below(B))
        _, acc, cm = lax.fori_loop(0, i, lambda jj, s: block(i - 1 - jj, s, None), state)
        o_ref[...] = acc.astype(o_ref.dtype)
        carry_ref[...] = cm

    return pl.pallas_call(
        body, name="sb_fwd", grid=(n_pairs, T // B),
        in_specs=[pl.BlockSpec((B, LANES), lambda p, i: (i, p)), pl.BlockSpec((T, LANES), lambda p, i: (0, p)),
                  pl.BlockSpec((T, LANES), lambda p, i: (0, p))],
        out_specs=[pl.BlockSpec((B, LANES), lambda p, i: (i, p))] * 2,
        out_shape=[jax.ShapeDtypeStruct((T, SB_WIDTH), BF16), jax.ShapeDtypeStruct((T, SB_WIDTH), F32)],
        compiler_params=_params(("parallel", "parallel")),
    )(qb, kb, vb)


def _sb_bwd(qb, kb, vb, carries, dyb):
    T = qb.shape[0]
    B = min(SB_BLOCK, T)
    n_pairs = SB_HEADS // 2

    def body(q_ref, k_ref, v_ref, carry_ref, do_ref, dq_ref, dk_ref, dv_ref):
        i = pl.program_id(1)

        @pl.when(i == 0)
        def _():
            dk_ref[...] = jnp.zeros_like(dk_ref)
            dv_ref[...] = jnp.zeros_like(dv_ref)

        qms = _head_masks(q_ref[...] * ATTN_SCALE)
        doms = _head_masks(do_ref[...])
        q2, do2 = jnp.concatenate(qms, axis=0), jnp.concatenate(doms, axis=0)
        cm = carry_ref[...]
        lane = _lane_iota((1, LANES))
        upper2 = _tri2(B, lambda r, c: r > c)
        lower2 = _tri2(B, lambda r, c: r < c)

        def block(j, state, valid):
            cgs, dq = state
            rows = pl.ds(pl.multiple_of(j * B, B), B)
            k = k_ref[rows, :]
            v = v_ref[rows, :]
            dzs, probs, new_cgs = [], [], []
            for hh in range(2):
                lom, lb = _sb_scores(qms[hh], k, valid)
                c = jnp.sum(jnp.where(lane == hh * HEAD_DIM + j, cm, 0.0), axis=1, keepdims=True)
                a = jnp.exp(lb + (c + _tri_sum(lom, upper2)))
                if valid is not None:
                    a = jnp.where(valid, a, 0.0)
                g = a * _dot_nt(doms[hh], v)
                gpre = cgs[hh] + _tri_sum(g, lower2)
                dz = g - jnp.exp(lb) * (g + gpre)
                if valid is not None:
                    dz = jnp.where(valid, dz, 0.0)
                dzs.append(dz.astype(BF16))
                probs.append(a.astype(BF16))
                new_cgs.append(cgs[hh] + jnp.sum(g, axis=1, keepdims=True))
            dq = dq + _dot(jnp.concatenate(dzs, axis=1), jnp.concatenate(_head_masks(k), axis=0))
            dk_ref[rows, :] += _dot_tn(jnp.concatenate(dzs, axis=0), q2)
            dv_ref[rows, :] += _dot_tn(jnp.concatenate(probs, axis=0), do2)
            return tuple(new_cgs), dq

        zero = jnp.zeros((B, 1), F32)
        state = lax.fori_loop(0, i, lambda j, s: block(j, s, None), ((zero, zero), jnp.zeros((B, LANES), F32)))
        _, dq = block(i, state, _strictly_below(B))
        dq_ref[...] = (dq * ATTN_SCALE).astype(dq_ref.dtype)

    blk = pl.BlockSpec((B, LANES), lambda p, i: (i, p))
    full = pl.BlockSpec((T, LANES), lambda p, i: (0, p))
    return pl.pallas_call(
        body, name="sb_bwd", grid=(n_pairs, T // B),
        in_specs=[blk, full, full, blk, blk],
        out_specs=[blk, full, full],
        out_shape=[jax.ShapeDtypeStruct((T, SB_WIDTH), BF16), jax.ShapeDtypeStruct((T, SB_WIDTH), F32),
                   jax.ShapeDtypeStruct((T, SB_WIDTH), F32)],
        compiler_params=_params(("parallel", "arbitrary")),
    )(qb, kb, vb, carries, dyb)


def _ln_stats(u):
    mu = jnp.mean(u, axis=-1, keepdims=True)
    xc = u - mu
    var = jnp.mean(xc * xc, axis=-1, keepdims=True)
    rstd = lax.rsqrt(var + LN_EPS)
    return xc * rstd, rstd


def _ln_bwd(dy, xhat, rstd, g):
    dxh = dy * g
    return rstd * (dxh - jnp.mean(dxh, axis=-1, keepdims=True) - xhat * jnp.mean(dxh * xhat, axis=-1, keepdims=True))


def _gates(gl_ref, bg_ref):
    ga = jax.nn.sigmoid(gl_ref[:, :D_MODEL] + bg_ref[:, :D_MODEL])
    gb = jax.nn.sigmoid(gl_ref[:, D_MODEL:] + bg_ref[:, D_MODEL:])
    return ga, gb


def _mix_fwd(ya, yb, gl, x, wa, wb, wo, b_gate, ln1_g, ln1_b):
    T = x.shape[0]
    tm = min(256, T)

    def body(ya_ref, yb_ref, gl_ref, x_ref, wa_ref, wb_ref, wo_ref, bg_ref, g_ref, b_ref, h_ref, u_ref, x1_ref):
        ga, gb = _gates(gl_ref, bg_ref)
        h = (ga * _dot(ya_ref[...], wa_ref[...]) + gb * _dot(yb_ref[...], wb_ref[...])).astype(BF16)
        h_ref[...] = h
        u = ALPHA * x_ref[...] + _dot(h, wo_ref[...])
        u_ref[...] = u
        xhat, _ = _ln_stats(u)
        x1_ref[...] = (xhat * g_ref[...] + b_ref[...]).astype(BF16)

    row = lambda n: pl.BlockSpec((tm, n), lambda i: (i, 0))
    const = lambda r, n: pl.BlockSpec((r, n), lambda i: (0, 0))
    return pl.pallas_call(
        body, name="mix_fwd", grid=(T // tm,),
        in_specs=[row(SWA_Q_WIDTH), row(SB_WIDTH), row(GATE_WIDTH), row(D_MODEL), const(SWA_Q_WIDTH, D_MODEL), const(SB_WIDTH, D_MODEL),
                  const(D_MODEL, D_MODEL), const(1, GATE_WIDTH), const(1, D_MODEL), const(1, D_MODEL)],
        out_specs=[row(D_MODEL)] * 3,
        out_shape=[jax.ShapeDtypeStruct((T, D_MODEL), BF16), jax.ShapeDtypeStruct((T, D_MODEL), F32), jax.ShapeDtypeStruct((T, D_MODEL), BF16)],
        compiler_params=_params(("parallel",)),
    )(ya, yb, gl, x, wa, wb, wo, b_gate, ln1_g, ln1_b)


def _mix_bwd(du1, ya, yb, gl, wa, wb, wo, b_gate):
    T = du1.shape[0]
    tm = min(256, T)

    def body(du_ref, ya_ref, yb_ref, gl_ref, wa_ref, wb_ref, wo_ref, bg_ref, dya_ref, dyb_ref, dgl_ref, dta_ref, dtb_ref, dbg_ref):
        @pl.when(pl.program_id(0) == 0)
        def _():
            dbg_ref[...] = jnp.zeros_like(dbg_ref)

        dh = _dot_nt(du_ref[...].astype(BF16), wo_ref[...])
        ga, gb = _gates(gl_ref, bg_ref)
        for gate, y_ref, w_ref, dy_ref, dt_ref, lo in ((ga, ya_ref, wa_ref, dya_ref, dta_ref, 0), (gb, yb_ref, wb_ref, dyb_ref, dtb_ref, D_MODEL)):
            t = _dot(y_ref[...], w_ref[...])
            dlogit = dh * t * gate * (1.0 - gate)
            dgl_ref[:, lo:lo + D_MODEL] = dlogit.astype(BF16)
            dbg_ref[:, lo:lo + D_MODEL] += jnp.sum(dlogit, axis=0, keepdims=True)
            dt = (dh * gate).astype(BF16)
            dt_ref[...] = dt
            dy_ref[...] = _dot_nt(dt, w_ref[...]).astype(BF16)

    row = lambda n: pl.BlockSpec((tm, n), lambda i: (i, 0))
    const = lambda r, n: pl.BlockSpec((r, n), lambda i: (0, 0))
    sds = lambda n, dt: jax.ShapeDtypeStruct((T, n), dt)
    return pl.pallas_call(
        body, name="mix_bwd", grid=(T // tm,),
        in_specs=[row(D_MODEL), row(SWA_Q_WIDTH), row(SB_WIDTH), row(GATE_WIDTH), const(SWA_Q_WIDTH, D_MODEL), const(SB_WIDTH, D_MODEL),
                  const(D_MODEL, D_MODEL), const(1, GATE_WIDTH)],
        out_specs=[row(SWA_Q_WIDTH), row(SB_WIDTH), row(GATE_WIDTH), row(D_MODEL), row(D_MODEL), const(1, GATE_WIDTH)],
        out_shape=[sds(SWA_Q_WIDTH, BF16), sds(SB_WIDTH, BF16), sds(GATE_WIDTH, BF16), sds(D_MODEL, BF16), sds(D_MODEL, BF16),
                   jax.ShapeDtypeStruct((1, GATE_WIDTH), F32)],
        compiler_params=_params(("arbitrary",)),
    )(du1, ya, yb, gl, wa, wb, wo, b_gate)


CONV_COLS = LANES


def _shift_down(v, k):
    row = lax.broadcasted_iota(jnp.int32, v.shape, 0)
    return jnp.where(row >= k, pltpu.roll(v, k, axis=0), 0.0)


def _shift_up(v, k):
    n = v.shape[0]
    row = lax.broadcasted_iota(jnp.int32, v.shape, 0)
    return jnp.where(row < n - k, pltpu.roll(v, n - k, axis=0), 0.0)


def _conv(pv, w_ref, b_ref):
    return w_ref[0:1, :] * _shift_down(pv, 2) + w_ref[1:2, :] * _shift_down(pv, 1) + w_ref[2:3, :] * pv + b_ref[...]


def _conv_specs(T):
    nb = D_FF // CONV_COLS
    pair = pl.BlockSpec((2, T, CONV_COLS), lambda j: (0, 0, j))
    gate = lambda r: pl.BlockSpec((r, CONV_COLS), lambda j: (0, j))
    up = lambda r: pl.BlockSpec((r, CONV_COLS), lambda j: (0, j + nb))
    return nb, pair, gate, up


def _conv_glu_fwd(p3, conv_w, conv_b):
    T = p3.shape[1]
    nb, pair, gate, up = _conv_specs(T)

    def body(p_ref, wg_ref, wu_ref, bg_ref, bu_ref, s_ref):
        ag = _conv(p_ref[0], wg_ref, bg_ref)
        au = _conv(p_ref[1], wu_ref, bu_ref)
        s_ref[...] = (ag * jax.nn.sigmoid(ag) * au).astype(BF16)

    return pl.pallas_call(
        body, name="conv_glu_fwd", grid=(nb,),
        in_specs=[pair, gate(3), up(3), gate(1), up(1)],
        out_specs=pl.BlockSpec((T, CONV_COLS), lambda j: (0, j)),
        out_shape=jax.ShapeDtypeStruct((T, D_FF), BF16),
        compiler_params=_params(("parallel",)),
    )(p3, conv_w, conv_w, conv_b, conv_b)


def _conv_glu_bwd(p3, ds, conv_w, conv_b):
    T = p3.shape[1]
    nb, pair, gate, up = _conv_specs(T)

    def body(p_ref, ds_ref, wg_ref, wu_ref, bg_ref, bu_ref, dp_ref, dwg_ref, dwu_ref, dbg_ref, dbu_ref):
        pg, pu = p_ref[0], p_ref[1]
        ag = _conv(pg, wg_ref, bg_ref)
        au = _conv(pu, wu_ref, bu_ref)
        sg = jax.nn.sigmoid(ag)
        d = ds_ref[...]
        dau = d * ag * sg
        dag = d * au * (sg * (1.0 + ag * (1.0 - sg)))
        for half, (da, pv, w_ref, dw_ref, db_ref) in enumerate(((dag, pg, wg_ref, dwg_ref, dbg_ref), (dau, pu, wu_ref, dwu_ref, dbu_ref))):
            db_ref[...] = jnp.sum(da, axis=0, keepdims=True)
            dw_ref[0:1, :] = jnp.sum(da * _shift_down(pv, 2), axis=0, keepdims=True)
            dw_ref[1:2, :] = jnp.sum(da * _shift_down(pv, 1), axis=0, keepdims=True)
            dw_ref[2:3, :] = jnp.sum(da * pv, axis=0, keepdims=True)
            dp = w_ref[2:3, :] * da + w_ref[1:2, :] * _shift_up(da, 1) + w_ref[0:1, :] * _shift_up(da, 2)
            dp_ref[half] = dp.astype(BF16)

    col = lambda r: pl.BlockSpec((r, CONV_COLS), lambda j: (0, j))
    return pl.pallas_call(
        body, name="conv_glu_bwd", grid=(nb,),
        in_specs=[pair, col(T), gate(3), up(3), gate(1), up(1)],
        out_specs=[pair, col(3), col(3), col(1), col(1)],
        out_shape=[jax.ShapeDtypeStruct((2, T, D_FF), BF16), jax.ShapeDtypeStruct((3, D_FF), F32), jax.ShapeDtypeStruct((3, D_FF), F32),
                   jax.ShapeDtypeStruct((1, D_FF), F32), jax.ShapeDtypeStruct((1, D_FF), F32)],
        compiler_params=_params(("parallel",)),
    )(p3, ds, conv_w, conv_w, conv_b, conv_b)


def _ffn_down_loss(s, w_down, u1, ln1_g, ln1_b, ln2_g, ln2_b, target):
    T = u1.shape[0]
    tm = min(256, T)

    def body(s_ref, w_ref, u1_ref, g1_ref, b1_ref, g2_ref, b2_ref, t_ref, du_ref, dub_ref, dg_ref, db_ref, loss_ref):
        @pl.when(pl.program_id(0) == 0)
        def _():
            dg_ref[...] = jnp.zeros_like(dg_ref)
            db_ref[...] = jnp.zeros_like(db_ref)
            loss_ref[...] = jnp.zeros_like(loss_ref)

        xh1, _ = _ln_stats(u1_ref[...])
        x1 = xh1 * g1_ref[...] + b1_ref[...]
        u2 = ALPHA * x1 + _dot(s_ref[...], w_ref[...])
        xh2, rstd2 = _ln_stats(u2)
        err = xh2 * g2_ref[...] + b2_ref[...] - t_ref[...]
        per_token = jnp.mean(err * err, axis=-1, keepdims=True)
        loss_ref[...] += 0.5 * jnp.sum(per_token, axis=0, keepdims=True)
        dy = err * (1.0 / D_MODEL)
        dg_ref[...] += jnp.sum(dy * xh2, axis=0, keepdims=True)
        db_ref[...] += jnp.sum(dy, axis=0, keepdims=True)
        du2 = _ln_bwd(dy, xh2, rstd2, g2_ref[...])
        du_ref[...] = du2
        dub_ref[...] = du2.astype(BF16)

    row = lambda n: pl.BlockSpec((tm, n), lambda i: (i, 0))
    const = lambda r, n: pl.BlockSpec((r, n), lambda i: (0, 0))
    vec = const(1, D_MODEL)
    return pl.pallas_call(
        body, name="ffn_down_loss", grid=(T // tm,),
        in_specs=[row(D_FF), const(D_FF, D_MODEL), row(D_MODEL), vec, vec, vec, vec, row(D_MODEL)],
        out_specs=[row(D_MODEL), row(D_MODEL), vec, vec, const(1, LANES)],
        out_shape=[jax.ShapeDtypeStruct((T, D_MODEL), F32), jax.ShapeDtypeStruct((T, D_MODEL), BF16), jax.ShapeDtypeStruct((1, D_MODEL), F32),
                   jax.ShapeDtypeStruct((1, D_MODEL), F32), jax.ShapeDtypeStruct((1, LANES), F32)],
        compiler_params=_params(("arbitrary",)),
    )(s, w_down, u1, ln1_g, ln1_b, ln2_g, ln2_b, target)


def _ffn_up_bwd_ln1(dp3, w_up, du2, u1, ln1_g):
    T = u1.shape[0]
    tm = min(256, T)

    def body(dp_ref, w_ref, du2_ref, u1_ref, g_ref, du_ref, dub_ref, dg_ref, db_ref):
        @pl.when(pl.program_id(0) == 0)
        def _():
            dg_ref[...] = jnp.zeros_like(dg_ref)
            db_ref[...] = jnp.zeros_like(db_ref)

        dx1 = _dot_nt(dp_ref[0], w_ref[:, :D_FF]) + _dot_nt(dp_ref[1], w_ref[:, D_FF:]) + ALPHA * du2_ref[...]
        xh, rstd = _ln_stats(u1_ref[...])
        dg_ref[...] += jnp.sum(dx1 * xh, axis=0, keepdims=True)
        db_ref[...] += jnp.sum(dx1, axis=0, keepdims=True)
        du1 = _ln_bwd(dx1, xh, rstd, g_ref[...])
        du_ref[...] = du1
        dub_ref[...] = du1.astype(BF16)

    row = lambda n: pl.BlockSpec((tm, n), lambda i: (i, 0))
    const = lambda r, n: pl.BlockSpec((r, n), lambda i: (0, 0))
    vec = const(1, D_MODEL)
    return pl.pallas_call(
        body, name="ffn_up_bwd_ln1", grid=(T // tm,),
        in_specs=[pl.BlockSpec((2, tm, D_FF), lambda i: (0, i, 0)), const(D_MODEL, 2 * D_FF), row(D_MODEL), row(D_MODEL), vec],
        out_specs=[row(D_MODEL), row(D_MODEL), vec, vec],
        out_shape=[jax.ShapeDtypeStruct((T, D_MODEL), F32), jax.ShapeDtypeStruct((T, D_MODEL), BF16), jax.ShapeDtypeStruct((1, D_MODEL), F32),
                   jax.ShapeDtypeStruct((1, D_MODEL), F32)],
        compiler_params=_params(("arbitrary",)),
    )(dp3, w_up, du2, u1, ln1_g)


def _local_step(x, positions, w_in, b_gate, sinks, wa, wb, wo, ln1_g, ln1_b, w_up, conv_w, conv_b, w_down, ln2_g, ln2_b, target):
    T = x.shape[0]
    inv_freq = 1.0 / (ROPE_THETA ** (jnp.arange(0, HEAD_DIM, 2, dtype=F32) / HEAD_DIM))
    cos, sin = _rope_tables(positions.reshape(T, 1), jnp.tile(inv_freq, LANES // (HEAD_DIM // 2)).reshape(1, LANES))

    xb, qa, ka, va, qb, kb, vb, gl = _in_proj(x, w_in)
    ya = _swa_fwd(qa, ka, va, cos, sin, sinks)
    yb, carries = _sb_fwd(qb, kb, vb)
    h, u1, x1 = _mix_fwd(ya, yb, gl, x, wa, wb, wo, b_gate, ln1_g, ln1_b)

    ff_tn = D_FF // 2
    nff = D_FF // ff_tn
    tm = min(512, T)
    p3 = _matmul(x1, w_up, kind="nn", name="ffn_up", grid=(T // tm, 2 * nff),
                 a_spec=pl.BlockSpec((tm, D_MODEL), lambda i, j: (i, 0)), b_spec=pl.BlockSpec((D_MODEL, ff_tn), lambda i, j: (0, j)),
                 out_spec=pl.BlockSpec((None, tm, ff_tn), lambda i, j: (j // nff, i, j % nff)),
                 out_shape=jax.ShapeDtypeStruct((2, T, D_FF), F32))
    s = _conv_glu_fwd(p3, conv_w, conv_b)
    du2, du2b, dln2_g, dln2_b, loss = _ffn_down_loss(s, w_down, u1, ln1_g, ln1_b, ln2_g, ln2_b, target)

    ds = _matmul(du2b, w_down, kind="nt", name="ffn_down_bwd", grid=(T // tm, nff),
                 a_spec=pl.BlockSpec((tm, D_MODEL), lambda i, j: (i, 0)), b_spec=pl.BlockSpec((ff_tn, D_MODEL), lambda i, j: (j, 0)),
                 out_spec=pl.BlockSpec((tm, ff_tn), lambda i, j: (i, j)), out_shape=jax.ShapeDtypeStruct((T, D_FF), F32))
    dp3, dcw_g, dcw_u, dcb_g, dcb_u = _conv_glu_bwd(p3, ds, conv_w, conv_b)
    tk = 256
    dw_down = _matmul(s, du2b, kind="tn", name="dw_down", grid=(D_FF // tk,),
                      a_spec=pl.BlockSpec((T, tk), lambda i: (0, i)), b_spec=pl.BlockSpec((T, D_MODEL), lambda i: (0, 0)),
                      out_spec=pl.BlockSpec((tk, D_MODEL), lambda i: (i, 0)), out_shape=jax.ShapeDtypeStruct((D_FF, D_MODEL), BF16))
    dw_up = _matmul(x1, dp3, kind="tn", name="dw_up", grid=(D_MODEL // 512, 2 * nff),
                    a_spec=pl.BlockSpec((T, 512), lambda i, j: (0, i)), b_spec=pl.BlockSpec((None, T, ff_tn), lambda i, j: (j // nff, 0, j % nff)),
                    out_spec=pl.BlockSpec((512, ff_tn), lambda i, j: (i, j)), out_shape=jax.ShapeDtypeStruct((D_MODEL, 2 * D_FF), BF16))
    du1, du1b, dln1_g, dln1_b = _ffn_up_bwd_ln1(dp3, w_up, du2, u1, ln1_g)
    dya, dyb, dgl, dta, dtb, db_gate = _mix_bwd(du1, ya, yb, gl, wa, wb, wo, b_gate)

    def dw_tn(a, g, name):
        rows, cols = a.shape[1], g.shape[1]
        tn = min(512, cols)
        return _matmul(a, g, kind="tn", name=name, grid=(rows // 512, cols // tn),
                       a_spec=pl.BlockSpec((T, 512), lambda i, j: (0, i)), b_spec=pl.BlockSpec((T, tn), lambda i, j: (0, j)),
                       out_spec=pl.BlockSpec((512, tn), lambda i, j: (i, j)), out_shape=jax.ShapeDtypeStruct((rows, cols), BF16))

    dwa = dw_tn(ya, dta, "dw_branch_a")
    dwb = dw_tn(yb, dtb, "dw_branch_b")
    dwo = dw_tn(h, du1b, "dw_out")

    dqb, dkb, dvb = _sb_bwd(qb, kb, vb, carries, dyb)
    dqa, dka, dva, dsinks = _swa_bwd(qa, ka, va, cos, sin, sinks, dya)
    dproj = (dqa, dka, dva, dqb, dkb, dvb, dgl)
    dw_in = tuple(dw_tn(xb, piece, f"dw_in_{k}") for k, piece in enumerate(dproj))
    grad_x = _grad_x(dproj, w_in, du1)
    grads = dict(
        w_in=dw_in, b_gate=db_gate, sinks=dsinks, w_branch_a=dwa, w_branch_b=dwb, w_out=dwo, ln1_g=dln1_g, ln1_b=dln1_b,
        w_up=dw_up, conv_w=jnp.concatenate([dcw_g, dcw_u], axis=1), conv_b=(dcb_g, dcb_u), w_down=dw_down, ln2_g=dln2_g, ln2_b=dln2_b)
    return loss, grad_x, grads


def _grad_x(dproj, w_in, du1):
    T = du1.shape[0]
    tm = min(256, T)
    offs = np.cumsum((0,) + IN_WIDTHS)

    def body(*refs):
        pieces, (w_ref, du_ref, o_ref) = refs[:len(IN_WIDTHS)], refs[len(IN_WIDTHS):]
        acc = ALPHA * du_ref[...]
        for p_ref, a, b in zip(pieces, offs[:-1], offs[1:]):
            acc = acc + _dot_nt(p_ref[...].astype(BF16), w_ref[:, a:b])
        o_ref[...] = acc

    row = lambda n: pl.BlockSpec((tm, n), lambda i: (i, 0))
    return pl.pallas_call(
        body, name="grad_x", grid=(T // tm,),
        in_specs=[row(n) for n in IN_WIDTHS] + [pl.BlockSpec((D_MODEL, IN_TOTAL), lambda i: (0, 0)), row(D_MODEL)],
        out_specs=row(D_MODEL), out_shape=jax.ShapeDtypeStruct((T, D_MODEL), F32),
        compiler_params=_params(("parallel",)),
    )(*dproj, w_in, du1)


ANY = pl.BlockSpec(memory_space=pl.ANY)


def _all_gather(slabs, name):
    n = len(slabs)

    def body(*refs):
        ins, outs = refs[:n], refs[n:2 * n]
        send_sems, recv_sems, local_sems = refs[2 * n:]
        x, y, c = lax.axis_index("x"), lax.axis_index("y"), lax.axis_index("c")
        me, sibling = (x, y, c), (x, y, 1 - c)
        chips = [(1 - x, y), (x, 1 - y), (1 - x, 1 - y)]

        def slot(pos):
            return 4 * pos[0] + 2 * pos[1] + pos[2]

        def copy(a, k, block, to, from_input=False):
            return pltpu.make_async_remote_copy(
                src_ref=ins[a] if from_input else outs[a].at[slot(block)], dst_ref=outs[a].at[slot(block)],
                send_sem=send_sems.at[a, k], recv_sem=recv_sems.at[a, k], device_id=to, device_id_type=MESH)

        mine = [pltpu.make_async_copy(ins[a], outs[a].at[slot(me)], local_sems.at[a]) for a in range(n)]
        for cp in mine:
            cp.start()
        first = []
        for a in range(n):
            first.append(copy(a, 0, me, sibling, from_input=True))
            first += [copy(a, 1 + j, me, (*chip, c), from_input=True) for j, chip in enumerate(chips)]
        for cp in first:
            cp.start()
        passed = []
        for j, chip in enumerate(chips):
            for a in range(n):
                copy(a, 1 + j, (*chip, c), me).wait_recv()
                fwd = copy(a, 4 + j, (*chip, c), sibling)
                fwd.start()
                passed.append(fwd)
        for a in range(n):
            copy(a, 0, sibling, me).wait_recv()
            for j, chip in enumerate(chips):
                copy(a, 4 + j, (*chip, 1 - c), me).wait_recv()
        for cp in first + passed:
            cp.wait_send()
        for cp in mine:
            cp.wait()

    return pl.pallas_call(
        body, name=name,
        in_specs=[ANY] * n, out_specs=[ANY] * n,
        out_shape=[jax.ShapeDtypeStruct((N_DEV,) + s.shape, s.dtype) for s in slabs],
        scratch_shapes=[pltpu.SemaphoreType.DMA((n, 7)), pltpu.SemaphoreType.DMA((n, 7)), pltpu.SemaphoreType.DMA((n,))],
    )(*slabs)


def _all_to_all(slabs, name):
    n = len(slabs)

    def body(*refs):
        ins, outs = refs[:n], refs[n:2 * n]
        send_sems, recv_sems, local_sems = refs[2 * n:]
        x, y, c = lax.axis_index("x"), lax.axis_index("y"), lax.axis_index("c")
        my_slot = 4 * x + 2 * y + c
        flips = [(fx, fy, fc) for fx in (0, 1) for fy in (0, 1) for fc in (0, 1) if (fx, fy, fc) != (0, 0, 0)]

        def copy(a, k):
            fx, fy, fc = flips[k]
            peer = (x ^ fx, y ^ fy, c ^ fc)
            peer_slot = 4 * peer[0] + 2 * peer[1] + peer[2]
            send = pltpu.make_async_remote_copy(src_ref=ins[a].at[peer_slot], dst_ref=outs[a].at[my_slot], send_sem=send_sems.at[a, k],
                                                recv_sem=recv_sems.at[a, k], device_id=peer, device_id_type=MESH)
            recv = pltpu.make_async_remote_copy(src_ref=ins[a].at[peer_slot], dst_ref=outs[a].at[peer_slot], send_sem=send_sems.at[a, k],
                                                recv_sem=recv_sems.at[a, k], device_id=peer, device_id_type=MESH)
            return send, recv

        mine = [pltpu.make_async_copy(ins[a].at[my_slot], outs[a].at[my_slot], local_sems.at[a]) for a in range(n)]
        for cp in mine:
            cp.start()
        copies = [copy(a, k) for a in range(n) for k in range(len(flips))]
        for send, _ in copies:
            send.start()
        for send, recv in copies:
            recv.wait_recv()
            send.wait_send()
        for cp in mine:
            cp.wait()

    return pl.pallas_call(
        body, name=name,
        in_specs=[ANY] * n, out_specs=[ANY] * n,
        out_shape=[jax.ShapeDtypeStruct(s.shape, s.dtype) for s in slabs],
        scratch_shapes=[pltpu.SemaphoreType.DMA((n, 7)), pltpu.SemaphoreType.DMA((n, 7)), pltpu.SemaphoreType.DMA((n,))],
    )(*slabs)


def _row_tile(rows):
    for cand in range(256, 7, -8):
        if rows % cand == 0:
            return cand
    return rows


def _window(w):
    wp = max(-(-((w * r) % LANES + w) // LANES) for r in range(N_DEV)) * LANES
    assert all((w * r) // LANES * LANES + wp <= N_DEV * w for r in range(N_DEV))
    return wp


def _join_cols(slabs, name):
    _, R, w = slabs.shape
    tr = _row_tile(R)
    wp = _window(w)

    def body(g_ref, o_ref, pad_ref):
        if w % LANES == 0:
            for r in range(N_DEV):
                o_ref[:, w * r:w * (r + 1)] = g_ref[r]
            return
        o_ref[...] = jnp.zeros_like(o_ref)
        pad_ref[...] = jnp.zeros_like(pad_ref)
        for r in range(N_DEV):
            q, s = divmod(w * r, LANES)
            pad_ref[:, :w] = g_ref[r]
            y = pad_ref[...]
            if s:
                y = pltpu.roll(y, s, axis=1)
            o_ref[:, LANES * q:LANES * q + wp] += y

    return pl.pallas_call(
        body, name=name, grid=(R // tr,),
        in_specs=[pl.BlockSpec((N_DEV, tr, w), lambda i: (0, i, 0))], out_specs=pl.BlockSpec((tr, N_DEV * w), lambda i: (i, 0)),
        out_shape=jax.ShapeDtypeStruct((R, N_DEV * w), slabs.dtype), scratch_shapes=[pltpu.VMEM((tr, wp), slabs.dtype)],
        compiler_params=_params(("parallel",)),
    )(slabs)


def _split_cols(pieces, name):
    R = pieces[0].shape[0]
    widths = [p.shape[1] for p in pieces]
    total = sum(widths)
    w = total // N_DEV
    tr = _row_tile(R)
    wp = _window(w)
    offs = np.cumsum([0] + widths)
    dtype = pieces[0].dtype

    def body(*refs):
        ins, (o_ref, full_ref) = refs[:len(pieces)], refs[len(pieces):]
        for p_ref, a, b in zip(ins, offs[:-1], offs[1:]):
            full_ref[:, a:b] = p_ref[...].astype(dtype)
        for r in range(N_DEV):
            q, s = divmod(w * r, LANES)
            y = full_ref[:, LANES * q:LANES * q + wp]
            if s:
                y = pltpu.roll(y, wp - s, axis=1)
            o_ref[r] = y[:, :w]

    return pl.pallas_call(
        body, name=name, grid=(R // tr,),
        in_specs=[pl.BlockSpec((tr, n), lambda i: (i, 0)) for n in widths], out_specs=pl.BlockSpec((N_DEV, tr, w), lambda i: (0, i, 0)),
        out_shape=jax.ShapeDtypeStruct((N_DEV, R, w), dtype), scratch_shapes=[pltpu.VMEM((tr, total), dtype)],
        compiler_params=_params(("parallel",)),
    )(*pieces)


def _adamw(g, w, m, v):
    m_new = ADAM_B1 * m + (1.0 - ADAM_B1) * g
    v_new = ADAM_B2 * v + (1.0 - ADAM_B2) * jnp.square(g)
    m_hat = m_new / (1.0 - ADAM_B1 ** ADAM_STEP)
    v_hat = v_new / (1.0 - ADAM_B2 ** ADAM_STEP)
    return -ADAM_LR * (m_hat / (jnp.sqrt(v_hat) + ADAM_EPS) + ADAM_WD * w), m_new, v_new


def _sum_parts(p_ref):
    g = p_ref[0].astype(F32)
    for d in range(1, N_DEV):
        g = g + p_ref[d].astype(F32)
    return g


def _reduce_adamw(parts, w, m, v, name):
    R, C = w.shape
    tr = _row_tile(R)

    def body(p_ref, w_ref, m_ref, v_ref, g_ref, d_ref, mo_ref, vo_ref):
        g = _sum_parts(p_ref)
        g_ref[...] = g
        d_ref[...], mo_ref[...], vo_ref[...] = _adamw(g, w_ref[...], m_ref[...], v_ref[...])

    row = pl.BlockSpec((tr, C), lambda i: (i, 0))
    return pl.pallas_call(
        body, name=name, grid=(R // tr,),
        in_specs=[pl.BlockSpec((N_DEV, tr, C), lambda i: (0, i, 0)), row, row, row],
        out_specs=[row] * 4, out_shape=[jax.ShapeDtypeStruct((R, C), F32)] * 4,
        compiler_params=_params(("parallel",)),
    )(parts, w, m, v)


def _reduce_adamw_small(parts, ws, ms, vs):
    sizes = [a.shape[1] for a in ws]
    k = len(sizes)
    offs = np.cumsum([0] + [-(-n // LANES) * LANES for n in sizes])

    def body(*refs):
        p_ref, w_refs, m_refs, v_refs = refs[0], refs[1:1 + k], refs[1 + k:1 + 2 * k], refs[1 + 2 * k:1 + 3 * k]
        outs, loss_ref = refs[1 + 3 * k:-1], refs[-1]
        g_all = _sum_parts(p_ref)
        for j, n in enumerate(sizes):
            g = g_all[:, offs[j]:offs[j] + LANES * (-(-n // LANES))][:, :n]
            outs[4 * j][...] = g
            outs[4 * j + 1][...], outs[4 * j + 2][...], outs[4 * j + 3][...] = _adamw(g, w_refs[j][...], m_refs[j][...], v_refs[j][...])
        loss_ref[...] = g_all[:, offs[k]:offs[k] + LANES]

    vm = pl.BlockSpec(memory_space=pltpu.VMEM)
    out_shape = [jax.ShapeDtypeStruct((1, n), F32) for n in sizes for _ in range(4)] + [jax.ShapeDtypeStruct((1, LANES), F32)]
    res = pl.pallas_call(
        body, name="reduce_adamw_replicated", in_specs=[vm] * (1 + 3 * k), out_specs=[vm] * len(out_shape), out_shape=out_shape,
        compiler_params=_params(),
    )(parts, *ws, *ms, *vs)
    return [res[4 * j:4 * j + 4] for j in range(k)], res[-1]


COL_SHARDED = ("w_in", "w_branch_a", "w_branch_b", "w_up", "conv_w")
ROW_SHARDED = ("w_out", "w_down")
SMALL = ("b_gate", "sinks", "ln1_g", "ln1_b", "conv_b", "ln2_g", "ln2_b")
ORDER = ("w_in", "b_gate", "sinks", "w_branch_a", "w_branch_b", "w_out", "ln1_g", "ln1_b", "w_up", "conv_w", "conv_b", "w_down", "ln2_g", "ln2_b")


def _pad_lanes(a):
    pad = (-a.shape[-1]) % LANES
    return a if pad == 0 else jnp.pad(a, ((0, 0), (0, pad)))


def kernel(x, positions, w_in, b_gate, sinks, w_branch_a, w_branch_b, w_out, ln1_g, ln1_b, w_up, conv_w, conv_b, w_down, ln2_g, ln2_b, loss_target, m_w_in, m_b_gate, m_sinks, m_w_branch_a, m_w_branch_b, m_w_out, m_ln1_g, m_ln1_b, m_w_up, m_conv_w, m_conv_b, m_w_down, m_ln2_g, m_ln2_b, v_w_in, v_b_gate, v_sinks, v_w_branch_a, v_w_branch_b, v_w_out, v_ln1_g, v_ln1_b, v_w_up, v_conv_w, v_conv_b, v_w_down, v_ln2_g, v_ln2_b):
    args = dict(locals())
    sharded = COL_SHARDED + ROW_SHARDED
    w = {n: args[n][0] if n in sharded else args[n] for n in ORDER}
    m = {n: args["m_" + n][0] if n in sharded else args["m_" + n] for n in ORDER}
    v = {n: args["v_" + n][0] if n in sharded else args["v_" + n] for n in ORDER}

    travel = {n: (w[n] if n == "conv_w" else w[n].astype(BF16)) for n in sharded}
    (g_in,) = _all_gather([travel["w_in"]], "all_gather_w_in")
    rest_names = [n for n in sharded if n != "w_in"]
    gathered = dict(zip(rest_names, _all_gather([travel[n] for n in rest_names], "all_gather_rest")))
    gathered["w_in"] = g_in
    full = {n: _join_cols(gathered[n], "join_" + n) for n in COL_SHARDED}
    for n in ROW_SHARDED:
        full[n] = gathered[n].reshape(-1, gathered[n].shape[-1])

    loss, grad_x, grads = _local_step(
        x[0], positions[0], full["w_in"], w["b_gate"], w["sinks"][0], full["w_branch_a"], full["w_branch_b"], full["w_out"],
        w["ln1_g"], w["ln1_b"], full["w_up"], full["conv_w"], w["conv_b"], full["w_down"], w["ln2_g"], w["ln2_b"], loss_target[0])

    send = {n: _split_cols(grads[n] if isinstance(grads[n], tuple) else (grads[n],), "split_d" + n) for n in COL_SHARDED}
    for n in ROW_SHARDED:
        send[n] = grads[n].reshape((N_DEV, -1, grads[n].shape[-1]))
    small_pack = jnp.concatenate(
        [_pad_lanes(p) for n in SMALL for p in (grads[n] if isinstance(grads[n], tuple) else (grads[n],))] + [loss], axis=1)
    recv = dict(zip(sharded, _all_to_all([send[n] for n in sharded], "all_to_all_grads")))
    (small_parts,) = _all_gather([small_pack], "all_gather_small_grads")

    res = {n: _reduce_adamw(recv[n], w[n], m[n], v[n], "reduce_adamw_" + n) for n in sharded}
    small_res, loss_sum = _reduce_adamw_small(small_parts, [w[n] for n in SMALL], [m[n] for n in SMALL], [v[n] for n in SMALL])
    res.update(zip(SMALL, small_res))
    out = [loss_sum[0, 0], grad_x[None]]
    for k in range(4):
        out += [res[n][k][None] if n in sharded else res[n][k] for n in ORDER]
    return tuple(out)
```

```python
import functools

import jax
import jax.numpy as jnp
import numpy as np
from jax import lax
from jax.experimental import pallas as pl
from jax.experimental.pallas import tpu as pltpu

D_MODEL = 1024
HEAD_DIM = 64
SWA_Q_HEADS = 8
SWA_KV_HEADS = 2
SB_HEADS = 8
WINDOW = 128
ROPE_THETA = 10000.0
D_FF = 2816
LN_EPS = 1e-5
DEPTH = 1
ALPHA = (2.0 * DEPTH) ** 0.25
SWA_Q_WIDTH = SWA_Q_HEADS * HEAD_DIM
SWA_KV_WIDTH = SWA_KV_HEADS * HEAD_DIM
SB_WIDTH = SB_HEADS * HEAD_DIM
GATE_WIDTH = 2 * D_MODEL
IN_WIDTHS = (SWA_Q_WIDTH, SWA_KV_WIDTH, SWA_KV_WIDTH, SB_WIDTH, SB_WIDTH, SB_WIDTH, GATE_WIDTH)
IN_TOTAL = sum(IN_WIDTHS)
ATTN_SCALE = HEAD_DIM ** -0.5

ADAM_LR = 0.001
ADAM_B1 = 0.9
ADAM_B2 = 0.999
ADAM_EPS = 1e-08
ADAM_WD = 0.01
ADAM_STEP = 10

N_DEV = 8
LANES = 128
SB_BLOCK = 256
ADAM_ROWS = 512
VMEM_LIMIT = 56 * 1024 * 1024

F32 = jnp.float32
BF16 = jnp.bfloat16
MESH = pl.DeviceIdType.MESH


def _params(sem=None):
    return pltpu.CompilerParams(dimension_semantics=sem, vmem_limit_bytes=VMEM_LIMIT)


def _dot(a, b):
    return jnp.dot(a, b, preferred_element_type=F32)


def _dot_nt(a, b):
    return lax.dot_general(a, b, (((1,), (1,)), ((), ())), preferred_element_type=F32)


def _dot_tn(a, b):
    return lax.dot_general(a, b, (((0,), (0,)), ((), ())), preferred_element_type=F32)


def _split_bf16(v):
    hi = v.astype(BF16)
    lo = (v - hi.astype(F32)).astype(BF16)
    return hi, lo


def _matmul(a, b, *, kind, out_shape, grid, a_spec, b_spec, out_spec, name, add=None, add_spec=None, add_scale=1.0):
    dot = {"nn": _dot, "nt": _dot_nt, "tn": _dot_tn}[kind]

    def body(*refs):
        if add is None:
            a_ref, b_ref, o_ref = refs
        else:
            a_ref, b_ref, add_ref, o_ref = refs
        r = dot(a_ref[...].astype(BF16), b_ref[...].astype(BF16))
        if add is not None:
            r = r + add_scale * add_ref[...]
        o_ref[...] = r.astype(o_ref.dtype)

    ins = [a, b] + ([] if add is None else [add])
    specs = [a_spec, b_spec] + ([] if add is None else [add_spec])
    return pl.pallas_call(
        body, name=name, grid=grid, in_specs=specs, out_specs=out_spec, out_shape=out_shape,
        compiler_params=_params(("parallel",) * len(grid)),
    )(*ins)


def _rope_tables(pos_col, inv_freq_lanes):
    T = pos_col.shape[0]
    tm = min(512, T)

    def body(pos_ref, f_ref, cos_ref, sin_ref):
        ang = pos_ref[...].astype(F32) * f_ref[...]
        cos_ref[...] = jnp.cos(ang)
        sin_ref[...] = jnp.sin(ang)

    return pl.pallas_call(
        body, name="rope_tables", grid=(T // tm,),
        in_specs=[pl.BlockSpec((tm, 1), lambda i: (i, 0)), pl.BlockSpec((1, LANES), lambda i: (0, 0))],
        out_specs=[pl.BlockSpec((tm, LANES), lambda i: (i, 0))] * 2,
        out_shape=[jax.ShapeDtypeStruct((T, LANES), F32)] * 2,
        compiler_params=_params(("parallel",)),
    )(pos_col, inv_freq_lanes)


def _lane_iota(shape):
    return lax.broadcasted_iota(jnp.int32, shape, len(shape) - 1)


def _rot_half(t):
    first = (_lane_iota(t.shape) % HEAD_DIM) < (HEAD_DIM // 2)
    return jnp.where(first, -pltpu.roll(t, LANES - HEAD_DIM // 2, axis=1), pltpu.roll(t, HEAD_DIM // 2, axis=1))


def _rope(t, cos, sin):
    return t * cos + _rot_half(t) * sin


def _rope_transpose(d, cos, sin):
    return d * cos - _rot_half(d * sin)


_IN_DTYPES = (F32, F32, BF16, BF16, BF16, BF16, F32)


def _in_proj(x, w_in_b):
    T = x.shape[0]
    tm = min(256, T)
    offs = np.cumsum((0,) + IN_WIDTHS)

    def body(x_ref, w_ref, xb_ref, *outs):
        xb = x_ref[...].astype(BF16)
        xb_ref[...] = xb
        for o_ref, a, b in zip(outs, offs[:-1], offs[1:]):
            o_ref[...] = _dot(xb, w_ref[:, a:b]).astype(o_ref.dtype)

    row = lambda n: pl.BlockSpec((tm, n), lambda i: (i, 0))
    return pl.pallas_call(
        body, name="in_proj", grid=(T // tm,),
        in_specs=[row(D_MODEL), pl.BlockSpec((D_MODEL, IN_TOTAL), lambda i: (0, 0))],
        out_specs=[row(D_MODEL)] + [row(n) for n in IN_WIDTHS],
        out_shape=[jax.ShapeDtypeStruct((T, D_MODEL), BF16)] + [jax.ShapeDtypeStruct((T, n), dt) for n, dt in zip(IN_WIDTHS, _IN_DTYPES)],
        compiler_params=_params(("parallel",)),
    )(x, w_in_b)


def _swa_specs(T):
    blk = WINDOW
    cur = lambda n: pl.BlockSpec((blk, n), lambda i: (i, 0))
    prev = lambda n: pl.BlockSpec((blk, n), lambda i: (jnp.maximum(i - 1, 0), 0))
    return blk, cur, prev


def _swa_window(i, kp, kc, vp, vc, cp, cc, sp, sc):
    kwin = jnp.concatenate([_rope(kp, cp, sp), _rope(kc, cc, sc)], axis=0)
    vwin = jnp.concatenate([vp, vc], axis=0)
    lane = _lane_iota(kwin.shape)
    low = lane < HEAD_DIM
    ks, vs = [], []
    for g in range(SWA_KV_HEADS):
        k0 = jnp.where(low, kwin if g == 0 else pltpu.roll(kwin, HEAD_DIM, axis=1), 0.0)
        v0 = jnp.where(low, vwin if g == 0 else pltpu.roll(vwin, HEAD_DIM, axis=1), 0.0)
        ks.append((k0, pltpu.roll(k0, HEAD_DIM, axis=1)))
        vs.append((v0, pltpu.roll(v0, HEAD_DIM, axis=1)))
    blk = WINDOW
    r = lax.broadcasted_iota(jnp.int32, (blk, 2 * blk), 0)
    c = lax.broadcasted_iota(jnp.int32, (blk, 2 * blk), 1)
    rel = blk + r - c
    valid = (rel >= 0) & (rel < WINDOW) & ((c >= blk) | (i > 0))
    return ks, vs, valid


def _swa_probs(qh, kk, valid, sink):
    s = _dot_nt(qh, kk) * ATTN_SCALE
    s = jnp.where(valid, s, -1e30)
    m = jnp.maximum(jnp.max(s, axis=1, keepdims=True), sink)
    p = jnp.where(valid, jnp.exp(s - m), 0.0)
    es = jnp.exp(sink - m)
    den = jnp.sum(p, axis=1, keepdims=True) + es
    return p / den, es / den


def _swa_fwd(qa, ka, va, cos, sin, sinks):
    T = qa.shape[0]
    blk, cur, prev = _swa_specs(T)

    def body(sink_ref, q_ref, kp_ref, kc_ref, vp_ref, vc_ref, cp_ref, cc_ref, sp_ref, sc_ref, o_ref):
        i = pl.program_id(0)
        cc, sc = cc_ref[...], sc_ref[...]
        ks, vs, valid = _swa_window(i, kp_ref[...], kc_ref[...], vp_ref[...].astype(F32), vc_ref[...].astype(F32),
                                    cp_ref[...], cc, sp_ref[...], sc)
        lane = _lane_iota((blk, LANES))
        for pp in range(SWA_Q_HEADS // 2):
            g = pp // (SWA_Q_HEADS // SWA_KV_HEADS // 2)
            qp = _rope(q_ref[:, pp * LANES:(pp + 1) * LANES], cc, sc)
            out = jnp.zeros((blk, LANES), F32)
            for hh in range(2):
                half = (lane >= hh * HEAD_DIM) & (lane < (hh + 1) * HEAD_DIM)
                qh = jnp.where(half, qp, 0.0).astype(BF16)
                probs, _ = _swa_probs(qh, ks[g][hh].astype(BF16), valid, sink_ref[2 * pp + hh])
                out = out + _dot(probs.astype(BF16), vs[g][hh].astype(BF16))
            o_ref[:, pp * LANES:(pp + 1) * LANES] = out.astype(o_ref.dtype)

    return pl.pallas_call(
        body, name="swa_fwd", grid=(T // blk,),
        in_specs=[pl.BlockSpec(memory_space=pltpu.SMEM), cur(SWA_Q_WIDTH), prev(LANES), cur(LANES), prev(LANES), cur(LANES),
                  prev(LANES), cur(LANES), prev(LANES), cur(LANES)],
        out_specs=cur(SWA_Q_WIDTH),
        out_shape=jax.ShapeDtypeStruct((T, SWA_Q_WIDTH), BF16),
        compiler_params=_params(("parallel",)),
    )(sinks, qa, ka, ka, va, va, cos, cos, sin, sin)


def _swa_bwd(qa, ka, va, cos, sin, sinks, dya):
    T = qa.shape[0]
    blk, cur, prev = _swa_specs(T)
    full = lambda n: pl.BlockSpec((T, n), lambda i: (0, 0))

    def body(sink_ref, q_ref, kp_ref, kc_ref, vp_ref, vc_ref, cp_ref, cc_ref, sp_ref, sc_ref, do_ref,
             dq_ref, dk_ref, dv_ref, dsink_ref):
        i = pl.program_id(0)

        @pl.when(i == 0)
        def _():
            dk_ref[...] = jnp.zeros_like(dk_ref)
            dv_ref[...] = jnp.zeros_like(dv_ref)
            dsink_ref[...] = jnp.zeros_like(dsink_ref)

        cp, cc, sp, sc = cp_ref[...], cc_ref[...], sp_ref[...], sc_ref[...]
        ks, vs, valid = _swa_window(i, kp_ref[...], kc_ref[...], vp_ref[...].astype(F32), vc_ref[...].astype(F32), cp, cc, sp, sc)
        lane = _lane_iota((blk, LANES))
        lane1 = _lane_iota((1, LANES))
        dkw = jnp.zeros((2 * blk, LANES), F32)
        dvw = jnp.zeros((2 * blk, LANES), F32)
        dsink = jnp.zeros((1, LANES), F32)
        for pp in range(SWA_Q_HEADS // 2):
            g = pp // (SWA_Q_HEADS // SWA_KV_HEADS // 2)
            qp = _rope(q_ref[:, pp * LANES:(pp + 1) * LANES], cc, sc)
            dop = do_ref[:, pp * LANES:(pp + 1) * LANES]
            dqp = jnp.zeros((blk, LANES), F32)
            for hh in range(2):
                half = (lane >= hh * HEAD_DIM) & (lane < (hh + 1) * HEAD_DIM)
                qh = jnp.where(half, qp, 0.0).astype(BF16)
                doh = jnp.where(half, dop, 0.0).astype(BF16)
                kk = ks[g][hh].astype(BF16)
                vv = vs[g][hh].astype(BF16)
                probs, psink = _swa_probs(qh, kk, valid, sink_ref[2 * pp + hh])
                dp = _dot_nt(doh, vv)
                dsum = jnp.sum(probs * dp, axis=1, keepdims=True)
                ds = (probs * (dp - dsum) * ATTN_SCALE).astype(BF16)
                dsink = dsink + jnp.where(lane1 == 2 * pp + hh, -jnp.sum(psink * dsum), 0.0)
                dqp = dqp + _dot(ds, kk)
                dk_h = _dot_tn(ds, qh)
                dv_h = _dot_tn(probs.astype(BF16), doh)
                if hh != g:
                    dk_h = pltpu.roll(dk_h, HEAD_DIM, axis=1)
                    dv_h = pltpu.roll(dv_h, HEAD_DIM, axis=1)
                dkw = dkw + dk_h
                dvw = dvw + dv_h
            dq_ref[:, pp * LANES:(pp + 1) * LANES] = _rope_transpose(dqp, cc, sc).astype(dq_ref.dtype)
        dsink_ref[...] += dsink
        ip = jnp.maximum(i - 1, 0)
        rows_p = pl.ds(pl.multiple_of(ip * blk, blk), blk)
        rows_c = pl.ds(pl.multiple_of(i * blk, blk), blk)
        dk_ref[rows_p, :] += _rope_transpose(dkw[:blk], cp, sp)
        dv_ref[rows_p, :] += dvw[:blk]
        dk_ref[rows_c, :] += _rope_transpose(dkw[blk:], cc, sc)
        dv_ref[rows_c, :] += dvw[blk:]

    return pl.pallas_call(
        body, name="swa_bwd", grid=(T // blk,),
        in_specs=[pl.BlockSpec(memory_space=pltpu.SMEM), cur(SWA_Q_WIDTH), prev(LANES), cur(LANES), prev(LANES), cur(LANES),
                  prev(LANES), cur(LANES), prev(LANES), cur(LANES), cur(SWA_Q_WIDTH)],
        out_specs=[cur(SWA_Q_WIDTH), full(LANES), full(LANES), pl.BlockSpec((1, LANES), lambda i: (0, 0))],
        out_shape=[jax.ShapeDtypeStruct((T, SWA_Q_WIDTH), BF16), jax.ShapeDtypeStruct((T, LANES), F32),
                   jax.ShapeDtypeStruct((T, LANES), F32), jax.ShapeDtypeStruct((1, LANES), F32)],
        compiler_params=_params(("arbitrary",)),
    )(sinks, qa, ka, ka, va, va, cos, cos, sin, sin, dya)


class _Exchange:
    FLIPS = [(fx, fy, fc) for fx in (0, 1) for fy in (0, 1) for fc in (0, 1) if (fx, fy, fc) != (0, 0, 0)]

    def __init__(self, arrays, kinds):
        self.arrays, self.kinds, self.n = list(arrays), list(kinds), len(arrays)

    def out_shape(self):
        return [jax.ShapeDtypeStruct(a.shape if k == "scatter" else (N_DEV,) + a.shape, a.dtype) for a, k in zip(self.arrays, self.kinds)]

    def scratch(self):
        return [pltpu.SemaphoreType.DMA((self.n, 7)), pltpu.SemaphoreType.DMA((self.n, 7)), pltpu.SemaphoreType.DMA((self.n,))]

    def bind(self, ins, outs, send_sems, recv_sems, local_sems):
        x, y, c = lax.axis_index("x"), lax.axis_index("y"), lax.axis_index("c")
        me = 4 * x + 2 * y + c
        local, remote = [], []
        for a, kind in enumerate(self.kinds):
            mine = ins[a].at[me] if kind == "scatter" else ins[a]
            local.append(pltpu.make_async_copy(mine, outs[a].at[me], local_sems.at[a]))
            for k, (fx, fy, fc) in enumerate(self.FLIPS):
                peer = (x ^ fx, y ^ fy, c ^ fc)
                peer_slot = 4 * peer[0] + 2 * peer[1] + peer[2]
                src = ins[a].at[peer_slot] if kind == "scatter" else ins[a]
                sems = dict(send_sem=send_sems.at[a, k], recv_sem=recv_sems.at[a, k], device_id=peer, device_id_type=MESH)
                remote.append((pltpu.make_async_remote_copy(src_ref=src, dst_ref=outs[a].at[me], **sems),
                               pltpu.make_async_remote_copy(src_ref=src, dst_ref=outs[a].at[peer_slot], **sems)))

        def start():
            for cp in local:
                cp.start()
            for send, _ in remote:
                send.start()

        def wait():
            for send, arrival in remote:
                arrival.wait_recv()
                send.wait_send()
            for cp in local:
                cp.wait()

        return start, wait


def _hosted_call(body, name, grid, exchange, *, in_specs, out_specs, out_shape, semantics, args):
    if exchange is None:
        outs = pl.pallas_call(body, name=name, grid=grid, in_specs=in_specs, out_specs=out_specs, out_shape=out_shape,
                              compiler_params=_params(semantics))(*args)
        return outs, []
    n, n_in, n_out = exchange.n, len(in_specs), len(out_specs)

    def hosted(*refs):
        ins, rest = refs[:n_in], refs[n_in:]
        ex_ins, rest = rest[:n], rest[n:]
        outs, rest = rest[:n_out], rest[n_out:]
        ex_outs, sems = rest[:n], rest[n:]
        start, wait = exchange.bind(ex_ins, ex_outs, *sems)
        ids = [pl.program_id(d) for d in range(len(grid))]
        first = functools.reduce(jnp.logical_and, [i == 0 for i in ids])
        last = functools.reduce(jnp.logical_and, [i == g - 1 for i, g in zip(ids, grid)])
        pl.when(first)(start)
        body(*ins, *outs)
        pl.when(last)(wait)

    res = pl.pallas_call(
        hosted, name=name, grid=grid, in_specs=list(in_specs) + [ANY] * n, out_specs=list(out_specs) + [ANY] * n,
        out_shape=list(out_shape) + exchange.out_shape(), scratch_shapes=exchange.scratch(),
        compiler_params=_params(("arbitrary",) * len(grid)),
    )(*args, *exchange.arrays)
    return res[:n_out], res[n_out:]


def _sb_scores(qm, k, valid):
    z = _dot_nt(qm, k)
    sp = jnp.maximum(z, 0.0) + jnp.log(1.0 + jnp.exp(-jnp.abs(z)))
    log_one_minus = -sp
    if valid is not None:
        log_one_minus = jnp.where(valid, log_one_minus, 0.0)
    return log_one_minus, z - sp


def _tri2(B, cmp):
    r = lax.broadcasted_iota(jnp.int32, (2 * B, B), 0) % B
    c = lax.broadcasted_iota(jnp.int32, (2 * B, B), 1)
    return cmp(r, c).astype(BF16)


def _tri_sum(v, tri2):
    hi, lo = _split_bf16(v)
    return _dot(jnp.concatenate([hi, lo], axis=1), tri2)


def _head_masks(x):
    low = _lane_iota(x.shape) < HEAD_DIM
    zero = jnp.zeros((), x.dtype)
    return jnp.where(low, x, zero), jnp.where(low, zero, x)


def _strictly_below(B):
    r = lax.broadcasted_iota(jnp.int32, (B, B), 0)
    c = lax.broadcasted_iota(jnp.int32, (B, B), 1)
    return c < r


def _sb_fwd(qb, kb, vb, exchange=None):
    T = qb.shape[0]
    B = min(SB_BLOCK, T)
    assert T // B <= HEAD_DIM
    n_pairs = SB_HEADS // 2

    def body(q_ref, k_ref, v_ref, o_ref, carry_ref):
        i = pl.program_id(1)
        qms = _head_masks(q_ref[...] * ATTN_SCALE)
        lane = _lane_iota((1, LANES))
        upper2 = _tri2(B, lambda r, c: r > c)

        def block(j, state, valid):
            cs, acc, cm = state
            rows = pl.ds(pl.multiple_of(j * B, B), B)
            k = k_ref[rows, :]
            vms = _head_masks(v_ref[rows, :])
            probs, new_cs = [], []
            for hh in range(2):
                lom, lb = _sb_scores(qms[hh], k, valid)
                cm = jnp.where(lane == hh * HEAD_DIM + j, cs[hh], cm)
                a = jnp.exp(lb + (cs[hh] + _tri_sum(lom, upper2)))
                if valid is not None:
                    a = jnp.where(valid, a, 0.0)
                probs.append(a.astype(BF16))
                new_cs.append(cs[hh] + jnp.sum(lom, axis=1, keepdims=True))
            acc = acc + _dot(jnp.concatenate(probs, axis=1), jnp.concatenate(vms, axis=0))
            return tuple(new_cs), acc, cm

        zero = jnp.zeros((B, 1), F32)
        state = block(i, ((zero, zero), jnp.zeros((B, LANES), F32), jnp.zeros((B, LANES), F32)), _strictly_below(B))
        _, acc, cm = lax.fori_loop(0, i, lambda jj, s: block(i - 1 - jj, s, None), state)
        o_ref[...] = acc.astype(o_ref.dtype)
        carry_ref[...] = cm

    return _hosted_call(
        body, "sb_fwd", (n_pairs, T // B), exchange,
        in_specs=[pl.BlockSpec((B, LANES), lambda p, i: (i, p)), pl.BlockSpec((T, LANES), lambda p, i: (0, p)),
                  pl.BlockSpec((T, LANES), lambda p, i: (0, p))],
        out_specs=[pl.BlockSpec((B, LANES), lambda p, i: (i, p))] * 2,
        out_shape=[jax.ShapeDtypeStruct((T, SB_WIDTH), BF16), jax.ShapeDtypeStruct((T, SB_WIDTH), F32)],
        semantics=("parallel", "parallel"), args=(qb, kb, vb))


def _sb_bwd(qb, kb, vb, carries, dyb, exchange=None):
    T = qb.shape[0]
    B = min(SB_BLOCK, T)
    n_pairs = SB_HEADS // 2

    def body(q_ref, k_ref, v_ref, carry_ref, do_ref, dq_ref, dk_ref, dv_ref):
        i = pl.program_id(1)

        @pl.when(i == 0)
        def _():
            dk_ref[...] = jnp.zeros_like(dk_ref)
            dv_ref[...] = jnp.zeros_like(dv_ref)

        qms = _head_masks(q_ref[...] * ATTN_SCALE)
        doms = _head_masks(do_ref[...])
        q2, do2 = jnp.concatenate(qms, axis=0), jnp.concatenate(doms, axis=0)
        cm = carry_ref[...]
        lane = _lane_iota((1, LANES))
        upper2 = _tri2(B, lambda r, c: r > c)
        lower2 = _tri2(B, lambda r, c: r < c)

        def block(j, state, valid):
            cgs, dq = state
            rows = pl.ds(pl.multiple_of(j * B, B), B)
            k = k_ref[rows, :]
            v = v_ref[rows, :]
            dzs, probs, new_cgs = [], [], []
            for hh in range(2):
                lom, lb = _sb_scores(qms[hh], k, valid)
                c = jnp.sum(jnp.where(lane == hh * HEAD_DIM + j, cm, 0.0), axis=1, keepdims=True)
                a = jnp.exp(lb + (c + _tri_sum(lom, upper2)))
                if valid is not None:
                    a = jnp.where(valid, a, 0.0)
                g = a * _dot_nt(doms[hh], v)
                gpre = cgs[hh] + _tri_sum(g, lower2)
                dz = g - jnp.exp(lb) * (g + gpre)
                if valid is not None:
                    dz = jnp.where(valid, dz, 0.0)
                dzs.append(dz.astype(BF16))
                probs.append(a.astype(BF16))
                new_cgs.append(cgs[hh] + jnp.sum(g, axis=1, keepdims=True))
            dq = dq + _dot(jnp.concatenate(dzs, axis=1), jnp.concatenate(_head_masks(k), axis=0))
            dk_ref[rows, :] += _dot_tn(jnp.concatenate(dzs, axis=0), q2)
            dv_ref[rows, :] += _dot_tn(jnp.concatenate(probs, axis=0), do2)
            return tuple(new_cgs), dq

        zero = jnp.zeros((B, 1), F32)
        state = lax.fori_loop(0, i, lambda j, s: block(j, s, None), ((zero, zero), jnp.zeros((B, LANES), F32)))
        _, dq = block(i, state, _strictly_below(B))
        dq_ref[...] = (dq * ATTN_SCALE).astype(dq_ref.dtype)

    blk = pl.BlockSpec((B, LANES), lambda p, i: (i, p))
    full = pl.BlockSpec((T, LANES), lambda p, i: (0, p))
    return _hosted_call(
        body, "sb_bwd", (n_pairs, T // B), exchange,
        in_specs=[blk, full, full, blk, blk],
        out_specs=[blk, full, full],
        out_shape=[jax.ShapeDtypeStruct((T, SB_WIDTH), BF16), jax.ShapeDtypeStruct((T, SB_WIDTH), F32),
                   jax.ShapeDtypeStruct((T, SB_WIDTH), F32)],
        semantics=("parallel", "arbitrary"), args=(qb, kb, vb, carries, dyb))


def _ln_stats(u):
    mu = jnp.mean(u, axis=-1, keepdims=True)
    xc = u - mu
    var = jnp.mean(xc * xc, axis=-1, keepdims=True)
    rstd = lax.rsqrt(var + LN_EPS)
    return xc * rstd, rstd


def _ln_bwd(dy, xhat, rstd, g):
    dxh = dy * g
    return rstd * (dxh - jnp.mean(dxh, axis=-1, keepdims=True) - xhat * jnp.mean(dxh * xhat, axis=-1, keepdims=True))


def _gates(gl_ref, bg_ref):
    ga = jax.nn.sigmoid(gl_ref[:, :D_MODEL] + bg_ref[:, :D_MODEL])
    gb = jax.nn.sigmoid(gl_ref[:, D_MODEL:] + bg_ref[:, D_MODEL:])
    return ga, gb


def _mix_fwd(ya, yb, gl, x, wa, wb, wo, b_gate, ln1_g, ln1_b):
    T = x.shape[0]
    tm = min(256, T)

    def body(ya_ref, yb_ref, gl_ref, x_ref, wa_ref, wb_ref, wo_ref, bg_ref, g_ref, b_ref, h_ref, u_ref, x1_ref):
        ga, gb = _gates(gl_ref, bg_ref)
        h = (ga * _dot(ya_ref[...], wa_ref[...]) + gb * _dot(yb_ref[...], wb_ref[...])).astype(BF16)
        h_ref[...] = h
        u = ALPHA * x_ref[...] + _dot(h, wo_ref[...])
        u_ref[...] = u
        xhat, _ = _ln_stats(u)
        x1_ref[...] = (xhat * g_ref[...] + b_ref[...]).astype(BF16)

    row = lambda n: pl.BlockSpec((tm, n), lambda i: (i, 0))
    const = lambda r, n: pl.BlockSpec((r, n), lambda i: (0, 0))
    return pl.pallas_call(
        body, name="mix_fwd", grid=(T // tm,),
        in_specs=[row(SWA_Q_WIDTH), row(SB_WIDTH), row(GATE_WIDTH), row(D_MODEL), const(SWA_Q_WIDTH, D_MODEL), const(SB_WIDTH, D_MODEL),
                  const(D_MODEL, D_MODEL), const(1, GATE_WIDTH), const(1, D_MODEL), const(1, D_MODEL)],
        out_specs=[row(D_MODEL)] * 3,
        out_shape=[jax.ShapeDtypeStruct((T, D_MODEL), BF16), jax.ShapeDtypeStruct((T, D_MODEL), F32), jax.ShapeDtypeStruct((T, D_MODEL), BF16)],
        compiler_params=_params(("parallel",)),
    )(ya, yb, gl, x, wa, wb, wo, b_gate, ln1_g, ln1_b)


def _mix_bwd(du1, ya, yb, gl, wa, wb, wo, b_gate):
    T = du1.shape[0]
    tm = min(256, T)

    def body(du_ref, ya_ref, yb_ref, gl_ref, wa_ref, wb_ref, wo_ref, bg_ref, dya_ref, dyb_ref, dgl_ref, dta_ref, dtb_ref, dbg_ref):
        @pl.when(pl.program_id(0) == 0)
        def _():
            dbg_ref[...] = jnp.zeros_like(dbg_ref)

        dh = _dot_nt(du_ref[...].astype(BF16), wo_ref[...])
        ga, gb = _gates(gl_ref, bg_ref)
        for gate, y_ref, w_ref, dy_ref, dt_ref, lo in ((ga, ya_ref, wa_ref, dya_ref, dta_ref, 0), (gb, yb_ref, wb_ref, dyb_ref, dtb_ref, D_MODEL)):
            t = _dot(y_ref[...], w_ref[...])
            dlogit = dh * t * gate * (1.0 - gate)
            dgl_ref[:, lo:lo + D_MODEL] = dlogit.astype(BF16)
            dbg_ref[:, lo:lo + D_MODEL] += jnp.sum(dlogit, axis=0, keepdims=True)
            dt = (dh * gate).astype(BF16)
            dt_ref[...] = dt
            dy_ref[...] = _dot_nt(dt, w_ref[...]).astype(BF16)

    row = lambda n: pl.BlockSpec((tm, n), lambda i: (i, 0))
    const = lambda r, n: pl.BlockSpec((r, n), lambda i: (0, 0))
    sds = lambda n, dt: jax.ShapeDtypeStruct((T, n), dt)
    return pl.pallas_call(
        body, name="mix_bwd", grid=(T // tm,),
        in_specs=[row(D_MODEL), row(SWA_Q_WIDTH), row(SB_WIDTH), row(GATE_WIDTH), const(SWA_Q_WIDTH, D_MODEL), const(SB_WIDTH, D_MODEL),
                  const(D_MODEL, D_MODEL), const(1, GATE_WIDTH)],
        out_specs=[row(SWA_Q_WIDTH), row(SB_WIDTH), row(GATE_WIDTH), row(D_MODEL), row(D_MODEL), const(1, GATE_WIDTH)],
        out_shape=[sds(SWA_Q_WIDTH, BF16), sds(SB_WIDTH, BF16), sds(GATE_WIDTH, BF16), sds(D_MODEL, BF16), sds(D_MODEL, BF16),
                   jax.ShapeDtypeStruct((1, GATE_WIDTH), F32)],
        compiler_params=_params(("arbitrary",)),
    )(du1, ya, yb, gl, wa, wb, wo, b_gate)


CONV_COLS = LANES


def _shift_down(v, k):
    row = lax.broadcasted_iota(jnp.int32, v.shape, 0)
    return jnp.where(row >= k, pltpu.roll(v, k, axis=0), 0.0)


def _shift_up(v, k):
    n = v.shape[0]
    row = lax.broadcasted_iota(jnp.int32, v.shape, 0)
    return jnp.where(row < n - k, pltpu.roll(v, n - k, axis=0), 0.0)


def _conv(pv, w_ref, b_ref):
    return w_ref[0:1, :] * _shift_down(pv, 2) + w_ref[1:2, :] * _shift_down(pv, 1) + w_ref[2:3, :] * pv + b_ref[...]


def _conv_specs(T):
    nb = D_FF // CONV_COLS
    pair = pl.BlockSpec((2, T, CONV_COLS), lambda j: (0, 0, j))
    gate = lambda r: pl.BlockSpec((r, CONV_COLS), lambda j: (0, j))
    up = lambda r: pl.BlockSpec((r, CONV_COLS), lambda j: (0, j + nb))
    return nb, pair, gate, up


def _conv_glu_fwd(p3, conv_w, conv_b):
    T = p3.shape[1]
    nb, pair, gate, up = _conv_specs(T)

    def body(p_ref, wg_ref, wu_ref, bg_ref, bu_ref, s_ref):
        ag = _conv(p_ref[0], wg_ref, bg_ref)
        au = _conv(p_ref[1], wu_ref, bu_ref)
        s_ref[...] = (ag * jax.nn.sigmoid(ag) * au).astype(BF16)

    return pl.pallas_call(
        body, name="conv_glu_fwd", grid=(nb,),
        in_specs=[pair, gate(3), up(3), gate(1), up(1)],
        out_specs=pl.BlockSpec((T, CONV_COLS), lambda j: (0, j)),
        out_shape=jax.ShapeDtypeStruct((T, D_FF), BF16),
        compiler_params=_params(("parallel",)),
    )(p3, conv_w, conv_w, conv_b, conv_b)


def _conv_glu_bwd(p3, ds, conv_w, conv_b):
    T = p3.shape[1]
    nb, pair, gate, up = _conv_specs(T)

    def body(p_ref, ds_ref, wg_ref, wu_ref, bg_ref, bu_ref, dp_ref, dwg_ref, dwu_ref, dbg_ref, dbu_ref):
        pg, pu = p_ref[0], p_ref[1]
        ag = _conv(pg, wg_ref, bg_ref)
        au = _conv(pu, wu_ref, bu_ref)
        sg = jax.nn.sigmoid(ag)
        d = ds_ref[...]
        dau = d * ag * sg
        dag = d * au * (sg * (1.0 + ag * (1.0 - sg)))
        for half, (da, pv, w_ref, dw_ref, db_ref) in enumerate(((dag, pg, wg_ref, dwg_ref, dbg_ref), (dau, pu, wu_ref, dwu_ref, dbu_ref))):
            db_ref[...] = jnp.sum(da, axis=0, keepdims=True)
            dw_ref[0:1, :] = jnp.sum(da * _shift_down(pv, 2), axis=0, keepdims=True)
            dw_ref[1:2, :] = jnp.sum(da * _shift_down(pv, 1), axis=0, keepdims=True)
            dw_ref[2:3, :] = jnp.sum(da * pv, axis=0, keepdims=True)
            dp = w_ref[2:3, :] * da + w_ref[1:2, :] * _shift_up(da, 1) + w_ref[0:1, :] * _shift_up(da, 2)
            dp_ref[half] = dp.astype(BF16)

    col = lambda r: pl.BlockSpec((r, CONV_COLS), lambda j: (0, j))
    return pl.pallas_call(
        body, name="conv_glu_bwd", grid=(nb,),
        in_specs=[pair, col(T), gate(3), up(3), gate(1), up(1)],
        out_specs=[pair, col(3), col(3), col(1), col(1)],
        out_shape=[jax.ShapeDtypeStruct((2, T, D_FF), BF16), jax.ShapeDtypeStruct((3, D_FF), F32), jax.ShapeDtypeStruct((3, D_FF), F32),
                   jax.ShapeDtypeStruct((1, D_FF), F32), jax.ShapeDtypeStruct((1, D_FF), F32)],
        compiler_params=_params(("parallel",)),
    )(p3, ds, conv_w, conv_w, conv_b, conv_b)


def _ffn_down_loss(s, w_down, u1, ln1_g, ln1_b, ln2_g, ln2_b, target):
    T = u1.shape[0]
    tm = min(256, T)

    def body(s_ref, w_ref, u1_ref, g1_ref, b1_ref, g2_ref, b2_ref, t_ref, du_ref, dub_ref, dg_ref, db_ref, loss_ref):
        @pl.when(pl.program_id(0) == 0)
        def _():
            dg_ref[...] = jnp.zeros_like(dg_ref)
            db_ref[...] = jnp.zeros_like(db_ref)
            loss_ref[...] = jnp.zeros_like(loss_ref)

        xh1, _ = _ln_stats(u1_ref[...])
        x1 = xh1 * g1_ref[...] + b1_ref[...]
        u2 = ALPHA * x1 + _dot(s_ref[...], w_ref[...])
        xh2, rstd2 = _ln_stats(u2)
        err = xh2 * g2_ref[...] + b2_ref[...] - t_ref[...]
        per_token = jnp.mean(err * err, axis=-1, keepdims=True)
        loss_ref[...] += 0.5 * jnp.sum(per_token, axis=0, keepdims=True)
        dy = err * (1.0 / D_MODEL)
        dg_ref[...] += jnp.sum(dy * xh2, axis=0, keepdims=True)
        db_ref[...] += jnp.sum(dy, axis=0, keepdims=True)
        du2 = _ln_bwd(dy, xh2, rstd2, g2_ref[...])
        du_ref[...] = du2
        dub_ref[...] = du2.astype(BF16)

    row = lambda n: pl.BlockSpec((tm, n), lambda i: (i, 0))
    const = lambda r, n: pl.BlockSpec((r, n), lambda i: (0, 0))
    vec = const(1, D_MODEL)
    return pl.pallas_call(
        body, name="ffn_down_loss", grid=(T // tm,),
        in_specs=[row(D_FF), const(D_FF, D_MODEL), row(D_MODEL), vec, vec, vec, vec, row(D_MODEL)],
        out_specs=[row(D_MODEL), row(D_MODEL), vec, vec, const(1, LANES)],
        out_shape=[jax.ShapeDtypeStruct((T, D_MODEL), F32), jax.ShapeDtypeStruct((T, D_MODEL), BF16), jax.ShapeDtypeStruct((1, D_MODEL), F32),
                   jax.ShapeDtypeStruct((1, D_MODEL), F32), jax.ShapeDtypeStruct((1, LANES), F32)],
        compiler_params=_params(("arbitrary",)),
    )(s, w_down, u1, ln1_g, ln1_b, ln2_g, ln2_b, target)


def _ffn_up_bwd_ln1(dp3, w_up, du2, u1, ln1_g):
    T = u1.shape[0]
    tm = min(256, T)

    def body(dp_ref, w_ref, du2_ref, u1_ref, g_ref, du_ref, dub_ref, dg_ref, db_ref):
        @pl.when(pl.program_id(0) == 0)
        def _():
            dg_ref[...] = jnp.zeros_like(dg_ref)
            db_ref[...] = jnp.zeros_like(db_ref)

        dx1 = _dot_nt(dp_ref[0], w_ref[:, :D_FF]) + _dot_nt(dp_ref[1], w_ref[:, D_FF:]) + ALPHA * du2_ref[...]
        xh, rstd = _ln_stats(u1_ref[...])
        dg_ref[...] += jnp.sum(dx1 * xh, axis=0, keepdims=True)
        db_ref[...] += jnp.sum(dx1, axis=0, keepdims=True)
        du1 = _ln_bwd(dx1, xh, rstd, g_ref[...])
        du_ref[...] = du1
        dub_ref[...] = du1.astype(BF16)

    row = lambda n: pl.BlockSpec((tm, n), lambda i: (i, 0))
    const = lambda r, n: pl.BlockSpec((r, n), lambda i: (0, 0))
    vec = const(1, D_MODEL)
    return pl.pallas_call(
        body, name="ffn_up_bwd_ln1", grid=(T // tm,),
        in_specs=[pl.BlockSpec((2, tm, D_FF), lambda i: (0, i, 0)), const(D_MODEL, 2 * D_FF), row(D_MODEL), row(D_MODEL), vec],
        out_specs=[row(D_MODEL), row(D_MODEL), vec, vec],
        out_shape=[jax.ShapeDtypeStruct((T, D_MODEL), F32), jax.ShapeDtypeStruct((T, D_MODEL), BF16), jax.ShapeDtypeStruct((1, D_MODEL), F32),
                   jax.ShapeDtypeStruct((1, D_MODEL), F32)],
        compiler_params=_params(("arbitrary",)),
    )(dp3, w_up, du2, u1, ln1_g)


def _local_step(x, positions, w_in, b_gate, sinks, ln1_g, ln1_b, conv_b, ln2_g, ln2_b, target, later_weights,
                early_exchange=None, tail_exchange=None):
    T = x.shape[0]
    inv_freq = 1.0 / (ROPE_THETA ** (jnp.arange(0, HEAD_DIM, 2, dtype=F32) / HEAD_DIM))
    cos, sin = _rope_tables(positions.reshape(T, 1), jnp.tile(inv_freq, LANES // (HEAD_DIM // 2)).reshape(1, LANES))

    xb, qa, ka, va, qb, kb, vb, gl = _in_proj(x, w_in)
    ya = _swa_fwd(qa, ka, va, cos, sin, sinks)
    if isinstance(later_weights, tuple):
        exchange, finish = later_weights
        (yb, carries), arrived = _sb_fwd(qb, kb, vb, exchange)
        later_weights = finish(arrived)
    else:
        (yb, carries), _ = _sb_fwd(qb, kb, vb)
    wa, wb, wo, w_up, conv_w, w_down = later_weights
    h, u1, x1 = _mix_fwd(ya, yb, gl, x, wa, wb, wo, b_gate, ln1_g, ln1_b)

    ff_tn = D_FF // 2
    nff = D_FF // ff_tn
    tm = min(512, T)
    p3 = _matmul(x1, w_up, kind="nn", name="ffn_up", grid=(T // tm, 2 * nff),
                 a_spec=pl.BlockSpec((tm, D_MODEL), lambda i, j: (i, 0)), b_spec=pl.BlockSpec((D_MODEL, ff_tn), lambda i, j: (0, j)),
                 out_spec=pl.BlockSpec((None, tm, ff_tn), lambda i, j: (j // nff, i, j % nff)),
                 out_shape=jax.ShapeDtypeStruct((2, T, D_FF), F32))
    s = _conv_glu_fwd(p3, conv_w, conv_b)
    du2, du2b, dln2_g, dln2_b, loss = _ffn_down_loss(s, w_down, u1, ln1_g, ln1_b, ln2_g, ln2_b, target)

    ds = _matmul(du2b, w_down, kind="nt", name="ffn_down_bwd", grid=(T // tm, nff),
                 a_spec=pl.BlockSpec((tm, D_MODEL), lambda i, j: (i, 0)), b_spec=pl.BlockSpec((ff_tn, D_MODEL), lambda i, j: (j, 0)),
                 out_spec=pl.BlockSpec((tm, ff_tn), lambda i, j: (i, j)), out_shape=jax.ShapeDtypeStruct((T, D_FF), F32))
    dp3, dcw_g, dcw_u, dcb_g, dcb_u = _conv_glu_bwd(p3, ds, conv_w, conv_b)
    tk = 256
    dw_down = _matmul(s, du2b, kind="tn", name="dw_down", grid=(D_FF // tk,),
                      a_spec=pl.BlockSpec((T, tk), lambda i: (0, i)), b_spec=pl.BlockSpec((T, D_MODEL), lambda i: (0, 0)),
                      out_spec=pl.BlockSpec((tk, D_MODEL), lambda i: (i, 0)), out_shape=jax.ShapeDtypeStruct((D_FF, D_MODEL), BF16))
    dw_up = _matmul(x1, dp3, kind="tn", name="dw_up", grid=(D_MODEL // 512, 2 * nff),
                    a_spec=pl.BlockSpec((T, 512), lambda i, j: (0, i)), b_spec=pl.BlockSpec((None, T, ff_tn), lambda i, j: (j // nff, 0, j % nff)),
                    out_spec=pl.BlockSpec((512, ff_tn), lambda i, j: (i, j)), out_shape=jax.ShapeDtypeStruct((D_MODEL, 2 * D_FF), BF16))
    du1, du1b, dln1_g, dln1_b = _ffn_up_bwd_ln1(dp3, w_up, du2, u1, ln1_g)
    dya, dyb, dgl, dta, dtb, db_gate = _mix_bwd(du1, ya, yb, gl, wa, wb, wo, b_gate)

    def dw_tn(a, g, name):
        rows, cols = a.shape[1], g.shape[1]
        tn = min(512, cols)
        return _matmul(a, g, kind="tn", name=name, grid=(rows // 512, cols // tn),
                       a_spec=pl.BlockSpec((T, 512), lambda i, j: (0, i)), b_spec=pl.BlockSpec((T, tn), lambda i, j: (0, j)),
                       out_spec=pl.BlockSpec((512, tn), lambda i, j: (i, j)), out_shape=jax.ShapeDtypeStruct((rows, cols), BF16))

    dwa = dw_tn(ya, dta, "dw_branch_a")
    dwb = dw_tn(yb, dtb, "dw_branch_b")
    dwo = dw_tn(h, du1b, "dw_out")

    grads = dict(
        b_gate=db_gate, w_branch_a=dwa, w_branch_b=dwb, w_out=dwo, ln1_g=dln1_g, ln1_b=dln1_b,
        w_up=dw_up, conv_w=jnp.concatenate([dcw_g, dcw_u], axis=1), conv_b=(dcb_g, dcb_u), w_down=dw_down, ln2_g=dln2_g, ln2_b=dln2_b)
    (dqb, dkb, dvb), early_out = _sb_bwd(qb, kb, vb, carries, dyb, early_exchange(grads) if early_exchange else None)
    dqa, dka, dva, grads["sinks"] = _swa_bwd(qa, ka, va, cos, sin, sinks, dya)
    dproj = (dqa, dka, dva, dqb, dkb, dvb, dgl)
    grads["w_in"] = tuple(dw_tn(xb, piece, f"dw_in_{k}") for k, piece in enumerate(dproj))
    grad_x, tail_out = _grad_x(dproj, w_in, du1, tail_exchange(grads, loss) if tail_exchange else None)
    return loss, grad_x, grads, early_out, tail_out


def _grad_x(dproj, w_in, du1, exchange=None):
    T = du1.shape[0]
    tm = min(256, T)
    offs = np.cumsum((0,) + IN_WIDTHS)

    def body(*refs):
        pieces, (w_ref, du_ref, o_ref) = refs[:len(IN_WIDTHS)], refs[len(IN_WIDTHS):]
        acc = ALPHA * du_ref[...]
        for p_ref, a, b in zip(pieces, offs[:-1], offs[1:]):
            acc = acc + _dot_nt(p_ref[...].astype(BF16), w_ref[:, a:b])
        o_ref[...] = acc

    row = lambda n: pl.BlockSpec((tm, n), lambda i: (i, 0))
    (grad_x,), arrived = _hosted_call(
        body, "grad_x", (T // tm,), exchange,
        in_specs=[row(n) for n in IN_WIDTHS] + [pl.BlockSpec((D_MODEL, IN_TOTAL), lambda i: (0, 0)), row(D_MODEL)],
        out_specs=[row(D_MODEL)], out_shape=[jax.ShapeDtypeStruct((T, D_MODEL), F32)], semantics=("parallel",),
        args=(*dproj, w_in, du1))
    return grad_x, arrived


ANY = pl.BlockSpec(memory_space=pl.ANY)


def _all_gather(slabs, name):
    n = len(slabs)

    def body(*refs):
        ins, outs = refs[:n], refs[n:2 * n]
        send_sems, recv_sems, local_sems = refs[2 * n:]
        x, y, c = lax.axis_index("x"), lax.axis_index("y"), lax.axis_index("c")
        me, sibling = (x, y, c), (x, y, 1 - c)
        chips = [(1 - x, y), (x, 1 - y), (1 - x, 1 - y)]

        def slot(pos):
            return 4 * pos[0] + 2 * pos[1] + pos[2]

        def copy(a, k, block, to, from_input=False):
            return pltpu.make_async_remote_copy(
                src_ref=ins[a] if from_input else outs[a].at[slot(block)], dst_ref=outs[a].at[slot(block)],
                send_sem=send_sems.at[a, k], recv_sem=recv_sems.at[a, k], device_id=to, device_id_type=MESH)

        mine = [pltpu.make_async_copy(ins[a], outs[a].at[slot(me)], local_sems.at[a]) for a in range(n)]
        for cp in mine:
            cp.start()
        first = []
        for a in range(n):
            first.append(copy(a, 0, me, sibling, from_input=True))
            first += [copy(a, 1 + j, me, (*chip, c), from_input=True) for j, chip in enumerate(chips)]
        for cp in first:
            cp.start()
        passed = []
        for j, chip in enumerate(chips):
            for a in range(n):
                copy(a, 1 + j, (*chip, c), me).wait_recv()
                fwd = copy(a, 4 + j, (*chip, c), sibling)
                fwd.start()
                passed.append(fwd)
        for a in range(n):
            copy(a, 0, sibling, me).wait_recv()
            for j, chip in enumerate(chips):
                copy(a, 4 + j, (*chip, 1 - c), me).wait_recv()
        for cp in first + passed:
            cp.wait_send()
        for cp in mine:
            cp.wait()

    return pl.pallas_call(
        body, name=name,
        in_specs=[ANY] * n, out_specs=[ANY] * n,
        out_shape=[jax.ShapeDtypeStruct((N_DEV,) + s.shape, s.dtype) for s in slabs],
        scratch_shapes=[pltpu.SemaphoreType.DMA((n, 7)), pltpu.SemaphoreType.DMA((n, 7)), pltpu.SemaphoreType.DMA((n,))],
    )(*slabs)


def _all_to_all(slabs, name):
    n = len(slabs)

    def body(*refs):
        ins, outs = refs[:n], refs[n:2 * n]
        send_sems, recv_sems, local_sems = refs[2 * n:]
        x, y, c = lax.axis_index("x"), lax.axis_index("y"), lax.axis_index("c")
        my_slot = 4 * x + 2 * y + c
        flips = [(fx, fy, fc) for fx in (0, 1) for fy in (0, 1) for fc in (0, 1) if (fx, fy, fc) != (0, 0, 0)]

        def copy(a, k):
            fx, fy, fc = flips[k]
            peer = (x ^ fx, y ^ fy, c ^ fc)
            peer_slot = 4 * peer[0] + 2 * peer[1] + peer[2]
            send = pltpu.make_async_remote_copy(src_ref=ins[a].at[peer_slot], dst_ref=outs[a].at[my_slot], send_sem=send_sems.at[a, k],
                                                recv_sem=recv_sems.at[a, k], device_id=peer, device_id_type=MESH)
            recv = pltpu.make_async_remote_copy(src_ref=ins[a].at[peer_slot], dst_ref=outs[a].at[peer_slot], send_sem=send_sems.at[a, k],
                                                recv_sem=recv_sems.at[a, k], device_id=peer, device_id_type=MESH)
            return send, recv

        mine = [pltpu.make_async_copy(ins[a].at[my_slot], outs[a].at[my_slot], local_sems.at[a]) for a in range(n)]
        for cp in mine:
            cp.start()
        copies = [copy(a, k) for a in range(n) for k in range(len(flips))]
        for send, _ in copies:
            send.start()
        for send, recv in copies:
            recv.wait_recv()
            send.wait_send()
        for cp in mine:
            cp.wait()

    return pl.pallas_call(
        body, name=name,
        in_specs=[ANY] * n, out_specs=[ANY] * n,
        out_shape=[jax.ShapeDtypeStruct(s.shape, s.dtype) for s in slabs],
        scratch_shapes=[pltpu.SemaphoreType.DMA((n, 7)), pltpu.SemaphoreType.DMA((n, 7)), pltpu.SemaphoreType.DMA((n,))],
    )(*slabs)


def _row_tile(rows):
    for cand in range(256, 7, -8):
        if rows % cand == 0:
            return cand
    return rows


def _window(w):
    wp = max(-(-((w * r) % LANES + w) // LANES) for r in range(N_DEV)) * LANES
    assert all((w * r) // LANES * LANES + wp <= N_DEV * w for r in range(N_DEV))
    return wp


def _join_cols(slabs, name):
    _, R, w = slabs.shape
    tr = _row_tile(R)
    wp = _window(w)

    def body(g_ref, o_ref, pad_ref):
        if w % LANES == 0:
            for r in range(N_DEV):
                o_ref[:, w * r:w * (r + 1)] = g_ref[r]
            return
        o_ref[...] = jnp.zeros_like(o_ref)
        pad_ref[...] = jnp.zeros_like(pad_ref)
        for r in range(N_DEV):
            q, s = divmod(w * r, LANES)
            pad_ref[:, :w] = g_ref[r]
            y = pad_ref[...]
            if s:
                y = pltpu.roll(y, s, axis=1)
            o_ref[:, LANES * q:LANES * q + wp] += y

    return pl.pallas_call(
        body, name=name, grid=(R // tr,),
        in_specs=[pl.BlockSpec((N_DEV, tr, w), lambda i: (0, i, 0))], out_specs=pl.BlockSpec((tr, N_DEV * w), lambda i: (i, 0)),
        out_shape=jax.ShapeDtypeStruct((R, N_DEV * w), slabs.dtype), scratch_shapes=[pltpu.VMEM((tr, wp), slabs.dtype)],
        compiler_params=_params(("parallel",)),
    )(slabs)


def _split_cols(pieces, name):
    R = pieces[0].shape[0]
    widths = [p.shape[1] for p in pieces]
    total = sum(widths)
    w = total // N_DEV
    tr = _row_tile(R)
    wp = _window(w)
    offs = np.cumsum([0] + widths)
    dtype = pieces[0].dtype

    def body(*refs):
        ins, (o_ref, full_ref) = refs[:len(pieces)], refs[len(pieces):]
        for p_ref, a, b in zip(ins, offs[:-1], offs[1:]):
            full_ref[:, a:b] = p_ref[...].astype(dtype)
        for r in range(N_DEV):
            q, s = divmod(w * r, LANES)
            y = full_ref[:, LANES * q:LANES * q + wp]
            if s:
                y = pltpu.roll(y, wp - s, axis=1)
            o_ref[r] = y[:, :w]

    return pl.pallas_call(
        body, name=name, grid=(R // tr,),
        in_specs=[pl.BlockSpec((tr, n), lambda i: (i, 0)) for n in widths], out_specs=pl.BlockSpec((N_DEV, tr, w), lambda i: (0, i, 0)),
        out_shape=jax.ShapeDtypeStruct((N_DEV, R, w), dtype), scratch_shapes=[pltpu.VMEM((tr, total), dtype)],
        compiler_params=_params(("parallel",)),
    )(*pieces)


def _adamw(g, w, m, v):
    m_new = ADAM_B1 * m + (1.0 - ADAM_B1) * g
    v_new = ADAM_B2 * v + (1.0 - ADAM_B2) * jnp.square(g)
    m_hat = m_new / (1.0 - ADAM_B1 ** ADAM_STEP)
    v_hat = v_new / (1.0 - ADAM_B2 ** ADAM_STEP)
    return -ADAM_LR * (m_hat / (jnp.sqrt(v_hat) + ADAM_EPS) + ADAM_WD * w), m_new, v_new


def _sum_parts(p_ref):
    g = p_ref[0].astype(F32)
    for d in range(1, N_DEV):
        g = g + p_ref[d].astype(F32)
    return g


def _reduce_adamw(parts, w, m, v, name):
    R, C = w.shape
    tr = _row_tile(R)

    def body(p_ref, w_ref, m_ref, v_ref, g_ref, d_ref, mo_ref, vo_ref):
        g = _sum_parts(p_ref)
        g_ref[...] = g
        d_ref[...], mo_ref[...], vo_ref[...] = _adamw(g, w_ref[...], m_ref[...], v_ref[...])

    row = pl.BlockSpec((tr, C), lambda i: (i, 0))
    return pl.pallas_call(
        body, name=name, grid=(R // tr,),
        in_specs=[pl.BlockSpec((N_DEV, tr, C), lambda i: (0, i, 0)), row, row, row],
        out_specs=[row] * 4, out_shape=[jax.ShapeDtypeStruct((R, C), F32)] * 4,
        compiler_params=_params(("parallel",)),
    )(parts, w, m, v)


def _reduce_adamw_small(parts, ws, ms, vs):
    sizes = [a.shape[1] for a in ws]
    k = len(sizes)
    offs = np.cumsum([0] + [-(-n // LANES) * LANES for n in sizes])

    def body(*refs):
        p_ref, w_refs, m_refs, v_refs = refs[0], refs[1:1 + k], refs[1 + k:1 + 2 * k], refs[1 + 2 * k:1 + 3 * k]
        outs, loss_ref = refs[1 + 3 * k:-1], refs[-1]
        g_all = _sum_parts(p_ref)
        for j, n in enumerate(sizes):
            g = g_all[:, offs[j]:offs[j] + LANES * (-(-n // LANES))][:, :n]
            outs[4 * j][...] = g
            outs[4 * j + 1][...], outs[4 * j + 2][...], outs[4 * j + 3][...] = _adamw(g, w_refs[j][...], m_refs[j][...], v_refs[j][...])
        loss_ref[...] = g_all[:, offs[k]:offs[k] + LANES]

    vm = pl.BlockSpec(memory_space=pltpu.VMEM)
    out_shape = [jax.ShapeDtypeStruct((1, n), F32) for n in sizes for _ in range(4)] + [jax.ShapeDtypeStruct((1, LANES), F32)]
    res = pl.pallas_call(
        body, name="reduce_adamw_replicated", in_specs=[vm] * (1 + 3 * k), out_specs=[vm] * len(out_shape), out_shape=out_shape,
        compiler_params=_params(),
    )(parts, *ws, *ms, *vs)
    return [res[4 * j:4 * j + 4] for j in range(k)], res[-1]


COL_SHARDED = ("w_in", "w_branch_a", "w_branch_b", "w_up", "conv_w")
ROW_SHARDED = ("w_out", "w_down")
SMALL = ("b_gate", "sinks", "ln1_g", "ln1_b", "conv_b", "ln2_g", "ln2_b")
ORDER = ("w_in", "b_gate", "sinks", "w_branch_a", "w_branch_b", "w_out", "ln1_g", "ln1_b", "w_up", "conv_w", "conv_b", "w_down", "ln2_g", "ln2_b")


def _pad_lanes(a):
    pad = (-a.shape[-1]) % LANES
    return a if pad == 0 else jnp.pad(a, ((0, 0), (0, pad)))


def kernel(x, positions, w_in, b_gate, sinks, w_branch_a, w_branch_b, w_out, ln1_g, ln1_b, w_up, conv_w, conv_b, w_down, ln2_g, ln2_b, loss_target, m_w_in, m_b_gate, m_sinks, m_w_branch_a, m_w_branch_b, m_w_out, m_ln1_g, m_ln1_b, m_w_up, m_conv_w, m_conv_b, m_w_down, m_ln2_g, m_ln2_b, v_w_in, v_b_gate, v_sinks, v_w_branch_a, v_w_branch_b, v_w_out, v_ln1_g, v_ln1_b, v_w_up, v_conv_w, v_conv_b, v_w_down, v_ln2_g, v_ln2_b):
    args = dict(locals())
    sharded = COL_SHARDED + ROW_SHARDED
    w = {n: args[n][0] if n in sharded else args[n] for n in ORDER}
    m = {n: args["m_" + n][0] if n in sharded else args["m_" + n] for n in ORDER}
    v = {n: args["v_" + n][0] if n in sharded else args["v_" + n] for n in ORDER}

    travel = {n: (w[n] if n == "conv_w" else w[n].astype(BF16)) for n in sharded}
    (g_in,) = _all_gather([travel["w_in"]], "all_gather_w_in")
    w_in_full = _join_cols(g_in, "join_w_in")
    later = ("w_branch_a", "w_branch_b", "w_out", "w_up", "conv_w", "w_down")

    def join(name, slabs):
        return _join_cols(slabs, "join_" + name) if name in COL_SHARDED else slabs.reshape(-1, slabs.shape[-1])

    def split(name, grad):
        if name in COL_SHARDED:
            return _split_cols(grad if isinstance(grad, tuple) else (grad,), "split_d" + name)
        return grad.reshape((N_DEV, -1, grad.shape[-1]))

    def early_exchange(grads):
        return _Exchange([split(n, grads[n]) for n in later], ["scatter"] * len(later))

    def tail_exchange(grads, loss):
        small_pack = jnp.concatenate(
            [_pad_lanes(p) for n in SMALL for p in (grads[n] if isinstance(grads[n], tuple) else (grads[n],))] + [loss], axis=1)
        return _Exchange([split("w_in", grads["w_in"]), small_pack], ["scatter", "gather"])

    gather_later = _Exchange([travel[n] for n in later], ["gather"] * len(later))
    _, grad_x, _, early_out, (recv_w_in, small_parts) = _local_step(
        x[0], positions[0], w_in_full, w["b_gate"], w["sinks"][0], w["ln1_g"], w["ln1_b"], w["conv_b"], w["ln2_g"], w["ln2_b"], loss_target[0],
        (gather_later, lambda arrived: [join(n, a) for n, a in zip(later, arrived)]), early_exchange, tail_exchange)
    recv = dict(zip(later, early_out), w_in=recv_w_in)

    res = {n: _reduce_adamw(recv[n], w[n], m[n], v[n], "reduce_adamw_" + n) for n in sharded}
    small_res, loss_sum = _reduce_adamw_small(small_parts, [w[n] for n in SMALL], [m[n] for n in SMALL], [v[n] for n in SMALL])
    res.update(zip(SMALL, small_res))
    out = [loss_sum[0, 0], grad_x[None]]
    for k in range(4):
        out += [res[n][k][None] if n in sharded else res[n][k] for n in ORDER]
    return tuple(out)
```

```python
import functools

import jax
import jax.numpy as jnp
import numpy as np
from jax import lax
from jax.experimental import pallas as pl
from jax.experimental.pallas import tpu as pltpu

D_MODEL = 1024
HEAD_DIM = 64
SWA_Q_HEADS = 8
SWA_KV_HEADS = 2
SB_HEADS = 8
WINDOW = 128
ROPE_THETA = 10000.0
D_FF = 2816
LN_EPS = 1e-5
DEPTH = 1
ALPHA = (2.0 * DEPTH) ** 0.25
SWA_Q_WIDTH = SWA_Q_HEADS * HEAD_DIM
SWA_KV_WIDTH = SWA_KV_HEADS * HEAD_DIM
SB_WIDTH = SB_HEADS * HEAD_DIM
GATE_WIDTH = 2 * D_MODEL
IN_WIDTHS = (SWA_Q_WIDTH, SWA_KV_WIDTH, SWA_KV_WIDTH, SB_WIDTH, SB_WIDTH, SB_WIDTH, GATE_WIDTH)
IN_TOTAL = sum(IN_WIDTHS)
ATTN_SCALE = HEAD_DIM ** -0.5

ADAM_LR = 0.001
ADAM_B1 = 0.9
ADAM_B2 = 0.999
ADAM_EPS = 1e-08
ADAM_WD = 0.01
ADAM_STEP = 10

N_DEV = 8
LANES = 128
SB_BLOCK = 256
SB_PAIRS = 2
VMEM_LIMIT = 56 * 1024 * 1024

F32 = jnp.float32
BF16 = jnp.bfloat16
MESH = pl.DeviceIdType.MESH


def _params(sem=None):
    return pltpu.CompilerParams(dimension_semantics=sem, vmem_limit_bytes=VMEM_LIMIT)


def _dot(a, b):
    return jnp.dot(a, b, preferred_element_type=F32)


def _dot_nt(a, b):
    return lax.dot_general(a, b, (((1,), (1,)), ((), ())), preferred_element_type=F32)


def _dot_tn(a, b):
    return lax.dot_general(a, b, (((0,), (0,)), ((), ())), preferred_element_type=F32)


def _split_bf16(v):
    hi = v.astype(BF16)
    lo = (v - hi.astype(F32)).astype(BF16)
    return hi, lo


def _matmul(a, b, *, kind, out_shape, grid, a_spec, b_spec, out_spec, name, add=None, add_spec=None, add_scale=1.0):
    dot = {"nn": _dot, "nt": _dot_nt, "tn": _dot_tn}[kind]

    def body(*refs):
        if add is None:
            a_ref, b_ref, o_ref = refs
        else:
            a_ref, b_ref, add_ref, o_ref = refs
        r = dot(a_ref[...].astype(BF16), b_ref[...].astype(BF16))
        if add is not None:
            r = r + add_scale * add_ref[...]
        o_ref[...] = r.astype(o_ref.dtype)

    ins = [a, b] + ([] if add is None else [add])
    specs = [a_spec, b_spec] + ([] if add is None else [add_spec])
    return pl.pallas_call(
        body, name=name, grid=grid, in_specs=specs, out_specs=out_spec, out_shape=out_shape,
        compiler_params=_params(("parallel",) * len(grid)),
    )(*ins)


def _rope_tables(pos_col, inv_freq_lanes):
    T = pos_col.shape[0]
    tm = min(512, T)

    def body(pos_ref, f_ref, cos_ref, sin_ref):
        ang = pos_ref[...].astype(F32) * f_ref[...]
        cos_ref[...] = jnp.cos(ang)
        sin_ref[...] = jnp.sin(ang)

    return pl.pallas_call(
        body, name="rope_tables", grid=(T // tm,),
        in_specs=[pl.BlockSpec((tm, 1), lambda i: (i, 0)), pl.BlockSpec((1, LANES), lambda i: (0, 0))],
        out_specs=[pl.BlockSpec((tm, LANES), lambda i: (i, 0))] * 2,
        out_shape=[jax.ShapeDtypeStruct((T, LANES), F32)] * 2,
        compiler_params=_params(("parallel",)),
    )(pos_col, inv_freq_lanes)


def _lane_iota(shape):
    return lax.broadcasted_iota(jnp.int32, shape, len(shape) - 1)


def _rot_half(t):
    first = (_lane_iota(t.shape) % HEAD_DIM) < (HEAD_DIM // 2)
    return jnp.where(first, -pltpu.roll(t, LANES - HEAD_DIM // 2, axis=1), pltpu.roll(t, HEAD_DIM // 2, axis=1))


def _rope(t, cos, sin):
    return t * cos + _rot_half(t) * sin


def _rope_transpose(d, cos, sin):
    return d * cos - _rot_half(d * sin)


_IN_DTYPES = (F32, F32, BF16, BF16, BF16, BF16, F32)


def _in_proj(x, w_in_b):
    T = x.shape[0]
    tm = min(256, T)
    offs = np.cumsum((0,) + IN_WIDTHS)

    def body(x_ref, w_ref, xb_ref, *outs):
        xb = x_ref[...].astype(BF16)
        xb_ref[...] = xb
        for o_ref, a, b in zip(outs, offs[:-1], offs[1:]):
            o_ref[...] = _dot(xb, w_ref[:, a:b]).astype(o_ref.dtype)

    row = lambda n: pl.BlockSpec((tm, n), lambda i: (i, 0))
    return pl.pallas_call(
        body, name="in_proj", grid=(T // tm,),
        in_specs=[row(D_MODEL), pl.BlockSpec((D_MODEL, IN_TOTAL), lambda i: (0, 0))],
        out_specs=[row(D_MODEL)] + [row(n) for n in IN_WIDTHS],
        out_shape=[jax.ShapeDtypeStruct((T, D_MODEL), BF16)] + [jax.ShapeDtypeStruct((T, n), dt) for n, dt in zip(IN_WIDTHS, _IN_DTYPES)],
        compiler_params=_params(("parallel",)),
    )(x, w_in_b)


def _swa_specs(T):
    blk = WINDOW
    cur = lambda n: pl.BlockSpec((blk, n), lambda i: (i, 0))
    prev = lambda n: pl.BlockSpec((blk, n), lambda i: (jnp.maximum(i - 1, 0), 0))
    return blk, cur, prev


def _swa_window(i, kp, kc, vp, vc, cp, cc, sp, sc):
    kwin = jnp.concatenate([_rope(kp, cp, sp), _rope(kc, cc, sc)], axis=0)
    vwin = jnp.concatenate([vp, vc], axis=0)
    lane = _lane_iota(kwin.shape)
    low = lane < HEAD_DIM
    ks, vs = [], []
    for g in range(SWA_KV_HEADS):
        k0 = jnp.where(low, kwin if g == 0 else pltpu.roll(kwin, HEAD_DIM, axis=1), 0.0)
        v0 = jnp.where(low, vwin if g == 0 else pltpu.roll(vwin, HEAD_DIM, axis=1), 0.0)
        ks.append((k0, pltpu.roll(k0, HEAD_DIM, axis=1)))
        vs.append((v0, pltpu.roll(v0, HEAD_DIM, axis=1)))
    blk = WINDOW
    r = lax.broadcasted_iota(jnp.int32, (blk, 2 * blk), 0)
    c = lax.broadcasted_iota(jnp.int32, (blk, 2 * blk), 1)
    rel = blk + r - c
    valid = (rel >= 0) & (rel < WINDOW) & ((c >= blk) | (i > 0))
    return ks, vs, valid


def _swa_probs(qh, kk, valid, sink):
    s = _dot_nt(qh, kk) * ATTN_SCALE
    s = jnp.where(valid, s, -1e30)
    m = jnp.maximum(jnp.max(s, axis=1, keepdims=True), sink)
    p = jnp.where(valid, jnp.exp(s - m), 0.0)
    es = jnp.exp(sink - m)
    den = jnp.sum(p, axis=1, keepdims=True) + es
    return p / den, es / den


def _swa_fwd(qa, ka, va, cos, sin, sinks):
    T = qa.shape[0]
    blk, cur, prev = _swa_specs(T)

    def body(sink_ref, q_ref, kp_ref, kc_ref, vp_ref, vc_ref, cp_ref, cc_ref, sp_ref, sc_ref, o_ref):
        i = pl.program_id(0)
        cc, sc = cc_ref[...], sc_ref[...]
        ks, vs, valid = _swa_window(i, kp_ref[...], kc_ref[...], vp_ref[...].astype(F32), vc_ref[...].astype(F32),
                                    cp_ref[...], cc, sp_ref[...], sc)
        lane = _lane_iota((blk, LANES))
        for pp in range(SWA_Q_HEADS // 2):
            g = pp // (SWA_Q_HEADS // SWA_KV_HEADS // 2)
            qp = _rope(q_ref[:, pp * LANES:(pp + 1) * LANES], cc, sc)
            out = jnp.zeros((blk, LANES), F32)
            for hh in range(2):
                half = (lane >= hh * HEAD_DIM) & (lane < (hh + 1) * HEAD_DIM)
                qh = jnp.where(half, qp, 0.0).astype(BF16)
                probs, _ = _swa_probs(qh, ks[g][hh].astype(BF16), valid, sink_ref[2 * pp + hh])
                out = out + _dot(probs.astype(BF16), vs[g][hh].astype(BF16))
            o_ref[:, pp * LANES:(pp + 1) * LANES] = out.astype(o_ref.dtype)

    return pl.pallas_call(
        body, name="swa_fwd", grid=(T // blk,),
        in_specs=[pl.BlockSpec(memory_space=pltpu.SMEM), cur(SWA_Q_WIDTH), prev(LANES), cur(LANES), prev(LANES), cur(LANES),
                  prev(LANES), cur(LANES), prev(LANES), cur(LANES)],
        out_specs=cur(SWA_Q_WIDTH),
        out_shape=jax.ShapeDtypeStruct((T, SWA_Q_WIDTH), BF16),
        compiler_params=_params(("parallel",)),
    )(sinks, qa, ka, ka, va, va, cos, cos, sin, sin)


def _swa_bwd(qa, ka, va, cos, sin, sinks, dya):
    T = qa.shape[0]
    blk, cur, prev = _swa_specs(T)
    full = lambda n: pl.BlockSpec((T, n), lambda i: (0, 0))

    def body(sink_ref, q_ref, kp_ref, kc_ref, vp_ref, vc_ref, cp_ref, cc_ref, sp_ref, sc_ref, do_ref,
             dq_ref, dk_ref, dv_ref, dsink_ref):
        i = pl.program_id(0)

        @pl.when(i == 0)
        def _():
            dk_ref[...] = jnp.zeros_like(dk_ref)
            dv_ref[...] = jnp.zeros_like(dv_ref)
            dsink_ref[...] = jnp.zeros_like(dsink_ref)

        cp, cc, sp, sc = cp_ref[...], cc_ref[...], sp_ref[...], sc_ref[...]
        ks, vs, valid = _swa_window(i, kp_ref[...], kc_ref[...], vp_ref[...].astype(F32), vc_ref[...].astype(F32), cp, cc, sp, sc)
        lane = _lane_iota((blk, LANES))
        lane1 = _lane_iota((1, LANES))
        dkw = jnp.zeros((2 * blk, LANES), F32)
        dvw = jnp.zeros((2 * blk, LANES), F32)
        dsink = jnp.zeros((1, LANES), F32)
        for pp in range(SWA_Q_HEADS // 2):
            g = pp // (SWA_Q_HEADS // SWA_KV_HEADS // 2)
            qp = _rope(q_ref[:, pp * LANES:(pp + 1) * LANES], cc, sc)
            dop = do_ref[:, pp * LANES:(pp + 1) * LANES]
            dqp = jnp.zeros((blk, LANES), F32)
            for hh in range(2):
                half = (lane >= hh * HEAD_DIM) & (lane < (hh + 1) * HEAD_DIM)
                qh = jnp.where(half, qp, 0.0).astype(BF16)
                doh = jnp.where(half, dop, 0.0).astype(BF16)
                kk = ks[g][hh].astype(BF16)
                vv = vs[g][hh].astype(BF16)
                probs, psink = _swa_probs(qh, kk, valid, sink_ref[2 * pp + hh])
                dp = _dot_nt(doh, vv)
                dsum = jnp.sum(probs * dp, axis=1, keepdims=True)
                ds = (probs * (dp - dsum) * ATTN_SCALE).astype(BF16)
                dsink = dsink + jnp.where(lane1 == 2 * pp + hh, -jnp.sum(psink * dsum), 0.0)
                dqp = dqp + _dot(ds, kk)
                dk_h = _dot_tn(ds, qh)
                dv_h = _dot_tn(probs.astype(BF16), doh)
                if hh != g:
                    dk_h = pltpu.roll(dk_h, HEAD_DIM, axis=1)
                    dv_h = pltpu.roll(dv_h, HEAD_DIM, axis=1)
                dkw = dkw + dk_h
                dvw = dvw + dv_h
            dq_ref[:, pp * LANES:(pp + 1) * LANES] = _rope_transpose(dqp, cc, sc).astype(dq_ref.dtype)
        dsink_ref[...] += dsink
        ip = jnp.maximum(i - 1, 0)
        rows_p = pl.ds(pl.multiple_of(ip * blk, blk), blk)
        rows_c = pl.ds(pl.multiple_of(i * blk, blk), blk)
        dk_ref[rows_p, :] += _rope_transpose(dkw[:blk], cp, sp)
        dv_ref[rows_p, :] += dvw[:blk]
        dk_ref[rows_c, :] += _rope_transpose(dkw[blk:], cc, sc)
        dv_ref[rows_c, :] += dvw[blk:]

    return pl.pallas_call(
        body, name="swa_bwd", grid=(T // blk,),
        in_specs=[pl.BlockSpec(memory_space=pltpu.SMEM), cur(SWA_Q_WIDTH), prev(LANES), cur(LANES), prev(LANES), cur(LANES),
                  prev(LANES), cur(LANES), prev(LANES), cur(LANES), cur(SWA_Q_WIDTH)],
        out_specs=[cur(SWA_Q_WIDTH), full(LANES), full(LANES), pl.BlockSpec((1, LANES), lambda i: (0, 0))],
        out_shape=[jax.ShapeDtypeStruct((T, SWA_Q_WIDTH), BF16), jax.ShapeDtypeStruct((T, LANES), F32),
                   jax.ShapeDtypeStruct((T, LANES), F32), jax.ShapeDtypeStruct((1, LANES), F32)],
        compiler_params=_params(("arbitrary",)),
    )(sinks, qa, ka, ka, va, va, cos, cos, sin, sin, dya)


class _Exchange:
    FLIPS = [(fx, fy, fc) for fx in (0, 1) for fy in (0, 1) for fc in (0, 1) if (fx, fy, fc) != (0, 0, 0)]

    def __init__(self, arrays, kinds):
        self.arrays, self.kinds, self.n = list(arrays), list(kinds), len(arrays)

    def out_shape(self):
        return [jax.ShapeDtypeStruct(a.shape if k == "scatter" else (N_DEV,) + a.shape, a.dtype) for a, k in zip(self.arrays, self.kinds)]

    def scratch(self):
        return [pltpu.SemaphoreType.DMA((self.n, 7)), pltpu.SemaphoreType.DMA((self.n, 7)), pltpu.SemaphoreType.DMA((self.n,))]

    def bind(self, ins, outs, send_sems, recv_sems, local_sems):
        x, y, c = lax.axis_index("x"), lax.axis_index("y"), lax.axis_index("c")
        me = 4 * x + 2 * y + c
        local, remote = [], []
        for a, kind in enumerate(self.kinds):
            mine = ins[a].at[me] if kind == "scatter" else ins[a]
            local.append(pltpu.make_async_copy(mine, outs[a].at[me], local_sems.at[a]))
            for k, (fx, fy, fc) in enumerate(self.FLIPS):
                peer = (x ^ fx, y ^ fy, c ^ fc)
                peer_slot = 4 * peer[0] + 2 * peer[1] + peer[2]
                src = ins[a].at[peer_slot] if kind == "scatter" else ins[a]
                sems = dict(send_sem=send_sems.at[a, k], recv_sem=recv_sems.at[a, k], device_id=peer, device_id_type=MESH)
                remote.append((pltpu.make_async_remote_copy(src_ref=src, dst_ref=outs[a].at[me], **sems),
                               pltpu.make_async_remote_copy(src_ref=src, dst_ref=outs[a].at[peer_slot], **sems)))

        def start():
            for cp in local:
                cp.start()
            for send, _ in remote:
                send.start()

        def wait():
            for send, arrival in remote:
                arrival.wait_recv()
                send.wait_send()
            for cp in local:
                cp.wait()

        return start, wait


def _hosted_call(body, name, grid, exchange, *, in_specs, out_specs, out_shape, semantics, args):
    if exchange is None:
        outs = pl.pallas_call(body, name=name, grid=grid, in_specs=in_specs, out_specs=out_specs, out_shape=out_shape,
                              compiler_params=_params(semantics))(*args)
        return outs, []
    n, n_in, n_out = exchange.n, len(in_specs), len(out_specs)

    def hosted(*refs):
        ins, rest = refs[:n_in], refs[n_in:]
        ex_ins, rest = rest[:n], rest[n:]
        outs, rest = rest[:n_out], rest[n_out:]
        ex_outs, sems = rest[:n], rest[n:]
        start, wait = exchange.bind(ex_ins, ex_outs, *sems)
        ids = [pl.program_id(d) for d in range(len(grid))]
        first = functools.reduce(jnp.logical_and, [i == 0 for i in ids])
        last = functools.reduce(jnp.logical_and, [i == g - 1 for i, g in zip(ids, grid)])
        pl.when(first)(start)
        body(*ins, *outs)
        pl.when(last)(wait)

    res = pl.pallas_call(
        hosted, name=name, grid=grid, in_specs=list(in_specs) + [ANY] * n, out_specs=list(out_specs) + [ANY] * n,
        out_shape=list(out_shape) + exchange.out_shape(), scratch_shapes=exchange.scratch(),
        compiler_params=_params(("arbitrary",) * len(grid)),
    )(*args, *exchange.arrays)
    return res[:n_out], res[n_out:]


SOFTPLUS_LINEAR_FROM = 30.0


def _sb_scores(qm, k, valid):
    z = _dot_nt(qm, k)
    sp = jnp.where(z > SOFTPLUS_LINEAR_FROM, z, jnp.log(1.0 + jnp.exp(z)))
    log_beta = z - sp
    if valid is not None:
        sp = jnp.where(valid, sp, 0.0)
    return sp, log_beta


def _tri2(B, cmp):
    r = lax.broadcasted_iota(jnp.int32, (2 * B, B), 0) % B
    c = lax.broadcasted_iota(jnp.int32, (2 * B, B), 1)
    return cmp(r, c).astype(BF16)


def _tri_sum(v, tri2):
    hi, lo = _split_bf16(v)
    return _dot(jnp.concatenate([hi, lo], axis=1), tri2)


def _head_masks(x):
    low = _lane_iota(x.shape) < HEAD_DIM
    zero = jnp.zeros((), x.dtype)
    return jnp.where(low, x, zero), jnp.where(low, zero, x)


def _strictly_below(B):
    r = lax.broadcasted_iota(jnp.int32, (B, B), 0)
    c = lax.broadcasted_iota(jnp.int32, (B, B), 1)
    return c < r


def _sb_fwd(qb, kb, vb, exchange=None):
    T = qb.shape[0]
    B = min(SB_BLOCK, T)
    assert T // B <= HEAD_DIM
    n_pairs = SB_HEADS // 2

    W = SB_PAIRS * LANES

    def body(q_ref, k_ref, v_ref, o_ref, carry_ref):
        i = pl.program_id(1)
        lane = _lane_iota((1, LANES))
        upper2 = _tri2(B, lambda r, c: r > c)
        qms = [_head_masks(q_ref[:, p * LANES:(p + 1) * LANES] * ATTN_SCALE) for p in range(SB_PAIRS)]

        def pair_block(p, j, rows, state, valid):
            cs, acc, cm = state
            cols = slice(p * LANES, (p + 1) * LANES)
            k = k_ref[rows, cols]
            vms = _head_masks(v_ref[rows, cols])
            probs, new_cs = [], []
            for hh in range(2):
                sp, lb = _sb_scores(qms[p][hh], k, valid)
                cm = jnp.where(lane == hh * HEAD_DIM + j, cs[hh], cm)
                a = jnp.exp(lb - (cs[hh] + _tri_sum(sp, upper2)))
                if valid is not None:
                    a = jnp.where(valid, a, 0.0)
                probs.append(a.astype(BF16))
                new_cs.append(cs[hh] + jnp.sum(sp, axis=1, keepdims=True))
            acc = acc + _dot(jnp.concatenate(probs, axis=1), jnp.concatenate(vms, axis=0))
            return tuple(new_cs), acc, cm

        def block(j, states, valid):
            rows = pl.ds(pl.multiple_of(j * B, B), B)
            return tuple(pair_block(p, j, rows, states[p], valid) for p in range(SB_PAIRS))

        zero = jnp.zeros((B, 1), F32)
        start = ((zero, zero), jnp.zeros((B, LANES), F32), jnp.zeros((B, LANES), F32))
        states = block(i, (start,) * SB_PAIRS, _strictly_below(B))
        states = lax.fori_loop(0, i, lambda jj, s: block(i - 1 - jj, s, None), states)
        for p, (_, acc, cm) in enumerate(states):
            o_ref[:, p * LANES:(p + 1) * LANES] = acc.astype(o_ref.dtype)
            carry_ref[:, p * LANES:(p + 1) * LANES] = cm

    return _hosted_call(
        body, "sb_fwd", (n_pairs // SB_PAIRS, T // B), exchange,
        in_specs=[pl.BlockSpec((B, W), lambda p, i: (i, p)), pl.BlockSpec((T, W), lambda p, i: (0, p)),
                  pl.BlockSpec((T, W), lambda p, i: (0, p))],
        out_specs=[pl.BlockSpec((B, W), lambda p, i: (i, p))] * 2,
        out_shape=[jax.ShapeDtypeStruct((T, SB_WIDTH), BF16), jax.ShapeDtypeStruct((T, SB_WIDTH), F32)],
        semantics=("parallel", "parallel"), args=(qb, kb, vb))


def _sb_bwd(qb, kb, vb, carries, dyb, exchange=None):
    T = qb.shape[0]
    B = min(SB_BLOCK, T)
    n_pairs = SB_HEADS // 2

    def body(q_ref, k_ref, v_ref, carry_ref, do_ref, dq_ref, dk_ref, dv_ref):
        i = pl.program_id(1)

        @pl.when(i == 0)
        def _():
            dk_ref[...] = jnp.zeros_like(dk_ref)
            dv_ref[...] = jnp.zeros_like(dv_ref)

        lane = _lane_iota((1, LANES))
        upper2 = _tri2(B, lambda r, c: r > c)
        lower = _tri2(B, lambda r, c: r < c)[:B]
        pair_cols = [slice(p * LANES, (p + 1) * LANES) for p in range(SB_PAIRS)]
        qms = [_head_masks(q_ref[:, cols] * ATTN_SCALE) for cols in pair_cols]
        doms = [_head_masks(do_ref[:, cols]) for cols in pair_cols]
        q2s, do2s = [jnp.concatenate(m, axis=0) for m in qms], [jnp.concatenate(m, axis=0) for m in doms]
        cms = [carry_ref[:, cols] for cols in pair_cols]

        def pair_block(p, j, rows, state, valid):
            cgs, dq = state
            k = k_ref[rows, pair_cols[p]]
            v = v_ref[rows, pair_cols[p]]
            dzs, probs, new_cgs = [], [], []
            for hh in range(2):
                sp, lb = _sb_scores(qms[p][hh], k, valid)
                c = jnp.sum(jnp.where(lane == hh * HEAD_DIM + j, cms[p], 0.0), axis=1, keepdims=True)
                a = jnp.exp(lb - (c + _tri_sum(sp, upper2)))
                if valid is not None:
                    a = jnp.where(valid, a, 0.0)
                g = a * _dot_nt(doms[p][hh], v)
                gpre = cgs[hh] + _dot(g.astype(BF16), lower)
                dz = g - jnp.exp(lb) * (g + gpre)
                if valid is not None:
                    dz = jnp.where(valid, dz, 0.0)
                dzs.append(dz.astype(BF16))
                probs.append(a.astype(BF16))
                new_cgs.append(cgs[hh] + jnp.sum(g, axis=1, keepdims=True))
            dq = dq + _dot(jnp.concatenate(dzs, axis=1), jnp.concatenate(_head_masks(k), axis=0))
            dk_ref[rows, pair_cols[p]] += _dot_tn(jnp.concatenate(dzs, axis=0), q2s[p])
            dv_ref[rows, pair_cols[p]] += _dot_tn(jnp.concatenate(probs, axis=0), do2s[p])
            return tuple(new_cgs), dq

        def block(j, states, valid):
            rows = pl.ds(pl.multiple_of(j * B, B), B)
            return tuple(pair_block(p, j, rows, states[p], valid) for p in range(SB_PAIRS))

        zero = jnp.zeros((B, 1), F32)
        states = lax.fori_loop(0, i, lambda j, s: block(j, s, None), (((zero, zero), jnp.zeros((B, LANES), F32)),) * SB_PAIRS)
        states = block(i, states, _strictly_below(B))
        for cols, (_, dq) in zip(pair_cols, states):
            dq_ref[:, cols] = (dq * ATTN_SCALE).astype(dq_ref.dtype)

    W = SB_PAIRS * LANES
    blk = pl.BlockSpec((B, W), lambda p, i: (i, p))
    full = pl.BlockSpec((T, W), lambda p, i: (0, p))
    return _hosted_call(
        body, "sb_bwd", (n_pairs // SB_PAIRS, T // B), exchange,
        in_specs=[blk, full, full, blk, blk],
        out_specs=[blk, full, full],
        out_shape=[jax.ShapeDtypeStruct((T, SB_WIDTH), BF16), jax.ShapeDtypeStruct((T, SB_WIDTH), F32),
                   jax.ShapeDtypeStruct((T, SB_WIDTH), F32)],
        semantics=("parallel", "arbitrary"), args=(qb, kb, vb, carries, dyb))


def _ln_stats(u):
    mu = jnp.mean(u, axis=-1, keepdims=True)
    xc = u - mu
    var = jnp.mean(xc * xc, axis=-1, keepdims=True)
    rstd = lax.rsqrt(var + LN_EPS)
    return xc * rstd, rstd


def _ln_bwd(dy, xhat, rstd, g):
    dxh = dy * g
    return rstd * (dxh - jnp.mean(dxh, axis=-1, keepdims=True) - xhat * jnp.mean(dxh * xhat, axis=-1, keepdims=True))


def _gates(gl_ref, bg_ref):
    ga = jax.nn.sigmoid(gl_ref[:, :D_MODEL] + bg_ref[:, :D_MODEL])
    gb = jax.nn.sigmoid(gl_ref[:, D_MODEL:] + bg_ref[:, D_MODEL:])
    return ga, gb


def _mix_fwd(ya, yb, gl, x, wa, wb, wo, b_gate, ln1_g, ln1_b):
    T = x.shape[0]
    tm = min(256, T)

    def body(ya_ref, yb_ref, gl_ref, x_ref, wa_ref, wb_ref, wo_ref, bg_ref, g_ref, b_ref, h_ref, u_ref, x1_ref):
        ga, gb = _gates(gl_ref, bg_ref)
        h = (ga * _dot(ya_ref[...], wa_ref[...]) + gb * _dot(yb_ref[...], wb_ref[...])).astype(BF16)
        h_ref[...] = h
        u = ALPHA * x_ref[...] + _dot(h, wo_ref[...])
        u_ref[...] = u
        xhat, _ = _ln_stats(u)
        x1_ref[...] = (xhat * g_ref[...] + b_ref[...]).astype(BF16)

    row = lambda n: pl.BlockSpec((tm, n), lambda i: (i, 0))
    const = lambda r, n: pl.BlockSpec((r, n), lambda i: (0, 0))
    return pl.pallas_call(
        body, name="mix_fwd", grid=(T // tm,),
        in_specs=[row(SWA_Q_WIDTH), row(SB_WIDTH), row(GATE_WIDTH), row(D_MODEL), const(SWA_Q_WIDTH, D_MODEL), const(SB_WIDTH, D_MODEL),
                  const(D_MODEL, D_MODEL), const(1, GATE_WIDTH), const(1, D_MODEL), const(1, D_MODEL)],
        out_specs=[row(D_MODEL)] * 3,
        out_shape=[jax.ShapeDtypeStruct((T, D_MODEL), BF16), jax.ShapeDtypeStruct((T, D_MODEL), F32), jax.ShapeDtypeStruct((T, D_MODEL), BF16)],
        compiler_params=_params(("parallel",)),
    )(ya, yb, gl, x, wa, wb, wo, b_gate, ln1_g, ln1_b)


def _mix_bwd(du1, ya, yb, gl, wa, wb, wo, b_gate):
    T = du1.shape[0]
    tm = min(256, T)

    def body(du_ref, ya_ref, yb_ref, gl_ref, wa_ref, wb_ref, wo_ref, bg_ref, dya_ref, dyb_ref, dgl_ref, dta_ref, dtb_ref, dbg_ref):
        @pl.when(pl.program_id(0) == 0)
        def _():
            dbg_ref[...] = jnp.zeros_like(dbg_ref)

        dh = _dot_nt(du_ref[...].astype(BF16), wo_ref[...])
        ga, gb = _gates(gl_ref, bg_ref)
        for gate, y_ref, w_ref, dy_ref, dt_ref, lo in ((ga, ya_ref, wa_ref, dya_ref, dta_ref, 0), (gb, yb_ref, wb_ref, dyb_ref, dtb_ref, D_MODEL)):
            t = _dot(y_ref[...], w_ref[...])
            dlogit = dh * t * gate * (1.0 - gate)
            dgl_ref[:, lo:lo + D_MODEL] = dlogit.astype(BF16)
            dbg_ref[:, lo:lo + D_MODEL] += jnp.sum(dlogit, axis=0, keepdims=True)
            dt = (dh * gate).astype(BF16)
            dt_ref[...] = dt
            dy_ref[...] = _dot_nt(dt, w_ref[...]).astype(BF16)

    row = lambda n: pl.BlockSpec((tm, n), lambda i: (i, 0))
    const = lambda r, n: pl.BlockSpec((r, n), lambda i: (0, 0))
    sds = lambda n, dt: jax.ShapeDtypeStruct((T, n), dt)
    return pl.pallas_call(
        body, name="mix_bwd", grid=(T // tm,),
        in_specs=[row(D_MODEL), row(SWA_Q_WIDTH), row(SB_WIDTH), row(GATE_WIDTH), const(SWA_Q_WIDTH, D_MODEL), const(SB_WIDTH, D_MODEL),
                  const(D_MODEL, D_MODEL), const(1, GATE_WIDTH)],
        out_specs=[row(SWA_Q_WIDTH), row(SB_WIDTH), row(GATE_WIDTH), row(D_MODEL), row(D_MODEL), const(1, GATE_WIDTH)],
        out_shape=[sds(SWA_Q_WIDTH, BF16), sds(SB_WIDTH, BF16), sds(GATE_WIDTH, BF16), sds(D_MODEL, BF16), sds(D_MODEL, BF16),
                   jax.ShapeDtypeStruct((1, GATE_WIDTH), F32)],
        compiler_params=_params(("arbitrary",)),
    )(du1, ya, yb, gl, wa, wb, wo, b_gate)


CONV_COLS = LANES


def _shift_down(v, k):
    row = lax.broadcasted_iota(jnp.int32, v.shape, 0)
    return jnp.where(row >= k, pltpu.roll(v, k, axis=0), 0.0)


def _shift_up(v, k):
    n = v.shape[0]
    row = lax.broadcasted_iota(jnp.int32, v.shape, 0)
    return jnp.where(row < n - k, pltpu.roll(v, n - k, axis=0), 0.0)


def _conv(pv, w_ref, b_ref):
    return w_ref[0:1, :] * _shift_down(pv, 2) + w_ref[1:2, :] * _shift_down(pv, 1) + w_ref[2:3, :] * pv + b_ref[...]


def _conv_specs(T):
    nb = D_FF // CONV_COLS
    pair = pl.BlockSpec((2, T, CONV_COLS), lambda j: (0, 0, j))
    gate = lambda r: pl.BlockSpec((r, CONV_COLS), lambda j: (0, j))
    up = lambda r: pl.BlockSpec((r, CONV_COLS), lambda j: (0, j + nb))
    return nb, pair, gate, up


def _conv_glu_fwd(p3, conv_w, conv_b):
    T = p3.shape[1]
    nb, pair, gate, up = _conv_specs(T)

    def body(p_ref, wg_ref, wu_ref, bg_ref, bu_ref, s_ref):
        ag = _conv(p_ref[0], wg_ref, bg_ref)
        au = _conv(p_ref[1], wu_ref, bu_ref)
        s_ref[...] = (ag * jax.nn.sigmoid(ag) * au).astype(BF16)

    return pl.pallas_call(
        body, name="conv_glu_fwd", grid=(nb,),
        in_specs=[pair, gate(3), up(3), gate(1), up(1)],
        out_specs=pl.BlockSpec((T, CONV_COLS), lambda j: (0, j)),
        out_shape=jax.ShapeDtypeStruct((T, D_FF), BF16),
        compiler_params=_params(("parallel",)),
    )(p3, conv_w, conv_w, conv_b, conv_b)


def _conv_glu_bwd(p3, ds, conv_w, conv_b):
    T = p3.shape[1]
    nb, pair, gate, up = _conv_specs(T)

    def body(p_ref, ds_ref, wg_ref, wu_ref, bg_ref, bu_ref, dp_ref, dwg_ref, dwu_ref, dbg_ref, dbu_ref):
        pg, pu = p_ref[0], p_ref[1]
        ag = _conv(pg, wg_ref, bg_ref)
        au = _conv(pu, wu_ref, bu_ref)
        sg = jax.nn.sigmoid(ag)
        d = ds_ref[...]
        dau = d * ag * sg
        dag = d * au * (sg * (1.0 + ag * (1.0 - sg)))
        for half, (da, pv, w_ref, dw_ref, db_ref) in enumerate(((dag, pg, wg_ref, dwg_ref, dbg_ref), (dau, pu, wu_ref, dwu_ref, dbu_ref))):
            db_ref[...] = jnp.sum(da, axis=0, keepdims=True)
            dw_ref[0:1, :] = jnp.sum(da * _shift_down(pv, 2), axis=0, keepdims=True)
            dw_ref[1:2, :] = jnp.sum(da * _shift_down(pv, 1), axis=0, keepdims=True)
            dw_ref[2:3, :] = jnp.sum(da * pv, axis=0, keepdims=True)
            dp = w_ref[2:3, :] * da + w_ref[1:2, :] * _shift_up(da, 1) + w_ref[0:1, :] * _shift_up(da, 2)
            dp_ref[half] = dp.astype(BF16)

    col = lambda r: pl.BlockSpec((r, CONV_COLS), lambda j: (0, j))
    return pl.pallas_call(
        body, name="conv_glu_bwd", grid=(nb,),
        in_specs=[pair, col(T), gate(3), up(3), gate(1), up(1)],
        out_specs=[pair, col(3), col(3), col(1), col(1)],
        out_shape=[jax.ShapeDtypeStruct((2, T, D_FF), BF16), jax.ShapeDtypeStruct((3, D_FF), F32), jax.ShapeDtypeStruct((3, D_FF), F32),
                   jax.ShapeDtypeStruct((1, D_FF), F32), jax.ShapeDtypeStruct((1, D_FF), F32)],
        compiler_params=_params(("parallel",)),
    )(p3, ds, conv_w, conv_w, conv_b, conv_b)


def _ffn_down_loss(s, w_down, u1, ln1_g, ln1_b, ln2_g, ln2_b, target):
    T = u1.shape[0]
    tm = min(256, T)

    def body(s_ref, w_ref, u1_ref, g1_ref, b1_ref, g2_ref, b2_ref, t_ref, du_ref, dub_ref, dg_ref, db_ref, loss_ref):
        @pl.when(pl.program_id(0) == 0)
        def _():
            dg_ref[...] = jnp.zeros_like(dg_ref)
            db_ref[...] = jnp.zeros_like(db_ref)
            loss_ref[...] = jnp.zeros_like(loss_ref)

        xh1, _ = _ln_stats(u1_ref[...])
        x1 = xh1 * g1_ref[...] + b1_ref[...]
        u2 = ALPHA * x1 + _dot(s_ref[...], w_ref[...])
        xh2, rstd2 = _ln_stats(u2)
        err = xh2 * g2_ref[...] + b2_ref[...] - t_ref[...]
        per_token = jnp.mean(err * err, axis=-1, keepdims=True)
        loss_ref[...] += 0.5 * jnp.sum(per_token, axis=0, keepdims=True)
        dy = err * (1.0 / D_MODEL)
        dg_ref[...] += jnp.sum(dy * xh2, axis=0, keepdims=True)
        db_ref[...] += jnp.sum(dy, axis=0, keepdims=True)
        du2 = _ln_bwd(dy, xh2, rstd2, g2_ref[...])
        du_ref[...] = du2
        dub_ref[...] = du2.astype(BF16)

    row = lambda n: pl.BlockSpec((tm, n), lambda i: (i, 0))
    const = lambda r, n: pl.BlockSpec((r, n), lambda i: (0, 0))
    vec = const(1, D_MODEL)
    return pl.pallas_call(
        body, name="ffn_down_loss", grid=(T // tm,),
        in_specs=[row(D_FF), const(D_FF, D_MODEL), row(D_MODEL), vec, vec, vec, vec, row(D_MODEL)],
        out_specs=[row(D_MODEL), row(D_MODEL), vec, vec, const(1, LANES)],
        out_shape=[jax.ShapeDtypeStruct((T, D_MODEL), F32), jax.ShapeDtypeStruct((T, D_MODEL), BF16), jax.ShapeDtypeStruct((1, D_MODEL), F32),
                   jax.ShapeDtypeStruct((1, D_MODEL), F32), jax.ShapeDtypeStruct((1, LANES), F32)],
        compiler_params=_params(("arbitrary",)),
    )(s, w_down, u1, ln1_g, ln1_b, ln2_g, ln2_b, target)


def _ffn_up_bwd_ln1(dp3, w_up, du2, u1, ln1_g):
    T = u1.shape[0]
    tm = min(256, T)

    def body(dp_ref, w_ref, du2_ref, u1_ref, g_ref, du_ref, dub_ref, dg_ref, db_ref):
        @pl.when(pl.program_id(0) == 0)
        def _():
            dg_ref[...] = jnp.zeros_like(dg_ref)
            db_ref[...] = jnp.zeros_like(db_ref)

        dx1 = _dot_nt(dp_ref[0], w_ref[:, :D_FF]) + _dot_nt(dp_ref[1], w_ref[:, D_FF:]) + ALPHA * du2_ref[...]
        xh, rstd = _ln_stats(u1_ref[...])
        dg_ref[...] += jnp.sum(dx1 * xh, axis=0, keepdims=True)
        db_ref[...] += jnp.sum(dx1, axis=0, keepdims=True)
        du1 = _ln_bwd(dx1, xh, rstd, g_ref[...])
        du_ref[...] = du1
        dub_ref[...] = du1.astype(BF16)

    row = lambda n: pl.BlockSpec((tm, n), lambda i: (i, 0))
    const = lambda r, n: pl.BlockSpec((r, n), lambda i: (0, 0))
    vec = const(1, D_MODEL)
    return pl.pallas_call(
        body, name="ffn_up_bwd_ln1", grid=(T // tm,),
        in_specs=[pl.BlockSpec((2, tm, D_FF), lambda i: (0, i, 0)), const(D_MODEL, 2 * D_FF), row(D_MODEL), row(D_MODEL), vec],
        out_specs=[row(D_MODEL), row(D_MODEL), vec, vec],
        out_shape=[jax.ShapeDtypeStruct((T, D_MODEL), F32), jax.ShapeDtypeStruct((T, D_MODEL), BF16), jax.ShapeDtypeStruct((1, D_MODEL), F32),
                   jax.ShapeDtypeStruct((1, D_MODEL), F32)],
        compiler_params=_params(("arbitrary",)),
    )(dp3, w_up, du2, u1, ln1_g)


def _local_step(x, positions, w_in, b_gate, sinks, ln1_g, ln1_b, conv_b, ln2_g, ln2_b, target, later_weights,
                early_exchange=None, tail_exchange=None):
    T = x.shape[0]
    inv_freq = 1.0 / (ROPE_THETA ** (jnp.arange(0, HEAD_DIM, 2, dtype=F32) / HEAD_DIM))
    cos, sin = _rope_tables(positions.reshape(T, 1), jnp.tile(inv_freq, LANES // (HEAD_DIM // 2)).reshape(1, LANES))

    xb, qa, ka, va, qb, kb, vb, gl = _in_proj(x, w_in)
    ya = _swa_fwd(qa, ka, va, cos, sin, sinks)
    if isinstance(later_weights, tuple):
        exchange, finish = later_weights
        (yb, carries), arrived = _sb_fwd(qb, kb, vb, exchange)
        later_weights = finish(arrived)
    else:
        (yb, carries), _ = _sb_fwd(qb, kb, vb)
    wa, wb, wo, w_up, conv_w, w_down = later_weights
    h, u1, x1 = _mix_fwd(ya, yb, gl, x, wa, wb, wo, b_gate, ln1_g, ln1_b)

    ff_tn = D_FF // 2
    nff = D_FF // ff_tn
    tm = min(512, T)
    p3 = _matmul(x1, w_up, kind="nn", name="ffn_up", grid=(T // tm, 2 * nff),
                 a_spec=pl.BlockSpec((tm, D_MODEL), lambda i, j: (i, 0)), b_spec=pl.BlockSpec((D_MODEL, ff_tn), lambda i, j: (0, j)),
                 out_spec=pl.BlockSpec((None, tm, ff_tn), lambda i, j: (j // nff, i, j % nff)),
                 out_shape=jax.ShapeDtypeStruct((2, T, D_FF), F32))
    s = _conv_glu_fwd(p3, conv_w, conv_b)
    du2, du2b, dln2_g, dln2_b, loss = _ffn_down_loss(s, w_down, u1, ln1_g, ln1_b, ln2_g, ln2_b, target)

    ds = _matmul(du2b, w_down, kind="nt", name="ffn_down_bwd", grid=(T // tm, nff),
                 a_spec=pl.BlockSpec((tm, D_MODEL), lambda i, j: (i, 0)), b_spec=pl.BlockSpec((ff_tn, D_MODEL), lambda i, j: (j, 0)),
                 out_spec=pl.BlockSpec((tm, ff_tn), lambda i, j: (i, j)), out_shape=jax.ShapeDtypeStruct((T, D_FF), F32))
    dp3, dcw_g, dcw_u, dcb_g, dcb_u = _conv_glu_bwd(p3, ds, conv_w, conv_b)
    tk = 256
    dw_down = _matmul(s, du2b, kind="tn", name="dw_down", grid=(D_FF // tk,),
                      a_spec=pl.BlockSpec((T, tk), lambda i: (0, i)), b_spec=pl.BlockSpec((T, D_MODEL), lambda i: (0, 0)),
                      out_spec=pl.BlockSpec((tk, D_MODEL), lambda i: (i, 0)), out_shape=jax.ShapeDtypeStruct((D_FF, D_MODEL), BF16))
    dw_up = _matmul(x1, dp3, kind="tn", name="dw_up", grid=(D_MODEL // 512, 2 * nff),
                    a_spec=pl.BlockSpec((T, 512), lambda i, j: (0, i)), b_spec=pl.BlockSpec((None, T, ff_tn), lambda i, j: (j // nff, 0, j % nff)),
                    out_spec=pl.BlockSpec((512, ff_tn), lambda i, j: (i, j)), out_shape=jax.ShapeDtypeStruct((D_MODEL, 2 * D_FF), BF16))
    du1, du1b, dln1_g, dln1_b = _ffn_up_bwd_ln1(dp3, w_up, du2, u1, ln1_g)
    dya, dyb, dgl, dta, dtb, db_gate = _mix_bwd(du1, ya, yb, gl, wa, wb, wo, b_gate)

    def dw_tn(a, g, name):
        rows, cols = a.shape[1], g.shape[1]
        tn = min(512, cols)
        return _matmul(a, g, kind="tn", name=name, grid=(rows // 512, cols // tn),
                       a_spec=pl.BlockSpec((T, 512), lambda i, j: (0, i)), b_spec=pl.BlockSpec((T, tn), lambda i, j: (0, j)),
                       out_spec=pl.BlockSpec((512, tn), lambda i, j: (i, j)), out_shape=jax.ShapeDtypeStruct((rows, cols), BF16))

    dwa = dw_tn(ya, dta, "dw_branch_a")
    dwb = dw_tn(yb, dtb, "dw_branch_b")
    dwo = dw_tn(h, du1b, "dw_out")

    grads = dict(
        b_gate=db_gate, w_branch_a=dwa, w_branch_b=dwb, w_out=dwo, ln1_g=dln1_g, ln1_b=dln1_b,
        w_up=dw_up, conv_w=jnp.concatenate([dcw_g, dcw_u], axis=1), conv_b=(dcb_g, dcb_u), w_down=dw_down, ln2_g=dln2_g, ln2_b=dln2_b)
    (dqb, dkb, dvb), early_out = _sb_bwd(qb, kb, vb, carries, dyb, early_exchange(grads) if early_exchange else None)
    dqa, dka, dva, grads["sinks"] = _swa_bwd(qa, ka, va, cos, sin, sinks, dya)
    dproj = (dqa, dka, dva, dqb, dkb, dvb, dgl)
    grads["w_in"] = tuple(dw_tn(xb, piece, f"dw_in_{k}") for k, piece in enumerate(dproj))
    grad_x, tail_out = _grad_x(dproj, w_in, du1, tail_exchange(grads, loss) if tail_exchange else None)
    return loss, grad_x, grads, early_out, tail_out


def _grad_x(dproj, w_in, du1, exchange=None):
    T = du1.shape[0]
    tm = min(256, T)
    offs = np.cumsum((0,) + IN_WIDTHS)

    def body(*refs):
        pieces, (w_ref, du_ref, o_ref) = refs[:len(IN_WIDTHS)], refs[len(IN_WIDTHS):]
        acc = ALPHA * du_ref[...]
        for p_ref, a, b in zip(pieces, offs[:-1], offs[1:]):
            acc = acc + _dot_nt(p_ref[...].astype(BF16), w_ref[:, a:b])
        o_ref[...] = acc

    row = lambda n: pl.BlockSpec((tm, n), lambda i: (i, 0))
    (grad_x,), arrived = _hosted_call(
        body, "grad_x", (T // tm,), exchange,
        in_specs=[row(n) for n in IN_WIDTHS] + [pl.BlockSpec((D_MODEL, IN_TOTAL), lambda i: (0, 0)), row(D_MODEL)],
        out_specs=[row(D_MODEL)], out_shape=[jax.ShapeDtypeStruct((T, D_MODEL), F32)], semantics=("parallel",),
        args=(*dproj, w_in, du1))
    return grad_x, arrived


ANY = pl.BlockSpec(memory_space=pl.ANY)


def _all_gather(slabs, name):
    n = len(slabs)

    def body(*refs):
        ins, outs = refs[:n], refs[n:2 * n]
        send_sems, recv_sems, local_sems = refs[2 * n:]
        x, y, c = lax.axis_index("x"), lax.axis_index("y"), lax.axis_index("c")
        me, sibling = (x, y, c), (x, y, 1 - c)
        chips = [(1 - x, y), (x, 1 - y), (1 - x, 1 - y)]

        def slot(pos):
            return 4 * pos[0] + 2 * pos[1] + pos[2]

        def copy(a, k, block, to, from_input=False):
            return pltpu.make_async_remote_copy(
                src_ref=ins[a] if from_input else outs[a].at[slot(block)], dst_ref=outs[a].at[slot(block)],
                send_sem=send_sems.at[a, k], recv_sem=recv_sems.at[a, k], device_id=to, device_id_type=MESH)

        mine = [pltpu.make_async_copy(ins[a], outs[a].at[slot(me)], local_sems.at[a]) for a in range(n)]
        for cp in mine:
            cp.start()
        first = []
        for a in range(n):
            first.append(copy(a, 0, me, sibling, from_input=True))
            first += [copy(a, 1 + j, me, (*chip, c), from_input=True) for j, chip in enumerate(chips)]
        for cp in first:
            cp.start()
        passed = []
        for j, chip in enumerate(chips):
            for a in range(n):
                copy(a, 1 + j, (*chip, c), me).wait_recv()
                fwd = copy(a, 4 + j, (*chip, c), sibling)
                fwd.start()
                passed.append(fwd)
        for a in range(n):
            copy(a, 0, sibling, me).wait_recv()
            for j, chip in enumerate(chips):
                copy(a, 4 + j, (*chip, 1 - c), me).wait_recv()
        for cp in first + passed:
            cp.wait_send()
        for cp in mine:
            cp.wait()

    return pl.pallas_call(
        body, name=name,
        in_specs=[ANY] * n, out_specs=[ANY] * n,
        out_shape=[jax.ShapeDtypeStruct((N_DEV,) + s.shape, s.dtype) for s in slabs],
        scratch_shapes=[pltpu.SemaphoreType.DMA((n, 7)), pltpu.SemaphoreType.DMA((n, 7)), pltpu.SemaphoreType.DMA((n,))],
    )(*slabs)


def _all_to_all(slabs, name):
    n = len(slabs)

    def body(*refs):
        ins, outs = refs[:n], refs[n:2 * n]
        send_sems, recv_sems, local_sems = refs[2 * n:]
        x, y, c = lax.axis_index("x"), lax.axis_index("y"), lax.axis_index("c")
        my_slot = 4 * x + 2 * y + c
        flips = [(fx, fy, fc) for fx in (0, 1) for fy in (0, 1) for fc in (0, 1) if (fx, fy, fc) != (0, 0, 0)]

        def copy(a, k):
            fx, fy, fc = flips[k]
            peer = (x ^ fx, y ^ fy, c ^ fc)
            peer_slot = 4 * peer[0] + 2 * peer[1] + peer[2]
            send = pltpu.make_async_remote_copy(src_ref=ins[a].at[peer_slot], dst_ref=outs[a].at[my_slot], send_sem=send_sems.at[a, k],
                                                recv_sem=recv_sems.at[a, k], device_id=peer, device_id_type=MESH)
            recv = pltpu.make_async_remote_copy(src_ref=ins[a].at[peer_slot], dst_ref=outs[a].at[peer_slot], send_sem=send_sems.at[a, k],
                                                recv_sem=recv_sems.at[a, k], device_id=peer, device_id_type=MESH)
            return send, recv

        mine = [pltpu.make_async_copy(ins[a].at[my_slot], outs[a].at[my_slot], local_sems.at[a]) for a in range(n)]
        for cp in mine:
            cp.start()
        copies = [copy(a, k) for a in range(n) for k in range(len(flips))]
        for send, _ in copies:
            send.start()
        for send, recv in copies:
            recv.wait_recv()
            send.wait_send()
        for cp in mine:
            cp.wait()

    return pl.pallas_call(
        body, name=name,
        in_specs=[ANY] * n, out_specs=[ANY] * n,
        out_shape=[jax.ShapeDtypeStruct(s.shape, s.dtype) for s in slabs],
        scratch_shapes=[pltpu.SemaphoreType.DMA((n, 7)), pltpu.SemaphoreType.DMA((n, 7)), pltpu.SemaphoreType.DMA((n,))],
    )(*slabs)


def _row_tile(rows):
    for cand in range(256, 7, -8):
        if rows % cand == 0:
            return cand
    return rows


def _window(w):
    wp = max(-(-((w * r) % LANES + w) // LANES) for r in range(N_DEV)) * LANES
    assert all((w * r) // LANES * LANES + wp <= N_DEV * w for r in range(N_DEV))
    return wp


def _join_cols(slabs, name):
    _, R, w = slabs.shape
    tr = _row_tile(R)
    wp = _window(w)

    def body(g_ref, o_ref, pad_ref):
        if w % LANES == 0:
            for r in range(N_DEV):
                o_ref[:, w * r:w * (r + 1)] = g_ref[r]
            return
        o_ref[...] = jnp.zeros_like(o_ref)
        pad_ref[...] = jnp.zeros_like(pad_ref)
        for r in range(N_DEV):
            q, s = divmod(w * r, LANES)
            pad_ref[:, :w] = g_ref[r]
            y = pad_ref[...]
            if s:
                y = pltpu.roll(y, s, axis=1)
            o_ref[:, LANES * q:LANES * q + wp] += y

    return pl.pallas_call(
        body, name=name, grid=(R // tr,),
        in_specs=[pl.BlockSpec((N_DEV, tr, w), lambda i: (0, i, 0))], out_specs=pl.BlockSpec((tr, N_DEV * w), lambda i: (i, 0)),
        out_shape=jax.ShapeDtypeStruct((R, N_DEV * w), slabs.dtype), scratch_shapes=[pltpu.VMEM((tr, wp), slabs.dtype)],
        compiler_params=_params(("parallel",)),
    )(slabs)


def _split_cols(pieces, name):
    R = pieces[0].shape[0]
    widths = [p.shape[1] for p in pieces]
    total = sum(widths)
    w = total // N_DEV
    tr = _row_tile(R)
    wp = _window(w)
    offs = np.cumsum([0] + widths)
    dtype = pieces[0].dtype

    def body(*refs):
        ins, (o_ref, full_ref) = refs[:len(pieces)], refs[len(pieces):]
        for p_ref, a, b in zip(ins, offs[:-1], offs[1:]):
            full_ref[:, a:b] = p_ref[...].astype(dtype)
        for r in range(N_DEV):
            q, s = divmod(w * r, LANES)
            y = full_ref[:, LANES * q:LANES * q + wp]
            if s:
                y = pltpu.roll(y, wp - s, axis=1)
            o_ref[r] = y[:, :w]

    return pl.pallas_call(
        body, name=name, grid=(R // tr,),
        in_specs=[pl.BlockSpec((tr, n), lambda i: (i, 0)) for n in widths], out_specs=pl.BlockSpec((N_DEV, tr, w), lambda i: (0, i, 0)),
        out_shape=jax.ShapeDtypeStruct((N_DEV, R, w), dtype), scratch_shapes=[pltpu.VMEM((tr, total), dtype)],
        compiler_params=_params(("parallel",)),
    )(*pieces)


def _adamw(g, w, m, v):
    m_new = ADAM_B1 * m + (1.0 - ADAM_B1) * g
    v_new = ADAM_B2 * v + (1.0 - ADAM_B2) * jnp.square(g)
    m_hat = m_new / (1.0 - ADAM_B1 ** ADAM_STEP)
    v_hat = v_new / (1.0 - ADAM_B2 ** ADAM_STEP)
    return -ADAM_LR * (m_hat / (jnp.sqrt(v_hat) + ADAM_EPS) + ADAM_WD * w), m_new, v_new


def _sum_parts(p_ref):
    g = p_ref[0].astype(F32)
    for d in range(1, N_DEV):
        g = g + p_ref[d].astype(F32)
    return g


def _reduce_adamw(parts, w, m, v, name):
    R, C = w.shape
    tr = _row_tile(R)

    def body(p_ref, w_ref, m_ref, v_ref, g_ref, d_ref, mo_ref, vo_ref):
        g = _sum_parts(p_ref)
        g_ref[...] = g
        d_ref[...], mo_ref[...], vo_ref[...] = _adamw(g, w_ref[...], m_ref[...], v_ref[...])

    row = pl.BlockSpec((tr, C), lambda i: (i, 0))
    return pl.pallas_call(
        body, name=name, grid=(R // tr,),
        in_specs=[pl.BlockSpec((N_DEV, tr, C), lambda i: (0, i, 0)), row, row, row],
        out_specs=[row] * 4, out_shape=[jax.ShapeDtypeStruct((R, C), F32)] * 4,
        compiler_params=_params(("parallel",)),
    )(parts, w, m, v)


def _reduce_adamw_small(parts, ws, ms, vs):
    sizes = [a.shape[1] for a in ws]
    k = len(sizes)
    offs = np.cumsum([0] + [-(-n // LANES) * LANES for n in sizes])

    def body(*refs):
        p_ref, w_refs, m_refs, v_refs = refs[0], refs[1:1 + k], refs[1 + k:1 + 2 * k], refs[1 + 2 * k:1 + 3 * k]
        outs, loss_ref = refs[1 + 3 * k:-1], refs[-1]
        g_all = _sum_parts(p_ref)
        for j, n in enumerate(sizes):
            g = g_all[:, offs[j]:offs[j] + LANES * (-(-n // LANES))][:, :n]
            outs[4 * j][...] = g
            outs[4 * j + 1][...], outs[4 * j + 2][...], outs[4 * j + 3][...] = _adamw(g, w_refs[j][...], m_refs[j][...], v_refs[j][...])
        loss_ref[...] = g_all[:, offs[k]:offs[k] + LANES]

    vm = pl.BlockSpec(memory_space=pltpu.VMEM)
    out_shape = [jax.ShapeDtypeStruct((1, n), F32) for n in sizes for _ in range(4)] + [jax.ShapeDtypeStruct((1, LANES), F32)]
    res = pl.pallas_call(
        body, name="reduce_adamw_replicated", in_specs=[vm] * (1 + 3 * k), out_specs=[vm] * len(out_shape), out_shape=out_shape,
        compiler_params=_params(),
    )(parts, *ws, *ms, *vs)
    return [res[4 * j:4 * j + 4] for j in range(k)], res[-1]


COL_SHARDED = ("w_in", "w_branch_a", "w_branch_b", "w_up", "conv_w")
ROW_SHARDED = ("w_out", "w_down")
SMALL = ("b_gate", "sinks", "ln1_g", "ln1_b", "conv_b", "ln2_g", "ln2_b")
ORDER = ("w_in", "b_gate", "sinks", "w_branch_a", "w_branch_b", "w_out", "ln1_g", "ln1_b", "w_up", "conv_w", "conv_b", "w_down", "ln2_g", "ln2_b")


def _pad_lanes(a):
    pad = (-a.shape[-1]) % LANES
    return a if pad == 0 else jnp.pad(a, ((0, 0), (0, pad)))


def kernel(x, positions, w_in, b_gate, sinks, w_branch_a, w_branch_b, w_out, ln1_g, ln1_b, w_up, conv_w, conv_b, w_down, ln2_g, ln2_b, loss_target, m_w_in, m_b_gate, m_sinks, m_w_branch_a, m_w_branch_b, m_w_out, m_ln1_g, m_ln1_b, m_w_up, m_conv_w, m_conv_b, m_w_down, m_ln2_g, m_ln2_b, v_w_in, v_b_gate, v_sinks, v_w_branch_a, v_w_branch_b, v_w_out, v_ln1_g, v_ln1_b, v_w_up, v_conv_w, v_conv_b, v_w_down, v_ln2_g, v_ln2_b):
    args = dict(locals())
    sharded = COL_SHARDED + ROW_SHARDED
    w = {n: args[n][0] if n in sharded else args[n] for n in ORDER}
    m = {n: args["m_" + n][0] if n in sharded else args["m_" + n] for n in ORDER}
    v = {n: args["v_" + n][0] if n in sharded else args["v_" + n] for n in ORDER}

    travel = {n: (w[n] if n == "conv_w" else w[n].astype(BF16)) for n in sharded}
    (g_in,) = _all_gather([travel["w_in"]], "all_gather_w_in")
    w_in_full = _join_cols(g_in, "join_w_in")
    later = ("w_branch_a", "w_branch_b", "w_out", "w_up", "conv_w", "w_down")

    def join(name, slabs):
        return _join_cols(slabs, "join_" + name) if name in COL_SHARDED else slabs.reshape(-1, slabs.shape[-1])

    def split(name, grad):
        if name in COL_SHARDED:
            return _split_cols(grad if isinstance(grad, tuple) else (grad,), "split_d" + name)
        return grad.reshape((N_DEV, -1, grad.shape[-1]))

    def early_exchange(grads):
        return _Exchange([split(n, grads[n]) for n in later], ["scatter"] * len(later))

    def tail_exchange(grads, loss):
        small_pack = jnp.concatenate(
            [_pad_lanes(p) for n in SMALL for p in (grads[n] if isinstance(grads[n], tuple) else (grads[n],))] + [loss], axis=1)
        return _Exchange([split("w_in", grads["w_in"]), small_pack], ["scatter", "gather"])

    gather_later = _Exchange([travel[n] for n in later], ["gather"] * len(later))
    _, grad_x, _, early_out, (recv_w_in, small_parts) = _local_step(
        x[0], positions[0], w_in_full, w["b_gate"], w["sinks"][0], w["ln1_g"], w["ln1_b"], w["conv_b"], w["ln2_g"], w["ln2_b"], loss_target[0],
        (gather_later, lambda arrived: [join(n, a) for n, a in zip(later, arrived)]), early_exchange, tail_exchange)
    recv = dict(zip(later, early_out), w_in=recv_w_in)

    res = {n: _reduce_adamw(recv[n], w[n], m[n], v[n], "reduce_adamw_" + n) for n in sharded}
    small_res, loss_sum = _reduce_adamw_small(small_parts, [w[n] for n in SMALL], [m[n] for n in SMALL], [v[n] for n in SMALL])
    res.update(zip(SMALL, small_res))
    out = [loss_sum[0, 0], grad_x[None]]
    for k in range(4):
        out += [res[n][k][None] if n in sharded else res[n][k] for n in ORDER]
    return tuple(out)
```

```python
import functools

import jax
import jax.numpy as jnp
import numpy as np
from jax import lax
from jax.experimental import pallas as pl
from jax.experimental.pallas import tpu as pltpu

D_MODEL = 1024
HEAD_DIM = 64
SWA_Q_HEADS = 8
SWA_KV_HEADS = 2
SB_HEADS = 8
WINDOW = 128
ROPE_THETA = 10000.0
D_FF = 2816
LN_EPS = 1e-5
DEPTH = 1
ALPHA = (2.0 * DEPTH) ** 0.25
SWA_Q_WIDTH = SWA_Q_HEADS * HEAD_DIM
SWA_KV_WIDTH = SWA_KV_HEADS * HEAD_DIM
SB_WIDTH = SB_HEADS * HEAD_DIM
GATE_WIDTH = 2 * D_MODEL
IN_WIDTHS = (SWA_Q_WIDTH, SWA_KV_WIDTH, SWA_KV_WIDTH, SB_WIDTH, SB_WIDTH, SB_WIDTH, GATE_WIDTH)
IN_TOTAL = sum(IN_WIDTHS)
ATTN_SCALE = HEAD_DIM ** -0.5

ADAM_LR = 0.001
ADAM_B1 = 0.9
ADAM_B2 = 0.999
ADAM_EPS = 1e-08
ADAM_WD = 0.01
ADAM_STEP = 10

N_DEV = 8
LANES = 128
SB_BLOCK = 256
SB_PAIRS = 4
VMEM_LIMIT = 56 * 1024 * 1024

F32 = jnp.float32
BF16 = jnp.bfloat16
MESH = pl.DeviceIdType.MESH


def _params(sem=None):
    return pltpu.CompilerParams(dimension_semantics=sem, vmem_limit_bytes=VMEM_LIMIT)


def _dot(a, b):
    return jnp.dot(a, b, preferred_element_type=F32)


def _dot_nt(a, b):
    return lax.dot_general(a, b, (((1,), (1,)), ((), ())), preferred_element_type=F32)


def _dot_tn(a, b):
    return lax.dot_general(a, b, (((0,), (0,)), ((), ())), preferred_element_type=F32)


def _split_bf16(v):
    hi = v.astype(BF16)
    lo = (v - hi.astype(F32)).astype(BF16)
    return hi, lo


def _matmul(a, b, *, kind, out_shape, grid, a_spec, b_spec, out_spec, name, add=None, add_spec=None, add_scale=1.0):
    dot = {"nn": _dot, "nt": _dot_nt, "tn": _dot_tn}[kind]

    def body(*refs):
        if add is None:
            a_ref, b_ref, o_ref = refs
        else:
            a_ref, b_ref, add_ref, o_ref = refs
        r = dot(a_ref[...].astype(BF16), b_ref[...].astype(BF16))
        if add is not None:
            r = r + add_scale * add_ref[...]
        o_ref[...] = r.astype(o_ref.dtype)

    ins = [a, b] + ([] if add is None else [add])
    specs = [a_spec, b_spec] + ([] if add is None else [add_spec])
    return pl.pallas_call(
        body, name=name, grid=grid, in_specs=specs, out_specs=out_spec, out_shape=out_shape,
        compiler_params=_params(("parallel",) * len(grid)),
    )(*ins)


def _rope_tables(pos_col, inv_freq_lanes):
    T = pos_col.shape[0]
    tm = min(512, T)

    def body(pos_ref, f_ref, cos_ref, sin_ref):
        ang = pos_ref[...].astype(F32) * f_ref[...]
        cos_ref[...] = jnp.cos(ang)
        sin_ref[...] = jnp.sin(ang)

    return pl.pallas_call(
        body, name="rope_tables", grid=(T // tm,),
        in_specs=[pl.BlockSpec((tm, 1), lambda i: (i, 0)), pl.BlockSpec((1, LANES), lambda i: (0, 0))],
        out_specs=[pl.BlockSpec((tm, LANES), lambda i: (i, 0))] * 2,
        out_shape=[jax.ShapeDtypeStruct((T, LANES), F32)] * 2,
        compiler_params=_params(("parallel",)),
    )(pos_col, inv_freq_lanes)


def _lane_iota(shape):
    return lax.broadcasted_iota(jnp.int32, shape, len(shape) - 1)


def _rot_half(t):
    first = (_lane_iota(t.shape) % HEAD_DIM) < (HEAD_DIM // 2)
    return jnp.where(first, -pltpu.roll(t, LANES - HEAD_DIM // 2, axis=1), pltpu.roll(t, HEAD_DIM // 2, axis=1))


def _rope(t, cos, sin):
    return t * cos + _rot_half(t) * sin


def _rope_transpose(d, cos, sin):
    return d * cos - _rot_half(d * sin)


_IN_DTYPES = (F32, F32, BF16, BF16, BF16, BF16, F32)


def _in_proj(x, w_in_b):
    T = x.shape[0]
    tm = min(256, T)
    offs = np.cumsum((0,) + IN_WIDTHS)

    def body(x_ref, w_ref, xb_ref, *outs):
        xb = x_ref[...].astype(BF16)
        xb_ref[...] = xb
        for o_ref, a, b in zip(outs, offs[:-1], offs[1:]):
            o_ref[...] = _dot(xb, w_ref[:, a:b]).astype(o_ref.dtype)

    row = lambda n: pl.BlockSpec((tm, n), lambda i: (i, 0))
    return pl.pallas_call(
        body, name="in_proj", grid=(T // tm,),
        in_specs=[row(D_MODEL), pl.BlockSpec((D_MODEL, IN_TOTAL), lambda i: (0, 0))],
        out_specs=[row(D_MODEL)] + [row(n) for n in IN_WIDTHS],
        out_shape=[jax.ShapeDtypeStruct((T, D_MODEL), BF16)] + [jax.ShapeDtypeStruct((T, n), dt) for n, dt in zip(IN_WIDTHS, _IN_DTYPES)],
        compiler_params=_params(("parallel",)),
    )(x, w_in_b)


def _swa_specs(T):
    blk = WINDOW
    cur = lambda n: pl.BlockSpec((blk, n), lambda i: (i, 0))
    prev = lambda n: pl.BlockSpec((blk, n), lambda i: (jnp.maximum(i - 1, 0), 0))
    return blk, cur, prev


def _swa_window(i, kp, kc, vp, vc, cp, cc, sp, sc):
    kwin = jnp.concatenate([_rope(kp, cp, sp), _rope(kc, cc, sc)], axis=0)
    vwin = jnp.concatenate([vp, vc], axis=0)
    lane = _lane_iota(kwin.shape)
    low = lane < HEAD_DIM
    ks, vs = [], []
    for g in range(SWA_KV_HEADS):
        k0 = jnp.where(low, kwin if g == 0 else pltpu.roll(kwin, HEAD_DIM, axis=1), 0.0)
        v0 = jnp.where(low, vwin if g == 0 else pltpu.roll(vwin, HEAD_DIM, axis=1), 0.0)
        ks.append((k0, pltpu.roll(k0, HEAD_DIM, axis=1)))
        vs.append((v0, pltpu.roll(v0, HEAD_DIM, axis=1)))
    blk = WINDOW
    r = lax.broadcasted_iota(jnp.int32, (blk, 2 * blk), 0)
    c = lax.broadcasted_iota(jnp.int32, (blk, 2 * blk), 1)
    rel = blk + r - c
    valid = (rel >= 0) & (rel < WINDOW) & ((c >= blk) | (i > 0))
    return ks, vs, valid


def _swa_probs(qh, kk, valid, sink):
    s = _dot_nt(qh, kk) * ATTN_SCALE
    s = jnp.where(valid, s, -1e30)
    m = jnp.maximum(jnp.max(s, axis=1, keepdims=True), sink)
    p = jnp.where(valid, jnp.exp(s - m), 0.0)
    es = jnp.exp(sink - m)
    den = jnp.sum(p, axis=1, keepdims=True) + es
    return p / den, es / den


def _swa_fwd(qa, ka, va, cos, sin, sinks):
    T = qa.shape[0]
    blk, cur, prev = _swa_specs(T)

    def body(sink_ref, q_ref, kp_ref, kc_ref, vp_ref, vc_ref, cp_ref, cc_ref, sp_ref, sc_ref, o_ref):
        i = pl.program_id(0)
        cc, sc = cc_ref[...], sc_ref[...]
        ks, vs, valid = _swa_window(i, kp_ref[...], kc_ref[...], vp_ref[...].astype(F32), vc_ref[...].astype(F32),
                                    cp_ref[...], cc, sp_ref[...], sc)
        lane = _lane_iota((blk, LANES))
        for pp in range(SWA_Q_HEADS // 2):
            g = pp // (SWA_Q_HEADS // SWA_KV_HEADS // 2)
            qp = _rope(q_ref[:, pp * LANES:(pp + 1) * LANES], cc, sc)
            out = jnp.zeros((blk, LANES), F32)
            for hh in range(2):
                half = (lane >= hh * HEAD_DIM) & (lane < (hh + 1) * HEAD_DIM)
                qh = jnp.where(half, qp, 0.0).astype(BF16)
                probs, _ = _swa_probs(qh, ks[g][hh].astype(BF16), valid, sink_ref[2 * pp + hh])
                out = out + _dot(probs.astype(BF16), vs[g][hh].astype(BF16))
            o_ref[:, pp * LANES:(pp + 1) * LANES] = out.astype(o_ref.dtype)

    return pl.pallas_call(
        body, name="swa_fwd", grid=(T // blk,),
        in_specs=[pl.BlockSpec(memory_space=pltpu.SMEM), cur(SWA_Q_WIDTH), prev(LANES), cur(LANES), prev(LANES), cur(LANES),
                  prev(LANES), cur(LANES), prev(LANES), cur(LANES)],
        out_specs=cur(SWA_Q_WIDTH),
        out_shape=jax.ShapeDtypeStruct((T, SWA_Q_WIDTH), BF16),
        compiler_params=_params(("parallel",)),
    )(sinks, qa, ka, ka, va, va, cos, cos, sin, sin)


def _swa_bwd(qa, ka, va, cos, sin, sinks, dya):
    T = qa.shape[0]
    blk, cur, prev = _swa_specs(T)
    full = lambda n: pl.BlockSpec((T, n), lambda i: (0, 0))

    def body(sink_ref, q_ref, kp_ref, kc_ref, vp_ref, vc_ref, cp_ref, cc_ref, sp_ref, sc_ref, do_ref,
             dq_ref, dk_ref, dv_ref, dsink_ref):
        i = pl.program_id(0)

        @pl.when(i == 0)
        def _():
            dk_ref[...] = jnp.zeros_like(dk_ref)
            dv_ref[...] = jnp.zeros_like(dv_ref)
            dsink_ref[...] = jnp.zeros_like(dsink_ref)

        cp, cc, sp, sc = cp_ref[...], cc_ref[...], sp_ref[...], sc_ref[...]
        ks, vs, valid = _swa_window(i, kp_ref[...], kc_ref[...], vp_ref[...].astype(F32), vc_ref[...].astype(F32), cp, cc, sp, sc)
        lane = _lane_iota((blk, LANES))
        lane1 = _lane_iota((1, LANES))
        dkw = jnp.zeros((2 * blk, LANES), F32)
        dvw = jnp.zeros((2 * blk, LANES), F32)
        dsink = jnp.zeros((1, LANES), F32)
        for pp in range(SWA_Q_HEADS // 2):
            g = pp // (SWA_Q_HEADS // SWA_KV_HEADS // 2)
            qp = _rope(q_ref[:, pp * LANES:(pp + 1) * LANES], cc, sc)
            dop = do_ref[:, pp * LANES:(pp + 1) * LANES]
            dqp = jnp.zeros((blk, LANES), F32)
            for hh in range(2):
                half = (lane >= hh * HEAD_DIM) & (lane < (hh + 1) * HEAD_DIM)
                qh = jnp.where(half, qp, 0.0).astype(BF16)
                doh = jnp.where(half, dop, 0.0).astype(BF16)
                kk = ks[g][hh].astype(BF16)
                vv = vs[g][hh].astype(BF16)
                probs, psink = _swa_probs(qh, kk, valid, sink_ref[2 * pp + hh])
                dp = _dot_nt(doh, vv)
                dsum = jnp.sum(probs * dp, axis=1, keepdims=True)
                ds = (probs * (dp - dsum) * ATTN_SCALE).astype(BF16)
                dsink = dsink + jnp.where(lane1 == 2 * pp + hh, -jnp.sum(psink * dsum), 0.0)
                dqp = dqp + _dot(ds, kk)
                dk_h = _dot_tn(ds, qh)
                dv_h = _dot_tn(probs.astype(BF16), doh)
                if hh != g:
                    dk_h = pltpu.roll(dk_h, HEAD_DIM, axis=1)
                    dv_h = pltpu.roll(dv_h, HEAD_DIM, axis=1)
                dkw = dkw + dk_h
                dvw = dvw + dv_h
            dq_ref[:, pp * LANES:(pp + 1) * LANES] = _rope_transpose(dqp, cc, sc).astype(dq_ref.dtype)
        dsink_ref[...] += dsink
        ip = jnp.maximum(i - 1, 0)
        rows_p = pl.ds(pl.multiple_of(ip * blk, blk), blk)
        rows_c = pl.ds(pl.multiple_of(i * blk, blk), blk)
        dk_ref[rows_p, :] += _rope_transpose(dkw[:blk], cp, sp)
        dv_ref[rows_p, :] += dvw[:blk]
        dk_ref[rows_c, :] += _rope_transpose(dkw[blk:], cc, sc)
        dv_ref[rows_c, :] += dvw[blk:]

    return pl.pallas_call(
        body, name="swa_bwd", grid=(T // blk,),
        in_specs=[pl.BlockSpec(memory_space=pltpu.SMEM), cur(SWA_Q_WIDTH), prev(LANES), cur(LANES), prev(LANES), cur(LANES),
                  prev(LANES), cur(LANES), prev(LANES), cur(LANES), cur(SWA_Q_WIDTH)],
        out_specs=[cur(SWA_Q_WIDTH), full(LANES), full(LANES), pl.BlockSpec((1, LANES), lambda i: (0, 0))],
        out_shape=[jax.ShapeDtypeStruct((T, SWA_Q_WIDTH), BF16), jax.ShapeDtypeStruct((T, LANES), F32),
                   jax.ShapeDtypeStruct((T, LANES), F32), jax.ShapeDtypeStruct((1, LANES), F32)],
        compiler_params=_params(("arbitrary",)),
    )(sinks, qa, ka, ka, va, va, cos, cos, sin, sin, dya)


class _Exchange:
    FLIPS = [(fx, fy, fc) for fx in (0, 1) for fy in (0, 1) for fc in (0, 1) if (fx, fy, fc) != (0, 0, 0)]

    def __init__(self, arrays, kinds):
        self.arrays, self.kinds, self.n = list(arrays), list(kinds), len(arrays)

    def out_shape(self):
        return [jax.ShapeDtypeStruct(a.shape if k == "scatter" else (N_DEV,) + a.shape, a.dtype) for a, k in zip(self.arrays, self.kinds)]

    def scratch(self):
        return [pltpu.SemaphoreType.DMA((self.n, 7)), pltpu.SemaphoreType.DMA((self.n, 7)), pltpu.SemaphoreType.DMA((self.n,))]

    def bind(self, ins, outs, send_sems, recv_sems, local_sems):
        x, y, c = lax.axis_index("x"), lax.axis_index("y"), lax.axis_index("c")
        me = 4 * x + 2 * y + c
        local, remote = [], []
        for a, kind in enumerate(self.kinds):
            mine = ins[a].at[me] if kind == "scatter" else ins[a]
            local.append(pltpu.make_async_copy(mine, outs[a].at[me], local_sems.at[a]))
            for k, (fx, fy, fc) in enumerate(self.FLIPS):
                peer = (x ^ fx, y ^ fy, c ^ fc)
                peer_slot = 4 * peer[0] + 2 * peer[1] + peer[2]
                src = ins[a].at[peer_slot] if kind == "scatter" else ins[a]
                sems = dict(send_sem=send_sems.at[a, k], recv_sem=recv_sems.at[a, k], device_id=peer, device_id_type=MESH)
                remote.append((pltpu.make_async_remote_copy(src_ref=src, dst_ref=outs[a].at[me], **sems),
                               pltpu.make_async_remote_copy(src_ref=src, dst_ref=outs[a].at[peer_slot], **sems)))

        def start():
            for cp in local:
                cp.start()
            for send, _ in remote:
                send.start()

        def wait():
            for send, arrival in remote:
                arrival.wait_recv()
                send.wait_send()
            for cp in local:
                cp.wait()

        return start, wait


def _hosted_call(body, name, grid, exchange, *, in_specs, out_specs, out_shape, semantics, args):
    if exchange is None:
        outs = pl.pallas_call(body, name=name, grid=grid, in_specs=in_specs, out_specs=out_specs, out_shape=out_shape,
                              compiler_params=_params(semantics))(*args)
        return outs, []
    n, n_in, n_out = exchange.n, len(in_specs), len(out_specs)

    def hosted(*refs):
        ins, rest = refs[:n_in], refs[n_in:]
        ex_ins, rest = rest[:n], rest[n:]
        outs, rest = rest[:n_out], rest[n_out:]
        ex_outs, sems = rest[:n], rest[n:]
        start, wait = exchange.bind(ex_ins, ex_outs, *sems)
        ids = [pl.program_id(d) for d in range(len(grid))]
        first = functools.reduce(jnp.logical_and, [i == 0 for i in ids])
        last = functools.reduce(jnp.logical_and, [i == g - 1 for i, g in zip(ids, grid)])
        pl.when(first)(start)
        body(*ins, *outs)
        pl.when(last)(wait)

    res = pl.pallas_call(
        hosted, name=name, grid=grid, in_specs=list(in_specs) + [ANY] * n, out_specs=list(out_specs) + [ANY] * n,
        out_shape=list(out_shape) + exchange.out_shape(), scratch_shapes=exchange.scratch(),
        compiler_params=_params(("arbitrary",) * len(grid)),
    )(*args, *exchange.arrays)
    return res[:n_out], res[n_out:]


SOFTPLUS_LINEAR_FROM = 30.0


def _sb_scores(qm, k, valid):
    z = _dot_nt(qm, k)
    sp = jnp.where(z > SOFTPLUS_LINEAR_FROM, z, jnp.log(1.0 + jnp.exp(z)))
    log_beta = z - sp
    if valid is not None:
        sp = jnp.where(valid, sp, 0.0)
    return sp, log_beta


def _tri2(B, cmp):
    r = lax.broadcasted_iota(jnp.int32, (2 * B, B), 0) % B
    c = lax.broadcasted_iota(jnp.int32, (2 * B, B), 1)
    return cmp(r, c).astype(BF16)


def _tri_sum(v, tri2):
    hi, lo = _split_bf16(v)
    return _dot(jnp.concatenate([hi, lo], axis=1), tri2)


def _head_masks(x):
    low = _lane_iota(x.shape) < HEAD_DIM
    zero = jnp.zeros((), x.dtype)
    return jnp.where(low, x, zero), jnp.where(low, zero, x)


def _strictly_below(B):
    r = lax.broadcasted_iota(jnp.int32, (B, B), 0)
    c = lax.broadcasted_iota(jnp.int32, (B, B), 1)
    return c < r


def _sb_fwd(qb, kb, vb, exchange=None):
    T = qb.shape[0]
    B = min(SB_BLOCK, T)
    assert T // B <= HEAD_DIM
    n_pairs = SB_HEADS // 2

    W = SB_PAIRS * LANES

    def body(q_ref, k_ref, v_ref, o_ref, carry_ref):
        i = pl.program_id(1)
        lane = _lane_iota((1, LANES))
        upper2 = _tri2(B, lambda r, c: r > c)
        qms = [_head_masks(q_ref[:, p * LANES:(p + 1) * LANES] * ATTN_SCALE) for p in range(SB_PAIRS)]

        def pair_block(p, j, rows, state, valid):
            cs, acc, cm = state
            cols = slice(p * LANES, (p + 1) * LANES)
            k = k_ref[rows, cols]
            vms = _head_masks(v_ref[rows, cols])
            probs, new_cs = [], []
            for hh in range(2):
                sp, lb = _sb_scores(qms[p][hh], k, valid)
                cm = jnp.where(lane == hh * HEAD_DIM + j, cs[hh], cm)
                a = jnp.exp(lb - (cs[hh] + _tri_sum(sp, upper2)))
                if valid is not None:
                    a = jnp.where(valid, a, 0.0)
                probs.append(a.astype(BF16))
                new_cs.append(cs[hh] + jnp.sum(sp, axis=1, keepdims=True))
            acc = acc + _dot(jnp.concatenate(probs, axis=1), jnp.concatenate(vms, axis=0))
            return tuple(new_cs), acc, cm

        def block(j, states, valid):
            rows = pl.ds(pl.multiple_of(j * B, B), B)
            return tuple(pair_block(p, j, rows, states[p], valid) for p in range(SB_PAIRS))

        zero = jnp.zeros((B, 1), F32)
        start = ((zero, zero), jnp.zeros((B, LANES), F32), jnp.zeros((B, LANES), F32))
        states = block(i, (start,) * SB_PAIRS, _strictly_below(B))
        states = lax.fori_loop(0, i, lambda jj, s: block(i - 1 - jj, s, None), states)
        for p, (_, acc, cm) in enumerate(states):
            o_ref[:, p * LANES:(p + 1) * LANES] = acc.astype(o_ref.dtype)
            carry_ref[:, p * LANES:(p + 1) * LANES] = cm

    return _hosted_call(
        body, "sb_fwd", (n_pairs // SB_PAIRS, T // B), exchange,
        in_specs=[pl.BlockSpec((B, W), lambda p, i: (i, p)), pl.BlockSpec((T, W), lambda p, i: (0, p)),
                  pl.BlockSpec((T, W), lambda p, i: (0, p))],
        out_specs=[pl.BlockSpec((B, W), lambda p, i: (i, p))] * 2,
        out_shape=[jax.ShapeDtypeStruct((T, SB_WIDTH), BF16), jax.ShapeDtypeStruct((T, SB_WIDTH), F32)],
        semantics=("parallel", "parallel"), args=(qb, kb, vb))


def _sb_bwd(qb, kb, vb, carries, dyb, exchange=None):
    T = qb.shape[0]
    B = min(SB_BLOCK, T)
    n_pairs = SB_HEADS // 2

    def body(q_ref, k_ref, v_ref, carry_ref, do_ref, dq_ref, dk_ref, dv_ref):
        i = pl.program_id(1)

        @pl.when(i == 0)
        def _():
            dk_ref[...] = jnp.zeros_like(dk_ref)
            dv_ref[...] = jnp.zeros_like(dv_ref)

        lane = _lane_iota((1, LANES))
        upper2 = _tri2(B, lambda r, c: r > c)
        lower = _tri2(B, lambda r, c: r < c)[:B]
        pair_cols = [slice(p * LANES, (p + 1) * LANES) for p in range(SB_PAIRS)]
        qms = [_head_masks(q_ref[:, cols] * ATTN_SCALE) for cols in pair_cols]
        doms = [_head_masks(do_ref[:, cols]) for cols in pair_cols]
        q2s, do2s = [jnp.concatenate(m, axis=0) for m in qms], [jnp.concatenate(m, axis=0) for m in doms]
        cms = [carry_ref[:, cols] for cols in pair_cols]

        def pair_block(p, j, rows, state, valid):
            cgs, dq = state
            k = k_ref[rows, pair_cols[p]]
            v = v_ref[rows, pair_cols[p]]
            dzs, probs, new_cgs = [], [], []
            for hh in range(2):
                sp, lb = _sb_scores(qms[p][hh], k, valid)
                c = jnp.sum(jnp.where(lane == hh * HEAD_DIM + j, cms[p], 0.0), axis=1, keepdims=True)
                a = jnp.exp(lb - (c + _tri_sum(sp, upper2)))
                if valid is not None:
                    a = jnp.where(valid, a, 0.0)
                g = a * _dot_nt(doms[p][hh], v)
                gpre = cgs[hh] + _dot(g.astype(BF16), lower)
                dz = g - jnp.exp(lb) * (g + gpre)
                if valid is not None:
                    dz = jnp.where(valid, dz, 0.0)
                dzs.append(dz.astype(BF16))
                probs.append(a.astype(BF16))
                new_cgs.append(cgs[hh] + jnp.sum(g, axis=1, keepdims=True))
            dq = dq + _dot(jnp.concatenate(dzs, axis=1), jnp.concatenate(_head_masks(k), axis=0))
            dk_ref[rows, pair_cols[p]] += _dot_tn(jnp.concatenate(dzs, axis=0), q2s[p])
            dv_ref[rows, pair_cols[p]] += _dot_tn(jnp.concatenate(probs, axis=0), do2s[p])
            return tuple(new_cgs), dq

        def block(j, states, valid):
            rows = pl.ds(pl.multiple_of(j * B, B), B)
            return tuple(pair_block(p, j, rows, states[p], valid) for p in range(SB_PAIRS))

        zero = jnp.zeros((B, 1), F32)
        states = lax.fori_loop(0, i, lambda j, s: block(j, s, None), (((zero, zero), jnp.zeros((B, LANES), F32)),) * SB_PAIRS)
        states = block(i, states, _strictly_below(B))
        for cols, (_, dq) in zip(pair_cols, states):
            dq_ref[:, cols] = (dq * ATTN_SCALE).astype(dq_ref.dtype)

    W = SB_PAIRS * LANES
    blk = pl.BlockSpec((B, W), lambda p, i: (i, p))
    full = pl.BlockSpec((T, W), lambda p, i: (0, p))
    return _hosted_call(
        body, "sb_bwd", (n_pairs // SB_PAIRS, T // B), exchange,
        in_specs=[blk, full, full, blk, blk],
        out_specs=[blk, full, full],
        out_shape=[jax.ShapeDtypeStruct((T, SB_WIDTH), BF16), jax.ShapeDtypeStruct((T, SB_WIDTH), F32),
                   jax.ShapeDtypeStruct((T, SB_WIDTH), F32)],
        semantics=("parallel", "arbitrary"), args=(qb, kb, vb, carries, dyb))


def _ln_stats(u):
    mu = jnp.mean(u, axis=-1, keepdims=True)
    xc = u - mu
    var = jnp.mean(xc * xc, axis=-1, keepdims=True)
    rstd = lax.rsqrt(var + LN_EPS)
    return xc * rstd, rstd


def _ln_bwd(dy, xhat, rstd, g):
    dxh = dy * g
    return rstd * (dxh - jnp.mean(dxh, axis=-1, keepdims=True) - xhat * jnp.mean(dxh * xhat, axis=-1, keepdims=True))


def _gates(gl_ref, bg_ref):
    ga = jax.nn.sigmoid(gl_ref[:, :D_MODEL] + bg_ref[:, :D_MODEL])
    gb = jax.nn.sigmoid(gl_ref[:, D_MODEL:] + bg_ref[:, D_MODEL:])
    return ga, gb


def _mix_fwd(ya, yb, gl, x, wa, wb, wo, b_gate, ln1_g, ln1_b):
    T = x.shape[0]
    tm = min(256, T)

    def body(ya_ref, yb_ref, gl_ref, x_ref, wa_ref, wb_ref, wo_ref, bg_ref, g_ref, b_ref, h_ref, u_ref, x1_ref):
        ga, gb = _gates(gl_ref, bg_ref)
        h = (ga * _dot(ya_ref[...], wa_ref[...]) + gb * _dot(yb_ref[...], wb_ref[...])).astype(BF16)
        h_ref[...] = h
        u = ALPHA * x_ref[...] + _dot(h, wo_ref[...])
        u_ref[...] = u
        xhat, _ = _ln_stats(u)
        x1_ref[...] = (xhat * g_ref[...] + b_ref[...]).astype(BF16)

    row = lambda n: pl.BlockSpec((tm, n), lambda i: (i, 0))
    const = lambda r, n: pl.BlockSpec((r, n), lambda i: (0, 0))
    return pl.pallas_call(
        body, name="mix_fwd", grid=(T // tm,),
        in_specs=[row(SWA_Q_WIDTH), row(SB_WIDTH), row(GATE_WIDTH), row(D_MODEL), const(SWA_Q_WIDTH, D_MODEL), const(SB_WIDTH, D_MODEL),
                  const(D_MODEL, D_MODEL), const(1, GATE_WIDTH), const(1, D_MODEL), const(1, D_MODEL)],
        out_specs=[row(D_MODEL)] * 3,
        out_shape=[jax.ShapeDtypeStruct((T, D_MODEL), BF16), jax.ShapeDtypeStruct((T, D_MODEL), F32), jax.ShapeDtypeStruct((T, D_MODEL), BF16)],
        compiler_params=_params(("parallel",)),
    )(ya, yb, gl, x, wa, wb, wo, b_gate, ln1_g, ln1_b)


def _mix_bwd(du1, ya, yb, gl, wa, wb, wo, b_gate):
    T = du1.shape[0]
    tm = min(256, T)

    def body(du_ref, ya_ref, yb_ref, gl_ref, wa_ref, wb_ref, wo_ref, bg_ref, dya_ref, dyb_ref, dgl_ref, dta_ref, dtb_ref, dbg_ref):
        @pl.when(pl.program_id(0) == 0)
        def _():
            dbg_ref[...] = jnp.zeros_like(dbg_ref)

        dh = _dot_nt(du_ref[...].astype(BF16), wo_ref[...])
        ga, gb = _gates(gl_ref, bg_ref)
        for gate, y_ref, w_ref, dy_ref, dt_ref, lo in ((ga, ya_ref, wa_ref, dya_ref, dta_ref, 0), (gb, yb_ref, wb_ref, dyb_ref, dtb_ref, D_MODEL)):
            t = _dot(y_ref[...], w_ref[...])
            dlogit = dh * t * gate * (1.0 - gate)
            dgl_ref[:, lo:lo + D_MODEL] = dlogit.astype(BF16)
            dbg_ref[:, lo:lo + D_MODEL] += jnp.sum(dlogit, axis=0, keepdims=True)
            dt = (dh * gate).astype(BF16)
            dt_ref[...] = dt
            dy_ref[...] = _dot_nt(dt, w_ref[...]).astype(BF16)

    row = lambda n: pl.BlockSpec((tm, n), lambda i: (i, 0))
    const = lambda r, n: pl.BlockSpec((r, n), lambda i: (0, 0))
    sds = lambda n, dt: jax.ShapeDtypeStruct((T, n), dt)
    return pl.pallas_call(
        body, name="mix_bwd", grid=(T // tm,),
        in_specs=[row(D_MODEL), row(SWA_Q_WIDTH), row(SB_WIDTH), row(GATE_WIDTH), const(SWA_Q_WIDTH, D_MODEL), const(SB_WIDTH, D_MODEL),
                  const(D_MODEL, D_MODEL), const(1, GATE_WIDTH)],
        out_specs=[row(SWA_Q_WIDTH), row(SB_WIDTH), row(GATE_WIDTH), row(D_MODEL), row(D_MODEL), const(1, GATE_WIDTH)],
        out_shape=[sds(SWA_Q_WIDTH, BF16), sds(SB_WIDTH, BF16), sds(GATE_WIDTH, BF16), sds(D_MODEL, BF16), sds(D_MODEL, BF16),
                   jax.ShapeDtypeStruct((1, GATE_WIDTH), F32)],
        compiler_params=_params(("arbitrary",)),
    )(du1, ya, yb, gl, wa, wb, wo, b_gate)


CONV_COLS = LANES


CONV_CHUNK = 64
CONV_CHUNK_FWD = 256
HALO = 8


def _taps(ref, r0, rows, lead):
    return [ref[pl.ds(r0 + lead + k, rows), :] for k in ((-2, -1, 0) if lead else (0, 1, 2))]


def _chunks(T, rows, step, init=None):
    def body(c, carry):
        out = step(pl.multiple_of(c * rows, rows), *(() if init is None else (carry,)))
        return carry if init is None else out
    return lax.fori_loop(0, T // rows, body, 0 if init is None else init)


def _conv_chunk(taps, w_ref, b_ref):
    return w_ref[0:1, :] * taps[0] + w_ref[1:2, :] * taps[1] + w_ref[2:3, :] * taps[2] + b_ref[...]


def _fold(x):
    return jnp.sum(x.reshape(x.shape[0] // 8, 8, x.shape[1]), axis=0)


def _conv_specs(T):
    nb = D_FF // CONV_COLS
    pair = pl.BlockSpec((2, T, CONV_COLS), lambda j: (0, 0, j))
    gate = lambda r: pl.BlockSpec((r, CONV_COLS), lambda j: (0, j))
    up = lambda r: pl.BlockSpec((r, CONV_COLS), lambda j: (0, j + nb))
    return nb, pair, gate, up


def _conv_glu_fwd(p3, conv_w, conv_b):
    T = p3.shape[1]
    nb, pair, gate, up = _conv_specs(T)

    R = min(CONV_CHUNK_FWD, T)

    def body(p_ref, wg_ref, wu_ref, bg_ref, bu_ref, s_ref, pg_s, pu_s):
        for half, scr in enumerate((pg_s, pu_s)):
            scr[0:HALO, :] = jnp.zeros((HALO, CONV_COLS), F32)
            scr[HALO:HALO + T, :] = p_ref[half]
        def step(r0):
            ag = _conv_chunk(_taps(pg_s, r0, R, HALO), wg_ref, bg_ref)
            au = _conv_chunk(_taps(pu_s, r0, R, HALO), wu_ref, bu_ref)
            s_ref[pl.ds(r0, R), :] = (ag * jax.nn.sigmoid(ag) * au).astype(BF16)

        _chunks(T, R, step)

    return pl.pallas_call(
        body, name="conv_glu_fwd", grid=(nb,),
        in_specs=[pair, gate(3), up(3), gate(1), up(1)],
        out_specs=pl.BlockSpec((T, CONV_COLS), lambda j: (0, j)),
        out_shape=jax.ShapeDtypeStruct((T, D_FF), BF16),
        scratch_shapes=[pltpu.VMEM((T + HALO, CONV_COLS), F32)] * 2,
        compiler_params=_params(("parallel",)),
    )(p3, conv_w, conv_w, conv_b, conv_b)


def _conv_glu_bwd(p3, ds, conv_w, conv_b):
    T = p3.shape[1]
    nb, pair, gate, up = _conv_specs(T)

    R = min(CONV_CHUNK, T)

    def body(p_ref, ds_ref, wg_ref, wu_ref, bg_ref, bu_ref, dp_ref, dwg_ref, dwu_ref, dbg_ref, dbu_ref, pg_s, pu_s, dag_s, dau_s):
        for half, scr in enumerate((pg_s, pu_s)):
            scr[0:HALO, :] = jnp.zeros((HALO, CONV_COLS), F32)
            scr[HALO:HALO + T, :] = p_ref[half]
        for scr in (dag_s, dau_s):
            scr[T:T + HALO, :] = jnp.zeros((HALO, CONV_COLS), F32)
        halves = ((pg_s, dag_s, wg_ref, dwg_ref, dbg_ref), (pu_s, dau_s, wu_ref, dwu_ref, dbu_ref))

        def step(r0, sums):
            taps = [_taps(p_s, r0, R, HALO) for p_s, *_ in halves]
            ag = _conv_chunk(taps[0], wg_ref, bg_ref)
            au = _conv_chunk(taps[1], wu_ref, bu_ref)
            sg = jax.nn.sigmoid(ag)
            d = ds_ref[pl.ds(r0, R), :]
            das = (d * au * (sg * (1.0 + ag * (1.0 - sg))), d * ag * sg)
            out = []
            for half, (_, da_s, *_) in enumerate(halves):
                da_s[pl.ds(r0, R), :] = das[half]
                out.append(tuple(sums[half][k] + _fold(das[half] * taps[half][k]) for k in range(3)) + (sums[half][3] + _fold(das[half]),))
            return tuple(out)

        sums = _chunks(T, R, step, ((jnp.zeros((8, CONV_COLS), F32),) * 4,) * 2)
        for half, (_, da_s, w_ref, dw_ref, db_ref) in enumerate(halves):
            for k in range(3):
                dw_ref[k:k + 1, :] = jnp.sum(sums[half][k], axis=0, keepdims=True)
            db_ref[...] = jnp.sum(sums[half][3], axis=0, keepdims=True)

            def transposed(r0, da_s=da_s, w_ref=w_ref, half=half):
                da0, da1, da2 = _taps(da_s, r0, R, 0)
                dp_ref[half, pl.ds(r0, R), :] = (w_ref[2:3, :] * da0 + w_ref[1:2, :] * da1 + w_ref[0:1, :] * da2).astype(BF16)

            _chunks(T, R, transposed)

    col = lambda r: pl.BlockSpec((r, CONV_COLS), lambda j: (0, j))
    return pl.pallas_call(
        body, name="conv_glu_bwd", grid=(nb,),
        in_specs=[pair, col(T), gate(3), up(3), gate(1), up(1)],
        out_specs=[pair, col(3), col(3), col(1), col(1)],
        out_shape=[jax.ShapeDtypeStruct((2, T, D_FF), BF16), jax.ShapeDtypeStruct((3, D_FF), F32), jax.ShapeDtypeStruct((3, D_FF), F32),
                   jax.ShapeDtypeStruct((1, D_FF), F32), jax.ShapeDtypeStruct((1, D_FF), F32)],
        scratch_shapes=[pltpu.VMEM((T + HALO, CONV_COLS), F32)] * 4,
        compiler_params=_params(("parallel",)),
    )(p3, ds, conv_w, conv_w, conv_b, conv_b)


def _ffn_down_loss(s, w_down, u1, ln1_g, ln1_b, ln2_g, ln2_b, target):
    T = u1.shape[0]
    tm = min(256, T)

    def body(s_ref, w_ref, u1_ref, g1_ref, b1_ref, g2_ref, b2_ref, t_ref, du_ref, dub_ref, dg_ref, db_ref, loss_ref):
        @pl.when(pl.program_id(0) == 0)
        def _():
            dg_ref[...] = jnp.zeros_like(dg_ref)
            db_ref[...] = jnp.zeros_like(db_ref)
            loss_ref[...] = jnp.zeros_like(loss_ref)

        xh1, _ = _ln_stats(u1_ref[...])
        x1 = xh1 * g1_ref[...] + b1_ref[...]
        u2 = ALPHA * x1 + _dot(s_ref[...], w_ref[...])
        xh2, rstd2 = _ln_stats(u2)
        err = xh2 * g2_ref[...] + b2_ref[...] - t_ref[...]
        per_token = jnp.mean(err * err, axis=-1, keepdims=True)
        loss_ref[...] += 0.5 * jnp.sum(per_token, axis=0, keepdims=True)
        dy = err * (1.0 / D_MODEL)
        dg_ref[...] += jnp.sum(dy * xh2, axis=0, keepdims=True)
        db_ref[...] += jnp.sum(dy, axis=0, keepdims=True)
        du2 = _ln_bwd(dy, xh2, rstd2, g2_ref[...])
        du_ref[...] = du2
        dub_ref[...] = du2.astype(BF16)

    row = lambda n: pl.BlockSpec((tm, n), lambda i: (i, 0))
    const = lambda r, n: pl.BlockSpec((r, n), lambda i: (0, 0))
    vec = const(1, D_MODEL)
    return pl.pallas_call(
        body, name="ffn_down_loss", grid=(T // tm,),
        in_specs=[row(D_FF), const(D_FF, D_MODEL), row(D_MODEL), vec, vec, vec, vec, row(D_MODEL)],
        out_specs=[row(D_MODEL), row(D_MODEL), vec, vec, const(1, LANES)],
        out_shape=[jax.ShapeDtypeStruct((T, D_MODEL), F32), jax.ShapeDtypeStruct((T, D_MODEL), BF16), jax.ShapeDtypeStruct((1, D_MODEL), F32),
                   jax.ShapeDtypeStruct((1, D_MODEL), F32), jax.ShapeDtypeStruct((1, LANES), F32)],
        compiler_params=_params(("arbitrary",)),
    )(s, w_down, u1, ln1_g, ln1_b, ln2_g, ln2_b, target)


def _ffn_up_bwd_ln1(dp3, w_up, du2, u1, ln1_g):
    T = u1.shape[0]
    tm = min(256, T)

    def body(dp_ref, w_ref, du2_ref, u1_ref, g_ref, du_ref, dub_ref, dg_ref, db_ref):
        @pl.when(pl.program_id(0) == 0)
        def _():
            dg_ref[...] = jnp.zeros_like(dg_ref)
            db_ref[...] = jnp.zeros_like(db_ref)

        dx1 = _dot_nt(dp_ref[0], w_ref[:, :D_FF]) + _dot_nt(dp_ref[1], w_ref[:, D_FF:]) + ALPHA * du2_ref[...]
        xh, rstd = _ln_stats(u1_ref[...])
        dg_ref[...] += jnp.sum(dx1 * xh, axis=0, keepdims=True)
        db_ref[...] += jnp.sum(dx1, axis=0, keepdims=True)
        du1 = _ln_bwd(dx1, xh, rstd, g_ref[...])
        du_ref[...] = du1
        dub_ref[...] = du1.astype(BF16)

    row = lambda n: pl.BlockSpec((tm, n), lambda i: (i, 0))
    const = lambda r, n: pl.BlockSpec((r, n), lambda i: (0, 0))
    vec = const(1, D_MODEL)
    return pl.pallas_call(
        body, name="ffn_up_bwd_ln1", grid=(T // tm,),
        in_specs=[pl.BlockSpec((2, tm, D_FF), lambda i: (0, i, 0)), const(D_MODEL, 2 * D_FF), row(D_MODEL), row(D_MODEL), vec],
        out_specs=[row(D_MODEL), row(D_MODEL), vec, vec],
        out_shape=[jax.ShapeDtypeStruct((T, D_MODEL), F32), jax.ShapeDtypeStruct((T, D_MODEL), BF16), jax.ShapeDtypeStruct((1, D_MODEL), F32),
                   jax.ShapeDtypeStruct((1, D_MODEL), F32)],
        compiler_params=_params(("arbitrary",)),
    )(dp3, w_up, du2, u1, ln1_g)


def _local_step(x, positions, w_in, b_gate, sinks, ln1_g, ln1_b, conv_b, ln2_g, ln2_b, target, later_weights,
                early_exchange=None, tail_exchange=None):
    T = x.shape[0]
    inv_freq = 1.0 / (ROPE_THETA ** (jnp.arange(0, HEAD_DIM, 2, dtype=F32) / HEAD_DIM))
    cos, sin = _rope_tables(positions.reshape(T, 1), jnp.tile(inv_freq, LANES // (HEAD_DIM // 2)).reshape(1, LANES))

    xb, qa, ka, va, qb, kb, vb, gl = _in_proj(x, w_in)
    ya = _swa_fwd(qa, ka, va, cos, sin, sinks)
    if isinstance(later_weights, tuple):
        exchange, finish = later_weights
        (yb, carries), arrived = _sb_fwd(qb, kb, vb, exchange)
        later_weights = finish(arrived)
    else:
        (yb, carries), _ = _sb_fwd(qb, kb, vb)
    wa, wb, wo, w_up, conv_w, w_down = later_weights
    h, u1, x1 = _mix_fwd(ya, yb, gl, x, wa, wb, wo, b_gate, ln1_g, ln1_b)

    ff_tn = D_FF // 2
    nff = D_FF // ff_tn
    tm = min(512, T)
    p3 = _matmul(x1, w_up, kind="nn", name="ffn_up", grid=(T // tm, 2 * nff),
                 a_spec=pl.BlockSpec((tm, D_MODEL), lambda i, j: (i, 0)), b_spec=pl.BlockSpec((D_MODEL, ff_tn), lambda i, j: (0, j)),
                 out_spec=pl.BlockSpec((None, tm, ff_tn), lambda i, j: (j // nff, i, j % nff)),
                 out_shape=jax.ShapeDtypeStruct((2, T, D_FF), F32))
    s = _conv_glu_fwd(p3, conv_w, conv_b)
    du2, du2b, dln2_g, dln2_b, loss = _ffn_down_loss(s, w_down, u1, ln1_g, ln1_b, ln2_g, ln2_b, target)

    ds = _matmul(du2b, w_down, kind="nt", name="ffn_down_bwd", grid=(T // tm, nff),
                 a_spec=pl.BlockSpec((tm, D_MODEL), lambda i, j: (i, 0)), b_spec=pl.BlockSpec((ff_tn, D_MODEL), lambda i, j: (j, 0)),
                 out_spec=pl.BlockSpec((tm, ff_tn), lambda i, j: (i, j)), out_shape=jax.ShapeDtypeStruct((T, D_FF), F32))
    dp3, dcw_g, dcw_u, dcb_g, dcb_u = _conv_glu_bwd(p3, ds, conv_w, conv_b)
    tk = 256
    dw_down = _matmul(s, du2b, kind="tn", name="dw_down", grid=(D_FF // tk,),
                      a_spec=pl.BlockSpec((T, tk), lambda i: (0, i)), b_spec=pl.BlockSpec((T, D_MODEL), lambda i: (0, 0)),
                      out_spec=pl.BlockSpec((tk, D_MODEL), lambda i: (i, 0)), out_shape=jax.ShapeDtypeStruct((D_FF, D_MODEL), BF16))
    dw_up = _matmul(x1, dp3, kind="tn", name="dw_up", grid=(D_MODEL // 512, 2 * nff),
                    a_spec=pl.BlockSpec((T, 512), lambda i, j: (0, i)), b_spec=pl.BlockSpec((None, T, ff_tn), lambda i, j: (j // nff, 0, j % nff)),
                    out_spec=pl.BlockSpec((512, ff_tn), lambda i, j: (i, j)), out_shape=jax.ShapeDtypeStruct((D_MODEL, 2 * D_FF), BF16))
    du1, du1b, dln1_g, dln1_b = _ffn_up_bwd_ln1(dp3, w_up, du2, u1, ln1_g)
    dya, dyb, dgl, dta, dtb, db_gate = _mix_bwd(du1, ya, yb, gl, wa, wb, wo, b_gate)

    def dw_tn(a, g, name):
        rows, cols = a.shape[1], g.shape[1]
        tn = min(512, cols)
        return _matmul(a, g, kind="tn", name=name, grid=(rows // 512, cols // tn),
                       a_spec=pl.BlockSpec((T, 512), lambda i, j: (0, i)), b_spec=pl.BlockSpec((T, tn), lambda i, j: (0, j)),
                       out_spec=pl.BlockSpec((512, tn), lambda i, j: (i, j)), out_shape=jax.ShapeDtypeStruct((rows, cols), BF16))

    dwa = dw_tn(ya, dta, "dw_branch_a")
    dwb = dw_tn(yb, dtb, "dw_branch_b")
    dwo = dw_tn(h, du1b, "dw_out")

    grads = dict(
        b_gate=db_gate, w_branch_a=dwa, w_branch_b=dwb, w_out=dwo, ln1_g=dln1_g, ln1_b=dln1_b,
        w_up=dw_up, conv_w=jnp.concatenate([dcw_g, dcw_u], axis=1), conv_b=(dcb_g, dcb_u), w_down=dw_down, ln2_g=dln2_g, ln2_b=dln2_b)
    (dqb, dkb, dvb), early_out = _sb_bwd(qb, kb, vb, carries, dyb, early_exchange(grads) if early_exchange else None)
    dqa, dka, dva, grads["sinks"] = _swa_bwd(qa, ka, va, cos, sin, sinks, dya)
    dproj = (dqa, dka, dva, dqb, dkb, dvb, dgl)
    grads["w_in"] = tuple(dw_tn(xb, piece, f"dw_in_{k}") for k, piece in enumerate(dproj))
    grad_x, tail_out = _grad_x(dproj, w_in, du1, tail_exchange(grads, loss) if tail_exchange else None)
    return loss, grad_x, grads, early_out, tail_out


def _grad_x(dproj, w_in, du1, exchange=None):
    T = du1.shape[0]
    tm = min(256, T)
    offs = np.cumsum((0,) + IN_WIDTHS)

    def body(*refs):
        pieces, (w_ref, du_ref, o_ref) = refs[:len(IN_WIDTHS)], refs[len(IN_WIDTHS):]
        acc = ALPHA * du_ref[...]
        for p_ref, a, b in zip(pieces, offs[:-1], offs[1:]):
            acc = acc + _dot_nt(p_ref[...].astype(BF16), w_ref[:, a:b])
        o_ref[...] = acc

    row = lambda n: pl.BlockSpec((tm, n), lambda i: (i, 0))
    (grad_x,), arrived = _hosted_call(
        body, "grad_x", (T // tm,), exchange,
        in_specs=[row(n) for n in IN_WIDTHS] + [pl.BlockSpec((D_MODEL, IN_TOTAL), lambda i: (0, 0)), row(D_MODEL)],
        out_specs=[row(D_MODEL)], out_shape=[jax.ShapeDtypeStruct((T, D_MODEL), F32)], semantics=("parallel",),
        args=(*dproj, w_in, du1))
    return grad_x, arrived


ANY = pl.BlockSpec(memory_space=pl.ANY)


def _all_gather(slabs, name):
    n = len(slabs)

    def body(*refs):
        ins, outs = refs[:n], refs[n:2 * n]
        send_sems, recv_sems, local_sems = refs[2 * n:]
        x, y, c = lax.axis_index("x"), lax.axis_index("y"), lax.axis_index("c")
        me, sibling = (x, y, c), (x, y, 1 - c)
        chips = [(1 - x, y), (x, 1 - y), (1 - x, 1 - y)]

        def slot(pos):
            return 4 * pos[0] + 2 * pos[1] + pos[2]

        def copy(a, k, block, to, from_input=False):
            return pltpu.make_async_remote_copy(
                src_ref=ins[a] if from_input else outs[a].at[slot(block)], dst_ref=outs[a].at[slot(block)],
                send_sem=send_sems.at[a, k], recv_sem=recv_sems.at[a, k], device_id=to, device_id_type=MESH)

        mine = [pltpu.make_async_copy(ins[a], outs[a].at[slot(me)], local_sems.at[a]) for a in range(n)]
        for cp in mine:
            cp.start()
        first = []
        for a in range(n):
            first.append(copy(a, 0, me, sibling, from_input=True))
            first += [copy(a, 1 + j, me, (*chip, c), from_input=True) for j, chip in enumerate(chips)]
        for cp in first:
            cp.start()
        passed = []
        for j, chip in enumerate(chips):
            for a in range(n):
                copy(a, 1 + j, (*chip, c), me).wait_recv()
                fwd = copy(a, 4 + j, (*chip, c), sibling)
                fwd.start()
                passed.append(fwd)
        for a in range(n):
            copy(a, 0, sibling, me).wait_recv()
            for j, chip in enumerate(chips):
                copy(a, 4 + j, (*chip, 1 - c), me).wait_recv()
        for cp in first + passed:
            cp.wait_send()
        for cp in mine:
            cp.wait()

    return pl.pallas_call(
        body, name=name,
        in_specs=[ANY] * n, out_specs=[ANY] * n,
        out_shape=[jax.ShapeDtypeStruct((N_DEV,) + s.shape, s.dtype) for s in slabs],
        scratch_shapes=[pltpu.SemaphoreType.DMA((n, 7)), pltpu.SemaphoreType.DMA((n, 7)), pltpu.SemaphoreType.DMA((n,))],
    )(*slabs)


def _all_to_all(slabs, name):
    n = len(slabs)

    def body(*refs):
        ins, outs = refs[:n], refs[n:2 * n]
        send_sems, recv_sems, local_sems = refs[2 * n:]
        x, y, c = lax.axis_index("x"), lax.axis_index("y"), lax.axis_index("c")
        my_slot = 4 * x + 2 * y + c
        flips = [(fx, fy, fc) for fx in (0, 1) for fy in (0, 1) for fc in (0, 1) if (fx, fy, fc) != (0, 0, 0)]

        def copy(a, k):
            fx, fy, fc = flips[k]
            peer = (x ^ fx, y ^ fy, c ^ fc)
            peer_slot = 4 * peer[0] + 2 * peer[1] + peer[2]
            send = pltpu.make_async_remote_copy(src_ref=ins[a].at[peer_slot], dst_ref=outs[a].at[my_slot], send_sem=send_sems.at[a, k],
                                                recv_sem=recv_sems.at[a, k], device_id=peer, device_id_type=MESH)
            recv = pltpu.make_async_remote_copy(src_ref=ins[a].at[peer_slot], dst_ref=outs[a].at[peer_slot], send_sem=send_sems.at[a, k],
                                                recv_sem=recv_sems.at[a, k], device_id=peer, device_id_type=MESH)
            return send, recv

        mine = [pltpu.make_async_copy(ins[a].at[my_slot], outs[a].at[my_slot], local_sems.at[a]) for a in range(n)]
        for cp in mine:
            cp.start()
        copies = [copy(a, k) for a in range(n) for k in range(len(flips))]
        for send, _ in copies:
            send.start()
        for send, recv in copies:
            recv.wait_recv()
            send.wait_send()
        for cp in mine:
            cp.wait()

    return pl.pallas_call(
        body, name=name,
        in_specs=[ANY] * n, out_specs=[ANY] * n,
        out_shape=[jax.ShapeDtypeStruct(s.shape, s.dtype) for s in slabs],
        scratch_shapes=[pltpu.SemaphoreType.DMA((n, 7)), pltpu.SemaphoreType.DMA((n, 7)), pltpu.SemaphoreType.DMA((n,))],
    )(*slabs)


def _row_tile(rows):
    for cand in range(256, 7, -8):
        if rows % cand == 0:
            return cand
    return rows


def _window(w):
    wp = max(-(-((w * r) % LANES + w) // LANES) for r in range(N_DEV)) * LANES
    assert all((w * r) // LANES * LANES + wp <= N_DEV * w for r in range(N_DEV))
    return wp


def _join_cols(slabs, name):
    _, R, w = slabs.shape
    tr = _row_tile(R)
    wp = _window(w)

    def body(g_ref, o_ref, pad_ref):
        if w % LANES == 0:
            for r in range(N_DEV):
                o_ref[:, w * r:w * (r + 1)] = g_ref[r]
            return
        o_ref[...] = jnp.zeros_like(o_ref)
        pad_ref[...] = jnp.zeros_like(pad_ref)
        for r in range(N_DEV):
            q, s = divmod(w * r, LANES)
            pad_ref[:, :w] = g_ref[r]
            y = pad_ref[...]
            if s:
                y = pltpu.roll(y, s, axis=1)
            o_ref[:, LANES * q:LANES * q + wp] += y

    return pl.pallas_call(
        body, name=name, grid=(R // tr,),
        in_specs=[pl.BlockSpec((N_DEV, tr, w), lambda i: (0, i, 0))], out_specs=pl.BlockSpec((tr, N_DEV * w), lambda i: (i, 0)),
        out_shape=jax.ShapeDtypeStruct((R, N_DEV * w), slabs.dtype), scratch_shapes=[pltpu.VMEM((tr, wp), slabs.dtype)],
        compiler_params=_params(("parallel",)),
    )(slabs)


def _split_cols(pieces, name):
    R = pieces[0].shape[0]
    widths = [p.shape[1] for p in pieces]
    total = sum(widths)
    w = total // N_DEV
    tr = _row_tile(R)
    wp = _window(w)
    offs = np.cumsum([0] + widths)
    dtype = pieces[0].dtype

    def body(*refs):
        ins, (o_ref, full_ref) = refs[:len(pieces)], refs[len(pieces):]
        for p_ref, a, b in zip(ins, offs[:-1], offs[1:]):
            full_ref[:, a:b] = p_ref[...].astype(dtype)
        for r in range(N_DEV):
            q, s = divmod(w * r, LANES)
            y = full_ref[:, LANES * q:LANES * q + wp]
            if s:
                y = pltpu.roll(y, wp - s, axis=1)
            o_ref[r] = y[:, :w]

    return pl.pallas_call(
        body, name=name, grid=(R // tr,),
        in_specs=[pl.BlockSpec((tr, n), lambda i: (i, 0)) for n in widths], out_specs=pl.BlockSpec((N_DEV, tr, w), lambda i: (0, i, 0)),
        out_shape=jax.ShapeDtypeStruct((N_DEV, R, w), dtype), scratch_shapes=[pltpu.VMEM((tr, total), dtype)],
        compiler_params=_params(("parallel",)),
    )(*pieces)


def _adamw(g, w, m, v):
    m_new = ADAM_B1 * m + (1.0 - ADAM_B1) * g
    v_new = ADAM_B2 * v + (1.0 - ADAM_B2) * jnp.square(g)
    m_hat = m_new / (1.0 - ADAM_B1 ** ADAM_STEP)
    v_hat = v_new / (1.0 - ADAM_B2 ** ADAM_STEP)
    return -ADAM_LR * (m_hat / (jnp.sqrt(v_hat) + ADAM_EPS) + ADAM_WD * w), m_new, v_new


def _sum_parts(p_ref):
    g = p_ref[0].astype(F32)
    for d in range(1, N_DEV):
        g = g + p_ref[d].astype(F32)
    return g


def _reduce_adamw(parts, w, m, v, name):
    R, C = w.shape
    tr = _row_tile(R)

    def body(p_ref, w_ref, m_ref, v_ref, g_ref, d_ref, mo_ref, vo_ref):
        g = _sum_parts(p_ref)
        g_ref[...] = g
        d_ref[...], mo_ref[...], vo_ref[...] = _adamw(g, w_ref[...], m_ref[...], v_ref[...])

    row = pl.BlockSpec((tr, C), lambda i: (i, 0))
    return pl.pallas_call(
        body, name=name, grid=(R // tr,),
        in_specs=[pl.BlockSpec((N_DEV, tr, C), lambda i: (0, i, 0)), row, row, row],
        out_specs=[row] * 4, out_shape=[jax.ShapeDtypeStruct((R, C), F32)] * 4,
        compiler_params=_params(("parallel",)),
    )(parts, w, m, v)


def _reduce_adamw_small(parts, ws, ms, vs):
    sizes = [a.shape[1] for a in ws]
    k = len(sizes)
    offs = np.cumsum([0] + [-(-n // LANES) * LANES for n in sizes])

    def body(*refs):
        p_ref, w_refs, m_refs, v_refs = refs[0], refs[1:1 + k], refs[1 + k:1 + 2 * k], refs[1 + 2 * k:1 + 3 * k]
        outs, loss_ref = refs[1 + 3 * k:-1], refs[-1]
        g_all = _sum_parts(p_ref)
        for j, n in enumerate(sizes):
            g = g_all[:, offs[j]:offs[j] + LANES * (-(-n // LANES))][:, :n]
            outs[4 * j][...] = g
            outs[4 * j + 1][...], outs[4 * j + 2][...], outs[4 * j + 3][...] = _adamw(g, w_refs[j][...], m_refs[j][...], v_refs[j][...])
        loss_ref[...] = g_all[:, offs[k]:offs[k] + LANES]

    vm = pl.BlockSpec(memory_space=pltpu.VMEM)
    out_shape = [jax.ShapeDtypeStruct((1, n), F32) for n in sizes for _ in range(4)] + [jax.ShapeDtypeStruct((1, LANES), F32)]
    res = pl.pallas_call(
        body, name="reduce_adamw_replicated", in_specs=[vm] * (1 + 3 * k), out_specs=[vm] * len(out_shape), out_shape=out_shape,
        compiler_params=_params(),
    )(parts, *ws, *ms, *vs)
    return [res[4 * j:4 * j + 4] for j in range(k)], res[-1]


COL_SHARDED = ("w_in", "w_branch_a", "w_branch_b", "w_up", "conv_w")
ROW_SHARDED = ("w_out", "w_down")
SMALL = ("b_gate", "sinks", "ln1_g", "ln1_b", "conv_b", "ln2_g", "ln2_b")
ORDER = ("w_in", "b_gate", "sinks", "w_branch_a", "w_branch_b", "w_out", "ln1_g", "ln1_b", "w_up", "conv_w", "conv_b", "w_down", "ln2_g", "ln2_b")


def _pad_lanes(a):
    pad = (-a.shape[-1]) % LANES
    return a if pad == 0 else jnp.pad(a, ((0, 0), (0, pad)))


def kernel(x, positions, w_in, b_gate, sinks, w_branch_a, w_branch_b, w_out, ln1_g, ln1_b, w_up, conv_w, conv_b, w_down, ln2_g, ln2_b, loss_target, m_w_in, m_b_gate, m_sinks, m_w_branch_a, m_w_branch_b, m_w_out, m_ln1_g, m_ln1_b, m_w_up, m_conv_w, m_conv_b, m_w_down, m_ln2_g, m_ln2_b, v_w_in, v_b_gate, v_sinks, v_w_branch_a, v_w_branch_b, v_w_out, v_ln1_g, v_ln1_b, v_w_up, v_conv_w, v_conv_b, v_w_down, v_ln2_g, v_ln2_b):
    args = dict(locals())
    sharded = COL_SHARDED + ROW_SHARDED
    w = {n: args[n][0] if n in sharded else args[n] for n in ORDER}
    m = {n: args["m_" + n][0] if n in sharded else args["m_" + n] for n in ORDER}
    v = {n: args["v_" + n][0] if n in sharded else args["v_" + n] for n in ORDER}

    travel = {n: (w[n] if n == "conv_w" else w[n].astype(BF16)) for n in sharded}
    (g_in,) = _all_gather([travel["w_in"]], "all_gather_w_in")
    w_in_full = _join_cols(g_in, "join_w_in")
    later = ("w_branch_a", "w_branch_b", "w_out", "w_up", "conv_w", "w_down")

    def join(name, slabs):
        return _join_cols(slabs, "join_" + name) if name in COL_SHARDED else slabs.reshape(-1, slabs.shape[-1])

    def split(name, grad):
        if name in COL_SHARDED:
            return _split_cols(grad if isinstance(grad, tuple) else (grad,), "split_d" + name)
        return grad.reshape((N_DEV, -1, grad.shape[-1]))

    def early_exchange(grads):
        return _Exchange([split(n, grads[n]) for n in later], ["scatter"] * len(later))

    def tail_exchange(grads, loss):
        small_pack = jnp.concatenate(
            [_pad_lanes(p) for n in SMALL for p in (grads[n] if isinstance(grads[n], tuple) else (grads[n],))] + [loss], axis=1)
        return _Exchange([split("w_in", grads["w_in"]), small_pack], ["scatter", "gather"])

    gather_later = _Exchange([travel[n] for n in later], ["gather"] * len(later))
    _, grad_x, _, early_out, (recv_w_in, small_parts) = _local_step(
        x[0], positions[0], w_in_full, w["b_gate"], w["sinks"][0], w["ln1_g"], w["ln1_b"], w["conv_b"], w["ln2_g"], w["ln2_b"], loss_target[0],
        (gather_later, lambda arrived: [join(n, a) for n, a in zip(later, arrived)]), early_exchange, tail_exchange)
    recv = dict(zip(later, early_out), w_in=recv_w_in)

    res = {n: _reduce_adamw(recv[n], w[n], m[n], v[n], "reduce_adamw_" + n) for n in sharded}
    small_res, loss_sum = _reduce_adamw_small(small_parts, [w[n] for n in SMALL], [m[n] for n in SMALL], [v[n] for n in SMALL])
    res.update(zip(SMALL, small_res))
    out = [loss_sum[0, 0], grad_x[None]]
    for k in range(4):
        out += [res[n][k][None] if n in sharded else res[n][k] for n in ORDER]
    return tuple(out)
```

```python
import functools

import jax
import jax.numpy as jnp
import numpy as np
from jax import lax
from jax.experimental import pallas as pl
from jax.experimental.pallas import tpu as pltpu

D_MODEL = 1024
HEAD_DIM = 64
SWA_Q_HEADS = 8
SWA_KV_HEADS = 2
SB_HEADS = 8
WINDOW = 128
ROPE_THETA = 10000.0
D_FF = 2816
LN_EPS = 1e-5
DEPTH = 1
ALPHA = (2.0 * DEPTH) ** 0.25
SWA_Q_WIDTH = SWA_Q_HEADS * HEAD_DIM
SWA_KV_WIDTH = SWA_KV_HEADS * HEAD_DIM
SB_WIDTH = SB_HEADS * HEAD_DIM
GATE_WIDTH = 2 * D_MODEL
IN_WIDTHS = (SWA_Q_WIDTH, SWA_KV_WIDTH, SWA_KV_WIDTH, SB_WIDTH, SB_WIDTH, SB_WIDTH, GATE_WIDTH)
IN_TOTAL = sum(IN_WIDTHS)
ATTN_SCALE = HEAD_DIM ** -0.5

ADAM_LR = 0.001
ADAM_B1 = 0.9
ADAM_B2 = 0.999
ADAM_EPS = 1e-08
ADAM_WD = 0.01
ADAM_STEP = 10

N_DEV = 8
LANES = 128
SB_BLOCK = 256
SB_PAIRS = 4
VMEM_LIMIT = 56 * 1024 * 1024

F32 = jnp.float32
BF16 = jnp.bfloat16
ACT_DTYPE = BF16
MESH = pl.DeviceIdType.MESH


def _params(sem=None):
    return pltpu.CompilerParams(dimension_semantics=sem, vmem_limit_bytes=VMEM_LIMIT)


def _dot(a, b):
    return jnp.dot(a, b, preferred_element_type=F32)


def _dot_nt(a, b):
    return lax.dot_general(a, b, (((1,), (1,)), ((), ())), preferred_element_type=F32)


def _dot_tn(a, b):
    return lax.dot_general(a, b, (((0,), (0,)), ((), ())), preferred_element_type=F32)


def _split_bf16(v):
    hi = v.astype(BF16)
    lo = (v - hi.astype(F32)).astype(BF16)
    return hi, lo


def _matmul(a, b, *, kind, out_shape, grid, a_spec, b_spec, out_spec, name, add=None, add_spec=None, add_scale=1.0):
    dot = {"nn": _dot, "nt": _dot_nt, "tn": _dot_tn}[kind]

    def body(*refs):
        if add is None:
            a_ref, b_ref, o_ref = refs
        else:
            a_ref, b_ref, add_ref, o_ref = refs
        r = dot(a_ref[...].astype(BF16), b_ref[...].astype(BF16))
        if add is not None:
            r = r + add_scale * add_ref[...]
        o_ref[...] = r.astype(o_ref.dtype)

    ins = [a, b] + ([] if add is None else [add])
    specs = [a_spec, b_spec] + ([] if add is None else [add_spec])
    return pl.pallas_call(
        body, name=name, grid=grid, in_specs=specs, out_specs=out_spec, out_shape=out_shape,
        compiler_params=_params(("parallel",) * len(grid)),
    )(*ins)


def _rope_tables(pos_col, inv_freq_lanes):
    T = pos_col.shape[0]
    tm = min(512, T)

    def body(pos_ref, f_ref, cos_ref, sin_ref):
        ang = pos_ref[...].astype(F32) * f_ref[...]
        cos_ref[...] = jnp.cos(ang)
        sin_ref[...] = jnp.sin(ang)

    return pl.pallas_call(
        body, name="rope_tables", grid=(T // tm,),
        in_specs=[pl.BlockSpec((tm, 1), lambda i: (i, 0)), pl.BlockSpec((1, LANES), lambda i: (0, 0))],
        out_specs=[pl.BlockSpec((tm, LANES), lambda i: (i, 0))] * 2,
        out_shape=[jax.ShapeDtypeStruct((T, LANES), F32)] * 2,
        compiler_params=_params(("parallel",)),
    )(pos_col, inv_freq_lanes)


def _lane_iota(shape):
    return lax.broadcasted_iota(jnp.int32, shape, len(shape) - 1)


def _rot_half(t):
    first = (_lane_iota(t.shape) % HEAD_DIM) < (HEAD_DIM // 2)
    return jnp.where(first, -pltpu.roll(t, LANES - HEAD_DIM // 2, axis=1), pltpu.roll(t, HEAD_DIM // 2, axis=1))


def _rope(t, cos, sin):
    return t * cos + _rot_half(t) * sin


def _rope_transpose(d, cos, sin):
    return d * cos - _rot_half(d * sin)


_IN_DTYPES = (F32, F32, BF16, BF16, BF16, BF16, F32)


def _in_proj(x, w_in_b):
    T = x.shape[0]
    tm = min(256, T)
    offs = np.cumsum((0,) + IN_WIDTHS)

    def body(x_ref, w_ref, xb_ref, *outs):
        xb = x_ref[...].astype(BF16)
        xb_ref[...] = xb
        for o_ref, a, b in zip(outs, offs[:-1], offs[1:]):
            o_ref[...] = _dot(xb, w_ref[:, a:b]).astype(o_ref.dtype)

    row = lambda n: pl.BlockSpec((tm, n), lambda i: (i, 0))
    return pl.pallas_call(
        body, name="in_proj", grid=(T // tm,),
        in_specs=[row(D_MODEL), pl.BlockSpec((D_MODEL, IN_TOTAL), lambda i: (0, 0))],
        out_specs=[row(D_MODEL)] + [row(n) for n in IN_WIDTHS],
        out_shape=[jax.ShapeDtypeStruct((T, D_MODEL), BF16)] + [jax.ShapeDtypeStruct((T, n), dt) for n, dt in zip(IN_WIDTHS, _IN_DTYPES)],
        compiler_params=_params(("parallel",)),
    )(x, w_in_b)


def _swa_specs(T):
    blk = WINDOW
    cur = lambda n: pl.BlockSpec((blk, n), lambda i: (i, 0))
    prev = lambda n: pl.BlockSpec((blk, n), lambda i: (jnp.maximum(i - 1, 0), 0))
    return blk, cur, prev


def _swa_window(i, kp, kc, vp, vc, cp, cc, sp, sc):
    kwin = jnp.concatenate([_rope(kp, cp, sp), _rope(kc, cc, sc)], axis=0)
    vwin = jnp.concatenate([vp, vc], axis=0)
    lane = _lane_iota(kwin.shape)
    low = lane < HEAD_DIM
    ks, vs = [], []
    for g in range(SWA_KV_HEADS):
        k0 = jnp.where(low, kwin if g == 0 else pltpu.roll(kwin, HEAD_DIM, axis=1), 0.0)
        v0 = jnp.where(low, vwin if g == 0 else pltpu.roll(vwin, HEAD_DIM, axis=1), 0.0)
        ks.append((k0, pltpu.roll(k0, HEAD_DIM, axis=1)))
        vs.append((v0, pltpu.roll(v0, HEAD_DIM, axis=1)))
    blk = WINDOW
    r = lax.broadcasted_iota(jnp.int32, (blk, 2 * blk), 0)
    c = lax.broadcasted_iota(jnp.int32, (blk, 2 * blk), 1)
    rel = blk + r - c
    valid = (rel >= 0) & (rel < WINDOW) & ((c >= blk) | (i > 0))
    return ks, vs, valid


def _swa_probs(qh, kk, valid, sink):
    s = _dot_nt(qh, kk) * ATTN_SCALE
    s = jnp.where(valid, s, -1e30)
    m = jnp.maximum(jnp.max(s, axis=1, keepdims=True), sink)
    p = jnp.where(valid, jnp.exp(s - m), 0.0)
    es = jnp.exp(sink - m)
    den = jnp.sum(p, axis=1, keepdims=True) + es
    return p / den, es / den


def _swa_fwd(qa, ka, va, cos, sin, sinks):
    T = qa.shape[0]
    blk, cur, prev = _swa_specs(T)

    def body(sink_ref, q_ref, kp_ref, kc_ref, vp_ref, vc_ref, cp_ref, cc_ref, sp_ref, sc_ref, o_ref):
        i = pl.program_id(0)
        cc, sc = cc_ref[...], sc_ref[...]
        ks, vs, valid = _swa_window(i, kp_ref[...], kc_ref[...], vp_ref[...].astype(F32), vc_ref[...].astype(F32),
                                    cp_ref[...], cc, sp_ref[...], sc)
        lane = _lane_iota((blk, LANES))
        for pp in range(SWA_Q_HEADS // 2):
            g = pp // (SWA_Q_HEADS // SWA_KV_HEADS // 2)
            qp = _rope(q_ref[:, pp * LANES:(pp + 1) * LANES], cc, sc)
            out = jnp.zeros((blk, LANES), F32)
            for hh in range(2):
                half = (lane >= hh * HEAD_DIM) & (lane < (hh + 1) * HEAD_DIM)
                qh = jnp.where(half, qp, 0.0).astype(BF16)
                probs, _ = _swa_probs(qh, ks[g][hh].astype(BF16), valid, sink_ref[2 * pp + hh])
                out = out + _dot(probs.astype(BF16), vs[g][hh].astype(BF16))
            o_ref[:, pp * LANES:(pp + 1) * LANES] = out.astype(o_ref.dtype)

    return pl.pallas_call(
        body, name="swa_fwd", grid=(T // blk,),
        in_specs=[pl.BlockSpec(memory_space=pltpu.SMEM), cur(SWA_Q_WIDTH), prev(LANES), cur(LANES), prev(LANES), cur(LANES),
                  prev(LANES), cur(LANES), prev(LANES), cur(LANES)],
        out_specs=cur(SWA_Q_WIDTH),
        out_shape=jax.ShapeDtypeStruct((T, SWA_Q_WIDTH), BF16),
        compiler_params=_params(("parallel",)),
    )(sinks, qa, ka, ka, va, va, cos, cos, sin, sin)


def _swa_bwd(qa, ka, va, cos, sin, sinks, dya):
    T = qa.shape[0]
    blk, cur, prev = _swa_specs(T)
    full = lambda n: pl.BlockSpec((T, n), lambda i: (0, 0))

    def body(sink_ref, q_ref, kp_ref, kc_ref, vp_ref, vc_ref, cp_ref, cc_ref, sp_ref, sc_ref, do_ref,
             dq_ref, dk_out, dv_out, dsink_ref, dk_ref, dv_ref):
        i = pl.program_id(0)

        @pl.when(i == 0)
        def _():
            dk_ref[...] = jnp.zeros_like(dk_ref)
            dv_ref[...] = jnp.zeros_like(dv_ref)
            dsink_ref[...] = jnp.zeros_like(dsink_ref)

        cp, cc, sp, sc = cp_ref[...], cc_ref[...], sp_ref[...], sc_ref[...]
        ks, vs, valid = _swa_window(i, kp_ref[...], kc_ref[...], vp_ref[...].astype(F32), vc_ref[...].astype(F32), cp, cc, sp, sc)
        lane = _lane_iota((blk, LANES))
        lane1 = _lane_iota((1, LANES))
        dkw = jnp.zeros((2 * blk, LANES), F32)
        dvw = jnp.zeros((2 * blk, LANES), F32)
        dsink = jnp.zeros((1, LANES), F32)
        for pp in range(SWA_Q_HEADS // 2):
            g = pp // (SWA_Q_HEADS // SWA_KV_HEADS // 2)
            qp = _rope(q_ref[:, pp * LANES:(pp + 1) * LANES], cc, sc)
            dop = do_ref[:, pp * LANES:(pp + 1) * LANES]
            dqp = jnp.zeros((blk, LANES), F32)
            for hh in range(2):
                half = (lane >= hh * HEAD_DIM) & (lane < (hh + 1) * HEAD_DIM)
                qh = jnp.where(half, qp, 0.0).astype(BF16)
                doh = jnp.where(half, dop, 0.0).astype(BF16)
                kk = ks[g][hh].astype(BF16)
                vv = vs[g][hh].astype(BF16)
                probs, psink = _swa_probs(qh, kk, valid, sink_ref[2 * pp + hh])
                dp = _dot_nt(doh, vv)
                dsum = jnp.sum(probs * dp, axis=1, keepdims=True)
                ds = (probs * (dp - dsum) * ATTN_SCALE).astype(BF16)
                dsink = dsink + jnp.where(lane1 == 2 * pp + hh, -jnp.sum(psink * dsum), 0.0)
                dqp = dqp + _dot(ds, kk)
                dk_h = _dot_tn(ds, qh)
                dv_h = _dot_tn(probs.astype(BF16), doh)
                if hh != g:
                    dk_h = pltpu.roll(dk_h, HEAD_DIM, axis=1)
                    dv_h = pltpu.roll(dv_h, HEAD_DIM, axis=1)
                dkw = dkw + dk_h
                dvw = dvw + dv_h
            dq_ref[:, pp * LANES:(pp + 1) * LANES] = _rope_transpose(dqp, cc, sc).astype(dq_ref.dtype)
        dsink_ref[...] += dsink
        ip = jnp.maximum(i - 1, 0)
        rows_p = pl.ds(pl.multiple_of(ip * blk, blk), blk)
        rows_c = pl.ds(pl.multiple_of(i * blk, blk), blk)
        dk_ref[rows_p, :] += _rope_transpose(dkw[:blk], cp, sp)
        dv_ref[rows_p, :] += dvw[:blk]
        dk_ref[rows_c, :] += _rope_transpose(dkw[blk:], cc, sc)
        dv_ref[rows_c, :] += dvw[blk:]

        @pl.when(i == T // blk - 1)
        def _():
            dk_out[...] = dk_ref[...].astype(BF16)
            dv_out[...] = dv_ref[...].astype(BF16)

    return pl.pallas_call(
        body, name="swa_bwd", grid=(T // blk,),
        in_specs=[pl.BlockSpec(memory_space=pltpu.SMEM), cur(SWA_Q_WIDTH), prev(LANES), cur(LANES), prev(LANES), cur(LANES),
                  prev(LANES), cur(LANES), prev(LANES), cur(LANES), cur(SWA_Q_WIDTH)],
        out_specs=[cur(SWA_Q_WIDTH), full(LANES), full(LANES), pl.BlockSpec((1, LANES), lambda i: (0, 0))],
        out_shape=[jax.ShapeDtypeStruct((T, SWA_Q_WIDTH), BF16), jax.ShapeDtypeStruct((T, LANES), BF16),
                   jax.ShapeDtypeStruct((T, LANES), BF16), jax.ShapeDtypeStruct((1, LANES), F32)],
        scratch_shapes=[pltpu.VMEM((T, LANES), F32)] * 2,
        compiler_params=_params(("arbitrary",)),
    )(sinks, qa, ka, ka, va, va, cos, cos, sin, sin, dya)


class _Exchange:
    FLIPS = [(fx, fy, fc) for fx in (0, 1) for fy in (0, 1) for fc in (0, 1) if (fx, fy, fc) != (0, 0, 0)]

    def __init__(self, arrays, kinds):
        self.arrays, self.kinds, self.n = list(arrays), list(kinds), len(arrays)

    def out_shape(self):
        return [jax.ShapeDtypeStruct(a.shape if k == "scatter" else (N_DEV,) + a.shape, a.dtype) for a, k in zip(self.arrays, self.kinds)]

    def scratch(self):
        return [pltpu.SemaphoreType.DMA((self.n, 7)), pltpu.SemaphoreType.DMA((self.n, 7)), pltpu.SemaphoreType.DMA((self.n,))]

    def bind(self, ins, outs, send_sems, recv_sems, local_sems):
        x, y, c = lax.axis_index("x"), lax.axis_index("y"), lax.axis_index("c")
        me = 4 * x + 2 * y + c
        local, remote = [], []
        for a, kind in enumerate(self.kinds):
            mine = ins[a].at[me] if kind == "scatter" else ins[a]
            local.append(pltpu.make_async_copy(mine, outs[a].at[me], local_sems.at[a]))
            for k, (fx, fy, fc) in enumerate(self.FLIPS):
                peer = (x ^ fx, y ^ fy, c ^ fc)
                peer_slot = 4 * peer[0] + 2 * peer[1] + peer[2]
                src = ins[a].at[peer_slot] if kind == "scatter" else ins[a]
                sems = dict(send_sem=send_sems.at[a, k], recv_sem=recv_sems.at[a, k], device_id=peer, device_id_type=MESH)
                remote.append((pltpu.make_async_remote_copy(src_ref=src, dst_ref=outs[a].at[me], **sems),
                               pltpu.make_async_remote_copy(src_ref=src, dst_ref=outs[a].at[peer_slot], **sems)))

        def start():
            for cp in local:
                cp.start()
            for send, _ in remote:
                send.start()

        def wait():
            for send, arrival in remote:
                arrival.wait_recv()
                send.wait_send()
            for cp in local:
                cp.wait()

        return start, wait


def _hosted_call(body, name, grid, exchange, *, in_specs, out_specs, out_shape, semantics, args, scratch=()):
    if exchange is None:
        outs = pl.pallas_call(body, name=name, grid=grid, in_specs=in_specs, out_specs=out_specs, out_shape=out_shape,
                              scratch_shapes=list(scratch), compiler_params=_params(semantics))(*args)
        return outs, []
    n, n_in, n_out, n_scratch = exchange.n, len(in_specs), len(out_specs), len(scratch)

    def hosted(*refs):
        ins, rest = refs[:n_in], refs[n_in:]
        ex_ins, rest = rest[:n], rest[n:]
        outs, rest = rest[:n_out], rest[n_out:]
        ex_outs, rest = rest[:n], rest[n:]
        own, sems = rest[:n_scratch], rest[n_scratch:]
        start, wait = exchange.bind(ex_ins, ex_outs, *sems)
        ids = [pl.program_id(d) for d in range(len(grid))]
        first = functools.reduce(jnp.logical_and, [i == 0 for i in ids])
        last = functools.reduce(jnp.logical_and, [i == g - 1 for i, g in zip(ids, grid)])
        pl.when(first)(start)
        body(*ins, *outs, *own)
        pl.when(last)(wait)

    res = pl.pallas_call(
        hosted, name=name, grid=grid, in_specs=list(in_specs) + [ANY] * n, out_specs=list(out_specs) + [ANY] * n,
        out_shape=list(out_shape) + exchange.out_shape(), scratch_shapes=list(scratch) + exchange.scratch(),
        compiler_params=_params(("arbitrary",) * len(grid)),
    )(*args, *exchange.arrays)
    return res[:n_out], res[n_out:]


SOFTPLUS_LINEAR_FROM = 30.0


def _sb_scores(qm, k, valid):
    z = _dot_nt(qm, k)
    sp = jnp.where(z > SOFTPLUS_LINEAR_FROM, z, jnp.log(1.0 + jnp.exp(z)))
    log_beta = z - sp
    if valid is not None:
        sp = jnp.where(valid, sp, 0.0)
    return sp, log_beta


def _tri2(B, cmp):
    r = lax.broadcasted_iota(jnp.int32, (2 * B, B), 0) % B
    c = lax.broadcasted_iota(jnp.int32, (2 * B, B), 1)
    return cmp(r, c).astype(BF16)


def _tri_sum(v, tri2):
    hi, lo = _split_bf16(v)
    return _dot(jnp.concatenate([hi, lo], axis=1), tri2)


def _head_masks(x):
    low = _lane_iota(x.shape) < HEAD_DIM
    zero = jnp.zeros((), x.dtype)
    return jnp.where(low, x, zero), jnp.where(low, zero, x)


def _strictly_below(B):
    r = lax.broadcasted_iota(jnp.int32, (B, B), 0)
    c = lax.broadcasted_iota(jnp.int32, (B, B), 1)
    return c < r


def _sb_fwd(qb, kb, vb, exchange=None):
    T = qb.shape[0]
    B = min(SB_BLOCK, T)
    assert T // B <= HEAD_DIM
    n_pairs = SB_HEADS // 2

    W = SB_PAIRS * LANES

    def body(q_ref, k_ref, v_ref, o_ref, carry_ref):
        i = pl.program_id(1)
        lane = _lane_iota((1, LANES))
        upper2 = _tri2(B, lambda r, c: r > c)
        qms = [_head_masks(q_ref[:, p * LANES:(p + 1) * LANES] * ATTN_SCALE) for p in range(SB_PAIRS)]

        def pair_block(p, j, rows, state, valid):
            cs, acc, cm = state
            cols = slice(p * LANES, (p + 1) * LANES)
            k = k_ref[rows, cols]
            vms = _head_masks(v_ref[rows, cols])
            probs, new_cs = [], []
            for hh in range(2):
                sp, lb = _sb_scores(qms[p][hh], k, valid)
                cm = jnp.where(lane == hh * HEAD_DIM + j, cs[hh], cm)
                a = jnp.exp(lb - (cs[hh] + _tri_sum(sp, upper2)))
                if valid is not None:
                    a = jnp.where(valid, a, 0.0)
                probs.append(a.astype(BF16))
                new_cs.append(cs[hh] + jnp.sum(sp, axis=1, keepdims=True))
            acc = acc + _dot(jnp.concatenate(probs, axis=1), jnp.concatenate(vms, axis=0))
            return tuple(new_cs), acc, cm

        def block(j, states, valid):
            rows = pl.ds(pl.multiple_of(j * B, B), B)
            return tuple(pair_block(p, j, rows, states[p], valid) for p in range(SB_PAIRS))

        zero = jnp.zeros((B, 1), F32)
        start = ((zero, zero), jnp.zeros((B, LANES), F32), jnp.zeros((B, LANES), F32))
        states = block(i, (start,) * SB_PAIRS, _strictly_below(B))
        states = lax.fori_loop(0, i, lambda jj, s: block(i - 1 - jj, s, None), states)
        for p, (_, acc, cm) in enumerate(states):
            o_ref[:, p * LANES:(p + 1) * LANES] = acc.astype(o_ref.dtype)
            carry_ref[:, p * LANES:(p + 1) * LANES] = cm

    return _hosted_call(
        body, "sb_fwd", (n_pairs // SB_PAIRS, T // B), exchange,
        in_specs=[pl.BlockSpec((B, W), lambda p, i: (i, p)), pl.BlockSpec((T, W), lambda p, i: (0, p)),
                  pl.BlockSpec((T, W), lambda p, i: (0, p))],
        out_specs=[pl.BlockSpec((B, W), lambda p, i: (i, p))] * 2,
        out_shape=[jax.ShapeDtypeStruct((T, SB_WIDTH), BF16), jax.ShapeDtypeStruct((T, SB_WIDTH), F32)],
        semantics=("parallel", "parallel"), args=(qb, kb, vb))


def _sb_bwd(qb, kb, vb, carries, dyb, exchange=None):
    T = qb.shape[0]
    B = min(SB_BLOCK, T)
    n_pairs = SB_HEADS // 2

    def body(q_ref, k_ref, v_ref, carry_ref, do_ref, dq_ref, dk_out, dv_out, dk_ref, dv_ref):
        i = pl.program_id(1)

        @pl.when(i == 0)
        def _():
            dk_ref[...] = jnp.zeros_like(dk_ref)
            dv_ref[...] = jnp.zeros_like(dv_ref)

        lane = _lane_iota((1, LANES))
        upper2 = _tri2(B, lambda r, c: r > c)
        lower = _tri2(B, lambda r, c: r < c)[:B]
        pair_cols = [slice(p * LANES, (p + 1) * LANES) for p in range(SB_PAIRS)]
        qms = [_head_masks(q_ref[:, cols] * ATTN_SCALE) for cols in pair_cols]
        doms = [_head_masks(do_ref[:, cols]) for cols in pair_cols]
        q2s, do2s = [jnp.concatenate(m, axis=0) for m in qms], [jnp.concatenate(m, axis=0) for m in doms]
        cms = [carry_ref[:, cols] for cols in pair_cols]

        def pair_block(p, j, rows, state, valid):
            cgs, dq = state
            k = k_ref[rows, pair_cols[p]]
            v = v_ref[rows, pair_cols[p]]
            dzs, probs, new_cgs = [], [], []
            for hh in range(2):
                sp, lb = _sb_scores(qms[p][hh], k, valid)
                c = jnp.sum(jnp.where(lane == hh * HEAD_DIM + j, cms[p], 0.0), axis=1, keepdims=True)
                a = jnp.exp(lb - (c + _tri_sum(sp, upper2)))
                if valid is not None:
                    a = jnp.where(valid, a, 0.0)
                g = a * _dot_nt(doms[p][hh], v)
                gpre = cgs[hh] + _dot(g.astype(BF16), lower)
                dz = g - jnp.exp(lb) * (g + gpre)
                if valid is not None:
                    dz = jnp.where(valid, dz, 0.0)
                dzs.append(dz.astype(BF16))
                probs.append(a.astype(BF16))
                new_cgs.append(cgs[hh] + jnp.sum(g, axis=1, keepdims=True))
            dq = dq + _dot(jnp.concatenate(dzs, axis=1), jnp.concatenate(_head_masks(k), axis=0))
            dk_ref[rows, pair_cols[p]] += _dot_tn(jnp.concatenate(dzs, axis=0), q2s[p])
            dv_ref[rows, pair_cols[p]] += _dot_tn(jnp.concatenate(probs, axis=0), do2s[p])
            return tuple(new_cgs), dq

        def block(j, states, valid):
            rows = pl.ds(pl.multiple_of(j * B, B), B)
            return tuple(pair_block(p, j, rows, states[p], valid) for p in range(SB_PAIRS))

        zero = jnp.zeros((B, 1), F32)
        states = lax.fori_loop(0, i, lambda j, s: block(j, s, None), (((zero, zero), jnp.zeros((B, LANES), F32)),) * SB_PAIRS)
        states = block(i, states, _strictly_below(B))
        for cols, (_, dq) in zip(pair_cols, states):
            dq_ref[:, cols] = (dq * ATTN_SCALE).astype(dq_ref.dtype)

        @pl.when(i == T // B - 1)
        def _():
            dk_out[...] = dk_ref[...].astype(BF16)
            dv_out[...] = dv_ref[...].astype(BF16)

    W = SB_PAIRS * LANES
    blk = pl.BlockSpec((B, W), lambda p, i: (i, p))
    full = pl.BlockSpec((T, W), lambda p, i: (0, p))
    return _hosted_call(
        body, "sb_bwd", (n_pairs // SB_PAIRS, T // B), exchange,
        in_specs=[blk, full, full, blk, blk],
        out_specs=[blk, full, full],
        out_shape=[jax.ShapeDtypeStruct((T, SB_WIDTH), BF16)] * 3,
        scratch=[pltpu.VMEM((T, W), F32)] * 2,
        semantics=("parallel", "arbitrary"), args=(qb, kb, vb, carries, dyb))


def _ln_stats(u):
    mu = jnp.mean(u, axis=-1, keepdims=True)
    xc = u - mu
    var = jnp.mean(xc * xc, axis=-1, keepdims=True)
    rstd = lax.rsqrt(var + LN_EPS)
    return xc * rstd, rstd


def _ln_bwd(dy, xhat, rstd, g):
    dxh = dy * g
    return rstd * (dxh - jnp.mean(dxh, axis=-1, keepdims=True) - xhat * jnp.mean(dxh * xhat, axis=-1, keepdims=True))


def _gates(gl_ref, bg_ref):
    ga = jax.nn.sigmoid(gl_ref[:, :D_MODEL] + bg_ref[:, :D_MODEL])
    gb = jax.nn.sigmoid(gl_ref[:, D_MODEL:] + bg_ref[:, D_MODEL:])
    return ga, gb


def _mix_fwd(ya, yb, gl, x, wa, wb, wo, b_gate, ln1_g, ln1_b):
    T = x.shape[0]
    tm = min(256, T)

    def body(ya_ref, yb_ref, gl_ref, x_ref, wa_ref, wb_ref, wo_ref, bg_ref, g_ref, b_ref, h_ref, u_ref, x1_ref):
        ga, gb = _gates(gl_ref, bg_ref)
        h = (ga * _dot(ya_ref[...], wa_ref[...]) + gb * _dot(yb_ref[...], wb_ref[...])).astype(BF16)
        h_ref[...] = h
        u = ALPHA * x_ref[...] + _dot(h, wo_ref[...])
        u_ref[...] = u
        xhat, _ = _ln_stats(u)
        x1_ref[...] = (xhat * g_ref[...] + b_ref[...]).astype(BF16)

    row = lambda n: pl.BlockSpec((tm, n), lambda i: (i, 0))
    const = lambda r, n: pl.BlockSpec((r, n), lambda i: (0, 0))
    return pl.pallas_call(
        body, name="mix_fwd", grid=(T // tm,),
        in_specs=[row(SWA_Q_WIDTH), row(SB_WIDTH), row(GATE_WIDTH), row(D_MODEL), const(SWA_Q_WIDTH, D_MODEL), const(SB_WIDTH, D_MODEL),
                  const(D_MODEL, D_MODEL), const(1, GATE_WIDTH), const(1, D_MODEL), const(1, D_MODEL)],
        out_specs=[row(D_MODEL)] * 3,
        out_shape=[jax.ShapeDtypeStruct((T, D_MODEL), BF16), jax.ShapeDtypeStruct((T, D_MODEL), F32), jax.ShapeDtypeStruct((T, D_MODEL), BF16)],
        compiler_params=_params(("parallel",)),
    )(ya, yb, gl, x, wa, wb, wo, b_gate, ln1_g, ln1_b)


def _mix_bwd(du1, ya, yb, gl, wa, wb, wo, b_gate):
    T = du1.shape[0]
    tm = min(256, T)

    def body(du_ref, ya_ref, yb_ref, gl_ref, wa_ref, wb_ref, wo_ref, bg_ref, dya_ref, dyb_ref, dgl_ref, dta_ref, dtb_ref, dbg_ref):
        @pl.when(pl.program_id(0) == 0)
        def _():
            dbg_ref[...] = jnp.zeros_like(dbg_ref)

        dh = _dot_nt(du_ref[...].astype(BF16), wo_ref[...])
        ga, gb = _gates(gl_ref, bg_ref)
        for gate, y_ref, w_ref, dy_ref, dt_ref, lo in ((ga, ya_ref, wa_ref, dya_ref, dta_ref, 0), (gb, yb_ref, wb_ref, dyb_ref, dtb_ref, D_MODEL)):
            t = _dot(y_ref[...], w_ref[...])
            dlogit = dh * t * gate * (1.0 - gate)
            dgl_ref[:, lo:lo + D_MODEL] = dlogit.astype(BF16)
            dbg_ref[:, lo:lo + D_MODEL] += jnp.sum(dlogit, axis=0, keepdims=True)
            dt = (dh * gate).astype(BF16)
            dt_ref[...] = dt
            dy_ref[...] = _dot_nt(dt, w_ref[...]).astype(BF16)

    row = lambda n: pl.BlockSpec((tm, n), lambda i: (i, 0))
    const = lambda r, n: pl.BlockSpec((r, n), lambda i: (0, 0))
    sds = lambda n, dt: jax.ShapeDtypeStruct((T, n), dt)
    return pl.pallas_call(
        body, name="mix_bwd", grid=(T // tm,),
        in_specs=[row(D_MODEL), row(SWA_Q_WIDTH), row(SB_WIDTH), row(GATE_WIDTH), const(SWA_Q_WIDTH, D_MODEL), const(SB_WIDTH, D_MODEL),
                  const(D_MODEL, D_MODEL), const(1, GATE_WIDTH)],
        out_specs=[row(SWA_Q_WIDTH), row(SB_WIDTH), row(GATE_WIDTH), row(D_MODEL), row(D_MODEL), const(1, GATE_WIDTH)],
        out_shape=[sds(SWA_Q_WIDTH, BF16), sds(SB_WIDTH, BF16), sds(GATE_WIDTH, BF16), sds(D_MODEL, BF16), sds(D_MODEL, BF16),
                   jax.ShapeDtypeStruct((1, GATE_WIDTH), F32)],
        compiler_params=_params(("arbitrary",)),
    )(du1, ya, yb, gl, wa, wb, wo, b_gate)


CONV_COLS = LANES


CONV_CHUNK = 64
CONV_CHUNK_FWD = 256
HALO = 8


def _taps(ref, r0, rows, lead):
    return [ref[pl.ds(r0 + lead + k, rows), :] for k in ((-2, -1, 0) if lead else (0, 1, 2))]


def _chunks(T, rows, step, init=None):
    def body(c, carry):
        out = step(pl.multiple_of(c * rows, rows), *(() if init is None else (carry,)))
        return carry if init is None else out
    return lax.fori_loop(0, T // rows, body, 0 if init is None else init)


def _conv_chunk(taps, w_ref, b_ref):
    return w_ref[0:1, :] * taps[0] + w_ref[1:2, :] * taps[1] + w_ref[2:3, :] * taps[2] + b_ref[...]


def _fold(x):
    return jnp.sum(x.reshape(x.shape[0] // 8, 8, x.shape[1]), axis=0)


def _conv_specs(T):
    nb = D_FF // CONV_COLS
    pair = pl.BlockSpec((2, T, CONV_COLS), lambda j: (0, 0, j))
    gate = lambda r: pl.BlockSpec((r, CONV_COLS), lambda j: (0, j))
    up = lambda r: pl.BlockSpec((r, CONV_COLS), lambda j: (0, j + nb))
    return nb, pair, gate, up


def _conv_glu_fwd(p3, conv_w, conv_b):
    T = p3.shape[1]
    nb, pair, gate, up = _conv_specs(T)

    R = min(CONV_CHUNK_FWD, T)

    def body(p_ref, wg_ref, wu_ref, bg_ref, bu_ref, s_ref, pg_s, pu_s):
        for half, scr in enumerate((pg_s, pu_s)):
            scr[0:HALO, :] = jnp.zeros((HALO, CONV_COLS), F32)
            scr[HALO:HALO + T, :] = p_ref[half].astype(F32)
        def step(r0):
            ag = _conv_chunk(_taps(pg_s, r0, R, HALO), wg_ref, bg_ref)
            au = _conv_chunk(_taps(pu_s, r0, R, HALO), wu_ref, bu_ref)
            s_ref[pl.ds(r0, R), :] = (ag * jax.nn.sigmoid(ag) * au).astype(BF16)

        _chunks(T, R, step)

    return pl.pallas_call(
        body, name="conv_glu_fwd", grid=(nb,),
        in_specs=[pair, gate(3), up(3), gate(1), up(1)],
        out_specs=pl.BlockSpec((T, CONV_COLS), lambda j: (0, j)),
        out_shape=jax.ShapeDtypeStruct((T, D_FF), BF16),
        scratch_shapes=[pltpu.VMEM((T + HALO, CONV_COLS), F32)] * 2,
        compiler_params=_params(("parallel",)),
    )(p3, conv_w, conv_w, conv_b, conv_b)


def _conv_glu_bwd(p3, ds, conv_w, conv_b):
    T = p3.shape[1]
    nb, pair, gate, up = _conv_specs(T)

    R = min(CONV_CHUNK, T)

    def body(p_ref, ds_ref, wg_ref, wu_ref, bg_ref, bu_ref, dp_ref, dwg_ref, dwu_ref, dbg_ref, dbu_ref, pg_s, pu_s, dag_s, dau_s):
        for half, scr in enumerate((pg_s, pu_s)):
            scr[0:HALO, :] = jnp.zeros((HALO, CONV_COLS), F32)
            scr[HALO:HALO + T, :] = p_ref[half].astype(F32)
        for scr in (dag_s, dau_s):
            scr[T:T + HALO, :] = jnp.zeros((HALO, CONV_COLS), F32)
        halves = ((pg_s, dag_s, wg_ref, dwg_ref, dbg_ref), (pu_s, dau_s, wu_ref, dwu_ref, dbu_ref))

        def step(r0, sums):
            taps = [_taps(p_s, r0, R, HALO) for p_s, *_ in halves]
            ag = _conv_chunk(taps[0], wg_ref, bg_ref)
            au = _conv_chunk(taps[1], wu_ref, bu_ref)
            sg = jax.nn.sigmoid(ag)
            d = ds_ref[pl.ds(r0, R), :].astype(F32)
            das = (d * au * (sg * (1.0 + ag * (1.0 - sg))), d * ag * sg)
            out = []
            for half, (_, da_s, *_) in enumerate(halves):
                da_s[pl.ds(r0, R), :] = das[half]
                out.append(tuple(sums[half][k] + _fold(das[half] * taps[half][k]) for k in range(3)) + (sums[half][3] + _fold(das[half]),))
            return tuple(out)

        sums = _chunks(T, R, step, ((jnp.zeros((8, CONV_COLS), F32),) * 4,) * 2)
        for half, (_, da_s, w_ref, dw_ref, db_ref) in enumerate(halves):
            for k in range(3):
                dw_ref[k:k + 1, :] = jnp.sum(sums[half][k], axis=0, keepdims=True)
            db_ref[...] = jnp.sum(sums[half][3], axis=0, keepdims=True)

            def transposed(r0, da_s=da_s, w_ref=w_ref, half=half):
                da0, da1, da2 = _taps(da_s, r0, R, 0)
                dp_ref[half, pl.ds(r0, R), :] = (w_ref[2:3, :] * da0 + w_ref[1:2, :] * da1 + w_ref[0:1, :] * da2).astype(BF16)

            _chunks(T, R, transposed)

    col = lambda r: pl.BlockSpec((r, CONV_COLS), lambda j: (0, j))
    return pl.pallas_call(
        body, name="conv_glu_bwd", grid=(nb,),
        in_specs=[pair, col(T), gate(3), up(3), gate(1), up(1)],
        out_specs=[pair, col(3), col(3), col(1), col(1)],
        out_shape=[jax.ShapeDtypeStruct((2, T, D_FF), BF16), jax.ShapeDtypeStruct((3, D_FF), F32), jax.ShapeDtypeStruct((3, D_FF), F32),
                   jax.ShapeDtypeStruct((1, D_FF), F32), jax.ShapeDtypeStruct((1, D_FF), F32)],
        scratch_shapes=[pltpu.VMEM((T + HALO, CONV_COLS), F32)] * 4,
        compiler_params=_params(("parallel",)),
    )(p3, ds, conv_w, conv_w, conv_b, conv_b)


def _ffn_down_loss(s, w_down, u1, ln1_g, ln1_b, ln2_g, ln2_b, target):
    T = u1.shape[0]
    tm = min(256, T)

    def body(s_ref, w_ref, u1_ref, g1_ref, b1_ref, g2_ref, b2_ref, t_ref, du_ref, dub_ref, dg_ref, db_ref, loss_ref):
        @pl.when(pl.program_id(0) == 0)
        def _():
            dg_ref[...] = jnp.zeros_like(dg_ref)
            db_ref[...] = jnp.zeros_like(db_ref)
            loss_ref[...] = jnp.zeros_like(loss_ref)

        xh1, _ = _ln_stats(u1_ref[...])
        x1 = xh1 * g1_ref[...] + b1_ref[...]
        u2 = ALPHA * x1 + _dot(s_ref[...], w_ref[...])
        xh2, rstd2 = _ln_stats(u2)
        err = xh2 * g2_ref[...] + b2_ref[...] - t_ref[...]
        per_token = jnp.mean(err * err, axis=-1, keepdims=True)
        loss_ref[...] += 0.5 * jnp.sum(per_token, axis=0, keepdims=True)
        dy = err * (1.0 / D_MODEL)
        dg_ref[...] += jnp.sum(dy * xh2, axis=0, keepdims=True)
        db_ref[...] += jnp.sum(dy, axis=0, keepdims=True)
        du2 = _ln_bwd(dy, xh2, rstd2, g2_ref[...])
        du_ref[...] = du2
        dub_ref[...] = du2.astype(BF16)

    row = lambda n: pl.BlockSpec((tm, n), lambda i: (i, 0))
    const = lambda r, n: pl.BlockSpec((r, n), lambda i: (0, 0))
    vec = const(1, D_MODEL)
    return pl.pallas_call(
        body, name="ffn_down_loss", grid=(T // tm,),
        in_specs=[row(D_FF), const(D_FF, D_MODEL), row(D_MODEL), vec, vec, vec, vec, row(D_MODEL)],
        out_specs=[row(D_MODEL), row(D_MODEL), vec, vec, const(1, LANES)],
        out_shape=[jax.ShapeDtypeStruct((T, D_MODEL), F32), jax.ShapeDtypeStruct((T, D_MODEL), BF16), jax.ShapeDtypeStruct((1, D_MODEL), F32),
                   jax.ShapeDtypeStruct((1, D_MODEL), F32), jax.ShapeDtypeStruct((1, LANES), F32)],
        compiler_params=_params(("arbitrary",)),
    )(s, w_down, u1, ln1_g, ln1_b, ln2_g, ln2_b, target)


def _ffn_up_bwd_ln1(dp3, w_up, du2, u1, ln1_g):
    T = u1.shape[0]
    tm = min(256, T)

    def body(dp_ref, w_ref, du2_ref, u1_ref, g_ref, du_ref, dub_ref, dg_ref, db_ref):
        @pl.when(pl.program_id(0) == 0)
        def _():
            dg_ref[...] = jnp.zeros_like(dg_ref)
            db_ref[...] = jnp.zeros_like(db_ref)

        dx1 = _dot_nt(dp_ref[0], w_ref[:, :D_FF]) + _dot_nt(dp_ref[1], w_ref[:, D_FF:]) + ALPHA * du2_ref[...]
        xh, rstd = _ln_stats(u1_ref[...])
        dg_ref[...] += jnp.sum(dx1 * xh, axis=0, keepdims=True)
        db_ref[...] += jnp.sum(dx1, axis=0, keepdims=True)
        du1 = _ln_bwd(dx1, xh, rstd, g_ref[...])
        du_ref[...] = du1
        dub_ref[...] = du1.astype(BF16)

    row = lambda n: pl.BlockSpec((tm, n), lambda i: (i, 0))
    const = lambda r, n: pl.BlockSpec((r, n), lambda i: (0, 0))
    vec = const(1, D_MODEL)
    return pl.pallas_call(
        body, name="ffn_up_bwd_ln1", grid=(T // tm,),
        in_specs=[pl.BlockSpec((2, tm, D_FF), lambda i: (0, i, 0)), const(D_MODEL, 2 * D_FF), row(D_MODEL), row(D_MODEL), vec],
        out_specs=[row(D_MODEL), row(D_MODEL), vec, vec],
        out_shape=[jax.ShapeDtypeStruct((T, D_MODEL), F32), jax.ShapeDtypeStruct((T, D_MODEL), BF16), jax.ShapeDtypeStruct((1, D_MODEL), F32),
                   jax.ShapeDtypeStruct((1, D_MODEL), F32)],
        compiler_params=_params(("arbitrary",)),
    )(dp3, w_up, du2, u1, ln1_g)


def _local_step(x, positions, w_in, b_gate, sinks, ln1_g, ln1_b, conv_b, ln2_g, ln2_b, target, later_weights,
                early_exchange=None, tail_exchange=None):
    T = x.shape[0]
    inv_freq = 1.0 / (ROPE_THETA ** (jnp.arange(0, HEAD_DIM, 2, dtype=F32) / HEAD_DIM))
    cos, sin = _rope_tables(positions.reshape(T, 1), jnp.tile(inv_freq, LANES // (HEAD_DIM // 2)).reshape(1, LANES))

    xb, qa, ka, va, qb, kb, vb, gl = _in_proj(x, w_in)
    ya = _swa_fwd(qa, ka, va, cos, sin, sinks)
    if isinstance(later_weights, tuple):
        exchange, finish = later_weights
        (yb, carries), arrived = _sb_fwd(qb, kb, vb, exchange)
        later_weights = finish(arrived)
    else:
        (yb, carries), _ = _sb_fwd(qb, kb, vb)
    wa, wb, wo, w_up, conv_w, w_down = later_weights
    h, u1, x1 = _mix_fwd(ya, yb, gl, x, wa, wb, wo, b_gate, ln1_g, ln1_b)

    ff_tn = D_FF // 2
    nff = D_FF // ff_tn
    tm = min(512, T)
    p3 = _matmul(x1, w_up, kind="nn", name="ffn_up", grid=(T // tm, 2 * nff),
                 a_spec=pl.BlockSpec((tm, D_MODEL), lambda i, j: (i, 0)), b_spec=pl.BlockSpec((D_MODEL, ff_tn), lambda i, j: (0, j)),
                 out_spec=pl.BlockSpec((None, tm, ff_tn), lambda i, j: (j // nff, i, j % nff)),
                 out_shape=jax.ShapeDtypeStruct((2, T, D_FF), ACT_DTYPE))
    s = _conv_glu_fwd(p3, conv_w, conv_b)
    du2, du2b, dln2_g, dln2_b, loss = _ffn_down_loss(s, w_down, u1, ln1_g, ln1_b, ln2_g, ln2_b, target)

    ds = _matmul(du2b, w_down, kind="nt", name="ffn_down_bwd", grid=(T // tm, nff),
                 a_spec=pl.BlockSpec((tm, D_MODEL), lambda i, j: (i, 0)), b_spec=pl.BlockSpec((ff_tn, D_MODEL), lambda i, j: (j, 0)),
                 out_spec=pl.BlockSpec((tm, ff_tn), lambda i, j: (i, j)), out_shape=jax.ShapeDtypeStruct((T, D_FF), ACT_DTYPE))
    dp3, dcw_g, dcw_u, dcb_g, dcb_u = _conv_glu_bwd(p3, ds, conv_w, conv_b)
    tk = 256
    dw_down = _matmul(s, du2b, kind="tn", name="dw_down", grid=(D_FF // tk,),
                      a_spec=pl.BlockSpec((T, tk), lambda i: (0, i)), b_spec=pl.BlockSpec((T, D_MODEL), lambda i: (0, 0)),
                      out_spec=pl.BlockSpec((tk, D_MODEL), lambda i: (i, 0)), out_shape=jax.ShapeDtypeStruct((D_FF, D_MODEL), BF16))
    dw_up = _matmul(x1, dp3, kind="tn", name="dw_up", grid=(D_MODEL // 512, 2 * nff),
                    a_spec=pl.BlockSpec((T, 512), lambda i, j: (0, i)), b_spec=pl.BlockSpec((None, T, ff_tn), lambda i, j: (j // nff, 0, j % nff)),
                    out_spec=pl.BlockSpec((512, ff_tn), lambda i, j: (i, j)), out_shape=jax.ShapeDtypeStruct((D_MODEL, 2 * D_FF), BF16))
    du1, du1b, dln1_g, dln1_b = _ffn_up_bwd_ln1(dp3, w_up, du2, u1, ln1_g)
    dya, dyb, dgl, dta, dtb, db_gate = _mix_bwd(du1, ya, yb, gl, wa, wb, wo, b_gate)

    def dw_tn(a, g, name):
        rows, cols = a.shape[1], g.shape[1]
        tn = min(512, cols)
        return _matmul(a, g, kind="tn", name=name, grid=(rows // 512, cols // tn),
                       a_spec=pl.BlockSpec((T, 512), lambda i, j: (0, i)), b_spec=pl.BlockSpec((T, tn), lambda i, j: (0, j)),
                       out_spec=pl.BlockSpec((512, tn), lambda i, j: (i, j)), out_shape=jax.ShapeDtypeStruct((rows, cols), BF16))

    dwa = dw_tn(ya, dta, "dw_branch_a")
    dwb = dw_tn(yb, dtb, "dw_branch_b")
    dwo = dw_tn(h, du1b, "dw_out")

    grads = dict(
        b_gate=db_gate, w_branch_a=dwa, w_branch_b=dwb, w_out=dwo, ln1_g=dln1_g, ln1_b=dln1_b,
        w_up=dw_up, conv_w=jnp.concatenate([dcw_g, dcw_u], axis=1), conv_b=(dcb_g, dcb_u), w_down=dw_down, ln2_g=dln2_g, ln2_b=dln2_b)
    (dqb, dkb, dvb), early_out = _sb_bwd(qb, kb, vb, carries, dyb, early_exchange(grads) if early_exchange else None)
    dqa, dka, dva, grads["sinks"] = _swa_bwd(qa, ka, va, cos, sin, sinks, dya)
    dproj = (dqa, dka, dva, dqb, dkb, dvb, dgl)
    grads["w_in"] = _dw_in(xb, dproj)
    grad_x, tail_out = _grad_x(dproj, w_in, du1, tail_exchange(grads, loss) if tail_exchange else None)
    return loss, grad_x, grads, early_out, tail_out


def _dw_in(xb, dproj):
    T = xb.shape[0]
    tn = 2 * LANES
    groups, start, k = [], 0, 0
    while k < len(IN_WIDTHS):
        if IN_WIDTHS[k] >= tn:
            groups.append((start, IN_WIDTHS[k] // tn, [(k, 0, tn)]))
            k += 1
        else:
            members, off = [], 0
            while off < tn:
                members.append((k, off, IN_WIDTHS[k]))
                off += IN_WIDTHS[k]
                k += 1
            groups.append((start, 1, members))
        start += groups[-1][1]

    def body(x_ref, *refs):
        pieces, o_ref = refs[:-1], refs[-1]
        j = pl.program_id(0)
        for first, steps, members in groups:
            @pl.when((j >= first) & (j < first + steps))
            def _(members=members):
                for k, off, width in members:
                    o_ref[:, off:off + width] = _dot_tn(x_ref[...], pieces[k][...]).astype(o_ref.dtype)

    specs = [None] * len(IN_WIDTHS)
    for first, steps, members in groups:
        for k, _, width in members:
            specs[k] = pl.BlockSpec((T, width), lambda j, first=first, steps=steps: (0, jnp.clip(j - first, 0, steps - 1)))
    return pl.pallas_call(
        body, name="dw_in", grid=(IN_TOTAL // tn,),
        in_specs=[pl.BlockSpec((T, D_MODEL), lambda j: (0, 0))] + specs, out_specs=pl.BlockSpec((D_MODEL, tn), lambda j: (0, j)),
        out_shape=jax.ShapeDtypeStruct((D_MODEL, IN_TOTAL), BF16), compiler_params=_params(("arbitrary",)),
    )(xb, *dproj)


def _grad_x(dproj, w_in, du1, exchange=None):
    T = du1.shape[0]
    tm = min(256, T)
    offs = np.cumsum((0,) + IN_WIDTHS)

    def body(*refs):
        pieces, (w_ref, du_ref, o_ref) = refs[:len(IN_WIDTHS)], refs[len(IN_WIDTHS):]
        acc = ALPHA * du_ref[...]
        for p_ref, a, b in zip(pieces, offs[:-1], offs[1:]):
            acc = acc + _dot_nt(p_ref[...].astype(BF16), w_ref[:, a:b])
        o_ref[...] = acc

    row = lambda n: pl.BlockSpec((tm, n), lambda i: (i, 0))
    (grad_x,), arrived = _hosted_call(
        body, "grad_x", (T // tm,), exchange,
        in_specs=[row(n) for n in IN_WIDTHS] + [pl.BlockSpec((D_MODEL, IN_TOTAL), lambda i: (0, 0)), row(D_MODEL)],
        out_specs=[row(D_MODEL)], out_shape=[jax.ShapeDtypeStruct((T, D_MODEL), F32)], semantics=("parallel",),
        args=(*dproj, w_in, du1))
    return grad_x, arrived


ANY = pl.BlockSpec(memory_space=pl.ANY)


def _all_gather(slabs, name):
    n = len(slabs)

    def body(*refs):
        ins, outs = refs[:n], refs[n:2 * n]
        send_sems, recv_sems, local_sems = refs[2 * n:]
        x, y, c = lax.axis_index("x"), lax.axis_index("y"), lax.axis_index("c")
        me, sibling = (x, y, c), (x, y, 1 - c)
        chips = [(1 - x, y), (x, 1 - y), (1 - x, 1 - y)]

        def slot(pos):
            return 4 * pos[0] + 2 * pos[1] + pos[2]

        def copy(a, k, block, to, from_input=False):
            return pltpu.make_async_remote_copy(
                src_ref=ins[a] if from_input else outs[a].at[slot(block)], dst_ref=outs[a].at[slot(block)],
                send_sem=send_sems.at[a, k], recv_sem=recv_sems.at[a, k], device_id=to, device_id_type=MESH)

        mine = [pltpu.make_async_copy(ins[a], outs[a].at[slot(me)], local_sems.at[a]) for a in range(n)]
        for cp in mine:
            cp.start()
        first = []
        for a in range(n):
            first.append(copy(a, 0, me, sibling, from_input=True))
            first += [copy(a, 1 + j, me, (*chip, c), from_input=True) for j, chip in enumerate(chips)]
        for cp in first:
            cp.start()
        passed = []
        for j, chip in enumerate(chips):
            for a in range(n):
                copy(a, 1 + j, (*chip, c), me).wait_recv()
                fwd = copy(a, 4 + j, (*chip, c), sibling)
                fwd.start()
                passed.append(fwd)
        for a in range(n):
            copy(a, 0, sibling, me).wait_recv()
            for j, chip in enumerate(chips):
                copy(a, 4 + j, (*chip, 1 - c), me).wait_recv()
        for cp in first + passed:
            cp.wait_send()
        for cp in mine:
            cp.wait()

    return pl.pallas_call(
        body, name=name,
        in_specs=[ANY] * n, out_specs=[ANY] * n,
        out_shape=[jax.ShapeDtypeStruct((N_DEV,) + s.shape, s.dtype) for s in slabs],
        scratch_shapes=[pltpu.SemaphoreType.DMA((n, 7)), pltpu.SemaphoreType.DMA((n, 7)), pltpu.SemaphoreType.DMA((n,))],
    )(*slabs)


def _all_to_all(slabs, name):
    n = len(slabs)

    def body(*refs):
        ins, outs = refs[:n], refs[n:2 * n]
        send_sems, recv_sems, local_sems = refs[2 * n:]
        x, y, c = lax.axis_index("x"), lax.axis_index("y"), lax.axis_index("c")
        my_slot = 4 * x + 2 * y + c
        flips = [(fx, fy, fc) for fx in (0, 1) for fy in (0, 1) for fc in (0, 1) if (fx, fy, fc) != (0, 0, 0)]

        def copy(a, k):
            fx, fy, fc = flips[k]
            peer = (x ^ fx, y ^ fy, c ^ fc)
            peer_slot = 4 * peer[0] + 2 * peer[1] + peer[2]
            send = pltpu.make_async_remote_copy(src_ref=ins[a].at[peer_slot], dst_ref=outs[a].at[my_slot], send_sem=send_sems.at[a, k],
                                                recv_sem=recv_sems.at[a, k], device_id=peer, device_id_type=MESH)
            recv = pltpu.make_async_remote_copy(src_ref=ins[a].at[peer_slot], dst_ref=outs[a].at[peer_slot], send_sem=send_sems.at[a, k],
                                                recv_sem=recv_sems.at[a, k], device_id=peer, device_id_type=MESH)
            return send, recv

        mine = [pltpu.make_async_copy(ins[a].at[my_slot], outs[a].at[my_slot], local_sems.at[a]) for a in range(n)]
        for cp in mine:
            cp.start()
        copies = [copy(a, k) for a in range(n) for k in range(len(flips))]
        for send, _ in copies:
            send.start()
        for send, recv in copies:
            recv.wait_recv()
            send.wait_send()
        for cp in mine:
            cp.wait()

    return pl.pallas_call(
        body, name=name,
        in_specs=[ANY] * n, out_specs=[ANY] * n,
        out_shape=[jax.ShapeDtypeStruct(s.shape, s.dtype) for s in slabs],
        scratch_shapes=[pltpu.SemaphoreType.DMA((n, 7)), pltpu.SemaphoreType.DMA((n, 7)), pltpu.SemaphoreType.DMA((n,))],
    )(*slabs)


def _row_tile(rows):
    for cand in range(256, 7, -8):
        if rows % cand == 0:
            return cand
    return rows


def _window(w):
    wp = max(-(-((w * r) % LANES + w) // LANES) for r in range(N_DEV)) * LANES
    assert all((w * r) // LANES * LANES + wp <= N_DEV * w for r in range(N_DEV))
    return wp


def _join_cols(slabs, name):
    _, R, w = slabs.shape
    tr = _row_tile(R)
    wp = _window(w)

    def body(g_ref, o_ref, pad_ref):
        if w % LANES == 0:
            for r in range(N_DEV):
                o_ref[:, w * r:w * (r + 1)] = g_ref[r]
            return
        o_ref[...] = jnp.zeros_like(o_ref)
        pad_ref[...] = jnp.zeros_like(pad_ref)
        for r in range(N_DEV):
            q, s = divmod(w * r, LANES)
            pad_ref[:, :w] = g_ref[r]
            y = pad_ref[...]
            if s:
                y = pltpu.roll(y, s, axis=1)
            o_ref[:, LANES * q:LANES * q + wp] += y

    return pl.pallas_call(
        body, name=name, grid=(R // tr,),
        in_specs=[pl.BlockSpec((N_DEV, tr, w), lambda i: (0, i, 0))], out_specs=pl.BlockSpec((tr, N_DEV * w), lambda i: (i, 0)),
        out_shape=jax.ShapeDtypeStruct((R, N_DEV * w), slabs.dtype), scratch_shapes=[pltpu.VMEM((tr, wp), slabs.dtype)],
        compiler_params=_params(("parallel",)),
    )(slabs)


def _split_cols(pieces, name):
    R = pieces[0].shape[0]
    widths = [p.shape[1] for p in pieces]
    total = sum(widths)
    w = total // N_DEV
    tr = _row_tile(R)
    wp = _window(w)
    offs = np.cumsum([0] + widths)
    dtype = pieces[0].dtype

    def body(*refs):
        ins, (o_ref, full_ref) = refs[:len(pieces)], refs[len(pieces):]
        for p_ref, a, b in zip(ins, offs[:-1], offs[1:]):
            full_ref[:, a:b] = p_ref[...].astype(dtype)
        for r in range(N_DEV):
            q, s = divmod(w * r, LANES)
            y = full_ref[:, LANES * q:LANES * q + wp]
            if s:
                y = pltpu.roll(y, wp - s, axis=1)
            o_ref[r] = y[:, :w]

    return pl.pallas_call(
        body, name=name, grid=(R // tr,),
        in_specs=[pl.BlockSpec((tr, n), lambda i: (i, 0)) for n in widths], out_specs=pl.BlockSpec((N_DEV, tr, w), lambda i: (0, i, 0)),
        out_shape=jax.ShapeDtypeStruct((N_DEV, R, w), dtype), scratch_shapes=[pltpu.VMEM((tr, total), dtype)],
        compiler_params=_params(("parallel",)),
    )(*pieces)


def _adamw(g, w, m, v):
    m_new = ADAM_B1 * m + (1.0 - ADAM_B1) * g
    v_new = ADAM_B2 * v + (1.0 - ADAM_B2) * jnp.square(g)
    m_hat = m_new / (1.0 - ADAM_B1 ** ADAM_STEP)
    v_hat = v_new / (1.0 - ADAM_B2 ** ADAM_STEP)
    return -ADAM_LR * (m_hat / (jnp.sqrt(v_hat) + ADAM_EPS) + ADAM_WD * w), m_new, v_new


def _sum_parts(p_ref):
    g = p_ref[0].astype(F32)
    for d in range(1, N_DEV):
        g = g + p_ref[d].astype(F32)
    return g


def _reduce_adamw(parts, w, m, v, name):
    R, C = w.shape
    tr = _row_tile(R)

    def body(p_ref, w_ref, m_ref, v_ref, g_ref, d_ref, mo_ref, vo_ref):
        g = _sum_parts(p_ref)
        g_ref[...] = g
        d_ref[...], mo_ref[...], vo_ref[...] = _adamw(g, w_ref[...], m_ref[...], v_ref[...])

    row = pl.BlockSpec((tr, C), lambda i: (i, 0))
    return pl.pallas_call(
        body, name=name, grid=(R // tr,),
        in_specs=[pl.BlockSpec((N_DEV, tr, C), lambda i: (0, i, 0)), row, row, row],
        out_specs=[row] * 4, out_shape=[jax.ShapeDtypeStruct((R, C), F32)] * 4,
        compiler_params=_params(("parallel",)),
    )(parts, w, m, v)


def _reduce_adamw_small(parts, ws, ms, vs):
    sizes = [a.shape[1] for a in ws]
    k = len(sizes)
    offs = np.cumsum([0] + [-(-n // LANES) * LANES for n in sizes])

    def body(*refs):
        p_ref, w_refs, m_refs, v_refs = refs[0], refs[1:1 + k], refs[1 + k:1 + 2 * k], refs[1 + 2 * k:1 + 3 * k]
        outs, loss_ref = refs[1 + 3 * k:-1], refs[-1]
        g_all = _sum_parts(p_ref)
        for j, n in enumerate(sizes):
            g = g_all[:, offs[j]:offs[j] + LANES * (-(-n // LANES))][:, :n]
            outs[4 * j][...] = g
            outs[4 * j + 1][...], outs[4 * j + 2][...], outs[4 * j + 3][...] = _adamw(g, w_refs[j][...], m_refs[j][...], v_refs[j][...])
        loss_ref[...] = g_all[:, offs[k]:offs[k] + LANES]

    vm = pl.BlockSpec(memory_space=pltpu.VMEM)
    out_shape = [jax.ShapeDtypeStruct((1, n), F32) for n in sizes for _ in range(4)] + [jax.ShapeDtypeStruct((1, LANES), F32)]
    res = pl.pallas_call(
        body, name="reduce_adamw_replicated", in_specs=[vm] * (1 + 3 * k), out_specs=[vm] * len(out_shape), out_shape=out_shape,
        compiler_params=_params(),
    )(parts, *ws, *ms, *vs)
    return [res[4 * j:4 * j + 4] for j in range(k)], res[-1]


COL_SHARDED = ("w_in", "w_branch_a", "w_branch_b", "w_up", "conv_w")
ROW_SHARDED = ("w_out", "w_down")
SMALL = ("b_gate", "sinks", "ln1_g", "ln1_b", "conv_b", "ln2_g", "ln2_b")
ORDER = ("w_in", "b_gate", "sinks", "w_branch_a", "w_branch_b", "w_out", "ln1_g", "ln1_b", "w_up", "conv_w", "conv_b", "w_down", "ln2_g", "ln2_b")


def _pad_lanes(a):
    pad = (-a.shape[-1]) % LANES
    return a if pad == 0 else jnp.pad(a, ((0, 0), (0, pad)))


def kernel(x, positions, w_in, b_gate, sinks, w_branch_a, w_branch_b, w_out, ln1_g, ln1_b, w_up, conv_w, conv_b, w_down, ln2_g, ln2_b, loss_target, m_w_in, m_b_gate, m_sinks, m_w_branch_a, m_w_branch_b, m_w_out, m_ln1_g, m_ln1_b, m_w_up, m_conv_w, m_conv_b, m_w_down, m_ln2_g, m_ln2_b, v_w_in, v_b_gate, v_sinks, v_w_branch_a, v_w_branch_b, v_w_out, v_ln1_g, v_ln1_b, v_w_up, v_conv_w, v_conv_b, v_w_down, v_ln2_g, v_ln2_b):
    args = dict(locals())
    sharded = COL_SHARDED + ROW_SHARDED
    w = {n: args[n][0] if n in sharded else args[n] for n in ORDER}
    m = {n: args["m_" + n][0] if n in sharded else args["m_" + n] for n in ORDER}
    v = {n: args["v_" + n][0] if n in sharded else args["v_" + n] for n in ORDER}

    travel = {n: (w[n] if n == "conv_w" else w[n].astype(BF16)) for n in sharded}
    (g_in,) = _all_gather([travel["w_in"]], "all_gather_w_in")
    w_in_full = _join_cols(g_in, "join_w_in")
    later = ("w_branch_a", "w_branch_b", "w_out", "w_up", "conv_w", "w_down")

    def join(name, slabs):
        return _join_cols(slabs, "join_" + name) if name in COL_SHARDED else slabs.reshape(-1, slabs.shape[-1])

    def split(name, grad):
        if name in COL_SHARDED:
            return _split_cols(grad if isinstance(grad, tuple) else (grad,), "split_d" + name)
        return grad.reshape((N_DEV, -1, grad.shape[-1]))

    def early_exchange(grads):
        return _Exchange([split(n, grads[n]) for n in later], ["scatter"] * len(later))

    def tail_exchange(grads, loss):
        small_pack = jnp.concatenate(
            [_pad_lanes(p) for n in SMALL for p in (grads[n] if isinstance(grads[n], tuple) else (grads[n],))] + [loss], axis=1)
        return _Exchange([split("w_in", grads["w_in"]), small_pack], ["scatter", "gather"])

    gather_later = _Exchange([travel[n] for n in later], ["gather"] * len(later))
    _, grad_x, _, early_out, (recv_w_in, small_parts) = _local_step(
        x[0], positions[0], w_in_full, w["b_gate"], w["sinks"][0], w["ln1_g"], w["ln1_b"], w["conv_b"], w["ln2_g"], w["ln2_b"], loss_target[0],
        (gather_later, lambda arrived: [join(n, a) for n, a in zip(later, arrived)]), early_exchange, tail_exchange)
    recv = dict(zip(later, early_out), w_in=recv_w_in)

    res = {n: _reduce_adamw(recv[n], w[n], m[n], v[n], "reduce_adamw_" + n) for n in sharded}
    small_res, loss_sum = _reduce_adamw_small(small_parts, [w[n] for n in SMALL], [m[n] for n in SMALL], [v[n] for n in SMALL])
    res.update(zip(SMALL, small_res))
    out = [loss_sum[0, 0], grad_x[None]]
    for k in range(4):
        out += [res[n][k][None] if n in sharded else res[n][k] for n in ORDER]
    return tuple(out)
```

```python
import functools

import jax
import jax.numpy as jnp
import numpy as np
from jax import lax
from jax.experimental import pallas as pl
from jax.experimental.pallas import tpu as pltpu

D_MODEL = 1024
HEAD_DIM = 64
SWA_Q_HEADS = 8
SWA_KV_HEADS = 2
SB_HEADS = 8
WINDOW = 128
ROPE_THETA = 10000.0
D_FF = 2816
LN_EPS = 1e-5
DEPTH = 1
ALPHA = (2.0 * DEPTH) ** 0.25
SWA_Q_WIDTH = SWA_Q_HEADS * HEAD_DIM
SWA_KV_WIDTH = SWA_KV_HEADS * HEAD_DIM
SB_WIDTH = SB_HEADS * HEAD_DIM
GATE_WIDTH = 2 * D_MODEL
IN_WIDTHS = (SWA_Q_WIDTH, SWA_KV_WIDTH, SWA_KV_WIDTH, SB_WIDTH, SB_WIDTH, SB_WIDTH, GATE_WIDTH)
IN_TOTAL = sum(IN_WIDTHS)
ATTN_SCALE = HEAD_DIM ** -0.5

ADAM_LR = 0.001
ADAM_B1 = 0.9
ADAM_B2 = 0.999
ADAM_EPS = 1e-08
ADAM_WD = 0.01
ADAM_STEP = 10

N_DEV = 8
LANES = 128
SB_BLOCK = 256
SB_PAIRS = 4
VMEM_LIMIT = 56 * 1024 * 1024

F32 = jnp.float32
BF16 = jnp.bfloat16
ACT_DTYPE = BF16
MESH = pl.DeviceIdType.MESH


def _params(sem=None):
    return pltpu.CompilerParams(dimension_semantics=sem, vmem_limit_bytes=VMEM_LIMIT)


def _dot(a, b):
    return jnp.dot(a, b, preferred_element_type=F32)


def _dot_nt(a, b):
    return lax.dot_general(a, b, (((1,), (1,)), ((), ())), preferred_element_type=F32)


def _dot_tn(a, b):
    return lax.dot_general(a, b, (((0,), (0,)), ((), ())), preferred_element_type=F32)


def _split_bf16(v):
    hi = v.astype(BF16)
    lo = (v - hi.astype(F32)).astype(BF16)
    return hi, lo


def _matmul(a, b, *, kind, out_shape, grid, a_spec, b_spec, out_spec, name, add=None, add_spec=None, add_scale=1.0):
    dot = {"nn": _dot, "nt": _dot_nt, "tn": _dot_tn}[kind]

    def body(*refs):
        if add is None:
            a_ref, b_ref, o_ref = refs
        else:
            a_ref, b_ref, add_ref, o_ref = refs
        r = dot(a_ref[...].astype(BF16), b_ref[...].astype(BF16))
        if add is not None:
            r = r + add_scale * add_ref[...]
        o_ref[...] = r.astype(o_ref.dtype)

    ins = [a, b] + ([] if add is None else [add])
    specs = [a_spec, b_spec] + ([] if add is None else [add_spec])
    return pl.pallas_call(
        body, name=name, grid=grid, in_specs=specs, out_specs=out_spec, out_shape=out_shape,
        compiler_params=_params(("parallel",) * len(grid)),
    )(*ins)


def _rope_tables(pos_col, inv_freq_lanes):
    T = pos_col.shape[0]
    tm = min(512, T)

    def body(pos_ref, f_ref, cos_ref, sin_ref):
        ang = pos_ref[...].astype(F32) * f_ref[...]
        cos_ref[...] = jnp.cos(ang)
        sin_ref[...] = jnp.sin(ang)

    return pl.pallas_call(
        body, name="rope_tables", grid=(T // tm,),
        in_specs=[pl.BlockSpec((tm, 1), lambda i: (i, 0)), pl.BlockSpec((1, LANES), lambda i: (0, 0))],
        out_specs=[pl.BlockSpec((tm, LANES), lambda i: (i, 0))] * 2,
        out_shape=[jax.ShapeDtypeStruct((T, LANES), F32)] * 2,
        compiler_params=_params(("parallel",)),
    )(pos_col, inv_freq_lanes)


def _lane_iota(shape):
    return lax.broadcasted_iota(jnp.int32, shape, len(shape) - 1)


def _rot_half(t):
    first = (_lane_iota(t.shape) % HEAD_DIM) < (HEAD_DIM // 2)
    return jnp.where(first, -pltpu.roll(t, LANES - HEAD_DIM // 2, axis=1), pltpu.roll(t, HEAD_DIM // 2, axis=1))


def _rope(t, cos, sin):
    return t * cos + _rot_half(t) * sin


def _rope_transpose(d, cos, sin):
    return d * cos - _rot_half(d * sin)


_IN_DTYPES = (F32, F32, BF16, BF16, BF16, BF16, F32)


def _in_proj(x, w_in_b):
    T = x.shape[0]
    tm = min(256, T)
    offs = np.cumsum((0,) + IN_WIDTHS)

    def body(x_ref, w_ref, xb_ref, *outs):
        xb = x_ref[...].astype(BF16)
        xb_ref[...] = xb
        for o_ref, a, b in zip(outs, offs[:-1], offs[1:]):
            o_ref[...] = _dot(xb, w_ref[:, a:b]).astype(o_ref.dtype)

    row = lambda n: pl.BlockSpec((tm, n), lambda i: (i, 0))
    return pl.pallas_call(
        body, name="in_proj", grid=(T // tm,),
        in_specs=[row(D_MODEL), pl.BlockSpec((D_MODEL, IN_TOTAL), lambda i: (0, 0))],
        out_specs=[row(D_MODEL)] + [row(n) for n in IN_WIDTHS],
        out_shape=[jax.ShapeDtypeStruct((T, D_MODEL), BF16)] + [jax.ShapeDtypeStruct((T, n), dt) for n, dt in zip(IN_WIDTHS, _IN_DTYPES)],
        compiler_params=_params(("parallel",)),
    )(x, w_in_b)


def _swa_specs(T):
    blk = WINDOW
    cur = lambda n: pl.BlockSpec((blk, n), lambda i: (i, 0))
    prev = lambda n: pl.BlockSpec((blk, n), lambda i: (jnp.maximum(i - 1, 0), 0))
    return blk, cur, prev


def _swa_window(i, kp, kc, vp, vc, cp, cc, sp, sc):
    kwin = jnp.concatenate([_rope(kp, cp, sp), _rope(kc, cc, sc)], axis=0)
    vwin = jnp.concatenate([vp, vc], axis=0)
    lane = _lane_iota(kwin.shape)
    low = lane < HEAD_DIM
    ks, vs = [], []
    for g in range(SWA_KV_HEADS):
        k0 = jnp.where(low, kwin if g == 0 else pltpu.roll(kwin, HEAD_DIM, axis=1), 0.0)
        v0 = jnp.where(low, vwin if g == 0 else pltpu.roll(vwin, HEAD_DIM, axis=1), 0.0)
        ks.append((k0, pltpu.roll(k0, HEAD_DIM, axis=1)))
        vs.append((v0, pltpu.roll(v0, HEAD_DIM, axis=1)))
    blk = WINDOW
    r = lax.broadcasted_iota(jnp.int32, (blk, 2 * blk), 0)
    c = lax.broadcasted_iota(jnp.int32, (blk, 2 * blk), 1)
    rel = blk + r - c
    valid = (rel >= 0) & (rel < WINDOW) & ((c >= blk) | (i > 0))
    return ks, vs, valid


def _swa_probs(qh, kk, valid, sink):
    s = _dot_nt(qh, kk) * ATTN_SCALE
    s = jnp.where(valid, s, -1e30)
    m = jnp.maximum(jnp.max(s, axis=1, keepdims=True), sink)
    p = jnp.where(valid, jnp.exp(s - m), 0.0)
    es = jnp.exp(sink - m)
    den = jnp.sum(p, axis=1, keepdims=True) + es
    return p / den, es / den


def _swa_fwd(qa, ka, va, cos, sin, sinks):
    T = qa.shape[0]
    blk, cur, prev = _swa_specs(T)

    def body(sink_ref, q_ref, kp_ref, kc_ref, vp_ref, vc_ref, cp_ref, cc_ref, sp_ref, sc_ref, o_ref):
        i = pl.program_id(0)
        cc, sc = cc_ref[...], sc_ref[...]
        ks, vs, valid = _swa_window(i, kp_ref[...], kc_ref[...], vp_ref[...].astype(F32), vc_ref[...].astype(F32),
                                    cp_ref[...], cc, sp_ref[...], sc)
        lane = _lane_iota((blk, LANES))
        for pp in range(SWA_Q_HEADS // 2):
            g = pp // (SWA_Q_HEADS // SWA_KV_HEADS // 2)
            qp = _rope(q_ref[:, pp * LANES:(pp + 1) * LANES], cc, sc)
            out = jnp.zeros((blk, LANES), F32)
            for hh in range(2):
                half = (lane >= hh * HEAD_DIM) & (lane < (hh + 1) * HEAD_DIM)
                qh = jnp.where(half, qp, 0.0).astype(BF16)
                probs, _ = _swa_probs(qh, ks[g][hh].astype(BF16), valid, sink_ref[2 * pp + hh])
                out = out + _dot(probs.astype(BF16), vs[g][hh].astype(BF16))
            o_ref[:, pp * LANES:(pp + 1) * LANES] = out.astype(o_ref.dtype)

    return pl.pallas_call(
        body, name="swa_fwd", grid=(T // blk,),
        in_specs=[pl.BlockSpec(memory_space=pltpu.SMEM), cur(SWA_Q_WIDTH), prev(LANES), cur(LANES), prev(LANES), cur(LANES),
                  prev(LANES), cur(LANES), prev(LANES), cur(LANES)],
        out_specs=cur(SWA_Q_WIDTH),
        out_shape=jax.ShapeDtypeStruct((T, SWA_Q_WIDTH), BF16),
        compiler_params=_params(("parallel",)),
    )(sinks, qa, ka, ka, va, va, cos, cos, sin, sin)


def _swa_bwd(qa, ka, va, cos, sin, sinks, dya):
    T = qa.shape[0]
    blk, cur, prev = _swa_specs(T)
    full = lambda n: pl.BlockSpec((T, n), lambda i: (0, 0))

    def body(sink_ref, q_ref, kp_ref, kc_ref, vp_ref, vc_ref, cp_ref, cc_ref, sp_ref, sc_ref, do_ref,
             dq_ref, dk_out, dv_out, dsink_ref, dk_ref, dv_ref):
        i = pl.program_id(0)

        @pl.when(i == 0)
        def _():
            dk_ref[...] = jnp.zeros_like(dk_ref)
            dv_ref[...] = jnp.zeros_like(dv_ref)
            dsink_ref[...] = jnp.zeros_like(dsink_ref)

        cp, cc, sp, sc = cp_ref[...], cc_ref[...], sp_ref[...], sc_ref[...]
        ks, vs, valid = _swa_window(i, kp_ref[...], kc_ref[...], vp_ref[...].astype(F32), vc_ref[...].astype(F32), cp, cc, sp, sc)
        lane = _lane_iota((blk, LANES))
        lane1 = _lane_iota((1, LANES))
        dkw = jnp.zeros((2 * blk, LANES), F32)
        dvw = jnp.zeros((2 * blk, LANES), F32)
        dsink = jnp.zeros((1, LANES), F32)
        for pp in range(SWA_Q_HEADS // 2):
            g = pp // (SWA_Q_HEADS // SWA_KV_HEADS // 2)
            qp = _rope(q_ref[:, pp * LANES:(pp + 1) * LANES], cc, sc)
            dop = do_ref[:, pp * LANES:(pp + 1) * LANES]
            dqp = jnp.zeros((blk, LANES), F32)
            for hh in range(2):
                half = (lane >= hh * HEAD_DIM) & (lane < (hh + 1) * HEAD_DIM)
                qh = jnp.where(half, qp, 0.0).astype(BF16)
                doh = jnp.where(half, dop, 0.0).astype(BF16)
                kk = ks[g][hh].astype(BF16)
                vv = vs[g][hh].astype(BF16)
                probs, psink = _swa_probs(qh, kk, valid, sink_ref[2 * pp + hh])
                dp = _dot_nt(doh, vv)
                dsum = jnp.sum(probs * dp, axis=1, keepdims=True)
                ds = (probs * (dp - dsum) * ATTN_SCALE).astype(BF16)
                dsink = dsink + jnp.where(lane1 == 2 * pp + hh, -jnp.sum(psink * dsum), 0.0)
                dqp = dqp + _dot(ds, kk)
                dk_h = _dot_tn(ds, qh)
                dv_h = _dot_tn(probs.astype(BF16), doh)
                if hh != g:
                    dk_h = pltpu.roll(dk_h, HEAD_DIM, axis=1)
                    dv_h = pltpu.roll(dv_h, HEAD_DIM, axis=1)
                dkw = dkw + dk_h
                dvw = dvw + dv_h
            dq_ref[:, pp * LANES:(pp + 1) * LANES] = _rope_transpose(dqp, cc, sc).astype(dq_ref.dtype)
        dsink_ref[...] += dsink
        ip = jnp.maximum(i - 1, 0)
        rows_p = pl.ds(pl.multiple_of(ip * blk, blk), blk)
        rows_c = pl.ds(pl.multiple_of(i * blk, blk), blk)
        dk_ref[rows_p, :] += _rope_transpose(dkw[:blk], cp, sp)
        dv_ref[rows_p, :] += dvw[:blk]
        dk_ref[rows_c, :] += _rope_transpose(dkw[blk:], cc, sc)
        dv_ref[rows_c, :] += dvw[blk:]

        @pl.when(i == T // blk - 1)
        def _():
            dk_out[...] = dk_ref[...].astype(BF16)
            dv_out[...] = dv_ref[...].astype(BF16)

    return pl.pallas_call(
        body, name="swa_bwd", grid=(T // blk,),
        in_specs=[pl.BlockSpec(memory_space=pltpu.SMEM), cur(SWA_Q_WIDTH), prev(LANES), cur(LANES), prev(LANES), cur(LANES),
                  prev(LANES), cur(LANES), prev(LANES), cur(LANES), cur(SWA_Q_WIDTH)],
        out_specs=[cur(SWA_Q_WIDTH), full(LANES), full(LANES), pl.BlockSpec((1, LANES), lambda i: (0, 0))],
        out_shape=[jax.ShapeDtypeStruct((T, SWA_Q_WIDTH), BF16), jax.ShapeDtypeStruct((T, LANES), BF16),
                   jax.ShapeDtypeStruct((T, LANES), BF16), jax.ShapeDtypeStruct((1, LANES), F32)],
        scratch_shapes=[pltpu.VMEM((T, LANES), F32)] * 2,
        compiler_params=_params(("arbitrary",)),
    )(sinks, qa, ka, ka, va, va, cos, cos, sin, sin, dya)


class _Exchange:
    FLIPS = [(fx, fy, fc) for fx in (0, 1) for fy in (0, 1) for fc in (0, 1) if (fx, fy, fc) != (0, 0, 0)]

    def __init__(self, arrays, kinds):
        self.arrays, self.kinds, self.n = list(arrays), list(kinds), len(arrays)

    def out_shape(self):
        return [jax.ShapeDtypeStruct(a.shape if k == "scatter" else (N_DEV,) + a.shape, a.dtype) for a, k in zip(self.arrays, self.kinds)]

    def scratch(self):
        return [pltpu.SemaphoreType.DMA((self.n, 7)), pltpu.SemaphoreType.DMA((self.n, 7)), pltpu.SemaphoreType.DMA((self.n,))]

    def bind(self, ins, outs, send_sems, recv_sems, local_sems):
        x, y, c = lax.axis_index("x"), lax.axis_index("y"), lax.axis_index("c")
        me = 4 * x + 2 * y + c
        local, remote = [], []
        for a, kind in enumerate(self.kinds):
            mine = ins[a].at[me] if kind == "scatter" else ins[a]
            local.append(pltpu.make_async_copy(mine, outs[a].at[me], local_sems.at[a]))
            for k, (fx, fy, fc) in enumerate(self.FLIPS):
                peer = (x ^ fx, y ^ fy, c ^ fc)
                peer_slot = 4 * peer[0] + 2 * peer[1] + peer[2]
                src = ins[a].at[peer_slot] if kind == "scatter" else ins[a]
                sems = dict(send_sem=send_sems.at[a, k], recv_sem=recv_sems.at[a, k], device_id=peer, device_id_type=MESH)
                remote.append((pltpu.make_async_remote_copy(src_ref=src, dst_ref=outs[a].at[me], **sems),
                               pltpu.make_async_remote_copy(src_ref=src, dst_ref=outs[a].at[peer_slot], **sems)))

        def start():
            for cp in local:
                cp.start()
            for send, _ in remote:
                send.start()

        def wait():
            for send, arrival in remote:
                arrival.wait_recv()
                send.wait_send()
            for cp in local:
                cp.wait()

        return start, wait


def _hosted_call(body, name, grid, exchange, *, in_specs, out_specs, out_shape, semantics, args, scratch=(), prefetch=()):
    n = 0 if exchange is None else exchange.n
    n_pre, n_in, n_out, n_scratch = len(prefetch), len(in_specs), len(out_specs), len(scratch)

    def hosted(*refs):
        pre, rest = refs[:n_pre], refs[n_pre:]
        ins, rest = rest[:n_in], rest[n_in:]
        ex_ins, rest = rest[:n], rest[n:]
        outs, rest = rest[:n_out], rest[n_out:]
        ex_outs, rest = rest[:n], rest[n:]
        own, sems = rest[:n_scratch], rest[n_scratch:]
        if exchange is None:
            return body(*pre, *ins, *outs, *own)
        start, wait = exchange.bind(ex_ins, ex_outs, *sems)
        ids = [pl.program_id(d) for d in range(len(grid))]
        first = functools.reduce(jnp.logical_and, [i == 0 for i in ids])
        last = functools.reduce(jnp.logical_and, [i == g - 1 for i, g in zip(ids, grid)])
        pl.when(first)(start)
        body(*pre, *ins, *outs, *own)
        pl.when(last)(wait)

    grid_spec = pltpu.PrefetchScalarGridSpec(
        num_scalar_prefetch=n_pre, grid=grid, in_specs=list(in_specs) + [ANY] * n, out_specs=list(out_specs) + [ANY] * n,
        scratch_shapes=list(scratch) + ([] if exchange is None else exchange.scratch()))
    res = pl.pallas_call(
        hosted, name=name, grid_spec=grid_spec, out_shape=list(out_shape) + ([] if exchange is None else exchange.out_shape()),
        compiler_params=_params(semantics if exchange is None else ("arbitrary",) * len(grid)),
    )(*prefetch, *args, *([] if exchange is None else exchange.arrays))
    return res[:n_out], res[n_out:]


SOFTPLUS_LINEAR_FROM = 30.0


def _sb_scores(qm, k, valid):
    z = _dot_nt(qm, k)
    sp = jnp.where(z > SOFTPLUS_LINEAR_FROM, z, jnp.log(1.0 + jnp.exp(z)))
    log_beta = z - sp
    if valid is not None:
        sp = jnp.where(valid, sp, 0.0)
    return sp, log_beta


def _tri2(B, cmp):
    r = lax.broadcasted_iota(jnp.int32, (2 * B, B), 0) % B
    c = lax.broadcasted_iota(jnp.int32, (2 * B, B), 1)
    return cmp(r, c).astype(BF16)


def _tri_sum(v, tri2):
    hi, lo = _split_bf16(v)
    return _dot(jnp.concatenate([hi, lo], axis=1), tri2)


def _head_masks(x):
    low = _lane_iota(x.shape) < HEAD_DIM
    zero = jnp.zeros((), x.dtype)
    return jnp.where(low, x, zero), jnp.where(low, zero, x)


def _strictly_below(B):
    r = lax.broadcasted_iota(jnp.int32, (B, B), 0)
    c = lax.broadcasted_iota(jnp.int32, (B, B), 1)
    return c < r


def _sb_grid(T, descending):
    B = min(SB_BLOCK, T)
    n = T // B
    pairs = [(i, j) for i in range(n) for j in (range(i, -1, -1) if descending else range(i + 1))]
    return B, jnp.asarray([p[0] for p in pairs], jnp.int32), jnp.asarray([p[1] for p in pairs], jnp.int32)


N_PAIRS = SB_HEADS // 2
PAIR_COLS = [slice(p * LANES, (p + 1) * LANES) for p in range(N_PAIRS)]


def _sb_fwd(qb, kb, vb, exchange=None):
    T = qb.shape[0]
    B, i_tab, j_tab = _sb_grid(T, descending=True)
    n = T // B

    def body(i_ref, j_ref, q_ref, k_ref, v_ref, o_ref, a_ref, b_ref, acc_ref, c_ref, tri_ref):
        s = pl.program_id(0)
        i, j = i_ref[s], j_ref[s]

        @pl.when(s == 0)
        def _():
            tri_ref[...] = _tri2(B, lambda r, c: r > c)

        @pl.when(j == i)
        def _():
            acc_ref[...] = jnp.zeros_like(acc_ref)
            c_ref[...] = jnp.zeros_like(c_ref)

        def block(valid):
            for p, cols in enumerate(PAIR_COLS):
                qms = _head_masks(q_ref[:, cols] * ATTN_SCALE)
                k = k_ref[:, cols]
                probs = []
                for hh in range(2):
                    h = 2 * p + hh
                    sp, lb = _sb_scores(qms[hh], k, valid)
                    c = c_ref[h]
                    a = jnp.exp(lb - (c + _tri_sum(sp, tri_ref[...])))
                    beta = jnp.exp(lb)
                    if valid is not None:
                        a = jnp.where(valid, a, 0.0)
                        beta = jnp.where(valid, beta, 0.0)
                    probs.append(a.astype(BF16))
                    a_ref[h] = probs[-1]
                    b_ref[h] = beta.astype(BF16)
                    c_ref[h] = c + jnp.sum(sp, axis=1, keepdims=True)
                acc_ref[:, cols] += _dot(jnp.concatenate(probs, axis=1), jnp.concatenate(_head_masks(v_ref[:, cols]), axis=0))

        pl.when(j == i)(lambda: block(_strictly_below(B)))
        pl.when(j != i)(lambda: block(None))

        @pl.when(j == 0)
        def _():
            o_ref[...] = acc_ref[...].astype(o_ref.dtype)

    q_spec = pl.BlockSpec((B, SB_WIDTH), lambda s, i_ref, j_ref: (i_ref[s], 0))
    k_spec = pl.BlockSpec((B, SB_WIDTH), lambda s, i_ref, j_ref: (j_ref[s], 0))
    tile = pl.BlockSpec((None, None, SB_HEADS, B, B), lambda s, i_ref, j_ref: (i_ref[s], j_ref[s], 0, 0, 0))
    saved = jax.ShapeDtypeStruct((n, n, SB_HEADS, B, B), BF16)
    return _hosted_call(
        body, "sb_fwd", (int(i_tab.shape[0]),), exchange, prefetch=(i_tab, j_tab),
        in_specs=[q_spec, k_spec, k_spec], out_specs=[q_spec, tile, tile],
        out_shape=[jax.ShapeDtypeStruct((T, SB_WIDTH), BF16), saved, saved],
        scratch=[pltpu.VMEM((B, SB_WIDTH), F32), pltpu.VMEM((SB_HEADS, B, 1), F32), pltpu.VMEM((2 * B, B), BF16)],
        semantics=("arbitrary",), args=(qb, kb, vb))


def _sb_bwd(qb, kb, vb, probs, betas, dyb, exchange=None):
    T = qb.shape[0]
    B, i_tab, j_tab = _sb_grid(T, descending=False)
    n_steps = int(i_tab.shape[0])

    def body(i_ref, j_ref, q_ref, k_ref, v_ref, a_ref, b_ref, do_ref, dq_ref, dk_out, dv_out, dq_acc, cg_ref, dk_ref, dv_ref, tri_ref):
        s = pl.program_id(0)
        i, j = i_ref[s], j_ref[s]

        @pl.when(s == 0)
        def _():
            dk_ref[...] = jnp.zeros_like(dk_ref)
            dv_ref[...] = jnp.zeros_like(dv_ref)
            tri_ref[...] = _tri2(B, lambda r, c: r < c)[:B]

        @pl.when(j == 0)
        def _():
            dq_acc[...] = jnp.zeros_like(dq_acc)
            cg_ref[...] = jnp.zeros_like(cg_ref)

        rows = pl.ds(pl.multiple_of(j * B, B), B)
        for p, cols in enumerate(PAIR_COLS):
            qms = _head_masks(q_ref[:, cols] * ATTN_SCALE)
            doms = _head_masks(do_ref[:, cols])
            k, v = k_ref[:, cols], v_ref[:, cols]
            dzs = []
            for hh in range(2):
                h = 2 * p + hh
                g = a_ref[h].astype(F32) * _dot_nt(doms[hh], v)
                cg = cg_ref[h]
                gsum = g + (cg + _dot(g.astype(BF16), tri_ref[...]))
                dzs.append((g - b_ref[h].astype(F32) * gsum).astype(BF16))
                cg_ref[h] = cg + jnp.sum(g, axis=1, keepdims=True)
            dq_acc[:, cols] += _dot(jnp.concatenate(dzs, axis=1), jnp.concatenate(_head_masks(k), axis=0))
            dk_ref[rows, cols] += _dot_tn(jnp.concatenate(dzs, axis=0), jnp.concatenate(qms, axis=0))
            dv_ref[rows, cols] += _dot_tn(jnp.concatenate([a_ref[2 * p], a_ref[2 * p + 1]], axis=0), jnp.concatenate(doms, axis=0))

        @pl.when(j == i)
        def _():
            dq_ref[...] = (dq_acc[...] * ATTN_SCALE).astype(dq_ref.dtype)

        @pl.when(s == n_steps - 1)
        def _():
            dk_out[...] = dk_ref[...].astype(BF16)
            dv_out[...] = dv_ref[...].astype(BF16)

    q_spec = pl.BlockSpec((B, SB_WIDTH), lambda s, i_ref, j_ref: (i_ref[s], 0))
    k_spec = pl.BlockSpec((B, SB_WIDTH), lambda s, i_ref, j_ref: (j_ref[s], 0))
    tile = pl.BlockSpec((None, None, SB_HEADS, B, B), lambda s, i_ref, j_ref: (i_ref[s], j_ref[s], 0, 0, 0))
    full = pl.BlockSpec((T, SB_WIDTH), lambda s, i_ref, j_ref: (0, 0))
    return _hosted_call(
        body, "sb_bwd", (n_steps,), exchange, prefetch=(i_tab, j_tab),
        in_specs=[q_spec, k_spec, k_spec, tile, tile, q_spec], out_specs=[q_spec, full, full],
        out_shape=[jax.ShapeDtypeStruct((T, SB_WIDTH), BF16)] * 3,
        scratch=[pltpu.VMEM((B, SB_WIDTH), F32), pltpu.VMEM((SB_HEADS, B, 1), F32), pltpu.VMEM((T, SB_WIDTH), F32),
                 pltpu.VMEM((T, SB_WIDTH), F32), pltpu.VMEM((B, B), BF16)],
        semantics=("arbitrary",), args=(qb, kb, vb, probs, betas, dyb))


def _ln_stats(u):
    mu = jnp.mean(u, axis=-1, keepdims=True)
    xc = u - mu
    var = jnp.mean(xc * xc, axis=-1, keepdims=True)
    rstd = lax.rsqrt(var + LN_EPS)
    return xc * rstd, rstd


def _ln_bwd(dy, xhat, rstd, g):
    dxh = dy * g
    return rstd * (dxh - jnp.mean(dxh, axis=-1, keepdims=True) - xhat * jnp.mean(dxh * xhat, axis=-1, keepdims=True))


def _gates(gl_ref, bg_ref):
    ga = jax.nn.sigmoid(gl_ref[:, :D_MODEL] + bg_ref[:, :D_MODEL])
    gb = jax.nn.sigmoid(gl_ref[:, D_MODEL:] + bg_ref[:, D_MODEL:])
    return ga, gb


def _mix_fwd(ya, yb, gl, x, wa, wb, wo, b_gate, ln1_g, ln1_b):
    T = x.shape[0]
    tm = min(256, T)

    def body(ya_ref, yb_ref, gl_ref, x_ref, wa_ref, wb_ref, wo_ref, bg_ref, g_ref, b_ref, h_ref, u_ref, x1_ref):
        ga, gb = _gates(gl_ref, bg_ref)
        h = (ga * _dot(ya_ref[...], wa_ref[...]) + gb * _dot(yb_ref[...], wb_ref[...])).astype(BF16)
        h_ref[...] = h
        u = ALPHA * x_ref[...] + _dot(h, wo_ref[...])
        u_ref[...] = u
        xhat, _ = _ln_stats(u)
        x1_ref[...] = (xhat * g_ref[...] + b_ref[...]).astype(BF16)

    row = lambda n: pl.BlockSpec((tm, n), lambda i: (i, 0))
    const = lambda r, n: pl.BlockSpec((r, n), lambda i: (0, 0))
    return pl.pallas_call(
        body, name="mix_fwd", grid=(T // tm,),
        in_specs=[row(SWA_Q_WIDTH), row(SB_WIDTH), row(GATE_WIDTH), row(D_MODEL), const(SWA_Q_WIDTH, D_MODEL), const(SB_WIDTH, D_MODEL),
                  const(D_MODEL, D_MODEL), const(1, GATE_WIDTH), const(1, D_MODEL), const(1, D_MODEL)],
        out_specs=[row(D_MODEL)] * 3,
        out_shape=[jax.ShapeDtypeStruct((T, D_MODEL), BF16), jax.ShapeDtypeStruct((T, D_MODEL), F32), jax.ShapeDtypeStruct((T, D_MODEL), BF16)],
        compiler_params=_params(("parallel",)),
    )(ya, yb, gl, x, wa, wb, wo, b_gate, ln1_g, ln1_b)


def _mix_bwd(du1, ya, yb, gl, wa, wb, wo, b_gate):
    T = du1.shape[0]
    tm = min(256, T)

    def body(du_ref, ya_ref, yb_ref, gl_ref, wa_ref, wb_ref, wo_ref, bg_ref, dya_ref, dyb_ref, dgl_ref, dta_ref, dtb_ref, dbg_ref):
        @pl.when(pl.program_id(0) == 0)
        def _():
            dbg_ref[...] = jnp.zeros_like(dbg_ref)

        dh = _dot_nt(du_ref[...].astype(BF16), wo_ref[...])
        ga, gb = _gates(gl_ref, bg_ref)
        for gate, y_ref, w_ref, dy_ref, dt_ref, lo in ((ga, ya_ref, wa_ref, dya_ref, dta_ref, 0), (gb, yb_ref, wb_ref, dyb_ref, dtb_ref, D_MODEL)):
            t = _dot(y_ref[...], w_ref[...])
            dlogit = dh * t * gate * (1.0 - gate)
            dgl_ref[:, lo:lo + D_MODEL] = dlogit.astype(BF16)
            dbg_ref[:, lo:lo + D_MODEL] += jnp.sum(dlogit, axis=0, keepdims=True)
            dt = (dh * gate).astype(BF16)
            dt_ref[...] = dt
            dy_ref[...] = _dot_nt(dt, w_ref[...]).astype(BF16)

    row = lambda n: pl.BlockSpec((tm, n), lambda i: (i, 0))
    const = lambda r, n: pl.BlockSpec((r, n), lambda i: (0, 0))
    sds = lambda n, dt: jax.ShapeDtypeStruct((T, n), dt)
    return pl.pallas_call(
        body, name="mix_bwd", grid=(T // tm,),
        in_specs=[row(D_MODEL), row(SWA_Q_WIDTH), row(SB_WIDTH), row(GATE_WIDTH), const(SWA_Q_WIDTH, D_MODEL), const(SB_WIDTH, D_MODEL),
                  const(D_MODEL, D_MODEL), const(1, GATE_WIDTH)],
        out_specs=[row(SWA_Q_WIDTH), row(SB_WIDTH), row(GATE_WIDTH), row(D_MODEL), row(D_MODEL), const(1, GATE_WIDTH)],
        out_shape=[sds(SWA_Q_WIDTH, BF16), sds(SB_WIDTH, BF16), sds(GATE_WIDTH, BF16), sds(D_MODEL, BF16), sds(D_MODEL, BF16),
                   jax.ShapeDtypeStruct((1, GATE_WIDTH), F32)],
        compiler_params=_params(("arbitrary",)),
    )(du1, ya, yb, gl, wa, wb, wo, b_gate)


CONV_COLS = LANES


CONV_CHUNK = 64
CONV_CHUNK_FWD = 256
HALO = 8


def _taps(ref, r0, rows, lead):
    return [ref[pl.ds(r0 + lead + k, rows), :] for k in ((-2, -1, 0) if lead else (0, 1, 2))]


def _chunks(T, rows, step, init=None):
    def body(c, carry):
        out = step(pl.multiple_of(c * rows, rows), *(() if init is None else (carry,)))
        return carry if init is None else out
    return lax.fori_loop(0, T // rows, body, 0 if init is None else init)


def _conv_chunk(taps, w_ref, b_ref):
    return w_ref[0:1, :] * taps[0] + w_ref[1:2, :] * taps[1] + w_ref[2:3, :] * taps[2] + b_ref[...]


def _fold(x):
    return jnp.sum(x.reshape(x.shape[0] // 8, 8, x.shape[1]), axis=0)


def _conv_specs(T):
    nb = D_FF // CONV_COLS
    pair = pl.BlockSpec((2, T, CONV_COLS), lambda j: (0, 0, j))
    gate = lambda r: pl.BlockSpec((r, CONV_COLS), lambda j: (0, j))
    up = lambda r: pl.BlockSpec((r, CONV_COLS), lambda j: (0, j + nb))
    return nb, pair, gate, up


def _conv_glu_fwd(p3, conv_w, conv_b):
    T = p3.shape[1]
    nb, pair, gate, up = _conv_specs(T)

    R = min(CONV_CHUNK_FWD, T)

    def body(p_ref, wg_ref, wu_ref, bg_ref, bu_ref, s_ref, pg_s, pu_s):
        for half, scr in enumerate((pg_s, pu_s)):
            scr[0:HALO, :] = jnp.zeros((HALO, CONV_COLS), F32)
            scr[HALO:HALO + T, :] = p_ref[half].astype(F32)
        def step(r0):
            ag = _conv_chunk(_taps(pg_s, r0, R, HALO), wg_ref, bg_ref)
            au = _conv_chunk(_taps(pu_s, r0, R, HALO), wu_ref, bu_ref)
            s_ref[pl.ds(r0, R), :] = (ag * jax.nn.sigmoid(ag) * au).astype(BF16)

        _chunks(T, R, step)

    return pl.pallas_call(
        body, name="conv_glu_fwd", grid=(nb,),
        in_specs=[pair, gate(3), up(3), gate(1), up(1)],
        out_specs=pl.BlockSpec((T, CONV_COLS), lambda j: (0, j)),
        out_shape=jax.ShapeDtypeStruct((T, D_FF), BF16),
        scratch_shapes=[pltpu.VMEM((T + HALO, CONV_COLS), F32)] * 2,
        compiler_params=_params(("parallel",)),
    )(p3, conv_w, conv_w, conv_b, conv_b)


def _conv_glu_bwd(p3, ds, conv_w, conv_b):
    T = p3.shape[1]
    nb, pair, gate, up = _conv_specs(T)

    R = min(CONV_CHUNK, T)

    def body(p_ref, ds_ref, wg_ref, wu_ref, bg_ref, bu_ref, dp_ref, dwg_ref, dwu_ref, dbg_ref, dbu_ref, pg_s, pu_s, dag_s, dau_s):
        for half, scr in enumerate((pg_s, pu_s)):
            scr[0:HALO, :] = jnp.zeros((HALO, CONV_COLS), F32)
            scr[HALO:HALO + T, :] = p_ref[half].astype(F32)
        for scr in (dag_s, dau_s):
            scr[T:T + HALO, :] = jnp.zeros((HALO, CONV_COLS), F32)
        halves = ((pg_s, dag_s, wg_ref, dwg_ref, dbg_ref), (pu_s, dau_s, wu_ref, dwu_ref, dbu_ref))

        def step(r0, sums):
            taps = [_taps(p_s, r0, R, HALO) for p_s, *_ in halves]
            ag = _conv_chunk(taps[0], wg_ref, bg_ref)
            au = _conv_chunk(taps[1], wu_ref, bu_ref)
            sg = jax.nn.sigmoid(ag)
            d = ds_ref[pl.ds(r0, R), :].astype(F32)
            das = (d * au * (sg * (1.0 + ag * (1.0 - sg))), d * ag * sg)
            out = []
            for half, (_, da_s, *_) in enumerate(halves):
                da_s[pl.ds(r0, R), :] = das[half]
                out.append(tuple(sums[half][k] + _fold(das[half] * taps[half][k]) for k in range(3)) + (sums[half][3] + _fold(das[half]),))
            return tuple(out)

        sums = _chunks(T, R, step, ((jnp.zeros((8, CONV_COLS), F32),) * 4,) * 2)
        for half, (_, da_s, w_ref, dw_ref, db_ref) in enumerate(halves):
            for k in range(3):
                dw_ref[k:k + 1, :] = jnp.sum(sums[half][k], axis=0, keepdims=True)
            db_ref[...] = jnp.sum(sums[half][3], axis=0, keepdims=True)

            def transposed(r0, da_s=da_s, w_ref=w_ref, half=half):
                da0, da1, da2 = _taps(da_s, r0, R, 0)
                dp_ref[half, pl.ds(r0, R), :] = (w_ref[2:3, :] * da0 + w_ref[1:2, :] * da1 + w_ref[0:1, :] * da2).astype(BF16)

            _chunks(T, R, transposed)

    col = lambda r: pl.BlockSpec((r, CONV_COLS), lambda j: (0, j))
    return pl.pallas_call(
        body, name="conv_glu_bwd", grid=(nb,),
        in_specs=[pair, col(T), gate(3), up(3), gate(1), up(1)],
        out_specs=[pair, col(3), col(3), col(1), col(1)],
        out_shape=[jax.ShapeDtypeStruct((2, T, D_FF), BF16), jax.ShapeDtypeStruct((3, D_FF), F32), jax.ShapeDtypeStruct((3, D_FF), F32),
                   jax.ShapeDtypeStruct((1, D_FF), F32), jax.ShapeDtypeStruct((1, D_FF), F32)],
        scratch_shapes=[pltpu.VMEM((T + HALO, CONV_COLS), F32)] * 4,
        compiler_params=_params(("parallel",)),
    )(p3, ds, conv_w, conv_w, conv_b, conv_b)


def _ffn_down_loss(s, w_down, u1, ln1_g, ln1_b, ln2_g, ln2_b, target):
    T = u1.shape[0]
    tm = min(256, T)

    def body(s_ref, w_ref, u1_ref, g1_ref, b1_ref, g2_ref, b2_ref, t_ref, du_ref, dub_ref, dg_ref, db_ref, loss_ref):
        @pl.when(pl.program_id(0) == 0)
        def _():
            dg_ref[...] = jnp.zeros_like(dg_ref)
            db_ref[...] = jnp.zeros_like(db_ref)
            loss_ref[...] = jnp.zeros_like(loss_ref)

        xh1, _ = _ln_stats(u1_ref[...])
        x1 = xh1 * g1_ref[...] + b1_ref[...]
        u2 = ALPHA * x1 + _dot(s_ref[...], w_ref[...])
        xh2, rstd2 = _ln_stats(u2)
        err = xh2 * g2_ref[...] + b2_ref[...] - t_ref[...]
        per_token = jnp.mean(err * err, axis=-1, keepdims=True)
        loss_ref[...] += 0.5 * jnp.sum(per_token, axis=0, keepdims=True)
        dy = err * (1.0 / D_MODEL)
        dg_ref[...] += jnp.sum(dy * xh2, axis=0, keepdims=True)
        db_ref[...] += jnp.sum(dy, axis=0, keepdims=True)
        du2 = _ln_bwd(dy, xh2, rstd2, g2_ref[...])
        du_ref[...] = du2
        dub_ref[...] = du2.astype(BF16)

    row = lambda n: pl.BlockSpec((tm, n), lambda i: (i, 0))
    const = lambda r, n: pl.BlockSpec((r, n), lambda i: (0, 0))
    vec = const(1, D_MODEL)
    return pl.pallas_call(
        body, name="ffn_down_loss", grid=(T // tm,),
        in_specs=[row(D_FF), const(D_FF, D_MODEL), row(D_MODEL), vec, vec, vec, vec, row(D_MODEL)],
        out_specs=[row(D_MODEL), row(D_MODEL), vec, vec, const(1, LANES)],
        out_shape=[jax.ShapeDtypeStruct((T, D_MODEL), F32), jax.ShapeDtypeStruct((T, D_MODEL), BF16), jax.ShapeDtypeStruct((1, D_MODEL), F32),
                   jax.ShapeDtypeStruct((1, D_MODEL), F32), jax.ShapeDtypeStruct((1, LANES), F32)],
        compiler_params=_params(("arbitrary",)),
    )(s, w_down, u1, ln1_g, ln1_b, ln2_g, ln2_b, target)


def _ffn_up_bwd_ln1(dp3, w_up, du2, u1, ln1_g):
    T = u1.shape[0]
    tm = min(256, T)

    def body(dp_ref, w_ref, du2_ref, u1_ref, g_ref, du_ref, dub_ref, dg_ref, db_ref):
        @pl.when(pl.program_id(0) == 0)
        def _():
            dg_ref[...] = jnp.zeros_like(dg_ref)
            db_ref[...] = jnp.zeros_like(db_ref)

        dx1 = _dot_nt(dp_ref[0], w_ref[:, :D_FF]) + _dot_nt(dp_ref[1], w_ref[:, D_FF:]) + ALPHA * du2_ref[...]
        xh, rstd = _ln_stats(u1_ref[...])
        dg_ref[...] += jnp.sum(dx1 * xh, axis=0, keepdims=True)
        db_ref[...] += jnp.sum(dx1, axis=0, keepdims=True)
        du1 = _ln_bwd(dx1, xh, rstd, g_ref[...])
        du_ref[...] = du1
        dub_ref[...] = du1.astype(BF16)

    row = lambda n: pl.BlockSpec((tm, n), lambda i: (i, 0))
    const = lambda r, n: pl.BlockSpec((r, n), lambda i: (0, 0))
    vec = const(1, D_MODEL)
    return pl.pallas_call(
        body, name="ffn_up_bwd_ln1", grid=(T // tm,),
        in_specs=[pl.BlockSpec((2, tm, D_FF), lambda i: (0, i, 0)), const(D_MODEL, 2 * D_FF), row(D_MODEL), row(D_MODEL), vec],
        out_specs=[row(D_MODEL), row(D_MODEL), vec, vec],
        out_shape=[jax.ShapeDtypeStruct((T, D_MODEL), F32), jax.ShapeDtypeStruct((T, D_MODEL), BF16), jax.ShapeDtypeStruct((1, D_MODEL), F32),
                   jax.ShapeDtypeStruct((1, D_MODEL), F32)],
        compiler_params=_params(("arbitrary",)),
    )(dp3, w_up, du2, u1, ln1_g)


def _local_step(x, positions, w_in, b_gate, sinks, ln1_g, ln1_b, conv_b, ln2_g, ln2_b, target, later_weights,
                early_exchange=None, tail_exchange=None):
    T = x.shape[0]
    inv_freq = 1.0 / (ROPE_THETA ** (jnp.arange(0, HEAD_DIM, 2, dtype=F32) / HEAD_DIM))
    cos, sin = _rope_tables(positions.reshape(T, 1), jnp.tile(inv_freq, LANES // (HEAD_DIM // 2)).reshape(1, LANES))

    xb, qa, ka, va, qb, kb, vb, gl = _in_proj(x, w_in)
    ya = _swa_fwd(qa, ka, va, cos, sin, sinks)
    if isinstance(later_weights, tuple):
        exchange, finish = later_weights
        (yb, probs, betas), arrived = _sb_fwd(qb, kb, vb, exchange)
        later_weights = finish(arrived)
    else:
        (yb, probs, betas), _ = _sb_fwd(qb, kb, vb)
    wa, wb, wo, w_up, conv_w, w_down = later_weights
    h, u1, x1 = _mix_fwd(ya, yb, gl, x, wa, wb, wo, b_gate, ln1_g, ln1_b)

    ff_tn = D_FF // 2
    nff = D_FF // ff_tn
    tm = min(512, T)
    p3 = _matmul(x1, w_up, kind="nn", name="ffn_up", grid=(T // tm, 2 * nff),
                 a_spec=pl.BlockSpec((tm, D_MODEL), lambda i, j: (i, 0)), b_spec=pl.BlockSpec((D_MODEL, ff_tn), lambda i, j: (0, j)),
                 out_spec=pl.BlockSpec((None, tm, ff_tn), lambda i, j: (j // nff, i, j % nff)),
                 out_shape=jax.ShapeDtypeStruct((2, T, D_FF), ACT_DTYPE))
    s = _conv_glu_fwd(p3, conv_w, conv_b)
    du2, du2b, dln2_g, dln2_b, loss = _ffn_down_loss(s, w_down, u1, ln1_g, ln1_b, ln2_g, ln2_b, target)

    ds = _matmul(du2b, w_down, kind="nt", name="ffn_down_bwd", grid=(T // tm, nff),
                 a_spec=pl.BlockSpec((tm, D_MODEL), lambda i, j: (i, 0)), b_spec=pl.BlockSpec((ff_tn, D_MODEL), lambda i, j: (j, 0)),
                 out_spec=pl.BlockSpec((tm, ff_tn), lambda i, j: (i, j)), out_shape=jax.ShapeDtypeStruct((T, D_FF), ACT_DTYPE))
    dp3, dcw_g, dcw_u, dcb_g, dcb_u = _conv_glu_bwd(p3, ds, conv_w, conv_b)
    tk = 256
    dw_down = _matmul(s, du2b, kind="tn", name="dw_down", grid=(D_FF // tk,),
                      a_spec=pl.BlockSpec((T, tk), lambda i: (0, i)), b_spec=pl.BlockSpec((T, D_MODEL), lambda i: (0, 0)),
                      out_spec=pl.BlockSpec((tk, D_MODEL), lambda i: (i, 0)), out_shape=jax.ShapeDtypeStruct((D_FF, D_MODEL), BF16))
    dw_up = _matmul(x1, dp3, kind="tn", name="dw_up", grid=(D_MODEL // 512, 2 * nff),
                    a_spec=pl.BlockSpec((T, 512), lambda i, j: (0, i)), b_spec=pl.BlockSpec((None, T, ff_tn), lambda i, j: (j // nff, 0, j % nff)),
                    out_spec=pl.BlockSpec((512, ff_tn), lambda i, j: (i, j)), out_shape=jax.ShapeDtypeStruct((D_MODEL, 2 * D_FF), BF16))
    du1, du1b, dln1_g, dln1_b = _ffn_up_bwd_ln1(dp3, w_up, du2, u1, ln1_g)
    dya, dyb, dgl, dta, dtb, db_gate = _mix_bwd(du1, ya, yb, gl, wa, wb, wo, b_gate)

    def dw_tn(a, g, name):
        rows, cols = a.shape[1], g.shape[1]
        tn = min(512, cols)
        return _matmul(a, g, kind="tn", name=name, grid=(rows // 512, cols // tn),
                       a_spec=pl.BlockSpec((T, 512), lambda i, j: (0, i)), b_spec=pl.BlockSpec((T, tn), lambda i, j: (0, j)),
                       out_spec=pl.BlockSpec((512, tn), lambda i, j: (i, j)), out_shape=jax.ShapeDtypeStruct((rows, cols), BF16))

    dwa = dw_tn(ya, dta, "dw_branch_a")
    dwb = dw_tn(yb, dtb, "dw_branch_b")
    dwo = dw_tn(h, du1b, "dw_out")

    grads = dict(
        b_gate=db_gate, w_branch_a=dwa, w_branch_b=dwb, w_out=dwo, ln1_g=dln1_g, ln1_b=dln1_b,
        w_up=dw_up, conv_w=jnp.concatenate([dcw_g, dcw_u], axis=1), conv_b=(dcb_g, dcb_u), w_down=dw_down, ln2_g=dln2_g, ln2_b=dln2_b)
    (dqb, dkb, dvb), early_out = _sb_bwd(qb, kb, vb, probs, betas, dyb, early_exchange(grads) if early_exchange else None)
    dqa, dka, dva, grads["sinks"] = _swa_bwd(qa, ka, va, cos, sin, sinks, dya)
    dproj = (dqa, dka, dva, dqb, dkb, dvb, dgl)
    grads["w_in"] = _dw_in(xb, dproj)
    grad_x, tail_out = _grad_x(dproj, w_in, du1, tail_exchange(grads, loss) if tail_exchange else None)
    return loss, grad_x, grads, early_out, tail_out


def _dw_in(xb, dproj):
    T = xb.shape[0]
    tn = 2 * LANES
    groups, start, k = [], 0, 0
    while k < len(IN_WIDTHS):
        if IN_WIDTHS[k] >= tn:
            groups.append((start, IN_WIDTHS[k] // tn, [(k, 0, tn)]))
            k += 1
        else:
            members, off = [], 0
            while off < tn:
                members.append((k, off, IN_WIDTHS[k]))
                off += IN_WIDTHS[k]
                k += 1
            groups.append((start, 1, members))
        start += groups[-1][1]

    def body(x_ref, *refs):
        pieces, o_ref = refs[:-1], refs[-1]
        j = pl.program_id(0)
        for first, steps, members in groups:
            @pl.when((j >= first) & (j < first + steps))
            def _(members=members):
                for k, off, width in members:
                    o_ref[:, off:off + width] = _dot_tn(x_ref[...], pieces[k][...]).astype(o_ref.dtype)

    specs = [None] * len(IN_WIDTHS)
    for first, steps, members in groups:
        for k, _, width in members:
            specs[k] = pl.BlockSpec((T, width), lambda j, first=first, steps=steps: (0, jnp.clip(j - first, 0, steps - 1)))
    return pl.pallas_call(
        body, name="dw_in", grid=(IN_TOTAL // tn,),
        in_specs=[pl.BlockSpec((T, D_MODEL), lambda j: (0, 0))] + specs, out_specs=pl.BlockSpec((D_MODEL, tn), lambda j: (0, j)),
        out_shape=jax.ShapeDtypeStruct((D_MODEL, IN_TOTAL), BF16), compiler_params=_params(("arbitrary",)),
    )(xb, *dproj)


def _grad_x(dproj, w_in, du1, exchange=None):
    T = du1.shape[0]
    tm = min(256, T)
    offs = np.cumsum((0,) + IN_WIDTHS)

    def body(*refs):
        pieces, (w_ref, du_ref, o_ref) = refs[:len(IN_WIDTHS)], refs[len(IN_WIDTHS):]
        acc = ALPHA * du_ref[...]
        for p_ref, a, b in zip(pieces, offs[:-1], offs[1:]):
            acc = acc + _dot_nt(p_ref[...].astype(BF16), w_ref[:, a:b])
        o_ref[...] = acc

    row = lambda n: pl.BlockSpec((tm, n), lambda i: (i, 0))
    (grad_x,), arrived = _hosted_call(
        body, "grad_x", (T // tm,), exchange,
        in_specs=[row(n) for n in IN_WIDTHS] + [pl.BlockSpec((D_MODEL, IN_TOTAL), lambda i: (0, 0)), row(D_MODEL)],
        out_specs=[row(D_MODEL)], out_shape=[jax.ShapeDtypeStruct((T, D_MODEL), F32)], semantics=("parallel",),
        args=(*dproj, w_in, du1))
    return grad_x, arrived


ANY = pl.BlockSpec(memory_space=pl.ANY)


def _all_gather(slabs, name):
    n = len(slabs)

    def body(*refs):
        ins, outs = refs[:n], refs[n:2 * n]
        send_sems, recv_sems, local_sems = refs[2 * n:]
        x, y, c = lax.axis_index("x"), lax.axis_index("y"), lax.axis_index("c")
        me, sibling = (x, y, c), (x, y, 1 - c)
        chips = [(1 - x, y), (x, 1 - y), (1 - x, 1 - y)]

        def slot(pos):
            return 4 * pos[0] + 2 * pos[1] + pos[2]

        def copy(a, k, block, to, from_input=False):
            return pltpu.make_async_remote_copy(
                src_ref=ins[a] if from_input else outs[a].at[slot(block)], dst_ref=outs[a].at[slot(block)],
                send_sem=send_sems.at[a, k], recv_sem=recv_sems.at[a, k], device_id=to, device_id_type=MESH)

        mine = [pltpu.make_async_copy(ins[a], outs[a].at[slot(me)], local_sems.at[a]) for a in range(n)]
        for cp in mine:
            cp.start()
        first = []
        for a in range(n):
            first.append(copy(a, 0, me, sibling, from_input=True))
            first += [copy(a, 1 + j, me, (*chip, c), from_input=True) for j, chip in enumerate(chips)]
        for cp in first:
            cp.start()
        passed = []
        for j, chip in enumerate(chips):
            for a in range(n):
                copy(a, 1 + j, (*chip, c), me).wait_recv()
                fwd = copy(a, 4 + j, (*chip, c), sibling)
                fwd.start()
                passed.append(fwd)
        for a in range(n):
            copy(a, 0, sibling, me).wait_recv()
            for j, chip in enumerate(chips):
                copy(a, 4 + j, (*chip, 1 - c), me).wait_recv()
        for cp in first + passed:
            cp.wait_send()
        for cp in mine:
            cp.wait()

    return pl.pallas_call(
        body, name=name,
        in_specs=[ANY] * n, out_specs=[ANY] * n,
        out_shape=[jax.ShapeDtypeStruct((N_DEV,) + s.shape, s.dtype) for s in slabs],
        scratch_shapes=[pltpu.SemaphoreType.DMA((n, 7)), pltpu.SemaphoreType.DMA((n, 7)), pltpu.SemaphoreType.DMA((n,))],
    )(*slabs)


def _all_to_all(slabs, name):
    n = len(slabs)

    def body(*refs):
        ins, outs = refs[:n], refs[n:2 * n]
        send_sems, recv_sems, local_sems = refs[2 * n:]
        x, y, c = lax.axis_index("x"), lax.axis_index("y"), lax.axis_index("c")
        my_slot = 4 * x + 2 * y + c
        flips = [(fx, fy, fc) for fx in (0, 1) for fy in (0, 1) for fc in (0, 1) if (fx, fy, fc) != (0, 0, 0)]

        def copy(a, k):
            fx, fy, fc = flips[k]
            peer = (x ^ fx, y ^ fy, c ^ fc)
            peer_slot = 4 * peer[0] + 2 * peer[1] + peer[2]
            send = pltpu.make_async_remote_copy(src_ref=ins[a].at[peer_slot], dst_ref=outs[a].at[my_slot], send_sem=send_sems.at[a, k],
                                                recv_sem=recv_sems.at[a, k], device_id=peer, device_id_type=MESH)
            recv = pltpu.make_async_remote_copy(src_ref=ins[a].at[peer_slot], dst_ref=outs[a].at[peer_slot], send_sem=send_sems.at[a, k],
                                                recv_sem=recv_sems.at[a, k], device_id=peer, device_id_type=MESH)
            return send, recv

        mine = [pltpu.make_async_copy(ins[a].at[my_slot], outs[a].at[my_slot], local_sems.at[a]) for a in range(n)]
        for cp in mine:
            cp.start()
        copies = [copy(a, k) for a in range(n) for k in range(len(flips))]
        for send, _ in copies:
            send.start()
        for send, recv in copies:
            recv.wait_recv()
            send.wait_send()
        for cp in mine:
            cp.wait()

    return pl.pallas_call(
        body, name=name,
        in_specs=[ANY] * n, out_specs=[ANY] * n,
        out_shape=[jax.ShapeDtypeStruct(s.shape, s.dtype) for s in slabs],
        scratch_shapes=[pltpu.SemaphoreType.DMA((n, 7)), pltpu.SemaphoreType.DMA((n, 7)), pltpu.SemaphoreType.DMA((n,))],
    )(*slabs)


def _row_tile(rows):
    for cand in range(256, 7, -8):
        if rows % cand == 0:
            return cand
    return rows


def _window(w):
    wp = max(-(-((w * r) % LANES + w) // LANES) for r in range(N_DEV)) * LANES
    assert all((w * r) // LANES * LANES + wp <= N_DEV * w for r in range(N_DEV))
    return wp


def _join_cols(slabs, name):
    _, R, w = slabs.shape
    tr = _row_tile(R)
    wp = _window(w)

    def body(g_ref, o_ref, pad_ref):
        if w % LANES == 0:
            for r in range(N_DEV):
                o_ref[:, w * r:w * (r + 1)] = g_ref[r]
            return
        o_ref[...] = jnp.zeros_like(o_ref)
        pad_ref[...] = jnp.zeros_like(pad_ref)
        for r in range(N_DEV):
            q, s = divmod(w * r, LANES)
            pad_ref[:, :w] = g_ref[r]
            y = pad_ref[...]
            if s:
                y = pltpu.roll(y, s, axis=1)
            o_ref[:, LANES * q:LANES * q + wp] += y

    return pl.pallas_call(
        body, name=name, grid=(R // tr,),
        in_specs=[pl.BlockSpec((N_DEV, tr, w), lambda i: (0, i, 0))], out_specs=pl.BlockSpec((tr, N_DEV * w), lambda i: (i, 0)),
        out_shape=jax.ShapeDtypeStruct((R, N_DEV * w), slabs.dtype), scratch_shapes=[pltpu.VMEM((tr, wp), slabs.dtype)],
        compiler_params=_params(("parallel",)),
    )(slabs)


def _split_cols(pieces, name):
    R = pieces[0].shape[0]
    widths = [p.shape[1] for p in pieces]
    total = sum(widths)
    w = total // N_DEV
    tr = _row_tile(R)
    wp = _window(w)
    offs = np.cumsum([0] + widths)
    dtype = pieces[0].dtype

    def body(*refs):
        ins, (o_ref, full_ref) = refs[:len(pieces)], refs[len(pieces):]
        for p_ref, a, b in zip(ins, offs[:-1], offs[1:]):
            full_ref[:, a:b] = p_ref[...].astype(dtype)
        for r in range(N_DEV):
            q, s = divmod(w * r, LANES)
            y = full_ref[:, LANES * q:LANES * q + wp]
            if s:
                y = pltpu.roll(y, wp - s, axis=1)
            o_ref[r] = y[:, :w]

    return pl.pallas_call(
        body, name=name, grid=(R // tr,),
        in_specs=[pl.BlockSpec((tr, n), lambda i: (i, 0)) for n in widths], out_specs=pl.BlockSpec((N_DEV, tr, w), lambda i: (0, i, 0)),
        out_shape=jax.ShapeDtypeStruct((N_DEV, R, w), dtype), scratch_shapes=[pltpu.VMEM((tr, total), dtype)],
        compiler_params=_params(("parallel",)),
    )(*pieces)


def _adamw(g, w, m, v):
    m_new = ADAM_B1 * m + (1.0 - ADAM_B1) * g
    v_new = ADAM_B2 * v + (1.0 - ADAM_B2) * jnp.square(g)
    m_hat = m_new / (1.0 - ADAM_B1 ** ADAM_STEP)
    v_hat = v_new / (1.0 - ADAM_B2 ** ADAM_STEP)
    return -ADAM_LR * (m_hat / (jnp.sqrt(v_hat) + ADAM_EPS) + ADAM_WD * w), m_new, v_new


def _sum_parts(p_ref):
    g = p_ref[0].astype(F32)
    for d in range(1, N_DEV):
        g = g + p_ref[d].astype(F32)
    return g


def _reduce_adamw(parts, w, m, v, name):
    R, C = w.shape
    tr = _row_tile(R)

    def body(p_ref, w_ref, m_ref, v_ref, g_ref, d_ref, mo_ref, vo_ref):
        g = _sum_parts(p_ref)
        g_ref[...] = g
        d_ref[...], mo_ref[...], vo_ref[...] = _adamw(g, w_ref[...], m_ref[...], v_ref[...])

    row = pl.BlockSpec((tr, C), lambda i: (i, 0))
    return pl.pallas_call(
        body, name=name, grid=(R // tr,),
        in_specs=[pl.BlockSpec((N_DEV, tr, C), lambda i: (0, i, 0)), row, row, row],
        out_specs=[row] * 4, out_shape=[jax.ShapeDtypeStruct((R, C), F32)] * 4,
        compiler_params=_params(("parallel",)),
    )(parts, w, m, v)


def _reduce_adamw_small(parts, ws, ms, vs):
    sizes = [a.shape[1] for a in ws]
    k = len(sizes)
    offs = np.cumsum([0] + [-(-n // LANES) * LANES for n in sizes])

    def body(*refs):
        p_ref, w_refs, m_refs, v_refs = refs[0], refs[1:1 + k], refs[1 + k:1 + 2 * k], refs[1 + 2 * k:1 + 3 * k]
        outs, loss_ref = refs[1 + 3 * k:-1], refs[-1]
        g_all = _sum_parts(p_ref)
        for j, n in enumerate(sizes):
            g = g_all[:, offs[j]:offs[j] + LANES * (-(-n // LANES))][:, :n]
            outs[4 * j][...] = g
            outs[4 * j + 1][...], outs[4 * j + 2][...], outs[4 * j + 3][...] = _adamw(g, w_refs[j][...], m_refs[j][...], v_refs[j][...])
        loss_ref[...] = g_all[:, offs[k]:offs[k] + LANES]

    vm = pl.BlockSpec(memory_space=pltpu.VMEM)
    out_shape = [jax.ShapeDtypeStruct((1, n), F32) for n in sizes for _ in range(4)] + [jax.ShapeDtypeStruct((1, LANES), F32)]
    res = pl.pallas_call(
        body, name="reduce_adamw_replicated", in_specs=[vm] * (1 + 3 * k), out_specs=[vm] * len(out_shape), out_shape=out_shape,
        compiler_params=_params(),
    )(parts, *ws, *ms, *vs)
    return [res[4 * j:4 * j + 4] for j in range(k)], res[-1]


COL_SHARDED = ("w_in", "w_branch_a", "w_branch_b", "w_up", "conv_w")
ROW_SHARDED = ("w_out", "w_down")
SMALL = ("b_gate", "sinks", "ln1_g", "ln1_b", "conv_b", "ln2_g", "ln2_b")
ORDER = ("w_in", "b_gate", "sinks", "w_branch_a", "w_branch_b", "w_out", "ln1_g", "ln1_b", "w_up", "conv_w", "conv_b", "w_down", "ln2_g", "ln2_b")


def _pad_lanes(a):
    pad = (-a.shape[-1]) % LANES
    return a if pad == 0 else jnp.pad(a, ((0, 0), (0, pad)))


def kernel(x, positions, w_in, b_gate, sinks, w_branch_a, w_branch_b, w_out, ln1_g, ln1_b, w_up, conv_w, conv_b, w_down, ln2_g, ln2_b, loss_target, m_w_in, m_b_gate, m_sinks, m_w_branch_a, m_w_branch_b, m_w_out, m_ln1_g, m_ln1_b, m_w_up, m_conv_w, m_conv_b, m_w_down, m_ln2_g, m_ln2_b, v_w_in, v_b_gate, v_sinks, v_w_branch_a, v_w_branch_b, v_w_out, v_ln1_g, v_ln1_b, v_w_up, v_conv_w, v_conv_b, v_w_down, v_ln2_g, v_ln2_b):
    args = dict(locals())
    sharded = COL_SHARDED + ROW_SHARDED
    w = {n: args[n][0] if n in sharded else args[n] for n in ORDER}
    m = {n: args["m_" + n][0] if n in sharded else args["m_" + n] for n in ORDER}
    v = {n: args["v_" + n][0] if n in sharded else args["v_" + n] for n in ORDER}

    travel = {n: (w[n] if n == "conv_w" else w[n].astype(BF16)) for n in sharded}
    (g_in,) = _all_gather([travel["w_in"]], "all_gather_w_in")
    w_in_full = _join_cols(g_in, "join_w_in")
    later = ("w_branch_a", "w_branch_b", "w_out", "w_up", "conv_w", "w_down")

    def join(name, slabs):
        return _join_cols(slabs, "join_" + name) if name in COL_SHARDED else slabs.reshape(-1, slabs.shape[-1])

    def split(name, grad):
        if name in COL_SHARDED:
            return _split_cols(grad if isinstance(grad, tuple) else (grad,), "split_d" + name)
        return grad.reshape((N_DEV, -1, grad.shape[-1]))

    def early_exchange(grads):
        return _Exchange([split(n, grads[n]) for n in later], ["scatter"] * len(later))

    def tail_exchange(grads, loss):
        small_pack = jnp.concatenate(
            [_pad_lanes(p) for n in SMALL for p in (grads[n] if isinstance(grads[n], tuple) else (grads[n],))] + [loss], axis=1)
        return _Exchange([split("w_in", grads["w_in"]), small_pack], ["scatter", "gather"])

    gather_later = _Exchange([travel[n] for n in later], ["gather"] * len(later))
    _, grad_x, _, early_out, (recv_w_in, small_parts) = _local_step(
        x[0], positions[0], w_in_full, w["b_gate"], w["sinks"][0], w["ln1_g"], w["ln1_b"], w["conv_b"], w["ln2_g"], w["ln2_b"], loss_target[0],
        (gather_later, lambda arrived: [join(n, a) for n, a in zip(later, arrived)]), early_exchange, tail_exchange)
    recv = dict(zip(later, early_out), w_in=recv_w_in)

    res = {n: _reduce_adamw(recv[n], w[n], m[n], v[n], "reduce_adamw_" + n) for n in sharded}
    small_res, loss_sum = _reduce_adamw_small(small_parts, [w[n] for n in SMALL], [m[n] for n in SMALL], [v[n] for n in SMALL])
    res.update(zip(SMALL, small_res))
    out = [loss_sum[0, 0], grad_x[None]]
    for k in range(4):
        out += [res[n][k][None] if n in sharded else res[n][k] for n in ORDER]
    return tuple(out)
```

```python
import functools

import jax
import jax.numpy as jnp
import numpy as np
from jax import lax
from jax.experimental import pallas as pl
from jax.experimental.pallas import tpu as pltpu

D_MODEL = 1024
HEAD_DIM = 64
SWA_Q_HEADS = 8
SWA_KV_HEADS = 2
SB_HEADS = 8
WINDOW = 128
ROPE_THETA = 10000.0
D_FF = 2816
LN_EPS = 1e-5
DEPTH = 1
ALPHA = (2.0 * DEPTH) ** 0.25
SWA_Q_WIDTH = SWA_Q_HEADS * HEAD_DIM
SWA_KV_WIDTH = SWA_KV_HEADS * HEAD_DIM
SB_WIDTH = SB_HEADS * HEAD_DIM
GATE_WIDTH = 2 * D_MODEL
IN_WIDTHS = (SWA_Q_WIDTH, SWA_KV_WIDTH, SWA_KV_WIDTH, SB_WIDTH, SB_WIDTH, SB_WIDTH, GATE_WIDTH)
IN_TOTAL = sum(IN_WIDTHS)
ATTN_SCALE = HEAD_DIM ** -0.5

ADAM_LR = 0.001
ADAM_B1 = 0.9
ADAM_B2 = 0.999
ADAM_EPS = 1e-08
ADAM_WD = 0.01
ADAM_STEP = 10

N_DEV = 8
LANES = 128
SB_BLOCK = 256
SB_PAIRS = 4
VMEM_LIMIT = 56 * 1024 * 1024

F32 = jnp.float32
BF16 = jnp.bfloat16
ACT_DTYPE = BF16
MESH = pl.DeviceIdType.MESH


def _params(sem=None):
    return pltpu.CompilerParams(dimension_semantics=sem, vmem_limit_bytes=VMEM_LIMIT)


def _dot(a, b):
    return jnp.dot(a, b, preferred_element_type=F32)


def _dot_nt(a, b):
    return lax.dot_general(a, b, (((1,), (1,)), ((), ())), preferred_element_type=F32)


def _dot_tn(a, b):
    return lax.dot_general(a, b, (((0,), (0,)), ((), ())), preferred_element_type=F32)


def _split_bf16(v):
    hi = v.astype(BF16)
    lo = (v - hi.astype(F32)).astype(BF16)
    return hi, lo


def _matmul(a, b, *, kind, out_shape, grid, a_spec, b_spec, out_spec, name, add=None, add_spec=None, add_scale=1.0):
    dot = {"nn": _dot, "nt": _dot_nt, "tn": _dot_tn}[kind]

    def body(*refs):
        if add is None:
            a_ref, b_ref, o_ref = refs
        else:
            a_ref, b_ref, add_ref, o_ref = refs
        r = dot(a_ref[...].astype(BF16), b_ref[...].astype(BF16))
        if add is not None:
            r = r + add_scale * add_ref[...]
        o_ref[...] = r.astype(o_ref.dtype)

    ins = [a, b] + ([] if add is None else [add])
    specs = [a_spec, b_spec] + ([] if add is None else [add_spec])
    return pl.pallas_call(
        body, name=name, grid=grid, in_specs=specs, out_specs=out_spec, out_shape=out_shape,
        compiler_params=_params(("parallel",) * len(grid)),
    )(*ins)


def _rope_tables(pos_col, inv_freq_lanes):
    T = pos_col.shape[0]
    tm = min(512, T)

    def body(pos_ref, f_ref, cos_ref, sin_ref):
        ang = pos_ref[...].astype(F32) * f_ref[...]
        cos_ref[...] = jnp.cos(ang)
        sin_ref[...] = jnp.sin(ang)

    return pl.pallas_call(
        body, name="rope_tables", grid=(T // tm,),
        in_specs=[pl.BlockSpec((tm, 1), lambda i: (i, 0)), pl.BlockSpec((1, LANES), lambda i: (0, 0))],
        out_specs=[pl.BlockSpec((tm, LANES), lambda i: (i, 0))] * 2,
        out_shape=[jax.ShapeDtypeStruct((T, LANES), F32)] * 2,
        compiler_params=_params(("parallel",)),
    )(pos_col, inv_freq_lanes)


def _lane_iota(shape):
    return lax.broadcasted_iota(jnp.int32, shape, len(shape) - 1)


def _rot_half(t):
    first = (_lane_iota(t.shape) % HEAD_DIM) < (HEAD_DIM // 2)
    return jnp.where(first, -pltpu.roll(t, LANES - HEAD_DIM // 2, axis=1), pltpu.roll(t, HEAD_DIM // 2, axis=1))


def _rope(t, cos, sin):
    return t * cos + _rot_half(t) * sin


def _rope_transpose(d, cos, sin):
    return d * cos - _rot_half(d * sin)


_IN_DTYPES = (F32, F32, BF16, BF16, BF16, BF16, F32)


def _in_proj(x, w_in_b):
    T = x.shape[0]
    tm = min(256, T)
    offs = np.cumsum((0,) + IN_WIDTHS)

    def body(x_ref, w_ref, xb_ref, *outs):
        xb = x_ref[...].astype(BF16)
        xb_ref[...] = xb
        for o_ref, a, b in zip(outs, offs[:-1], offs[1:]):
            o_ref[...] = _dot(xb, w_ref[:, a:b]).astype(o_ref.dtype)

    row = lambda n: pl.BlockSpec((tm, n), lambda i: (i, 0))
    return pl.pallas_call(
        body, name="in_proj", grid=(T // tm,),
        in_specs=[row(D_MODEL), pl.BlockSpec((D_MODEL, IN_TOTAL), lambda i: (0, 0))],
        out_specs=[row(D_MODEL)] + [row(n) for n in IN_WIDTHS],
        out_shape=[jax.ShapeDtypeStruct((T, D_MODEL), BF16)] + [jax.ShapeDtypeStruct((T, n), dt) for n, dt in zip(IN_WIDTHS, _IN_DTYPES)],
        compiler_params=_params(("parallel",)),
    )(x, w_in_b)


def _swa_specs(T):
    blk = WINDOW
    cur = lambda n: pl.BlockSpec((blk, n), lambda i: (i, 0))
    prev = lambda n: pl.BlockSpec((blk, n), lambda i: (jnp.maximum(i - 1, 0), 0))
    return blk, cur, prev


def _swa_window(i, kp, kc, vp, vc, cp, cc, sp, sc):
    kwin = jnp.concatenate([_rope(kp, cp, sp), _rope(kc, cc, sc)], axis=0)
    vwin = jnp.concatenate([vp, vc], axis=0)
    lane = _lane_iota(kwin.shape)
    low = lane < HEAD_DIM
    ks, vs = [], []
    for g in range(SWA_KV_HEADS):
        k0 = jnp.where(low, kwin if g == 0 else pltpu.roll(kwin, HEAD_DIM, axis=1), 0.0)
        v0 = jnp.where(low, vwin if g == 0 else pltpu.roll(vwin, HEAD_DIM, axis=1), 0.0)
        ks.append((k0, pltpu.roll(k0, HEAD_DIM, axis=1)))
        vs.append((v0, pltpu.roll(v0, HEAD_DIM, axis=1)))
    blk = WINDOW
    r = lax.broadcasted_iota(jnp.int32, (blk, 2 * blk), 0)
    c = lax.broadcasted_iota(jnp.int32, (blk, 2 * blk), 1)
    rel = blk + r - c
    valid = (rel >= 0) & (rel < WINDOW) & ((c >= blk) | (i > 0))
    return ks, vs, valid


def _swa_probs(qh, kk, valid, sink):
    s = _dot_nt(qh, kk) * ATTN_SCALE
    s = jnp.where(valid, s, -1e30)
    m = jnp.maximum(jnp.max(s, axis=1, keepdims=True), sink)
    p = jnp.where(valid, jnp.exp(s - m), 0.0)
    es = jnp.exp(sink - m)
    den = jnp.sum(p, axis=1, keepdims=True) + es
    return p / den, es / den


def _swa_fwd(qa, ka, va, cos, sin, sinks):
    T = qa.shape[0]
    blk, cur, prev = _swa_specs(T)

    def body(sink_ref, q_ref, kp_ref, kc_ref, vp_ref, vc_ref, cp_ref, cc_ref, sp_ref, sc_ref, o_ref):
        i = pl.program_id(0)
        cc, sc = cc_ref[...], sc_ref[...]
        ks, vs, valid = _swa_window(i, kp_ref[...], kc_ref[...], vp_ref[...].astype(F32), vc_ref[...].astype(F32),
                                    cp_ref[...], cc, sp_ref[...], sc)
        lane = _lane_iota((blk, LANES))
        for pp in range(SWA_Q_HEADS // 2):
            g = pp // (SWA_Q_HEADS // SWA_KV_HEADS // 2)
            qp = _rope(q_ref[:, pp * LANES:(pp + 1) * LANES], cc, sc)
            out = jnp.zeros((blk, LANES), F32)
            for hh in range(2):
                half = (lane >= hh * HEAD_DIM) & (lane < (hh + 1) * HEAD_DIM)
                qh = jnp.where(half, qp, 0.0).astype(BF16)
                probs, _ = _swa_probs(qh, ks[g][hh].astype(BF16), valid, sink_ref[2 * pp + hh])
                out = out + _dot(probs.astype(BF16), vs[g][hh].astype(BF16))
            o_ref[:, pp * LANES:(pp + 1) * LANES] = out.astype(o_ref.dtype)

    return pl.pallas_call(
        body, name="swa_fwd", grid=(T // blk,),
        in_specs=[pl.BlockSpec(memory_space=pltpu.SMEM), cur(SWA_Q_WIDTH), prev(LANES), cur(LANES), prev(LANES), cur(LANES),
                  prev(LANES), cur(LANES), prev(LANES), cur(LANES)],
        out_specs=cur(SWA_Q_WIDTH),
        out_shape=jax.ShapeDtypeStruct((T, SWA_Q_WIDTH), BF16),
        compiler_params=_params(("parallel",)),
    )(sinks, qa, ka, ka, va, va, cos, cos, sin, sin)


def _swa_bwd(qa, ka, va, cos, sin, sinks, dya):
    T = qa.shape[0]
    blk, cur, prev = _swa_specs(T)
    full = lambda n: pl.BlockSpec((T, n), lambda i: (0, 0))

    def body(sink_ref, q_ref, kp_ref, kc_ref, vp_ref, vc_ref, cp_ref, cc_ref, sp_ref, sc_ref, do_ref,
             dq_ref, dk_out, dv_out, dsink_ref, dk_ref, dv_ref):
        i = pl.program_id(0)

        @pl.when(i == 0)
        def _():
            dk_ref[...] = jnp.zeros_like(dk_ref)
            dv_ref[...] = jnp.zeros_like(dv_ref)
            dsink_ref[...] = jnp.zeros_like(dsink_ref)

        cp, cc, sp, sc = cp_ref[...], cc_ref[...], sp_ref[...], sc_ref[...]
        ks, vs, valid = _swa_window(i, kp_ref[...], kc_ref[...], vp_ref[...].astype(F32), vc_ref[...].astype(F32), cp, cc, sp, sc)
        lane = _lane_iota((blk, LANES))
        lane1 = _lane_iota((1, LANES))
        dkw = jnp.zeros((2 * blk, LANES), F32)
        dvw = jnp.zeros((2 * blk, LANES), F32)
        dsink = jnp.zeros((1, LANES), F32)
        for pp in range(SWA_Q_HEADS // 2):
            g = pp // (SWA_Q_HEADS // SWA_KV_HEADS // 2)
            qp = _rope(q_ref[:, pp * LANES:(pp + 1) * LANES], cc, sc)
            dop = do_ref[:, pp * LANES:(pp + 1) * LANES]
            dqp = jnp.zeros((blk, LANES), F32)
            for hh in range(2):
                half = (lane >= hh * HEAD_DIM) & (lane < (hh + 1) * HEAD_DIM)
                qh = jnp.where(half, qp, 0.0).astype(BF16)
                doh = jnp.where(half, dop, 0.0).astype(BF16)
                kk = ks[g][hh].astype(BF16)
                vv = vs[g][hh].astype(BF16)
                probs, psink = _swa_probs(qh, kk, valid, sink_ref[2 * pp + hh])
                dp = _dot_nt(doh, vv)
                dsum = jnp.sum(probs * dp, axis=1, keepdims=True)
                ds = (probs * (dp - dsum) * ATTN_SCALE).astype(BF16)
                dsink = dsink + jnp.where(lane1 == 2 * pp + hh, -jnp.sum(psink * dsum), 0.0)
                dqp = dqp + _dot(ds, kk)
                dk_h = _dot_tn(ds, qh)
                dv_h = _dot_tn(probs.astype(BF16), doh)
                if hh != g:
                    dk_h = pltpu.roll(dk_h, HEAD_DIM, axis=1)
                    dv_h = pltpu.roll(dv_h, HEAD_DIM, axis=1)
                dkw = dkw + dk_h
                dvw = dvw + dv_h
            dq_ref[:, pp * LANES:(pp + 1) * LANES] = _rope_transpose(dqp, cc, sc).astype(dq_ref.dtype)
        dsink_ref[...] += dsink
        ip = jnp.maximum(i - 1, 0)
        rows_p = pl.ds(pl.multiple_of(ip * blk, blk), blk)
        rows_c = pl.ds(pl.multiple_of(i * blk, blk), blk)
        dk_ref[rows_p, :] += _rope_transpose(dkw[:blk], cp, sp)
        dv_ref[rows_p, :] += dvw[:blk]
        dk_ref[rows_c, :] += _rope_transpose(dkw[blk:], cc, sc)
        dv_ref[rows_c, :] += dvw[blk:]

        @pl.when(i == T // blk - 1)
        def _():
            dk_out[...] = dk_ref[...].astype(BF16)
            dv_out[...] = dv_ref[...].astype(BF16)

    return pl.pallas_call(
        body, name="swa_bwd", grid=(T // blk,),
        in_specs=[pl.BlockSpec(memory_space=pltpu.SMEM), cur(SWA_Q_WIDTH), prev(LANES), cur(LANES), prev(LANES), cur(LANES),
                  prev(LANES), cur(LANES), prev(LANES), cur(LANES), cur(SWA_Q_WIDTH)],
        out_specs=[cur(SWA_Q_WIDTH), full(LANES), full(LANES), pl.BlockSpec((1, LANES), lambda i: (0, 0))],
        out_shape=[jax.ShapeDtypeStruct((T, SWA_Q_WIDTH), BF16), jax.ShapeDtypeStruct((T, LANES), BF16),
                   jax.ShapeDtypeStruct((T, LANES), BF16), jax.ShapeDtypeStruct((1, LANES), F32)],
        scratch_shapes=[pltpu.VMEM((T, LANES), F32)] * 2,
        compiler_params=_params(("arbitrary",)),
    )(sinks, qa, ka, ka, va, va, cos, cos, sin, sin, dya)


class _Exchange:
    FLIPS = [(fx, fy, fc) for fx in (0, 1) for fy in (0, 1) for fc in (0, 1) if (fx, fy, fc) != (0, 0, 0)]

    def __init__(self, arrays, kinds):
        self.arrays, self.kinds, self.n = list(arrays), list(kinds), len(arrays)

    def out_shape(self):
        return [jax.ShapeDtypeStruct(a.shape if k == "scatter" else (N_DEV,) + a.shape, a.dtype) for a, k in zip(self.arrays, self.kinds)]

    def scratch(self):
        return [pltpu.SemaphoreType.DMA((self.n, 7)), pltpu.SemaphoreType.DMA((self.n, 7)), pltpu.SemaphoreType.DMA((self.n,))]

    def bind(self, ins, outs, send_sems, recv_sems, local_sems):
        x, y, c = lax.axis_index("x"), lax.axis_index("y"), lax.axis_index("c")
        me = 4 * x + 2 * y + c
        local, remote = [], []
        for a, kind in enumerate(self.kinds):
            mine = ins[a].at[me] if kind == "scatter" else ins[a]
            local.append(pltpu.make_async_copy(mine, outs[a].at[me], local_sems.at[a]))
            for k, (fx, fy, fc) in enumerate(self.FLIPS):
                peer = (x ^ fx, y ^ fy, c ^ fc)
                peer_slot = 4 * peer[0] + 2 * peer[1] + peer[2]
                src = ins[a].at[peer_slot] if kind == "scatter" else ins[a]
                sems = dict(send_sem=send_sems.at[a, k], recv_sem=recv_sems.at[a, k], device_id=peer, device_id_type=MESH)
                remote.append((pltpu.make_async_remote_copy(src_ref=src, dst_ref=outs[a].at[me], **sems),
                               pltpu.make_async_remote_copy(src_ref=src, dst_ref=outs[a].at[peer_slot], **sems)))

        def start():
            for cp in local:
                cp.start()
            for send, _ in remote:
                send.start()

        def wait():
            for send, arrival in remote:
                arrival.wait_recv()
                send.wait_send()
            for cp in local:
                cp.wait()

        return start, wait


def _hosted_call(body, name, grid, exchange, *, in_specs, out_specs, out_shape, semantics, args, scratch=(), prefetch=()):
    n = 0 if exchange is None else exchange.n
    n_pre, n_in, n_out, n_scratch = len(prefetch), len(in_specs), len(out_specs), len(scratch)

    def hosted(*refs):
        pre, rest = refs[:n_pre], refs[n_pre:]
        ins, rest = rest[:n_in], rest[n_in:]
        ex_ins, rest = rest[:n], rest[n:]
        outs, rest = rest[:n_out], rest[n_out:]
        ex_outs, rest = rest[:n], rest[n:]
        own, sems = rest[:n_scratch], rest[n_scratch:]
        if exchange is None:
            return body(*pre, *ins, *outs, *own)
        start, wait = exchange.bind(ex_ins, ex_outs, *sems)
        ids = [pl.program_id(d) for d in range(len(grid))]
        first = functools.reduce(jnp.logical_and, [i == 0 for i in ids])
        last = functools.reduce(jnp.logical_and, [i == g - 1 for i, g in zip(ids, grid)])
        pl.when(first)(start)
        body(*pre, *ins, *outs, *own)
        pl.when(last)(wait)

    grid_spec = pltpu.PrefetchScalarGridSpec(
        num_scalar_prefetch=n_pre, grid=grid, in_specs=list(in_specs) + [ANY] * n, out_specs=list(out_specs) + [ANY] * n,
        scratch_shapes=list(scratch) + ([] if exchange is None else exchange.scratch()))
    res = pl.pallas_call(
        hosted, name=name, grid_spec=grid_spec, out_shape=list(out_shape) + ([] if exchange is None else exchange.out_shape()),
        compiler_params=_params(semantics if exchange is None else ("arbitrary",) * len(grid)),
    )(*prefetch, *args, *([] if exchange is None else exchange.arrays))
    return res[:n_out], res[n_out:]


SOFTPLUS_LINEAR_FROM = 30.0


def _sb_scores(qm, k, valid):
    z = _dot_nt(qm, k)
    sp = jnp.where(z > SOFTPLUS_LINEAR_FROM, z, jnp.log(1.0 + jnp.exp(z)))
    log_beta = z - sp
    if valid is not None:
        sp = jnp.where(valid, sp, 0.0)
    return sp, log_beta


def _tri2(B, cmp):
    r = lax.broadcasted_iota(jnp.int32, (2 * B, B), 0) % B
    c = lax.broadcasted_iota(jnp.int32, (2 * B, B), 1)
    return cmp(r, c).astype(BF16)


def _tri_sum(v, tri2):
    hi, lo = _split_bf16(v)
    return _dot(jnp.concatenate([hi, lo], axis=1), tri2)


def _head_masks(x):
    low = _lane_iota(x.shape) < HEAD_DIM
    zero = jnp.zeros((), x.dtype)
    return jnp.where(low, x, zero), jnp.where(low, zero, x)


def _strictly_below(B):
    r = lax.broadcasted_iota(jnp.int32, (B, B), 0)
    c = lax.broadcasted_iota(jnp.int32, (B, B), 1)
    return c < r


def _sb_grid(T, descending):
    B = min(SB_BLOCK, T)
    n = T // B
    pairs = [(i, j) for i in range(n) for j in (range(i, -1, -1) if descending else range(i + 1))]
    return B, jnp.asarray([p[0] for p in pairs], jnp.int32), jnp.asarray([p[1] for p in pairs], jnp.int32)


N_PAIRS = SB_HEADS // 2
PAIR_COLS = [slice(p * LANES, (p + 1) * LANES) for p in range(N_PAIRS)]


def _sb_fwd(qb, kb, vb, exchange=None):
    T = qb.shape[0]
    B, i_tab, j_tab = _sb_grid(T, descending=True)
    n = T // B

    def body(i_ref, j_ref, q_ref, k_ref, v_ref, o_ref, a_ref, b_ref, acc_ref, c_ref, tri_ref):
        s = pl.program_id(0)
        i, j = i_ref[s], j_ref[s]

        @pl.when(s == 0)
        def _():
            tri_ref[...] = _tri2(B, lambda r, c: r > c)

        @pl.when(j == i)
        def _():
            acc_ref[...] = jnp.zeros_like(acc_ref)
            c_ref[...] = jnp.zeros_like(c_ref)

        def block(valid):
            for p, cols in enumerate(PAIR_COLS):
                qms = _head_masks(q_ref[:, cols] * ATTN_SCALE)
                k = k_ref[:, cols]
                probs = []
                for hh in range(2):
                    h = 2 * p + hh
                    sp, lb = _sb_scores(qms[hh], k, valid)
                    c = c_ref[h]
                    a = jnp.exp(lb - (c + _tri_sum(sp, tri_ref[...])))
                    beta = jnp.exp(lb)
                    if valid is not None:
                        a = jnp.where(valid, a, 0.0)
                        beta = jnp.where(valid, beta, 0.0)
                    probs.append(a.astype(BF16))
                    a_ref[h] = probs[-1]
                    b_ref[h] = beta.astype(BF16)
                    c_ref[h] = c + jnp.sum(sp, axis=1, keepdims=True)
                acc_ref[:, cols] += _dot(jnp.concatenate(probs, axis=1), jnp.concatenate(_head_masks(v_ref[:, cols]), axis=0))

        pl.when(j == i)(lambda: block(_strictly_below(B)))
        pl.when(j != i)(lambda: block(None))

        @pl.when(j == 0)
        def _():
            o_ref[...] = acc_ref[...].astype(o_ref.dtype)

    q_spec = pl.BlockSpec((B, SB_WIDTH), lambda s, i_ref, j_ref: (i_ref[s], 0))
    k_spec = pl.BlockSpec((B, SB_WIDTH), lambda s, i_ref, j_ref: (j_ref[s], 0))
    tile = pl.BlockSpec((None, None, SB_HEADS, B, B), lambda s, i_ref, j_ref: (i_ref[s], j_ref[s], 0, 0, 0))
    saved = jax.ShapeDtypeStruct((n, n, SB_HEADS, B, B), BF16)
    return _hosted_call(
        body, "sb_fwd", (int(i_tab.shape[0]),), exchange, prefetch=(i_tab, j_tab),
        in_specs=[q_spec, k_spec, k_spec], out_specs=[q_spec, tile, tile],
        out_shape=[jax.ShapeDtypeStruct((T, SB_WIDTH), BF16), saved, saved],
        scratch=[pltpu.VMEM((B, SB_WIDTH), F32), pltpu.VMEM((SB_HEADS, B, 1), F32), pltpu.VMEM((2 * B, B), BF16)],
        semantics=("arbitrary",), args=(qb, kb, vb))


def _sb_bwd(qb, kb, vb, probs, betas, dyb, exchange=None):
    T = qb.shape[0]
    B, i_tab, j_tab = _sb_grid(T, descending=False)
    n_steps = int(i_tab.shape[0])

    def block_diag_t(x):
        xt = x.T
        top = lax.broadcasted_iota(jnp.int32, xt.shape, 0) < HEAD_DIM
        zero = jnp.zeros((), x.dtype)
        return jnp.concatenate([jnp.where(top, xt, zero), jnp.where(top, zero, xt)], axis=1)

    def body(i_ref, j_ref, q_ref, k_ref, v_ref, a_ref, b_ref, do_ref, dq_ref, dk_out, dv_out,
             dq_acc, cg_ref, dkt_ref, dvt_ref, tri_ref, qt_ref, dot_ref):
        s = pl.program_id(0)
        i, j = i_ref[s], j_ref[s]

        @pl.when(s == 0)
        def _():
            dkt_ref[...] = jnp.zeros_like(dkt_ref)
            dvt_ref[...] = jnp.zeros_like(dvt_ref)
            tri_ref[...] = _tri2(B, lambda r, c: r < c)[:B]

        @pl.when(j == 0)
        def _():
            dq_acc[...] = jnp.zeros_like(dq_acc)
            cg_ref[...] = jnp.zeros_like(cg_ref)
            for p, cols in enumerate(PAIR_COLS):
                qt_ref[p] = block_diag_t(q_ref[:, cols] * ATTN_SCALE)
                dot_ref[p] = block_diag_t(do_ref[:, cols])

        for p, cols in enumerate(PAIR_COLS):
            doms = _head_masks(do_ref[:, cols])
            k, v = k_ref[:, cols], v_ref[:, cols]
            dzs = []
            for hh in range(2):
                h = 2 * p + hh
                g = a_ref[h].astype(F32) * _dot_nt(doms[hh], v)
                cg = cg_ref[h]
                gsum = g + (cg + _dot(g.astype(BF16), tri_ref[...]))
                dzs.append((g - b_ref[h].astype(F32) * gsum).astype(BF16))
                cg_ref[h] = cg + jnp.sum(g, axis=1, keepdims=True)
            dq_acc[:, cols] += _dot(jnp.concatenate(dzs, axis=1), jnp.concatenate(_head_masks(k), axis=0))
            dkt_ref[j, cols, :] += _dot(qt_ref[p], jnp.concatenate(dzs, axis=0))
            dvt_ref[j, cols, :] += _dot(dot_ref[p], jnp.concatenate([a_ref[2 * p], a_ref[2 * p + 1]], axis=0))

        @pl.when(j == i)
        def _():
            dq_ref[...] = (dq_acc[...] * ATTN_SCALE).astype(dq_ref.dtype)

        @pl.when(s == n_steps - 1)
        def _():
            for jb in range(T // B):
                dk_out[jb * B:(jb + 1) * B, :] = dkt_ref[jb].T.astype(BF16)
                dv_out[jb * B:(jb + 1) * B, :] = dvt_ref[jb].T.astype(BF16)

    q_spec = pl.BlockSpec((B, SB_WIDTH), lambda s, i_ref, j_ref: (i_ref[s], 0))
    k_spec = pl.BlockSpec((B, SB_WIDTH), lambda s, i_ref, j_ref: (j_ref[s], 0))
    tile = pl.BlockSpec((None, None, SB_HEADS, B, B), lambda s, i_ref, j_ref: (i_ref[s], j_ref[s], 0, 0, 0))
    full = pl.BlockSpec((T, SB_WIDTH), lambda s, i_ref, j_ref: (0, 0))
    return _hosted_call(
        body, "sb_bwd", (n_steps,), exchange, prefetch=(i_tab, j_tab),
        in_specs=[q_spec, k_spec, k_spec, tile, tile, q_spec], out_specs=[q_spec, full, full],
        out_shape=[jax.ShapeDtypeStruct((T, SB_WIDTH), BF16)] * 3,
        scratch=[pltpu.VMEM((B, SB_WIDTH), F32), pltpu.VMEM((SB_HEADS, B, 1), F32), pltpu.VMEM((T // B, SB_WIDTH, B), F32),
                 pltpu.VMEM((T // B, SB_WIDTH, B), F32), pltpu.VMEM((B, B), BF16), pltpu.VMEM((N_PAIRS, LANES, 2 * B), BF16),
                 pltpu.VMEM((N_PAIRS, LANES, 2 * B), BF16)],
        semantics=("arbitrary",), args=(qb, kb, vb, probs, betas, dyb))


def _ln_stats(u):
    mu = jnp.mean(u, axis=-1, keepdims=True)
    xc = u - mu
    var = jnp.mean(xc * xc, axis=-1, keepdims=True)
    rstd = lax.rsqrt(var + LN_EPS)
    return xc * rstd, rstd


def _ln_bwd(dy, xhat, rstd, g):
    dxh = dy * g
    return rstd * (dxh - jnp.mean(dxh, axis=-1, keepdims=True) - xhat * jnp.mean(dxh * xhat, axis=-1, keepdims=True))


def _gates(gl_ref, bg_ref):
    ga = jax.nn.sigmoid(gl_ref[:, :D_MODEL] + bg_ref[:, :D_MODEL])
    gb = jax.nn.sigmoid(gl_ref[:, D_MODEL:] + bg_ref[:, D_MODEL:])
    return ga, gb


def _mix_fwd(ya, yb, gl, x, wa, wb, wo, b_gate, ln1_g, ln1_b):
    T = x.shape[0]
    tm = min(256, T)

    def body(ya_ref, yb_ref, gl_ref, x_ref, wa_ref, wb_ref, wo_ref, bg_ref, g_ref, b_ref, h_ref, u_ref, x1_ref):
        ga, gb = _gates(gl_ref, bg_ref)
        h = (ga * _dot(ya_ref[...], wa_ref[...]) + gb * _dot(yb_ref[...], wb_ref[...])).astype(BF16)
        h_ref[...] = h
        u = ALPHA * x_ref[...] + _dot(h, wo_ref[...])
        u_ref[...] = u
        xhat, _ = _ln_stats(u)
        x1_ref[...] = (xhat * g_ref[...] + b_ref[...]).astype(BF16)

    row = lambda n: pl.BlockSpec((tm, n), lambda i: (i, 0))
    const = lambda r, n: pl.BlockSpec((r, n), lambda i: (0, 0))
    return pl.pallas_call(
        body, name="mix_fwd", grid=(T // tm,),
        in_specs=[row(SWA_Q_WIDTH), row(SB_WIDTH), row(GATE_WIDTH), row(D_MODEL), const(SWA_Q_WIDTH, D_MODEL), const(SB_WIDTH, D_MODEL),
                  const(D_MODEL, D_MODEL), const(1, GATE_WIDTH), const(1, D_MODEL), const(1, D_MODEL)],
        out_specs=[row(D_MODEL)] * 3,
        out_shape=[jax.ShapeDtypeStruct((T, D_MODEL), BF16), jax.ShapeDtypeStruct((T, D_MODEL), F32), jax.ShapeDtypeStruct((T, D_MODEL), BF16)],
        compiler_params=_params(("parallel",)),
    )(ya, yb, gl, x, wa, wb, wo, b_gate, ln1_g, ln1_b)


def _mix_bwd(du1, ya, yb, gl, wa, wb, wo, b_gate):
    T = du1.shape[0]
    tm = min(256, T)

    def body(du_ref, ya_ref, yb_ref, gl_ref, wa_ref, wb_ref, wo_ref, bg_ref, dya_ref, dyb_ref, dgl_ref, dta_ref, dtb_ref, dbg_ref):
        @pl.when(pl.program_id(0) == 0)
        def _():
            dbg_ref[...] = jnp.zeros_like(dbg_ref)

        dh = _dot_nt(du_ref[...].astype(BF16), wo_ref[...])
        ga, gb = _gates(gl_ref, bg_ref)
        for gate, y_ref, w_ref, dy_ref, dt_ref, lo in ((ga, ya_ref, wa_ref, dya_ref, dta_ref, 0), (gb, yb_ref, wb_ref, dyb_ref, dtb_ref, D_MODEL)):
            t = _dot(y_ref[...], w_ref[...])
            dlogit = dh * t * gate * (1.0 - gate)
            dgl_ref[:, lo:lo + D_MODEL] = dlogit.astype(BF16)
            dbg_ref[:, lo:lo + D_MODEL] += jnp.sum(dlogit, axis=0, keepdims=True)
            dt = (dh * gate).astype(BF16)
            dt_ref[...] = dt
            dy_ref[...] = _dot_nt(dt, w_ref[...]).astype(BF16)

    row = lambda n: pl.BlockSpec((tm, n), lambda i: (i, 0))
    const = lambda r, n: pl.BlockSpec((r, n), lambda i: (0, 0))
    sds = lambda n, dt: jax.ShapeDtypeStruct((T, n), dt)
    return pl.pallas_call(
        body, name="mix_bwd", grid=(T // tm,),
        in_specs=[row(D_MODEL), row(SWA_Q_WIDTH), row(SB_WIDTH), row(GATE_WIDTH), const(SWA_Q_WIDTH, D_MODEL), const(SB_WIDTH, D_MODEL),
                  const(D_MODEL, D_MODEL), const(1, GATE_WIDTH)],
        out_specs=[row(SWA_Q_WIDTH), row(SB_WIDTH), row(GATE_WIDTH), row(D_MODEL), row(D_MODEL), const(1, GATE_WIDTH)],
        out_shape=[sds(SWA_Q_WIDTH, BF16), sds(SB_WIDTH, BF16), sds(GATE_WIDTH, BF16), sds(D_MODEL, BF16), sds(D_MODEL, BF16),
                   jax.ShapeDtypeStruct((1, GATE_WIDTH), F32)],
        compiler_params=_params(("arbitrary",)),
    )(du1, ya, yb, gl, wa, wb, wo, b_gate)


CONV_COLS = LANES


CONV_CHUNK = 64
CONV_CHUNK_FWD = 256
HALO = 8


def _taps(ref, r0, rows, lead):
    return [ref[pl.ds(r0 + lead + k, rows), :] for k in ((-2, -1, 0) if lead else (0, 1, 2))]


def _chunks(T, rows, step, init=None):
    def body(c, carry):
        out = step(pl.multiple_of(c * rows, rows), *(() if init is None else (carry,)))
        return carry if init is None else out
    return lax.fori_loop(0, T // rows, body, 0 if init is None else init)


def _conv_chunk(taps, w_ref, b_ref):
    return w_ref[0:1, :] * taps[0] + w_ref[1:2, :] * taps[1] + w_ref[2:3, :] * taps[2] + b_ref[...]


def _fold(x):
    return jnp.sum(x.reshape(x.shape[0] // 8, 8, x.shape[1]), axis=0)


def _conv_specs(T):
    nb = D_FF // CONV_COLS
    pair = pl.BlockSpec((2, T, CONV_COLS), lambda j: (0, 0, j))
    gate = lambda r: pl.BlockSpec((r, CONV_COLS), lambda j: (0, j))
    up = lambda r: pl.BlockSpec((r, CONV_COLS), lambda j: (0, j + nb))
    return nb, pair, gate, up


def _conv_glu_fwd(p3, conv_w, conv_b):
    T = p3.shape[1]
    nb, pair, gate, up = _conv_specs(T)

    R = min(CONV_CHUNK_FWD, T)

    def body(p_ref, wg_ref, wu_ref, bg_ref, bu_ref, s_ref, pg_s, pu_s):
        for half, scr in enumerate((pg_s, pu_s)):
            scr[0:HALO, :] = jnp.zeros((HALO, CONV_COLS), F32)
            scr[HALO:HALO + T, :] = p_ref[half].astype(F32)
        def step(r0):
            ag = _conv_chunk(_taps(pg_s, r0, R, HALO), wg_ref, bg_ref)
            au = _conv_chunk(_taps(pu_s, r0, R, HALO), wu_ref, bu_ref)
            s_ref[pl.ds(r0, R), :] = (ag * jax.nn.sigmoid(ag) * au).astype(BF16)

        _chunks(T, R, step)

    return pl.pallas_call(
        body, name="conv_glu_fwd", grid=(nb,),
        in_specs=[pair, gate(3), up(3), gate(1), up(1)],
        out_specs=pl.BlockSpec((T, CONV_COLS), lambda j: (0, j)),
        out_shape=jax.ShapeDtypeStruct((T, D_FF), BF16),
        scratch_shapes=[pltpu.VMEM((T + HALO, CONV_COLS), F32)] * 2,
        compiler_params=_params(("parallel",)),
    )(p3, conv_w, conv_w, conv_b, conv_b)


def _conv_glu_bwd(p3, ds, conv_w, conv_b):
    T = p3.shape[1]
    nb, pair, gate, up = _conv_specs(T)

    R = min(CONV_CHUNK, T)

    def body(p_ref, ds_ref, wg_ref, wu_ref, bg_ref, bu_ref, dp_ref, dwg_ref, dwu_ref, dbg_ref, dbu_ref, pg_s, pu_s, dag_s, dau_s):
        for half, scr in enumerate((pg_s, pu_s)):
            scr[0:HALO, :] = jnp.zeros((HALO, CONV_COLS), F32)
            scr[HALO:HALO + T, :] = p_ref[half].astype(F32)
        for scr in (dag_s, dau_s):
            scr[T:T + HALO, :] = jnp.zeros((HALO, CONV_COLS), F32)
        halves = ((pg_s, dag_s, wg_ref, dwg_ref, dbg_ref), (pu_s, dau_s, wu_ref, dwu_ref, dbu_ref))

        def step(r0, sums):
            taps = [_taps(p_s, r0, R, HALO) for p_s, *_ in halves]
            ag = _conv_chunk(taps[0], wg_ref, bg_ref)
            au = _conv_chunk(taps[1], wu_ref, bu_ref)
            sg = jax.nn.sigmoid(ag)
            d = ds_ref[pl.ds(r0, R), :].astype(F32)
            das = (d * au * (sg * (1.0 + ag * (1.0 - sg))), d * ag * sg)
            out = []
            for half, (_, da_s, *_) in enumerate(halves):
                da_s[pl.ds(r0, R), :] = das[half]
                out.append(tuple(sums[half][k] + _fold(das[half] * taps[half][k]) for k in range(3)) + (sums[half][3] + _fold(das[half]),))
            return tuple(out)

        sums = _chunks(T, R, step, ((jnp.zeros((8, CONV_COLS), F32),) * 4,) * 2)
        for half, (_, da_s, w_ref, dw_ref, db_ref) in enumerate(halves):
            for k in range(3):
                dw_ref[k:k + 1, :] = jnp.sum(sums[half][k], axis=0, keepdims=True)
            db_ref[...] = jnp.sum(sums[half][3], axis=0, keepdims=True)

            def transposed(r0, da_s=da_s, w_ref=w_ref, half=half):
                da0, da1, da2 = _taps(da_s, r0, R, 0)
                dp_ref[half, pl.ds(r0, R), :] = (w_ref[2:3, :] * da0 + w_ref[1:2, :] * da1 + w_ref[0:1, :] * da2).astype(BF16)

            _chunks(T, R, transposed)

    col = lambda r: pl.BlockSpec((r, CONV_COLS), lambda j: (0, j))
    return pl.pallas_call(
        body, name="conv_glu_bwd", grid=(nb,),
        in_specs=[pair, col(T), gate(3), up(3), gate(1), up(1)],
        out_specs=[pair, col(3), col(3), col(1), col(1)],
        out_shape=[jax.ShapeDtypeStruct((2, T, D_FF), BF16), jax.ShapeDtypeStruct((3, D_FF), F32), jax.ShapeDtypeStruct((3, D_FF), F32),
                   jax.ShapeDtypeStruct((1, D_FF), F32), jax.ShapeDtypeStruct((1, D_FF), F32)],
        scratch_shapes=[pltpu.VMEM((T + HALO, CONV_COLS), F32)] * 4,
        compiler_params=_params(("parallel",)),
    )(p3, ds, conv_w, conv_w, conv_b, conv_b)


def _ffn_down_loss(s, w_down, u1, ln1_g, ln1_b, ln2_g, ln2_b, target):
    T = u1.shape[0]
    tm = min(256, T)

    def body(s_ref, w_ref, u1_ref, g1_ref, b1_ref, g2_ref, b2_ref, t_ref, du_ref, dub_ref, dg_ref, db_ref, loss_ref):
        @pl.when(pl.program_id(0) == 0)
        def _():
            dg_ref[...] = jnp.zeros_like(dg_ref)
            db_ref[...] = jnp.zeros_like(db_ref)
            loss_ref[...] = jnp.zeros_like(loss_ref)

        xh1, _ = _ln_stats(u1_ref[...])
        x1 = xh1 * g1_ref[...] + b1_ref[...]
        u2 = ALPHA * x1 + _dot(s_ref[...], w_ref[...])
        xh2, rstd2 = _ln_stats(u2)
        err = xh2 * g2_ref[...] + b2_ref[...] - t_ref[...]
        per_token = jnp.mean(err * err, axis=-1, keepdims=True)
        loss_ref[...] += 0.5 * jnp.sum(per_token, axis=0, keepdims=True)
        dy = err * (1.0 / D_MODEL)
        dg_ref[...] += jnp.sum(dy * xh2, axis=0, keepdims=True)
        db_ref[...] += jnp.sum(dy, axis=0, keepdims=True)
        du2 = _ln_bwd(dy, xh2, rstd2, g2_ref[...])
        du_ref[...] = du2
        dub_ref[...] = du2.astype(BF16)

    row = lambda n: pl.BlockSpec((tm, n), lambda i: (i, 0))
    const = lambda r, n: pl.BlockSpec((r, n), lambda i: (0, 0))
    vec = const(1, D_MODEL)
    return pl.pallas_call(
        body, name="ffn_down_loss", grid=(T // tm,),
        in_specs=[row(D_FF), const(D_FF, D_MODEL), row(D_MODEL), vec, vec, vec, vec, row(D_MODEL)],
        out_specs=[row(D_MODEL), row(D_MODEL), vec, vec, const(1, LANES)],
        out_shape=[jax.ShapeDtypeStruct((T, D_MODEL), F32), jax.ShapeDtypeStruct((T, D_MODEL), BF16), jax.ShapeDtypeStruct((1, D_MODEL), F32),
                   jax.ShapeDtypeStruct((1, D_MODEL), F32), jax.ShapeDtypeStruct((1, LANES), F32)],
        compiler_params=_params(("arbitrary",)),
    )(s, w_down, u1, ln1_g, ln1_b, ln2_g, ln2_b, target)


def _ffn_up_bwd_ln1(dp3, w_up, du2, u1, ln1_g):
    T = u1.shape[0]
    tm = min(256, T)

    def body(dp_ref, w_ref, du2_ref, u1_ref, g_ref, du_ref, dub_ref, dg_ref, db_ref):
        @pl.when(pl.program_id(0) == 0)
        def _():
            dg_ref[...] = jnp.zeros_like(dg_ref)
            db_ref[...] = jnp.zeros_like(db_ref)

        dx1 = _dot_nt(dp_ref[0], w_ref[:, :D_FF]) + _dot_nt(dp_ref[1], w_ref[:, D_FF:]) + ALPHA * du2_ref[...]
        xh, rstd = _ln_stats(u1_ref[...])
        dg_ref[...] += jnp.sum(dx1 * xh, axis=0, keepdims=True)
        db_ref[...] += jnp.sum(dx1, axis=0, keepdims=True)
        du1 = _ln_bwd(dx1, xh, rstd, g_ref[...])
        du_ref[...] = du1
        dub_ref[...] = du1.astype(BF16)

    row = lambda n: pl.BlockSpec((tm, n), lambda i: (i, 0))
    const = lambda r, n: pl.BlockSpec((r, n), lambda i: (0, 0))
    vec = const(1, D_MODEL)
    return pl.pallas_call(
        body, name="ffn_up_bwd_ln1", grid=(T // tm,),
        in_specs=[pl.BlockSpec((2, tm, D_FF), lambda i: (0, i, 0)), const(D_MODEL, 2 * D_FF), row(D_MODEL), row(D_MODEL), vec],
        out_specs=[row(D_MODEL), row(D_MODEL), vec, vec],
        out_shape=[jax.ShapeDtypeStruct((T, D_MODEL), F32), jax.ShapeDtypeStruct((T, D_MODEL), BF16), jax.ShapeDtypeStruct((1, D_MODEL), F32),
                   jax.ShapeDtypeStruct((1, D_MODEL), F32)],
        compiler_params=_params(("arbitrary",)),
    )(dp3, w_up, du2, u1, ln1_g)


def _local_step(x, positions, w_in, b_gate, sinks, ln1_g, ln1_b, conv_b, ln2_g, ln2_b, target, later_weights,
                early_exchange=None, tail_exchange=None):
    T = x.shape[0]
    inv_freq = 1.0 / (ROPE_THETA ** (jnp.arange(0, HEAD_DIM, 2, dtype=F32) / HEAD_DIM))
    cos, sin = _rope_tables(positions.reshape(T, 1), jnp.tile(inv_freq, LANES // (HEAD_DIM // 2)).reshape(1, LANES))

    xb, qa, ka, va, qb, kb, vb, gl = _in_proj(x, w_in)
    ya = _swa_fwd(qa, ka, va, cos, sin, sinks)
    if isinstance(later_weights, tuple):
        exchange, finish = later_weights
        (yb, probs, betas), arrived = _sb_fwd(qb, kb, vb, exchange)
        later_weights = finish(arrived)
    else:
        (yb, probs, betas), _ = _sb_fwd(qb, kb, vb)
    wa, wb, wo, w_up, conv_w, w_down = later_weights
    h, u1, x1 = _mix_fwd(ya, yb, gl, x, wa, wb, wo, b_gate, ln1_g, ln1_b)

    ff_tn = D_FF // 2
    nff = D_FF // ff_tn
    tm = min(512, T)
    p3 = _matmul(x1, w_up, kind="nn", name="ffn_up", grid=(T // tm, 2 * nff),
                 a_spec=pl.BlockSpec((tm, D_MODEL), lambda i, j: (i, 0)), b_spec=pl.BlockSpec((D_MODEL, ff_tn), lambda i, j: (0, j)),
                 out_spec=pl.BlockSpec((None, tm, ff_tn), lambda i, j: (j // nff, i, j % nff)),
                 out_shape=jax.ShapeDtypeStruct((2, T, D_FF), ACT_DTYPE))
    s = _conv_glu_fwd(p3, conv_w, conv_b)
    du2, du2b, dln2_g, dln2_b, loss = _ffn_down_loss(s, w_down, u1, ln1_g, ln1_b, ln2_g, ln2_b, target)

    ds = _matmul(du2b, w_down, kind="nt", name="ffn_down_bwd", grid=(T // tm, nff),
                 a_spec=pl.BlockSpec((tm, D_MODEL), lambda i, j: (i, 0)), b_spec=pl.BlockSpec((ff_tn, D_MODEL), lambda i, j: (j, 0)),
                 out_spec=pl.BlockSpec((tm, ff_tn), lambda i, j: (i, j)), out_shape=jax.ShapeDtypeStruct((T, D_FF), ACT_DTYPE))
    dp3, dcw_g, dcw_u, dcb_g, dcb_u = _conv_glu_bwd(p3, ds, conv_w, conv_b)
    tk = 256
    dw_down = _matmul(s, du2b, kind="tn", name="dw_down", grid=(D_FF // tk,),
                      a_spec=pl.BlockSpec((T, tk), lambda i: (0, i)), b_spec=pl.BlockSpec((T, D_MODEL), lambda i: (0, 0)),
                      out_spec=pl.BlockSpec((tk, D_MODEL), lambda i: (i, 0)), out_shape=jax.ShapeDtypeStruct((D_FF, D_MODEL), BF16))
    dw_up = _matmul(x1, dp3, kind="tn", name="dw_up", grid=(D_MODEL // 512, 2 * nff),
                    a_spec=pl.BlockSpec((T, 512), lambda i, j: (0, i)), b_spec=pl.BlockSpec((None, T, ff_tn), lambda i, j: (j // nff, 0, j % nff)),
                    out_spec=pl.BlockSpec((512, ff_tn), lambda i, j: (i, j)), out_shape=jax.ShapeDtypeStruct((D_MODEL, 2 * D_FF), BF16))
    du1, du1b, dln1_g, dln1_b = _ffn_up_bwd_ln1(dp3, w_up, du2, u1, ln1_g)
    dya, dyb, dgl, dta, dtb, db_gate = _mix_bwd(du1, ya, yb, gl, wa, wb, wo, b_gate)

    def dw_tn(a, g, name):
        rows, cols = a.shape[1], g.shape[1]
        tn = min(512, cols)
        return _matmul(a, g, kind="tn", name=name, grid=(rows // 512, cols // tn),
                       a_spec=pl.BlockSpec((T, 512), lambda i, j: (0, i)), b_spec=pl.BlockSpec((T, tn), lambda i, j: (0, j)),
                       out_spec=pl.BlockSpec((512, tn), lambda i, j: (i, j)), out_shape=jax.ShapeDtypeStruct((rows, cols), BF16))

    dwa = dw_tn(ya, dta, "dw_branch_a")
    dwb = dw_tn(yb, dtb, "dw_branch_b")
    dwo = dw_tn(h, du1b, "dw_out")

    grads = dict(
        b_gate=db_gate, w_branch_a=dwa, w_branch_b=dwb, w_out=dwo, ln1_g=dln1_g, ln1_b=dln1_b,
        w_up=dw_up, conv_w=jnp.concatenate([dcw_g, dcw_u], axis=1), conv_b=(dcb_g, dcb_u), w_down=dw_down, ln2_g=dln2_g, ln2_b=dln2_b)
    (dqb, dkb, dvb), early_out = _sb_bwd(qb, kb, vb, probs, betas, dyb, early_exchange(grads) if early_exchange else None)
    dqa, dka, dva, grads["sinks"] = _swa_bwd(qa, ka, va, cos, sin, sinks, dya)
    dproj = (dqa, dka, dva, dqb, dkb, dvb, dgl)
    grads["w_in"] = _dw_in(xb, dproj)
    grad_x, tail_out = _grad_x(dproj, w_in, du1, tail_exchange(grads, loss) if tail_exchange else None)
    return loss, grad_x, grads, early_out, tail_out


def _dw_in(xb, dproj):
    T = xb.shape[0]
    tn = 2 * LANES
    groups, start, k = [], 0, 0
    while k < len(IN_WIDTHS):
        if IN_WIDTHS[k] >= tn:
            groups.append((start, IN_WIDTHS[k] // tn, [(k, 0, tn)]))
            k += 1
        else:
            members, off = [], 0
            while off < tn:
                members.append((k, off, IN_WIDTHS[k]))
                off += IN_WIDTHS[k]
                k += 1
            groups.append((start, 1, members))
        start += groups[-1][1]

    def body(x_ref, *refs):
        pieces, o_ref = refs[:-1], refs[-1]
        j = pl.program_id(0)
        for first, steps, members in groups:
            @pl.when((j >= first) & (j < first + steps))
            def _(members=members):
                for k, off, width in members:
                    o_ref[:, off:off + width] = _dot_tn(x_ref[...], pieces[k][...]).astype(o_ref.dtype)

    specs = [None] * len(IN_WIDTHS)
    for first, steps, members in groups:
        for k, _, width in members:
            specs[k] = pl.BlockSpec((T, width), lambda j, first=first, steps=steps: (0, jnp.clip(j - first, 0, steps - 1)))
    return pl.pallas_call(
        body, name="dw_in", grid=(IN_TOTAL // tn,),
        in_specs=[pl.BlockSpec((T, D_MODEL), lambda j: (0, 0))] + specs, out_specs=pl.BlockSpec((D_MODEL, tn), lambda j: (0, j)),
        out_shape=jax.ShapeDtypeStruct((D_MODEL, IN_TOTAL), BF16), compiler_params=_params(("arbitrary",)),
    )(xb, *dproj)


def _grad_x(dproj, w_in, du1, exchange=None):
    T = du1.shape[0]
    tm = min(256, T)
    offs = np.cumsum((0,) + IN_WIDTHS)

    def body(*refs):
        pieces, (w_ref, du_ref, o_ref) = refs[:len(IN_WIDTHS)], refs[len(IN_WIDTHS):]
        acc = ALPHA * du_ref[...]
        for p_ref, a, b in zip(pieces, offs[:-1], offs[1:]):
            acc = acc + _dot_nt(p_ref[...].astype(BF16), w_ref[:, a:b])
        o_ref[...] = acc

    row = lambda n: pl.BlockSpec((tm, n), lambda i: (i, 0))
    (grad_x,), arrived = _hosted_call(
        body, "grad_x", (T // tm,), exchange,
        in_specs=[row(n) for n in IN_WIDTHS] + [pl.BlockSpec((D_MODEL, IN_TOTAL), lambda i: (0, 0)), row(D_MODEL)],
        out_specs=[row(D_MODEL)], out_shape=[jax.ShapeDtypeStruct((T, D_MODEL), F32)], semantics=("parallel",),
        args=(*dproj, w_in, du1))
    return grad_x, arrived


ANY = pl.BlockSpec(memory_space=pl.ANY)


def _all_gather(slabs, name):
    n = len(slabs)

    def body(*refs):
        ins, outs = refs[:n], refs[n:2 * n]
        send_sems, recv_sems, local_sems = refs[2 * n:]
        x, y, c = lax.axis_index("x"), lax.axis_index("y"), lax.axis_index("c")
        me, sibling = (x, y, c), (x, y, 1 - c)
        chips = [(1 - x, y), (x, 1 - y), (1 - x, 1 - y)]

        def slot(pos):
            return 4 * pos[0] + 2 * pos[1] + pos[2]

        def copy(a, k, block, to, from_input=False):
            return pltpu.make_async_remote_copy(
                src_ref=ins[a] if from_input else outs[a].at[slot(block)], dst_ref=outs[a].at[slot(block)],
                send_sem=send_sems.at[a, k], recv_sem=recv_sems.at[a, k], device_id=to, device_id_type=MESH)

        mine = [pltpu.make_async_copy(ins[a], outs[a].at[slot(me)], local_sems.at[a]) for a in range(n)]
        for cp in mine:
            cp.start()
        first = []
        for a in range(n):
            first.append(copy(a, 0, me, sibling, from_input=True))
            first += [copy(a, 1 + j, me, (*chip, c), from_input=True) for j, chip in enumerate(chips)]
        for cp in first:
            cp.start()
        passed = []
        for j, chip in enumerate(chips):
            for a in range(n):
                copy(a, 1 + j, (*chip, c), me).wait_recv()
                fwd = copy(a, 4 + j, (*chip, c), sibling)
                fwd.start()
                passed.append(fwd)
        for a in range(n):
            copy(a, 0, sibling, me).wait_recv()
            for j, chip in enumerate(chips):
                copy(a, 4 + j, (*chip, 1 - c), me).wait_recv()
        for cp in first + passed:
            cp.wait_send()
        for cp in mine:
            cp.wait()

    return pl.pallas_call(
        body, name=name,
        in_specs=[ANY] * n, out_specs=[ANY] * n,
        out_shape=[jax.ShapeDtypeStruct((N_DEV,) + s.shape, s.dtype) for s in slabs],
        scratch_shapes=[pltpu.SemaphoreType.DMA((n, 7)), pltpu.SemaphoreType.DMA((n, 7)), pltpu.SemaphoreType.DMA((n,))],
    )(*slabs)


def _all_to_all(slabs, name):
    n = len(slabs)

    def body(*refs):
        ins, outs = refs[:n], refs[n:2 * n]
        send_sems, recv_sems, local_sems = refs[2 * n:]
        x, y, c = lax.axis_index("x"), lax.axis_index("y"), lax.axis_index("c")
        my_slot = 4 * x + 2 * y + c
        flips = [(fx, fy, fc) for fx in (0, 1) for fy in (0, 1) for fc in (0, 1) if (fx, fy, fc) != (0, 0, 0)]

        def copy(a, k):
            fx, fy, fc = flips[k]
            peer = (x ^ fx, y ^ fy, c ^ fc)
            peer_slot = 4 * peer[0] + 2 * peer[1] + peer[2]
            send = pltpu.make_async_remote_copy(src_ref=ins[a].at[peer_slot], dst_ref=outs[a].at[my_slot], send_sem=send_sems.at[a, k],
                                                recv_sem=recv_sems.at[a, k], device_id=peer, device_id_type=MESH)
            recv = pltpu.make_async_remote_copy(src_ref=ins[a].at[peer_slot], dst_ref=outs[a].at[peer_slot], send_sem=send_sems.at[a, k],
                                                recv_sem=recv_sems.at[a, k], device_id=peer, device_id_type=MESH)
            return send, recv

        mine = [pltpu.make_async_copy(ins[a].at[my_slot], outs[a].at[my_slot], local_sems.at[a]) for a in range(n)]
        for cp in mine:
            cp.start()
        copies = [copy(a, k) for a in range(n) for k in range(len(flips))]
        for send, _ in copies:
            send.start()
        for send, recv in copies:
            recv.wait_recv()
            send.wait_send()
        for cp in mine:
            cp.wait()

    return pl.pallas_call(
        body, name=name,
        in_specs=[ANY] * n, out_specs=[ANY] * n,
        out_shape=[jax.ShapeDtypeStruct(s.shape, s.dtype) for s in slabs],
        scratch_shapes=[pltpu.SemaphoreType.DMA((n, 7)), pltpu.SemaphoreType.DMA((n, 7)), pltpu.SemaphoreType.DMA((n,))],
    )(*slabs)


def _row_tile(rows):
    for cand in range(256, 7, -8):
        if rows % cand == 0:
            return cand
    return rows


def _window(w):
    wp = max(-(-((w * r) % LANES + w) // LANES) for r in range(N_DEV)) * LANES
    assert all((w * r) // LANES * LANES + wp <= N_DEV * w for r in range(N_DEV))
    return wp


def _join_cols(slabs, name):
    _, R, w = slabs.shape
    tr = _row_tile(R)
    wp = _window(w)

    def body(g_ref, o_ref, pad_ref):
        if w % LANES == 0:
            for r in range(N_DEV):
                o_ref[:, w * r:w * (r + 1)] = g_ref[r]
            return
        o_ref[...] = jnp.zeros_like(o_ref)
        pad_ref[...] = jnp.zeros_like(pad_ref)
        for r in range(N_DEV):
            q, s = divmod(w * r, LANES)
            pad_ref[:, :w] = g_ref[r]
            y = pad_ref[...]
            if s:
                y = pltpu.roll(y, s, axis=1)
            o_ref[:, LANES * q:LANES * q + wp] += y

    return pl.pallas_call(
        body, name=name, grid=(R // tr,),
        in_specs=[pl.BlockSpec((N_DEV, tr, w), lambda i: (0, i, 0))], out_specs=pl.BlockSpec((tr, N_DEV * w), lambda i: (i, 0)),
        out_shape=jax.ShapeDtypeStruct((R, N_DEV * w), slabs.dtype), scratch_shapes=[pltpu.VMEM((tr, wp), slabs.dtype)],
        compiler_params=_params(("parallel",)),
    )(slabs)


def _split_cols(pieces, name):
    R = pieces[0].shape[0]
    widths = [p.shape[1] for p in pieces]
    total = sum(widths)
    w = total // N_DEV
    tr = _row_tile(R)
    wp = _window(w)
    offs = np.cumsum([0] + widths)
    dtype = pieces[0].dtype

    def body(*refs):
        ins, (o_ref, full_ref) = refs[:len(pieces)], refs[len(pieces):]
        for p_ref, a, b in zip(ins, offs[:-1], offs[1:]):
            full_ref[:, a:b] = p_ref[...].astype(dtype)
        for r in range(N_DEV):
            q, s = divmod(w * r, LANES)
            y = full_ref[:, LANES * q:LANES * q + wp]
            if s:
                y = pltpu.roll(y, wp - s, axis=1)
            o_ref[r] = y[:, :w]

    return pl.pallas_call(
        body, name=name, grid=(R // tr,),
        in_specs=[pl.BlockSpec((tr, n), lambda i: (i, 0)) for n in widths], out_specs=pl.BlockSpec((N_DEV, tr, w), lambda i: (0, i, 0)),
        out_shape=jax.ShapeDtypeStruct((N_DEV, R, w), dtype), scratch_shapes=[pltpu.VMEM((tr, total), dtype)],
        compiler_params=_params(("parallel",)),
    )(*pieces)


def _adamw(g, w, m, v):
    m_new = ADAM_B1 * m + (1.0 - ADAM_B1) * g
    v_new = ADAM_B2 * v + (1.0 - ADAM_B2) * jnp.square(g)
    m_hat = m_new / (1.0 - ADAM_B1 ** ADAM_STEP)
    v_hat = v_new / (1.0 - ADAM_B2 ** ADAM_STEP)
    return -ADAM_LR * (m_hat / (jnp.sqrt(v_hat) + ADAM_EPS) + ADAM_WD * w), m_new, v_new


def _sum_parts(p_ref):
    g = p_ref[0].astype(F32)
    for d in range(1, N_DEV):
        g = g + p_ref[d].astype(F32)
    return g


def _reduce_adamw(parts, w, m, v, name):
    R, C = w.shape
    tr = _row_tile(R)

    def body(p_ref, w_ref, m_ref, v_ref, g_ref, d_ref, mo_ref, vo_ref):
        g = _sum_parts(p_ref)
        g_ref[...] = g
        d_ref[...], mo_ref[...], vo_ref[...] = _adamw(g, w_ref[...], m_ref[...], v_ref[...])

    row = pl.BlockSpec((tr, C), lambda i: (i, 0))
    return pl.pallas_call(
        body, name=name, grid=(R // tr,),
        in_specs=[pl.BlockSpec((N_DEV, tr, C), lambda i: (0, i, 0)), row, row, row],
        out_specs=[row] * 4, out_shape=[jax.ShapeDtypeStruct((R, C), F32)] * 4,
        compiler_params=_params(("parallel",)),
    )(parts, w, m, v)


def _reduce_adamw_small(parts, ws, ms, vs):
    sizes = [a.shape[1] for a in ws]
    k = len(sizes)
    offs = np.cumsum([0] + [-(-n // LANES) * LANES for n in sizes])

    def body(*refs):
        p_ref, w_refs, m_refs, v_refs = refs[0], refs[1:1 + k], refs[1 + k:1 + 2 * k], refs[1 + 2 * k:1 + 3 * k]
        outs, loss_ref = refs[1 + 3 * k:-1], refs[-1]
        g_all = _sum_parts(p_ref)
        for j, n in enumerate(sizes):
            g = g_all[:, offs[j]:offs[j] + LANES * (-(-n // LANES))][:, :n]
            outs[4 * j][...] = g
            outs[4 * j + 1][...], outs[4 * j + 2][...], outs[4 * j + 3][...] = _adamw(g, w_refs[j][...], m_refs[j][...], v_refs[j][...])
        loss_ref[...] = g_all[:, offs[k]:offs[k] + LANES]

    vm = pl.BlockSpec(memory_space=pltpu.VMEM)
    out_shape = [jax.ShapeDtypeStruct((1, n), F32) for n in sizes for _ in range(4)] + [jax.ShapeDtypeStruct((1, LANES), F32)]
    res = pl.pallas_call(
        body, name="reduce_adamw_replicated", in_specs=[vm] * (1 + 3 * k), out_specs=[vm] * len(out_shape), out_shape=out_shape,
        compiler_params=_params(),
    )(parts, *ws, *ms, *vs)
    return [res[4 * j:4 * j + 4] for j in range(k)], res[-1]


COL_SHARDED = ("w_in", "w_branch_a", "w_branch_b", "w_up", "conv_w")
ROW_SHARDED = ("w_out", "w_down")
SMALL = ("b_gate", "sinks", "ln1_g", "ln1_b", "conv_b", "ln2_g", "ln2_b")
ORDER = ("w_in", "b_gate", "sinks", "w_branch_a", "w_branch_b", "w_out", "ln1_g", "ln1_b", "w_up", "conv_w", "conv_b", "w_down", "ln2_g", "ln2_b")


def _pad_lanes(a):
    pad = (-a.shape[-1]) % LANES
    return a if pad == 0 else jnp.pad(a, ((0, 0), (0, pad)))


def kernel(x, positions, w_in, b_gate, sinks, w_branch_a, w_branch_b, w_out, ln1_g, ln1_b, w_up, conv_w, conv_b, w_down, ln2_g, ln2_b, loss_target, m_w_in, m_b_gate, m_sinks, m_w_branch_a, m_w_branch_b, m_w_out, m_ln1_g, m_ln1_b, m_w_up, m_conv_w, m_conv_b, m_w_down, m_ln2_g, m_ln2_b, v_w_in, v_b_gate, v_sinks, v_w_branch_a, v_w_branch_b, v_w_out, v_ln1_g, v_ln1_b, v_w_up, v_conv_w, v_conv_b, v_w_down, v_ln2_g, v_ln2_b):
    args = dict(locals())
    sharded = COL_SHARDED + ROW_SHARDED
    w = {n: args[n][0] if n in sharded else args[n] for n in ORDER}
    m = {n: args["m_" + n][0] if n in sharded else args["m_" + n] for n in ORDER}
    v = {n: args["v_" + n][0] if n in sharded else args["v_" + n] for n in ORDER}

    travel = {n: (w[n] if n == "conv_w" else w[n].astype(BF16)) for n in sharded}
    (g_in,) = _all_gather([travel["w_in"]], "all_gather_w_in")
    w_in_full = _join_cols(g_in, "join_w_in")
    later = ("w_branch_a", "w_branch_b", "w_out", "w_up", "conv_w", "w_down")

    def join(name, slabs):
        return _join_cols(slabs, "join_" + name) if name in COL_SHARDED else slabs.reshape(-1, slabs.shape[-1])

    def split(name, grad):
        if name in COL_SHARDED:
            return _split_cols(grad if isinstance(grad, tuple) else (grad,), "split_d" + name)
        return grad.reshape((N_DEV, -1, grad.shape[-1]))

    def early_exchange(grads):
        return _Exchange([split(n, grads[n]) for n in later], ["scatter"] * len(later))

    def tail_exchange(grads, loss):
        small_pack = jnp.concatenate(
            [_pad_lanes(p) for n in SMALL for p in (grads[n] if isinstance(grads[n], tuple) else (grads[n],))] + [loss], axis=1)
        return _Exchange([split("w_in", grads["w_in"]), small_pack], ["scatter", "gather"])

    gather_later = _Exchange([travel[n] for n in later], ["gather"] * len(later))
    _, grad_x, _, early_out, (recv_w_in, small_parts) = _local_step(
        x[0], positions[0], w_in_full, w["b_gate"], w["sinks"][0], w["ln1_g"], w["ln1_b"], w["conv_b"], w["ln2_g"], w["ln2_b"], loss_target[0],
        (gather_later, lambda arrived: [join(n, a) for n, a in zip(later, arrived)]), early_exchange, tail_exchange)
    recv = dict(zip(later, early_out), w_in=recv_w_in)

    res = {n: _reduce_adamw(recv[n], w[n], m[n], v[n], "reduce_adamw_" + n) for n in sharded}
    small_res, loss_sum = _reduce_adamw_small(small_parts, [w[n] for n in SMALL], [m[n] for n in SMALL], [v[n] for n in SMALL])
    res.update(zip(SMALL, small_res))
    out = [loss_sum[0, 0], grad_x[None]]
    for k in range(4):
        out += [res[n][k][None] if n in sharded else res[n][k] for n in ORDER]
    return tuple(out)
```

```python
import functools

import jax
import jax.numpy as jnp
import numpy as np
from jax import lax
from jax.experimental import pallas as pl
from jax.experimental.pallas import tpu as pltpu

D_MODEL = 1024
HEAD_DIM = 64
SWA_Q_HEADS = 8
SWA_KV_HEADS = 2
SB_HEADS = 8
WINDOW = 128
ROPE_THETA = 10000.0
D_FF = 2816
LN_EPS = 1e-5
DEPTH = 1
ALPHA = (2.0 * DEPTH) ** 0.25
SWA_Q_WIDTH = SWA_Q_HEADS * HEAD_DIM
SWA_KV_WIDTH = SWA_KV_HEADS * HEAD_DIM
SB_WIDTH = SB_HEADS * HEAD_DIM
GATE_WIDTH = 2 * D_MODEL
IN_WIDTHS = (SWA_Q_WIDTH, SWA_KV_WIDTH, SWA_KV_WIDTH, SB_WIDTH, SB_WIDTH, SB_WIDTH, GATE_WIDTH)
IN_TOTAL = sum(IN_WIDTHS)
ATTN_SCALE = HEAD_DIM ** -0.5

ADAM_LR = 0.001
ADAM_B1 = 0.9
ADAM_B2 = 0.999
ADAM_EPS = 1e-08
ADAM_WD = 0.01
ADAM_STEP = 10

N_DEV = 8
LANES = 128
SB_BLOCK = 256
SB_PAIRS = 4
VMEM_LIMIT = 56 * 1024 * 1024

F32 = jnp.float32
BF16 = jnp.bfloat16
ACT_DTYPE = BF16
MESH = pl.DeviceIdType.MESH


def _params(sem=None):
    return pltpu.CompilerParams(dimension_semantics=sem, vmem_limit_bytes=VMEM_LIMIT)


def _dot(a, b):
    return jnp.dot(a, b, preferred_element_type=F32)


def _dot_nt(a, b):
    return lax.dot_general(a, b, (((1,), (1,)), ((), ())), preferred_element_type=F32)


def _dot_tn(a, b):
    return lax.dot_general(a, b, (((0,), (0,)), ((), ())), preferred_element_type=F32)


def _split_bf16(v):
    hi = v.astype(BF16)
    lo = (v - hi.astype(F32)).astype(BF16)
    return hi, lo


def _matmul(a, b, *, kind, out_shape, grid, a_spec, b_spec, out_spec, name, add=None, add_spec=None, add_scale=1.0):
    dot = {"nn": _dot, "nt": _dot_nt, "tn": _dot_tn}[kind]

    def body(*refs):
        if add is None:
            a_ref, b_ref, o_ref = refs
        else:
            a_ref, b_ref, add_ref, o_ref = refs
        r = dot(a_ref[...].astype(BF16), b_ref[...].astype(BF16))
        if add is not None:
            r = r + add_scale * add_ref[...]
        o_ref[...] = r.astype(o_ref.dtype)

    ins = [a, b] + ([] if add is None else [add])
    specs = [a_spec, b_spec] + ([] if add is None else [add_spec])
    return pl.pallas_call(
        body, name=name, grid=grid, in_specs=specs, out_specs=out_spec, out_shape=out_shape,
        compiler_params=_params(("parallel",) * len(grid)),
    )(*ins)


def _rope_tables(pos_col, inv_freq_lanes):
    T = pos_col.shape[0]
    tm = min(512, T)

    def body(pos_ref, f_ref, cos_ref, sin_ref):
        ang = pos_ref[...].astype(F32) * f_ref[...]
        cos_ref[...] = jnp.cos(ang)
        sin_ref[...] = jnp.sin(ang)

    return pl.pallas_call(
        body, name="rope_tables", grid=(T // tm,),
        in_specs=[pl.BlockSpec((tm, 1), lambda i: (i, 0)), pl.BlockSpec((1, LANES), lambda i: (0, 0))],
        out_specs=[pl.BlockSpec((tm, LANES), lambda i: (i, 0))] * 2,
        out_shape=[jax.ShapeDtypeStruct((T, LANES), F32)] * 2,
        compiler_params=_params(("parallel",)),
    )(pos_col, inv_freq_lanes)


def _lane_iota(shape):
    return lax.broadcasted_iota(jnp.int32, shape, len(shape) - 1)


def _rot_half(t):
    first = (_lane_iota(t.shape) % HEAD_DIM) < (HEAD_DIM // 2)
    return jnp.where(first, -pltpu.roll(t, LANES - HEAD_DIM // 2, axis=1), pltpu.roll(t, HEAD_DIM // 2, axis=1))


def _rope(t, cos, sin):
    return t * cos + _rot_half(t) * sin


def _rope_transpose(d, cos, sin):
    return d * cos - _rot_half(d * sin)


_IN_DTYPES = (F32, F32, BF16, BF16, BF16, BF16, F32)


def _in_proj(x, w_in_b):
    T = x.shape[0]
    tm = min(256, T)
    offs = np.cumsum((0,) + IN_WIDTHS)

    def body(x_ref, w_ref, xb_ref, *outs):
        xb = x_ref[...].astype(BF16)
        xb_ref[...] = xb
        for o_ref, a, b in zip(outs, offs[:-1], offs[1:]):
            o_ref[...] = _dot(xb, w_ref[:, a:b]).astype(o_ref.dtype)

    row = lambda n: pl.BlockSpec((tm, n), lambda i: (i, 0))
    return pl.pallas_call(
        body, name="in_proj", grid=(T // tm,),
        in_specs=[row(D_MODEL), pl.BlockSpec((D_MODEL, IN_TOTAL), lambda i: (0, 0))],
        out_specs=[row(D_MODEL)] + [row(n) for n in IN_WIDTHS],
        out_shape=[jax.ShapeDtypeStruct((T, D_MODEL), BF16)] + [jax.ShapeDtypeStruct((T, n), dt) for n, dt in zip(IN_WIDTHS, _IN_DTYPES)],
        compiler_params=_params(("parallel",)),
    )(x, w_in_b)


def _swa_specs(T):
    blk = WINDOW
    cur = lambda n: pl.BlockSpec((blk, n), lambda i: (i, 0))
    prev = lambda n: pl.BlockSpec((blk, n), lambda i: (jnp.maximum(i - 1, 0), 0))
    return blk, cur, prev


SWA_GROUP = SWA_Q_HEADS // SWA_KV_HEADS


def _swa_stack(pairs):
    lane = _lane_iota(pairs[0].shape)
    zero = jnp.zeros((), pairs[0].dtype)
    rows = []
    for h in range(SWA_Q_HEADS):
        hh, g = h % 2, h // SWA_GROUP
        x = jnp.where((lane >= hh * HEAD_DIM) & (lane < (hh + 1) * HEAD_DIM), pairs[h // 2], zero)
        rows.append(x if hh == g else pltpu.roll(x, HEAD_DIM, axis=1))
    return jnp.concatenate(rows, axis=0)


def _swa_unstack(stacked, blk):
    low = _lane_iota((blk, LANES)) < HEAD_DIM
    pairs = []
    for pp in range(SWA_Q_HEADS // 2):
        halves = []
        for hh in range(2):
            h = 2 * pp + hh
            x = stacked[h * blk:(h + 1) * blk]
            halves.append(x if hh == h // SWA_GROUP else pltpu.roll(x, HEAD_DIM, axis=1))
        pairs.append(jnp.where(low, halves[0], halves[1]))
    return pairs


def _swa_probs(i, q_stack, kwin, sink_ref, blk):
    r = lax.broadcasted_iota(jnp.int32, (blk, 2 * blk), 0)
    c = lax.broadcasted_iota(jnp.int32, (blk, 2 * blk), 1)
    rel = blk + r - c
    valid = (rel >= 0) & (rel < WINDOW) & ((c >= blk) | (i > 0))
    bias = jnp.concatenate([jnp.where(valid, 0.0, -1e30)] * SWA_Q_HEADS, axis=0)
    head = lax.broadcasted_iota(jnp.int32, (SWA_Q_HEADS * blk, 1), 0) // blk
    sink = jnp.zeros((SWA_Q_HEADS * blk, 1), F32)
    for h in range(SWA_Q_HEADS):
        sink = jnp.where(head == h, sink_ref[h], sink)
    s = _dot_nt(q_stack, kwin) * ATTN_SCALE + bias
    m = jnp.maximum(jnp.max(s, axis=1, keepdims=True), sink)
    p = jnp.exp(s - m)
    es = jnp.exp(sink - m)
    den = jnp.sum(p, axis=1, keepdims=True) + es
    return p / den, es / den


def _swa_inputs(q_ref, kp_ref, kc_ref, vp_ref, vc_ref, cp_ref, cc_ref, sp_ref, sc_ref):
    cc, sc = cc_ref[...], sc_ref[...]
    kwin = jnp.concatenate([_rope(kp_ref[...], cp_ref[...], sp_ref[...]), _rope(kc_ref[...], cc, sc)], axis=0).astype(BF16)
    vwin = jnp.concatenate([vp_ref[...], vc_ref[...]], axis=0)
    q_stack = _swa_stack([_rope(q_ref[:, pp * LANES:(pp + 1) * LANES], cc, sc) for pp in range(SWA_Q_HEADS // 2)]).astype(BF16)
    return q_stack, kwin, vwin


def _swa_fwd(qa, ka, va, cos, sin, sinks):
    T = qa.shape[0]
    blk, cur, prev = _swa_specs(T)

    def body(sink_ref, q_ref, kp_ref, kc_ref, vp_ref, vc_ref, cp_ref, cc_ref, sp_ref, sc_ref, o_ref):
        q_stack, kwin, vwin = _swa_inputs(q_ref, kp_ref, kc_ref, vp_ref, vc_ref, cp_ref, cc_ref, sp_ref, sc_ref)
        probs, _ = _swa_probs(pl.program_id(0), q_stack, kwin, sink_ref, blk)
        for pp, tile in enumerate(_swa_unstack(_dot(probs.astype(BF16), vwin), blk)):
            o_ref[:, pp * LANES:(pp + 1) * LANES] = tile.astype(o_ref.dtype)

    return pl.pallas_call(
        body, name="swa_fwd", grid=(T // blk,),
        in_specs=[pl.BlockSpec(memory_space=pltpu.SMEM), cur(SWA_Q_WIDTH), prev(LANES), cur(LANES), prev(LANES), cur(LANES),
                  prev(LANES), cur(LANES), prev(LANES), cur(LANES)],
        out_specs=cur(SWA_Q_WIDTH),
        out_shape=jax.ShapeDtypeStruct((T, SWA_Q_WIDTH), BF16),
        compiler_params=_params(("parallel",)),
    )(sinks, qa, ka, ka, va, va, cos, cos, sin, sin)


def _swa_bwd(qa, ka, va, cos, sin, sinks, dya):
    T = qa.shape[0]
    blk, cur, prev = _swa_specs(T)
    full = lambda n: pl.BlockSpec((T, n), lambda i: (0, 0))

    def body(sink_ref, q_ref, kp_ref, kc_ref, vp_ref, vc_ref, cp_ref, cc_ref, sp_ref, sc_ref, do_ref,
             dq_ref, dk_out, dv_out, dsink_ref, dk_ref, dv_ref):
        i = pl.program_id(0)

        @pl.when(i == 0)
        def _():
            dk_ref[...] = jnp.zeros_like(dk_ref)
            dv_ref[...] = jnp.zeros_like(dv_ref)
            dsink_ref[...] = jnp.zeros_like(dsink_ref)

        cp, cc, sp, sc = cp_ref[...], cc_ref[...], sp_ref[...], sc_ref[...]
        q_stack, kwin, vwin = _swa_inputs(q_ref, kp_ref, kc_ref, vp_ref, vc_ref, cp_ref, cc_ref, sp_ref, sc_ref)
        probs, psink = _swa_probs(i, q_stack, kwin, sink_ref, blk)
        do_stack = _swa_stack([do_ref[:, pp * LANES:(pp + 1) * LANES] for pp in range(SWA_Q_HEADS // 2)])
        dp = _dot_nt(do_stack, vwin)
        dsum = jnp.sum(probs * dp, axis=1, keepdims=True)
        ds = (probs * (dp - dsum) * ATTN_SCALE).astype(BF16)
        for pp, tile in enumerate(_swa_unstack(_dot(ds, kwin), blk)):
            dq_ref[:, pp * LANES:(pp + 1) * LANES] = _rope_transpose(tile, cc, sc).astype(dq_ref.dtype)
        dkw = _dot_tn(ds, q_stack)
        dvw = _dot_tn(probs.astype(BF16), do_stack)
        lane1 = _lane_iota((1, LANES))
        sink_share = psink * dsum
        dsink = jnp.zeros((1, LANES), F32)
        for h in range(SWA_Q_HEADS):
            dsink = dsink + jnp.where(lane1 == h, -jnp.sum(sink_share[h * blk:(h + 1) * blk]), 0.0)
        dsink_ref[...] += dsink
        ip = jnp.maximum(i - 1, 0)
        rows_p = pl.ds(pl.multiple_of(ip * blk, blk), blk)
        rows_c = pl.ds(pl.multiple_of(i * blk, blk), blk)
        dk_ref[rows_p, :] += _rope_transpose(dkw[:blk], cp, sp)
        dv_ref[rows_p, :] += dvw[:blk]
        dk_ref[rows_c, :] += _rope_transpose(dkw[blk:], cc, sc)
        dv_ref[rows_c, :] += dvw[blk:]

        @pl.when(i == T // blk - 1)
        def _():
            dk_out[...] = dk_ref[...].astype(BF16)
            dv_out[...] = dv_ref[...].astype(BF16)

    return pl.pallas_call(
        body, name="swa_bwd", grid=(T // blk,),
        in_specs=[pl.BlockSpec(memory_space=pltpu.SMEM), cur(SWA_Q_WIDTH), prev(LANES), cur(LANES), prev(LANES), cur(LANES),
                  prev(LANES), cur(LANES), prev(LANES), cur(LANES), cur(SWA_Q_WIDTH)],
        out_specs=[cur(SWA_Q_WIDTH), full(LANES), full(LANES), pl.BlockSpec((1, LANES), lambda i: (0, 0))],
        out_shape=[jax.ShapeDtypeStruct((T, SWA_Q_WIDTH), BF16), jax.ShapeDtypeStruct((T, LANES), BF16),
                   jax.ShapeDtypeStruct((T, LANES), BF16), jax.ShapeDtypeStruct((1, LANES), F32)],
        scratch_shapes=[pltpu.VMEM((T, LANES), F32)] * 2,
        compiler_params=_params(("arbitrary",)),
    )(sinks, qa, ka, ka, va, va, cos, cos, sin, sin, dya)


class _Exchange:
    FLIPS = [(fx, fy, fc) for fx in (0, 1) for fy in (0, 1) for fc in (0, 1) if (fx, fy, fc) != (0, 0, 0)]

    def __init__(self, arrays, kinds):
        self.arrays, self.kinds, self.n = list(arrays), list(kinds), len(arrays)

    def out_shape(self):
        return [jax.ShapeDtypeStruct(a.shape if k == "scatter" else (N_DEV,) + a.shape, a.dtype) for a, k in zip(self.arrays, self.kinds)]

    def scratch(self):
        return [pltpu.SemaphoreType.DMA((self.n, 7)), pltpu.SemaphoreType.DMA((self.n, 7)), pltpu.SemaphoreType.DMA((self.n,))]

    def bind(self, ins, outs, send_sems, recv_sems, local_sems):
        x, y, c = lax.axis_index("x"), lax.axis_index("y"), lax.axis_index("c")
        me = 4 * x + 2 * y + c
        local, remote = [], []
        for a, kind in enumerate(self.kinds):
            mine = ins[a].at[me] if kind == "scatter" else ins[a]
            local.append(pltpu.make_async_copy(mine, outs[a].at[me], local_sems.at[a]))
            for k, (fx, fy, fc) in enumerate(self.FLIPS):
                peer = (x ^ fx, y ^ fy, c ^ fc)
                peer_slot = 4 * peer[0] + 2 * peer[1] + peer[2]
                src = ins[a].at[peer_slot] if kind == "scatter" else ins[a]
                sems = dict(send_sem=send_sems.at[a, k], recv_sem=recv_sems.at[a, k], device_id=peer, device_id_type=MESH)
                remote.append((pltpu.make_async_remote_copy(src_ref=src, dst_ref=outs[a].at[me], **sems),
                               pltpu.make_async_remote_copy(src_ref=src, dst_ref=outs[a].at[peer_slot], **sems)))

        def start():
            for cp in local:
                cp.start()
            for send, _ in remote:
                send.start()

        def wait():
            for send, arrival in remote:
                arrival.wait_recv()
                send.wait_send()
            for cp in local:
                cp.wait()

        return start, wait


def _hosted_call(body, name, grid, exchange, *, in_specs, out_specs, out_shape, semantics, args, scratch=(), prefetch=()):
    n = 0 if exchange is None else exchange.n
    n_pre, n_in, n_out, n_scratch = len(prefetch), len(in_specs), len(out_specs), len(scratch)

    def hosted(*refs):
        pre, rest = refs[:n_pre], refs[n_pre:]
        ins, rest = rest[:n_in], rest[n_in:]
        ex_ins, rest = rest[:n], rest[n:]
        outs, rest = rest[:n_out], rest[n_out:]
        ex_outs, rest = rest[:n], rest[n:]
        own, sems = rest[:n_scratch], rest[n_scratch:]
        if exchange is None:
            return body(*pre, *ins, *outs, *own)
        start, wait = exchange.bind(ex_ins, ex_outs, *sems)
        ids = [pl.program_id(d) for d in range(len(grid))]
        first = functools.reduce(jnp.logical_and, [i == 0 for i in ids])
        last = functools.reduce(jnp.logical_and, [i == g - 1 for i, g in zip(ids, grid)])
        pl.when(first)(start)
        body(*pre, *ins, *outs, *own)
        pl.when(last)(wait)

    grid_spec = pltpu.PrefetchScalarGridSpec(
        num_scalar_prefetch=n_pre, grid=grid, in_specs=list(in_specs) + [ANY] * n, out_specs=list(out_specs) + [ANY] * n,
        scratch_shapes=list(scratch) + ([] if exchange is None else exchange.scratch()))
    res = pl.pallas_call(
        hosted, name=name, grid_spec=grid_spec, out_shape=list(out_shape) + ([] if exchange is None else exchange.out_shape()),
        compiler_params=_params(semantics if exchange is None else ("arbitrary",) * len(grid)),
    )(*prefetch, *args, *([] if exchange is None else exchange.arrays))
    return res[:n_out], res[n_out:]


SOFTPLUS_LINEAR_FROM = 30.0


def _sb_scores(qm, k, valid):
    z = _dot_nt(qm, k)
    sp = jnp.where(z > SOFTPLUS_LINEAR_FROM, z, jnp.log(1.0 + jnp.exp(z)))
    log_beta = z - sp
    if valid is not None:
        sp = jnp.where(valid, sp, 0.0)
    return sp, log_beta


def _tri2(B, cmp):
    r = lax.broadcasted_iota(jnp.int32, (2 * B, B), 0) % B
    c = lax.broadcasted_iota(jnp.int32, (2 * B, B), 1)
    return cmp(r, c).astype(BF16)


def _tri_sum(v, tri2):
    hi, lo = _split_bf16(v)
    return _dot(jnp.concatenate([hi, lo], axis=1), tri2)


def _head_masks(x):
    low = _lane_iota(x.shape) < HEAD_DIM
    zero = jnp.zeros((), x.dtype)
    return jnp.where(low, x, zero), jnp.where(low, zero, x)


def _strictly_below(B):
    r = lax.broadcasted_iota(jnp.int32, (B, B), 0)
    c = lax.broadcasted_iota(jnp.int32, (B, B), 1)
    return c < r


def _sb_grid(T, descending):
    B = min(SB_BLOCK, T)
    n = T // B
    pairs = [(i, j) for i in range(n) for j in (range(i, -1, -1) if descending else range(i + 1))]
    return B, jnp.asarray([p[0] for p in pairs], jnp.int32), jnp.asarray([p[1] for p in pairs], jnp.int32)


N_PAIRS = SB_HEADS // 2
PAIR_COLS = [slice(p * LANES, (p + 1) * LANES) for p in range(N_PAIRS)]


def _sb_fwd(qb, kb, vb, exchange=None):
    T = qb.shape[0]
    B, i_tab, j_tab = _sb_grid(T, descending=True)
    n = T // B

    def body(i_ref, j_ref, q_ref, k_ref, v_ref, o_ref, a_ref, b_ref, acc_ref, c_ref, tri_ref):
        s = pl.program_id(0)
        i, j = i_ref[s], j_ref[s]

        @pl.when(s == 0)
        def _():
            tri_ref[...] = _tri2(B, lambda r, c: r > c)

        @pl.when(j == i)
        def _():
            acc_ref[...] = jnp.zeros_like(acc_ref)
            c_ref[...] = jnp.zeros_like(c_ref)

        def block(valid):
            for p, cols in enumerate(PAIR_COLS):
                qms = _head_masks(q_ref[:, cols] * ATTN_SCALE)
                k = k_ref[:, cols]
                probs = []
                for hh in range(2):
                    h = 2 * p + hh
                    sp, lb = _sb_scores(qms[hh], k, valid)
                    c = c_ref[h]
                    a = jnp.exp(lb - (c + _tri_sum(sp, tri_ref[...])))
                    beta = jnp.exp(lb)
                    if valid is not None:
                        a = jnp.where(valid, a, 0.0)
                        beta = jnp.where(valid, beta, 0.0)
                    probs.append(a.astype(BF16))
                    a_ref[h] = probs[-1]
                    b_ref[h] = beta.astype(BF16)
                    c_ref[h] = c + jnp.sum(sp, axis=1, keepdims=True)
                acc_ref[:, cols] += _dot(jnp.concatenate(probs, axis=1), jnp.concatenate(_head_masks(v_ref[:, cols]), axis=0))

        pl.when(j == i)(lambda: block(_strictly_below(B)))
        pl.when(j != i)(lambda: block(None))

        @pl.when(j == 0)
        def _():
            o_ref[...] = acc_ref[...].astype(o_ref.dtype)

    q_spec = pl.BlockSpec((B, SB_WIDTH), lambda s, i_ref, j_ref: (i_ref[s], 0))
    k_spec = pl.BlockSpec((B, SB_WIDTH), lambda s, i_ref, j_ref: (j_ref[s], 0))
    tile = pl.BlockSpec((None, None, SB_HEADS, B, B), lambda s, i_ref, j_ref: (i_ref[s], j_ref[s], 0, 0, 0))
    saved = jax.ShapeDtypeStruct((n, n, SB_HEADS, B, B), BF16)
    return _hosted_call(
        body, "sb_fwd", (int(i_tab.shape[0]),), exchange, prefetch=(i_tab, j_tab),
        in_specs=[q_spec, k_spec, k_spec], out_specs=[q_spec, tile, tile],
        out_shape=[jax.ShapeDtypeStruct((T, SB_WIDTH), BF16), saved, saved],
        scratch=[pltpu.VMEM((B, SB_WIDTH), F32), pltpu.VMEM((SB_HEADS, B, 1), F32), pltpu.VMEM((2 * B, B), BF16)],
        semantics=("arbitrary",), args=(qb, kb, vb))


def _sb_bwd(qb, kb, vb, probs, betas, dyb, exchange=None):
    T = qb.shape[0]
    B, i_tab, j_tab = _sb_grid(T, descending=False)
    n_steps = int(i_tab.shape[0])

    def block_diag_t(x):
        xt = x.T
        top = lax.broadcasted_iota(jnp.int32, xt.shape, 0) < HEAD_DIM
        zero = jnp.zeros((), x.dtype)
        return jnp.concatenate([jnp.where(top, xt, zero), jnp.where(top, zero, xt)], axis=1)

    def body(i_ref, j_ref, q_ref, k_ref, v_ref, a_ref, b_ref, do_ref, dq_ref, dk_out, dv_out,
             dq_acc, cg_ref, dkt_ref, dvt_ref, tri_ref, qt_ref, dot_ref):
        s = pl.program_id(0)
        i, j = i_ref[s], j_ref[s]

        @pl.when(s == 0)
        def _():
            dkt_ref[...] = jnp.zeros_like(dkt_ref)
            dvt_ref[...] = jnp.zeros_like(dvt_ref)
            tri_ref[...] = _tri2(B, lambda r, c: r < c)[:B]

        @pl.when(j == 0)
        def _():
            dq_acc[...] = jnp.zeros_like(dq_acc)
            cg_ref[...] = jnp.zeros_like(cg_ref)
            for p, cols in enumerate(PAIR_COLS):
                qt_ref[p] = block_diag_t(q_ref[:, cols] * ATTN_SCALE)
                dot_ref[p] = block_diag_t(do_ref[:, cols])

        for p, cols in enumerate(PAIR_COLS):
            doms = _head_masks(do_ref[:, cols])
            k, v = k_ref[:, cols], v_ref[:, cols]
            dzs = []
            for hh in range(2):
                h = 2 * p + hh
                g = a_ref[h].astype(F32) * _dot_nt(doms[hh], v)
                cg = cg_ref[h]
                gsum = g + (cg + _dot(g.astype(BF16), tri_ref[...]))
                dzs.append((g - b_ref[h].astype(F32) * gsum).astype(BF16))
                cg_ref[h] = cg + jnp.sum(g, axis=1, keepdims=True)
            dq_acc[:, cols] += _dot(jnp.concatenate(dzs, axis=1), jnp.concatenate(_head_masks(k), axis=0))
            dkt_ref[j, cols, :] += _dot(qt_ref[p], jnp.concatenate(dzs, axis=0))
            dvt_ref[j, cols, :] += _dot(dot_ref[p], jnp.concatenate([a_ref[2 * p], a_ref[2 * p + 1]], axis=0))

        @pl.when(j == i)
        def _():
            dq_ref[...] = (dq_acc[...] * ATTN_SCALE).astype(dq_ref.dtype)

        @pl.when(s == n_steps - 1)
        def _():
            for jb in range(T // B):
                dk_out[jb * B:(jb + 1) * B, :] = dkt_ref[jb].T.astype(BF16)
                dv_out[jb * B:(jb + 1) * B, :] = dvt_ref[jb].T.astype(BF16)

    q_spec = pl.BlockSpec((B, SB_WIDTH), lambda s, i_ref, j_ref: (i_ref[s], 0))
    k_spec = pl.BlockSpec((B, SB_WIDTH), lambda s, i_ref, j_ref: (j_ref[s], 0))
    tile = pl.BlockSpec((None, None, SB_HEADS, B, B), lambda s, i_ref, j_ref: (i_ref[s], j_ref[s], 0, 0, 0))
    full = pl.BlockSpec((T, SB_WIDTH), lambda s, i_ref, j_ref: (0, 0))
    return _hosted_call(
        body, "sb_bwd", (n_steps,), exchange, prefetch=(i_tab, j_tab),
        in_specs=[q_spec, k_spec, k_spec, tile, tile, q_spec], out_specs=[q_spec, full, full],
        out_shape=[jax.ShapeDtypeStruct((T, SB_WIDTH), BF16)] * 3,
        scratch=[pltpu.VMEM((B, SB_WIDTH), F32), pltpu.VMEM((SB_HEADS, B, 1), F32), pltpu.VMEM((T // B, SB_WIDTH, B), F32),
                 pltpu.VMEM((T // B, SB_WIDTH, B), F32), pltpu.VMEM((B, B), BF16), pltpu.VMEM((N_PAIRS, LANES, 2 * B), BF16),
                 pltpu.VMEM((N_PAIRS, LANES, 2 * B), BF16)],
        semantics=("arbitrary",), args=(qb, kb, vb, probs, betas, dyb))


def _ln_stats(u):
    mu = jnp.mean(u, axis=-1, keepdims=True)
    xc = u - mu
    var = jnp.mean(xc * xc, axis=-1, keepdims=True)
    rstd = lax.rsqrt(var + LN_EPS)
    return xc * rstd, rstd


def _ln_bwd(dy, xhat, rstd, g):
    dxh = dy * g
    return rstd * (dxh - jnp.mean(dxh, axis=-1, keepdims=True) - xhat * jnp.mean(dxh * xhat, axis=-1, keepdims=True))


def _gates(gl_ref, bg_ref):
    ga = jax.nn.sigmoid(gl_ref[:, :D_MODEL] + bg_ref[:, :D_MODEL])
    gb = jax.nn.sigmoid(gl_ref[:, D_MODEL:] + bg_ref[:, D_MODEL:])
    return ga, gb


def _mix_fwd(ya, yb, gl, x, wa, wb, wo, b_gate, ln1_g, ln1_b):
    T = x.shape[0]
    tm = min(256, T)

    def body(ya_ref, yb_ref, gl_ref, x_ref, wa_ref, wb_ref, wo_ref, bg_ref, g_ref, b_ref, h_ref, u_ref, x1_ref):
        ga, gb = _gates(gl_ref, bg_ref)
        h = (ga * _dot(ya_ref[...], wa_ref[...]) + gb * _dot(yb_ref[...], wb_ref[...])).astype(BF16)
        h_ref[...] = h
        u = ALPHA * x_ref[...] + _dot(h, wo_ref[...])
        u_ref[...] = u
        xhat, _ = _ln_stats(u)
        x1_ref[...] = (xhat * g_ref[...] + b_ref[...]).astype(BF16)

    row = lambda n: pl.BlockSpec((tm, n), lambda i: (i, 0))
    const = lambda r, n: pl.BlockSpec((r, n), lambda i: (0, 0))
    return pl.pallas_call(
        body, name="mix_fwd", grid=(T // tm,),
        in_specs=[row(SWA_Q_WIDTH), row(SB_WIDTH), row(GATE_WIDTH), row(D_MODEL), const(SWA_Q_WIDTH, D_MODEL), const(SB_WIDTH, D_MODEL),
                  const(D_MODEL, D_MODEL), const(1, GATE_WIDTH), const(1, D_MODEL), const(1, D_MODEL)],
        out_specs=[row(D_MODEL)] * 3,
        out_shape=[jax.ShapeDtypeStruct((T, D_MODEL), BF16), jax.ShapeDtypeStruct((T, D_MODEL), F32), jax.ShapeDtypeStruct((T, D_MODEL), BF16)],
        compiler_params=_params(("parallel",)),
    )(ya, yb, gl, x, wa, wb, wo, b_gate, ln1_g, ln1_b)


def _mix_bwd(du1, ya, yb, gl, wa, wb, wo, b_gate):
    T = du1.shape[0]
    tm = min(256, T)

    def body(du_ref, ya_ref, yb_ref, gl_ref, wa_ref, wb_ref, wo_ref, bg_ref, dya_ref, dyb_ref, dgl_ref, dta_ref, dtb_ref, dbg_ref):
        @pl.when(pl.program_id(0) == 0)
        def _():
            dbg_ref[...] = jnp.zeros_like(dbg_ref)

        dh = _dot_nt(du_ref[...].astype(BF16), wo_ref[...])
        ga, gb = _gates(gl_ref, bg_ref)
        for gate, y_ref, w_ref, dy_ref, dt_ref, lo in ((ga, ya_ref, wa_ref, dya_ref, dta_ref, 0), (gb, yb_ref, wb_ref, dyb_ref, dtb_ref, D_MODEL)):
            t = _dot(y_ref[...], w_ref[...])
            dlogit = dh * t * gate * (1.0 - gate)
            dgl_ref[:, lo:lo + D_MODEL] = dlogit.astype(BF16)
            dbg_ref[:, lo:lo + D_MODEL] += jnp.sum(dlogit, axis=0, keepdims=True)
            dt = (dh * gate).astype(BF16)
            dt_ref[...] = dt
            dy_ref[...] = _dot_nt(dt, w_ref[...]).astype(BF16)

    row = lambda n: pl.BlockSpec((tm, n), lambda i: (i, 0))
    const = lambda r, n: pl.BlockSpec((r, n), lambda i: (0, 0))
    sds = lambda n, dt: jax.ShapeDtypeStruct((T, n), dt)
    return pl.pallas_call(
        body, name="mix_bwd", grid=(T // tm,),
        in_specs=[row(D_MODEL), row(SWA_Q_WIDTH), row(SB_WIDTH), row(GATE_WIDTH), const(SWA_Q_WIDTH, D_MODEL), const(SB_WIDTH, D_MODEL),
                  const(D_MODEL, D_MODEL), const(1, GATE_WIDTH)],
        out_specs=[row(SWA_Q_WIDTH), row(SB_WIDTH), row(GATE_WIDTH), row(D_MODEL), row(D_MODEL), const(1, GATE_WIDTH)],
        out_shape=[sds(SWA_Q_WIDTH, BF16), sds(SB_WIDTH, BF16), sds(GATE_WIDTH, BF16), sds(D_MODEL, BF16), sds(D_MODEL, BF16),
                   jax.ShapeDtypeStruct((1, GATE_WIDTH), F32)],
        compiler_params=_params(("arbitrary",)),
    )(du1, ya, yb, gl, wa, wb, wo, b_gate)


CONV_COLS = LANES


CONV_CHUNK = 64
CONV_CHUNK_FWD = 256
HALO = 8


def _taps(ref, r0, rows, lead):
    return [ref[pl.ds(r0 + lead + k, rows), :] for k in ((-2, -1, 0) if lead else (0, 1, 2))]


def _chunks(T, rows, step, init=None):
    def body(c, carry):
        out = step(pl.multiple_of(c * rows, rows), *(() if init is None else (carry,)))
        return carry if init is None else out
    return lax.fori_loop(0, T // rows, body, 0 if init is None else init)


def _conv_chunk(taps, w_ref, b_ref):
    return w_ref[0:1, :] * taps[0] + w_ref[1:2, :] * taps[1] + w_ref[2:3, :] * taps[2] + b_ref[...]


def _fold(x):
    return jnp.sum(x.reshape(x.shape[0] // 8, 8, x.shape[1]), axis=0)


def _conv_specs(T):
    nb = D_FF // CONV_COLS
    pair = pl.BlockSpec((2, T, CONV_COLS), lambda j: (0, 0, j))
    gate = lambda r: pl.BlockSpec((r, CONV_COLS), lambda j: (0, j))
    up = lambda r: pl.BlockSpec((r, CONV_COLS), lambda j: (0, j + nb))
    return nb, pair, gate, up


def _conv_glu_fwd(p3, conv_w, conv_b):
    T = p3.shape[1]
    nb, pair, gate, up = _conv_specs(T)

    R = min(CONV_CHUNK_FWD, T)

    def body(p_ref, wg_ref, wu_ref, bg_ref, bu_ref, s_ref, pg_s, pu_s):
        for half, scr in enumerate((pg_s, pu_s)):
            scr[0:HALO, :] = jnp.zeros((HALO, CONV_COLS), F32)
            scr[HALO:HALO + T, :] = p_ref[half].astype(F32)
        def step(r0):
            ag = _conv_chunk(_taps(pg_s, r0, R, HALO), wg_ref, bg_ref)
            au = _conv_chunk(_taps(pu_s, r0, R, HALO), wu_ref, bu_ref)
            s_ref[pl.ds(r0, R), :] = (ag * jax.nn.sigmoid(ag) * au).astype(BF16)

        _chunks(T, R, step)

    return pl.pallas_call(
        body, name="conv_glu_fwd", grid=(nb,),
        in_specs=[pair, gate(3), up(3), gate(1), up(1)],
        out_specs=pl.BlockSpec((T, CONV_COLS), lambda j: (0, j)),
        out_shape=jax.ShapeDtypeStruct((T, D_FF), BF16),
        scratch_shapes=[pltpu.VMEM((T + HALO, CONV_COLS), F32)] * 2,
        compiler_params=_params(("parallel",)),
    )(p3, conv_w, conv_w, conv_b, conv_b)


def _conv_glu_bwd(p3, ds, conv_w, conv_b):
    T = p3.shape[1]
    nb, pair, gate, up = _conv_specs(T)

    R = min(CONV_CHUNK, T)

    def body(p_ref, ds_ref, wg_ref, wu_ref, bg_ref, bu_ref, dp_ref, dwg_ref, dwu_ref, dbg_ref, dbu_ref, pg_s, pu_s, dag_s, dau_s):
        for half, scr in enumerate((pg_s, pu_s)):
            scr[0:HALO, :] = jnp.zeros((HALO, CONV_COLS), F32)
            scr[HALO:HALO + T, :] = p_ref[half].astype(F32)
        for scr in (dag_s, dau_s):
            scr[T:T + HALO, :] = jnp.zeros((HALO, CONV_COLS), F32)
        halves = ((pg_s, dag_s, wg_ref, dwg_ref, dbg_ref), (pu_s, dau_s, wu_ref, dwu_ref, dbu_ref))

        def step(r0, sums):
            taps = [_taps(p_s, r0, R, HALO) for p_s, *_ in halves]
            ag = _conv_chunk(taps[0], wg_ref, bg_ref)
            au = _conv_chunk(taps[1], wu_ref, bu_ref)
            sg = jax.nn.sigmoid(ag)
            d = ds_ref[pl.ds(r0, R), :].astype(F32)
            das = (d * au * (sg * (1.0 + ag * (1.0 - sg))), d * ag * sg)
            out = []
            for half, (_, da_s, *_) in enumerate(halves):
                da_s[pl.ds(r0, R), :] = das[half]
                out.append(tuple(sums[half][k] + _fold(das[half] * taps[half][k]) for k in range(3)) + (sums[half][3] + _fold(das[half]),))
            return tuple(out)

        sums = _chunks(T, R, step, ((jnp.zeros((8, CONV_COLS), F32),) * 4,) * 2)
        for half, (_, da_s, w_ref, dw_ref, db_ref) in enumerate(halves):
            for k in range(3):
                dw_ref[k:k + 1, :] = jnp.sum(sums[half][k], axis=0, keepdims=True)
            db_ref[...] = jnp.sum(sums[half][3], axis=0, keepdims=True)

            def transposed(r0, da_s=da_s, w_ref=w_ref, half=half):
                da0, da1, da2 = _taps(da_s, r0, R, 0)
                dp_ref[half, pl.ds(r0, R), :] = (w_ref[2:3, :] * da0 + w_ref[1:2, :] * da1 + w_ref[0:1, :] * da2).astype(BF16)

            _chunks(T, R, transposed)

    col = lambda r: pl.BlockSpec((r, CONV_COLS), lambda j: (0, j))
    return pl.pallas_call(
        body, name="conv_glu_bwd", grid=(nb,),
        in_specs=[pair, col(T), gate(3), up(3), gate(1), up(1)],
        out_specs=[pair, col(3), col(3), col(1), col(1)],
        out_shape=[jax.ShapeDtypeStruct((2, T, D_FF), BF16), jax.ShapeDtypeStruct((3, D_FF), F32), jax.ShapeDtypeStruct((3, D_FF), F32),
                   jax.ShapeDtypeStruct((1, D_FF), F32), jax.ShapeDtypeStruct((1, D_FF), F32)],
        scratch_shapes=[pltpu.VMEM((T + HALO, CONV_COLS), F32)] * 4,
        compiler_params=_params(("parallel",)),
    )(p3, ds, conv_w, conv_w, conv_b, conv_b)


def _ffn_down_loss(s, w_down, u1, ln1_g, ln1_b, ln2_g, ln2_b, target):
    T = u1.shape[0]
    tm = min(256, T)

    def body(s_ref, w_ref, u1_ref, g1_ref, b1_ref, g2_ref, b2_ref, t_ref, du_ref, dub_ref, dg_ref, db_ref, loss_ref):
        @pl.when(pl.program_id(0) == 0)
        def _():
            dg_ref[...] = jnp.zeros_like(dg_ref)
            db_ref[...] = jnp.zeros_like(db_ref)
            loss_ref[...] = jnp.zeros_like(loss_ref)

        xh1, _ = _ln_stats(u1_ref[...])
        x1 = xh1 * g1_ref[...] + b1_ref[...]
        u2 = ALPHA * x1 + _dot(s_ref[...], w_ref[...])
        xh2, rstd2 = _ln_stats(u2)
        err = xh2 * g2_ref[...] + b2_ref[...] - t_ref[...]
        per_token = jnp.mean(err * err, axis=-1, keepdims=True)
        loss_ref[...] += 0.5 * jnp.sum(per_token, axis=0, keepdims=True)
        dy = err * (1.0 / D_MODEL)
        dg_ref[...] += jnp.sum(dy * xh2, axis=0, keepdims=True)
        db_ref[...] += jnp.sum(dy, axis=0, keepdims=True)
        du2 = _ln_bwd(dy, xh2, rstd2, g2_ref[...])
        du_ref[...] = du2
        dub_ref[...] = du2.astype(BF16)

    row = lambda n: pl.BlockSpec((tm, n), lambda i: (i, 0))
    const = lambda r, n: pl.BlockSpec((r, n), lambda i: (0, 0))
    vec = const(1, D_MODEL)
    return pl.pallas_call(
        body, name="ffn_down_loss", grid=(T // tm,),
        in_specs=[row(D_FF), const(D_FF, D_MODEL), row(D_MODEL), vec, vec, vec, vec, row(D_MODEL)],
        out_specs=[row(D_MODEL), row(D_MODEL), vec, vec, const(1, LANES)],
        out_shape=[jax.ShapeDtypeStruct((T, D_MODEL), F32), jax.ShapeDtypeStruct((T, D_MODEL), BF16), jax.ShapeDtypeStruct((1, D_MODEL), F32),
                   jax.ShapeDtypeStruct((1, D_MODEL), F32), jax.ShapeDtypeStruct((1, LANES), F32)],
        compiler_params=_params(("arbitrary",)),
    )(s, w_down, u1, ln1_g, ln1_b, ln2_g, ln2_b, target)


def _ffn_up_bwd_ln1(dp3, w_up, du2, u1, ln1_g):
    T = u1.shape[0]
    tm = min(256, T)

    def body(dp_ref, w_ref, du2_ref, u1_ref, g_ref, du_ref, dub_ref, dg_ref, db_ref):
        @pl.when(pl.program_id(0) == 0)
        def _():
            dg_ref[...] = jnp.zeros_like(dg_ref)
            db_ref[...] = jnp.zeros_like(db_ref)

        dx1 = _dot_nt(dp_ref[0], w_ref[:, :D_FF]) + _dot_nt(dp_ref[1], w_ref[:, D_FF:]) + ALPHA * du2_ref[...]
        xh, rstd = _ln_stats(u1_ref[...])
        dg_ref[...] += jnp.sum(dx1 * xh, axis=0, keepdims=True)
        db_ref[...] += jnp.sum(dx1, axis=0, keepdims=True)
        du1 = _ln_bwd(dx1, xh, rstd, g_ref[...])
        du_ref[...] = du1
        dub_ref[...] = du1.astype(BF16)

    row = lambda n: pl.BlockSpec((tm, n), lambda i: (i, 0))
    const = lambda r, n: pl.BlockSpec((r, n), lambda i: (0, 0))
    vec = const(1, D_MODEL)
    return pl.pallas_call(
        body, name="ffn_up_bwd_ln1", grid=(T // tm,),
        in_specs=[pl.BlockSpec((2, tm, D_FF), lambda i: (0, i, 0)), const(D_MODEL, 2 * D_FF), row(D_MODEL), row(D_MODEL), vec],
        out_specs=[row(D_MODEL), row(D_MODEL), vec, vec],
        out_shape=[jax.ShapeDtypeStruct((T, D_MODEL), F32), jax.ShapeDtypeStruct((T, D_MODEL), BF16), jax.ShapeDtypeStruct((1, D_MODEL), F32),
                   jax.ShapeDtypeStruct((1, D_MODEL), F32)],
        compiler_params=_params(("arbitrary",)),
    )(dp3, w_up, du2, u1, ln1_g)


def _local_step(x, positions, w_in, b_gate, sinks, ln1_g, ln1_b, conv_b, ln2_g, ln2_b, target, later_weights,
                early_exchange=None, tail_exchange=None):
    T = x.shape[0]
    inv_freq = 1.0 / (ROPE_THETA ** (jnp.arange(0, HEAD_DIM, 2, dtype=F32) / HEAD_DIM))
    cos, sin = _rope_tables(positions.reshape(T, 1), jnp.tile(inv_freq, LANES // (HEAD_DIM // 2)).reshape(1, LANES))

    xb, qa, ka, va, qb, kb, vb, gl = _in_proj(x, w_in)
    ya = _swa_fwd(qa, ka, va, cos, sin, sinks)
    if isinstance(later_weights, tuple):
        exchange, finish = later_weights
        (yb, probs, betas), arrived = _sb_fwd(qb, kb, vb, exchange)
        later_weights = finish(arrived)
    else:
        (yb, probs, betas), _ = _sb_fwd(qb, kb, vb)
    wa, wb, wo, w_up, conv_w, w_down = later_weights
    h, u1, x1 = _mix_fwd(ya, yb, gl, x, wa, wb, wo, b_gate, ln1_g, ln1_b)

    ff_tn = D_FF // 2
    nff = D_FF // ff_tn
    tm = min(512, T)
    p3 = _matmul(x1, w_up, kind="nn", name="ffn_up", grid=(T // tm, 2 * nff),
                 a_spec=pl.BlockSpec((tm, D_MODEL), lambda i, j: (i, 0)), b_spec=pl.BlockSpec((D_MODEL, ff_tn), lambda i, j: (0, j)),
                 out_spec=pl.BlockSpec((None, tm, ff_tn), lambda i, j: (j // nff, i, j % nff)),
                 out_shape=jax.ShapeDtypeStruct((2, T, D_FF), ACT_DTYPE))
    s = _conv_glu_fwd(p3, conv_w, conv_b)
    du2, du2b, dln2_g, dln2_b, loss = _ffn_down_loss(s, w_down, u1, ln1_g, ln1_b, ln2_g, ln2_b, target)

    ds = _matmul(du2b, w_down, kind="nt", name="ffn_down_bwd", grid=(T // tm, nff),
                 a_spec=pl.BlockSpec((tm, D_MODEL), lambda i, j: (i, 0)), b_spec=pl.BlockSpec((ff_tn, D_MODEL), lambda i, j: (j, 0)),
                 out_spec=pl.BlockSpec((tm, ff_tn), lambda i, j: (i, j)), out_shape=jax.ShapeDtypeStruct((T, D_FF), ACT_DTYPE))
    dp3, dcw_g, dcw_u, dcb_g, dcb_u = _conv_glu_bwd(p3, ds, conv_w, conv_b)
    tk = 256
    dw_down = _matmul(s, du2b, kind="tn", name="dw_down", grid=(D_FF // tk,),
                      a_spec=pl.BlockSpec((T, tk), lambda i: (0, i)), b_spec=pl.BlockSpec((T, D_MODEL), lambda i: (0, 0)),
                      out_spec=pl.BlockSpec((tk, D_MODEL), lambda i: (i, 0)), out_shape=jax.ShapeDtypeStruct((D_FF, D_MODEL), BF16))
    dw_up = _matmul(x1, dp3, kind="tn", name="dw_up", grid=(D_MODEL // 512, 2 * nff),
                    a_spec=pl.BlockSpec((T, 512), lambda i, j: (0, i)), b_spec=pl.BlockSpec((None, T, ff_tn), lambda i, j: (j // nff, 0, j % nff)),
                    out_spec=pl.BlockSpec((512, ff_tn), lambda i, j: (i, j)), out_shape=jax.ShapeDtypeStruct((D_MODEL, 2 * D_FF), BF16))
    du1, du1b, dln1_g, dln1_b = _ffn_up_bwd_ln1(dp3, w_up, du2, u1, ln1_g)
    dya, dyb, dgl, dta, dtb, db_gate = _mix_bwd(du1, ya, yb, gl, wa, wb, wo, b_gate)

    def dw_tn(a, g, name):
        rows, cols = a.shape[1], g.shape[1]
        tn = min(512, cols)
        return _matmul(a, g, kind="tn", name=name, grid=(rows // 512, cols // tn),
                       a_spec=pl.BlockSpec((T, 512), lambda i, j: (0, i)), b_spec=pl.BlockSpec((T, tn), lambda i, j: (0, j)),
                       out_spec=pl.BlockSpec((512, tn), lambda i, j: (i, j)), out_shape=jax.ShapeDtypeStruct((rows, cols), BF16))

    dwa = dw_tn(ya, dta, "dw_branch_a")
    dwb = dw_tn(yb, dtb, "dw_branch_b")
    dwo = dw_tn(h, du1b, "dw_out")

    grads = dict(
        b_gate=db_gate, w_branch_a=dwa, w_branch_b=dwb, w_out=dwo, ln1_g=dln1_g, ln1_b=dln1_b,
        w_up=dw_up, conv_w=jnp.concatenate([dcw_g, dcw_u], axis=1), conv_b=(dcb_g, dcb_u), w_down=dw_down, ln2_g=dln2_g, ln2_b=dln2_b)
    (dqb, dkb, dvb), early_out = _sb_bwd(qb, kb, vb, probs, betas, dyb, early_exchange(grads) if early_exchange else None)
    dqa, dka, dva, grads["sinks"] = _swa_bwd(qa, ka, va, cos, sin, sinks, dya)
    dproj = (dqa, dka, dva, dqb, dkb, dvb, dgl)
    grads["w_in"] = _dw_in(xb, dproj)
    grad_x, tail_out = _grad_x(dproj, w_in, du1, tail_exchange(grads, loss) if tail_exchange else None)
    return loss, grad_x, grads, early_out, tail_out


def _dw_in(xb, dproj):
    T = xb.shape[0]
    tn = 2 * LANES
    groups, start, k = [], 0, 0
    while k < len(IN_WIDTHS):
        if IN_WIDTHS[k] >= tn:
            groups.append((start, IN_WIDTHS[k] // tn, [(k, 0, tn)]))
            k += 1
        else:
            members, off = [], 0
            while off < tn:
                members.append((k, off, IN_WIDTHS[k]))
                off += IN_WIDTHS[k]
                k += 1
            groups.append((start, 1, members))
        start += groups[-1][1]

    def body(x_ref, *refs):
        pieces, o_ref = refs[:-1], refs[-1]
        j = pl.program_id(0)
        for first, steps, members in groups:
            @pl.when((j >= first) & (j < first + steps))
            def _(members=members):
                for k, off, width in members:
                    o_ref[:, off:off + width] = _dot_tn(x_ref[...], pieces[k][...]).astype(o_ref.dtype)

    specs = [None] * len(IN_WIDTHS)
    for first, steps, members in groups:
        for k, _, width in members:
            specs[k] = pl.BlockSpec((T, width), lambda j, first=first, steps=steps: (0, jnp.clip(j - first, 0, steps - 1)))
    return pl.pallas_call(
        body, name="dw_in", grid=(IN_TOTAL // tn,),
        in_specs=[pl.BlockSpec((T, D_MODEL), lambda j: (0, 0))] + specs, out_specs=pl.BlockSpec((D_MODEL, tn), lambda j: (0, j)),
        out_shape=jax.ShapeDtypeStruct((D_MODEL, IN_TOTAL), BF16), compiler_params=_params(("arbitrary",)),
    )(xb, *dproj)


def _grad_x(dproj, w_in, du1, exchange=None):
    T = du1.shape[0]
    tm = min(256, T)
    offs = np.cumsum((0,) + IN_WIDTHS)

    def body(*refs):
        pieces, (w_ref, du_ref, o_ref) = refs[:len(IN_WIDTHS)], refs[len(IN_WIDTHS):]
        acc = ALPHA * du_ref[...]
        for p_ref, a, b in zip(pieces, offs[:-1], offs[1:]):
            acc = acc + _dot_nt(p_ref[...].astype(BF16), w_ref[:, a:b])
        o_ref[...] = acc

    row = lambda n: pl.BlockSpec((tm, n), lambda i: (i, 0))
    (grad_x,), arrived = _hosted_call(
        body, "grad_x", (T // tm,), exchange,
        in_specs=[row(n) for n in IN_WIDTHS] + [pl.BlockSpec((D_MODEL, IN_TOTAL), lambda i: (0, 0)), row(D_MODEL)],
        out_specs=[row(D_MODEL)], out_shape=[jax.ShapeDtypeStruct((T, D_MODEL), F32)], semantics=("parallel",),
        args=(*dproj, w_in, du1))
    return grad_x, arrived


ANY = pl.BlockSpec(memory_space=pl.ANY)


def _all_gather(slabs, name):
    n = len(slabs)

    def body(*refs):
        ins, outs = refs[:n], refs[n:2 * n]
        send_sems, recv_sems, local_sems = refs[2 * n:]
        x, y, c = lax.axis_index("x"), lax.axis_index("y"), lax.axis_index("c")
        me, sibling = (x, y, c), (x, y, 1 - c)
        chips = [(1 - x, y), (x, 1 - y), (1 - x, 1 - y)]

        def slot(pos):
            return 4 * pos[0] + 2 * pos[1] + pos[2]

        def copy(a, k, block, to, from_input=False):
            return pltpu.make_async_remote_copy(
                src_ref=ins[a] if from_input else outs[a].at[slot(block)], dst_ref=outs[a].at[slot(block)],
                send_sem=send_sems.at[a, k], recv_sem=recv_sems.at[a, k], device_id=to, device_id_type=MESH)

        mine = [pltpu.make_async_copy(ins[a], outs[a].at[slot(me)], local_sems.at[a]) for a in range(n)]
        for cp in mine:
            cp.start()
        first = []
        for a in range(n):
            first.append(copy(a, 0, me, sibling, from_input=True))
            first += [copy(a, 1 + j, me, (*chip, c), from_input=True) for j, chip in enumerate(chips)]
        for cp in first:
            cp.start()
        passed = []
        for j, chip in enumerate(chips):
            for a in range(n):
                copy(a, 1 + j, (*chip, c), me).wait_recv()
                fwd = copy(a, 4 + j, (*chip, c), sibling)
                fwd.start()
                passed.append(fwd)
        for a in range(n):
            copy(a, 0, sibling, me).wait_recv()
            for j, chip in enumerate(chips):
                copy(a, 4 + j, (*chip, 1 - c), me).wait_recv()
        for cp in first + passed:
            cp.wait_send()
        for cp in mine:
            cp.wait()

    return pl.pallas_call(
        body, name=name,
        in_specs=[ANY] * n, out_specs=[ANY] * n,
        out_shape=[jax.ShapeDtypeStruct((N_DEV,) + s.shape, s.dtype) for s in slabs],
        scratch_shapes=[pltpu.SemaphoreType.DMA((n, 7)), pltpu.SemaphoreType.DMA((n, 7)), pltpu.SemaphoreType.DMA((n,))],
    )(*slabs)


def _all_to_all(slabs, name):
    n = len(slabs)

    def body(*refs):
        ins, outs = refs[:n], refs[n:2 * n]
        send_sems, recv_sems, local_sems = refs[2 * n:]
        x, y, c = lax.axis_index("x"), lax.axis_index("y"), lax.axis_index("c")
        my_slot = 4 * x + 2 * y + c
        flips = [(fx, fy, fc) for fx in (0, 1) for fy in (0, 1) for fc in (0, 1) if (fx, fy, fc) != (0, 0, 0)]

        def copy(a, k):
            fx, fy, fc = flips[k]
            peer = (x ^ fx, y ^ fy, c ^ fc)
            peer_slot = 4 * peer[0] + 2 * peer[1] + peer[2]
            send = pltpu.make_async_remote_copy(src_ref=ins[a].at[peer_slot], dst_ref=outs[a].at[my_slot], send_sem=send_sems.at[a, k],
                                                recv_sem=recv_sems.at[a, k], device_id=peer, device_id_type=MESH)
            recv = pltpu.make_async_remote_copy(src_ref=ins[a].at[peer_slot], dst_ref=outs[a].at[peer_slot], send_sem=send_sems.at[a, k],
                                                recv_sem=recv_sems.at[a, k], device_id=peer, device_id_type=MESH)
            return send, recv

        mine = [pltpu.make_async_copy(ins[a].at[my_slot], outs[a].at[my_slot], local_sems.at[a]) for a in range(n)]
        for cp in mine:
            cp.start()
        copies = [copy(a, k) for a in range(n) for k in range(len(flips))]
        for send, _ in copies:
            send.start()
        for send, recv in copies:
            recv.wait_recv()
            send.wait_send()
        for cp in mine:
            cp.wait()

    return pl.pallas_call(
        body, name=name,
        in_specs=[ANY] * n, out_specs=[ANY] * n,
        out_shape=[jax.ShapeDtypeStruct(s.shape, s.dtype) for s in slabs],
        scratch_shapes=[pltpu.SemaphoreType.DMA((n, 7)), pltpu.SemaphoreType.DMA((n, 7)), pltpu.SemaphoreType.DMA((n,))],
    )(*slabs)


def _row_tile(rows):
    for cand in range(256, 7, -8):
        if rows % cand == 0:
            return cand
    return rows


def _window(w):
    wp = max(-(-((w * r) % LANES + w) // LANES) for r in range(N_DEV)) * LANES
    assert all((w * r) // LANES * LANES + wp <= N_DEV * w for r in range(N_DEV))
    return wp


def _join_cols(slabs, name):
    _, R, w = slabs.shape
    tr = _row_tile(R)
    wp = _window(w)

    def body(g_ref, o_ref, pad_ref):
        if w % LANES == 0:
            for r in range(N_DEV):
                o_ref[:, w * r:w * (r + 1)] = g_ref[r]
            return
        o_ref[...] = jnp.zeros_like(o_ref)
        pad_ref[...] = jnp.zeros_like(pad_ref)
        for r in range(N_DEV):
            q, s = divmod(w * r, LANES)
            pad_ref[:, :w] = g_ref[r]
            y = pad_ref[...]
            if s:
                y = pltpu.roll(y, s, axis=1)
            o_ref[:, LANES * q:LANES * q + wp] += y

    return pl.pallas_call(
        body, name=name, grid=(R // tr,),
        in_specs=[pl.BlockSpec((N_DEV, tr, w), lambda i: (0, i, 0))], out_specs=pl.BlockSpec((tr, N_DEV * w), lambda i: (i, 0)),
        out_shape=jax.ShapeDtypeStruct((R, N_DEV * w), slabs.dtype), scratch_shapes=[pltpu.VMEM((tr, wp), slabs.dtype)],
        compiler_params=_params(("parallel",)),
    )(slabs)


def _split_cols(pieces, name):
    R = pieces[0].shape[0]
    widths = [p.shape[1] for p in pieces]
    total = sum(widths)
    w = total // N_DEV
    tr = _row_tile(R)
    wp = _window(w)
    offs = np.cumsum([0] + widths)
    dtype = pieces[0].dtype

    def body(*refs):
        ins, (o_ref, full_ref) = refs[:len(pieces)], refs[len(pieces):]
        for p_ref, a, b in zip(ins, offs[:-1], offs[1:]):
            full_ref[:, a:b] = p_ref[...].astype(dtype)
        for r in range(N_DEV):
            q, s = divmod(w * r, LANES)
            y = full_ref[:, LANES * q:LANES * q + wp]
            if s:
                y = pltpu.roll(y, wp - s, axis=1)
            o_ref[r] = y[:, :w]

    return pl.pallas_call(
        body, name=name, grid=(R // tr,),
        in_specs=[pl.BlockSpec((tr, n), lambda i: (i, 0)) for n in widths], out_specs=pl.BlockSpec((N_DEV, tr, w), lambda i: (0, i, 0)),
        out_shape=jax.ShapeDtypeStruct((N_DEV, R, w), dtype), scratch_shapes=[pltpu.VMEM((tr, total), dtype)],
        compiler_params=_params(("parallel",)),
    )(*pieces)


def _adamw(g, w, m, v):
    m_new = ADAM_B1 * m + (1.0 - ADAM_B1) * g
    v_new = ADAM_B2 * v + (1.0 - ADAM_B2) * jnp.square(g)
    m_hat = m_new / (1.0 - ADAM_B1 ** ADAM_STEP)
    v_hat = v_new / (1.0 - ADAM_B2 ** ADAM_STEP)
    return -ADAM_LR * (m_hat / (jnp.sqrt(v_hat) + ADAM_EPS) + ADAM_WD * w), m_new, v_new


def _sum_parts(p_ref):
    g = p_ref[0].astype(F32)
    for d in range(1, N_DEV):
        g = g + p_ref[d].astype(F32)
    return g


def _reduce_adamw(parts, w, m, v, name):
    R, C = w.shape
    tr = _row_tile(R)

    def body(p_ref, w_ref, m_ref, v_ref, g_ref, d_ref, mo_ref, vo_ref):
        g = _sum_parts(p_ref)
        g_ref[...] = g
        d_ref[...], mo_ref[...], vo_ref[...] = _adamw(g, w_ref[...], m_ref[...], v_ref[...])

    row = pl.BlockSpec((tr, C), lambda i: (i, 0))
    return pl.pallas_call(
        body, name=name, grid=(R // tr,),
        in_specs=[pl.BlockSpec((N_DEV, tr, C), lambda i: (0, i, 0)), row, row, row],
        out_specs=[row] * 4, out_shape=[jax.ShapeDtypeStruct((R, C), F32)] * 4,
        compiler_params=_params(("parallel",)),
    )(parts, w, m, v)


def _reduce_adamw_small(parts, ws, ms, vs):
    sizes = [a.shape[1] for a in ws]
    k = len(sizes)
    offs = np.cumsum([0] + [-(-n // LANES) * LANES for n in sizes])

    def body(*refs):
        p_ref, w_refs, m_refs, v_refs = refs[0], refs[1:1 + k], refs[1 + k:1 + 2 * k], refs[1 + 2 * k:1 + 3 * k]
        outs, loss_ref = refs[1 + 3 * k:-1], refs[-1]
        g_all = _sum_parts(p_ref)
        for j, n in enumerate(sizes):
            g = g_all[:, offs[j]:offs[j] + LANES * (-(-n // LANES))][:, :n]
            outs[4 * j][...] = g
            outs[4 * j + 1][...], outs[4 * j + 2][...], outs[4 * j + 3][...] = _adamw(g, w_refs[j][...], m_refs[j][...], v_refs[j][...])
        loss_ref[...] = g_all[:, offs[k]:offs[k] + LANES]

    vm = pl.BlockSpec(memory_space=pltpu.VMEM)
    out_shape = [jax.ShapeDtypeStruct((1, n), F32) for n in sizes for _ in range(4)] + [jax.ShapeDtypeStruct((1, LANES), F32)]
    res = pl.pallas_call(
        body, name="reduce_adamw_replicated", in_specs=[vm] * (1 + 3 * k), out_specs=[vm] * len(out_shape), out_shape=out_shape,
        compiler_params=_params(),
    )(parts, *ws, *ms, *vs)
    return [res[4 * j:4 * j + 4] for j in range(k)], res[-1]


COL_SHARDED = ("w_in", "w_branch_a", "w_branch_b", "w_up", "conv_w")
ROW_SHARDED = ("w_out", "w_down")
SMALL = ("b_gate", "sinks", "ln1_g", "ln1_b", "conv_b", "ln2_g", "ln2_b")
ORDER = ("w_in", "b_gate", "sinks", "w_branch_a", "w_branch_b", "w_out", "ln1_g", "ln1_b", "w_up", "conv_w", "conv_b", "w_down", "ln2_g", "ln2_b")


def _pad_lanes(a):
    pad = (-a.shape[-1]) % LANES
    return a if pad == 0 else jnp.pad(a, ((0, 0), (0, pad)))


def kernel(x, positions, w_in, b_gate, sinks, w_branch_a, w_branch_b, w_out, ln1_g, ln1_b, w_up, conv_w, conv_b, w_down, ln2_g, ln2_b, loss_target, m_w_in, m_b_gate, m_sinks, m_w_branch_a, m_w_branch_b, m_w_out, m_ln1_g, m_ln1_b, m_w_up, m_conv_w, m_conv_b, m_w_down, m_ln2_g, m_ln2_b, v_w_in, v_b_gate, v_sinks, v_w_branch_a, v_w_branch_b, v_w_out, v_ln1_g, v_ln1_b, v_w_up, v_conv_w, v_conv_b, v_w_down, v_ln2_g, v_ln2_b):
    args = dict(locals())
    sharded = COL_SHARDED + ROW_SHARDED
    w = {n: args[n][0] if n in sharded else args[n] for n in ORDER}
    m = {n: args["m_" + n][0] if n in sharded else args["m_" + n] for n in ORDER}
    v = {n: args["v_" + n][0] if n in sharded else args["v_" + n] for n in ORDER}

    travel = {n: (w[n] if n == "conv_w" else w[n].astype(BF16)) for n in sharded}
    (g_in,) = _all_gather([travel["w_in"]], "all_gather_w_in")
    w_in_full = _join_cols(g_in, "join_w_in")
    later = ("w_branch_a", "w_branch_b", "w_out", "w_up", "conv_w", "w_down")

    def join(name, slabs):
        return _join_cols(slabs, "join_" + name) if name in COL_SHARDED else slabs.reshape(-1, slabs.shape[-1])

    def split(name, grad):
        if name in COL_SHARDED:
            return _split_cols(grad if isinstance(grad, tuple) else (grad,), "split_d" + name)
        return grad.reshape((N_DEV, -1, grad.shape[-1]))

    def early_exchange(grads):
        return _Exchange([split(n, grads[n]) for n in later], ["scatter"] * len(later))

    def tail_exchange(grads, loss):
        small_pack = jnp.concatenate(
            [_pad_lanes(p) for n in SMALL for p in (grads[n] if isinstance(grads[n], tuple) else (grads[n],))] + [loss], axis=1)
        return _Exchange([split("w_in", grads["w_in"]), small_pack], ["scatter", "gather"])

    gather_later = _Exchange([travel[n] for n in later], ["gather"] * len(later))
    _, grad_x, _, early_out, (recv_w_in, small_parts) = _local_step(
        x[0], positions[0], w_in_full, w["b_gate"], w["sinks"][0], w["ln1_g"], w["ln1_b"], w["conv_b"], w["ln2_g"], w["ln2_b"], loss_target[0],
        (gather_later, lambda arrived: [join(n, a) for n, a in zip(later, arrived)]), early_exchange, tail_exchange)
    recv = dict(zip(later, early_out), w_in=recv_w_in)

    res = {n: _reduce_adamw(recv[n], w[n], m[n], v[n], "reduce_adamw_" + n) for n in sharded}
    small_res, loss_sum = _reduce_adamw_small(small_parts, [w[n] for n in SMALL], [m[n] for n in SMALL], [v[n] for n in SMALL])
    res.update(zip(SMALL, small_res))
    out = [loss_sum[0, 0], grad_x[None]]
    for k in range(4):
        out += [res[n][k][None] if n in sharded else res[n][k] for n in ORDER]
    return tuple(out)
```

```python
import functools

import jax
import jax.numpy as jnp
import numpy as np
from jax import lax
from jax.experimental import pallas as pl
from jax.experimental.pallas import tpu as pltpu

D_MODEL = 1024
HEAD_DIM = 64
SWA_Q_HEADS = 8
SWA_KV_HEADS = 2
SB_HEADS = 8
WINDOW = 128
ROPE_THETA = 10000.0
D_FF = 2816
LN_EPS = 1e-5
DEPTH = 1
ALPHA = (2.0 * DEPTH) ** 0.25
SWA_Q_WIDTH = SWA_Q_HEADS * HEAD_DIM
SWA_KV_WIDTH = SWA_KV_HEADS * HEAD_DIM
SB_WIDTH = SB_HEADS * HEAD_DIM
GATE_WIDTH = 2 * D_MODEL
IN_WIDTHS = (SWA_Q_WIDTH, SWA_KV_WIDTH, SWA_KV_WIDTH, SB_WIDTH, SB_WIDTH, SB_WIDTH, GATE_WIDTH)
IN_TOTAL = sum(IN_WIDTHS)
ATTN_SCALE = HEAD_DIM ** -0.5

ADAM_LR = 0.001
ADAM_B1 = 0.9
ADAM_B2 = 0.999
ADAM_EPS = 1e-08
ADAM_WD = 0.01
ADAM_STEP = 10

N_DEV = 8
LANES = 128
SB_BLOCK = 256
SB_PAIRS = 4
VMEM_LIMIT = 56 * 1024 * 1024

F32 = jnp.float32
BF16 = jnp.bfloat16
ACT_DTYPE = BF16
MESH = pl.DeviceIdType.MESH


def _params(sem=None):
    return pltpu.CompilerParams(dimension_semantics=sem, vmem_limit_bytes=VMEM_LIMIT)


def _dot(a, b):
    return jnp.dot(a, b, preferred_element_type=F32)


def _dot_nt(a, b):
    return lax.dot_general(a, b, (((1,), (1,)), ((), ())), preferred_element_type=F32)


def _dot_tn(a, b):
    return lax.dot_general(a, b, (((0,), (0,)), ((), ())), preferred_element_type=F32)


def _split_bf16(v):
    hi = v.astype(BF16)
    lo = (v - hi.astype(F32)).astype(BF16)
    return hi, lo


def _matmul(a, b, *, kind, out_shape, grid, a_spec, b_spec, out_spec, name, add=None, add_spec=None, add_scale=1.0):
    dot = {"nn": _dot, "nt": _dot_nt, "tn": _dot_tn}[kind]

    def body(*refs):
        if add is None:
            a_ref, b_ref, o_ref = refs
        else:
            a_ref, b_ref, add_ref, o_ref = refs
        r = dot(a_ref[...].astype(BF16), b_ref[...].astype(BF16))
        if add is not None:
            r = r + add_scale * add_ref[...]
        o_ref[...] = r.astype(o_ref.dtype)

    ins = [a, b] + ([] if add is None else [add])
    specs = [a_spec, b_spec] + ([] if add is None else [add_spec])
    return pl.pallas_call(
        body, name=name, grid=grid, in_specs=specs, out_specs=out_spec, out_shape=out_shape,
        compiler_params=_params(("parallel",) * len(grid)),
    )(*ins)


def _rope_tables(pos_col, inv_freq_lanes):
    T = pos_col.shape[0]
    tm = min(512, T)

    def body(pos_ref, f_ref, cos_ref, sin_ref):
        ang = pos_ref[...].astype(F32) * f_ref[...]
        cos_ref[...] = jnp.cos(ang)
        sin_ref[...] = jnp.sin(ang)

    return pl.pallas_call(
        body, name="rope_tables", grid=(T // tm,),
        in_specs=[pl.BlockSpec((tm, 1), lambda i: (i, 0)), pl.BlockSpec((1, LANES), lambda i: (0, 0))],
        out_specs=[pl.BlockSpec((tm, LANES), lambda i: (i, 0))] * 2,
        out_shape=[jax.ShapeDtypeStruct((T, LANES), F32)] * 2,
        compiler_params=_params(("parallel",)),
    )(pos_col, inv_freq_lanes)


def _lane_iota(shape):
    return lax.broadcasted_iota(jnp.int32, shape, len(shape) - 1)


def _rot_half(t):
    first = (_lane_iota(t.shape) % HEAD_DIM) < (HEAD_DIM // 2)
    return jnp.where(first, -pltpu.roll(t, LANES - HEAD_DIM // 2, axis=1), pltpu.roll(t, HEAD_DIM // 2, axis=1))


def _rope(t, cos, sin):
    return t * cos + _rot_half(t) * sin


def _rope_transpose(d, cos, sin):
    return d * cos - _rot_half(d * sin)


_IN_DTYPES = (F32, F32, BF16, BF16, BF16, BF16, F32)


def _in_proj(x, w_in_b):
    T = x.shape[0]
    tm = min(256, T)
    offs = np.cumsum((0,) + IN_WIDTHS)

    def body(x_ref, w_ref, xb_ref, *outs):
        xb = x_ref[...].astype(BF16)
        xb_ref[...] = xb
        for o_ref, a, b in zip(outs, offs[:-1], offs[1:]):
            o_ref[...] = _dot(xb, w_ref[:, a:b]).astype(o_ref.dtype)

    row = lambda n: pl.BlockSpec((tm, n), lambda i: (i, 0))
    return pl.pallas_call(
        body, name="in_proj", grid=(T // tm,),
        in_specs=[row(D_MODEL), pl.BlockSpec((D_MODEL, IN_TOTAL), lambda i: (0, 0))],
        out_specs=[row(D_MODEL)] + [row(n) for n in IN_WIDTHS],
        out_shape=[jax.ShapeDtypeStruct((T, D_MODEL), BF16)] + [jax.ShapeDtypeStruct((T, n), dt) for n, dt in zip(IN_WIDTHS, _IN_DTYPES)],
        compiler_params=_params(("parallel",)),
    )(x, w_in_b)


def _swa_specs(T):
    blk = WINDOW
    cur = lambda n: pl.BlockSpec((blk, n), lambda i: (i, 0))
    prev = lambda n: pl.BlockSpec((blk, n), lambda i: (jnp.maximum(i - 1, 0), 0))
    return blk, cur, prev


SWA_GROUP = SWA_Q_HEADS // SWA_KV_HEADS


def _swa_stack(pairs):
    lane = _lane_iota(pairs[0].shape)
    zero = jnp.zeros((), pairs[0].dtype)
    rows = []
    for h in range(SWA_Q_HEADS):
        hh, g = h % 2, h // SWA_GROUP
        x = jnp.where((lane >= hh * HEAD_DIM) & (lane < (hh + 1) * HEAD_DIM), pairs[h // 2], zero)
        rows.append(x if hh == g else pltpu.roll(x, HEAD_DIM, axis=1))
    return jnp.concatenate(rows, axis=0)


def _swa_unstack(stacked, blk):
    low = _lane_iota((blk, LANES)) < HEAD_DIM
    pairs = []
    for pp in range(SWA_Q_HEADS // 2):
        halves = []
        for hh in range(2):
            h = 2 * pp + hh
            x = stacked[h * blk:(h + 1) * blk]
            halves.append(x if hh == h // SWA_GROUP else pltpu.roll(x, HEAD_DIM, axis=1))
        pairs.append(jnp.where(low, halves[0], halves[1]))
    return pairs


def _swa_probs(i, q_stack, kwin, sink_ref, blk):
    r = lax.broadcasted_iota(jnp.int32, (blk, 2 * blk), 0)
    c = lax.broadcasted_iota(jnp.int32, (blk, 2 * blk), 1)
    rel = blk + r - c
    valid = (rel >= 0) & (rel < WINDOW) & ((c >= blk) | (i > 0))
    bias = jnp.concatenate([jnp.where(valid, 0.0, -1e30)] * SWA_Q_HEADS, axis=0)
    head = lax.broadcasted_iota(jnp.int32, (SWA_Q_HEADS * blk, 1), 0) // blk
    sink = jnp.zeros((SWA_Q_HEADS * blk, 1), F32)
    for h in range(SWA_Q_HEADS):
        sink = jnp.where(head == h, sink_ref[h], sink)
    s = _dot_nt(q_stack, kwin) * ATTN_SCALE + bias
    m = jnp.maximum(jnp.max(s, axis=1, keepdims=True), sink)
    p = jnp.exp(s - m)
    es = jnp.exp(sink - m)
    den = jnp.sum(p, axis=1, keepdims=True) + es
    return p / den, es / den


def _swa_inputs(q_ref, kp_ref, kc_ref, vp_ref, vc_ref, cp_ref, cc_ref, sp_ref, sc_ref):
    cc, sc = cc_ref[...], sc_ref[...]
    kwin = jnp.concatenate([_rope(kp_ref[...], cp_ref[...], sp_ref[...]), _rope(kc_ref[...], cc, sc)], axis=0).astype(BF16)
    vwin = jnp.concatenate([vp_ref[...], vc_ref[...]], axis=0)
    q_stack = _swa_stack([_rope(q_ref[:, pp * LANES:(pp + 1) * LANES], cc, sc) for pp in range(SWA_Q_HEADS // 2)]).astype(BF16)
    return q_stack, kwin, vwin


def _swa_fwd(qa, ka, va, cos, sin, sinks):
    T = qa.shape[0]
    blk, cur, prev = _swa_specs(T)

    def body(sink_ref, q_ref, kp_ref, kc_ref, vp_ref, vc_ref, cp_ref, cc_ref, sp_ref, sc_ref, o_ref):
        q_stack, kwin, vwin = _swa_inputs(q_ref, kp_ref, kc_ref, vp_ref, vc_ref, cp_ref, cc_ref, sp_ref, sc_ref)
        probs, _ = _swa_probs(pl.program_id(0), q_stack, kwin, sink_ref, blk)
        for pp, tile in enumerate(_swa_unstack(_dot(probs.astype(BF16), vwin), blk)):
            o_ref[:, pp * LANES:(pp + 1) * LANES] = tile.astype(o_ref.dtype)

    return pl.pallas_call(
        body, name="swa_fwd", grid=(T // blk,),
        in_specs=[pl.BlockSpec(memory_space=pltpu.SMEM), cur(SWA_Q_WIDTH), prev(LANES), cur(LANES), prev(LANES), cur(LANES),
                  prev(LANES), cur(LANES), prev(LANES), cur(LANES)],
        out_specs=cur(SWA_Q_WIDTH),
        out_shape=jax.ShapeDtypeStruct((T, SWA_Q_WIDTH), BF16),
        compiler_params=_params(("parallel",)),
    )(sinks, qa, ka, ka, va, va, cos, cos, sin, sin)


def _swa_bwd(qa, ka, va, cos, sin, sinks, dya):
    T = qa.shape[0]
    blk, cur, prev = _swa_specs(T)
    full = lambda n: pl.BlockSpec((T, n), lambda i: (0, 0))

    def body(sink_ref, q_ref, kp_ref, kc_ref, vp_ref, vc_ref, cp_ref, cc_ref, sp_ref, sc_ref, do_ref,
             dq_ref, dk_out, dv_out, dsink_ref, dk_ref, dv_ref):
        i = pl.program_id(0)

        @pl.when(i == 0)
        def _():
            dk_ref[...] = jnp.zeros_like(dk_ref)
            dv_ref[...] = jnp.zeros_like(dv_ref)
            dsink_ref[...] = jnp.zeros_like(dsink_ref)

        cp, cc, sp, sc = cp_ref[...], cc_ref[...], sp_ref[...], sc_ref[...]
        q_stack, kwin, vwin = _swa_inputs(q_ref, kp_ref, kc_ref, vp_ref, vc_ref, cp_ref, cc_ref, sp_ref, sc_ref)
        probs, psink = _swa_probs(i, q_stack, kwin, sink_ref, blk)
        do_stack = _swa_stack([do_ref[:, pp * LANES:(pp + 1) * LANES] for pp in range(SWA_Q_HEADS // 2)])
        dp = _dot_nt(do_stack, vwin)
        dsum = jnp.sum(probs * dp, axis=1, keepdims=True)
        ds = (probs * (dp - dsum) * ATTN_SCALE).astype(BF16)
        for pp, tile in enumerate(_swa_unstack(_dot(ds, kwin), blk)):
            dq_ref[:, pp * LANES:(pp + 1) * LANES] = _rope_transpose(tile, cc, sc).astype(dq_ref.dtype)
        dkw = _dot_tn(ds, q_stack)
        dvw = _dot_tn(probs.astype(BF16), do_stack)
        lane1 = _lane_iota((1, LANES))
        sink_share = psink * dsum
        dsink = jnp.zeros((1, LANES), F32)
        for h in range(SWA_Q_HEADS):
            dsink = dsink + jnp.where(lane1 == h, -jnp.sum(sink_share[h * blk:(h + 1) * blk]), 0.0)
        dsink_ref[...] += dsink
        ip = jnp.maximum(i - 1, 0)
        rows_p = pl.ds(pl.multiple_of(ip * blk, blk), blk)
        rows_c = pl.ds(pl.multiple_of(i * blk, blk), blk)
        dk_ref[rows_p, :] += _rope_transpose(dkw[:blk], cp, sp)
        dv_ref[rows_p, :] += dvw[:blk]
        dk_ref[rows_c, :] += _rope_transpose(dkw[blk:], cc, sc)
        dv_ref[rows_c, :] += dvw[blk:]

        @pl.when(i == T // blk - 1)
        def _():
            dk_out[...] = dk_ref[...].astype(BF16)
            dv_out[...] = dv_ref[...].astype(BF16)

    return pl.pallas_call(
        body, name="swa_bwd", grid=(T // blk,),
        in_specs=[pl.BlockSpec(memory_space=pltpu.SMEM), cur(SWA_Q_WIDTH), prev(LANES), cur(LANES), prev(LANES), cur(LANES),
                  prev(LANES), cur(LANES), prev(LANES), cur(LANES), cur(SWA_Q_WIDTH)],
        out_specs=[cur(SWA_Q_WIDTH), full(LANES), full(LANES), pl.BlockSpec((1, LANES), lambda i: (0, 0))],
        out_shape=[jax.ShapeDtypeStruct((T, SWA_Q_WIDTH), BF16), jax.ShapeDtypeStruct((T, LANES), BF16),
                   jax.ShapeDtypeStruct((T, LANES), BF16), jax.ShapeDtypeStruct((1, LANES), F32)],
        scratch_shapes=[pltpu.VMEM((T, LANES), F32)] * 2,
        compiler_params=_params(("arbitrary",)),
    )(sinks, qa, ka, ka, va, va, cos, cos, sin, sin, dya)


class _Exchange:
    FLIPS = [(fx, fy, fc) for fx in (0, 1) for fy in (0, 1) for fc in (0, 1) if (fx, fy, fc) != (0, 0, 0)]

    def __init__(self, arrays, kinds):
        self.arrays, self.kinds, self.n = list(arrays), list(kinds), len(arrays)

    def out_shape(self):
        return [jax.ShapeDtypeStruct(a.shape if k == "scatter" else (N_DEV,) + a.shape, a.dtype) for a, k in zip(self.arrays, self.kinds)]

    def scratch(self):
        return [pltpu.SemaphoreType.DMA((self.n, 7)), pltpu.SemaphoreType.DMA((self.n, 7)), pltpu.SemaphoreType.DMA((self.n,))]

    def bind(self, ins, outs, send_sems, recv_sems, local_sems):
        x, y, c = lax.axis_index("x"), lax.axis_index("y"), lax.axis_index("c")
        me = 4 * x + 2 * y + c
        local, remote = [], []
        for a, kind in enumerate(self.kinds):
            mine = ins[a].at[me] if kind == "scatter" else ins[a]
            local.append(pltpu.make_async_copy(mine, outs[a].at[me], local_sems.at[a]))
            for k, (fx, fy, fc) in enumerate(self.FLIPS):
                peer = (x ^ fx, y ^ fy, c ^ fc)
                peer_slot = 4 * peer[0] + 2 * peer[1] + peer[2]
                src = ins[a].at[peer_slot] if kind == "scatter" else ins[a]
                sems = dict(send_sem=send_sems.at[a, k], recv_sem=recv_sems.at[a, k], device_id=peer, device_id_type=MESH)
                remote.append((pltpu.make_async_remote_copy(src_ref=src, dst_ref=outs[a].at[me], **sems),
                               pltpu.make_async_remote_copy(src_ref=src, dst_ref=outs[a].at[peer_slot], **sems)))

        def start():
            for cp in local:
                cp.start()
            for send, _ in remote:
                send.start()

        def wait():
            for send, arrival in remote:
                arrival.wait_recv()
                send.wait_send()
            for cp in local:
                cp.wait()

        return start, wait


def _hosted_call(body, name, grid, exchange, *, in_specs, out_specs, out_shape, semantics, args, scratch=(), prefetch=()):
    n = 0 if exchange is None else exchange.n
    n_pre, n_in, n_out, n_scratch = len(prefetch), len(in_specs), len(out_specs), len(scratch)

    def hosted(*refs):
        pre, rest = refs[:n_pre], refs[n_pre:]
        ins, rest = rest[:n_in], rest[n_in:]
        ex_ins, rest = rest[:n], rest[n:]
        outs, rest = rest[:n_out], rest[n_out:]
        ex_outs, rest = rest[:n], rest[n:]
        own, sems = rest[:n_scratch], rest[n_scratch:]
        if exchange is None:
            return body(*pre, *ins, *outs, *own)
        start, wait = exchange.bind(ex_ins, ex_outs, *sems)
        ids = [pl.program_id(d) for d in range(len(grid))]
        first = functools.reduce(jnp.logical_and, [i == 0 for i in ids])
        last = functools.reduce(jnp.logical_and, [i == g - 1 for i, g in zip(ids, grid)])
        pl.when(first)(start)
        body(*pre, *ins, *outs, *own)
        pl.when(last)(wait)

    grid_spec = pltpu.PrefetchScalarGridSpec(
        num_scalar_prefetch=n_pre, grid=grid, in_specs=list(in_specs) + [ANY] * n, out_specs=list(out_specs) + [ANY] * n,
        scratch_shapes=list(scratch) + ([] if exchange is None else exchange.scratch()))
    res = pl.pallas_call(
        hosted, name=name, grid_spec=grid_spec, out_shape=list(out_shape) + ([] if exchange is None else exchange.out_shape()),
        compiler_params=_params(semantics if exchange is None else ("arbitrary",) * len(grid)),
    )(*prefetch, *args, *([] if exchange is None else exchange.arrays))
    return res[:n_out], res[n_out:]


SOFTPLUS_LINEAR_FROM = 30.0


def _sb_scores(qm, k, valid):
    z = _dot_nt(qm, k)
    sp = jnp.where(z > SOFTPLUS_LINEAR_FROM, z, jnp.log(1.0 + jnp.exp(z)))
    log_beta = z - sp
    if valid is not None:
        sp = jnp.where(valid, sp, 0.0)
    return sp, log_beta


def _tri2(B, cmp):
    r = lax.broadcasted_iota(jnp.int32, (2 * B, B), 0) % B
    c = lax.broadcasted_iota(jnp.int32, (2 * B, B), 1)
    return cmp(r, c).astype(BF16)


def _tri_sum(v, tri2):
    hi, lo = _split_bf16(v)
    return _dot(jnp.concatenate([hi, lo], axis=1), tri2)


def _head_masks(x):
    low = _lane_iota(x.shape) < HEAD_DIM
    zero = jnp.zeros((), x.dtype)
    return jnp.where(low, x, zero), jnp.where(low, zero, x)


def _strictly_below(B):
    r = lax.broadcasted_iota(jnp.int32, (B, B), 0)
    c = lax.broadcasted_iota(jnp.int32, (B, B), 1)
    return c < r


def _sb_grid(T, descending):
    B = min(SB_BLOCK, T)
    n = T // B
    pairs = [(i, j) for i in range(n) for j in (range(i, -1, -1) if descending else range(i + 1))]
    return B, jnp.asarray([p[0] for p in pairs], jnp.int32), jnp.asarray([p[1] for p in pairs], jnp.int32)


N_PAIRS = SB_HEADS // 2
PAIR_COLS = [slice(p * LANES, (p + 1) * LANES) for p in range(N_PAIRS)]


def _sb_fwd(qb, kb, vb, exchange=None):
    T = qb.shape[0]
    B, i_tab, j_tab = _sb_grid(T, descending=True)
    n = T // B

    def body(i_ref, j_ref, q_ref, k_ref, v_ref, o_ref, a_ref, b_ref, acc_ref, c_ref, tri_ref):
        s = pl.program_id(0)
        i, j = i_ref[s], j_ref[s]

        @pl.when(s == 0)
        def _():
            tri_ref[...] = _tri2(B, lambda r, c: r > c)

        @pl.when(j == i)
        def _():
            acc_ref[...] = jnp.zeros_like(acc_ref)
            c_ref[...] = jnp.zeros_like(c_ref)

        def block(valid):
            for p, cols in enumerate(PAIR_COLS):
                qms = _head_masks(q_ref[:, cols] * ATTN_SCALE)
                k = k_ref[:, cols]
                probs = []
                for hh in range(2):
                    h = 2 * p + hh
                    sp, lb = _sb_scores(qms[hh], k, valid)
                    c = c_ref[h]
                    a = jnp.exp(lb - (c + _tri_sum(sp, tri_ref[...])))
                    beta = jnp.exp(lb)
                    if valid is not None:
                        a = jnp.where(valid, a, 0.0)
                        beta = jnp.where(valid, beta, 0.0)
                    probs.append(a.astype(BF16))
                    a_ref[h] = probs[-1]
                    b_ref[h] = beta.astype(BF16)
                    c_ref[h] = c + jnp.sum(sp, axis=1, keepdims=True)
                acc_ref[:, cols] += _dot(jnp.concatenate(probs, axis=1), jnp.concatenate(_head_masks(v_ref[:, cols]), axis=0))

        pl.when(j == i)(lambda: block(_strictly_below(B)))
        pl.when(j != i)(lambda: block(None))

        @pl.when(j == 0)
        def _():
            o_ref[...] = acc_ref[...].astype(o_ref.dtype)

    q_spec = pl.BlockSpec((B, SB_WIDTH), lambda s, i_ref, j_ref: (i_ref[s], 0))
    k_spec = pl.BlockSpec((B, SB_WIDTH), lambda s, i_ref, j_ref: (j_ref[s], 0))
    tile = pl.BlockSpec((None, None, SB_HEADS, B, B), lambda s, i_ref, j_ref: (i_ref[s], j_ref[s], 0, 0, 0))
    saved = jax.ShapeDtypeStruct((n, n, SB_HEADS, B, B), BF16)
    return _hosted_call(
        body, "sb_fwd", (int(i_tab.shape[0]),), exchange, prefetch=(i_tab, j_tab),
        in_specs=[q_spec, k_spec, k_spec], out_specs=[q_spec, tile, tile],
        out_shape=[jax.ShapeDtypeStruct((T, SB_WIDTH), BF16), saved, saved],
        scratch=[pltpu.VMEM((B, SB_WIDTH), F32), pltpu.VMEM((SB_HEADS, B, 1), F32), pltpu.VMEM((2 * B, B), BF16)],
        semantics=("arbitrary",), args=(qb, kb, vb))


def _sb_bwd(qb, kb, vb, probs, betas, dyb, exchange=None):
    T = qb.shape[0]
    B, i_tab, j_tab = _sb_grid(T, descending=False)
    n_steps = int(i_tab.shape[0])

    def block_diag_t(x):
        xt = x.T
        top = lax.broadcasted_iota(jnp.int32, xt.shape, 0) < HEAD_DIM
        zero = jnp.zeros((), x.dtype)
        return jnp.concatenate([jnp.where(top, xt, zero), jnp.where(top, zero, xt)], axis=1)

    def body(i_ref, j_ref, q_ref, k_ref, v_ref, a_ref, b_ref, do_ref, dq_ref, dk_out, dv_out,
             dq_acc, cg_ref, dkt_ref, dvt_ref, tri_ref, qt_ref, dot_ref):
        s = pl.program_id(0)
        i, j = i_ref[s], j_ref[s]

        @pl.when(s == 0)
        def _():
            dkt_ref[...] = jnp.zeros_like(dkt_ref)
            dvt_ref[...] = jnp.zeros_like(dvt_ref)
            tri_ref[...] = _tri2(B, lambda r, c: r < c)[:B]

        @pl.when(j == 0)
        def _():
            dq_acc[...] = jnp.zeros_like(dq_acc)
            cg_ref[...] = jnp.zeros_like(cg_ref)
            for p, cols in enumerate(PAIR_COLS):
                qt_ref[p] = block_diag_t(q_ref[:, cols] * ATTN_SCALE)
                dot_ref[p] = block_diag_t(do_ref[:, cols])

        for p, cols in enumerate(PAIR_COLS):
            doms = _head_masks(do_ref[:, cols])
            k, v = k_ref[:, cols], v_ref[:, cols]
            dzs = []
            for hh in range(2):
                h = 2 * p + hh
                g = a_ref[h].astype(F32) * _dot_nt(doms[hh], v)
                cg = cg_ref[h]
                gsum = g + (cg + _dot(g.astype(BF16), tri_ref[...]))
                dzs.append((g - b_ref[h].astype(F32) * gsum).astype(BF16))
                cg_ref[h] = cg + jnp.sum(g, axis=1, keepdims=True)
            dq_acc[:, cols] += _dot(jnp.concatenate(dzs, axis=1), jnp.concatenate(_head_masks(k), axis=0))
            dkt_ref[j, cols, :] += _dot(qt_ref[p], jnp.concatenate(dzs, axis=0))
            dvt_ref[j, cols, :] += _dot(dot_ref[p], jnp.concatenate([a_ref[2 * p], a_ref[2 * p + 1]], axis=0))

        @pl.when(j == i)
        def _():
            dq_ref[...] = (dq_acc[...] * ATTN_SCALE).astype(dq_ref.dtype)

        @pl.when(s == n_steps - 1)
        def _():
            for jb in range(T // B):
                dk_out[jb * B:(jb + 1) * B, :] = dkt_ref[jb].T.astype(BF16)
                dv_out[jb * B:(jb + 1) * B, :] = dvt_ref[jb].T.astype(BF16)

    q_spec = pl.BlockSpec((B, SB_WIDTH), lambda s, i_ref, j_ref: (i_ref[s], 0))
    k_spec = pl.BlockSpec((B, SB_WIDTH), lambda s, i_ref, j_ref: (j_ref[s], 0))
    tile = pl.BlockSpec((None, None, SB_HEADS, B, B), lambda s, i_ref, j_ref: (i_ref[s], j_ref[s], 0, 0, 0))
    full = pl.BlockSpec((T, SB_WIDTH), lambda s, i_ref, j_ref: (0, 0))
    return _hosted_call(
        body, "sb_bwd", (n_steps,), exchange, prefetch=(i_tab, j_tab),
        in_specs=[q_spec, k_spec, k_spec, tile, tile, q_spec], out_specs=[q_spec, full, full],
        out_shape=[jax.ShapeDtypeStruct((T, SB_WIDTH), BF16)] * 3,
        scratch=[pltpu.VMEM((B, SB_WIDTH), F32), pltpu.VMEM((SB_HEADS, B, 1), F32), pltpu.VMEM((T // B, SB_WIDTH, B), F32),
                 pltpu.VMEM((T // B, SB_WIDTH, B), F32), pltpu.VMEM((B, B), BF16), pltpu.VMEM((N_PAIRS, LANES, 2 * B), BF16),
                 pltpu.VMEM((N_PAIRS, LANES, 2 * B), BF16)],
        semantics=("arbitrary",), args=(qb, kb, vb, probs, betas, dyb))


def _ln_stats(u):
    mu = jnp.mean(u, axis=-1, keepdims=True)
    xc = u - mu
    var = jnp.mean(xc * xc, axis=-1, keepdims=True)
    rstd = lax.rsqrt(var + LN_EPS)
    return xc * rstd, rstd


def _ln_bwd(dy, xhat, rstd, g):
    dxh = dy * g
    return rstd * (dxh - jnp.mean(dxh, axis=-1, keepdims=True) - xhat * jnp.mean(dxh * xhat, axis=-1, keepdims=True))


def _gates(gl_ref, bg_ref):
    ga = jax.nn.sigmoid(gl_ref[:, :D_MODEL] + bg_ref[:, :D_MODEL])
    gb = jax.nn.sigmoid(gl_ref[:, D_MODEL:] + bg_ref[:, D_MODEL:])
    return ga, gb


def _mix_fwd(ya, yb, gl, x, wa, wb, wo, b_gate, ln1_g, ln1_b):
    T = x.shape[0]
    tm = min(256, T)

    def body(ya_ref, yb_ref, gl_ref, x_ref, wa_ref, wb_ref, wo_ref, bg_ref, g_ref, b_ref, h_ref, u_ref, x1_ref):
        ga, gb = _gates(gl_ref, bg_ref)
        h = (ga * _dot(ya_ref[...], wa_ref[...]) + gb * _dot(yb_ref[...], wb_ref[...])).astype(BF16)
        h_ref[...] = h
        u = ALPHA * x_ref[...] + _dot(h, wo_ref[...])
        u_ref[...] = u
        xhat, _ = _ln_stats(u)
        x1_ref[...] = (xhat * g_ref[...] + b_ref[...]).astype(BF16)

    row = lambda n: pl.BlockSpec((tm, n), lambda i: (i, 0))
    const = lambda r, n: pl.BlockSpec((r, n), lambda i: (0, 0))
    return pl.pallas_call(
        body, name="mix_fwd", grid=(T // tm,),
        in_specs=[row(SWA_Q_WIDTH), row(SB_WIDTH), row(GATE_WIDTH), row(D_MODEL), const(SWA_Q_WIDTH, D_MODEL), const(SB_WIDTH, D_MODEL),
                  const(D_MODEL, D_MODEL), const(1, GATE_WIDTH), const(1, D_MODEL), const(1, D_MODEL)],
        out_specs=[row(D_MODEL)] * 3,
        out_shape=[jax.ShapeDtypeStruct((T, D_MODEL), BF16), jax.ShapeDtypeStruct((T, D_MODEL), F32), jax.ShapeDtypeStruct((T, D_MODEL), BF16)],
        compiler_params=_params(("parallel",)),
    )(ya, yb, gl, x, wa, wb, wo, b_gate, ln1_g, ln1_b)


def _mix_bwd(du1, ya, yb, gl, wa, wb, wo, b_gate):
    T = du1.shape[0]
    tm = min(256, T)

    def body(du_ref, ya_ref, yb_ref, gl_ref, wa_ref, wb_ref, wo_ref, bg_ref, dya_ref, dyb_ref, dgl_ref, dta_ref, dtb_ref, dbg_ref):
        @pl.when(pl.program_id(0) == 0)
        def _():
            dbg_ref[...] = jnp.zeros_like(dbg_ref)

        dh = _dot_nt(du_ref[...].astype(BF16), wo_ref[...])
        ga, gb = _gates(gl_ref, bg_ref)
        for gate, y_ref, w_ref, dy_ref, dt_ref, lo in ((ga, ya_ref, wa_ref, dya_ref, dta_ref, 0), (gb, yb_ref, wb_ref, dyb_ref, dtb_ref, D_MODEL)):
            t = _dot(y_ref[...], w_ref[...])
            dlogit = dh * t * gate * (1.0 - gate)
            dgl_ref[:, lo:lo + D_MODEL] = dlogit.astype(BF16)
            dbg_ref[:, lo:lo + D_MODEL] += jnp.sum(dlogit, axis=0, keepdims=True)
            dt = (dh * gate).astype(BF16)
            dt_ref[...] = dt
            dy_ref[...] = _dot_nt(dt, w_ref[...]).astype(BF16)

    row = lambda n: pl.BlockSpec((tm, n), lambda i: (i, 0))
    const = lambda r, n: pl.BlockSpec((r, n), lambda i: (0, 0))
    sds = lambda n, dt: jax.ShapeDtypeStruct((T, n), dt)
    return pl.pallas_call(
        body, name="mix_bwd", grid=(T // tm,),
        in_specs=[row(D_MODEL), row(SWA_Q_WIDTH), row(SB_WIDTH), row(GATE_WIDTH), const(SWA_Q_WIDTH, D_MODEL), const(SB_WIDTH, D_MODEL),
                  const(D_MODEL, D_MODEL), const(1, GATE_WIDTH)],
        out_specs=[row(SWA_Q_WIDTH), row(SB_WIDTH), row(GATE_WIDTH), row(D_MODEL), row(D_MODEL), const(1, GATE_WIDTH)],
        out_shape=[sds(SWA_Q_WIDTH, BF16), sds(SB_WIDTH, BF16), sds(GATE_WIDTH, BF16), sds(D_MODEL, BF16), sds(D_MODEL, BF16),
                   jax.ShapeDtypeStruct((1, GATE_WIDTH), F32)],
        compiler_params=_params(("arbitrary",)),
    )(du1, ya, yb, gl, wa, wb, wo, b_gate)


CONV_COLS = LANES


CONV_CHUNK = 64
CONV_CHUNK_FWD = 256
HALO = 8


def _taps(ref, r0, rows, lead):
    return [ref[pl.ds(r0 + lead + k, rows), :] for k in ((-2, -1, 0) if lead else (0, 1, 2))]


def _chunks(T, rows, step, init=None):
    def body(c, carry):
        out = step(pl.multiple_of(c * rows, rows), *(() if init is None else (carry,)))
        return carry if init is None else out
    return lax.fori_loop(0, T // rows, body, 0 if init is None else init)


def _conv_chunk(taps, w_ref, b_ref):
    return w_ref[0:1, :] * taps[0] + w_ref[1:2, :] * taps[1] + w_ref[2:3, :] * taps[2] + b_ref[...]


def _fold(x):
    return jnp.sum(x.reshape(x.shape[0] // 8, 8, x.shape[1]), axis=0)


def _conv_specs(T):
    nb = D_FF // CONV_COLS
    pair = pl.BlockSpec((2, T, CONV_COLS), lambda j: (0, 0, j))
    gate = lambda r: pl.BlockSpec((r, CONV_COLS), lambda j: (0, j))
    up = lambda r: pl.BlockSpec((r, CONV_COLS), lambda j: (0, j + nb))
    return nb, pair, gate, up


def _conv_glu_fwd(p3, conv_w, conv_b):
    T = p3.shape[1]
    nb, pair, gate, up = _conv_specs(T)

    R = min(CONV_CHUNK_FWD, T)

    def body(p_ref, wg_ref, wu_ref, bg_ref, bu_ref, s_ref, pg_s, pu_s):
        for half, scr in enumerate((pg_s, pu_s)):
            scr[0:HALO, :] = jnp.zeros((HALO, CONV_COLS), F32)
            scr[HALO:HALO + T, :] = p_ref[half].astype(F32)
        def step(r0):
            ag = _conv_chunk(_taps(pg_s, r0, R, HALO), wg_ref, bg_ref)
            au = _conv_chunk(_taps(pu_s, r0, R, HALO), wu_ref, bu_ref)
            s_ref[pl.ds(r0, R), :] = (ag * jax.nn.sigmoid(ag) * au).astype(BF16)

        _chunks(T, R, step)

    return pl.pallas_call(
        body, name="conv_glu_fwd", grid=(nb,),
        in_specs=[pair, gate(3), up(3), gate(1), up(1)],
        out_specs=pl.BlockSpec((T, CONV_COLS), lambda j: (0, j)),
        out_shape=jax.ShapeDtypeStruct((T, D_FF), BF16),
        scratch_shapes=[pltpu.VMEM((T + HALO, CONV_COLS), F32)] * 2,
        compiler_params=_params(("parallel",)),
    )(p3, conv_w, conv_w, conv_b, conv_b)


def _conv_glu_bwd(p3, ds, conv_w, conv_b):
    T = p3.shape[1]
    nb, pair, gate, up = _conv_specs(T)

    R = min(CONV_CHUNK, T)

    def body(p_ref, ds_ref, wg_ref, wu_ref, bg_ref, bu_ref, dp_ref, dwg_ref, dwu_ref, dbg_ref, dbu_ref, pg_s, pu_s, dag_s, dau_s):
        for half, scr in enumerate((pg_s, pu_s)):
            scr[0:HALO, :] = jnp.zeros((HALO, CONV_COLS), F32)
            scr[HALO:HALO + T, :] = p_ref[half].astype(F32)
        for scr in (dag_s, dau_s):
            scr[T:T + HALO, :] = jnp.zeros((HALO, CONV_COLS), F32)
        halves = ((pg_s, dag_s, wg_ref, dwg_ref, dbg_ref), (pu_s, dau_s, wu_ref, dwu_ref, dbu_ref))

        def step(r0, sums):
            taps = [_taps(p_s, r0, R, HALO) for p_s, *_ in halves]
            ag = _conv_chunk(taps[0], wg_ref, bg_ref)
            au = _conv_chunk(taps[1], wu_ref, bu_ref)
            sg = jax.nn.sigmoid(ag)
            d = ds_ref[pl.ds(r0, R), :].astype(F32)
            das = (d * au * (sg * (1.0 + ag * (1.0 - sg))), d * ag * sg)
            out = []
            for half, (_, da_s, *_) in enumerate(halves):
                da_s[pl.ds(r0, R), :] = das[half]
                out.append(tuple(sums[half][k] + _fold(das[half] * taps[half][k]) for k in range(3)) + (sums[half][3] + _fold(das[half]),))
            return tuple(out)

        sums = _chunks(T, R, step, ((jnp.zeros((8, CONV_COLS), F32),) * 4,) * 2)
        for half, (_, da_s, w_ref, dw_ref, db_ref) in enumerate(halves):
            for k in range(3):
                dw_ref[k:k + 1, :] = jnp.sum(sums[half][k], axis=0, keepdims=True)
            db_ref[...] = jnp.sum(sums[half][3], axis=0, keepdims=True)

            def transposed(r0, da_s=da_s, w_ref=w_ref, half=half):
                da0, da1, da2 = _taps(da_s, r0, R, 0)
                dp_ref[half, pl.ds(r0, R), :] = (w_ref[2:3, :] * da0 + w_ref[1:2, :] * da1 + w_ref[0:1, :] * da2).astype(BF16)

            _chunks(T, R, transposed)

    col = lambda r: pl.BlockSpec((r, CONV_COLS), lambda j: (0, j))
    return pl.pallas_call(
        body, name="conv_glu_bwd", grid=(nb,),
        in_specs=[pair, col(T), gate(3), up(3), gate(1), up(1)],
        out_specs=[pair, col(3), col(3), col(1), col(1)],
        out_shape=[jax.ShapeDtypeStruct((2, T, D_FF), BF16), jax.ShapeDtypeStruct((3, D_FF), F32), jax.ShapeDtypeStruct((3, D_FF), F32),
                   jax.ShapeDtypeStruct((1, D_FF), F32), jax.ShapeDtypeStruct((1, D_FF), F32)],
        scratch_shapes=[pltpu.VMEM((T + HALO, CONV_COLS), F32)] * 4,
        compiler_params=_params(("parallel",)),
    )(p3, ds, conv_w, conv_w, conv_b, conv_b)


def _ffn_down_loss(s, w_down, u1, ln1_g, ln1_b, ln2_g, ln2_b, target):
    T = u1.shape[0]
    tm = min(256, T)

    def body(s_ref, w_ref, u1_ref, g1_ref, b1_ref, g2_ref, b2_ref, t_ref, du_ref, dub_ref, dg_ref, db_ref, loss_ref):
        @pl.when(pl.program_id(0) == 0)
        def _():
            dg_ref[...] = jnp.zeros_like(dg_ref)
            db_ref[...] = jnp.zeros_like(db_ref)
            loss_ref[...] = jnp.zeros_like(loss_ref)

        xh1, _ = _ln_stats(u1_ref[...])
        x1 = xh1 * g1_ref[...] + b1_ref[...]
        u2 = ALPHA * x1 + _dot(s_ref[...], w_ref[...])
        xh2, rstd2 = _ln_stats(u2)
        err = xh2 * g2_ref[...] + b2_ref[...] - t_ref[...]
        per_token = jnp.mean(err * err, axis=-1, keepdims=True)
        loss_ref[...] += 0.5 * jnp.sum(per_token, axis=0, keepdims=True)
        dy = err * (1.0 / D_MODEL)
        dg_ref[...] += jnp.sum(dy * xh2, axis=0, keepdims=True)
        db_ref[...] += jnp.sum(dy, axis=0, keepdims=True)
        du2 = _ln_bwd(dy, xh2, rstd2, g2_ref[...])
        du_ref[...] = du2
        dub_ref[...] = du2.astype(BF16)

    row = lambda n: pl.BlockSpec((tm, n), lambda i: (i, 0))
    const = lambda r, n: pl.BlockSpec((r, n), lambda i: (0, 0))
    vec = const(1, D_MODEL)
    return pl.pallas_call(
        body, name="ffn_down_loss", grid=(T // tm,),
        in_specs=[row(D_FF), const(D_FF, D_MODEL), row(D_MODEL), vec, vec, vec, vec, row(D_MODEL)],
        out_specs=[row(D_MODEL), row(D_MODEL), vec, vec, const(1, LANES)],
        out_shape=[jax.ShapeDtypeStruct((T, D_MODEL), F32), jax.ShapeDtypeStruct((T, D_MODEL), BF16), jax.ShapeDtypeStruct((1, D_MODEL), F32),
                   jax.ShapeDtypeStruct((1, D_MODEL), F32), jax.ShapeDtypeStruct((1, LANES), F32)],
        compiler_params=_params(("arbitrary",)),
    )(s, w_down, u1, ln1_g, ln1_b, ln2_g, ln2_b, target)


def _ffn_up_bwd_ln1(dp3, w_up, du2, u1, ln1_g):
    T = u1.shape[0]
    tm = min(256, T)

    def body(dp_ref, w_ref, du2_ref, u1_ref, g_ref, du_ref, dub_ref, dg_ref, db_ref):
        @pl.when(pl.program_id(0) == 0)
        def _():
            dg_ref[...] = jnp.zeros_like(dg_ref)
            db_ref[...] = jnp.zeros_like(db_ref)

        dx1 = _dot_nt(dp_ref[0], w_ref[:, :D_FF]) + _dot_nt(dp_ref[1], w_ref[:, D_FF:]) + ALPHA * du2_ref[...]
        xh, rstd = _ln_stats(u1_ref[...])
        dg_ref[...] += jnp.sum(dx1 * xh, axis=0, keepdims=True)
        db_ref[...] += jnp.sum(dx1, axis=0, keepdims=True)
        du1 = _ln_bwd(dx1, xh, rstd, g_ref[...])
        du_ref[...] = du1
        dub_ref[...] = du1.astype(BF16)

    row = lambda n: pl.BlockSpec((tm, n), lambda i: (i, 0))
    const = lambda r, n: pl.BlockSpec((r, n), lambda i: (0, 0))
    vec = const(1, D_MODEL)
    return pl.pallas_call(
        body, name="ffn_up_bwd_ln1", grid=(T // tm,),
        in_specs=[pl.BlockSpec((2, tm, D_FF), lambda i: (0, i, 0)), const(D_MODEL, 2 * D_FF), row(D_MODEL), row(D_MODEL), vec],
        out_specs=[row(D_MODEL), row(D_MODEL), vec, vec],
        out_shape=[jax.ShapeDtypeStruct((T, D_MODEL), F32), jax.ShapeDtypeStruct((T, D_MODEL), BF16), jax.ShapeDtypeStruct((1, D_MODEL), F32),
                   jax.ShapeDtypeStruct((1, D_MODEL), F32)],
        compiler_params=_params(("arbitrary",)),
    )(dp3, w_up, du2, u1, ln1_g)


def _local_step(x, positions, w_in, b_gate, sinks, ln1_g, ln1_b, conv_b, ln2_g, ln2_b, target, later_weights,
                early_exchange=None, mid_exchange=None, tail_exchange=None):
    T = x.shape[0]
    inv_freq = 1.0 / (ROPE_THETA ** (jnp.arange(0, HEAD_DIM, 2, dtype=F32) / HEAD_DIM))
    cos, sin = _rope_tables(positions.reshape(T, 1), jnp.tile(inv_freq, LANES // (HEAD_DIM // 2)).reshape(1, LANES))

    xb, qa, ka, va, qb, kb, vb, gl = _in_proj(x, w_in)
    ya = _swa_fwd(qa, ka, va, cos, sin, sinks)
    if isinstance(later_weights, tuple):
        exchange, finish = later_weights
        (yb, probs, betas), arrived = _sb_fwd(qb, kb, vb, exchange)
        later_weights = finish(arrived)
    else:
        (yb, probs, betas), _ = _sb_fwd(qb, kb, vb)
    wa, wb, wo, w_up, conv_w, w_down = later_weights
    h, u1, x1 = _mix_fwd(ya, yb, gl, x, wa, wb, wo, b_gate, ln1_g, ln1_b)

    ff_tn = D_FF // 2
    nff = D_FF // ff_tn
    tm = min(512, T)
    p3 = _matmul(x1, w_up, kind="nn", name="ffn_up", grid=(T // tm, 2 * nff),
                 a_spec=pl.BlockSpec((tm, D_MODEL), lambda i, j: (i, 0)), b_spec=pl.BlockSpec((D_MODEL, ff_tn), lambda i, j: (0, j)),
                 out_spec=pl.BlockSpec((None, tm, ff_tn), lambda i, j: (j // nff, i, j % nff)),
                 out_shape=jax.ShapeDtypeStruct((2, T, D_FF), ACT_DTYPE))
    s = _conv_glu_fwd(p3, conv_w, conv_b)
    du2, du2b, dln2_g, dln2_b, loss = _ffn_down_loss(s, w_down, u1, ln1_g, ln1_b, ln2_g, ln2_b, target)

    ds = _matmul(du2b, w_down, kind="nt", name="ffn_down_bwd", grid=(T // tm, nff),
                 a_spec=pl.BlockSpec((tm, D_MODEL), lambda i, j: (i, 0)), b_spec=pl.BlockSpec((ff_tn, D_MODEL), lambda i, j: (j, 0)),
                 out_spec=pl.BlockSpec((tm, ff_tn), lambda i, j: (i, j)), out_shape=jax.ShapeDtypeStruct((T, D_FF), ACT_DTYPE))
    dp3, dcw_g, dcw_u, dcb_g, dcb_u = _conv_glu_bwd(p3, ds, conv_w, conv_b)
    tk = 256
    dw_down = _matmul(s, du2b, kind="tn", name="dw_down", grid=(D_FF // tk,),
                      a_spec=pl.BlockSpec((T, tk), lambda i: (0, i)), b_spec=pl.BlockSpec((T, D_MODEL), lambda i: (0, 0)),
                      out_spec=pl.BlockSpec((tk, D_MODEL), lambda i: (i, 0)), out_shape=jax.ShapeDtypeStruct((D_FF, D_MODEL), BF16))
    dw_up = _matmul(x1, dp3, kind="tn", name="dw_up", grid=(D_MODEL // 512, 2 * nff),
                    a_spec=pl.BlockSpec((T, 512), lambda i, j: (0, i)), b_spec=pl.BlockSpec((None, T, ff_tn), lambda i, j: (j // nff, 0, j % nff)),
                    out_spec=pl.BlockSpec((512, ff_tn), lambda i, j: (i, j)), out_shape=jax.ShapeDtypeStruct((D_MODEL, 2 * D_FF), BF16))
    du1, du1b, dln1_g, dln1_b = _ffn_up_bwd_ln1(dp3, w_up, du2, u1, ln1_g)
    dya, dyb, dgl, dta, dtb, db_gate = _mix_bwd(du1, ya, yb, gl, wa, wb, wo, b_gate)

    def dw_tn(a, g, name):
        rows, cols = a.shape[1], g.shape[1]
        tn = min(512, cols)
        return _matmul(a, g, kind="tn", name=name, grid=(rows // 512, cols // tn),
                       a_spec=pl.BlockSpec((T, 512), lambda i, j: (0, i)), b_spec=pl.BlockSpec((T, tn), lambda i, j: (0, j)),
                       out_spec=pl.BlockSpec((512, tn), lambda i, j: (i, j)), out_shape=jax.ShapeDtypeStruct((rows, cols), BF16))

    dwa = dw_tn(ya, dta, "dw_branch_a")
    dwb = dw_tn(yb, dtb, "dw_branch_b")
    dwo = dw_tn(h, du1b, "dw_out")

    grads = dict(
        b_gate=db_gate, w_branch_a=dwa, w_branch_b=dwb, w_out=dwo, ln1_g=dln1_g, ln1_b=dln1_b,
        w_up=dw_up, conv_w=jnp.concatenate([dcw_g, dcw_u], axis=1), conv_b=(dcb_g, dcb_u), w_down=dw_down, ln2_g=dln2_g, ln2_b=dln2_b)
    (dqb, dkb, dvb), early_out = _sb_bwd(qb, kb, vb, probs, betas, dyb, early_exchange(grads) if early_exchange else None)
    dqa, dka, dva, grads["sinks"] = _swa_bwd(qa, ka, va, cos, sin, sinks, dya)
    dproj = (dqa, dka, dva, dqb, dkb, dvb, dgl)
    first_half, _ = _dw_in(xb, dproj, 0)
    second_half, mid_out = _dw_in(xb, dproj, 1, mid_exchange(first_half) if mid_exchange else None)
    grads["w_in"] = (first_half, second_half)
    grad_x, tail_out = _grad_x(dproj, w_in, du1, tail_exchange(grads, loss) if tail_exchange else None)
    return loss, grad_x, grads, early_out, mid_out, tail_out


def _dw_in(xb, dproj, half, exchange=None):
    T = xb.shape[0]
    rows = D_MODEL // 2
    tn = 2 * LANES
    groups, start, k = [], 0, 0
    while k < len(IN_WIDTHS):
        if IN_WIDTHS[k] >= tn:
            groups.append((start, IN_WIDTHS[k] // tn, [(k, 0, tn)]))
            k += 1
        else:
            members, off = [], 0
            while off < tn:
                members.append((k, off, IN_WIDTHS[k]))
                off += IN_WIDTHS[k]
                k += 1
            groups.append((start, 1, members))
        start += groups[-1][1]

    def body(x_ref, *refs):
        pieces, o_ref = refs[:-1], refs[-1]
        j = pl.program_id(0)
        for first, steps, members in groups:
            @pl.when((j >= first) & (j < first + steps))
            def _(members=members):
                for k, off, width in members:
                    o_ref[:, off:off + width] = _dot_tn(x_ref[...], pieces[k][...]).astype(o_ref.dtype)

    specs = [None] * len(IN_WIDTHS)
    for first, steps, members in groups:
        for k, _, width in members:
            specs[k] = pl.BlockSpec((T, width), lambda j, first=first, steps=steps: (0, jnp.clip(j - first, 0, steps - 1)))
    (dw,), arrived = _hosted_call(
        body, f"dw_in_{half}", (IN_TOTAL // tn,), exchange,
        in_specs=[pl.BlockSpec((T, rows), lambda j: (0, half))] + specs, out_specs=[pl.BlockSpec((rows, tn), lambda j: (0, j))],
        out_shape=[jax.ShapeDtypeStruct((rows, IN_TOTAL), BF16)], semantics=("arbitrary",), args=(xb, *dproj))
    return dw, arrived


def _grad_x(dproj, w_in, du1, exchange=None):
    T = du1.shape[0]
    tm = min(256, T)
    offs = np.cumsum((0,) + IN_WIDTHS)

    def body(*refs):
        pieces, (w_ref, du_ref, o_ref) = refs[:len(IN_WIDTHS)], refs[len(IN_WIDTHS):]
        acc = ALPHA * du_ref[...]
        for p_ref, a, b in zip(pieces, offs[:-1], offs[1:]):
            acc = acc + _dot_nt(p_ref[...].astype(BF16), w_ref[:, a:b])
        o_ref[...] = acc

    row = lambda n: pl.BlockSpec((tm, n), lambda i: (i, 0))
    (grad_x,), arrived = _hosted_call(
        body, "grad_x", (T // tm,), exchange,
        in_specs=[row(n) for n in IN_WIDTHS] + [pl.BlockSpec((D_MODEL, IN_TOTAL), lambda i: (0, 0)), row(D_MODEL)],
        out_specs=[row(D_MODEL)], out_shape=[jax.ShapeDtypeStruct((T, D_MODEL), F32)], semantics=("parallel",),
        args=(*dproj, w_in, du1))
    return grad_x, arrived


ANY = pl.BlockSpec(memory_space=pl.ANY)


def _all_gather(slabs, name):
    n = len(slabs)

    def body(*refs):
        ins, outs = refs[:n], refs[n:2 * n]
        send_sems, recv_sems, local_sems = refs[2 * n:]
        x, y, c = lax.axis_index("x"), lax.axis_index("y"), lax.axis_index("c")
        me, sibling = (x, y, c), (x, y, 1 - c)
        chips = [(1 - x, y), (x, 1 - y), (1 - x, 1 - y)]

        def slot(pos):
            return 4 * pos[0] + 2 * pos[1] + pos[2]

        def copy(a, k, block, to, from_input=False):
            return pltpu.make_async_remote_copy(
                src_ref=ins[a] if from_input else outs[a].at[slot(block)], dst_ref=outs[a].at[slot(block)],
                send_sem=send_sems.at[a, k], recv_sem=recv_sems.at[a, k], device_id=to, device_id_type=MESH)

        mine = [pltpu.make_async_copy(ins[a], outs[a].at[slot(me)], local_sems.at[a]) for a in range(n)]
        for cp in mine:
            cp.start()
        first = []
        for a in range(n):
            first.append(copy(a, 0, me, sibling, from_input=True))
            first += [copy(a, 1 + j, me, (*chip, c), from_input=True) for j, chip in enumerate(chips)]
        for cp in first:
            cp.start()
        passed = []
        for j, chip in enumerate(chips):
            for a in range(n):
                copy(a, 1 + j, (*chip, c), me).wait_recv()
                fwd = copy(a, 4 + j, (*chip, c), sibling)
                fwd.start()
                passed.append(fwd)
        for a in range(n):
            copy(a, 0, sibling, me).wait_recv()
            for j, chip in enumerate(chips):
                copy(a, 4 + j, (*chip, 1 - c), me).wait_recv()
        for cp in first + passed:
            cp.wait_send()
        for cp in mine:
            cp.wait()

    return pl.pallas_call(
        body, name=name,
        in_specs=[ANY] * n, out_specs=[ANY] * n,
        out_shape=[jax.ShapeDtypeStruct((N_DEV,) + s.shape, s.dtype) for s in slabs],
        scratch_shapes=[pltpu.SemaphoreType.DMA((n, 7)), pltpu.SemaphoreType.DMA((n, 7)), pltpu.SemaphoreType.DMA((n,))],
    )(*slabs)


def _all_to_all(slabs, name):
    n = len(slabs)

    def body(*refs):
        ins, outs = refs[:n], refs[n:2 * n]
        send_sems, recv_sems, local_sems = refs[2 * n:]
        x, y, c = lax.axis_index("x"), lax.axis_index("y"), lax.axis_index("c")
        my_slot = 4 * x + 2 * y + c
        flips = [(fx, fy, fc) for fx in (0, 1) for fy in (0, 1) for fc in (0, 1) if (fx, fy, fc) != (0, 0, 0)]

        def copy(a, k):
            fx, fy, fc = flips[k]
            peer = (x ^ fx, y ^ fy, c ^ fc)
            peer_slot = 4 * peer[0] + 2 * peer[1] + peer[2]
            send = pltpu.make_async_remote_copy(src_ref=ins[a].at[peer_slot], dst_ref=outs[a].at[my_slot], send_sem=send_sems.at[a, k],
                                                recv_sem=recv_sems.at[a, k], device_id=peer, device_id_type=MESH)
            recv = pltpu.make_async_remote_copy(src_ref=ins[a].at[peer_slot], dst_ref=outs[a].at[peer_slot], send_sem=send_sems.at[a, k],
                                                recv_sem=recv_sems.at[a, k], device_id=peer, device_id_type=MESH)
            return send, recv

        mine = [pltpu.make_async_copy(ins[a].at[my_slot], outs[a].at[my_slot], local_sems.at[a]) for a in range(n)]
        for cp in mine:
            cp.start()
        copies = [copy(a, k) for a in range(n) for k in range(len(flips))]
        for send, _ in copies:
            send.start()
        for send, recv in copies:
            recv.wait_recv()
            send.wait_send()
        for cp in mine:
            cp.wait()

    return pl.pallas_call(
        body, name=name,
        in_specs=[ANY] * n, out_specs=[ANY] * n,
        out_shape=[jax.ShapeDtypeStruct(s.shape, s.dtype) for s in slabs],
        scratch_shapes=[pltpu.SemaphoreType.DMA((n, 7)), pltpu.SemaphoreType.DMA((n, 7)), pltpu.SemaphoreType.DMA((n,))],
    )(*slabs)


def _row_tile(rows):
    for cand in range(256, 7, -8):
        if rows % cand == 0:
            return cand
    return rows


def _window(w):
    wp = max(-(-((w * r) % LANES + w) // LANES) for r in range(N_DEV)) * LANES
    assert all((w * r) // LANES * LANES + wp <= N_DEV * w for r in range(N_DEV))
    return wp


def _join_cols(slabs, name):
    _, R, w = slabs.shape
    tr = _row_tile(R)
    wp = _window(w)

    def body(g_ref, o_ref, pad_ref):
        if w % LANES == 0:
            for r in range(N_DEV):
                o_ref[:, w * r:w * (r + 1)] = g_ref[r]
            return
        o_ref[...] = jnp.zeros_like(o_ref)
        pad_ref[...] = jnp.zeros_like(pad_ref)
        for r in range(N_DEV):
            q, s = divmod(w * r, LANES)
            pad_ref[:, :w] = g_ref[r]
            y = pad_ref[...]
            if s:
                y = pltpu.roll(y, s, axis=1)
            o_ref[:, LANES * q:LANES * q + wp] += y

    return pl.pallas_call(
        body, name=name, grid=(R // tr,),
        in_specs=[pl.BlockSpec((N_DEV, tr, w), lambda i: (0, i, 0))], out_specs=pl.BlockSpec((tr, N_DEV * w), lambda i: (i, 0)),
        out_shape=jax.ShapeDtypeStruct((R, N_DEV * w), slabs.dtype), scratch_shapes=[pltpu.VMEM((tr, wp), slabs.dtype)],
        compiler_params=_params(("parallel",)),
    )(slabs)


def _split_cols(pieces, name):
    R = pieces[0].shape[0]
    widths = [p.shape[1] for p in pieces]
    total = sum(widths)
    w = total // N_DEV
    tr = _row_tile(R)
    wp = _window(w)
    offs = np.cumsum([0] + widths)
    dtype = pieces[0].dtype

    def body(*refs):
        ins, (o_ref, full_ref) = refs[:len(pieces)], refs[len(pieces):]
        for p_ref, a, b in zip(ins, offs[:-1], offs[1:]):
            full_ref[:, a:b] = p_ref[...].astype(dtype)
        for r in range(N_DEV):
            q, s = divmod(w * r, LANES)
            y = full_ref[:, LANES * q:LANES * q + wp]
            if s:
                y = pltpu.roll(y, wp - s, axis=1)
            o_ref[r] = y[:, :w]

    return pl.pallas_call(
        body, name=name, grid=(R // tr,),
        in_specs=[pl.BlockSpec((tr, n), lambda i: (i, 0)) for n in widths], out_specs=pl.BlockSpec((N_DEV, tr, w), lambda i: (0, i, 0)),
        out_shape=jax.ShapeDtypeStruct((N_DEV, R, w), dtype), scratch_shapes=[pltpu.VMEM((tr, total), dtype)],
        compiler_params=_params(("parallel",)),
    )(*pieces)


def _adamw(g, w, m, v):
    m_new = ADAM_B1 * m + (1.0 - ADAM_B1) * g
    v_new = ADAM_B2 * v + (1.0 - ADAM_B2) * jnp.square(g)
    m_hat = m_new / (1.0 - ADAM_B1 ** ADAM_STEP)
    v_hat = v_new / (1.0 - ADAM_B2 ** ADAM_STEP)
    return -ADAM_LR * (m_hat / (jnp.sqrt(v_hat) + ADAM_EPS) + ADAM_WD * w), m_new, v_new


def _sum_parts(p_ref):
    g = p_ref[0].astype(F32)
    for d in range(1, N_DEV):
        g = g + p_ref[d].astype(F32)
    return g


def _reduce_adamw(parts, w, m, v, name):
    parts = parts if isinstance(parts, tuple) else (parts,)
    R, C = w.shape
    tr = _row_tile(parts[0].shape[1])
    firsts = np.cumsum([0] + [p.shape[1] // tr for p in parts])

    def body(*refs):
        p_refs, (w_ref, m_ref, v_ref, g_ref, d_ref, mo_ref, vo_ref) = refs[:len(parts)], refs[len(parts):]
        i = pl.program_id(0)
        for k, p_ref in enumerate(p_refs):
            @pl.when((i >= firsts[k]) & (i < firsts[k + 1]))
            def _(p_ref=p_ref):
                g = _sum_parts(p_ref)
                g_ref[...] = g
                d_ref[...], mo_ref[...], vo_ref[...] = _adamw(g, w_ref[...], m_ref[...], v_ref[...])

    row = pl.BlockSpec((tr, C), lambda i: (i, 0))
    part_specs = [pl.BlockSpec((N_DEV, tr, C), lambda i, a=int(firsts[k]), n=int(firsts[k + 1] - firsts[k]): (0, jnp.clip(i - a, 0, n - 1), 0))
                  for k in range(len(parts))]
    return pl.pallas_call(
        body, name=name, grid=(R // tr,),
        in_specs=part_specs + [row, row, row],
        out_specs=[row] * 4, out_shape=[jax.ShapeDtypeStruct((R, C), F32)] * 4,
        compiler_params=_params(("arbitrary",)),
    )(*parts, w, m, v)


def _reduce_adamw_small(parts, ws, ms, vs):
    sizes = [a.shape[1] for a in ws]
    k = len(sizes)
    offs = np.cumsum([0] + [-(-n // LANES) * LANES for n in sizes])

    def body(*refs):
        p_ref, w_refs, m_refs, v_refs = refs[0], refs[1:1 + k], refs[1 + k:1 + 2 * k], refs[1 + 2 * k:1 + 3 * k]
        outs, loss_ref = refs[1 + 3 * k:-1], refs[-1]
        g_all = _sum_parts(p_ref)
        for j, n in enumerate(sizes):
            g = g_all[:, offs[j]:offs[j] + LANES * (-(-n // LANES))][:, :n]
            outs[4 * j][...] = g
            outs[4 * j + 1][...], outs[4 * j + 2][...], outs[4 * j + 3][...] = _adamw(g, w_refs[j][...], m_refs[j][...], v_refs[j][...])
        loss_ref[...] = g_all[:, offs[k]:offs[k] + LANES]

    vm = pl.BlockSpec(memory_space=pltpu.VMEM)
    out_shape = [jax.ShapeDtypeStruct((1, n), F32) for n in sizes for _ in range(4)] + [jax.ShapeDtypeStruct((1, LANES), F32)]
    res = pl.pallas_call(
        body, name="reduce_adamw_replicated", in_specs=[vm] * (1 + 3 * k), out_specs=[vm] * len(out_shape), out_shape=out_shape,
        compiler_params=_params(),
    )(parts, *ws, *ms, *vs)
    return [res[4 * j:4 * j + 4] for j in range(k)], res[-1]


COL_SHARDED = ("w_in", "w_branch_a", "w_branch_b", "w_up", "conv_w")
ROW_SHARDED = ("w_out", "w_down")
SMALL = ("b_gate", "sinks", "ln1_g", "ln1_b", "conv_b", "ln2_g", "ln2_b")
ORDER = ("w_in", "b_gate", "sinks", "w_branch_a", "w_branch_b", "w_out", "ln1_g", "ln1_b", "w_up", "conv_w", "conv_b", "w_down", "ln2_g", "ln2_b")


def _pad_lanes(a):
    pad = (-a.shape[-1]) % LANES
    return a if pad == 0 else jnp.pad(a, ((0, 0), (0, pad)))


def kernel(x, positions, w_in, b_gate, sinks, w_branch_a, w_branch_b, w_out, ln1_g, ln1_b, w_up, conv_w, conv_b, w_down, ln2_g, ln2_b, loss_target, m_w_in, m_b_gate, m_sinks, m_w_branch_a, m_w_branch_b, m_w_out, m_ln1_g, m_ln1_b, m_w_up, m_conv_w, m_conv_b, m_w_down, m_ln2_g, m_ln2_b, v_w_in, v_b_gate, v_sinks, v_w_branch_a, v_w_branch_b, v_w_out, v_ln1_g, v_ln1_b, v_w_up, v_conv_w, v_conv_b, v_w_down, v_ln2_g, v_ln2_b):
    args = dict(locals())
    sharded = COL_SHARDED + ROW_SHARDED
    w = {n: args[n][0] if n in sharded else args[n] for n in ORDER}
    m = {n: args["m_" + n][0] if n in sharded else args["m_" + n] for n in ORDER}
    v = {n: args["v_" + n][0] if n in sharded else args["v_" + n] for n in ORDER}

    travel = {n: (w[n] if n == "conv_w" else w[n].astype(BF16)) for n in sharded}
    (g_in,) = _all_gather([travel["w_in"]], "all_gather_w_in")
    w_in_full = _join_cols(g_in, "join_w_in")
    later = ("w_branch_a", "w_branch_b", "w_out", "w_up", "conv_w", "w_down")

    def join(name, slabs):
        return _join_cols(slabs, "join_" + name) if name in COL_SHARDED else slabs.reshape(-1, slabs.shape[-1])

    def split(name, grad):
        if name in COL_SHARDED:
            return _split_cols(grad if isinstance(grad, tuple) else (grad,), "split_d" + name)
        return grad.reshape((N_DEV, -1, grad.shape[-1]))

    def early_exchange(grads):
        return _Exchange([split(n, grads[n]) for n in later], ["scatter"] * len(later))

    def mid_exchange(first_half):
        return _Exchange([_split_cols((first_half,), "split_dw_in_0")], ["scatter"])

    def tail_exchange(grads, loss):
        small_pack = jnp.concatenate(
            [_pad_lanes(p) for n in SMALL for p in (grads[n] if isinstance(grads[n], tuple) else (grads[n],))] + [loss], axis=1)
        return _Exchange([_split_cols((grads["w_in"][1],), "split_dw_in_1"), small_pack], ["scatter", "gather"])

    gather_later = _Exchange([travel[n] for n in later], ["gather"] * len(later))
    _, grad_x, _, early_out, (recv_w_in_0,), (recv_w_in_1, small_parts) = _local_step(
        x[0], positions[0], w_in_full, w["b_gate"], w["sinks"][0], w["ln1_g"], w["ln1_b"], w["conv_b"], w["ln2_g"], w["ln2_b"], loss_target[0],
        (gather_later, lambda arrived: [join(n, a) for n, a in zip(later, arrived)]), early_exchange, mid_exchange, tail_exchange)
    recv = dict(zip(later, early_out), w_in=(recv_w_in_0, recv_w_in_1))

    res = {n: _reduce_adamw(recv[n], w[n], m[n], v[n], "reduce_adamw_" + n) for n in sharded}
    small_res, loss_sum = _reduce_adamw_small(small_parts, [w[n] for n in SMALL], [m[n] for n in SMALL], [v[n] for n in SMALL])
    res.update(zip(SMALL, small_res))
    out = [loss_sum[0, 0], grad_x[None]]
    for k in range(4):
        out += [res[n][k][None] if n in sharded else res[n][k] for n in ORDER]
    return tuple(out)
```

```python
import functools

import jax
import jax.numpy as jnp
import numpy as np
from jax import lax
from jax.experimental import pallas as pl
from jax.experimental.pallas import tpu as pltpu

D_MODEL = 1024
HEAD_DIM = 64
SWA_Q_HEADS = 8
SWA_KV_HEADS = 2
SB_HEADS = 8
WINDOW = 128
ROPE_THETA = 10000.0
D_FF = 2816
LN_EPS = 1e-5
DEPTH = 1
ALPHA = (2.0 * DEPTH) ** 0.25
SWA_Q_WIDTH = SWA_Q_HEADS * HEAD_DIM
SWA_KV_WIDTH = SWA_KV_HEADS * HEAD_DIM
SB_WIDTH = SB_HEADS * HEAD_DIM
GATE_WIDTH = 2 * D_MODEL
IN_WIDTHS = (SWA_Q_WIDTH, SWA_KV_WIDTH, SWA_KV_WIDTH, SB_WIDTH, SB_WIDTH, SB_WIDTH, GATE_WIDTH)
IN_TOTAL = sum(IN_WIDTHS)
ATTN_SCALE = HEAD_DIM ** -0.5

ADAM_LR = 0.001
ADAM_B1 = 0.9
ADAM_B2 = 0.999
ADAM_EPS = 1e-08
ADAM_WD = 0.01
ADAM_STEP = 10

N_DEV = 8
LANES = 128
SB_BLOCK = 256
SB_PAIRS = 4
VMEM_LIMIT = 56 * 1024 * 1024

F32 = jnp.float32
BF16 = jnp.bfloat16
ACT_DTYPE = BF16
MESH = pl.DeviceIdType.MESH


def _params(sem=None):
    return pltpu.CompilerParams(dimension_semantics=sem, vmem_limit_bytes=VMEM_LIMIT)


def _dot(a, b):
    return jnp.dot(a, b, preferred_element_type=F32)


def _dot_nt(a, b):
    return lax.dot_general(a, b, (((1,), (1,)), ((), ())), preferred_element_type=F32)


def _dot_tn(a, b):
    return lax.dot_general(a, b, (((0,), (0,)), ((), ())), preferred_element_type=F32)


def _split_bf16(v):
    hi = v.astype(BF16)
    lo = (v - hi.astype(F32)).astype(BF16)
    return hi, lo


def _matmul(a, b, *, kind, out_shape, grid, a_spec, b_spec, out_spec, name, add=None, add_spec=None, add_scale=1.0):
    dot = {"nn": _dot, "nt": _dot_nt, "tn": _dot_tn}[kind]

    def body(*refs):
        if add is None:
            a_ref, b_ref, o_ref = refs
        else:
            a_ref, b_ref, add_ref, o_ref = refs
        r = dot(a_ref[...].astype(BF16), b_ref[...].astype(BF16))
        if add is not None:
            r = r + add_scale * add_ref[...]
        o_ref[...] = r.astype(o_ref.dtype)

    ins = [a, b] + ([] if add is None else [add])
    specs = [a_spec, b_spec] + ([] if add is None else [add_spec])
    return pl.pallas_call(
        body, name=name, grid=grid, in_specs=specs, out_specs=out_spec, out_shape=out_shape,
        compiler_params=_params(("parallel",) * len(grid)),
    )(*ins)


def _rope_tables(pos_col, inv_freq_lanes):
    T = pos_col.shape[0]
    tm = min(512, T)

    def body(pos_ref, f_ref, cos_ref, sin_ref):
        ang = pos_ref[...].astype(F32) * f_ref[...]
        cos_ref[...] = jnp.cos(ang)
        sin_ref[...] = jnp.sin(ang)

    return pl.pallas_call(
        body, name="rope_tables", grid=(T // tm,),
        in_specs=[pl.BlockSpec((tm, 1), lambda i: (i, 0)), pl.BlockSpec((1, LANES), lambda i: (0, 0))],
        out_specs=[pl.BlockSpec((tm, LANES), lambda i: (i, 0))] * 2,
        out_shape=[jax.ShapeDtypeStruct((T, LANES), F32)] * 2,
        compiler_params=_params(("parallel",)),
    )(pos_col, inv_freq_lanes)


def _lane_iota(shape):
    return lax.broadcasted_iota(jnp.int32, shape, len(shape) - 1)


def _rot_half(t):
    first = (_lane_iota(t.shape) % HEAD_DIM) < (HEAD_DIM // 2)
    return jnp.where(first, -pltpu.roll(t, LANES - HEAD_DIM // 2, axis=1), pltpu.roll(t, HEAD_DIM // 2, axis=1))


def _rope(t, cos, sin):
    return t * cos + _rot_half(t) * sin


def _rope_transpose(d, cos, sin):
    return d * cos - _rot_half(d * sin)


_IN_DTYPES = (F32, F32, BF16, BF16, BF16, BF16, F32)


def _in_proj(x, w_in_b):
    T = x.shape[0]
    tm = min(256, T)
    offs = np.cumsum((0,) + IN_WIDTHS)

    def body(x_ref, w_ref, xb_ref, *outs):
        xb = x_ref[...].astype(BF16)
        xb_ref[...] = xb
        for o_ref, a, b in zip(outs, offs[:-1], offs[1:]):
            o_ref[...] = _dot(xb, w_ref[:, a:b]).astype(o_ref.dtype)

    row = lambda n: pl.BlockSpec((tm, n), lambda i: (i, 0))
    return pl.pallas_call(
        body, name="in_proj", grid=(T // tm,),
        in_specs=[row(D_MODEL), pl.BlockSpec((D_MODEL, IN_TOTAL), lambda i: (0, 0))],
        out_specs=[row(D_MODEL)] + [row(n) for n in IN_WIDTHS],
        out_shape=[jax.ShapeDtypeStruct((T, D_MODEL), BF16)] + [jax.ShapeDtypeStruct((T, n), dt) for n, dt in zip(IN_WIDTHS, _IN_DTYPES)],
        compiler_params=_params(("parallel",)),
    )(x, w_in_b)


def _swa_specs(T):
    blk = WINDOW
    cur = lambda n: pl.BlockSpec((blk, n), lambda i: (i, 0))
    prev = lambda n: pl.BlockSpec((blk, n), lambda i: (jnp.maximum(i - 1, 0), 0))
    return blk, cur, prev


SWA_GROUP = SWA_Q_HEADS // SWA_KV_HEADS


def _swa_stack(pairs):
    lane = _lane_iota(pairs[0].shape)
    zero = jnp.zeros((), pairs[0].dtype)
    rows = []
    for h in range(SWA_Q_HEADS):
        hh, g = h % 2, h // SWA_GROUP
        x = jnp.where((lane >= hh * HEAD_DIM) & (lane < (hh + 1) * HEAD_DIM), pairs[h // 2], zero)
        rows.append(x if hh == g else pltpu.roll(x, HEAD_DIM, axis=1))
    return jnp.concatenate(rows, axis=0)


def _swa_unstack(stacked, blk):
    low = _lane_iota((blk, LANES)) < HEAD_DIM
    pairs = []
    for pp in range(SWA_Q_HEADS // 2):
        halves = []
        for hh in range(2):
            h = 2 * pp + hh
            x = stacked[h * blk:(h + 1) * blk]
            halves.append(x if hh == h // SWA_GROUP else pltpu.roll(x, HEAD_DIM, axis=1))
        pairs.append(jnp.where(low, halves[0], halves[1]))
    return pairs


def _swa_probs(i, q_stack, kwin, sink_ref, blk):
    r = lax.broadcasted_iota(jnp.int32, (blk, 2 * blk), 0)
    c = lax.broadcasted_iota(jnp.int32, (blk, 2 * blk), 1)
    rel = blk + r - c
    valid = (rel >= 0) & (rel < WINDOW) & ((c >= blk) | (i > 0))
    bias = jnp.concatenate([jnp.where(valid, 0.0, -1e30)] * SWA_Q_HEADS, axis=0)
    head = lax.broadcasted_iota(jnp.int32, (SWA_Q_HEADS * blk, 1), 0) // blk
    sink = jnp.zeros((SWA_Q_HEADS * blk, 1), F32)
    for h in range(SWA_Q_HEADS):
        sink = jnp.where(head == h, sink_ref[h], sink)
    s = _dot_nt(q_stack, kwin) * ATTN_SCALE + bias
    m = jnp.maximum(jnp.max(s, axis=1, keepdims=True), sink)
    p = jnp.exp(s - m)
    es = jnp.exp(sink - m)
    den = jnp.sum(p, axis=1, keepdims=True) + es
    return p / den, es / den


def _swa_inputs(q_ref, kp_ref, kc_ref, vp_ref, vc_ref, cp_ref, cc_ref, sp_ref, sc_ref):
    cc, sc = cc_ref[...], sc_ref[...]
    kwin = jnp.concatenate([_rope(kp_ref[...], cp_ref[...], sp_ref[...]), _rope(kc_ref[...], cc, sc)], axis=0).astype(BF16)
    vwin = jnp.concatenate([vp_ref[...], vc_ref[...]], axis=0)
    q_stack = _swa_stack([_rope(q_ref[:, pp * LANES:(pp + 1) * LANES], cc, sc) for pp in range(SWA_Q_HEADS // 2)]).astype(BF16)
    return q_stack, kwin, vwin


def _swa_fwd(qa, ka, va, cos, sin, sinks):
    T = qa.shape[0]
    blk, cur, prev = _swa_specs(T)

    def body(sink_ref, q_ref, kp_ref, kc_ref, vp_ref, vc_ref, cp_ref, cc_ref, sp_ref, sc_ref, o_ref):
        q_stack, kwin, vwin = _swa_inputs(q_ref, kp_ref, kc_ref, vp_ref, vc_ref, cp_ref, cc_ref, sp_ref, sc_ref)
        probs, _ = _swa_probs(pl.program_id(0), q_stack, kwin, sink_ref, blk)
        for pp, tile in enumerate(_swa_unstack(_dot(probs.astype(BF16), vwin), blk)):
            o_ref[:, pp * LANES:(pp + 1) * LANES] = tile.astype(o_ref.dtype)

    return pl.pallas_call(
        body, name="swa_fwd", grid=(T // blk,),
        in_specs=[pl.BlockSpec(memory_space=pltpu.SMEM), cur(SWA_Q_WIDTH), prev(LANES), cur(LANES), prev(LANES), cur(LANES),
                  prev(LANES), cur(LANES), prev(LANES), cur(LANES)],
        out_specs=cur(SWA_Q_WIDTH),
        out_shape=jax.ShapeDtypeStruct((T, SWA_Q_WIDTH), BF16),
        compiler_params=_params(("parallel",)),
    )(sinks, qa, ka, ka, va, va, cos, cos, sin, sin)


def _swa_bwd(qa, ka, va, cos, sin, sinks, dya):
    T = qa.shape[0]
    blk, cur, prev = _swa_specs(T)
    full = lambda n: pl.BlockSpec((T, n), lambda i: (0, 0))

    def body(sink_ref, q_ref, kp_ref, kc_ref, vp_ref, vc_ref, cp_ref, cc_ref, sp_ref, sc_ref, do_ref,
             dq_ref, dk_out, dv_out, dsink_ref, dk_ref, dv_ref):
        i = pl.program_id(0)

        @pl.when(i == 0)
        def _():
            dk_ref[...] = jnp.zeros_like(dk_ref)
            dv_ref[...] = jnp.zeros_like(dv_ref)
            dsink_ref[...] = jnp.zeros_like(dsink_ref)

        cp, cc, sp, sc = cp_ref[...], cc_ref[...], sp_ref[...], sc_ref[...]
        q_stack, kwin, vwin = _swa_inputs(q_ref, kp_ref, kc_ref, vp_ref, vc_ref, cp_ref, cc_ref, sp_ref, sc_ref)
        probs, psink = _swa_probs(i, q_stack, kwin, sink_ref, blk)
        do_stack = _swa_stack([do_ref[:, pp * LANES:(pp + 1) * LANES] for pp in range(SWA_Q_HEADS // 2)])
        dp = _dot_nt(do_stack, vwin)
        dsum = jnp.sum(probs * dp, axis=1, keepdims=True)
        ds = (probs * (dp - dsum) * ATTN_SCALE).astype(BF16)
        for pp, tile in enumerate(_swa_unstack(_dot(ds, kwin), blk)):
            dq_ref[:, pp * LANES:(pp + 1) * LANES] = _rope_transpose(tile, cc, sc).astype(dq_ref.dtype)
        dkw = _dot_tn(ds, q_stack)
        dvw = _dot_tn(probs.astype(BF16), do_stack)
        lane1 = _lane_iota((1, LANES))
        sink_share = psink * dsum
        dsink = jnp.zeros((1, LANES), F32)
        for h in range(SWA_Q_HEADS):
            dsink = dsink + jnp.where(lane1 == h, -jnp.sum(sink_share[h * blk:(h + 1) * blk]), 0.0)
        dsink_ref[...] += dsink
        ip = jnp.maximum(i - 1, 0)
        rows_p = pl.ds(pl.multiple_of(ip * blk, blk), blk)
        rows_c = pl.ds(pl.multiple_of(i * blk, blk), blk)
        dk_ref[rows_p, :] += _rope_transpose(dkw[:blk], cp, sp)
        dv_ref[rows_p, :] += dvw[:blk]
        dk_ref[rows_c, :] += _rope_transpose(dkw[blk:], cc, sc)
        dv_ref[rows_c, :] += dvw[blk:]

        @pl.when(i == T // blk - 1)
        def _():
            dk_out[...] = dk_ref[...].astype(BF16)
            dv_out[...] = dv_ref[...].astype(BF16)

    return pl.pallas_call(
        body, name="swa_bwd", grid=(T // blk,),
        in_specs=[pl.BlockSpec(memory_space=pltpu.SMEM), cur(SWA_Q_WIDTH), prev(LANES), cur(LANES), prev(LANES), cur(LANES),
                  prev(LANES), cur(LANES), prev(LANES), cur(LANES), cur(SWA_Q_WIDTH)],
        out_specs=[cur(SWA_Q_WIDTH), full(LANES), full(LANES), pl.BlockSpec((1, LANES), lambda i: (0, 0))],
        out_shape=[jax.ShapeDtypeStruct((T, SWA_Q_WIDTH), BF16), jax.ShapeDtypeStruct((T, LANES), BF16),
                   jax.ShapeDtypeStruct((T, LANES), BF16), jax.ShapeDtypeStruct((1, LANES), F32)],
        scratch_shapes=[pltpu.VMEM((T, LANES), F32)] * 2,
        compiler_params=_params(("arbitrary",)),
    )(sinks, qa, ka, ka, va, va, cos, cos, sin, sin, dya)


class _Exchange:
    FLIPS = [(fx, fy, fc) for fx in (0, 1) for fy in (0, 1) for fc in (0, 1) if (fx, fy, fc) != (0, 0, 0)]

    def __init__(self, arrays, kinds):
        self.arrays, self.kinds, self.n = list(arrays), list(kinds), len(arrays)

    def out_shape(self):
        return [jax.ShapeDtypeStruct(a.shape if k == "scatter" else (N_DEV,) + a.shape, a.dtype) for a, k in zip(self.arrays, self.kinds)]

    def scratch(self):
        return [pltpu.SemaphoreType.DMA((self.n * 7,)), pltpu.SemaphoreType.DMA((self.n * 7,)), pltpu.SemaphoreType.DMA((self.n,))]

    def bind(self, ins, outs, send_sems, recv_sems, local_sems=None):
        x, y, c = lax.axis_index("x"), lax.axis_index("y"), lax.axis_index("c")
        me = 4 * x + 2 * y + c
        local, remote = [], []
        for a, kind in enumerate(self.kinds):
            mine = ins[a].at[me] if kind == "scatter" else ins[a]
            if local_sems is not None:
                local.append(pltpu.make_async_copy(mine, outs[a].at[me], local_sems.at[a]))
            for k, (fx, fy, fc) in enumerate(self.FLIPS):
                peer = (x ^ fx, y ^ fy, c ^ fc)
                peer_slot = 4 * peer[0] + 2 * peer[1] + peer[2]
                src = ins[a].at[peer_slot] if kind == "scatter" else ins[a]
                sems = dict(send_sem=send_sems.at[7 * a + k], recv_sem=recv_sems.at[7 * a + k], device_id=peer, device_id_type=MESH)
                remote.append((pltpu.make_async_remote_copy(src_ref=src, dst_ref=outs[a].at[me], **sems),
                               pltpu.make_async_remote_copy(src_ref=src, dst_ref=outs[a].at[peer_slot], **sems)))

        def start():
            for cp in local:
                cp.start()
            for send, _ in remote:
                send.start()

        def wait():
            for send, arrival in remote:
                arrival.wait_recv()
                send.wait_send()
            for cp in local:
                cp.wait()

        return start, wait


def _hosted_call(body, name, grid, exchange, *, in_specs, out_specs, out_shape, semantics, args, scratch=(), prefetch=()):
    n = 0 if exchange is None else exchange.n
    n_pre, n_in, n_out, n_scratch = len(prefetch), len(in_specs), len(out_specs), len(scratch)

    def hosted(*refs):
        pre, rest = refs[:n_pre], refs[n_pre:]
        ins, rest = rest[:n_in], rest[n_in:]
        ex_ins, rest = rest[:n], rest[n:]
        outs, rest = rest[:n_out], rest[n_out:]
        ex_outs, rest = rest[:n], rest[n:]
        own, sems = rest[:n_scratch], rest[n_scratch:]
        if exchange is None:
            return body(*pre, *ins, *outs, *own)
        start, wait = exchange.bind(ex_ins, ex_outs, *sems)
        ids = [pl.program_id(d) for d in range(len(grid))]
        first = functools.reduce(jnp.logical_and, [i == 0 for i in ids])
        last = functools.reduce(jnp.logical_and, [i == g - 1 for i, g in zip(ids, grid)])
        pl.when(first)(start)
        body(*pre, *ins, *outs, *own)
        pl.when(last)(wait)

    grid_spec = pltpu.PrefetchScalarGridSpec(
        num_scalar_prefetch=n_pre, grid=grid, in_specs=list(in_specs) + [ANY] * n, out_specs=list(out_specs) + [ANY] * n,
        scratch_shapes=list(scratch) + ([] if exchange is None else exchange.scratch()))
    res = pl.pallas_call(
        hosted, name=name, grid_spec=grid_spec, out_shape=list(out_shape) + ([] if exchange is None else exchange.out_shape()),
        compiler_params=_params(semantics if exchange is None else ("arbitrary",) * len(grid)),
    )(*prefetch, *args, *([] if exchange is None else exchange.arrays))
    return res[:n_out], res[n_out:]


HBM = pl.BlockSpec(memory_space=pltpu.HBM)
SEM = pl.BlockSpec(memory_space=pltpu.SEMAPHORE)
SPLIT_COPY = pltpu.SideEffectType.DATAFLOW_SIDE_EFFECTING


def _start_call(body, name, grid, exchange, *, in_specs, out_specs, out_shape, args):
    n, n_in, n_out = exchange.n, len(in_specs), len(out_specs)
    landing = [lax.empty(s.shape, s.dtype) for s in exchange.out_shape()]

    def hosted(*refs):
        ins, rest = refs[:n_in], refs[n_in:]
        sources, rest = rest[:n], rest[n:]
        zones, rest = rest[:n], rest[n:]
        outs, rest = rest[:n_out], rest[n_out:]
        send_sems, recv_sems = rest[2 * n:]
        start, _ = exchange.bind(sources, zones, send_sems, recv_sems)
        first = functools.reduce(jnp.logical_and, [pl.program_id(d) == 0 for d in range(len(grid))])
        pl.when(first)(start)
        body(*ins, *outs)

    in_hbm = lambda arrs: [pltpu.with_memory_space_constraint(a, pltpu.HBM) for a in arrs]
    as_hbm = lambda arrs: [pltpu.HBM(a.shape, a.dtype) for a in arrs]
    res = pl.pallas_call(
        hosted, name=name, grid=grid, in_specs=list(in_specs) + [HBM] * (2 * n), out_specs=list(out_specs) + [HBM] * (2 * n) + [SEM, SEM],
        out_shape=list(out_shape) + as_hbm(exchange.arrays) + as_hbm(landing) + exchange.scratch()[:2],
        input_output_aliases={n_in + k: n_out + k for k in range(2 * n)},
        compiler_params=pltpu.CompilerParams(dimension_semantics=("arbitrary",) * len(grid), vmem_limit_bytes=VMEM_LIMIT,
                                             has_side_effects=SPLIT_COPY),
    )(*args, *in_hbm(exchange.arrays), *in_hbm(landing))
    return res[:n_out], res[n_out:]


def _finish_exchange(exchange, handle, after, name):
    n = exchange.n

    def body(*refs):
        sources, zones, (send_sems, recv_sems) = refs[:n], refs[n:2 * n], refs[2 * n:2 * n + 2]
        _, wait = exchange.bind(sources, zones, send_sems, recv_sems)
        wait()

    specs = [pltpu.HBM(a.shape, a.dtype) for a in handle[:2 * n]]
    res = pl.pallas_call(
        body, name=name, in_specs=[HBM] * (2 * n) + [SEM, SEM, ANY], out_specs=[HBM] * (2 * n), out_shape=specs,
        input_output_aliases={k: k for k in range(2 * n)}, compiler_params=pltpu.CompilerParams(has_side_effects=SPLIT_COPY),
    )(*handle, after)
    return res[n:]


SOFTPLUS_LINEAR_FROM = 30.0


def _sb_scores(qm, k, valid):
    z = _dot_nt(qm, k)
    sp = jnp.where(z > SOFTPLUS_LINEAR_FROM, z, jnp.log(1.0 + jnp.exp(z)))
    log_beta = z - sp
    if valid is not None:
        sp = jnp.where(valid, sp, 0.0)
    return sp, log_beta


def _tri2(B, cmp):
    r = lax.broadcasted_iota(jnp.int32, (2 * B, B), 0) % B
    c = lax.broadcasted_iota(jnp.int32, (2 * B, B), 1)
    return cmp(r, c).astype(BF16)


def _tri_sum(v, tri2):
    hi, lo = _split_bf16(v)
    return _dot(jnp.concatenate([hi, lo], axis=1), tri2)


def _head_masks(x):
    low = _lane_iota(x.shape) < HEAD_DIM
    zero = jnp.zeros((), x.dtype)
    return jnp.where(low, x, zero), jnp.where(low, zero, x)


def _strictly_below(B):
    r = lax.broadcasted_iota(jnp.int32, (B, B), 0)
    c = lax.broadcasted_iota(jnp.int32, (B, B), 1)
    return c < r


def _sb_grid(T, descending):
    B = min(SB_BLOCK, T)
    n = T // B
    pairs = [(i, j) for i in range(n) for j in (range(i, -1, -1) if descending else range(i + 1))]
    return B, jnp.asarray([p[0] for p in pairs], jnp.int32), jnp.asarray([p[1] for p in pairs], jnp.int32)


N_PAIRS = SB_HEADS // 2
PAIR_COLS = [slice(p * LANES, (p + 1) * LANES) for p in range(N_PAIRS)]


def _sb_fwd(qb, kb, vb, exchange=None):
    T = qb.shape[0]
    B, i_tab, j_tab = _sb_grid(T, descending=True)
    n = T // B

    def body(i_ref, j_ref, q_ref, k_ref, v_ref, o_ref, a_ref, b_ref, acc_ref, c_ref, tri_ref):
        s = pl.program_id(0)
        i, j = i_ref[s], j_ref[s]

        @pl.when(s == 0)
        def _():
            tri_ref[...] = _tri2(B, lambda r, c: r > c)

        @pl.when(j == i)
        def _():
            acc_ref[...] = jnp.zeros_like(acc_ref)
            c_ref[...] = jnp.zeros_like(c_ref)

        def block(valid):
            for p, cols in enumerate(PAIR_COLS):
                qms = _head_masks(q_ref[:, cols] * ATTN_SCALE)
                k = k_ref[:, cols]
                probs = []
                for hh in range(2):
                    h = 2 * p + hh
                    sp, lb = _sb_scores(qms[hh], k, valid)
                    c = c_ref[h]
                    a = jnp.exp(lb - (c + _tri_sum(sp, tri_ref[...])))
                    beta = jnp.exp(lb)
                    if valid is not None:
                        a = jnp.where(valid, a, 0.0)
                        beta = jnp.where(valid, beta, 0.0)
                    probs.append(a.astype(BF16))
                    a_ref[h] = probs[-1]
                    b_ref[h] = beta.astype(BF16)
                    c_ref[h] = c + jnp.sum(sp, axis=1, keepdims=True)
                acc_ref[:, cols] += _dot(jnp.concatenate(probs, axis=1), jnp.concatenate(_head_masks(v_ref[:, cols]), axis=0))

        pl.when(j == i)(lambda: block(_strictly_below(B)))
        pl.when(j != i)(lambda: block(None))

        @pl.when(j == 0)
        def _():
            o_ref[...] = acc_ref[...].astype(o_ref.dtype)

    q_spec = pl.BlockSpec((B, SB_WIDTH), lambda s, i_ref, j_ref: (i_ref[s], 0))
    k_spec = pl.BlockSpec((B, SB_WIDTH), lambda s, i_ref, j_ref: (j_ref[s], 0))
    tile = pl.BlockSpec((None, None, SB_HEADS, B, B), lambda s, i_ref, j_ref: (i_ref[s], j_ref[s], 0, 0, 0))
    saved = jax.ShapeDtypeStruct((n, n, SB_HEADS, B, B), BF16)
    return _hosted_call(
        body, "sb_fwd", (int(i_tab.shape[0]),), exchange, prefetch=(i_tab, j_tab),
        in_specs=[q_spec, k_spec, k_spec], out_specs=[q_spec, tile, tile],
        out_shape=[jax.ShapeDtypeStruct((T, SB_WIDTH), BF16), saved, saved],
        scratch=[pltpu.VMEM((B, SB_WIDTH), F32), pltpu.VMEM((SB_HEADS, B, 1), F32), pltpu.VMEM((2 * B, B), BF16)],
        semantics=("arbitrary",), args=(qb, kb, vb))


def _sb_bwd(qb, kb, vb, probs, betas, dyb, exchange=None):
    T = qb.shape[0]
    B, i_tab, j_tab = _sb_grid(T, descending=False)
    n_steps = int(i_tab.shape[0])

    def block_diag_t(x):
        xt = x.T
        top = lax.broadcasted_iota(jnp.int32, xt.shape, 0) < HEAD_DIM
        zero = jnp.zeros((), x.dtype)
        return jnp.concatenate([jnp.where(top, xt, zero), jnp.where(top, zero, xt)], axis=1)

    def body(i_ref, j_ref, q_ref, k_ref, v_ref, a_ref, b_ref, do_ref, dq_ref, dk_out, dv_out,
             dq_acc, cg_ref, dkt_ref, dvt_ref, tri_ref, qt_ref, dot_ref):
        s = pl.program_id(0)
        i, j = i_ref[s], j_ref[s]

        @pl.when(s == 0)
        def _():
            dkt_ref[...] = jnp.zeros_like(dkt_ref)
            dvt_ref[...] = jnp.zeros_like(dvt_ref)
            tri_ref[...] = _tri2(B, lambda r, c: r < c)[:B]

        @pl.when(j == 0)
        def _():
            dq_acc[...] = jnp.zeros_like(dq_acc)
            cg_ref[...] = jnp.zeros_like(cg_ref)
            for p, cols in enumerate(PAIR_COLS):
                qt_ref[p] = block_diag_t(q_ref[:, cols] * ATTN_SCALE)
                dot_ref[p] = block_diag_t(do_ref[:, cols])

        for p, cols in enumerate(PAIR_COLS):
            doms = _head_masks(do_ref[:, cols])
            k, v = k_ref[:, cols], v_ref[:, cols]
            dzs = []
            for hh in range(2):
                h = 2 * p + hh
                g = a_ref[h].astype(F32) * _dot_nt(doms[hh], v)
                cg = cg_ref[h]
                gsum = g + (cg + _dot(g.astype(BF16), tri_ref[...]))
                dzs.append((g - b_ref[h].astype(F32) * gsum).astype(BF16))
                cg_ref[h] = cg + jnp.sum(g, axis=1, keepdims=True)
            dq_acc[:, cols] += _dot(jnp.concatenate(dzs, axis=1), jnp.concatenate(_head_masks(k), axis=0))
            dkt_ref[j, cols, :] += _dot(qt_ref[p], jnp.concatenate(dzs, axis=0))
            dvt_ref[j, cols, :] += _dot(dot_ref[p], jnp.concatenate([a_ref[2 * p], a_ref[2 * p + 1]], axis=0))

        @pl.when(j == i)
        def _():
            dq_ref[...] = (dq_acc[...] * ATTN_SCALE).astype(dq_ref.dtype)

        @pl.when(s == n_steps - 1)
        def _():
            for jb in range(T // B):
                dk_out[jb * B:(jb + 1) * B, :] = dkt_ref[jb].T.astype(BF16)
                dv_out[jb * B:(jb + 1) * B, :] = dvt_ref[jb].T.astype(BF16)

    q_spec = pl.BlockSpec((B, SB_WIDTH), lambda s, i_ref, j_ref: (i_ref[s], 0))
    k_spec = pl.BlockSpec((B, SB_WIDTH), lambda s, i_ref, j_ref: (j_ref[s], 0))
    tile = pl.BlockSpec((None, None, SB_HEADS, B, B), lambda s, i_ref, j_ref: (i_ref[s], j_ref[s], 0, 0, 0))
    full = pl.BlockSpec((T, SB_WIDTH), lambda s, i_ref, j_ref: (0, 0))
    return _hosted_call(
        body, "sb_bwd", (n_steps,), exchange, prefetch=(i_tab, j_tab),
        in_specs=[q_spec, k_spec, k_spec, tile, tile, q_spec], out_specs=[q_spec, full, full],
        out_shape=[jax.ShapeDtypeStruct((T, SB_WIDTH), BF16)] * 3,
        scratch=[pltpu.VMEM((B, SB_WIDTH), F32), pltpu.VMEM((SB_HEADS, B, 1), F32), pltpu.VMEM((T // B, SB_WIDTH, B), F32),
                 pltpu.VMEM((T // B, SB_WIDTH, B), F32), pltpu.VMEM((B, B), BF16), pltpu.VMEM((N_PAIRS, LANES, 2 * B), BF16),
                 pltpu.VMEM((N_PAIRS, LANES, 2 * B), BF16)],
        semantics=("arbitrary",), args=(qb, kb, vb, probs, betas, dyb))


def _ln_stats(u):
    mu = jnp.mean(u, axis=-1, keepdims=True)
    xc = u - mu
    var = jnp.mean(xc * xc, axis=-1, keepdims=True)
    rstd = lax.rsqrt(var + LN_EPS)
    return xc * rstd, rstd


def _ln_bwd(dy, xhat, rstd, g):
    dxh = dy * g
    return rstd * (dxh - jnp.mean(dxh, axis=-1, keepdims=True) - xhat * jnp.mean(dxh * xhat, axis=-1, keepdims=True))


def _gates(gl_ref, bg_ref):
    ga = jax.nn.sigmoid(gl_ref[:, :D_MODEL] + bg_ref[:, :D_MODEL])
    gb = jax.nn.sigmoid(gl_ref[:, D_MODEL:] + bg_ref[:, D_MODEL:])
    return ga, gb


def _mix_fwd(ya, yb, gl, x, wa, wb, wo, b_gate, ln1_g, ln1_b):
    T = x.shape[0]
    tm = min(256, T)

    def body(ya_ref, yb_ref, gl_ref, x_ref, wa_ref, wb_ref, wo_ref, bg_ref, g_ref, b_ref, h_ref, u_ref, x1_ref):
        ga, gb = _gates(gl_ref, bg_ref)
        h = (ga * _dot(ya_ref[...], wa_ref[...]) + gb * _dot(yb_ref[...], wb_ref[...])).astype(BF16)
        h_ref[...] = h
        u = ALPHA * x_ref[...] + _dot(h, wo_ref[...])
        u_ref[...] = u
        xhat, _ = _ln_stats(u)
        x1_ref[...] = (xhat * g_ref[...] + b_ref[...]).astype(BF16)

    row = lambda n: pl.BlockSpec((tm, n), lambda i: (i, 0))
    const = lambda r, n: pl.BlockSpec((r, n), lambda i: (0, 0))
    return pl.pallas_call(
        body, name="mix_fwd", grid=(T // tm,),
        in_specs=[row(SWA_Q_WIDTH), row(SB_WIDTH), row(GATE_WIDTH), row(D_MODEL), const(SWA_Q_WIDTH, D_MODEL), const(SB_WIDTH, D_MODEL),
                  const(D_MODEL, D_MODEL), const(1, GATE_WIDTH), const(1, D_MODEL), const(1, D_MODEL)],
        out_specs=[row(D_MODEL)] * 3,
        out_shape=[jax.ShapeDtypeStruct((T, D_MODEL), BF16), jax.ShapeDtypeStruct((T, D_MODEL), F32), jax.ShapeDtypeStruct((T, D_MODEL), BF16)],
        compiler_params=_params(("parallel",)),
    )(ya, yb, gl, x, wa, wb, wo, b_gate, ln1_g, ln1_b)


def _mix_bwd(du1, ya, yb, gl, wa, wb, wo, b_gate):
    T = du1.shape[0]
    tm = min(256, T)

    def body(du_ref, ya_ref, yb_ref, gl_ref, wa_ref, wb_ref, wo_ref, bg_ref, dya_ref, dyb_ref, dgl_ref, dta_ref, dtb_ref, dbg_ref):
        @pl.when(pl.program_id(0) == 0)
        def _():
            dbg_ref[...] = jnp.zeros_like(dbg_ref)

        dh = _dot_nt(du_ref[...].astype(BF16), wo_ref[...])
        ga, gb = _gates(gl_ref, bg_ref)
        for gate, y_ref, w_ref, dy_ref, dt_ref, lo in ((ga, ya_ref, wa_ref, dya_ref, dta_ref, 0), (gb, yb_ref, wb_ref, dyb_ref, dtb_ref, D_MODEL)):
            t = _dot(y_ref[...], w_ref[...])
            dlogit = dh * t * gate * (1.0 - gate)
            dgl_ref[:, lo:lo + D_MODEL] = dlogit.astype(BF16)
            dbg_ref[:, lo:lo + D_MODEL] += jnp.sum(dlogit, axis=0, keepdims=True)
            dt = (dh * gate).astype(BF16)
            dt_ref[...] = dt
            dy_ref[...] = _dot_nt(dt, w_ref[...]).astype(BF16)

    row = lambda n: pl.BlockSpec((tm, n), lambda i: (i, 0))
    const = lambda r, n: pl.BlockSpec((r, n), lambda i: (0, 0))
    sds = lambda n, dt: jax.ShapeDtypeStruct((T, n), dt)
    return pl.pallas_call(
        body, name="mix_bwd", grid=(T // tm,),
        in_specs=[row(D_MODEL), row(SWA_Q_WIDTH), row(SB_WIDTH), row(GATE_WIDTH), const(SWA_Q_WIDTH, D_MODEL), const(SB_WIDTH, D_MODEL),
                  const(D_MODEL, D_MODEL), const(1, GATE_WIDTH)],
        out_specs=[row(SWA_Q_WIDTH), row(SB_WIDTH), row(GATE_WIDTH), row(D_MODEL), row(D_MODEL), const(1, GATE_WIDTH)],
        out_shape=[sds(SWA_Q_WIDTH, BF16), sds(SB_WIDTH, BF16), sds(GATE_WIDTH, BF16), sds(D_MODEL, BF16), sds(D_MODEL, BF16),
                   jax.ShapeDtypeStruct((1, GATE_WIDTH), F32)],
        compiler_params=_params(("arbitrary",)),
    )(du1, ya, yb, gl, wa, wb, wo, b_gate)


CONV_COLS = LANES


CONV_CHUNK = 64
CONV_CHUNK_FWD = 256
HALO = 8


def _taps(ref, r0, rows, lead):
    return [ref[pl.ds(r0 + lead + k, rows), :] for k in ((-2, -1, 0) if lead else (0, 1, 2))]


def _chunks(T, rows, step, init=None):
    def body(c, carry):
        out = step(pl.multiple_of(c * rows, rows), *(() if init is None else (carry,)))
        return carry if init is None else out
    return lax.fori_loop(0, T // rows, body, 0 if init is None else init)


def _conv_chunk(taps, w_ref, b_ref):
    return w_ref[0:1, :] * taps[0] + w_ref[1:2, :] * taps[1] + w_ref[2:3, :] * taps[2] + b_ref[...]


def _fold(x):
    return jnp.sum(x.reshape(x.shape[0] // 8, 8, x.shape[1]), axis=0)


def _conv_specs(T):
    nb = D_FF // CONV_COLS
    pair = pl.BlockSpec((2, T, CONV_COLS), lambda j: (0, 0, j))
    gate = lambda r: pl.BlockSpec((r, CONV_COLS), lambda j: (0, j))
    up = lambda r: pl.BlockSpec((r, CONV_COLS), lambda j: (0, j + nb))
    return nb, pair, gate, up


def _conv_glu_fwd(p3, conv_w, conv_b):
    T = p3.shape[1]
    nb, pair, gate, up = _conv_specs(T)

    R = min(CONV_CHUNK_FWD, T)

    def body(p_ref, wg_ref, wu_ref, bg_ref, bu_ref, s_ref, pg_s, pu_s):
        for half, scr in enumerate((pg_s, pu_s)):
            scr[0:HALO, :] = jnp.zeros((HALO, CONV_COLS), F32)
            scr[HALO:HALO + T, :] = p_ref[half].astype(F32)
        def step(r0):
            ag = _conv_chunk(_taps(pg_s, r0, R, HALO), wg_ref, bg_ref)
            au = _conv_chunk(_taps(pu_s, r0, R, HALO), wu_ref, bu_ref)
            s_ref[pl.ds(r0, R), :] = (ag * jax.nn.sigmoid(ag) * au).astype(BF16)

        _chunks(T, R, step)

    return pl.pallas_call(
        body, name="conv_glu_fwd", grid=(nb,),
        in_specs=[pair, gate(3), up(3), gate(1), up(1)],
        out_specs=pl.BlockSpec((T, CONV_COLS), lambda j: (0, j)),
        out_shape=jax.ShapeDtypeStruct((T, D_FF), BF16),
        scratch_shapes=[pltpu.VMEM((T + HALO, CONV_COLS), F32)] * 2,
        compiler_params=_params(("parallel",)),
    )(p3, conv_w, conv_w, conv_b, conv_b)


def _conv_glu_bwd(p3, ds, conv_w, conv_b):
    T = p3.shape[1]
    nb, pair, gate, up = _conv_specs(T)

    R = min(CONV_CHUNK, T)

    def body(p_ref, ds_ref, wg_ref, wu_ref, bg_ref, bu_ref, dp_ref, dwg_ref, dwu_ref, dbg_ref, dbu_ref, pg_s, pu_s, dag_s, dau_s):
        for half, scr in enumerate((pg_s, pu_s)):
            scr[0:HALO, :] = jnp.zeros((HALO, CONV_COLS), F32)
            scr[HALO:HALO + T, :] = p_ref[half].astype(F32)
        for scr in (dag_s, dau_s):
            scr[T:T + HALO, :] = jnp.zeros((HALO, CONV_COLS), F32)
        halves = ((pg_s, dag_s, wg_ref, dwg_ref, dbg_ref), (pu_s, dau_s, wu_ref, dwu_ref, dbu_ref))

        def step(r0, sums):
            taps = [_taps(p_s, r0, R, HALO) for p_s, *_ in halves]
            ag = _conv_chunk(taps[0], wg_ref, bg_ref)
            au = _conv_chunk(taps[1], wu_ref, bu_ref)
            sg = jax.nn.sigmoid(ag)
            d = ds_ref[pl.ds(r0, R), :].astype(F32)
            das = (d * au * (sg * (1.0 + ag * (1.0 - sg))), d * ag * sg)
            out = []
            for half, (_, da_s, *_) in enumerate(halves):
                da_s[pl.ds(r0, R), :] = das[half]
                out.append(tuple(sums[half][k] + _fold(das[half] * taps[half][k]) for k in range(3)) + (sums[half][3] + _fold(das[half]),))
            return tuple(out)

        sums = _chunks(T, R, step, ((jnp.zeros((8, CONV_COLS), F32),) * 4,) * 2)
        for half, (_, da_s, w_ref, dw_ref, db_ref) in enumerate(halves):
            for k in range(3):
                dw_ref[k:k + 1, :] = jnp.sum(sums[half][k], axis=0, keepdims=True)
            db_ref[...] = jnp.sum(sums[half][3], axis=0, keepdims=True)

            def transposed(r0, da_s=da_s, w_ref=w_ref, half=half):
                da0, da1, da2 = _taps(da_s, r0, R, 0)
                dp_ref[half, pl.ds(r0, R), :] = (w_ref[2:3, :] * da0 + w_ref[1:2, :] * da1 + w_ref[0:1, :] * da2).astype(BF16)

            _chunks(T, R, transposed)

    col = lambda r: pl.BlockSpec((r, CONV_COLS), lambda j: (0, j))
    return pl.pallas_call(
        body, name="conv_glu_bwd", grid=(nb,),
        in_specs=[pair, col(T), gate(3), up(3), gate(1), up(1)],
        out_specs=[pair, col(3), col(3), col(1), col(1)],
        out_shape=[jax.ShapeDtypeStruct((2, T, D_FF), BF16), jax.ShapeDtypeStruct((3, D_FF), F32), jax.ShapeDtypeStruct((3, D_FF), F32),
                   jax.ShapeDtypeStruct((1, D_FF), F32), jax.ShapeDtypeStruct((1, D_FF), F32)],
        scratch_shapes=[pltpu.VMEM((T + HALO, CONV_COLS), F32)] * 4,
        compiler_params=_params(("parallel",)),
    )(p3, ds, conv_w, conv_w, conv_b, conv_b)


def _ffn_down_loss(s, w_down, u1, ln1_g, ln1_b, ln2_g, ln2_b, target):
    T = u1.shape[0]
    tm = min(256, T)

    def body(s_ref, w_ref, u1_ref, g1_ref, b1_ref, g2_ref, b2_ref, t_ref, du_ref, dub_ref, dg_ref, db_ref, loss_ref):
        @pl.when(pl.program_id(0) == 0)
        def _():
            dg_ref[...] = jnp.zeros_like(dg_ref)
            db_ref[...] = jnp.zeros_like(db_ref)
            loss_ref[...] = jnp.zeros_like(loss_ref)

        xh1, _ = _ln_stats(u1_ref[...])
        x1 = xh1 * g1_ref[...] + b1_ref[...]
        u2 = ALPHA * x1 + _dot(s_ref[...], w_ref[...])
        xh2, rstd2 = _ln_stats(u2)
        err = xh2 * g2_ref[...] + b2_ref[...] - t_ref[...]
        per_token = jnp.mean(err * err, axis=-1, keepdims=True)
        loss_ref[...] += 0.5 * jnp.sum(per_token, axis=0, keepdims=True)
        dy = err * (1.0 / D_MODEL)
        dg_ref[...] += jnp.sum(dy * xh2, axis=0, keepdims=True)
        db_ref[...] += jnp.sum(dy, axis=0, keepdims=True)
        du2 = _ln_bwd(dy, xh2, rstd2, g2_ref[...])
        du_ref[...] = du2
        dub_ref[...] = du2.astype(BF16)

    row = lambda n: pl.BlockSpec((tm, n), lambda i: (i, 0))
    const = lambda r, n: pl.BlockSpec((r, n), lambda i: (0, 0))
    vec = const(1, D_MODEL)
    return pl.pallas_call(
        body, name="ffn_down_loss", grid=(T // tm,),
        in_specs=[row(D_FF), const(D_FF, D_MODEL), row(D_MODEL), vec, vec, vec, vec, row(D_MODEL)],
        out_specs=[row(D_MODEL), row(D_MODEL), vec, vec, const(1, LANES)],
        out_shape=[jax.ShapeDtypeStruct((T, D_MODEL), F32), jax.ShapeDtypeStruct((T, D_MODEL), BF16), jax.ShapeDtypeStruct((1, D_MODEL), F32),
                   jax.ShapeDtypeStruct((1, D_MODEL), F32), jax.ShapeDtypeStruct((1, LANES), F32)],
        compiler_params=_params(("arbitrary",)),
    )(s, w_down, u1, ln1_g, ln1_b, ln2_g, ln2_b, target)


def _ffn_up_bwd_ln1(dp3, w_up, du2, u1, ln1_g):
    T = u1.shape[0]
    tm = min(256, T)

    def body(dp_ref, w_ref, du2_ref, u1_ref, g_ref, du_ref, dub_ref, dg_ref, db_ref):
        @pl.when(pl.program_id(0) == 0)
        def _():
            dg_ref[...] = jnp.zeros_like(dg_ref)
            db_ref[...] = jnp.zeros_like(db_ref)

        dx1 = _dot_nt(dp_ref[0], w_ref[:, :D_FF]) + _dot_nt(dp_ref[1], w_ref[:, D_FF:]) + ALPHA * du2_ref[...]
        xh, rstd = _ln_stats(u1_ref[...])
        dg_ref[...] += jnp.sum(dx1 * xh, axis=0, keepdims=True)
        db_ref[...] += jnp.sum(dx1, axis=0, keepdims=True)
        du1 = _ln_bwd(dx1, xh, rstd, g_ref[...])
        du_ref[...] = du1
        dub_ref[...] = du1.astype(BF16)

    row = lambda n: pl.BlockSpec((tm, n), lambda i: (i, 0))
    const = lambda r, n: pl.BlockSpec((r, n), lambda i: (0, 0))
    vec = const(1, D_MODEL)
    return pl.pallas_call(
        body, name="ffn_up_bwd_ln1", grid=(T // tm,),
        in_specs=[pl.BlockSpec((2, tm, D_FF), lambda i: (0, i, 0)), const(D_MODEL, 2 * D_FF), row(D_MODEL), row(D_MODEL), vec],
        out_specs=[row(D_MODEL), row(D_MODEL), vec, vec],
        out_shape=[jax.ShapeDtypeStruct((T, D_MODEL), F32), jax.ShapeDtypeStruct((T, D_MODEL), BF16), jax.ShapeDtypeStruct((1, D_MODEL), F32),
                   jax.ShapeDtypeStruct((1, D_MODEL), F32)],
        compiler_params=_params(("arbitrary",)),
    )(dp3, w_up, du2, u1, ln1_g)


def _local_step(x, positions, w_in, b_gate, sinks, ln1_g, ln1_b, conv_b, ln2_g, ln2_b, target, later_weights,
                early_exchange=None, tail_exchange=None):
    T = x.shape[0]
    inv_freq = 1.0 / (ROPE_THETA ** (jnp.arange(0, HEAD_DIM, 2, dtype=F32) / HEAD_DIM))
    cos, sin = _rope_tables(positions.reshape(T, 1), jnp.tile(inv_freq, LANES // (HEAD_DIM // 2)).reshape(1, LANES))

    xb, qa, ka, va, qb, kb, vb, gl = _in_proj(x, w_in)
    ya = _swa_fwd(qa, ka, va, cos, sin, sinks)
    if isinstance(later_weights, tuple):
        exchange, finish = later_weights
        (yb, probs, betas), arrived = _sb_fwd(qb, kb, vb, exchange)
        later_weights = finish(arrived)
    else:
        (yb, probs, betas), _ = _sb_fwd(qb, kb, vb)
    wa, wb, wo, w_up, conv_w, w_down = later_weights
    h, u1, x1 = _mix_fwd(ya, yb, gl, x, wa, wb, wo, b_gate, ln1_g, ln1_b)

    ff_tn = D_FF // 2
    nff = D_FF // ff_tn
    tm = min(512, T)
    p3 = _matmul(x1, w_up, kind="nn", name="ffn_up", grid=(T // tm, 2 * nff),
                 a_spec=pl.BlockSpec((tm, D_MODEL), lambda i, j: (i, 0)), b_spec=pl.BlockSpec((D_MODEL, ff_tn), lambda i, j: (0, j)),
                 out_spec=pl.BlockSpec((None, tm, ff_tn), lambda i, j: (j // nff, i, j % nff)),
                 out_shape=jax.ShapeDtypeStruct((2, T, D_FF), ACT_DTYPE))
    s = _conv_glu_fwd(p3, conv_w, conv_b)
    du2, du2b, dln2_g, dln2_b, loss = _ffn_down_loss(s, w_down, u1, ln1_g, ln1_b, ln2_g, ln2_b, target)

    ds = _matmul(du2b, w_down, kind="nt", name="ffn_down_bwd", grid=(T // tm, nff),
                 a_spec=pl.BlockSpec((tm, D_MODEL), lambda i, j: (i, 0)), b_spec=pl.BlockSpec((ff_tn, D_MODEL), lambda i, j: (j, 0)),
                 out_spec=pl.BlockSpec((tm, ff_tn), lambda i, j: (i, j)), out_shape=jax.ShapeDtypeStruct((T, D_FF), ACT_DTYPE))
    dp3, dcw_g, dcw_u, dcb_g, dcb_u = _conv_glu_bwd(p3, ds, conv_w, conv_b)
    tk = 256
    dw_down = _matmul(s, du2b, kind="tn", name="dw_down", grid=(D_FF // tk,),
                      a_spec=pl.BlockSpec((T, tk), lambda i: (0, i)), b_spec=pl.BlockSpec((T, D_MODEL), lambda i: (0, 0)),
                      out_spec=pl.BlockSpec((tk, D_MODEL), lambda i: (i, 0)), out_shape=jax.ShapeDtypeStruct((D_FF, D_MODEL), BF16))
    dw_up = _matmul(x1, dp3, kind="tn", name="dw_up", grid=(D_MODEL // 512, 2 * nff),
                    a_spec=pl.BlockSpec((T, 512), lambda i, j: (0, i)), b_spec=pl.BlockSpec((None, T, ff_tn), lambda i, j: (j // nff, 0, j % nff)),
                    out_spec=pl.BlockSpec((512, ff_tn), lambda i, j: (i, j)), out_shape=jax.ShapeDtypeStruct((D_MODEL, 2 * D_FF), BF16))
    du1, du1b, dln1_g, dln1_b = _ffn_up_bwd_ln1(dp3, w_up, du2, u1, ln1_g)
    dya, dyb, dgl, dta, dtb, db_gate = _mix_bwd(du1, ya, yb, gl, wa, wb, wo, b_gate)

    def dw_tn(a, g, name):
        rows, cols = a.shape[1], g.shape[1]
        tn = min(512, cols)
        return _matmul(a, g, kind="tn", name=name, grid=(rows // 512, cols // tn),
                       a_spec=pl.BlockSpec((T, 512), lambda i, j: (0, i)), b_spec=pl.BlockSpec((T, tn), lambda i, j: (0, j)),
                       out_spec=pl.BlockSpec((512, tn), lambda i, j: (i, j)), out_shape=jax.ShapeDtypeStruct((rows, cols), BF16))

    dwa = dw_tn(ya, dta, "dw_branch_a")
    dwb = dw_tn(yb, dtb, "dw_branch_b")
    dwo = dw_tn(h, du1b, "dw_out")

    grads = dict(
        b_gate=db_gate, w_branch_a=dwa, w_branch_b=dwb, w_out=dwo, ln1_g=dln1_g, ln1_b=dln1_b,
        w_up=dw_up, conv_w=jnp.concatenate([dcw_g, dcw_u], axis=1), conv_b=(dcb_g, dcb_u), w_down=dw_down, ln2_g=dln2_g, ln2_b=dln2_b)
    (dqb, dkb, dvb), early_out = _sb_bwd(qb, kb, vb, probs, betas, dyb, early_exchange(grads) if early_exchange else None)
    dqa, dka, dva, grads["sinks"] = _swa_bwd(qa, ka, va, cos, sin, sinks, dya)
    dproj = (dqa, dka, dva, dqb, dkb, dvb, dgl)
    grads["w_in"] = _dw_in(xb, dproj)
    grad_x, tail_out = _grad_x(dproj, w_in, du1, tail_exchange(grads, loss) if tail_exchange else None)
    return loss, grad_x, grads, early_out, tail_out


def _dw_in(xb, dproj):
    T = xb.shape[0]
    tn = 2 * LANES
    groups, start, k = [], 0, 0
    while k < len(IN_WIDTHS):
        if IN_WIDTHS[k] >= tn:
            groups.append((start, IN_WIDTHS[k] // tn, [(k, 0, tn)]))
            k += 1
        else:
            members, off = [], 0
            while off < tn:
                members.append((k, off, IN_WIDTHS[k]))
                off += IN_WIDTHS[k]
                k += 1
            groups.append((start, 1, members))
        start += groups[-1][1]

    def body(x_ref, *refs):
        pieces, o_ref = refs[:-1], refs[-1]
        j = pl.program_id(0)
        for first, steps, members in groups:
            @pl.when((j >= first) & (j < first + steps))
            def _(members=members):
                for k, off, width in members:
                    o_ref[:, off:off + width] = _dot_tn(x_ref[...], pieces[k][...]).astype(o_ref.dtype)

    specs = [None] * len(IN_WIDTHS)
    for first, steps, members in groups:
        for k, _, width in members:
            specs[k] = pl.BlockSpec((T, width), lambda j, first=first, steps=steps: (0, jnp.clip(j - first, 0, steps - 1)))
    return pl.pallas_call(
        body, name="dw_in", grid=(IN_TOTAL // tn,),
        in_specs=[pl.BlockSpec((T, D_MODEL), lambda j: (0, 0))] + specs, out_specs=pl.BlockSpec((D_MODEL, tn), lambda j: (0, j)),
        out_shape=jax.ShapeDtypeStruct((D_MODEL, IN_TOTAL), BF16), compiler_params=_params(("arbitrary",)),
    )(xb, *dproj)


def _grad_x(dproj, w_in, du1, exchange=None):
    T = du1.shape[0]
    tm = min(256, T)
    offs = np.cumsum((0,) + IN_WIDTHS)

    def body(*refs):
        pieces, (w_ref, du_ref, o_ref) = refs[:len(IN_WIDTHS)], refs[len(IN_WIDTHS):]
        acc = ALPHA * du_ref[...]
        for p_ref, a, b in zip(pieces, offs[:-1], offs[1:]):
            acc = acc + _dot_nt(p_ref[...].astype(BF16), w_ref[:, a:b])
        o_ref[...] = acc

    row = lambda n: pl.BlockSpec((tm, n), lambda i: (i, 0))
    specs = dict(in_specs=[row(n) for n in IN_WIDTHS] + [pl.BlockSpec((D_MODEL, IN_TOTAL), lambda i: (0, 0)), row(D_MODEL)],
                 out_specs=[row(D_MODEL)], out_shape=[jax.ShapeDtypeStruct((T, D_MODEL), F32)], args=(*dproj, w_in, du1))
    if exchange is None:
        (grad_x,), _ = _hosted_call(body, "grad_x", (T // tm,), None, semantics=("parallel",), **specs)
        return grad_x, None
    (grad_x,), handle = _start_call(body, "grad_x", (T // tm,), exchange, **specs)
    return grad_x, (exchange, handle)


ANY = pl.BlockSpec(memory_space=pl.ANY)


def _all_gather(slabs, name):
    n = len(slabs)

    def body(*refs):
        ins, outs = refs[:n], refs[n:2 * n]
        send_sems, recv_sems, local_sems = refs[2 * n:]
        x, y, c = lax.axis_index("x"), lax.axis_index("y"), lax.axis_index("c")
        me, sibling = (x, y, c), (x, y, 1 - c)
        chips = [(1 - x, y), (x, 1 - y), (1 - x, 1 - y)]

        def slot(pos):
            return 4 * pos[0] + 2 * pos[1] + pos[2]

        def copy(a, k, block, to, from_input=False):
            return pltpu.make_async_remote_copy(
                src_ref=ins[a] if from_input else outs[a].at[slot(block)], dst_ref=outs[a].at[slot(block)],
                send_sem=send_sems.at[a, k], recv_sem=recv_sems.at[a, k], device_id=to, device_id_type=MESH)

        mine = [pltpu.make_async_copy(ins[a], outs[a].at[slot(me)], local_sems.at[a]) for a in range(n)]
        for cp in mine:
            cp.start()
        first = []
        for a in range(n):
            first.append(copy(a, 0, me, sibling, from_input=True))
            first += [copy(a, 1 + j, me, (*chip, c), from_input=True) for j, chip in enumerate(chips)]
        for cp in first:
            cp.start()
        passed = []
        for j, chip in enumerate(chips):
            for a in range(n):
                copy(a, 1 + j, (*chip, c), me).wait_recv()
                fwd = copy(a, 4 + j, (*chip, c), sibling)
                fwd.start()
                passed.append(fwd)
        for a in range(n):
            copy(a, 0, sibling, me).wait_recv()
            for j, chip in enumerate(chips):
                copy(a, 4 + j, (*chip, 1 - c), me).wait_recv()
        for cp in first + passed:
            cp.wait_send()
        for cp in mine:
            cp.wait()

    return pl.pallas_call(
        body, name=name,
        in_specs=[ANY] * n, out_specs=[ANY] * n,
        out_shape=[jax.ShapeDtypeStruct((N_DEV,) + s.shape, s.dtype) for s in slabs],
        scratch_shapes=[pltpu.SemaphoreType.DMA((n, 7)), pltpu.SemaphoreType.DMA((n, 7)), pltpu.SemaphoreType.DMA((n,))],
    )(*slabs)


def _all_to_all(slabs, name):
    n = len(slabs)

    def body(*refs):
        ins, outs = refs[:n], refs[n:2 * n]
        send_sems, recv_sems, local_sems = refs[2 * n:]
        x, y, c = lax.axis_index("x"), lax.axis_index("y"), lax.axis_index("c")
        my_slot = 4 * x + 2 * y + c
        flips = [(fx, fy, fc) for fx in (0, 1) for fy in (0, 1) for fc in (0, 1) if (fx, fy, fc) != (0, 0, 0)]

        def copy(a, k):
            fx, fy, fc = flips[k]
            peer = (x ^ fx, y ^ fy, c ^ fc)
            peer_slot = 4 * peer[0] + 2 * peer[1] + peer[2]
            send = pltpu.make_async_remote_copy(src_ref=ins[a].at[peer_slot], dst_ref=outs[a].at[my_slot], send_sem=send_sems.at[a, k],
                                                recv_sem=recv_sems.at[a, k], device_id=peer, device_id_type=MESH)
            recv = pltpu.make_async_remote_copy(src_ref=ins[a].at[peer_slot], dst_ref=outs[a].at[peer_slot], send_sem=send_sems.at[a, k],
                                                recv_sem=recv_sems.at[a, k], device_id=peer, device_id_type=MESH)
            return send, recv

        mine = [pltpu.make_async_copy(ins[a].at[my_slot], outs[a].at[my_slot], local_sems.at[a]) for a in range(n)]
        for cp in mine:
            cp.start()
        copies = [copy(a, k) for a in range(n) for k in range(len(flips))]
        for send, _ in copies:
            send.start()
        for send, recv in copies:
            recv.wait_recv()
            send.wait_send()
        for cp in mine:
            cp.wait()

    return pl.pallas_call(
        body, name=name,
        in_specs=[ANY] * n, out_specs=[ANY] * n,
        out_shape=[jax.ShapeDtypeStruct(s.shape, s.dtype) for s in slabs],
        scratch_shapes=[pltpu.SemaphoreType.DMA((n, 7)), pltpu.SemaphoreType.DMA((n, 7)), pltpu.SemaphoreType.DMA((n,))],
    )(*slabs)


def _row_tile(rows):
    for cand in range(256, 7, -8):
        if rows % cand == 0:
            return cand
    return rows


def _window(w):
    wp = max(-(-((w * r) % LANES + w) // LANES) for r in range(N_DEV)) * LANES
    assert all((w * r) // LANES * LANES + wp <= N_DEV * w for r in range(N_DEV))
    return wp


def _join_cols(slabs, name):
    _, R, w = slabs.shape
    tr = _row_tile(R)
    wp = _window(w)

    def body(g_ref, o_ref, pad_ref):
        if w % LANES == 0:
            for r in range(N_DEV):
                o_ref[:, w * r:w * (r + 1)] = g_ref[r]
            return
        o_ref[...] = jnp.zeros_like(o_ref)
        pad_ref[...] = jnp.zeros_like(pad_ref)
        for r in range(N_DEV):
            q, s = divmod(w * r, LANES)
            pad_ref[:, :w] = g_ref[r]
            y = pad_ref[...]
            if s:
                y = pltpu.roll(y, s, axis=1)
            o_ref[:, LANES * q:LANES * q + wp] += y

    return pl.pallas_call(
        body, name=name, grid=(R // tr,),
        in_specs=[pl.BlockSpec((N_DEV, tr, w), lambda i: (0, i, 0))], out_specs=pl.BlockSpec((tr, N_DEV * w), lambda i: (i, 0)),
        out_shape=jax.ShapeDtypeStruct((R, N_DEV * w), slabs.dtype), scratch_shapes=[pltpu.VMEM((tr, wp), slabs.dtype)],
        compiler_params=_params(("parallel",)),
    )(slabs)


def _split_cols(pieces, name):
    R = pieces[0].shape[0]
    widths = [p.shape[1] for p in pieces]
    total = sum(widths)
    w = total // N_DEV
    tr = _row_tile(R)
    wp = _window(w)
    offs = np.cumsum([0] + widths)
    dtype = pieces[0].dtype

    def body(*refs):
        ins, (o_ref, full_ref) = refs[:len(pieces)], refs[len(pieces):]
        for p_ref, a, b in zip(ins, offs[:-1], offs[1:]):
            full_ref[:, a:b] = p_ref[...].astype(dtype)
        for r in range(N_DEV):
            q, s = divmod(w * r, LANES)
            y = full_ref[:, LANES * q:LANES * q + wp]
            if s:
                y = pltpu.roll(y, wp - s, axis=1)
            o_ref[r] = y[:, :w]

    return pl.pallas_call(
        body, name=name, grid=(R // tr,),
        in_specs=[pl.BlockSpec((tr, n), lambda i: (i, 0)) for n in widths], out_specs=pl.BlockSpec((N_DEV, tr, w), lambda i: (0, i, 0)),
        out_shape=jax.ShapeDtypeStruct((N_DEV, R, w), dtype), scratch_shapes=[pltpu.VMEM((tr, total), dtype)],
        compiler_params=_params(("parallel",)),
    )(*pieces)


def _adamw(g, w, m, v):
    m_new = ADAM_B1 * m + (1.0 - ADAM_B1) * g
    v_new = ADAM_B2 * v + (1.0 - ADAM_B2) * jnp.square(g)
    m_hat = m_new / (1.0 - ADAM_B1 ** ADAM_STEP)
    v_hat = v_new / (1.0 - ADAM_B2 ** ADAM_STEP)
    return -ADAM_LR * (m_hat / (jnp.sqrt(v_hat) + ADAM_EPS) + ADAM_WD * w), m_new, v_new


def _sum_parts(p_ref):
    g = p_ref[0].astype(F32)
    for d in range(1, N_DEV):
        g = g + p_ref[d].astype(F32)
    return g


def _reduce_adamw(parts, w, m, v, name):
    R, C = w.shape
    tr = _row_tile(R)

    def body(p_ref, w_ref, m_ref, v_ref, g_ref, d_ref, mo_ref, vo_ref):
        g = _sum_parts(p_ref)
        g_ref[...] = g
        d_ref[...], mo_ref[...], vo_ref[...] = _adamw(g, w_ref[...], m_ref[...], v_ref[...])

    row = pl.BlockSpec((tr, C), lambda i: (i, 0))
    return pl.pallas_call(
        body, name=name, grid=(R // tr,),
        in_specs=[pl.BlockSpec((N_DEV, tr, C), lambda i: (0, i, 0)), row, row, row],
        out_specs=[row] * 4, out_shape=[jax.ShapeDtypeStruct((R, C), F32)] * 4,
        compiler_params=_params(("parallel",)),
    )(parts, w, m, v)


def _reduce_adamw_small(parts, ws, ms, vs):
    sizes = [a.shape[1] for a in ws]
    k = len(sizes)
    offs = np.cumsum([0] + [-(-n // LANES) * LANES for n in sizes])

    def body(*refs):
        p_ref, w_refs, m_refs, v_refs = refs[0], refs[1:1 + k], refs[1 + k:1 + 2 * k], refs[1 + 2 * k:1 + 3 * k]
        outs, loss_ref = refs[1 + 3 * k:-1], refs[-1]
        g_all = _sum_parts(p_ref)
        for j, n in enumerate(sizes):
            g = g_all[:, offs[j]:offs[j] + LANES * (-(-n // LANES))][:, :n]
            outs[4 * j][...] = g
            outs[4 * j + 1][...], outs[4 * j + 2][...], outs[4 * j + 3][...] = _adamw(g, w_refs[j][...], m_refs[j][...], v_refs[j][...])
        loss_ref[...] = g_all[:, offs[k]:offs[k] + LANES]

    vm = pl.BlockSpec(memory_space=pltpu.VMEM)
    out_shape = [jax.ShapeDtypeStruct((1, n), F32) for n in sizes for _ in range(4)] + [jax.ShapeDtypeStruct((1, LANES), F32)]
    res = pl.pallas_call(
        body, name="reduce_adamw_replicated", in_specs=[vm] * (1 + 3 * k), out_specs=[vm] * len(out_shape), out_shape=out_shape,
        compiler_params=_params(),
    )(parts, *ws, *ms, *vs)
    return [res[4 * j:4 * j + 4] for j in range(k)], res[-1]


COL_SHARDED = ("w_in", "w_branch_a", "w_branch_b", "w_up", "conv_w")
ROW_SHARDED = ("w_out", "w_down")
SMALL = ("b_gate", "sinks", "ln1_g", "ln1_b", "conv_b", "ln2_g", "ln2_b")
ORDER = ("w_in", "b_gate", "sinks", "w_branch_a", "w_branch_b", "w_out", "ln1_g", "ln1_b", "w_up", "conv_w", "conv_b", "w_down", "ln2_g", "ln2_b")


def _pad_lanes(a):
    pad = (-a.shape[-1]) % LANES
    return a if pad == 0 else jnp.pad(a, ((0, 0), (0, pad)))


def kernel(x, positions, w_in, b_gate, sinks, w_branch_a, w_branch_b, w_out, ln1_g, ln1_b, w_up, conv_w, conv_b, w_down, ln2_g, ln2_b, loss_target, m_w_in, m_b_gate, m_sinks, m_w_branch_a, m_w_branch_b, m_w_out, m_ln1_g, m_ln1_b, m_w_up, m_conv_w, m_conv_b, m_w_down, m_ln2_g, m_ln2_b, v_w_in, v_b_gate, v_sinks, v_w_branch_a, v_w_branch_b, v_w_out, v_ln1_g, v_ln1_b, v_w_up, v_conv_w, v_conv_b, v_w_down, v_ln2_g, v_ln2_b):
    args = dict(locals())
    sharded = COL_SHARDED + ROW_SHARDED
    w = {n: args[n][0] if n in sharded else args[n] for n in ORDER}
    m = {n: args["m_" + n][0] if n in sharded else args["m_" + n] for n in ORDER}
    v = {n: args["v_" + n][0] if n in sharded else args["v_" + n] for n in ORDER}

    travel = {n: (w[n] if n == "conv_w" else w[n].astype(BF16)) for n in sharded}
    (g_in,) = _all_gather([travel["w_in"]], "all_gather_w_in")
    w_in_full = _join_cols(g_in, "join_w_in")
    later = ("w_branch_a", "w_branch_b", "w_out", "w_up", "conv_w", "w_down")

    def join(name, slabs):
        return _join_cols(slabs, "join_" + name) if name in COL_SHARDED else slabs.reshape(-1, slabs.shape[-1])

    def split(name, grad):
        if name in COL_SHARDED:
            return _split_cols(grad if isinstance(grad, tuple) else (grad,), "split_d" + name)
        return grad.reshape((N_DEV, -1, grad.shape[-1]))

    def early_exchange(grads):
        return _Exchange([split(n, grads[n]) for n in later], ["scatter"] * len(later))

    def tail_exchange(grads, loss):
        small_pack = jnp.concatenate(
            [_pad_lanes(p) for n in SMALL for p in (grads[n] if isinstance(grads[n], tuple) else (grads[n],))] + [loss], axis=1)
        return _Exchange([split("w_in", grads["w_in"]), small_pack], ["scatter", "gather"])

    gather_later = _Exchange([travel[n] for n in later], ["gather"] * len(later))
    _, grad_x, _, early_out, (tail, tail_handle) = _local_step(
        x[0], positions[0], w_in_full, w["b_gate"], w["sinks"][0], w["ln1_g"], w["ln1_b"], w["conv_b"], w["ln2_g"], w["ln2_b"], loss_target[0],
        (gather_later, lambda arrived: [join(n, a) for n, a in zip(later, arrived)]), early_exchange, tail_exchange)

    res = {n: _reduce_adamw(a, w[n], m[n], v[n], "reduce_adamw_" + n) for n, a in zip(later, early_out)}
    recv_w_in, small_parts = _finish_exchange(tail, tail_handle, res[later[-1]][3], "finish_tail_exchange")
    me = 4 * lax.axis_index("x") + 2 * lax.axis_index("y") + lax.axis_index("c")
    sent_w_in, small_pack = tail.arrays
    recv_w_in = lax.dynamic_update_index_in_dim(recv_w_in, lax.dynamic_index_in_dim(sent_w_in, me, 0, keepdims=False), me, 0)
    small_parts = lax.dynamic_update_index_in_dim(small_parts, small_pack, me, 0)
    res["w_in"] = _reduce_adamw(recv_w_in, w["w_in"], m["w_in"], v["w_in"], "reduce_adamw_w_in")
    small_res, loss_sum = _reduce_adamw_small(small_parts, [w[n] for n in SMALL], [m[n] for n in SMALL], [v[n] for n in SMALL])
    res.update(zip(SMALL, small_res))
    out = [loss_sum[0, 0], grad_x[None]]
    for k in range(4):
        out += [res[n][k][None] if n in sharded else res[n][k] for n in ORDER]
    return tuple(out)
```

```python
import functools

import jax
import jax.numpy as jnp
import numpy as np
from jax import lax
from jax.experimental import pallas as pl
from jax.experimental.pallas import tpu as pltpu

D_MODEL = 1024
HEAD_DIM = 64
SWA_Q_HEADS = 8
SWA_KV_HEADS = 2
SB_HEADS = 8
WINDOW = 128
ROPE_THETA = 10000.0
D_FF = 2816
LN_EPS = 1e-5
DEPTH = 1
ALPHA = (2.0 * DEPTH) ** 0.25
SWA_Q_WIDTH = SWA_Q_HEADS * HEAD_DIM
SWA_KV_WIDTH = SWA_KV_HEADS * HEAD_DIM
SB_WIDTH = SB_HEADS * HEAD_DIM
GATE_WIDTH = 2 * D_MODEL
IN_WIDTHS = (SWA_Q_WIDTH, SWA_KV_WIDTH, SWA_KV_WIDTH, SB_WIDTH, SB_WIDTH, SB_WIDTH, GATE_WIDTH)
IN_TOTAL = sum(IN_WIDTHS)
ATTN_SCALE = HEAD_DIM ** -0.5

ADAM_LR = 0.001
ADAM_B1 = 0.9
ADAM_B2 = 0.999
ADAM_EPS = 1e-08
ADAM_WD = 0.01
ADAM_STEP = 10

N_DEV = 8
LANES = 128
SB_BLOCK = 256
VMEM_LIMIT = 56 * 1024 * 1024

F32 = jnp.float32
BF16 = jnp.bfloat16
ACT_DTYPE = BF16
MESH = pl.DeviceIdType.MESH


def _params(sem=None):
    return pltpu.CompilerParams(dimension_semantics=sem, vmem_limit_bytes=VMEM_LIMIT)


def _dot(a, b):
    return jnp.dot(a, b, preferred_element_type=F32)


def _dot_nt(a, b):
    return lax.dot_general(a, b, (((1,), (1,)), ((), ())), preferred_element_type=F32)


def _dot_tn(a, b):
    return lax.dot_general(a, b, (((0,), (0,)), ((), ())), preferred_element_type=F32)


def _split_bf16(v):
    hi = v.astype(BF16)
    lo = (v - hi.astype(F32)).astype(BF16)
    return hi, lo


def _matmul(a, b, *, kind, out_shape, grid, a_spec, b_spec, out_spec, name, add=None, add_spec=None, add_scale=1.0):
    dot = {"nn": _dot, "nt": _dot_nt, "tn": _dot_tn}[kind]

    def body(*refs):
        if add is None:
            a_ref, b_ref, o_ref = refs
        else:
            a_ref, b_ref, add_ref, o_ref = refs
        r = dot(a_ref[...].astype(BF16), b_ref[...].astype(BF16))
        if add is not None:
            r = r + add_scale * add_ref[...]
        o_ref[...] = r.astype(o_ref.dtype)

    ins = [a, b] + ([] if add is None else [add])
    specs = [a_spec, b_spec] + ([] if add is None else [add_spec])
    return pl.pallas_call(
        body, name=name, grid=grid, in_specs=specs, out_specs=out_spec, out_shape=out_shape,
        compiler_params=_params(("parallel",) * len(grid)),
    )(*ins)


def _rope_tables(pos_col, inv_freq_lanes):
    T = pos_col.shape[0]
    tm = min(512, T)

    def body(pos_ref, f_ref, cos_ref, sin_ref):
        ang = pos_ref[...].astype(F32) * f_ref[...]
        cos_ref[...] = jnp.cos(ang)
        sin_ref[...] = jnp.sin(ang)

    return pl.pallas_call(
        body, name="rope_tables", grid=(T // tm,),
        in_specs=[pl.BlockSpec((tm, 1), lambda i: (i, 0)), pl.BlockSpec((1, LANES), lambda i: (0, 0))],
        out_specs=[pl.BlockSpec((tm, LANES), lambda i: (i, 0))] * 2,
        out_shape=[jax.ShapeDtypeStruct((T, LANES), F32)] * 2,
        compiler_params=_params(("parallel",)),
    )(pos_col, inv_freq_lanes)


def _lane_iota(shape):
    return lax.broadcasted_iota(jnp.int32, shape, len(shape) - 1)


def _rot_half(t):
    first = (_lane_iota(t.shape) % HEAD_DIM) < (HEAD_DIM // 2)
    return jnp.where(first, -pltpu.roll(t, LANES - HEAD_DIM // 2, axis=1), pltpu.roll(t, HEAD_DIM // 2, axis=1))


def _rope(t, cos, sin):
    return t * cos + _rot_half(t) * sin


def _rope_transpose(d, cos, sin):
    return d * cos - _rot_half(d * sin)


_IN_DTYPES = (F32, F32, BF16, BF16, BF16, BF16, F32)


def _in_proj(x, w_in_b):
    T = x.shape[0]
    tm = min(256, T)
    offs = np.cumsum((0,) + IN_WIDTHS)

    def body(x_ref, w_ref, xb_ref, *outs):
        xb = x_ref[...].astype(BF16)
        xb_ref[...] = xb
        for o_ref, a, b in zip(outs, offs[:-1], offs[1:]):
            o_ref[...] = _dot(xb, w_ref[:, a:b]).astype(o_ref.dtype)

    row = lambda n: pl.BlockSpec((tm, n), lambda i: (i, 0))
    return pl.pallas_call(
        body, name="in_proj", grid=(T // tm,),
        in_specs=[row(D_MODEL), pl.BlockSpec((D_MODEL, IN_TOTAL), lambda i: (0, 0))],
        out_specs=[row(D_MODEL)] + [row(n) for n in IN_WIDTHS],
        out_shape=[jax.ShapeDtypeStruct((T, D_MODEL), BF16)] + [jax.ShapeDtypeStruct((T, n), dt) for n, dt in zip(IN_WIDTHS, _IN_DTYPES)],
        compiler_params=_params(("parallel",)),
    )(x, w_in_b)


def _swa_specs(T):
    blk = WINDOW
    cur = lambda n: pl.BlockSpec((blk, n), lambda i: (i, 0))
    prev = lambda n: pl.BlockSpec((blk, n), lambda i: (jnp.maximum(i - 1, 0), 0))
    return blk, cur, prev


SWA_GROUP = SWA_Q_HEADS // SWA_KV_HEADS


def _swa_stack(pairs):
    lane = _lane_iota(pairs[0].shape)
    zero = jnp.zeros((), pairs[0].dtype)
    rows = []
    for h in range(SWA_Q_HEADS):
        hh, g = h % 2, h // SWA_GROUP
        x = jnp.where((lane >= hh * HEAD_DIM) & (lane < (hh + 1) * HEAD_DIM), pairs[h // 2], zero)
        rows.append(x if hh == g else pltpu.roll(x, HEAD_DIM, axis=1))
    return jnp.concatenate(rows, axis=0)


def _swa_unstack(stacked, blk):
    low = _lane_iota((blk, LANES)) < HEAD_DIM
    pairs = []
    for pp in range(SWA_Q_HEADS // 2):
        halves = []
        for hh in range(2):
            h = 2 * pp + hh
            x = stacked[h * blk:(h + 1) * blk]
            halves.append(x if hh == h // SWA_GROUP else pltpu.roll(x, HEAD_DIM, axis=1))
        pairs.append(jnp.where(low, halves[0], halves[1]))
    return pairs


def _swa_probs(i, q_stack, kwin, sink_ref, blk):
    r = lax.broadcasted_iota(jnp.int32, (blk, 2 * blk), 0)
    c = lax.broadcasted_iota(jnp.int32, (blk, 2 * blk), 1)
    rel = blk + r - c
    valid = (rel >= 0) & (rel < WINDOW) & ((c >= blk) | (i > 0))
    bias = jnp.concatenate([jnp.where(valid, 0.0, -1e30)] * SWA_Q_HEADS, axis=0)
    head = lax.broadcasted_iota(jnp.int32, (SWA_Q_HEADS * blk, 1), 0) // blk
    sink = jnp.zeros((SWA_Q_HEADS * blk, 1), F32)
    for h in range(SWA_Q_HEADS):
        sink = jnp.where(head == h, sink_ref[h], sink)
    s = _dot_nt(q_stack, kwin) * ATTN_SCALE + bias
    m = jnp.maximum(jnp.max(s, axis=1, keepdims=True), sink)
    p = jnp.exp(s - m)
    es = jnp.exp(sink - m)
    den = jnp.sum(p, axis=1, keepdims=True) + es
    return p / den, es / den


def _swa_inputs(q_ref, kp_ref, kc_ref, vp_ref, vc_ref, cp_ref, cc_ref, sp_ref, sc_ref):
    cc, sc = cc_ref[...], sc_ref[...]
    kwin = jnp.concatenate([_rope(kp_ref[...], cp_ref[...], sp_ref[...]), _rope(kc_ref[...], cc, sc)], axis=0).astype(BF16)
    vwin = jnp.concatenate([vp_ref[...], vc_ref[...]], axis=0)
    q_stack = _swa_stack([_rope(q_ref[:, pp * LANES:(pp + 1) * LANES], cc, sc) for pp in range(SWA_Q_HEADS // 2)]).astype(BF16)
    return q_stack, kwin, vwin


def _swa_fwd(qa, ka, va, cos, sin, sinks):
    T = qa.shape[0]
    blk, cur, prev = _swa_specs(T)

    def body(sink_ref, q_ref, kp_ref, kc_ref, vp_ref, vc_ref, cp_ref, cc_ref, sp_ref, sc_ref, o_ref):
        q_stack, kwin, vwin = _swa_inputs(q_ref, kp_ref, kc_ref, vp_ref, vc_ref, cp_ref, cc_ref, sp_ref, sc_ref)
        probs, _ = _swa_probs(pl.program_id(0), q_stack, kwin, sink_ref, blk)
        for pp, tile in enumerate(_swa_unstack(_dot(probs.astype(BF16), vwin), blk)):
            o_ref[:, pp * LANES:(pp + 1) * LANES] = tile.astype(o_ref.dtype)

    return pl.pallas_call(
        body, name="swa_fwd", grid=(T // blk,),
        in_specs=[pl.BlockSpec(memory_space=pltpu.SMEM), cur(SWA_Q_WIDTH), prev(LANES), cur(LANES), prev(LANES), cur(LANES),
                  prev(LANES), cur(LANES), prev(LANES), cur(LANES)],
        out_specs=cur(SWA_Q_WIDTH),
        out_shape=jax.ShapeDtypeStruct((T, SWA_Q_WIDTH), BF16),
        compiler_params=_params(("parallel",)),
    )(sinks, qa, ka, ka, va, va, cos, cos, sin, sin)


def _swa_bwd(qa, ka, va, cos, sin, sinks, dya):
    T = qa.shape[0]
    blk, cur, prev = _swa_specs(T)
    full = lambda n: pl.BlockSpec((T, n), lambda i: (0, 0))

    def body(sink_ref, q_ref, kp_ref, kc_ref, vp_ref, vc_ref, cp_ref, cc_ref, sp_ref, sc_ref, do_ref,
             dq_ref, dk_out, dv_out, dsink_ref, dk_ref, dv_ref):
        i = pl.program_id(0)

        @pl.when(i == 0)
        def _():
            dk_ref[...] = jnp.zeros_like(dk_ref)
            dv_ref[...] = jnp.zeros_like(dv_ref)
            dsink_ref[...] = jnp.zeros_like(dsink_ref)

        cp, cc, sp, sc = cp_ref[...], cc_ref[...], sp_ref[...], sc_ref[...]
        q_stack, kwin, vwin = _swa_inputs(q_ref, kp_ref, kc_ref, vp_ref, vc_ref, cp_ref, cc_ref, sp_ref, sc_ref)
        probs, psink = _swa_probs(i, q_stack, kwin, sink_ref, blk)
        do_stack = _swa_stack([do_ref[:, pp * LANES:(pp + 1) * LANES] for pp in range(SWA_Q_HEADS // 2)])
        dp = _dot_nt(do_stack, vwin)
        dsum = jnp.sum(probs * dp, axis=1, keepdims=True)
        ds = (probs * (dp - dsum) * ATTN_SCALE).astype(BF16)
        for pp, tile in enumerate(_swa_unstack(_dot(ds, kwin), blk)):
            dq_ref[:, pp * LANES:(pp + 1) * LANES] = _rope_transpose(tile, cc, sc).astype(dq_ref.dtype)
        dkw = _dot_tn(ds, q_stack)
        dvw = _dot_tn(probs.astype(BF16), do_stack)
        lane1 = _lane_iota((1, LANES))
        sink_share = psink * dsum
        dsink = jnp.zeros((1, LANES), F32)
        for h in range(SWA_Q_HEADS):
            dsink = dsink + jnp.where(lane1 == h, -jnp.sum(sink_share[h * blk:(h + 1) * blk]), 0.0)
        dsink_ref[...] += dsink
        ip = jnp.maximum(i - 1, 0)
        rows_p = pl.ds(pl.multiple_of(ip * blk, blk), blk)
        rows_c = pl.ds(pl.multiple_of(i * blk, blk), blk)
        dk_ref[rows_p, :] += _rope_transpose(dkw[:blk], cp, sp)
        dv_ref[rows_p, :] += dvw[:blk]
        dk_ref[rows_c, :] += _rope_transpose(dkw[blk:], cc, sc)
        dv_ref[rows_c, :] += dvw[blk:]

        @pl.when(i == T // blk - 1)
        def _():
            dk_out[...] = dk_ref[...].astype(BF16)
            dv_out[...] = dv_ref[...].astype(BF16)

    return pl.pallas_call(
        body, name="swa_bwd", grid=(T // blk,),
        in_specs=[pl.BlockSpec(memory_space=pltpu.SMEM), cur(SWA_Q_WIDTH), prev(LANES), cur(LANES), prev(LANES), cur(LANES),
                  prev(LANES), cur(LANES), prev(LANES), cur(LANES), cur(SWA_Q_WIDTH)],
        out_specs=[cur(SWA_Q_WIDTH), full(LANES), full(LANES), pl.BlockSpec((1, LANES), lambda i: (0, 0))],
        out_shape=[jax.ShapeDtypeStruct((T, SWA_Q_WIDTH), BF16), jax.ShapeDtypeStruct((T, LANES), BF16),
                   jax.ShapeDtypeStruct((T, LANES), BF16), jax.ShapeDtypeStruct((1, LANES), F32)],
        scratch_shapes=[pltpu.VMEM((T, LANES), F32)] * 2,
        compiler_params=_params(("arbitrary",)),
    )(sinks, qa, ka, ka, va, va, cos, cos, sin, sin, dya)


class _Exchange:
    FLIPS = [(fx, fy, fc) for fx in (0, 1) for fy in (0, 1) for fc in (0, 1) if (fx, fy, fc) != (0, 0, 0)]

    def __init__(self, arrays, kinds):
        self.arrays, self.kinds, self.n = list(arrays), list(kinds), len(arrays)

    def out_shape(self):
        return [jax.ShapeDtypeStruct(a.shape if k == "scatter" else (N_DEV,) + a.shape, a.dtype) for a, k in zip(self.arrays, self.kinds)]

    def scratch(self):
        return [pltpu.SemaphoreType.DMA((self.n, 7)), pltpu.SemaphoreType.DMA((self.n, 7)), pltpu.SemaphoreType.DMA((self.n,))]

    def bind(self, ins, outs, send_sems, recv_sems, local_sems):
        x, y, c = lax.axis_index("x"), lax.axis_index("y"), lax.axis_index("c")
        me = 4 * x + 2 * y + c
        local, remote = [], []
        for a, kind in enumerate(self.kinds):
            mine = ins[a].at[me] if kind == "scatter" else ins[a]
            local.append(pltpu.make_async_copy(mine, outs[a].at[me], local_sems.at[a]))
            for k, (fx, fy, fc) in enumerate(self.FLIPS):
                peer = (x ^ fx, y ^ fy, c ^ fc)
                peer_slot = 4 * peer[0] + 2 * peer[1] + peer[2]
                src = ins[a].at[peer_slot] if kind == "scatter" else ins[a]
                sems = dict(send_sem=send_sems.at[a, k], recv_sem=recv_sems.at[a, k], device_id=peer, device_id_type=MESH)
                remote.append((pltpu.make_async_remote_copy(src_ref=src, dst_ref=outs[a].at[me], **sems),
                               pltpu.make_async_remote_copy(src_ref=src, dst_ref=outs[a].at[peer_slot], **sems)))

        def start():
            for cp in local:
                cp.start()
            for send, _ in remote:
                send.start()

        def wait():
            for send, arrival in remote:
                arrival.wait_recv()
                send.wait_send()
            for cp in local:
                cp.wait()

        return start, wait


def _hosted_call(body, name, grid, exchange, *, in_specs, out_specs, out_shape, semantics, args, scratch=(), prefetch=()):
    n = 0 if exchange is None else exchange.n
    n_pre, n_in, n_out, n_scratch = len(prefetch), len(in_specs), len(out_specs), len(scratch)

    def hosted(*refs):
        pre, rest = refs[:n_pre], refs[n_pre:]
        ins, rest = rest[:n_in], rest[n_in:]
        ex_ins, rest = rest[:n], rest[n:]
        outs, rest = rest[:n_out], rest[n_out:]
        ex_outs, rest = rest[:n], rest[n:]
        own, sems = rest[:n_scratch], rest[n_scratch:]
        if exchange is None:
            return body(*pre, *ins, *outs, *own)
        start, wait = exchange.bind(ex_ins, ex_outs, *sems)
        ids = [pl.program_id(d) for d in range(len(grid))]
        first = functools.reduce(jnp.logical_and, [i == 0 for i in ids])
        last = functools.reduce(jnp.logical_and, [i == g - 1 for i, g in zip(ids, grid)])
        pl.when(first)(start)
        body(*pre, *ins, *outs, *own)
        pl.when(last)(wait)

    grid_spec = pltpu.PrefetchScalarGridSpec(
        num_scalar_prefetch=n_pre, grid=grid, in_specs=list(in_specs) + [ANY] * n, out_specs=list(out_specs) + [ANY] * n,
        scratch_shapes=list(scratch) + ([] if exchange is None else exchange.scratch()))
    res = pl.pallas_call(
        hosted, name=name, grid_spec=grid_spec, out_shape=list(out_shape) + ([] if exchange is None else exchange.out_shape()),
        compiler_params=_params(semantics if exchange is None else ("arbitrary",) * len(grid)),
    )(*prefetch, *args, *([] if exchange is None else exchange.arrays))
    return res[:n_out], res[n_out:]


SOFTPLUS_LINEAR_FROM = 30.0


def _sb_scores(qm, k, valid):
    z = _dot_nt(qm, k)
    sp = jnp.where(z > SOFTPLUS_LINEAR_FROM, z, jnp.log(1.0 + jnp.exp(z)))
    log_beta = z - sp
    if valid is not None:
        sp = jnp.where(valid, sp, 0.0)
    return sp, log_beta


def _tri2(B, cmp):
    r = lax.broadcasted_iota(jnp.int32, (2 * B, B), 0) % B
    c = lax.broadcasted_iota(jnp.int32, (2 * B, B), 1)
    return cmp(r, c).astype(BF16)


def _tri_sum(v, tri2):
    hi, lo = _split_bf16(v)
    return _dot(jnp.concatenate([hi, lo], axis=1), tri2)


def _head_masks(x):
    low = _lane_iota(x.shape) < HEAD_DIM
    zero = jnp.zeros((), x.dtype)
    return jnp.where(low, x, zero), jnp.where(low, zero, x)


def _strictly_below(B):
    r = lax.broadcasted_iota(jnp.int32, (B, B), 0)
    c = lax.broadcasted_iota(jnp.int32, (B, B), 1)
    return c < r


def _sb_grid(T, descending):
    B = min(SB_BLOCK, T)
    n = T // B
    pairs = [(i, j) for i in range(n) for j in (range(i, -1, -1) if descending else range(i + 1))]
    return B, jnp.asarray([p[0] for p in pairs], jnp.int32), jnp.asarray([p[1] for p in pairs], jnp.int32)


N_PAIRS = SB_HEADS // 2
PAIR_COLS = [slice(p * LANES, (p + 1) * LANES) for p in range(N_PAIRS)]


def _sb_fwd(qb, kb, vb, exchange=None):
    T = qb.shape[0]
    B, i_tab, j_tab = _sb_grid(T, descending=True)
    n = T // B

    def body(i_ref, j_ref, q_ref, k_ref, v_ref, o_ref, a_ref, b_ref, acc_ref, c_ref, tri_ref):
        s = pl.program_id(0)
        i, j = i_ref[s], j_ref[s]

        @pl.when(s == 0)
        def _():
            tri_ref[...] = _tri2(B, lambda r, c: r > c)

        @pl.when(j == i)
        def _():
            acc_ref[...] = jnp.zeros_like(acc_ref)
            c_ref[...] = jnp.zeros_like(c_ref)

        def block(valid):
            for p, cols in enumerate(PAIR_COLS):
                qms = _head_masks(q_ref[:, cols] * ATTN_SCALE)
                k = k_ref[:, cols]
                probs = []
                for hh in range(2):
                    h = 2 * p + hh
                    sp, lb = _sb_scores(qms[hh], k, valid)
                    c = c_ref[h]
                    a = jnp.exp(lb - (c + _tri_sum(sp, tri_ref[...])))
                    beta = jnp.exp(lb)
                    if valid is not None:
                        a = jnp.where(valid, a, 0.0)
                        beta = jnp.where(valid, beta, 0.0)
                    probs.append(a.astype(BF16))
                    a_ref[h] = probs[-1]
                    b_ref[h] = beta.astype(BF16)
                    c_ref[h] = c + jnp.sum(sp, axis=1, keepdims=True)
                acc_ref[:, cols] += _dot(jnp.concatenate(probs, axis=1), jnp.concatenate(_head_masks(v_ref[:, cols]), axis=0))

        pl.when(j == i)(lambda: block(_strictly_below(B)))
        pl.when(j != i)(lambda: block(None))

        @pl.when(j == 0)
        def _():
            o_ref[...] = acc_ref[...].astype(o_ref.dtype)

    q_spec = pl.BlockSpec((B, SB_WIDTH), lambda s, i_ref, j_ref: (i_ref[s], 0))
    k_spec = pl.BlockSpec((B, SB_WIDTH), lambda s, i_ref, j_ref: (j_ref[s], 0))
    tile = pl.BlockSpec((None, None, SB_HEADS, B, B), lambda s, i_ref, j_ref: (i_ref[s], j_ref[s], 0, 0, 0))
    saved = jax.ShapeDtypeStruct((n, n, SB_HEADS, B, B), BF16)
    return _hosted_call(
        body, "sb_fwd", (int(i_tab.shape[0]),), exchange, prefetch=(i_tab, j_tab),
        in_specs=[q_spec, k_spec, k_spec], out_specs=[q_spec, tile, tile],
        out_shape=[jax.ShapeDtypeStruct((T, SB_WIDTH), BF16), saved, saved],
        scratch=[pltpu.VMEM((B, SB_WIDTH), F32), pltpu.VMEM((SB_HEADS, B, 1), F32), pltpu.VMEM((2 * B, B), BF16)],
        semantics=("arbitrary",), args=(qb, kb, vb))


def _sb_bwd(qb, kb, vb, probs, betas, dyb, exchange=None):
    T = qb.shape[0]
    B, i_tab, j_tab = _sb_grid(T, descending=False)
    n_steps = int(i_tab.shape[0])

    def block_diag_t(x):
        xt = x.T
        top = lax.broadcasted_iota(jnp.int32, xt.shape, 0) < HEAD_DIM
        zero = jnp.zeros((), x.dtype)
        return jnp.concatenate([jnp.where(top, xt, zero), jnp.where(top, zero, xt)], axis=1)

    def body(i_ref, j_ref, q_ref, k_ref, v_ref, a_ref, b_ref, do_ref, dq_ref, dk_out, dv_out,
             dq_acc, cg_ref, dkt_ref, dvt_ref, tri_ref, qt_ref, dot_ref):
        s = pl.program_id(0)
        i, j = i_ref[s], j_ref[s]

        @pl.when(s == 0)
        def _():
            dkt_ref[...] = jnp.zeros_like(dkt_ref)
            dvt_ref[...] = jnp.zeros_like(dvt_ref)
            tri_ref[...] = _tri2(B, lambda r, c: r < c)[:B]

        @pl.when(j == 0)
        def _():
            dq_acc[...] = jnp.zeros_like(dq_acc)
            cg_ref[...] = jnp.zeros_like(cg_ref)
            for p, cols in enumerate(PAIR_COLS):
                qt_ref[p] = block_diag_t(q_ref[:, cols] * ATTN_SCALE)
                dot_ref[p] = block_diag_t(do_ref[:, cols])

        for p, cols in enumerate(PAIR_COLS):
            doms = _head_masks(do_ref[:, cols])
            k, v = k_ref[:, cols], v_ref[:, cols]
            dzs = []
            for hh in range(2):
                h = 2 * p + hh
                g = a_ref[h].astype(F32) * _dot_nt(doms[hh], v)
                cg = cg_ref[h]
                gsum = g + (cg + _dot(g.astype(BF16), tri_ref[...]))
                dzs.append((g - b_ref[h].astype(F32) * gsum).astype(BF16))
                cg_ref[h] = cg + jnp.sum(g, axis=1, keepdims=True)
            dq_acc[:, cols] += _dot(jnp.concatenate(dzs, axis=1), jnp.concatenate(_head_masks(k), axis=0))
            dkt_ref[j, cols, :] += _dot(qt_ref[p], jnp.concatenate(dzs, axis=0))
            dvt_ref[j, cols, :] += _dot(dot_ref[p], jnp.concatenate([a_ref[2 * p], a_ref[2 * p + 1]], axis=0))

        @pl.when(j == i)
        def _():
            dq_ref[...] = (dq_acc[...] * ATTN_SCALE).astype(dq_ref.dtype)

        @pl.when(s == n_steps - 1)
        def _():
            for jb in range(T // B):
                dk_out[jb * B:(jb + 1) * B, :] = dkt_ref[jb].T.astype(BF16)
                dv_out[jb * B:(jb + 1) * B, :] = dvt_ref[jb].T.astype(BF16)

    q_spec = pl.BlockSpec((B, SB_WIDTH), lambda s, i_ref, j_ref: (i_ref[s], 0))
    k_spec = pl.BlockSpec((B, SB_WIDTH), lambda s, i_ref, j_ref: (j_ref[s], 0))
    tile = pl.BlockSpec((None, None, SB_HEADS, B, B), lambda s, i_ref, j_ref: (i_ref[s], j_ref[s], 0, 0, 0))
    full = pl.BlockSpec((T, SB_WIDTH), lambda s, i_ref, j_ref: (0, 0))
    return _hosted_call(
        body, "sb_bwd", (n_steps,), exchange, prefetch=(i_tab, j_tab),
        in_specs=[q_spec, k_spec, k_spec, tile, tile, q_spec], out_specs=[q_spec, full, full],
        out_shape=[jax.ShapeDtypeStruct((T, SB_WIDTH), BF16)] * 3,
        scratch=[pltpu.VMEM((B, SB_WIDTH), F32), pltpu.VMEM((SB_HEADS, B, 1), F32), pltpu.VMEM((T // B, SB_WIDTH, B), F32),
                 pltpu.VMEM((T // B, SB_WIDTH, B), F32), pltpu.VMEM((B, B), BF16), pltpu.VMEM((N_PAIRS, LANES, 2 * B), BF16),
                 pltpu.VMEM((N_PAIRS, LANES, 2 * B), BF16)],
        semantics=("arbitrary",), args=(qb, kb, vb, probs, betas, dyb))


def _ln_stats(u):
    mu = jnp.mean(u, axis=-1, keepdims=True)
    xc = u - mu
    var = jnp.mean(xc * xc, axis=-1, keepdims=True)
    rstd = lax.rsqrt(var + LN_EPS)
    return xc * rstd, rstd


def _ln_bwd(dy, xhat, rstd, g):
    dxh = dy * g
    return rstd * (dxh - jnp.mean(dxh, axis=-1, keepdims=True) - xhat * jnp.mean(dxh * xhat, axis=-1, keepdims=True))


def _gates(gl_ref, bg_ref):
    ga = jax.nn.sigmoid(gl_ref[:, :D_MODEL] + bg_ref[:, :D_MODEL])
    gb = jax.nn.sigmoid(gl_ref[:, D_MODEL:] + bg_ref[:, D_MODEL:])
    return ga, gb


def _mix_fwd(ya, yb, gl, x, wa, wb, wo, b_gate, ln1_g, ln1_b):
    T = x.shape[0]
    tm = min(256, T)

    def body(ya_ref, yb_ref, gl_ref, x_ref, wa_ref, wb_ref, wo_ref, bg_ref, g_ref, b_ref, h_ref, u_ref, x1_ref):
        ga, gb = _gates(gl_ref, bg_ref)
        h = (ga * _dot(ya_ref[...], wa_ref[...]) + gb * _dot(yb_ref[...], wb_ref[...])).astype(BF16)
        h_ref[...] = h
        u = ALPHA * x_ref[...] + _dot(h, wo_ref[...])
        u_ref[...] = u
        xhat, _ = _ln_stats(u)
        x1_ref[...] = (xhat * g_ref[...] + b_ref[...]).astype(BF16)

    row = lambda n: pl.BlockSpec((tm, n), lambda i: (i, 0))
    const = lambda r, n: pl.BlockSpec((r, n), lambda i: (0, 0))
    return pl.pallas_call(
        body, name="mix_fwd", grid=(T // tm,),
        in_specs=[row(SWA_Q_WIDTH), row(SB_WIDTH), row(GATE_WIDTH), row(D_MODEL), const(SWA_Q_WIDTH, D_MODEL), const(SB_WIDTH, D_MODEL),
                  const(D_MODEL, D_MODEL), const(1, GATE_WIDTH), const(1, D_MODEL), const(1, D_MODEL)],
        out_specs=[row(D_MODEL)] * 3,
        out_shape=[jax.ShapeDtypeStruct((T, D_MODEL), BF16), jax.ShapeDtypeStruct((T, D_MODEL), F32), jax.ShapeDtypeStruct((T, D_MODEL), BF16)],
        compiler_params=_params(("parallel",)),
    )(ya, yb, gl, x, wa, wb, wo, b_gate, ln1_g, ln1_b)


def _mix_bwd(du1, ya, yb, gl, wa, wb, wo, b_gate):
    T = du1.shape[0]
    tm = min(256, T)

    def body(du_ref, ya_ref, yb_ref, gl_ref, wa_ref, wb_ref, wo_ref, bg_ref, dya_ref, dyb_ref, dgl_ref, dta_ref, dtb_ref, dbg_ref):
        @pl.when(pl.program_id(0) == 0)
        def _():
            dbg_ref[...] = jnp.zeros_like(dbg_ref)

        dh = _dot_nt(du_ref[...].astype(BF16), wo_ref[...])
        ga, gb = _gates(gl_ref, bg_ref)
        for gate, y_ref, w_ref, dy_ref, dt_ref, lo in ((ga, ya_ref, wa_ref, dya_ref, dta_ref, 0), (gb, yb_ref, wb_ref, dyb_ref, dtb_ref, D_MODEL)):
            t = _dot(y_ref[...], w_ref[...])
            dlogit = dh * t * gate * (1.0 - gate)
            dgl_ref[:, lo:lo + D_MODEL] = dlogit.astype(BF16)
            dbg_ref[:, lo:lo + D_MODEL] += jnp.sum(dlogit, axis=0, keepdims=True)
            dt = (dh * gate).astype(BF16)
            dt_ref[...] = dt
            dy_ref[...] = _dot_nt(dt, w_ref[...]).astype(BF16)

    row = lambda n: pl.BlockSpec((tm, n), lambda i: (i, 0))
    const = lambda r, n: pl.BlockSpec((r, n), lambda i: (0, 0))
    sds = lambda n, dt: jax.ShapeDtypeStruct((T, n), dt)
    return pl.pallas_call(
        body, name="mix_bwd", grid=(T // tm,),
        in_specs=[row(D_MODEL), row(SWA_Q_WIDTH), row(SB_WIDTH), row(GATE_WIDTH), const(SWA_Q_WIDTH, D_MODEL), const(SB_WIDTH, D_MODEL),
                  const(D_MODEL, D_MODEL), const(1, GATE_WIDTH)],
        out_specs=[row(SWA_Q_WIDTH), row(SB_WIDTH), row(GATE_WIDTH), row(D_MODEL), row(D_MODEL), const(1, GATE_WIDTH)],
        out_shape=[sds(SWA_Q_WIDTH, BF16), sds(SB_WIDTH, BF16), sds(GATE_WIDTH, BF16), sds(D_MODEL, BF16), sds(D_MODEL, BF16),
                   jax.ShapeDtypeStruct((1, GATE_WIDTH), F32)],
        compiler_params=_params(("arbitrary",)),
    )(du1, ya, yb, gl, wa, wb, wo, b_gate)


CONV_COLS = LANES


CONV_CHUNK = 64
CONV_CHUNK_FWD = 256
HALO = 8


def _taps(ref, r0, rows, lead):
    return [ref[pl.ds(r0 + lead + k, rows), :] for k in ((-2, -1, 0) if lead else (0, 1, 2))]


def _chunks(T, rows, step, init=None):
    def body(c, carry):
        out = step(pl.multiple_of(c * rows, rows), *(() if init is None else (carry,)))
        return carry if init is None else out
    return lax.fori_loop(0, T // rows, body, 0 if init is None else init)


def _conv_chunk(taps, w_ref, b_ref):
    return w_ref[0:1, :] * taps[0] + w_ref[1:2, :] * taps[1] + w_ref[2:3, :] * taps[2] + b_ref[...]


def _fold(x):
    return jnp.sum(x.reshape(x.shape[0] // 8, 8, x.shape[1]), axis=0)


def _conv_specs(T):
    nb = D_FF // CONV_COLS
    pair = pl.BlockSpec((2, T, CONV_COLS), lambda j: (0, 0, j))
    gate = lambda r: pl.BlockSpec((r, CONV_COLS), lambda j: (0, j))
    up = lambda r: pl.BlockSpec((r, CONV_COLS), lambda j: (0, j + nb))
    return nb, pair, gate, up


def _conv_glu_fwd(p3, conv_w, conv_b):
    T = p3.shape[1]
    nb, pair, gate, up = _conv_specs(T)

    R = min(CONV_CHUNK_FWD, T)

    def body(p_ref, wg_ref, wu_ref, bg_ref, bu_ref, s_ref, pg_s, pu_s):
        for half, scr in enumerate((pg_s, pu_s)):
            scr[0:HALO, :] = jnp.zeros((HALO, CONV_COLS), F32)
            scr[HALO:HALO + T, :] = p_ref[half].astype(F32)
        def step(r0):
            ag = _conv_chunk(_taps(pg_s, r0, R, HALO), wg_ref, bg_ref)
            au = _conv_chunk(_taps(pu_s, r0, R, HALO), wu_ref, bu_ref)
            s_ref[pl.ds(r0, R), :] = (ag * jax.nn.sigmoid(ag) * au).astype(BF16)

        _chunks(T, R, step)

    return pl.pallas_call(
        body, name="conv_glu_fwd", grid=(nb,),
        in_specs=[pair, gate(3), up(3), gate(1), up(1)],
        out_specs=pl.BlockSpec((T, CONV_COLS), lambda j: (0, j)),
        out_shape=jax.ShapeDtypeStruct((T, D_FF), BF16),
        scratch_shapes=[pltpu.VMEM((T + HALO, CONV_COLS), F32)] * 2,
        compiler_params=_params(("parallel",)),
    )(p3, conv_w, conv_w, conv_b, conv_b)


def _conv_glu_bwd(p3, ds, conv_w, conv_b):
    T = p3.shape[1]
    nb, pair, gate, up = _conv_specs(T)

    R = min(CONV_CHUNK, T)

    def body(p_ref, ds_ref, wg_ref, wu_ref, bg_ref, bu_ref, dp_ref, dwg_ref, dwu_ref, dbg_ref, dbu_ref, pg_s, pu_s, dag_s, dau_s):
        for half, scr in enumerate((pg_s, pu_s)):
            scr[0:HALO, :] = jnp.zeros((HALO, CONV_COLS), F32)
            scr[HALO:HALO + T, :] = p_ref[half].astype(F32)
        for scr in (dag_s, dau_s):
            scr[T:T + HALO, :] = jnp.zeros((HALO, CONV_COLS), F32)
        halves = ((pg_s, dag_s, wg_ref, dwg_ref, dbg_ref), (pu_s, dau_s, wu_ref, dwu_ref, dbu_ref))

        def step(r0, sums):
            taps = [_taps(p_s, r0, R, HALO) for p_s, *_ in halves]
            ag = _conv_chunk(taps[0], wg_ref, bg_ref)
            au = _conv_chunk(taps[1], wu_ref, bu_ref)
            sg = jax.nn.sigmoid(ag)
            d = ds_ref[pl.ds(r0, R), :].astype(F32)
            das = (d * au * (sg * (1.0 + ag * (1.0 - sg))), d * ag * sg)
            out = []
            for half, (_, da_s, *_) in enumerate(halves):
                da_s[pl.ds(r0, R), :] = das[half]
                out.append(tuple(sums[half][k] + _fold(das[half] * taps[half][k]) for k in range(3)) + (sums[half][3] + _fold(das[half]),))
            return tuple(out)

        sums = _chunks(T, R, step, ((jnp.zeros((8, CONV_COLS), F32),) * 4,) * 2)
        for half, (_, da_s, w_ref, dw_ref, db_ref) in enumerate(halves):
            for k in range(3):
                dw_ref[k:k + 1, :] = jnp.sum(sums[half][k], axis=0, keepdims=True)
            db_ref[...] = jnp.sum(sums[half][3], axis=0, keepdims=True)

            def transposed(r0, da_s=da_s, w_ref=w_ref, half=half):
                da0, da1, da2 = _taps(da_s, r0, R, 0)
                dp_ref[half, pl.ds(r0, R), :] = (w_ref[2:3, :] * da0 + w_ref[1:2, :] * da1 + w_ref[0:1, :] * da2).astype(BF16)

            _chunks(T, R, transposed)

    col = lambda r: pl.BlockSpec((r, CONV_COLS), lambda j: (0, j))
    return pl.pallas_call(
        body, name="conv_glu_bwd", grid=(nb,),
        in_specs=[pair, col(T), gate(3), up(3), gate(1), up(1)],
        out_specs=[pair, col(3), col(3), col(1), col(1)],
        out_shape=[jax.ShapeDtypeStruct((2, T, D_FF), BF16), jax.ShapeDtypeStruct((3, D_FF), F32), jax.ShapeDtypeStruct((3, D_FF), F32),
                   jax.ShapeDtypeStruct((1, D_FF), F32), jax.ShapeDtypeStruct((1, D_FF), F32)],
        scratch_shapes=[pltpu.VMEM((T + HALO, CONV_COLS), F32)] * 4,
        compiler_params=_params(("parallel",)),
    )(p3, ds, conv_w, conv_w, conv_b, conv_b)


def _ffn_down_loss(s, w_down, u1, ln1_g, ln1_b, ln2_g, ln2_b, target):
    T = u1.shape[0]
    tm = min(256, T)

    def body(s_ref, w_ref, u1_ref, g1_ref, b1_ref, g2_ref, b2_ref, t_ref, du_ref, dub_ref, dg_ref, db_ref, loss_ref):
        @pl.when(pl.program_id(0) == 0)
        def _():
            dg_ref[...] = jnp.zeros_like(dg_ref)
            db_ref[...] = jnp.zeros_like(db_ref)
            loss_ref[...] = jnp.zeros_like(loss_ref)

        xh1, _ = _ln_stats(u1_ref[...])
        x1 = xh1 * g1_ref[...] + b1_ref[...]
        u2 = ALPHA * x1 + _dot(s_ref[...], w_ref[...])
        xh2, rstd2 = _ln_stats(u2)
        err = xh2 * g2_ref[...] + b2_ref[...] - t_ref[...]
        per_token = jnp.mean(err * err, axis=-1, keepdims=True)
        loss_ref[...] += 0.5 * jnp.sum(per_token, axis=0, keepdims=True)
        dy = err * (1.0 / D_MODEL)
        dg_ref[...] += jnp.sum(dy * xh2, axis=0, keepdims=True)
        db_ref[...] += jnp.sum(dy, axis=0, keepdims=True)
        du2 = _ln_bwd(dy, xh2, rstd2, g2_ref[...])
        du_ref[...] = du2
        dub_ref[...] = du2.astype(BF16)

    row = lambda n: pl.BlockSpec((tm, n), lambda i: (i, 0))
    const = lambda r, n: pl.BlockSpec((r, n), lambda i: (0, 0))
    vec = const(1, D_MODEL)
    return pl.pallas_call(
        body, name="ffn_down_loss", grid=(T // tm,),
        in_specs=[row(D_FF), const(D_FF, D_MODEL), row(D_MODEL), vec, vec, vec, vec, row(D_MODEL)],
        out_specs=[row(D_MODEL), row(D_MODEL), vec, vec, const(1, LANES)],
        out_shape=[jax.ShapeDtypeStruct((T, D_MODEL), F32), jax.ShapeDtypeStruct((T, D_MODEL), BF16), jax.ShapeDtypeStruct((1, D_MODEL), F32),
                   jax.ShapeDtypeStruct((1, D_MODEL), F32), jax.ShapeDtypeStruct((1, LANES), F32)],
        compiler_params=_params(("arbitrary",)),
    )(s, w_down, u1, ln1_g, ln1_b, ln2_g, ln2_b, target)


def _ffn_up_bwd_ln1(dp3, w_up, du2, u1, ln1_g):
    T = u1.shape[0]
    tm = min(256, T)

    def body(dp_ref, w_ref, du2_ref, u1_ref, g_ref, du_ref, dub_ref, dg_ref, db_ref):
        @pl.when(pl.program_id(0) == 0)
        def _():
            dg_ref[...] = jnp.zeros_like(dg_ref)
            db_ref[...] = jnp.zeros_like(db_ref)

        dx1 = _dot_nt(dp_ref[0], w_ref[:, :D_FF]) + _dot_nt(dp_ref[1], w_ref[:, D_FF:]) + ALPHA * du2_ref[...]
        xh, rstd = _ln_stats(u1_ref[...])
        dg_ref[...] += jnp.sum(dx1 * xh, axis=0, keepdims=True)
        db_ref[...] += jnp.sum(dx1, axis=0, keepdims=True)
        du1 = _ln_bwd(dx1, xh, rstd, g_ref[...])
        du_ref[...] = du1
        dub_ref[...] = du1.astype(BF16)

    row = lambda n: pl.BlockSpec((tm, n), lambda i: (i, 0))
    const = lambda r, n: pl.BlockSpec((r, n), lambda i: (0, 0))
    vec = const(1, D_MODEL)
    return pl.pallas_call(
        body, name="ffn_up_bwd_ln1", grid=(T // tm,),
        in_specs=[pl.BlockSpec((2, tm, D_FF), lambda i: (0, i, 0)), const(D_MODEL, 2 * D_FF), row(D_MODEL), row(D_MODEL), vec],
        out_specs=[row(D_MODEL), row(D_MODEL), vec, vec],
        out_shape=[jax.ShapeDtypeStruct((T, D_MODEL), F32), jax.ShapeDtypeStruct((T, D_MODEL), BF16), jax.ShapeDtypeStruct((1, D_MODEL), F32),
                   jax.ShapeDtypeStruct((1, D_MODEL), F32)],
        compiler_params=_params(("arbitrary",)),
    )(dp3, w_up, du2, u1, ln1_g)


def _local_step(x, positions, w_in, b_gate, sinks, ln1_g, ln1_b, conv_b, ln2_g, ln2_b, target, later_weights,
                early_exchange=None, tail_exchange=None):
    T = x.shape[0]
    inv_freq = 1.0 / (ROPE_THETA ** (jnp.arange(0, HEAD_DIM, 2, dtype=F32) / HEAD_DIM))
    cos, sin = _rope_tables(positions.reshape(T, 1), jnp.tile(inv_freq, LANES // (HEAD_DIM // 2)).reshape(1, LANES))

    xb, qa, ka, va, qb, kb, vb, gl = _in_proj(x, w_in)
    ya = _swa_fwd(qa, ka, va, cos, sin, sinks)
    if isinstance(later_weights, tuple):
        exchange, finish = later_weights
        (yb, probs, betas), arrived = _sb_fwd(qb, kb, vb, exchange)
        later_weights = finish(arrived)
    else:
        (yb, probs, betas), _ = _sb_fwd(qb, kb, vb)
    wa, wb, wo, w_up, conv_w, w_down = later_weights
    h, u1, x1 = _mix_fwd(ya, yb, gl, x, wa, wb, wo, b_gate, ln1_g, ln1_b)

    ff_tn = D_FF // 2
    nff = D_FF // ff_tn
    tm = min(1024, T)
    p3 = _matmul(x1, w_up, kind="nn", name="ffn_up", grid=(T // tm, 2 * nff),
                 a_spec=pl.BlockSpec((tm, D_MODEL), lambda i, j: (i, 0)), b_spec=pl.BlockSpec((D_MODEL, ff_tn), lambda i, j: (0, j)),
                 out_spec=pl.BlockSpec((None, tm, ff_tn), lambda i, j: (j // nff, i, j % nff)),
                 out_shape=jax.ShapeDtypeStruct((2, T, D_FF), ACT_DTYPE))
    s = _conv_glu_fwd(p3, conv_w, conv_b)
    du2, du2b, dln2_g, dln2_b, loss = _ffn_down_loss(s, w_down, u1, ln1_g, ln1_b, ln2_g, ln2_b, target)

    ds = _matmul(du2b, w_down, kind="nt", name="ffn_down_bwd", grid=(T // tm, nff),
                 a_spec=pl.BlockSpec((tm, D_MODEL), lambda i, j: (i, 0)), b_spec=pl.BlockSpec((ff_tn, D_MODEL), lambda i, j: (j, 0)),
                 out_spec=pl.BlockSpec((tm, ff_tn), lambda i, j: (i, j)), out_shape=jax.ShapeDtypeStruct((T, D_FF), ACT_DTYPE))
    dp3, dcw_g, dcw_u, dcb_g, dcb_u = _conv_glu_bwd(p3, ds, conv_w, conv_b)
    tk = 256
    dw_down = _matmul(s, du2b, kind="tn", name="dw_down", grid=(D_FF // tk,),
                      a_spec=pl.BlockSpec((T, tk), lambda i: (0, i)), b_spec=pl.BlockSpec((T, D_MODEL), lambda i: (0, 0)),
                      out_spec=pl.BlockSpec((tk, D_MODEL), lambda i: (i, 0)), out_shape=jax.ShapeDtypeStruct((D_FF, D_MODEL), BF16))
    dw_up = _matmul(x1, dp3, kind="tn", name="dw_up", grid=(D_MODEL // 512, 2 * nff),
                    a_spec=pl.BlockSpec((T, 512), lambda i, j: (0, i)), b_spec=pl.BlockSpec((None, T, ff_tn), lambda i, j: (j // nff, 0, j % nff)),
                    out_spec=pl.BlockSpec((512, ff_tn), lambda i, j: (i, j)), out_shape=jax.ShapeDtypeStruct((D_MODEL, 2 * D_FF), BF16))
    du1, du1b, dln1_g, dln1_b = _ffn_up_bwd_ln1(dp3, w_up, du2, u1, ln1_g)
    dya, dyb, dgl, dta, dtb, db_gate = _mix_bwd(du1, ya, yb, gl, wa, wb, wo, b_gate)

    def dw_tn(a, g, name):
        rows, cols = a.shape[1], g.shape[1]
        tn = min(512, cols)
        return _matmul(a, g, kind="tn", name=name, grid=(rows // 512, cols // tn),
                       a_spec=pl.BlockSpec((T, 512), lambda i, j: (0, i)), b_spec=pl.BlockSpec((T, tn), lambda i, j: (0, j)),
                       out_spec=pl.BlockSpec((512, tn), lambda i, j: (i, j)), out_shape=jax.ShapeDtypeStruct((rows, cols), BF16))

    dwa = dw_tn(ya, dta, "dw_branch_a")
    dwb = dw_tn(yb, dtb, "dw_branch_b")
    dwo = dw_tn(h, du1b, "dw_out")

    grads = dict(
        b_gate=db_gate, w_branch_a=dwa, w_branch_b=dwb, w_out=dwo, ln1_g=dln1_g, ln1_b=dln1_b,
        w_up=dw_up, conv_w=jnp.concatenate([dcw_g, dcw_u], axis=1), conv_b=(dcb_g, dcb_u), w_down=dw_down, ln2_g=dln2_g, ln2_b=dln2_b)
    (dqb, dkb, dvb), early_out = _sb_bwd(qb, kb, vb, probs, betas, dyb, early_exchange(grads) if early_exchange else None)
    dqa, dka, dva, grads["sinks"] = _swa_bwd(qa, ka, va, cos, sin, sinks, dya)
    dproj = (dqa, dka, dva, dqb, dkb, dvb, dgl)
    grads["w_in"] = _dw_in(xb, dproj)
    grad_x, tail_out = _grad_x(dproj, w_in, du1, tail_exchange(grads, loss) if tail_exchange else None)
    return loss, grad_x, grads, early_out, tail_out


def _dw_in(xb, dproj):
    T = xb.shape[0]
    tn = 2 * LANES
    groups, start, k = [], 0, 0
    while k < len(IN_WIDTHS):
        if IN_WIDTHS[k] >= tn:
            groups.append((start, IN_WIDTHS[k] // tn, [(k, 0, tn)]))
            k += 1
        else:
            members, off = [], 0
            while off < tn:
                members.append((k, off, IN_WIDTHS[k]))
                off += IN_WIDTHS[k]
                k += 1
            groups.append((start, 1, members))
        start += groups[-1][1]

    def body(x_ref, *refs):
        pieces, o_ref = refs[:-1], refs[-1]
        j = pl.program_id(0)
        for first, steps, members in groups:
            @pl.when((j >= first) & (j < first + steps))
            def _(members=members):
                for k, off, width in members:
                    o_ref[:, off:off + width] = _dot_tn(x_ref[...], pieces[k][...]).astype(o_ref.dtype)

    specs = [None] * len(IN_WIDTHS)
    for first, steps, members in groups:
        for k, _, width in members:
            specs[k] = pl.BlockSpec((T, width), lambda j, first=first, steps=steps: (0, jnp.clip(j - first, 0, steps - 1)))
    return pl.pallas_call(
        body, name="dw_in", grid=(IN_TOTAL // tn,),
        in_specs=[pl.BlockSpec((T, D_MODEL), lambda j: (0, 0))] + specs, out_specs=pl.BlockSpec((D_MODEL, tn), lambda j: (0, j)),
        out_shape=jax.ShapeDtypeStruct((D_MODEL, IN_TOTAL), BF16), compiler_params=_params(("arbitrary",)),
    )(xb, *dproj)


def _grad_x(dproj, w_in, du1, exchange=None):
    T = du1.shape[0]
    tm = min(256, T)
    offs = np.cumsum((0,) + IN_WIDTHS)

    def body(*refs):
        pieces, (w_ref, du_ref, o_ref) = refs[:len(IN_WIDTHS)], refs[len(IN_WIDTHS):]
        acc = ALPHA * du_ref[...]
        for p_ref, a, b in zip(pieces, offs[:-1], offs[1:]):
            acc = acc + _dot_nt(p_ref[...].astype(BF16), w_ref[:, a:b])
        o_ref[...] = acc

    row = lambda n: pl.BlockSpec((tm, n), lambda i: (i, 0))
    (grad_x,), arrived = _hosted_call(
        body, "grad_x", (T // tm,), exchange,
        in_specs=[row(n) for n in IN_WIDTHS] + [pl.BlockSpec((D_MODEL, IN_TOTAL), lambda i: (0, 0)), row(D_MODEL)],
        out_specs=[row(D_MODEL)], out_shape=[jax.ShapeDtypeStruct((T, D_MODEL), F32)], semantics=("parallel",),
        args=(*dproj, w_in, du1))
    return grad_x, arrived


ANY = pl.BlockSpec(memory_space=pl.ANY)


def _all_gather(slabs, name):
    n = len(slabs)

    def body(*refs):
        ins, outs = refs[:n], refs[n:2 * n]
        send_sems, recv_sems, local_sems = refs[2 * n:]
        x, y, c = lax.axis_index("x"), lax.axis_index("y"), lax.axis_index("c")
        me, sibling = (x, y, c), (x, y, 1 - c)
        chips = [(1 - x, y), (x, 1 - y), (1 - x, 1 - y)]

        def slot(pos):
            return 4 * pos[0] + 2 * pos[1] + pos[2]

        def copy(a, k, block, to, from_input=False):
            return pltpu.make_async_remote_copy(
                src_ref=ins[a] if from_input else outs[a].at[slot(block)], dst_ref=outs[a].at[slot(block)],
                send_sem=send_sems.at[a, k], recv_sem=recv_sems.at[a, k], device_id=to, device_id_type=MESH)

        mine = [pltpu.make_async_copy(ins[a], outs[a].at[slot(me)], local_sems.at[a]) for a in range(n)]
        for cp in mine:
            cp.start()
        first = []
        for a in range(n):
            first.append(copy(a, 0, me, sibling, from_input=True))
            first += [copy(a, 1 + j, me, (*chip, c), from_input=True) for j, chip in enumerate(chips)]
        for cp in first:
            cp.start()
        passed = []
        for j, chip in enumerate(chips):
            for a in range(n):
                copy(a, 1 + j, (*chip, c), me).wait_recv()
                fwd = copy(a, 4 + j, (*chip, c), sibling)
                fwd.start()
                passed.append(fwd)
        for a in range(n):
            copy(a, 0, sibling, me).wait_recv()
            for j, chip in enumerate(chips):
                copy(a, 4 + j, (*chip, 1 - c), me).wait_recv()
        for cp in first + passed:
            cp.wait_send()
        for cp in mine:
            cp.wait()

    return pl.pallas_call(
        body, name=name,
        in_specs=[ANY] * n, out_specs=[ANY] * n,
        out_shape=[jax.ShapeDtypeStruct((N_DEV,) + s.shape, s.dtype) for s in slabs],
        scratch_shapes=[pltpu.SemaphoreType.DMA((n, 7)), pltpu.SemaphoreType.DMA((n, 7)), pltpu.SemaphoreType.DMA((n,))],
    )(*slabs)


def _row_tile(rows):
    for cand in range(256, 7, -8):
        if rows % cand == 0:
            return cand
    return rows


def _window(w):
    wp = max(-(-((w * r) % LANES + w) // LANES) for r in range(N_DEV)) * LANES
    assert all((w * r) // LANES * LANES + wp <= N_DEV * w for r in range(N_DEV))
    return wp


def _join_cols(slabs, name):
    _, R, w = slabs.shape
    tr = _row_tile(R)
    wp = _window(w)

    def body(g_ref, o_ref, pad_ref):
        if w % LANES == 0:
            for r in range(N_DEV):
                o_ref[:, w * r:w * (r + 1)] = g_ref[r]
            return
        o_ref[...] = jnp.zeros_like(o_ref)
        pad_ref[...] = jnp.zeros_like(pad_ref)
        for r in range(N_DEV):
            q, s = divmod(w * r, LANES)
            pad_ref[:, :w] = g_ref[r]
            y = pad_ref[...]
            if s:
                y = pltpu.roll(y, s, axis=1)
            o_ref[:, LANES * q:LANES * q + wp] += y

    return pl.pallas_call(
        body, name=name, grid=(R // tr,),
        in_specs=[pl.BlockSpec((N_DEV, tr, w), lambda i: (0, i, 0))], out_specs=pl.BlockSpec((tr, N_DEV * w), lambda i: (i, 0)),
        out_shape=jax.ShapeDtypeStruct((R, N_DEV * w), slabs.dtype), scratch_shapes=[pltpu.VMEM((tr, wp), slabs.dtype)],
        compiler_params=_params(("parallel",)),
    )(slabs)


def _split_cols(pieces, name):
    R = pieces[0].shape[0]
    widths = [p.shape[1] for p in pieces]
    total = sum(widths)
    w = total // N_DEV
    tr = _row_tile(R)
    wp = _window(w)
    offs = np.cumsum([0] + widths)
    dtype = pieces[0].dtype

    def body(*refs):
        ins, (o_ref, full_ref) = refs[:len(pieces)], refs[len(pieces):]
        for p_ref, a, b in zip(ins, offs[:-1], offs[1:]):
            full_ref[:, a:b] = p_ref[...].astype(dtype)
        for r in range(N_DEV):
            q, s = divmod(w * r, LANES)
            y = full_ref[:, LANES * q:LANES * q + wp]
            if s:
                y = pltpu.roll(y, wp - s, axis=1)
            o_ref[r] = y[:, :w]

    return pl.pallas_call(
        body, name=name, grid=(R // tr,),
        in_specs=[pl.BlockSpec((tr, n), lambda i: (i, 0)) for n in widths], out_specs=pl.BlockSpec((N_DEV, tr, w), lambda i: (0, i, 0)),
        out_shape=jax.ShapeDtypeStruct((N_DEV, R, w), dtype), scratch_shapes=[pltpu.VMEM((tr, total), dtype)],
        compiler_params=_params(("parallel",)),
    )(*pieces)


def _adamw(g, w, m, v):
    m_new = ADAM_B1 * m + (1.0 - ADAM_B1) * g
    v_new = ADAM_B2 * v + (1.0 - ADAM_B2) * jnp.square(g)
    m_hat = m_new / (1.0 - ADAM_B1 ** ADAM_STEP)
    v_hat = v_new / (1.0 - ADAM_B2 ** ADAM_STEP)
    return -ADAM_LR * (m_hat / (jnp.sqrt(v_hat) + ADAM_EPS) + ADAM_WD * w), m_new, v_new


def _sum_parts(p_ref):
    g = p_ref[0].astype(F32)
    for d in range(1, N_DEV):
        g = g + p_ref[d].astype(F32)
    return g


def _reduce_adamw(parts, w, m, v, name):
    R, C = w.shape
    tr = _row_tile(R)

    def body(p_ref, w_ref, m_ref, v_ref, g_ref, d_ref, mo_ref, vo_ref):
        g = _sum_parts(p_ref)
        g_ref[...] = g
        d_ref[...], mo_ref[...], vo_ref[...] = _adamw(g, w_ref[...], m_ref[...], v_ref[...])

    row = pl.BlockSpec((tr, C), lambda i: (i, 0))
    return pl.pallas_call(
        body, name=name, grid=(R // tr,),
        in_specs=[pl.BlockSpec((N_DEV, tr, C), lambda i: (0, i, 0)), row, row, row],
        out_specs=[row] * 4, out_shape=[jax.ShapeDtypeStruct((R, C), F32)] * 4,
        compiler_params=_params(("parallel",)),
    )(parts, w, m, v)


def _reduce_adamw_small(parts, ws, ms, vs):
    sizes = [a.shape[1] for a in ws]
    k = len(sizes)
    offs = np.cumsum([0] + [-(-n // LANES) * LANES for n in sizes])

    def body(*refs):
        p_ref, w_refs, m_refs, v_refs = refs[0], refs[1:1 + k], refs[1 + k:1 + 2 * k], refs[1 + 2 * k:1 + 3 * k]
        outs, loss_ref = refs[1 + 3 * k:-1], refs[-1]
        g_all = _sum_parts(p_ref)
        for j, n in enumerate(sizes):
            g = g_all[:, offs[j]:offs[j] + LANES * (-(-n // LANES))][:, :n]
            outs[4 * j][...] = g
            outs[4 * j + 1][...], outs[4 * j + 2][...], outs[4 * j + 3][...] = _adamw(g, w_refs[j][...], m_refs[j][...], v_refs[j][...])
        loss_ref[...] = g_all[:, offs[k]:offs[k] + LANES]

    vm = pl.BlockSpec(memory_space=pltpu.VMEM)
    out_shape = [jax.ShapeDtypeStruct((1, n), F32) for n in sizes for _ in range(4)] + [jax.ShapeDtypeStruct((1, LANES), F32)]
    res = pl.pallas_call(
        body, name="reduce_adamw_replicated", in_specs=[vm] * (1 + 3 * k), out_specs=[vm] * len(out_shape), out_shape=out_shape,
        compiler_params=_params(),
    )(parts, *ws, *ms, *vs)
    return [res[4 * j:4 * j + 4] for j in range(k)], res[-1]


COL_SHARDED = ("w_in", "w_branch_a", "w_branch_b", "w_up", "conv_w")
ROW_SHARDED = ("w_out", "w_down")
SMALL = ("b_gate", "sinks", "ln1_g", "ln1_b", "conv_b", "ln2_g", "ln2_b")
ORDER = ("w_in", "b_gate", "sinks", "w_branch_a", "w_branch_b", "w_out", "ln1_g", "ln1_b", "w_up", "conv_w", "conv_b", "w_down", "ln2_g", "ln2_b")


def _pad_lanes(a):
    pad = (-a.shape[-1]) % LANES
    return a if pad == 0 else jnp.pad(a, ((0, 0), (0, pad)))


def kernel(x, positions, w_in, b_gate, sinks, w_branch_a, w_branch_b, w_out, ln1_g, ln1_b, w_up, conv_w, conv_b, w_down, ln2_g, ln2_b, loss_target, m_w_in, m_b_gate, m_sinks, m_w_branch_a, m_w_branch_b, m_w_out, m_ln1_g, m_ln1_b, m_w_up, m_conv_w, m_conv_b, m_w_down, m_ln2_g, m_ln2_b, v_w_in, v_b_gate, v_sinks, v_w_branch_a, v_w_branch_b, v_w_out, v_ln1_g, v_ln1_b, v_w_up, v_conv_w, v_conv_b, v_w_down, v_ln2_g, v_ln2_b):
    args = dict(locals())
    sharded = COL_SHARDED + ROW_SHARDED
    w = {n: args[n][0] if n in sharded else args[n] for n in ORDER}
    m = {n: args["m_" + n][0] if n in sharded else args["m_" + n] for n in ORDER}
    v = {n: args["v_" + n][0] if n in sharded else args["v_" + n] for n in ORDER}

    travel = {n: (w[n] if n == "conv_w" else w[n].astype(BF16)) for n in sharded}
    (g_in,) = _all_gather([travel["w_in"]], "all_gather_w_in")
    w_in_full = _join_cols(g_in, "join_w_in")
    later = ("w_branch_a", "w_branch_b", "w_out", "w_up", "conv_w", "w_down")

    def join(name, slabs):
        return _join_cols(slabs, "join_" + name) if name in COL_SHARDED else slabs.reshape(-1, slabs.shape[-1])

    def split(name, grad):
        if name in COL_SHARDED:
            return _split_cols(grad if isinstance(grad, tuple) else (grad,), "split_d" + name)
        return grad.reshape((N_DEV, -1, grad.shape[-1]))

    def early_exchange(grads):
        return _Exchange([split(n, grads[n]) for n in later], ["scatter"] * len(later))

    def tail_exchange(grads, loss):
        small_pack = jnp.concatenate(
            [_pad_lanes(p) for n in SMALL for p in (grads[n] if isinstance(grads[n], tuple) else (grads[n],))] + [loss], axis=1)
        return _Exchange([split("w_in", grads["w_in"]), small_pack], ["scatter", "gather"])

    gather_later = _Exchange([travel[n] for n in later], ["gather"] * len(later))
    _, grad_x, _, early_out, (recv_w_in, small_parts) = _local_step(
        x[0], positions[0], w_in_full, w["b_gate"], w["sinks"][0], w["ln1_g"], w["ln1_b"], w["conv_b"], w["ln2_g"], w["ln2_b"], loss_target[0],
        (gather_later, lambda arrived: [join(n, a) for n, a in zip(later, arrived)]), early_exchange, tail_exchange)
    recv = dict(zip(later, early_out), w_in=recv_w_in)

    res = {n: _reduce_adamw(recv[n], w[n], m[n], v[n], "reduce_adamw_" + n) for n in sharded}
    small_res, loss_sum = _reduce_adamw_small(small_parts, [w[n] for n in SMALL], [m[n] for n in SMALL], [v[n] for n in SMALL])
    res.update(zip(SMALL, small_res))
    out = [loss_sum[0, 0], grad_x[None]]
    for k in range(4):
        out += [res[n][k][None] if n in sharded else res[n][k] for n in ORDER]
    return tuple(out)
```

```python
import functools

import jax
import jax.numpy as jnp
import numpy as np
from jax import lax
from jax.experimental import pallas as pl
from jax.experimental.pallas import tpu as pltpu

D_MODEL = 1024
HEAD_DIM = 64
SWA_Q_HEADS = 8
SWA_KV_HEADS = 2
SB_HEADS = 8
WINDOW = 128
ROPE_THETA = 10000.0
D_FF = 2816
LN_EPS = 1e-5
DEPTH = 1
ALPHA = (2.0 * DEPTH) ** 0.25
SWA_Q_WIDTH = SWA_Q_HEADS * HEAD_DIM
SWA_KV_WIDTH = SWA_KV_HEADS * HEAD_DIM
SB_WIDTH = SB_HEADS * HEAD_DIM
GATE_WIDTH = 2 * D_MODEL
IN_WIDTHS = (SWA_Q_WIDTH, SWA_KV_WIDTH, SWA_KV_WIDTH, SB_WIDTH, SB_WIDTH, SB_WIDTH, GATE_WIDTH)
IN_TOTAL = sum(IN_WIDTHS)
ATTN_SCALE = HEAD_DIM ** -0.5

ADAM_LR = 0.001
ADAM_B1 = 0.9
ADAM_B2 = 0.999
ADAM_EPS = 1e-08
ADAM_WD = 0.01
ADAM_STEP = 10

N_DEV = 8
LANES = 128
SB_BLOCK = 256
VMEM_LIMIT = 56 * 1024 * 1024

F32 = jnp.float32
BF16 = jnp.bfloat16
ACT_DTYPE = BF16
MESH = pl.DeviceIdType.MESH


def _params(sem=None):
    return pltpu.CompilerParams(dimension_semantics=sem, vmem_limit_bytes=VMEM_LIMIT)


def _dot(a, b):
    return jnp.dot(a, b, preferred_element_type=F32)


def _dot_nt(a, b):
    return lax.dot_general(a, b, (((1,), (1,)), ((), ())), preferred_element_type=F32)


def _dot_tn(a, b):
    return lax.dot_general(a, b, (((0,), (0,)), ((), ())), preferred_element_type=F32)


def _split_bf16(v):
    hi = v.astype(BF16)
    lo = (v - hi.astype(F32)).astype(BF16)
    return hi, lo


def _matmul(a, b, *, kind, out_shape, grid, a_spec, b_spec, out_spec, name, add=None, add_spec=None, add_scale=1.0):
    dot = {"nn": _dot, "nt": _dot_nt, "tn": _dot_tn}[kind]

    def body(*refs):
        if add is None:
            a_ref, b_ref, o_ref = refs
        else:
            a_ref, b_ref, add_ref, o_ref = refs
        r = dot(a_ref[...].astype(BF16), b_ref[...].astype(BF16))
        if add is not None:
            r = r + add_scale * add_ref[...]
        o_ref[...] = r.astype(o_ref.dtype)

    ins = [a, b] + ([] if add is None else [add])
    specs = [a_spec, b_spec] + ([] if add is None else [add_spec])
    return pl.pallas_call(
        body, name=name, grid=grid, in_specs=specs, out_specs=out_spec, out_shape=out_shape,
        compiler_params=_params(("parallel",) * len(grid)),
    )(*ins)


def _rope_tables(pos_col, inv_freq_lanes):
    T = pos_col.shape[0]
    tm = min(512, T)

    def body(pos_ref, f_ref, cos_ref, sin_ref):
        ang = pos_ref[...].astype(F32) * f_ref[...]
        cos_ref[...] = jnp.cos(ang)
        sin_ref[...] = jnp.sin(ang)

    return pl.pallas_call(
        body, name="rope_tables", grid=(T // tm,),
        in_specs=[pl.BlockSpec((tm, 1), lambda i: (i, 0)), pl.BlockSpec((1, LANES), lambda i: (0, 0))],
        out_specs=[pl.BlockSpec((tm, LANES), lambda i: (i, 0))] * 2,
        out_shape=[jax.ShapeDtypeStruct((T, LANES), F32)] * 2,
        compiler_params=_params(("parallel",)),
    )(pos_col, inv_freq_lanes)


def _lane_iota(shape):
    return lax.broadcasted_iota(jnp.int32, shape, len(shape) - 1)


def _rot_half(t):
    first = (_lane_iota(t.shape) % HEAD_DIM) < (HEAD_DIM // 2)
    return jnp.where(first, -pltpu.roll(t, LANES - HEAD_DIM // 2, axis=1), pltpu.roll(t, HEAD_DIM // 2, axis=1))


def _rope(t, cos, sin):
    return t * cos + _rot_half(t) * sin


def _rope_transpose(d, cos, sin):
    return d * cos - _rot_half(d * sin)


_IN_DTYPES = (F32, F32, BF16, BF16, BF16, BF16, F32)


def _in_proj(x, w_in_b):
    T = x.shape[0]
    tm = min(512, T)
    offs = np.cumsum((0,) + IN_WIDTHS)

    def body(x_ref, w_ref, xb_ref, *outs):
        xb = x_ref[...].astype(BF16)
        xb_ref[...] = xb
        for o_ref, a, b in zip(outs, offs[:-1], offs[1:]):
            o_ref[...] = _dot(xb, w_ref[:, a:b]).astype(o_ref.dtype)

    row = lambda n: pl.BlockSpec((tm, n), lambda i: (i, 0))
    return pl.pallas_call(
        body, name="in_proj", grid=(T // tm,),
        in_specs=[row(D_MODEL), pl.BlockSpec((D_MODEL, IN_TOTAL), lambda i: (0, 0))],
        out_specs=[row(D_MODEL)] + [row(n) for n in IN_WIDTHS],
        out_shape=[jax.ShapeDtypeStruct((T, D_MODEL), BF16)] + [jax.ShapeDtypeStruct((T, n), dt) for n, dt in zip(IN_WIDTHS, _IN_DTYPES)],
        compiler_params=_params(("parallel",)),
    )(x, w_in_b)


def _swa_specs(T):
    blk = WINDOW
    cur = lambda n: pl.BlockSpec((blk, n), lambda i: (i, 0))
    prev = lambda n: pl.BlockSpec((blk, n), lambda i: (jnp.maximum(i - 1, 0), 0))
    return blk, cur, prev


SWA_GROUP = SWA_Q_HEADS // SWA_KV_HEADS


def _swa_stack(pairs):
    lane = _lane_iota(pairs[0].shape)
    zero = jnp.zeros((), pairs[0].dtype)
    rows = []
    for h in range(SWA_Q_HEADS):
        hh, g = h % 2, h // SWA_GROUP
        x = jnp.where((lane >= hh * HEAD_DIM) & (lane < (hh + 1) * HEAD_DIM), pairs[h // 2], zero)
        rows.append(x if hh == g else pltpu.roll(x, HEAD_DIM, axis=1))
    return jnp.concatenate(rows, axis=0)


def _swa_unstack(stacked, blk):
    low = _lane_iota((blk, LANES)) < HEAD_DIM
    pairs = []
    for pp in range(SWA_Q_HEADS // 2):
        halves = []
        for hh in range(2):
            h = 2 * pp + hh
            x = stacked[h * blk:(h + 1) * blk]
            halves.append(x if hh == h // SWA_GROUP else pltpu.roll(x, HEAD_DIM, axis=1))
        pairs.append(jnp.where(low, halves[0], halves[1]))
    return pairs


def _swa_probs(i, q_stack, kwin, sink_ref, blk):
    r = lax.broadcasted_iota(jnp.int32, (blk, 2 * blk), 0)
    c = lax.broadcasted_iota(jnp.int32, (blk, 2 * blk), 1)
    rel = blk + r - c
    valid = (rel >= 0) & (rel < WINDOW) & ((c >= blk) | (i > 0))
    bias = jnp.concatenate([jnp.where(valid, 0.0, -1e30)] * SWA_Q_HEADS, axis=0)
    head = lax.broadcasted_iota(jnp.int32, (SWA_Q_HEADS * blk, 1), 0) // blk
    sink = jnp.zeros((SWA_Q_HEADS * blk, 1), F32)
    for h in range(SWA_Q_HEADS):
        sink = jnp.where(head == h, sink_ref[h], sink)
    s = _dot_nt(q_stack, kwin) * ATTN_SCALE + bias
    m = jnp.maximum(jnp.max(s, axis=1, keepdims=True), sink)
    p = jnp.exp(s - m)
    es = jnp.exp(sink - m)
    den = jnp.sum(p, axis=1, keepdims=True) + es
    return p / den, es / den


def _swa_inputs(q_ref, kp_ref, kc_ref, vp_ref, vc_ref, cp_ref, cc_ref, sp_ref, sc_ref):
    cc, sc = cc_ref[...], sc_ref[...]
    kwin = jnp.concatenate([_rope(kp_ref[...], cp_ref[...], sp_ref[...]), _rope(kc_ref[...], cc, sc)], axis=0).astype(BF16)
    vwin = jnp.concatenate([vp_ref[...], vc_ref[...]], axis=0)
    q_stack = _swa_stack([_rope(q_ref[:, pp * LANES:(pp + 1) * LANES], cc, sc) for pp in range(SWA_Q_HEADS // 2)]).astype(BF16)
    return q_stack, kwin, vwin


def _swa_fwd(qa, ka, va, cos, sin, sinks):
    T = qa.shape[0]
    blk, cur, prev = _swa_specs(T)

    def body(sink_ref, q_ref, kp_ref, kc_ref, vp_ref, vc_ref, cp_ref, cc_ref, sp_ref, sc_ref, o_ref):
        q_stack, kwin, vwin = _swa_inputs(q_ref, kp_ref, kc_ref, vp_ref, vc_ref, cp_ref, cc_ref, sp_ref, sc_ref)
        probs, _ = _swa_probs(pl.program_id(0), q_stack, kwin, sink_ref, blk)
        for pp, tile in enumerate(_swa_unstack(_dot(probs.astype(BF16), vwin), blk)):
            o_ref[:, pp * LANES:(pp + 1) * LANES] = tile.astype(o_ref.dtype)

    return pl.pallas_call(
        body, name="swa_fwd", grid=(T // blk,),
        in_specs=[pl.BlockSpec(memory_space=pltpu.SMEM), cur(SWA_Q_WIDTH), prev(LANES), cur(LANES), prev(LANES), cur(LANES),
                  prev(LANES), cur(LANES), prev(LANES), cur(LANES)],
        out_specs=cur(SWA_Q_WIDTH),
        out_shape=jax.ShapeDtypeStruct((T, SWA_Q_WIDTH), BF16),
        compiler_params=_params(("parallel",)),
    )(sinks, qa, ka, ka, va, va, cos, cos, sin, sin)


def _swa_bwd(qa, ka, va, cos, sin, sinks, dya):
    T = qa.shape[0]
    blk, cur, prev = _swa_specs(T)
    full = lambda n: pl.BlockSpec((T, n), lambda i: (0, 0))

    def body(sink_ref, q_ref, kp_ref, kc_ref, vp_ref, vc_ref, cp_ref, cc_ref, sp_ref, sc_ref, do_ref,
             dq_ref, dk_out, dv_out, dsink_ref, dk_ref, dv_ref):
        i = pl.program_id(0)

        @pl.when(i == 0)
        def _():
            dk_ref[...] = jnp.zeros_like(dk_ref)
            dv_ref[...] = jnp.zeros_like(dv_ref)
            dsink_ref[...] = jnp.zeros_like(dsink_ref)

        cp, cc, sp, sc = cp_ref[...], cc_ref[...], sp_ref[...], sc_ref[...]
        q_stack, kwin, vwin = _swa_inputs(q_ref, kp_ref, kc_ref, vp_ref, vc_ref, cp_ref, cc_ref, sp_ref, sc_ref)
        probs, psink = _swa_probs(i, q_stack, kwin, sink_ref, blk)
        do_stack = _swa_stack([do_ref[:, pp * LANES:(pp + 1) * LANES] for pp in range(SWA_Q_HEADS // 2)])
        dp = _dot_nt(do_stack, vwin)
        dsum = jnp.sum(probs * dp, axis=1, keepdims=True)
        ds = (probs * (dp - dsum) * ATTN_SCALE).astype(BF16)
        for pp, tile in enumerate(_swa_unstack(_dot(ds, kwin), blk)):
            dq_ref[:, pp * LANES:(pp + 1) * LANES] = _rope_transpose(tile, cc, sc).astype(dq_ref.dtype)
        dkw = _dot_tn(ds, q_stack)
        dvw = _dot_tn(probs.astype(BF16), do_stack)
        lane1 = _lane_iota((1, LANES))
        sink_share = psink * dsum
        dsink = jnp.zeros((1, LANES), F32)
        for h in range(SWA_Q_HEADS):
            dsink = dsink + jnp.where(lane1 == h, -jnp.sum(sink_share[h * blk:(h + 1) * blk]), 0.0)
        dsink_ref[...] += dsink
        ip = jnp.maximum(i - 1, 0)
        rows_p = pl.ds(pl.multiple_of(ip * blk, blk), blk)
        rows_c = pl.ds(pl.multiple_of(i * blk, blk), blk)
        dk_ref[rows_p, :] += _rope_transpose(dkw[:blk], cp, sp)
        dv_ref[rows_p, :] += dvw[:blk]
        dk_ref[rows_c, :] += _rope_transpose(dkw[blk:], cc, sc)
        dv_ref[rows_c, :] += dvw[blk:]

        @pl.when(i == T // blk - 1)
        def _():
            dk_out[...] = dk_ref[...].astype(BF16)
            dv_out[...] = dv_ref[...].astype(BF16)

    return pl.pallas_call(
        body, name="swa_bwd", grid=(T // blk,),
        in_specs=[pl.BlockSpec(memory_space=pltpu.SMEM), cur(SWA_Q_WIDTH), prev(LANES), cur(LANES), prev(LANES), cur(LANES),
                  prev(LANES), cur(LANES), prev(LANES), cur(LANES), cur(SWA_Q_WIDTH)],
        out_specs=[cur(SWA_Q_WIDTH), full(LANES), full(LANES), pl.BlockSpec((1, LANES), lambda i: (0, 0))],
        out_shape=[jax.ShapeDtypeStruct((T, SWA_Q_WIDTH), BF16), jax.ShapeDtypeStruct((T, LANES), BF16),
                   jax.ShapeDtypeStruct((T, LANES), BF16), jax.ShapeDtypeStruct((1, LANES), F32)],
        scratch_shapes=[pltpu.VMEM((T, LANES), F32)] * 2,
        compiler_params=_params(("arbitrary",)),
    )(sinks, qa, ka, ka, va, va, cos, cos, sin, sin, dya)


class _Exchange:
    FLIPS = [(fx, fy, fc) for fx in (0, 1) for fy in (0, 1) for fc in (0, 1) if (fx, fy, fc) != (0, 0, 0)]

    def __init__(self, arrays, kinds):
        self.arrays, self.kinds, self.n = list(arrays), list(kinds), len(arrays)

    def out_shape(self):
        return [jax.ShapeDtypeStruct(a.shape if k == "scatter" else (N_DEV,) + a.shape, a.dtype) for a, k in zip(self.arrays, self.kinds)]

    def scratch(self):
        return [pltpu.SemaphoreType.DMA((self.n, 7)), pltpu.SemaphoreType.DMA((self.n, 7)), pltpu.SemaphoreType.DMA((self.n,))]

    def bind(self, ins, outs, send_sems, recv_sems, local_sems):
        x, y, c = lax.axis_index("x"), lax.axis_index("y"), lax.axis_index("c")
        me = 4 * x + 2 * y + c
        local, remote = [], []
        for a, kind in enumerate(self.kinds):
            mine = ins[a].at[me] if kind == "scatter" else ins[a]
            local.append(pltpu.make_async_copy(mine, outs[a].at[me], local_sems.at[a]))
            for k, (fx, fy, fc) in enumerate(self.FLIPS):
                peer = (x ^ fx, y ^ fy, c ^ fc)
                peer_slot = 4 * peer[0] + 2 * peer[1] + peer[2]
                src = ins[a].at[peer_slot] if kind == "scatter" else ins[a]
                sems = dict(send_sem=send_sems.at[a, k], recv_sem=recv_sems.at[a, k], device_id=peer, device_id_type=MESH)
                remote.append((pltpu.make_async_remote_copy(src_ref=src, dst_ref=outs[a].at[me], **sems),
                               pltpu.make_async_remote_copy(src_ref=src, dst_ref=outs[a].at[peer_slot], **sems)))

        def start():
            for cp in local:
                cp.start()
            for send, _ in remote:
                send.start()

        def wait():
            for send, arrival in remote:
                arrival.wait_recv()
                send.wait_send()
            for cp in local:
                cp.wait()

        return start, wait


def _hosted_call(body, name, grid, exchange, *, in_specs, out_specs, out_shape, semantics, args, scratch=(), prefetch=()):
    n = 0 if exchange is None else exchange.n
    n_pre, n_in, n_out, n_scratch = len(prefetch), len(in_specs), len(out_specs), len(scratch)

    def hosted(*refs):
        pre, rest = refs[:n_pre], refs[n_pre:]
        ins, rest = rest[:n_in], rest[n_in:]
        ex_ins, rest = rest[:n], rest[n:]
        outs, rest = rest[:n_out], rest[n_out:]
        ex_outs, rest = rest[:n], rest[n:]
        own, sems = rest[:n_scratch], rest[n_scratch:]
        if exchange is None:
            return body(*pre, *ins, *outs, *own)
        start, wait = exchange.bind(ex_ins, ex_outs, *sems)
        ids = [pl.program_id(d) for d in range(len(grid))]
        first = functools.reduce(jnp.logical_and, [i == 0 for i in ids])
        last = functools.reduce(jnp.logical_and, [i == g - 1 for i, g in zip(ids, grid)])
        pl.when(first)(start)
        body(*pre, *ins, *outs, *own)
        pl.when(last)(wait)

    grid_spec = pltpu.PrefetchScalarGridSpec(
        num_scalar_prefetch=n_pre, grid=grid, in_specs=list(in_specs) + [ANY] * n, out_specs=list(out_specs) + [ANY] * n,
        scratch_shapes=list(scratch) + ([] if exchange is None else exchange.scratch()))
    res = pl.pallas_call(
        hosted, name=name, grid_spec=grid_spec, out_shape=list(out_shape) + ([] if exchange is None else exchange.out_shape()),
        compiler_params=_params(semantics if exchange is None else ("arbitrary",) * len(grid)),
    )(*prefetch, *args, *([] if exchange is None else exchange.arrays))
    return res[:n_out], res[n_out:]


SOFTPLUS_LINEAR_FROM = 30.0


def _sb_scores(qm, k, valid):
    z = _dot_nt(qm, k)
    sp = jnp.where(z > SOFTPLUS_LINEAR_FROM, z, jnp.log(1.0 + jnp.exp(z)))
    log_beta = z - sp
    if valid is not None:
        sp = jnp.where(valid, sp, 0.0)
    return sp, log_beta


def _tri2(B, cmp):
    r = lax.broadcasted_iota(jnp.int32, (2 * B, B), 0) % B
    c = lax.broadcasted_iota(jnp.int32, (2 * B, B), 1)
    return cmp(r, c).astype(BF16)


def _tri_sum(v, tri2):
    hi, lo = _split_bf16(v)
    return _dot(jnp.concatenate([hi, lo], axis=1), tri2)


def _head_masks(x):
    low = _lane_iota(x.shape) < HEAD_DIM
    zero = jnp.zeros((), x.dtype)
    return jnp.where(low, x, zero), jnp.where(low, zero, x)


def _strictly_below(B):
    r = lax.broadcasted_iota(jnp.int32, (B, B), 0)
    c = lax.broadcasted_iota(jnp.int32, (B, B), 1)
    return c < r


def _sb_grid(T, descending):
    B = min(SB_BLOCK, T)
    n = T // B
    pairs = [(i, j) for i in range(n) for j in (range(i, -1, -1) if descending else range(i + 1))]
    return B, jnp.asarray([p[0] for p in pairs], jnp.int32), jnp.asarray([p[1] for p in pairs], jnp.int32)


N_PAIRS = SB_HEADS // 2
PAIR_COLS = [slice(p * LANES, (p + 1) * LANES) for p in range(N_PAIRS)]


def _sb_fwd(qb, kb, vb, exchange=None):
    T = qb.shape[0]
    B, i_tab, j_tab = _sb_grid(T, descending=True)
    n = T // B

    def body(i_ref, j_ref, q_ref, k_ref, v_ref, o_ref, a_ref, b_ref, acc_ref, c_ref, tri_ref):
        s = pl.program_id(0)
        i, j = i_ref[s], j_ref[s]

        @pl.when(s == 0)
        def _():
            tri_ref[...] = _tri2(B, lambda r, c: r > c)

        @pl.when(j == i)
        def _():
            acc_ref[...] = jnp.zeros_like(acc_ref)
            c_ref[...] = jnp.zeros_like(c_ref)

        def block(valid):
            for p, cols in enumerate(PAIR_COLS):
                qms = _head_masks(q_ref[:, cols] * ATTN_SCALE)
                k = k_ref[:, cols]
                probs = []
                for hh in range(2):
                    h = 2 * p + hh
                    sp, lb = _sb_scores(qms[hh], k, valid)
                    c = c_ref[h]
                    a = jnp.exp(lb - (c + _tri_sum(sp, tri_ref[...])))
                    beta = jnp.exp(lb)
                    if valid is not None:
                        a = jnp.where(valid, a, 0.0)
                        beta = jnp.where(valid, beta, 0.0)
                    probs.append(a.astype(BF16))
                    a_ref[h] = probs[-1]
                    b_ref[h] = beta.astype(BF16)
                    c_ref[h] = c + jnp.sum(sp, axis=1, keepdims=True)
                acc_ref[:, cols] += _dot(jnp.concatenate(probs, axis=1), jnp.concatenate(_head_masks(v_ref[:, cols]), axis=0))

        pl.when(j == i)(lambda: block(_strictly_below(B)))
        pl.when(j != i)(lambda: block(None))

        @pl.when(j == 0)
        def _():
            o_ref[...] = acc_ref[...].astype(o_ref.dtype)

    q_spec = pl.BlockSpec((B, SB_WIDTH), lambda s, i_ref, j_ref: (i_ref[s], 0))
    k_spec = pl.BlockSpec((B, SB_WIDTH), lambda s, i_ref, j_ref: (j_ref[s], 0))
    tile = pl.BlockSpec((None, None, SB_HEADS, B, B), lambda s, i_ref, j_ref: (i_ref[s], j_ref[s], 0, 0, 0))
    saved = jax.ShapeDtypeStruct((n, n, SB_HEADS, B, B), BF16)
    return _hosted_call(
        body, "sb_fwd", (int(i_tab.shape[0]),), exchange, prefetch=(i_tab, j_tab),
        in_specs=[q_spec, k_spec, k_spec], out_specs=[q_spec, tile, tile],
        out_shape=[jax.ShapeDtypeStruct((T, SB_WIDTH), BF16), saved, saved],
        scratch=[pltpu.VMEM((B, SB_WIDTH), F32), pltpu.VMEM((SB_HEADS, B, 1), F32), pltpu.VMEM((2 * B, B), BF16)],
        semantics=("arbitrary",), args=(qb, kb, vb))


def _sb_bwd(qb, kb, vb, probs, betas, dyb, exchange=None):
    T = qb.shape[0]
    B, i_tab, j_tab = _sb_grid(T, descending=False)
    n_steps = int(i_tab.shape[0])

    def block_diag_t(x):
        xt = x.T
        top = lax.broadcasted_iota(jnp.int32, xt.shape, 0) < HEAD_DIM
        zero = jnp.zeros((), x.dtype)
        return jnp.concatenate([jnp.where(top, xt, zero), jnp.where(top, zero, xt)], axis=1)

    def body(i_ref, j_ref, q_ref, k_ref, v_ref, a_ref, b_ref, do_ref, dq_ref, dk_out, dv_out,
             dq_acc, cg_ref, dkt_ref, dvt_ref, tri_ref, qt_ref, dot_ref):
        s = pl.program_id(0)
        i, j = i_ref[s], j_ref[s]

        @pl.when(s == 0)
        def _():
            dkt_ref[...] = jnp.zeros_like(dkt_ref)
            dvt_ref[...] = jnp.zeros_like(dvt_ref)
            tri_ref[...] = _tri2(B, lambda r, c: r < c)[:B]

        @pl.when(j == 0)
        def _():
            dq_acc[...] = jnp.zeros_like(dq_acc)
            cg_ref[...] = jnp.zeros_like(cg_ref)
            for p, cols in enumerate(PAIR_COLS):
                qt_ref[p] = block_diag_t(q_ref[:, cols] * ATTN_SCALE)
                dot_ref[p] = block_diag_t(do_ref[:, cols])

        for p, cols in enumerate(PAIR_COLS):
            doms = _head_masks(do_ref[:, cols])
            k, v = k_ref[:, cols], v_ref[:, cols]
            dzs = []
            for hh in range(2):
                h = 2 * p + hh
                g = a_ref[h].astype(F32) * _dot_nt(doms[hh], v)
                cg = cg_ref[h]
                gsum = g + (cg + _dot(g.astype(BF16), tri_ref[...]))
                dzs.append((g - b_ref[h].astype(F32) * gsum).astype(BF16))
                cg_ref[h] = cg + jnp.sum(g, axis=1, keepdims=True)
            dq_acc[:, cols] += _dot(jnp.concatenate(dzs, axis=1), jnp.concatenate(_head_masks(k), axis=0))
            dkt_ref[j, cols, :] += _dot(qt_ref[p], jnp.concatenate(dzs, axis=0))
            dvt_ref[j, cols, :] += _dot(dot_ref[p], jnp.concatenate([a_ref[2 * p], a_ref[2 * p + 1]], axis=0))

        @pl.when(j == i)
        def _():
            dq_ref[...] = (dq_acc[...] * ATTN_SCALE).astype(dq_ref.dtype)

        @pl.when(s == n_steps - 1)
        def _():
            for jb in range(T // B):
                dk_out[jb * B:(jb + 1) * B, :] = dkt_ref[jb].T.astype(BF16)
                dv_out[jb * B:(jb + 1) * B, :] = dvt_ref[jb].T.astype(BF16)

    q_spec = pl.BlockSpec((B, SB_WIDTH), lambda s, i_ref, j_ref: (i_ref[s], 0))
    k_spec = pl.BlockSpec((B, SB_WIDTH), lambda s, i_ref, j_ref: (j_ref[s], 0))
    tile = pl.BlockSpec((None, None, SB_HEADS, B, B), lambda s, i_ref, j_ref: (i_ref[s], j_ref[s], 0, 0, 0))
    full = pl.BlockSpec((T, SB_WIDTH), lambda s, i_ref, j_ref: (0, 0))
    return _hosted_call(
        body, "sb_bwd", (n_steps,), exchange, prefetch=(i_tab, j_tab),
        in_specs=[q_spec, k_spec, k_spec, tile, tile, q_spec], out_specs=[q_spec, full, full],
        out_shape=[jax.ShapeDtypeStruct((T, SB_WIDTH), BF16)] * 3,
        scratch=[pltpu.VMEM((B, SB_WIDTH), F32), pltpu.VMEM((SB_HEADS, B, 1), F32), pltpu.VMEM((T // B, SB_WIDTH, B), F32),
                 pltpu.VMEM((T // B, SB_WIDTH, B), F32), pltpu.VMEM((B, B), BF16), pltpu.VMEM((N_PAIRS, LANES, 2 * B), BF16),
                 pltpu.VMEM((N_PAIRS, LANES, 2 * B), BF16)],
        semantics=("arbitrary",), args=(qb, kb, vb, probs, betas, dyb))


def _ln_stats(u):
    mu = jnp.mean(u, axis=-1, keepdims=True)
    xc = u - mu
    var = jnp.mean(xc * xc, axis=-1, keepdims=True)
    rstd = lax.rsqrt(var + LN_EPS)
    return xc * rstd, rstd


def _ln_bwd(dy, xhat, rstd, g):
    dxh = dy * g
    return rstd * (dxh - jnp.mean(dxh, axis=-1, keepdims=True) - xhat * jnp.mean(dxh * xhat, axis=-1, keepdims=True))


def _gates(gl_ref, bg_ref):
    ga = jax.nn.sigmoid(gl_ref[:, :D_MODEL] + bg_ref[:, :D_MODEL])
    gb = jax.nn.sigmoid(gl_ref[:, D_MODEL:] + bg_ref[:, D_MODEL:])
    return ga, gb


def _mix_fwd(ya, yb, gl, x, wa, wb, wo, b_gate, ln1_g, ln1_b):
    T = x.shape[0]
    tm = min(512, T)

    def body(ya_ref, yb_ref, gl_ref, x_ref, wa_ref, wb_ref, wo_ref, bg_ref, g_ref, b_ref, h_ref, u_ref, x1_ref):
        ga, gb = _gates(gl_ref, bg_ref)
        h = (ga * _dot(ya_ref[...], wa_ref[...]) + gb * _dot(yb_ref[...], wb_ref[...])).astype(BF16)
        h_ref[...] = h
        u = ALPHA * x_ref[...] + _dot(h, wo_ref[...])
        u_ref[...] = u
        xhat, _ = _ln_stats(u)
        x1_ref[...] = (xhat * g_ref[...] + b_ref[...]).astype(BF16)

    row = lambda n: pl.BlockSpec((tm, n), lambda i: (i, 0))
    const = lambda r, n: pl.BlockSpec((r, n), lambda i: (0, 0))
    return pl.pallas_call(
        body, name="mix_fwd", grid=(T // tm,),
        in_specs=[row(SWA_Q_WIDTH), row(SB_WIDTH), row(GATE_WIDTH), row(D_MODEL), const(SWA_Q_WIDTH, D_MODEL), const(SB_WIDTH, D_MODEL),
                  const(D_MODEL, D_MODEL), const(1, GATE_WIDTH), const(1, D_MODEL), const(1, D_MODEL)],
        out_specs=[row(D_MODEL)] * 3,
        out_shape=[jax.ShapeDtypeStruct((T, D_MODEL), BF16), jax.ShapeDtypeStruct((T, D_MODEL), F32), jax.ShapeDtypeStruct((T, D_MODEL), BF16)],
        compiler_params=_params(("parallel",)),
    )(ya, yb, gl, x, wa, wb, wo, b_gate, ln1_g, ln1_b)


def _mix_bwd(du1, ya, yb, gl, wa, wb, wo, b_gate):
    T = du1.shape[0]
    tm = min(512, T)

    def body(du_ref, ya_ref, yb_ref, gl_ref, wa_ref, wb_ref, wo_ref, bg_ref, dya_ref, dyb_ref, dgl_ref, dta_ref, dtb_ref, dbg_ref):
        @pl.when(pl.program_id(0) == 0)
        def _():
            dbg_ref[...] = jnp.zeros_like(dbg_ref)

        dh = _dot_nt(du_ref[...].astype(BF16), wo_ref[...])
        ga, gb = _gates(gl_ref, bg_ref)
        for gate, y_ref, w_ref, dy_ref, dt_ref, lo in ((ga, ya_ref, wa_ref, dya_ref, dta_ref, 0), (gb, yb_ref, wb_ref, dyb_ref, dtb_ref, D_MODEL)):
            t = _dot(y_ref[...], w_ref[...])
            dlogit = dh * t * gate * (1.0 - gate)
            dgl_ref[:, lo:lo + D_MODEL] = dlogit.astype(BF16)
            dbg_ref[:, lo:lo + D_MODEL] += jnp.sum(dlogit, axis=0, keepdims=True)
            dt = (dh * gate).astype(BF16)
            dt_ref[...] = dt
            dy_ref[...] = _dot_nt(dt, w_ref[...]).astype(BF16)

    row = lambda n: pl.BlockSpec((tm, n), lambda i: (i, 0))
    const = lambda r, n: pl.BlockSpec((r, n), lambda i: (0, 0))
    sds = lambda n, dt: jax.ShapeDtypeStruct((T, n), dt)
    return pl.pallas_call(
        body, name="mix_bwd", grid=(T // tm,),
        in_specs=[row(D_MODEL), row(SWA_Q_WIDTH), row(SB_WIDTH), row(GATE_WIDTH), const(SWA_Q_WIDTH, D_MODEL), const(SB_WIDTH, D_MODEL),
                  const(D_MODEL, D_MODEL), const(1, GATE_WIDTH)],
        out_specs=[row(SWA_Q_WIDTH), row(SB_WIDTH), row(GATE_WIDTH), row(D_MODEL), row(D_MODEL), const(1, GATE_WIDTH)],
        out_shape=[sds(SWA_Q_WIDTH, BF16), sds(SB_WIDTH, BF16), sds(GATE_WIDTH, BF16), sds(D_MODEL, BF16), sds(D_MODEL, BF16),
                   jax.ShapeDtypeStruct((1, GATE_WIDTH), F32)],
        compiler_params=_params(("arbitrary",)),
    )(du1, ya, yb, gl, wa, wb, wo, b_gate)


CONV_COLS = LANES


CONV_CHUNK = 64
CONV_CHUNK_FWD = 256
HALO = 8


def _taps(ref, r0, rows, lead):
    return [ref[pl.ds(r0 + lead + k, rows), :] for k in ((-2, -1, 0) if lead else (0, 1, 2))]


def _chunks(T, rows, step, init=None):
    def body(c, carry):
        out = step(pl.multiple_of(c * rows, rows), *(() if init is None else (carry,)))
        return carry if init is None else out
    return lax.fori_loop(0, T // rows, body, 0 if init is None else init)


def _conv_chunk(taps, w_ref, b_ref):
    return w_ref[0:1, :] * taps[0] + w_ref[1:2, :] * taps[1] + w_ref[2:3, :] * taps[2] + b_ref[...]


def _fold(x):
    return jnp.sum(x.reshape(x.shape[0] // 8, 8, x.shape[1]), axis=0)


def _conv_specs(T):
    nb = D_FF // CONV_COLS
    pair = pl.BlockSpec((2, T, CONV_COLS), lambda j: (0, 0, j))
    gate = lambda r: pl.BlockSpec((r, CONV_COLS), lambda j: (0, j))
    up = lambda r: pl.BlockSpec((r, CONV_COLS), lambda j: (0, j + nb))
    return nb, pair, gate, up


def _conv_glu_fwd(p3, conv_w, conv_b):
    T = p3.shape[1]
    nb, pair, gate, up = _conv_specs(T)

    R = min(CONV_CHUNK_FWD, T)

    def body(p_ref, wg_ref, wu_ref, bg_ref, bu_ref, s_ref, pg_s, pu_s):
        for half, scr in enumerate((pg_s, pu_s)):
            scr[0:HALO, :] = jnp.zeros((HALO, CONV_COLS), F32)
            scr[HALO:HALO + T, :] = p_ref[half].astype(F32)
        def step(r0):
            ag = _conv_chunk(_taps(pg_s, r0, R, HALO), wg_ref, bg_ref)
            au = _conv_chunk(_taps(pu_s, r0, R, HALO), wu_ref, bu_ref)
            s_ref[pl.ds(r0, R), :] = (ag * jax.nn.sigmoid(ag) * au).astype(BF16)

        _chunks(T, R, step)

    return pl.pallas_call(
        body, name="conv_glu_fwd", grid=(nb,),
        in_specs=[pair, gate(3), up(3), gate(1), up(1)],
        out_specs=pl.BlockSpec((T, CONV_COLS), lambda j: (0, j)),
        out_shape=jax.ShapeDtypeStruct((T, D_FF), BF16),
        scratch_shapes=[pltpu.VMEM((T + HALO, CONV_COLS), F32)] * 2,
        compiler_params=_params(("parallel",)),
    )(p3, conv_w, conv_w, conv_b, conv_b)


def _conv_glu_bwd(p3, ds, conv_w, conv_b):
    T = p3.shape[1]
    nb, pair, gate, up = _conv_specs(T)

    R = min(CONV_CHUNK, T)

    def body(p_ref, ds_ref, wg_ref, wu_ref, bg_ref, bu_ref, dp_ref, dwg_ref, dwu_ref, dbg_ref, dbu_ref, pg_s, pu_s, dag_s, dau_s):
        for half, scr in enumerate((pg_s, pu_s)):
            scr[0:HALO, :] = jnp.zeros((HALO, CONV_COLS), F32)
            scr[HALO:HALO + T, :] = p_ref[half].astype(F32)
        for scr in (dag_s, dau_s):
            scr[T:T + HALO, :] = jnp.zeros((HALO, CONV_COLS), F32)
        halves = ((pg_s, dag_s, wg_ref, dwg_ref, dbg_ref), (pu_s, dau_s, wu_ref, dwu_ref, dbu_ref))

        def step(r0, sums):
            taps = [_taps(p_s, r0, R, HALO) for p_s, *_ in halves]
            ag = _conv_chunk(taps[0], wg_ref, bg_ref)
            au = _conv_chunk(taps[1], wu_ref, bu_ref)
            sg = jax.nn.sigmoid(ag)
            d = ds_ref[pl.ds(r0, R), :].astype(F32)
            das = (d * au * (sg * (1.0 + ag * (1.0 - sg))), d * ag * sg)
            out = []
            for half, (_, da_s, *_) in enumerate(halves):
                da_s[pl.ds(r0, R), :] = das[half]
                out.append(tuple(sums[half][k] + _fold(das[half] * taps[half][k]) for k in range(3)) + (sums[half][3] + _fold(das[half]),))
            return tuple(out)

        sums = _chunks(T, R, step, ((jnp.zeros((8, CONV_COLS), F32),) * 4,) * 2)
        for half, (_, da_s, w_ref, dw_ref, db_ref) in enumerate(halves):
            for k in range(3):
                dw_ref[k:k + 1, :] = jnp.sum(sums[half][k], axis=0, keepdims=True)
            db_ref[...] = jnp.sum(sums[half][3], axis=0, keepdims=True)

            def transposed(r0, da_s=da_s, w_ref=w_ref, half=half):
                da0, da1, da2 = _taps(da_s, r0, R, 0)
                dp_ref[half, pl.ds(r0, R), :] = (w_ref[2:3, :] * da0 + w_ref[1:2, :] * da1 + w_ref[0:1, :] * da2).astype(BF16)

            _chunks(T, R, transposed)

    col = lambda r: pl.BlockSpec((r, CONV_COLS), lambda j: (0, j))
    return pl.pallas_call(
        body, name="conv_glu_bwd", grid=(nb,),
        in_specs=[pair, col(T), gate(3), up(3), gate(1), up(1)],
        out_specs=[pair, col(3), col(3), col(1), col(1)],
        out_shape=[jax.ShapeDtypeStruct((2, T, D_FF), BF16), jax.ShapeDtypeStruct((3, D_FF), F32), jax.ShapeDtypeStruct((3, D_FF), F32),
                   jax.ShapeDtypeStruct((1, D_FF), F32), jax.ShapeDtypeStruct((1, D_FF), F32)],
        scratch_shapes=[pltpu.VMEM((T + HALO, CONV_COLS), F32)] * 4,
        compiler_params=_params(("parallel",)),
    )(p3, ds, conv_w, conv_w, conv_b, conv_b)


def _ffn_down_loss(s, w_down, u1, ln1_g, ln1_b, ln2_g, ln2_b, target):
    T = u1.shape[0]
    tm = min(512, T)

    def body(s_ref, w_ref, u1_ref, g1_ref, b1_ref, g2_ref, b2_ref, t_ref, du_ref, dub_ref, dg_ref, db_ref, loss_ref):
        @pl.when(pl.program_id(0) == 0)
        def _():
            dg_ref[...] = jnp.zeros_like(dg_ref)
            db_ref[...] = jnp.zeros_like(db_ref)
            loss_ref[...] = jnp.zeros_like(loss_ref)

        xh1, _ = _ln_stats(u1_ref[...])
        x1 = xh1 * g1_ref[...] + b1_ref[...]
        u2 = ALPHA * x1 + _dot(s_ref[...], w_ref[...])
        xh2, rstd2 = _ln_stats(u2)
        err = xh2 * g2_ref[...] + b2_ref[...] - t_ref[...]
        per_token = jnp.mean(err * err, axis=-1, keepdims=True)
        loss_ref[...] += 0.5 * jnp.sum(per_token, axis=0, keepdims=True)
        dy = err * (1.0 / D_MODEL)
        dg_ref[...] += jnp.sum(dy * xh2, axis=0, keepdims=True)
        db_ref[...] += jnp.sum(dy, axis=0, keepdims=True)
        du2 = _ln_bwd(dy, xh2, rstd2, g2_ref[...])
        du_ref[...] = du2
        dub_ref[...] = du2.astype(BF16)

    row = lambda n: pl.BlockSpec((tm, n), lambda i: (i, 0))
    const = lambda r, n: pl.BlockSpec((r, n), lambda i: (0, 0))
    vec = const(1, D_MODEL)
    return pl.pallas_call(
        body, name="ffn_down_loss", grid=(T // tm,),
        in_specs=[row(D_FF), const(D_FF, D_MODEL), row(D_MODEL), vec, vec, vec, vec, row(D_MODEL)],
        out_specs=[row(D_MODEL), row(D_MODEL), vec, vec, const(1, LANES)],
        out_shape=[jax.ShapeDtypeStruct((T, D_MODEL), F32), jax.ShapeDtypeStruct((T, D_MODEL), BF16), jax.ShapeDtypeStruct((1, D_MODEL), F32),
                   jax.ShapeDtypeStruct((1, D_MODEL), F32), jax.ShapeDtypeStruct((1, LANES), F32)],
        compiler_params=_params(("arbitrary",)),
    )(s, w_down, u1, ln1_g, ln1_b, ln2_g, ln2_b, target)


def _ffn_up_bwd_ln1(dp3, w_up, du2, u1, ln1_g):
    T = u1.shape[0]
    tm = min(256, T)

    def body(dp_ref, w_ref, du2_ref, u1_ref, g_ref, du_ref, dub_ref, dg_ref, db_ref):
        @pl.when(pl.program_id(0) == 0)
        def _():
            dg_ref[...] = jnp.zeros_like(dg_ref)
            db_ref[...] = jnp.zeros_like(db_ref)

        dx1 = _dot_nt(dp_ref[0], w_ref[:, :D_FF]) + _dot_nt(dp_ref[1], w_ref[:, D_FF:]) + ALPHA * du2_ref[...]
        xh, rstd = _ln_stats(u1_ref[...])
        dg_ref[...] += jnp.sum(dx1 * xh, axis=0, keepdims=True)
        db_ref[...] += jnp.sum(dx1, axis=0, keepdims=True)
        du1 = _ln_bwd(dx1, xh, rstd, g_ref[...])
        du_ref[...] = du1
        dub_ref[...] = du1.astype(BF16)

    row = lambda n: pl.BlockSpec((tm, n), lambda i: (i, 0))
    const = lambda r, n: pl.BlockSpec((r, n), lambda i: (0, 0))
    vec = const(1, D_MODEL)
    return pl.pallas_call(
        body, name="ffn_up_bwd_ln1", grid=(T // tm,),
        in_specs=[pl.BlockSpec((2, tm, D_FF), lambda i: (0, i, 0)), const(D_MODEL, 2 * D_FF), row(D_MODEL), row(D_MODEL), vec],
        out_specs=[row(D_MODEL), row(D_MODEL), vec, vec],
        out_shape=[jax.ShapeDtypeStruct((T, D_MODEL), F32), jax.ShapeDtypeStruct((T, D_MODEL), BF16), jax.ShapeDtypeStruct((1, D_MODEL), F32),
                   jax.ShapeDtypeStruct((1, D_MODEL), F32)],
        compiler_params=_params(("arbitrary",)),
    )(dp3, w_up, du2, u1, ln1_g)


def _local_step(x, positions, w_in, b_gate, sinks, ln1_g, ln1_b, conv_b, ln2_g, ln2_b, target, later_weights,
                early_exchange=None, tail_exchange=None):
    T = x.shape[0]
    inv_freq = 1.0 / (ROPE_THETA ** (jnp.arange(0, HEAD_DIM, 2, dtype=F32) / HEAD_DIM))
    cos, sin = _rope_tables(positions.reshape(T, 1), jnp.tile(inv_freq, LANES // (HEAD_DIM // 2)).reshape(1, LANES))

    xb, qa, ka, va, qb, kb, vb, gl = _in_proj(x, w_in)
    ya = _swa_fwd(qa, ka, va, cos, sin, sinks)
    if isinstance(later_weights, tuple):
        exchange, finish = later_weights
        (yb, probs, betas), arrived = _sb_fwd(qb, kb, vb, exchange)
        later_weights = finish(arrived)
    else:
        (yb, probs, betas), _ = _sb_fwd(qb, kb, vb)
    wa, wb, wo, w_up, conv_w, w_down = later_weights
    h, u1, x1 = _mix_fwd(ya, yb, gl, x, wa, wb, wo, b_gate, ln1_g, ln1_b)

    ff_tn = D_FF // 2
    nff = D_FF // ff_tn
    tm = min(1024, T)
    p3 = _matmul(x1, w_up, kind="nn", name="ffn_up", grid=(T // tm, 2 * nff),
                 a_spec=pl.BlockSpec((tm, D_MODEL), lambda i, j: (i, 0)), b_spec=pl.BlockSpec((D_MODEL, ff_tn), lambda i, j: (0, j)),
                 out_spec=pl.BlockSpec((None, tm, ff_tn), lambda i, j: (j // nff, i, j % nff)),
                 out_shape=jax.ShapeDtypeStruct((2, T, D_FF), ACT_DTYPE))
    s = _conv_glu_fwd(p3, conv_w, conv_b)
    du2, du2b, dln2_g, dln2_b, loss = _ffn_down_loss(s, w_down, u1, ln1_g, ln1_b, ln2_g, ln2_b, target)

    ds = _matmul(du2b, w_down, kind="nt", name="ffn_down_bwd", grid=(T // tm, nff),
                 a_spec=pl.BlockSpec((tm, D_MODEL), lambda i, j: (i, 0)), b_spec=pl.BlockSpec((ff_tn, D_MODEL), lambda i, j: (j, 0)),
                 out_spec=pl.BlockSpec((tm, ff_tn), lambda i, j: (i, j)), out_shape=jax.ShapeDtypeStruct((T, D_FF), ACT_DTYPE))
    dp3, dcw_g, dcw_u, dcb_g, dcb_u = _conv_glu_bwd(p3, ds, conv_w, conv_b)
    tk = 256
    dw_down = _matmul(s, du2b, kind="tn", name="dw_down", grid=(D_FF // tk,),
                      a_spec=pl.BlockSpec((T, tk), lambda i: (0, i)), b_spec=pl.BlockSpec((T, D_MODEL), lambda i: (0, 0)),
                      out_spec=pl.BlockSpec((tk, D_MODEL), lambda i: (i, 0)), out_shape=jax.ShapeDtypeStruct((D_FF, D_MODEL), BF16))
    dw_up = _matmul(x1, dp3, kind="tn", name="dw_up", grid=(D_MODEL // 512, 2 * nff),
                    a_spec=pl.BlockSpec((T, 512), lambda i, j: (0, i)), b_spec=pl.BlockSpec((None, T, ff_tn), lambda i, j: (j // nff, 0, j % nff)),
                    out_spec=pl.BlockSpec((512, ff_tn), lambda i, j: (i, j)), out_shape=jax.ShapeDtypeStruct((D_MODEL, 2 * D_FF), BF16))
    du1, du1b, dln1_g, dln1_b = _ffn_up_bwd_ln1(dp3, w_up, du2, u1, ln1_g)
    dya, dyb, dgl, dta, dtb, db_gate = _mix_bwd(du1, ya, yb, gl, wa, wb, wo, b_gate)

    def dw_tn(a, g, name):
        rows, cols = a.shape[1], g.shape[1]
        tn = min(512, cols)
        return _matmul(a, g, kind="tn", name=name, grid=(rows // 512, cols // tn),
                       a_spec=pl.BlockSpec((T, 512), lambda i, j: (0, i)), b_spec=pl.BlockSpec((T, tn), lambda i, j: (0, j)),
                       out_spec=pl.BlockSpec((512, tn), lambda i, j: (i, j)), out_shape=jax.ShapeDtypeStruct((rows, cols), BF16))

    dwa = dw_tn(ya, dta, "dw_branch_a")
    dwb = dw_tn(yb, dtb, "dw_branch_b")
    dwo = dw_tn(h, du1b, "dw_out")

    grads = dict(
        b_gate=db_gate, w_branch_a=dwa, w_branch_b=dwb, w_out=dwo, ln1_g=dln1_g, ln1_b=dln1_b,
        w_up=dw_up, conv_w=jnp.concatenate([dcw_g, dcw_u], axis=1), conv_b=(dcb_g, dcb_u), w_down=dw_down, ln2_g=dln2_g, ln2_b=dln2_b)
    (dqb, dkb, dvb), early_out = _sb_bwd(qb, kb, vb, probs, betas, dyb, early_exchange(grads) if early_exchange else None)
    dqa, dka, dva, grads["sinks"] = _swa_bwd(qa, ka, va, cos, sin, sinks, dya)
    dproj = (dqa, dka, dva, dqb, dkb, dvb, dgl)
    grads["w_in"] = _dw_in(xb, dproj)
    grad_x, tail_out = _grad_x(dproj, w_in, du1, tail_exchange(grads, loss) if tail_exchange else None)
    return loss, grad_x, grads, early_out, tail_out


def _dw_in(xb, dproj):
    T = xb.shape[0]
    tn = 2 * LANES
    groups, start, k = [], 0, 0
    while k < len(IN_WIDTHS):
        if IN_WIDTHS[k] >= tn:
            groups.append((start, IN_WIDTHS[k] // tn, [(k, 0, tn)]))
            k += 1
        else:
            members, off = [], 0
            while off < tn:
                members.append((k, off, IN_WIDTHS[k]))
                off += IN_WIDTHS[k]
                k += 1
            groups.append((start, 1, members))
        start += groups[-1][1]

    def body(x_ref, *refs):
        pieces, o_ref = refs[:-1], refs[-1]
        j = pl.program_id(0)
        for first, steps, members in groups:
            @pl.when((j >= first) & (j < first + steps))
            def _(members=members):
                for k, off, width in members:
                    o_ref[:, off:off + width] = _dot_tn(x_ref[...], pieces[k][...]).astype(o_ref.dtype)

    specs = [None] * len(IN_WIDTHS)
    for first, steps, members in groups:
        for k, _, width in members:
            specs[k] = pl.BlockSpec((T, width), lambda j, first=first, steps=steps: (0, jnp.clip(j - first, 0, steps - 1)))
    return pl.pallas_call(
        body, name="dw_in", grid=(IN_TOTAL // tn,),
        in_specs=[pl.BlockSpec((T, D_MODEL), lambda j: (0, 0))] + specs, out_specs=pl.BlockSpec((D_MODEL, tn), lambda j: (0, j)),
        out_shape=jax.ShapeDtypeStruct((D_MODEL, IN_TOTAL), BF16), compiler_params=_params(("arbitrary",)),
    )(xb, *dproj)


def _grad_x(dproj, w_in, du1, exchange=None):
    T = du1.shape[0]
    tm = min(512, T)
    offs = np.cumsum((0,) + IN_WIDTHS)

    def body(*refs):
        pieces, (w_ref, du_ref, o_ref) = refs[:len(IN_WIDTHS)], refs[len(IN_WIDTHS):]
        acc = ALPHA * du_ref[...]
        for p_ref, a, b in zip(pieces, offs[:-1], offs[1:]):
            acc = acc + _dot_nt(p_ref[...].astype(BF16), w_ref[:, a:b])
        o_ref[...] = acc

    row = lambda n: pl.BlockSpec((tm, n), lambda i: (i, 0))
    (grad_x,), arrived = _hosted_call(
        body, "grad_x", (T // tm,), exchange,
        in_specs=[row(n) for n in IN_WIDTHS] + [pl.BlockSpec((D_MODEL, IN_TOTAL), lambda i: (0, 0)), row(D_MODEL)],
        out_specs=[row(D_MODEL)], out_shape=[jax.ShapeDtypeStruct((T, D_MODEL), F32)], semantics=("parallel",),
        args=(*dproj, w_in, du1))
    return grad_x, arrived


ANY = pl.BlockSpec(memory_space=pl.ANY)


def _all_gather(slabs, name):
    n = len(slabs)

    def body(*refs):
        ins, outs = refs[:n], refs[n:2 * n]
        send_sems, recv_sems, local_sems = refs[2 * n:]
        x, y, c = lax.axis_index("x"), lax.axis_index("y"), lax.axis_index("c")
        me, sibling = (x, y, c), (x, y, 1 - c)
        chips = [(1 - x, y), (x, 1 - y), (1 - x, 1 - y)]

        def slot(pos):
            return 4 * pos[0] + 2 * pos[1] + pos[2]

        def copy(a, k, block, to, from_input=False):
            return pltpu.make_async_remote_copy(
                src_ref=ins[a] if from_input else outs[a].at[slot(block)], dst_ref=outs[a].at[slot(block)],
                send_sem=send_sems.at[a, k], recv_sem=recv_sems.at[a, k], device_id=to, device_id_type=MESH)

        mine = [pltpu.make_async_copy(ins[a], outs[a].at[slot(me)], local_sems.at[a]) for a in range(n)]
        for cp in mine:
            cp.start()
        first = []
        for a in range(n):
            first.append(copy(a, 0, me, sibling, from_input=True))
            first += [copy(a, 1 + j, me, (*chip, c), from_input=True) for j, chip in enumerate(chips)]
        for cp in first:
            cp.start()
        passed = []
        for j, chip in enumerate(chips):
            for a in range(n):
                copy(a, 1 + j, (*chip, c), me).wait_recv()
                fwd = copy(a, 4 + j, (*chip, c), sibling)
                fwd.start()
                passed.append(fwd)
        for a in range(n):
            copy(a, 0, sibling, me).wait_recv()
            for j, chip in enumerate(chips):
                copy(a, 4 + j, (*chip, 1 - c), me).wait_recv()
        for cp in first + passed:
            cp.wait_send()
        for cp in mine:
            cp.wait()

    return pl.pallas_call(
        body, name=name,
        in_specs=[ANY] * n, out_specs=[ANY] * n,
        out_shape=[jax.ShapeDtypeStruct((N_DEV,) + s.shape, s.dtype) for s in slabs],
        scratch_shapes=[pltpu.SemaphoreType.DMA((n, 7)), pltpu.SemaphoreType.DMA((n, 7)), pltpu.SemaphoreType.DMA((n,))],
    )(*slabs)


def _row_tile(rows):
    for cand in range(256, 7, -8):
        if rows % cand == 0:
            return cand
    return rows


def _window(w):
    wp = max(-(-((w * r) % LANES + w) // LANES) for r in range(N_DEV)) * LANES
    assert all((w * r) // LANES * LANES + wp <= N_DEV * w for r in range(N_DEV))
    return wp


def _join_cols(slabs, name):
    _, R, w = slabs.shape
    tr = _row_tile(R)
    wp = _window(w)

    def body(g_ref, o_ref, pad_ref):
        if w % LANES == 0:
            for r in range(N_DEV):
                o_ref[:, w * r:w * (r + 1)] = g_ref[r]
            return
        o_ref[...] = jnp.zeros_like(o_ref)
        pad_ref[...] = jnp.zeros_like(pad_ref)
        for r in range(N_DEV):
            q, s = divmod(w * r, LANES)
            pad_ref[:, :w] = g_ref[r]
            y = pad_ref[...]
            if s:
                y = pltpu.roll(y, s, axis=1)
            o_ref[:, LANES * q:LANES * q + wp] += y

    return pl.pallas_call(
        body, name=name, grid=(R // tr,),
        in_specs=[pl.BlockSpec((N_DEV, tr, w), lambda i: (0, i, 0))], out_specs=pl.BlockSpec((tr, N_DEV * w), lambda i: (i, 0)),
        out_shape=jax.ShapeDtypeStruct((R, N_DEV * w), slabs.dtype), scratch_shapes=[pltpu.VMEM((tr, wp), slabs.dtype)],
        compiler_params=_params(("parallel",)),
    )(slabs)


def _split_cols(pieces, name):
    R = pieces[0].shape[0]
    widths = [p.shape[1] for p in pieces]
    total = sum(widths)
    w = total // N_DEV
    tr = _row_tile(R)
    wp = _window(w)
    offs = np.cumsum([0] + widths)
    dtype = pieces[0].dtype

    def body(*refs):
        ins, (o_ref, full_ref) = refs[:len(pieces)], refs[len(pieces):]
        for p_ref, a, b in zip(ins, offs[:-1], offs[1:]):
            full_ref[:, a:b] = p_ref[...].astype(dtype)
        for r in range(N_DEV):
            q, s = divmod(w * r, LANES)
            y = full_ref[:, LANES * q:LANES * q + wp]
            if s:
                y = pltpu.roll(y, wp - s, axis=1)
            o_ref[r] = y[:, :w]

    return pl.pallas_call(
        body, name=name, grid=(R // tr,),
        in_specs=[pl.BlockSpec((tr, n), lambda i: (i, 0)) for n in widths], out_specs=pl.BlockSpec((N_DEV, tr, w), lambda i: (0, i, 0)),
        out_shape=jax.ShapeDtypeStruct((N_DEV, R, w), dtype), scratch_shapes=[pltpu.VMEM((tr, total), dtype)],
        compiler_params=_params(("parallel",)),
    )(*pieces)


def _adamw(g, w, m, v):
    m_new = ADAM_B1 * m + (1.0 - ADAM_B1) * g
    v_new = ADAM_B2 * v + (1.0 - ADAM_B2) * jnp.square(g)
    m_hat = m_new / (1.0 - ADAM_B1 ** ADAM_STEP)
    v_hat = v_new / (1.0 - ADAM_B2 ** ADAM_STEP)
    return -ADAM_LR * (m_hat / (jnp.sqrt(v_hat) + ADAM_EPS) + ADAM_WD * w), m_new, v_new


def _sum_parts(p_ref):
    g = p_ref[0].astype(F32)
    for d in range(1, N_DEV):
        g = g + p_ref[d].astype(F32)
    return g


def _reduce_adamw(parts, w, m, v, name):
    R, C = w.shape
    tr = _row_tile(R)

    def body(p_ref, w_ref, m_ref, v_ref, g_ref, d_ref, mo_ref, vo_ref):
        g = _sum_parts(p_ref)
        g_ref[...] = g
        d_ref[...], mo_ref[...], vo_ref[...] = _adamw(g, w_ref[...], m_ref[...], v_ref[...])

    row = pl.BlockSpec((tr, C), lambda i: (i, 0))
    return pl.pallas_call(
        body, name=name, grid=(R // tr,),
        in_specs=[pl.BlockSpec((N_DEV, tr, C), lambda i: (0, i, 0)), row, row, row],
        out_specs=[row] * 4, out_shape=[jax.ShapeDtypeStruct((R, C), F32)] * 4,
        compiler_params=_params(("parallel",)),
    )(parts, w, m, v)


def _reduce_adamw_small(parts, ws, ms, vs):
    sizes = [a.shape[1] for a in ws]
    k = len(sizes)
    offs = np.cumsum([0] + [-(-n // LANES) * LANES for n in sizes])

    def body(*refs):
        p_ref, w_refs, m_refs, v_refs = refs[0], refs[1:1 + k], refs[1 + k:1 + 2 * k], refs[1 + 2 * k:1 + 3 * k]
        outs, loss_ref = refs[1 + 3 * k:-1], refs[-1]
        g_all = _sum_parts(p_ref)
        for j, n in enumerate(sizes):
            g = g_all[:, offs[j]:offs[j] + LANES * (-(-n // LANES))][:, :n]
            outs[4 * j][...] = g
            outs[4 * j + 1][...], outs[4 * j + 2][...], outs[4 * j + 3][...] = _adamw(g, w_refs[j][...], m_refs[j][...], v_refs[j][...])
        loss_ref[...] = g_all[:, offs[k]:offs[k] + LANES]

    vm = pl.BlockSpec(memory_space=pltpu.VMEM)
    out_shape = [jax.ShapeDtypeStruct((1, n), F32) for n in sizes for _ in range(4)] + [jax.ShapeDtypeStruct((1, LANES), F32)]
    res = pl.pallas_call(
        body, name="reduce_adamw_replicated", in_specs=[vm] * (1 + 3 * k), out_specs=[vm] * len(out_shape), out_shape=out_shape,
        compiler_params=_params(),
    )(parts, *ws, *ms, *vs)
    return [res[4 * j:4 * j + 4] for j in range(k)], res[-1]


COL_SHARDED = ("w_in", "w_branch_a", "w_branch_b", "w_up", "conv_w")
ROW_SHARDED = ("w_out", "w_down")
SMALL = ("b_gate", "sinks", "ln1_g", "ln1_b", "conv_b", "ln2_g", "ln2_b")
ORDER = ("w_in", "b_gate", "sinks", "w_branch_a", "w_branch_b", "w_out", "ln1_g", "ln1_b", "w_up", "conv_w", "conv_b", "w_down", "ln2_g", "ln2_b")


def _pad_lanes(a):
    pad = (-a.shape[-1]) % LANES
    return a if pad == 0 else jnp.pad(a, ((0, 0), (0, pad)))


def kernel(x, positions, w_in, b_gate, sinks, w_branch_a, w_branch_b, w_out, ln1_g, ln1_b, w_up, conv_w, conv_b, w_down, ln2_g, ln2_b, loss_target, m_w_in, m_b_gate, m_sinks, m_w_branch_a, m_w_branch_b, m_w_out, m_ln1_g, m_ln1_b, m_w_up, m_conv_w, m_conv_b, m_w_down, m_ln2_g, m_ln2_b, v_w_in, v_b_gate, v_sinks, v_w_branch_a, v_w_branch_b, v_w_out, v_ln1_g, v_ln1_b, v_w_up, v_conv_w, v_conv_b, v_w_down, v_ln2_g, v_ln2_b):
    args = dict(locals())
    sharded = COL_SHARDED + ROW_SHARDED
    w = {n: args[n][0] if n in sharded else args[n] for n in ORDER}
    m = {n: args["m_" + n][0] if n in sharded else args["m_" + n] for n in ORDER}
    v = {n: args["v_" + n][0] if n in sharded else args["v_" + n] for n in ORDER}

    travel = {n: (w[n] if n == "conv_w" else w[n].astype(BF16)) for n in sharded}
    (g_in,) = _all_gather([travel["w_in"]], "all_gather_w_in")
    w_in_full = _join_cols(g_in, "join_w_in")
    later = ("w_branch_a", "w_branch_b", "w_out", "w_up", "conv_w", "w_down")

    def join(name, slabs):
        return _join_cols(slabs, "join_" + name) if name in COL_SHARDED else slabs.reshape(-1, slabs.shape[-1])

    def split(name, grad):
        if name in COL_SHARDED:
            return _split_cols(grad if isinstance(grad, tuple) else (grad,), "split_d" + name)
        return grad.reshape((N_DEV, -1, grad.shape[-1]))

    def early_exchange(grads):
        return _Exchange([split(n, grads[n]) for n in later], ["scatter"] * len(later))

    def tail_exchange(grads, loss):
        small_pack = jnp.concatenate(
            [_pad_lanes(p) for n in SMALL for p in (grads[n] if isinstance(grads[n], tuple) else (grads[n],))] + [loss], axis=1)
        return _Exchange([split("w_in", grads["w_in"]), small_pack], ["scatter", "gather"])

    gather_later = _Exchange([travel[n] for n in later], ["gather"] * len(later))
    _, grad_x, _, early_out, (recv_w_in, small_parts) = _local_step(
        x[0], positions[0], w_in_full, w["b_gate"], w["sinks"][0], w["ln1_g"], w["ln1_b"], w["conv_b"], w["ln2_g"], w["ln2_b"], loss_target[0],
        (gather_later, lambda arrived: [join(n, a) for n, a in zip(later, arrived)]), early_exchange, tail_exchange)
    recv = dict(zip(later, early_out), w_in=recv_w_in)

    res = {n: _reduce_adamw(recv[n], w[n], m[n], v[n], "reduce_adamw_" + n) for n in sharded}
    small_res, loss_sum = _reduce_adamw_small(small_parts, [w[n] for n in SMALL], [m[n] for n in SMALL], [v[n] for n in SMALL])
    res.update(zip(SMALL, small_res))
    out = [loss_sum[0, 0], grad_x[None]]
    for k in range(4):
        out += [res[n][k][None] if n in sharded else res[n][k] for n in ORDER]
    return tuple(out)
```

```python
import functools

import jax
import jax.numpy as jnp
import numpy as np
from jax import lax
from jax.experimental import pallas as pl
from jax.experimental.pallas import tpu as pltpu

D_MODEL = 1024
HEAD_DIM = 64
SWA_Q_HEADS = 8
SWA_KV_HEADS = 2
SB_HEADS = 8
WINDOW = 128
ROPE_THETA = 10000.0
D_FF = 2816
LN_EPS = 1e-5
DEPTH = 1
ALPHA = (2.0 * DEPTH) ** 0.25
SWA_Q_WIDTH = SWA_Q_HEADS * HEAD_DIM
SWA_KV_WIDTH = SWA_KV_HEADS * HEAD_DIM
SB_WIDTH = SB_HEADS * HEAD_DIM
GATE_WIDTH = 2 * D_MODEL
IN_WIDTHS = (SWA_Q_WIDTH, SWA_KV_WIDTH, SWA_KV_WIDTH, SB_WIDTH, SB_WIDTH, SB_WIDTH, GATE_WIDTH)
IN_TOTAL = sum(IN_WIDTHS)
ATTN_SCALE = HEAD_DIM ** -0.5

ADAM_LR = 0.001
ADAM_B1 = 0.9
ADAM_B2 = 0.999
ADAM_EPS = 1e-08
ADAM_WD = 0.01
ADAM_STEP = 10

N_DEV = 8
LANES = 128
SB_BLOCK = 256
VMEM_LIMIT = 56 * 1024 * 1024

F32 = jnp.float32
BF16 = jnp.bfloat16
ACT_DTYPE = BF16
MESH = pl.DeviceIdType.MESH


def _params(sem=None):
    return pltpu.CompilerParams(dimension_semantics=sem, vmem_limit_bytes=VMEM_LIMIT)


def _dot(a, b):
    return jnp.dot(a, b, preferred_element_type=F32)


def _dot_nt(a, b):
    return lax.dot_general(a, b, (((1,), (1,)), ((), ())), preferred_element_type=F32)


def _dot_tn(a, b):
    return lax.dot_general(a, b, (((0,), (0,)), ((), ())), preferred_element_type=F32)


def _split_bf16(v):
    hi = v.astype(BF16)
    lo = (v - hi.astype(F32)).astype(BF16)
    return hi, lo


def _matmul(a, b, *, kind, out_shape, grid, a_spec, b_spec, out_spec, name, add=None, add_spec=None, add_scale=1.0):
    dot = {"nn": _dot, "nt": _dot_nt, "tn": _dot_tn}[kind]

    def body(*refs):
        if add is None:
            a_ref, b_ref, o_ref = refs
        else:
            a_ref, b_ref, add_ref, o_ref = refs
        r = dot(a_ref[...].astype(BF16), b_ref[...].astype(BF16))
        if add is not None:
            r = r + add_scale * add_ref[...]
        o_ref[...] = r.astype(o_ref.dtype)

    ins = [a, b] + ([] if add is None else [add])
    specs = [a_spec, b_spec] + ([] if add is None else [add_spec])
    return pl.pallas_call(
        body, name=name, grid=grid, in_specs=specs, out_specs=out_spec, out_shape=out_shape,
        compiler_params=_params(("parallel",) * len(grid)),
    )(*ins)


def _rope_tables(pos_col, inv_freq_lanes):
    T = pos_col.shape[0]
    tm = min(512, T)

    def body(pos_ref, f_ref, cos_ref, sin_ref):
        ang = pos_ref[...].astype(F32) * f_ref[...]
        cos_ref[...] = jnp.cos(ang)
        sin_ref[...] = jnp.sin(ang)

    return pl.pallas_call(
        body, name="rope_tables", grid=(T // tm,),
        in_specs=[pl.BlockSpec((tm, 1), lambda i: (i, 0)), pl.BlockSpec((1, LANES), lambda i: (0, 0))],
        out_specs=[pl.BlockSpec((tm, LANES), lambda i: (i, 0))] * 2,
        out_shape=[jax.ShapeDtypeStruct((T, LANES), F32)] * 2,
        compiler_params=_params(("parallel",)),
    )(pos_col, inv_freq_lanes)


def _lane_iota(shape):
    return lax.broadcasted_iota(jnp.int32, shape, len(shape) - 1)


def _rot_half(t):
    first = (_lane_iota(t.shape) % HEAD_DIM) < (HEAD_DIM // 2)
    return jnp.where(first, -pltpu.roll(t, LANES - HEAD_DIM // 2, axis=1), pltpu.roll(t, HEAD_DIM // 2, axis=1))


def _rope(t, cos, sin):
    return t * cos + _rot_half(t) * sin


def _rope_transpose(d, cos, sin):
    return d * cos - _rot_half(d * sin)


_IN_DTYPES = (F32, F32, BF16, BF16, BF16, BF16, F32)


def _in_proj(x, w_in_t):
    T = x.shape[0]
    tm = min(512, T)
    offs = np.cumsum((0,) + IN_WIDTHS)

    def body(x_ref, w_ref, xb_ref, *outs):
        xb = x_ref[...].astype(BF16)
        xb_ref[...] = xb
        for o_ref, a, b in zip(outs, offs[:-1], offs[1:]):
            o_ref[...] = _dot_nt(xb, w_ref[a:b, :]).astype(o_ref.dtype)

    row = lambda n: pl.BlockSpec((tm, n), lambda i: (i, 0))
    return pl.pallas_call(
        body, name="in_proj", grid=(T // tm,),
        in_specs=[row(D_MODEL), pl.BlockSpec((IN_TOTAL, D_MODEL), lambda i: (0, 0))],
        out_specs=[row(D_MODEL)] + [row(n) for n in IN_WIDTHS],
        out_shape=[jax.ShapeDtypeStruct((T, D_MODEL), BF16)] + [jax.ShapeDtypeStruct((T, n), dt) for n, dt in zip(IN_WIDTHS, _IN_DTYPES)],
        compiler_params=_params(("parallel",)),
    )(x, w_in_t)


def _swa_specs(T):
    blk = WINDOW
    cur = lambda n: pl.BlockSpec((blk, n), lambda i: (i, 0))
    prev = lambda n: pl.BlockSpec((blk, n), lambda i: (jnp.maximum(i - 1, 0), 0))
    return blk, cur, prev


SWA_GROUP = SWA_Q_HEADS // SWA_KV_HEADS


def _swa_stack(pairs):
    lane = _lane_iota(pairs[0].shape)
    zero = jnp.zeros((), pairs[0].dtype)
    rows = []
    for h in range(SWA_Q_HEADS):
        hh, g = h % 2, h // SWA_GROUP
        x = jnp.where((lane >= hh * HEAD_DIM) & (lane < (hh + 1) * HEAD_DIM), pairs[h // 2], zero)
        rows.append(x if hh == g else pltpu.roll(x, HEAD_DIM, axis=1))
    return jnp.concatenate(rows, axis=0)


def _swa_unstack(stacked, blk):
    low = _lane_iota((blk, LANES)) < HEAD_DIM
    pairs = []
    for pp in range(SWA_Q_HEADS // 2):
        halves = []
        for hh in range(2):
            h = 2 * pp + hh
            x = stacked[h * blk:(h + 1) * blk]
            halves.append(x if hh == h // SWA_GROUP else pltpu.roll(x, HEAD_DIM, axis=1))
        pairs.append(jnp.where(low, halves[0], halves[1]))
    return pairs


def _swa_probs(i, q_stack, kwin, sink_ref, blk):
    r = lax.broadcasted_iota(jnp.int32, (blk, 2 * blk), 0)
    c = lax.broadcasted_iota(jnp.int32, (blk, 2 * blk), 1)
    rel = blk + r - c
    valid = (rel >= 0) & (rel < WINDOW) & ((c >= blk) | (i > 0))
    bias = jnp.concatenate([jnp.where(valid, 0.0, -1e30)] * SWA_Q_HEADS, axis=0)
    head = lax.broadcasted_iota(jnp.int32, (SWA_Q_HEADS * blk, 1), 0) // blk
    sink = jnp.zeros((SWA_Q_HEADS * blk, 1), F32)
    for h in range(SWA_Q_HEADS):
        sink = jnp.where(head == h, sink_ref[h], sink)
    s = _dot_nt(q_stack, kwin) * ATTN_SCALE + bias
    m = jnp.maximum(jnp.max(s, axis=1, keepdims=True), sink)
    p = jnp.exp(s - m)
    es = jnp.exp(sink - m)
    den = jnp.sum(p, axis=1, keepdims=True) + es
    return p / den, es / den


def _swa_inputs(q_ref, kp_ref, kc_ref, vp_ref, vc_ref, cp_ref, cc_ref, sp_ref, sc_ref):
    cc, sc = cc_ref[...], sc_ref[...]
    kwin = jnp.concatenate([_rope(kp_ref[...], cp_ref[...], sp_ref[...]), _rope(kc_ref[...], cc, sc)], axis=0).astype(BF16)
    vwin = jnp.concatenate([vp_ref[...], vc_ref[...]], axis=0)
    q_stack = _swa_stack([_rope(q_ref[:, pp * LANES:(pp + 1) * LANES], cc, sc) for pp in range(SWA_Q_HEADS // 2)]).astype(BF16)
    return q_stack, kwin, vwin


def _swa_fwd(qa, ka, va, cos, sin, sinks):
    T = qa.shape[0]
    blk, cur, prev = _swa_specs(T)

    def body(sink_ref, q_ref, kp_ref, kc_ref, vp_ref, vc_ref, cp_ref, cc_ref, sp_ref, sc_ref, o_ref):
        q_stack, kwin, vwin = _swa_inputs(q_ref, kp_ref, kc_ref, vp_ref, vc_ref, cp_ref, cc_ref, sp_ref, sc_ref)
        probs, _ = _swa_probs(pl.program_id(0), q_stack, kwin, sink_ref, blk)
        for pp, tile in enumerate(_swa_unstack(_dot(probs.astype(BF16), vwin), blk)):
            o_ref[:, pp * LANES:(pp + 1) * LANES] = tile.astype(o_ref.dtype)

    return pl.pallas_call(
        body, name="swa_fwd", grid=(T // blk,),
        in_specs=[pl.BlockSpec(memory_space=pltpu.SMEM), cur(SWA_Q_WIDTH), prev(LANES), cur(LANES), prev(LANES), cur(LANES),
                  prev(LANES), cur(LANES), prev(LANES), cur(LANES)],
        out_specs=cur(SWA_Q_WIDTH),
        out_shape=jax.ShapeDtypeStruct((T, SWA_Q_WIDTH), BF16),
        compiler_params=_params(("parallel",)),
    )(sinks, qa, ka, ka, va, va, cos, cos, sin, sin)


def _swa_bwd(qa, ka, va, cos, sin, sinks, dya):
    T = qa.shape[0]
    blk, cur, prev = _swa_specs(T)
    full = lambda n: pl.BlockSpec((T, n), lambda i: (0, 0))

    def body(sink_ref, q_ref, kp_ref, kc_ref, vp_ref, vc_ref, cp_ref, cc_ref, sp_ref, sc_ref, do_ref,
             dq_ref, dk_out, dv_out, dsink_ref, dk_ref, dv_ref):
        i = pl.program_id(0)

        @pl.when(i == 0)
        def _():
            dk_ref[...] = jnp.zeros_like(dk_ref)
            dv_ref[...] = jnp.zeros_like(dv_ref)
            dsink_ref[...] = jnp.zeros_like(dsink_ref)

        cp, cc, sp, sc = cp_ref[...], cc_ref[...], sp_ref[...], sc_ref[...]
        q_stack, kwin, vwin = _swa_inputs(q_ref, kp_ref, kc_ref, vp_ref, vc_ref, cp_ref, cc_ref, sp_ref, sc_ref)
        probs, psink = _swa_probs(i, q_stack, kwin, sink_ref, blk)
        do_stack = _swa_stack([do_ref[:, pp * LANES:(pp + 1) * LANES] for pp in range(SWA_Q_HEADS // 2)])
        dp = _dot_nt(do_stack, vwin)
        dsum = jnp.sum(probs * dp, axis=1, keepdims=True)
        ds = (probs * (dp - dsum) * ATTN_SCALE).astype(BF16)
        for pp, tile in enumerate(_swa_unstack(_dot(ds, kwin), blk)):
            dq_ref[:, pp * LANES:(pp + 1) * LANES] = _rope_transpose(tile, cc, sc).astype(dq_ref.dtype)
        dkw = _dot_tn(ds, q_stack)
        dvw = _dot_tn(probs.astype(BF16), do_stack)
        lane1 = _lane_iota((1, LANES))
        sink_share = psink * dsum
        dsink = jnp.zeros((1, LANES), F32)
        for h in range(SWA_Q_HEADS):
            dsink = dsink + jnp.where(lane1 == h, -jnp.sum(sink_share[h * blk:(h + 1) * blk]), 0.0)
        dsink_ref[...] += dsink
        ip = jnp.maximum(i - 1, 0)
        rows_p = pl.ds(pl.multiple_of(ip * blk, blk), blk)
        rows_c = pl.ds(pl.multiple_of(i * blk, blk), blk)
        dk_ref[rows_p, :] += _rope_transpose(dkw[:blk], cp, sp)
        dv_ref[rows_p, :] += dvw[:blk]
        dk_ref[rows_c, :] += _rope_transpose(dkw[blk:], cc, sc)
        dv_ref[rows_c, :] += dvw[blk:]

        @pl.when(i == T // blk - 1)
        def _():
            dk_out[...] = dk_ref[...].astype(BF16)
            dv_out[...] = dv_ref[...].astype(BF16)

    return pl.pallas_call(
        body, name="swa_bwd", grid=(T // blk,),
        in_specs=[pl.BlockSpec(memory_space=pltpu.SMEM), cur(SWA_Q_WIDTH), prev(LANES), cur(LANES), prev(LANES), cur(LANES),
                  prev(LANES), cur(LANES), prev(LANES), cur(LANES), cur(SWA_Q_WIDTH)],
        out_specs=[cur(SWA_Q_WIDTH), full(LANES), full(LANES), pl.BlockSpec((1, LANES), lambda i: (0, 0))],
        out_shape=[jax.ShapeDtypeStruct((T, SWA_Q_WIDTH), BF16), jax.ShapeDtypeStruct((T, LANES), BF16),
                   jax.ShapeDtypeStruct((T, LANES), BF16), jax.ShapeDtypeStruct((1, LANES), F32)],
        scratch_shapes=[pltpu.VMEM((T, LANES), F32)] * 2,
        compiler_params=_params(("arbitrary",)),
    )(sinks, qa, ka, ka, va, va, cos, cos, sin, sin, dya)


class _Exchange:
    FLIPS = [(fx, fy, fc) for fx in (0, 1) for fy in (0, 1) for fc in (0, 1) if (fx, fy, fc) != (0, 0, 0)]

    def __init__(self, arrays, kinds):
        self.arrays, self.kinds, self.n = list(arrays), list(kinds), len(arrays)

    def out_shape(self):
        return [jax.ShapeDtypeStruct(a.shape if k == "scatter" else (N_DEV,) + a.shape, a.dtype) for a, k in zip(self.arrays, self.kinds)]

    def scratch(self):
        return [pltpu.SemaphoreType.DMA((self.n, 7)), pltpu.SemaphoreType.DMA((self.n, 7)), pltpu.SemaphoreType.DMA((self.n,))]

    def bind(self, ins, outs, send_sems, recv_sems, local_sems):
        x, y, c = lax.axis_index("x"), lax.axis_index("y"), lax.axis_index("c")
        me = 4 * x + 2 * y + c
        local, remote = [], []
        for a, kind in enumerate(self.kinds):
            mine = ins[a].at[me] if kind == "scatter" else ins[a]
            local.append(pltpu.make_async_copy(mine, outs[a].at[me], local_sems.at[a]))
            for k, (fx, fy, fc) in enumerate(self.FLIPS):
                peer = (x ^ fx, y ^ fy, c ^ fc)
                peer_slot = 4 * peer[0] + 2 * peer[1] + peer[2]
                src = ins[a].at[peer_slot] if kind == "scatter" else ins[a]
                sems = dict(send_sem=send_sems.at[a, k], recv_sem=recv_sems.at[a, k], device_id=peer, device_id_type=MESH)
                remote.append((pltpu.make_async_remote_copy(src_ref=src, dst_ref=outs[a].at[me], **sems),
                               pltpu.make_async_remote_copy(src_ref=src, dst_ref=outs[a].at[peer_slot], **sems)))

        def start():
            for cp in local:
                cp.start()
            for send, _ in remote:
                send.start()

        def wait():
            for send, arrival in remote:
                arrival.wait_recv()
                send.wait_send()
            for cp in local:
                cp.wait()

        return start, wait


def _hosted_call(body, name, grid, exchange, *, in_specs, out_specs, out_shape, semantics, args, scratch=(), prefetch=()):
    n = 0 if exchange is None else exchange.n
    n_pre, n_in, n_out, n_scratch = len(prefetch), len(in_specs), len(out_specs), len(scratch)

    def hosted(*refs):
        pre, rest = refs[:n_pre], refs[n_pre:]
        ins, rest = rest[:n_in], rest[n_in:]
        ex_ins, rest = rest[:n], rest[n:]
        outs, rest = rest[:n_out], rest[n_out:]
        ex_outs, rest = rest[:n], rest[n:]
        own, sems = rest[:n_scratch], rest[n_scratch:]
        if exchange is None:
            return body(*pre, *ins, *outs, *own)
        start, wait = exchange.bind(ex_ins, ex_outs, *sems)
        ids = [pl.program_id(d) for d in range(len(grid))]
        first = functools.reduce(jnp.logical_and, [i == 0 for i in ids])
        last = functools.reduce(jnp.logical_and, [i == g - 1 for i, g in zip(ids, grid)])
        pl.when(first)(start)
        body(*pre, *ins, *outs, *own)
        pl.when(last)(wait)

    grid_spec = pltpu.PrefetchScalarGridSpec(
        num_scalar_prefetch=n_pre, grid=grid, in_specs=list(in_specs) + [ANY] * n, out_specs=list(out_specs) + [ANY] * n,
        scratch_shapes=list(scratch) + ([] if exchange is None else exchange.scratch()))
    res = pl.pallas_call(
        hosted, name=name, grid_spec=grid_spec, out_shape=list(out_shape) + ([] if exchange is None else exchange.out_shape()),
        compiler_params=_params(semantics if exchange is None else ("arbitrary",) * len(grid)),
    )(*prefetch, *args, *([] if exchange is None else exchange.arrays))
    return res[:n_out], res[n_out:]


SOFTPLUS_LINEAR_FROM = 30.0


def _sb_scores(qm, k, valid):
    z = _dot_nt(qm, k)
    sp = jnp.where(z > SOFTPLUS_LINEAR_FROM, z, jnp.log(1.0 + jnp.exp(z)))
    log_beta = z - sp
    if valid is not None:
        sp = jnp.where(valid, sp, 0.0)
    return sp, log_beta


def _tri2(B, cmp):
    r = lax.broadcasted_iota(jnp.int32, (2 * B, B), 0) % B
    c = lax.broadcasted_iota(jnp.int32, (2 * B, B), 1)
    return cmp(r, c).astype(BF16)


def _tri_sum(v, tri2):
    hi, lo = _split_bf16(v)
    return _dot(jnp.concatenate([hi, lo], axis=1), tri2)


def _head_masks(x):
    low = _lane_iota(x.shape) < HEAD_DIM
    zero = jnp.zeros((), x.dtype)
    return jnp.where(low, x, zero), jnp.where(low, zero, x)


def _strictly_below(B):
    r = lax.broadcasted_iota(jnp.int32, (B, B), 0)
    c = lax.broadcasted_iota(jnp.int32, (B, B), 1)
    return c < r


def _sb_grid(T, descending):
    B = min(SB_BLOCK, T)
    n = T // B
    pairs = [(i, j) for i in range(n) for j in (range(i, -1, -1) if descending else range(i + 1))]
    return B, jnp.asarray([p[0] for p in pairs], jnp.int32), jnp.asarray([p[1] for p in pairs], jnp.int32)


N_PAIRS = SB_HEADS // 2
PAIR_COLS = [slice(p * LANES, (p + 1) * LANES) for p in range(N_PAIRS)]


def _sb_fwd(qb, kb, vb, exchange=None):
    T = qb.shape[0]
    B, i_tab, j_tab = _sb_grid(T, descending=True)
    n = T // B

    def body(i_ref, j_ref, q_ref, k_ref, v_ref, o_ref, a_ref, b_ref, acc_ref, c_ref, tri_ref):
        s = pl.program_id(0)
        i, j = i_ref[s], j_ref[s]

        @pl.when(s == 0)
        def _():
            tri_ref[...] = _tri2(B, lambda r, c: r > c)

        @pl.when(j == i)
        def _():
            acc_ref[...] = jnp.zeros_like(acc_ref)
            c_ref[...] = jnp.zeros_like(c_ref)

        def block(valid):
            for p, cols in enumerate(PAIR_COLS):
                qms = _head_masks(q_ref[:, cols] * ATTN_SCALE)
                k = k_ref[:, cols]
                probs = []
                for hh in range(2):
                    h = 2 * p + hh
                    sp, lb = _sb_scores(qms[hh], k, valid)
                    c = c_ref[h]
                    a = jnp.exp(lb - (c + _tri_sum(sp, tri_ref[...])))
                    beta = jnp.exp(lb)
                    if valid is not None:
                        a = jnp.where(valid, a, 0.0)
                        beta = jnp.where(valid, beta, 0.0)
                    probs.append(a.astype(BF16))
                    a_ref[h] = probs[-1]
                    b_ref[h] = beta.astype(BF16)
                    c_ref[h] = c + jnp.sum(sp, axis=1, keepdims=True)
                acc_ref[:, cols] += _dot(jnp.concatenate(probs, axis=1), jnp.concatenate(_head_masks(v_ref[:, cols]), axis=0))

        pl.when(j == i)(lambda: block(_strictly_below(B)))
        pl.when(j != i)(lambda: block(None))

        @pl.when(j == 0)
        def _():
            o_ref[...] = acc_ref[...].astype(o_ref.dtype)

    q_spec = pl.BlockSpec((B, SB_WIDTH), lambda s, i_ref, j_ref: (i_ref[s], 0))
    k_spec = pl.BlockSpec((B, SB_WIDTH), lambda s, i_ref, j_ref: (j_ref[s], 0))
    tile = pl.BlockSpec((None, None, SB_HEADS, B, B), lambda s, i_ref, j_ref: (i_ref[s], j_ref[s], 0, 0, 0))
    saved = jax.ShapeDtypeStruct((n, n, SB_HEADS, B, B), BF16)
    return _hosted_call(
        body, "sb_fwd", (int(i_tab.shape[0]),), exchange, prefetch=(i_tab, j_tab),
        in_specs=[q_spec, k_spec, k_spec], out_specs=[q_spec, tile, tile],
        out_shape=[jax.ShapeDtypeStruct((T, SB_WIDTH), BF16), saved, saved],
        scratch=[pltpu.VMEM((B, SB_WIDTH), F32), pltpu.VMEM((SB_HEADS, B, 1), F32), pltpu.VMEM((2 * B, B), BF16)],
        semantics=("arbitrary",), args=(qb, kb, vb))


def _sb_bwd(qb, kb, vb, probs, betas, dyb, exchange=None):
    T = qb.shape[0]
    B, i_tab, j_tab = _sb_grid(T, descending=False)
    n_steps = int(i_tab.shape[0])

    def block_diag_t(x):
        xt = x.T
        top = lax.broadcasted_iota(jnp.int32, xt.shape, 0) < HEAD_DIM
        zero = jnp.zeros((), x.dtype)
        return jnp.concatenate([jnp.where(top, xt, zero), jnp.where(top, zero, xt)], axis=1)

    def body(i_ref, j_ref, q_ref, k_ref, v_ref, a_ref, b_ref, do_ref, dq_ref, dk_out, dv_out,
             dq_acc, cg_ref, dkt_ref, dvt_ref, tri_ref, qt_ref, dot_ref):
        s = pl.program_id(0)
        i, j = i_ref[s], j_ref[s]

        @pl.when(s == 0)
        def _():
            dkt_ref[...] = jnp.zeros_like(dkt_ref)
            dvt_ref[...] = jnp.zeros_like(dvt_ref)
            tri_ref[...] = _tri2(B, lambda r, c: r < c)[:B]

        @pl.when(j == 0)
        def _():
            dq_acc[...] = jnp.zeros_like(dq_acc)
            cg_ref[...] = jnp.zeros_like(cg_ref)
            for p, cols in enumerate(PAIR_COLS):
                qt_ref[p] = block_diag_t(q_ref[:, cols] * ATTN_SCALE)
                dot_ref[p] = block_diag_t(do_ref[:, cols])

        for p, cols in enumerate(PAIR_COLS):
            doms = _head_masks(do_ref[:, cols])
            k, v = k_ref[:, cols], v_ref[:, cols]
            dzs = []
            for hh in range(2):
                h = 2 * p + hh
                g = a_ref[h].astype(F32) * _dot_nt(doms[hh], v)
                cg = cg_ref[h]
                gsum = g + (cg + _dot(g.astype(BF16), tri_ref[...]))
                dzs.append((g - b_ref[h].astype(F32) * gsum).astype(BF16))
                cg_ref[h] = cg + jnp.sum(g, axis=1, keepdims=True)
            dq_acc[:, cols] += _dot(jnp.concatenate(dzs, axis=1), jnp.concatenate(_head_masks(k), axis=0))
            dkt_ref[j, cols, :] += _dot(qt_ref[p], jnp.concatenate(dzs, axis=0))
            dvt_ref[j, cols, :] += _dot(dot_ref[p], jnp.concatenate([a_ref[2 * p], a_ref[2 * p + 1]], axis=0))

        @pl.when(j == i)
        def _():
            dq_ref[...] = (dq_acc[...] * ATTN_SCALE).astype(dq_ref.dtype)

        @pl.when(s == n_steps - 1)
        def _():
            for jb in range(T // B):
                dk_out[jb * B:(jb + 1) * B, :] = dkt_ref[jb].T.astype(BF16)
                dv_out[jb * B:(jb + 1) * B, :] = dvt_ref[jb].T.astype(BF16)

    q_spec = pl.BlockSpec((B, SB_WIDTH), lambda s, i_ref, j_ref: (i_ref[s], 0))
    k_spec = pl.BlockSpec((B, SB_WIDTH), lambda s, i_ref, j_ref: (j_ref[s], 0))
    tile = pl.BlockSpec((None, None, SB_HEADS, B, B), lambda s, i_ref, j_ref: (i_ref[s], j_ref[s], 0, 0, 0))
    full = pl.BlockSpec((T, SB_WIDTH), lambda s, i_ref, j_ref: (0, 0))
    return _hosted_call(
        body, "sb_bwd", (n_steps,), exchange, prefetch=(i_tab, j_tab),
        in_specs=[q_spec, k_spec, k_spec, tile, tile, q_spec], out_specs=[q_spec, full, full],
        out_shape=[jax.ShapeDtypeStruct((T, SB_WIDTH), BF16)] * 3,
        scratch=[pltpu.VMEM((B, SB_WIDTH), F32), pltpu.VMEM((SB_HEADS, B, 1), F32), pltpu.VMEM((T // B, SB_WIDTH, B), F32),
                 pltpu.VMEM((T // B, SB_WIDTH, B), F32), pltpu.VMEM((B, B), BF16), pltpu.VMEM((N_PAIRS, LANES, 2 * B), BF16),
                 pltpu.VMEM((N_PAIRS, LANES, 2 * B), BF16)],
        semantics=("arbitrary",), args=(qb, kb, vb, probs, betas, dyb))


def _ln_stats(u):
    mu = jnp.mean(u, axis=-1, keepdims=True)
    xc = u - mu
    var = jnp.mean(xc * xc, axis=-1, keepdims=True)
    rstd = lax.rsqrt(var + LN_EPS)
    return xc * rstd, rstd


def _ln_bwd(dy, xhat, rstd, g):
    dxh = dy * g
    return rstd * (dxh - jnp.mean(dxh, axis=-1, keepdims=True) - xhat * jnp.mean(dxh * xhat, axis=-1, keepdims=True))


def _gates(gl_ref, bg_ref):
    ga = jax.nn.sigmoid(gl_ref[:, :D_MODEL] + bg_ref[:, :D_MODEL])
    gb = jax.nn.sigmoid(gl_ref[:, D_MODEL:] + bg_ref[:, D_MODEL:])
    return ga, gb


def _mix_fwd(ya, yb, gl, x, wa, wb, wo, b_gate, ln1_g, ln1_b):
    T = x.shape[0]
    tm = min(512, T)

    def body(ya_ref, yb_ref, gl_ref, x_ref, wa_ref, wb_ref, wo_ref, bg_ref, g_ref, b_ref, h_ref, u_ref, x1_ref):
        ga, gb = _gates(gl_ref, bg_ref)
        h = (ga * _dot(ya_ref[...], wa_ref[...]) + gb * _dot(yb_ref[...], wb_ref[...])).astype(BF16)
        h_ref[...] = h
        u = ALPHA * x_ref[...] + _dot(h, wo_ref[...])
        u_ref[...] = u
        xhat, _ = _ln_stats(u)
        x1_ref[...] = (xhat * g_ref[...] + b_ref[...]).astype(BF16)

    row = lambda n: pl.BlockSpec((tm, n), lambda i: (i, 0))
    const = lambda r, n: pl.BlockSpec((r, n), lambda i: (0, 0))
    return pl.pallas_call(
        body, name="mix_fwd", grid=(T // tm,),
        in_specs=[row(SWA_Q_WIDTH), row(SB_WIDTH), row(GATE_WIDTH), row(D_MODEL), const(SWA_Q_WIDTH, D_MODEL), const(SB_WIDTH, D_MODEL),
                  const(D_MODEL, D_MODEL), const(1, GATE_WIDTH), const(1, D_MODEL), const(1, D_MODEL)],
        out_specs=[row(D_MODEL)] * 3,
        out_shape=[jax.ShapeDtypeStruct((T, D_MODEL), BF16), jax.ShapeDtypeStruct((T, D_MODEL), F32), jax.ShapeDtypeStruct((T, D_MODEL), BF16)],
        compiler_params=_params(("parallel",)),
    )(ya, yb, gl, x, wa, wb, wo, b_gate, ln1_g, ln1_b)


def _mix_bwd(du1, ya, yb, gl, wa, wb, wo, b_gate):
    T = du1.shape[0]
    tm = min(512, T)

    def body(du_ref, ya_ref, yb_ref, gl_ref, wa_ref, wb_ref, wo_ref, bg_ref, dya_ref, dyb_ref, dgl_ref, dta_ref, dtb_ref, dbg_ref):
        @pl.when(pl.program_id(0) == 0)
        def _():
            dbg_ref[...] = jnp.zeros_like(dbg_ref)

        dh = _dot_nt(du_ref[...].astype(BF16), wo_ref[...])
        ga, gb = _gates(gl_ref, bg_ref)
        for gate, y_ref, w_ref, dy_ref, dt_ref, lo in ((ga, ya_ref, wa_ref, dya_ref, dta_ref, 0), (gb, yb_ref, wb_ref, dyb_ref, dtb_ref, D_MODEL)):
            t = _dot(y_ref[...], w_ref[...])
            dlogit = dh * t * gate * (1.0 - gate)
            dgl_ref[:, lo:lo + D_MODEL] = dlogit.astype(BF16)
            dbg_ref[:, lo:lo + D_MODEL] += jnp.sum(dlogit, axis=0, keepdims=True)
            dt = (dh * gate).astype(BF16)
            dt_ref[...] = dt
            dy_ref[...] = _dot_nt(dt, w_ref[...]).astype(BF16)

    row = lambda n: pl.BlockSpec((tm, n), lambda i: (i, 0))
    const = lambda r, n: pl.BlockSpec((r, n), lambda i: (0, 0))
    sds = lambda n, dt: jax.ShapeDtypeStruct((T, n), dt)
    return pl.pallas_call(
        body, name="mix_bwd", grid=(T // tm,),
        in_specs=[row(D_MODEL), row(SWA_Q_WIDTH), row(SB_WIDTH), row(GATE_WIDTH), const(SWA_Q_WIDTH, D_MODEL), const(SB_WIDTH, D_MODEL),
                  const(D_MODEL, D_MODEL), const(1, GATE_WIDTH)],
        out_specs=[row(SWA_Q_WIDTH), row(SB_WIDTH), row(GATE_WIDTH), row(D_MODEL), row(D_MODEL), const(1, GATE_WIDTH)],
        out_shape=[sds(SWA_Q_WIDTH, BF16), sds(SB_WIDTH, BF16), sds(GATE_WIDTH, BF16), sds(D_MODEL, BF16), sds(D_MODEL, BF16),
                   jax.ShapeDtypeStruct((1, GATE_WIDTH), F32)],
        compiler_params=_params(("arbitrary",)),
    )(du1, ya, yb, gl, wa, wb, wo, b_gate)


CONV_COLS = LANES


CONV_CHUNK = 64
CONV_CHUNK_FWD = 256
HALO = 8


def _taps(ref, r0, rows, lead):
    return [ref[pl.ds(r0 + lead + k, rows), :] for k in ((-2, -1, 0) if lead else (0, 1, 2))]


def _chunks(T, rows, step, init=None):
    def body(c, carry):
        out = step(pl.multiple_of(c * rows, rows), *(() if init is None else (carry,)))
        return carry if init is None else out
    return lax.fori_loop(0, T // rows, body, 0 if init is None else init)


def _conv_chunk(taps, w_ref, b_ref):
    return w_ref[0:1, :] * taps[0] + w_ref[1:2, :] * taps[1] + w_ref[2:3, :] * taps[2] + b_ref[...]


def _fold(x):
    return jnp.sum(x.reshape(x.shape[0] // 8, 8, x.shape[1]), axis=0)


def _conv_specs(T):
    nb = D_FF // CONV_COLS
    pair = pl.BlockSpec((2, T, CONV_COLS), lambda j: (0, 0, j))
    gate = lambda r: pl.BlockSpec((r, CONV_COLS), lambda j: (0, j))
    up = lambda r: pl.BlockSpec((r, CONV_COLS), lambda j: (0, j + nb))
    return nb, pair, gate, up


def _conv_glu_fwd(p3, conv_w, conv_b):
    T = p3.shape[1]
    nb, pair, gate, up = _conv_specs(T)

    R = min(CONV_CHUNK_FWD, T)

    def body(p_ref, wg_ref, wu_ref, bg_ref, bu_ref, s_ref, pg_s, pu_s):
        for half, scr in enumerate((pg_s, pu_s)):
            scr[0:HALO, :] = jnp.zeros((HALO, CONV_COLS), F32)
            scr[HALO:HALO + T, :] = p_ref[half].astype(F32)
        def step(r0):
            ag = _conv_chunk(_taps(pg_s, r0, R, HALO), wg_ref, bg_ref)
            au = _conv_chunk(_taps(pu_s, r0, R, HALO), wu_ref, bu_ref)
            s_ref[pl.ds(r0, R), :] = (ag * jax.nn.sigmoid(ag) * au).astype(BF16)

        _chunks(T, R, step)

    return pl.pallas_call(
        body, name="conv_glu_fwd", grid=(nb,),
        in_specs=[pair, gate(3), up(3), gate(1), up(1)],
        out_specs=pl.BlockSpec((T, CONV_COLS), lambda j: (0, j)),
        out_shape=jax.ShapeDtypeStruct((T, D_FF), BF16),
        scratch_shapes=[pltpu.VMEM((T + HALO, CONV_COLS), F32)] * 2,
        compiler_params=_params(("parallel",)),
    )(p3, conv_w, conv_w, conv_b, conv_b)


def _conv_glu_bwd(p3, ds, conv_w, conv_b):
    T = p3.shape[1]
    nb, pair, gate, up = _conv_specs(T)

    R = min(CONV_CHUNK, T)

    def body(p_ref, ds_ref, wg_ref, wu_ref, bg_ref, bu_ref, dp_ref, dwg_ref, dwu_ref, dbg_ref, dbu_ref, pg_s, pu_s, dag_s, dau_s):
        for half, scr in enumerate((pg_s, pu_s)):
            scr[0:HALO, :] = jnp.zeros((HALO, CONV_COLS), F32)
            scr[HALO:HALO + T, :] = p_ref[half].astype(F32)
        for scr in (dag_s, dau_s):
            scr[T:T + HALO, :] = jnp.zeros((HALO, CONV_COLS), F32)
        halves = ((pg_s, dag_s, wg_ref, dwg_ref, dbg_ref), (pu_s, dau_s, wu_ref, dwu_ref, dbu_ref))

        def step(r0, sums):
            taps = [_taps(p_s, r0, R, HALO) for p_s, *_ in halves]
            ag = _conv_chunk(taps[0], wg_ref, bg_ref)
            au = _conv_chunk(taps[1], wu_ref, bu_ref)
            sg = jax.nn.sigmoid(ag)
            d = ds_ref[pl.ds(r0, R), :].astype(F32)
            das = (d * au * (sg * (1.0 + ag * (1.0 - sg))), d * ag * sg)
            out = []
            for half, (_, da_s, *_) in enumerate(halves):
                da_s[pl.ds(r0, R), :] = das[half]
                out.append(tuple(sums[half][k] + _fold(das[half] * taps[half][k]) for k in range(3)) + (sums[half][3] + _fold(das[half]),))
            return tuple(out)

        sums = _chunks(T, R, step, ((jnp.zeros((8, CONV_COLS), F32),) * 4,) * 2)
        for half, (_, da_s, w_ref, dw_ref, db_ref) in enumerate(halves):
            for k in range(3):
                dw_ref[k:k + 1, :] = jnp.sum(sums[half][k], axis=0, keepdims=True)
            db_ref[...] = jnp.sum(sums[half][3], axis=0, keepdims=True)

            def transposed(r0, da_s=da_s, w_ref=w_ref, half=half):
                da0, da1, da2 = _taps(da_s, r0, R, 0)
                dp_ref[half, pl.ds(r0, R), :] = (w_ref[2:3, :] * da0 + w_ref[1:2, :] * da1 + w_ref[0:1, :] * da2).astype(BF16)

            _chunks(T, R, transposed)

    col = lambda r: pl.BlockSpec((r, CONV_COLS), lambda j: (0, j))
    return pl.pallas_call(
        body, name="conv_glu_bwd", grid=(nb,),
        in_specs=[pair, col(T), gate(3), up(3), gate(1), up(1)],
        out_specs=[pair, col(3), col(3), col(1), col(1)],
        out_shape=[jax.ShapeDtypeStruct((2, T, D_FF), BF16), jax.ShapeDtypeStruct((3, D_FF), F32), jax.ShapeDtypeStruct((3, D_FF), F32),
                   jax.ShapeDtypeStruct((1, D_FF), F32), jax.ShapeDtypeStruct((1, D_FF), F32)],
        scratch_shapes=[pltpu.VMEM((T + HALO, CONV_COLS), F32)] * 4,
        compiler_params=_params(("parallel",)),
    )(p3, ds, conv_w, conv_w, conv_b, conv_b)


def _ffn_down_loss(s, w_down, u1, ln1_g, ln1_b, ln2_g, ln2_b, target):
    T = u1.shape[0]
    tm = min(512, T)

    def body(s_ref, w_ref, u1_ref, g1_ref, b1_ref, g2_ref, b2_ref, t_ref, du_ref, dub_ref, dg_ref, db_ref, loss_ref):
        @pl.when(pl.program_id(0) == 0)
        def _():
            dg_ref[...] = jnp.zeros_like(dg_ref)
            db_ref[...] = jnp.zeros_like(db_ref)
            loss_ref[...] = jnp.zeros_like(loss_ref)

        xh1, _ = _ln_stats(u1_ref[...])
        x1 = xh1 * g1_ref[...] + b1_ref[...]
        u2 = ALPHA * x1 + _dot(s_ref[...], w_ref[...])
        xh2, rstd2 = _ln_stats(u2)
        err = xh2 * g2_ref[...] + b2_ref[...] - t_ref[...]
        per_token = jnp.mean(err * err, axis=-1, keepdims=True)
        loss_ref[...] += 0.5 * jnp.sum(per_token, axis=0, keepdims=True)
        dy = err * (1.0 / D_MODEL)
        dg_ref[...] += jnp.sum(dy * xh2, axis=0, keepdims=True)
        db_ref[...] += jnp.sum(dy, axis=0, keepdims=True)
        du2 = _ln_bwd(dy, xh2, rstd2, g2_ref[...])
        du_ref[...] = du2
        dub_ref[...] = du2.astype(BF16)

    row = lambda n: pl.BlockSpec((tm, n), lambda i: (i, 0))
    const = lambda r, n: pl.BlockSpec((r, n), lambda i: (0, 0))
    vec = const(1, D_MODEL)
    return pl.pallas_call(
        body, name="ffn_down_loss", grid=(T // tm,),
        in_specs=[row(D_FF), const(D_FF, D_MODEL), row(D_MODEL), vec, vec, vec, vec, row(D_MODEL)],
        out_specs=[row(D_MODEL), row(D_MODEL), vec, vec, const(1, LANES)],
        out_shape=[jax.ShapeDtypeStruct((T, D_MODEL), F32), jax.ShapeDtypeStruct((T, D_MODEL), BF16), jax.ShapeDtypeStruct((1, D_MODEL), F32),
                   jax.ShapeDtypeStruct((1, D_MODEL), F32), jax.ShapeDtypeStruct((1, LANES), F32)],
        compiler_params=_params(("arbitrary",)),
    )(s, w_down, u1, ln1_g, ln1_b, ln2_g, ln2_b, target)


def _ffn_up_bwd_ln1(dp3, w_up, du2, u1, ln1_g):
    T = u1.shape[0]
    tm = min(256, T)

    def body(dp_ref, w_ref, du2_ref, u1_ref, g_ref, du_ref, dub_ref, dg_ref, db_ref):
        @pl.when(pl.program_id(0) == 0)
        def _():
            dg_ref[...] = jnp.zeros_like(dg_ref)
            db_ref[...] = jnp.zeros_like(db_ref)

        dx1 = _dot(dp_ref[0], w_ref[:D_FF, :]) + _dot(dp_ref[1], w_ref[D_FF:, :]) + ALPHA * du2_ref[...]
        xh, rstd = _ln_stats(u1_ref[...])
        dg_ref[...] += jnp.sum(dx1 * xh, axis=0, keepdims=True)
        db_ref[...] += jnp.sum(dx1, axis=0, keepdims=True)
        du1 = _ln_bwd(dx1, xh, rstd, g_ref[...])
        du_ref[...] = du1
        dub_ref[...] = du1.astype(BF16)

    row = lambda n: pl.BlockSpec((tm, n), lambda i: (i, 0))
    const = lambda r, n: pl.BlockSpec((r, n), lambda i: (0, 0))
    vec = const(1, D_MODEL)
    return pl.pallas_call(
        body, name="ffn_up_bwd_ln1", grid=(T // tm,),
        in_specs=[pl.BlockSpec((2, tm, D_FF), lambda i: (0, i, 0)), const(2 * D_FF, D_MODEL), row(D_MODEL), row(D_MODEL), vec],
        out_specs=[row(D_MODEL), row(D_MODEL), vec, vec],
        out_shape=[jax.ShapeDtypeStruct((T, D_MODEL), F32), jax.ShapeDtypeStruct((T, D_MODEL), BF16), jax.ShapeDtypeStruct((1, D_MODEL), F32),
                   jax.ShapeDtypeStruct((1, D_MODEL), F32)],
        compiler_params=_params(("arbitrary",)),
    )(dp3, w_up, du2, u1, ln1_g)


def _local_step(x, positions, w_in, b_gate, sinks, ln1_g, ln1_b, conv_b, ln2_g, ln2_b, target, later_weights,
                early_exchange=None, tail_exchange=None):
    T = x.shape[0]
    inv_freq = 1.0 / (ROPE_THETA ** (jnp.arange(0, HEAD_DIM, 2, dtype=F32) / HEAD_DIM))
    cos, sin = _rope_tables(positions.reshape(T, 1), jnp.tile(inv_freq, LANES // (HEAD_DIM // 2)).reshape(1, LANES))

    xb, qa, ka, va, qb, kb, vb, gl = _in_proj(x, w_in)
    ya = _swa_fwd(qa, ka, va, cos, sin, sinks)
    if isinstance(later_weights, tuple):
        exchange, finish = later_weights
        (yb, probs, betas), arrived = _sb_fwd(qb, kb, vb, exchange)
        later_weights = finish(arrived)
    else:
        (yb, probs, betas), _ = _sb_fwd(qb, kb, vb)
    wa, wb, wo, w_up, conv_w, w_down = later_weights
    h, u1, x1 = _mix_fwd(ya, yb, gl, x, wa, wb, wo, b_gate, ln1_g, ln1_b)

    ff_tn = D_FF // 2
    nff = D_FF // ff_tn
    tm = min(1024, T)
    p3 = _matmul(x1, w_up, kind="nt", name="ffn_up", grid=(T // tm, 2 * nff),
                 a_spec=pl.BlockSpec((tm, D_MODEL), lambda i, j: (i, 0)), b_spec=pl.BlockSpec((ff_tn, D_MODEL), lambda i, j: (j, 0)),
                 out_spec=pl.BlockSpec((None, tm, ff_tn), lambda i, j: (j // nff, i, j % nff)),
                 out_shape=jax.ShapeDtypeStruct((2, T, D_FF), ACT_DTYPE))
    s = _conv_glu_fwd(p3, conv_w, conv_b)
    du2, du2b, dln2_g, dln2_b, loss = _ffn_down_loss(s, w_down, u1, ln1_g, ln1_b, ln2_g, ln2_b, target)

    ds = _matmul(du2b, w_down, kind="nt", name="ffn_down_bwd", grid=(T // tm, nff),
                 a_spec=pl.BlockSpec((tm, D_MODEL), lambda i, j: (i, 0)), b_spec=pl.BlockSpec((ff_tn, D_MODEL), lambda i, j: (j, 0)),
                 out_spec=pl.BlockSpec((tm, ff_tn), lambda i, j: (i, j)), out_shape=jax.ShapeDtypeStruct((T, D_FF), ACT_DTYPE))
    dp3, dcw_g, dcw_u, dcb_g, dcb_u = _conv_glu_bwd(p3, ds, conv_w, conv_b)
    tk = 256
    dw_down = _matmul(s, du2b, kind="tn", name="dw_down", grid=(D_FF // tk,),
                      a_spec=pl.BlockSpec((T, tk), lambda i: (0, i)), b_spec=pl.BlockSpec((T, D_MODEL), lambda i: (0, 0)),
                      out_spec=pl.BlockSpec((tk, D_MODEL), lambda i: (i, 0)), out_shape=jax.ShapeDtypeStruct((D_FF, D_MODEL), BF16))
    dw_up = _matmul(dp3, x1, kind="tn", name="dw_up", grid=(2 * nff,),
                    a_spec=pl.BlockSpec((None, T, ff_tn), lambda j: (j // nff, 0, j % nff)), b_spec=pl.BlockSpec((T, D_MODEL), lambda j: (0, 0)),
                    out_spec=pl.BlockSpec((ff_tn, D_MODEL), lambda j: (j, 0)), out_shape=jax.ShapeDtypeStruct((2 * D_FF, D_MODEL), BF16))
    du1, du1b, dln1_g, dln1_b = _ffn_up_bwd_ln1(dp3, w_up, du2, u1, ln1_g)
    dya, dyb, dgl, dta, dtb, db_gate = _mix_bwd(du1, ya, yb, gl, wa, wb, wo, b_gate)

    def dw_tn(a, g, name):
        rows, cols = a.shape[1], g.shape[1]
        tn = min(512, cols)
        return _matmul(a, g, kind="tn", name=name, grid=(rows // 512, cols // tn),
                       a_spec=pl.BlockSpec((T, 512), lambda i, j: (0, i)), b_spec=pl.BlockSpec((T, tn), lambda i, j: (0, j)),
                       out_spec=pl.BlockSpec((512, tn), lambda i, j: (i, j)), out_shape=jax.ShapeDtypeStruct((rows, cols), BF16))

    dwa = dw_tn(ya, dta, "dw_branch_a")
    dwb = dw_tn(yb, dtb, "dw_branch_b")
    dwo = dw_tn(h, du1b, "dw_out")

    grads = dict(
        b_gate=db_gate, w_branch_a=dwa, w_branch_b=dwb, w_out=dwo, ln1_g=dln1_g, ln1_b=dln1_b,
        w_up=dw_up, conv_w=jnp.concatenate([dcw_g, dcw_u], axis=1), conv_b=(dcb_g, dcb_u), w_down=dw_down, ln2_g=dln2_g, ln2_b=dln2_b)
    (dqb, dkb, dvb), early_out = _sb_bwd(qb, kb, vb, probs, betas, dyb, early_exchange(grads) if early_exchange else None)
    dqa, dka, dva, grads["sinks"] = _swa_bwd(qa, ka, va, cos, sin, sinks, dya)
    dproj = (dqa, dka, dva, dqb, dkb, dvb, dgl)
    grads["w_in"] = _dw_in(xb, dproj)
    grad_x, tail_out = _grad_x(dproj, w_in, du1, tail_exchange(grads, loss) if tail_exchange else None)
    return loss, grad_x, grads, early_out, tail_out


def _dw_in(xb, dproj):
    T = xb.shape[0]
    tn = 2 * LANES
    groups, start, k = [], 0, 0
    while k < len(IN_WIDTHS):
        if IN_WIDTHS[k] >= tn:
            groups.append((start, IN_WIDTHS[k] // tn, [(k, 0, tn)]))
            k += 1
        else:
            members, off = [], 0
            while off < tn:
                members.append((k, off, IN_WIDTHS[k]))
                off += IN_WIDTHS[k]
                k += 1
            groups.append((start, 1, members))
        start += groups[-1][1]

    def body(x_ref, *refs):
        pieces, o_ref = refs[:-1], refs[-1]
        j = pl.program_id(0)
        for first, steps, members in groups:
            @pl.when((j >= first) & (j < first + steps))
            def _(members=members):
                for k, off, width in members:
                    o_ref[off:off + width, :] = _dot_tn(pieces[k][...], x_ref[...]).astype(o_ref.dtype)

    specs = [None] * len(IN_WIDTHS)
    for first, steps, members in groups:
        for k, _, width in members:
            specs[k] = pl.BlockSpec((T, width), lambda j, first=first, steps=steps: (0, jnp.clip(j - first, 0, steps - 1)))
    return pl.pallas_call(
        body, name="dw_in", grid=(IN_TOTAL // tn,),
        in_specs=[pl.BlockSpec((T, D_MODEL), lambda j: (0, 0))] + specs, out_specs=pl.BlockSpec((tn, D_MODEL), lambda j: (j, 0)),
        out_shape=jax.ShapeDtypeStruct((IN_TOTAL, D_MODEL), BF16), compiler_params=_params(("arbitrary",)),
    )(xb, *dproj)


def _grad_x(dproj, w_in, du1, exchange=None):
    T = du1.shape[0]
    tm = min(512, T)
    offs = np.cumsum((0,) + IN_WIDTHS)

    def body(*refs):
        pieces, (w_ref, du_ref, o_ref) = refs[:len(IN_WIDTHS)], refs[len(IN_WIDTHS):]
        acc = ALPHA * du_ref[...]
        for p_ref, a, b in zip(pieces, offs[:-1], offs[1:]):
            acc = acc + _dot(p_ref[...].astype(BF16), w_ref[a:b, :])
        o_ref[...] = acc

    row = lambda n: pl.BlockSpec((tm, n), lambda i: (i, 0))
    (grad_x,), arrived = _hosted_call(
        body, "grad_x", (T // tm,), exchange,
        in_specs=[row(n) for n in IN_WIDTHS] + [pl.BlockSpec((IN_TOTAL, D_MODEL), lambda i: (0, 0)), row(D_MODEL)],
        out_specs=[row(D_MODEL)], out_shape=[jax.ShapeDtypeStruct((T, D_MODEL), F32)], semantics=("parallel",),
        args=(*dproj, w_in, du1))
    return grad_x, arrived


ANY = pl.BlockSpec(memory_space=pl.ANY)


def _all_gather(slabs, name):
    n = len(slabs)

    def body(*refs):
        ins, outs = refs[:n], refs[n:2 * n]
        send_sems, recv_sems, local_sems = refs[2 * n:]
        x, y, c = lax.axis_index("x"), lax.axis_index("y"), lax.axis_index("c")
        me, sibling = (x, y, c), (x, y, 1 - c)
        chips = [(1 - x, y), (x, 1 - y), (1 - x, 1 - y)]

        def slot(pos):
            return 4 * pos[0] + 2 * pos[1] + pos[2]

        def copy(a, k, block, to, from_input=False):
            return pltpu.make_async_remote_copy(
                src_ref=ins[a] if from_input else outs[a].at[slot(block)], dst_ref=outs[a].at[slot(block)],
                send_sem=send_sems.at[a, k], recv_sem=recv_sems.at[a, k], device_id=to, device_id_type=MESH)

        mine = [pltpu.make_async_copy(ins[a], outs[a].at[slot(me)], local_sems.at[a]) for a in range(n)]
        for cp in mine:
            cp.start()
        first = []
        for a in range(n):
            first.append(copy(a, 0, me, sibling, from_input=True))
            first += [copy(a, 1 + j, me, (*chip, c), from_input=True) for j, chip in enumerate(chips)]
        for cp in first:
            cp.start()
        passed = []
        for j, chip in enumerate(chips):
            for a in range(n):
                copy(a, 1 + j, (*chip, c), me).wait_recv()
                fwd = copy(a, 4 + j, (*chip, c), sibling)
                fwd.start()
                passed.append(fwd)
        for a in range(n):
            copy(a, 0, sibling, me).wait_recv()
            for j, chip in enumerate(chips):
                copy(a, 4 + j, (*chip, 1 - c), me).wait_recv()
        for cp in first + passed:
            cp.wait_send()
        for cp in mine:
            cp.wait()

    return pl.pallas_call(
        body, name=name,
        in_specs=[ANY] * n, out_specs=[ANY] * n,
        out_shape=[jax.ShapeDtypeStruct((N_DEV,) + s.shape, s.dtype) for s in slabs],
        scratch_shapes=[pltpu.SemaphoreType.DMA((n, 7)), pltpu.SemaphoreType.DMA((n, 7)), pltpu.SemaphoreType.DMA((n,))],
    )(*slabs)


def _row_tile(rows):
    for cand in range(256, 7, -8):
        if rows % cand == 0:
            return cand
    return rows


def _window(w):
    wp = max(-(-((w * r) % LANES + w) // LANES) for r in range(N_DEV)) * LANES
    assert all((w * r) // LANES * LANES + wp <= N_DEV * w for r in range(N_DEV))
    return wp


def _join_cols(slabs, name):
    _, R, w = slabs.shape
    tr = _row_tile(R)
    wp = _window(w)

    def body(g_ref, o_ref, pad_ref):
        if w % LANES == 0:
            for r in range(N_DEV):
                o_ref[:, w * r:w * (r + 1)] = g_ref[r]
            return
        o_ref[...] = jnp.zeros_like(o_ref)
        pad_ref[...] = jnp.zeros_like(pad_ref)
        for r in range(N_DEV):
            q, s = divmod(w * r, LANES)
            pad_ref[:, :w] = g_ref[r]
            y = pad_ref[...]
            if s:
                y = pltpu.roll(y, s, axis=1)
            o_ref[:, LANES * q:LANES * q + wp] += y

    return pl.pallas_call(
        body, name=name, grid=(R // tr,),
        in_specs=[pl.BlockSpec((N_DEV, tr, w), lambda i: (0, i, 0))], out_specs=pl.BlockSpec((tr, N_DEV * w), lambda i: (i, 0)),
        out_shape=jax.ShapeDtypeStruct((R, N_DEV * w), slabs.dtype), scratch_shapes=[pltpu.VMEM((tr, wp), slabs.dtype)],
        compiler_params=_params(("parallel",)),
    )(slabs)


def _split_cols(pieces, name):
    R = pieces[0].shape[0]
    widths = [p.shape[1] for p in pieces]
    total = sum(widths)
    w = total // N_DEV
    tr = _row_tile(R)
    wp = _window(w)
    offs = np.cumsum([0] + widths)
    dtype = pieces[0].dtype

    def body(*refs):
        ins, (o_ref, full_ref) = refs[:len(pieces)], refs[len(pieces):]
        for p_ref, a, b in zip(ins, offs[:-1], offs[1:]):
            full_ref[:, a:b] = p_ref[...].astype(dtype)
        for r in range(N_DEV):
            q, s = divmod(w * r, LANES)
            y = full_ref[:, LANES * q:LANES * q + wp]
            if s:
                y = pltpu.roll(y, wp - s, axis=1)
            o_ref[r] = y[:, :w]

    return pl.pallas_call(
        body, name=name, grid=(R // tr,),
        in_specs=[pl.BlockSpec((tr, n), lambda i: (i, 0)) for n in widths], out_specs=pl.BlockSpec((N_DEV, tr, w), lambda i: (0, i, 0)),
        out_shape=jax.ShapeDtypeStruct((N_DEV, R, w), dtype), scratch_shapes=[pltpu.VMEM((tr, total), dtype)],
        compiler_params=_params(("parallel",)),
    )(*pieces)


def _adamw(g, w, m, v):
    m_new = ADAM_B1 * m + (1.0 - ADAM_B1) * g
    v_new = ADAM_B2 * v + (1.0 - ADAM_B2) * jnp.square(g)
    m_hat = m_new / (1.0 - ADAM_B1 ** ADAM_STEP)
    v_hat = v_new / (1.0 - ADAM_B2 ** ADAM_STEP)
    return -ADAM_LR * (m_hat / (jnp.sqrt(v_hat) + ADAM_EPS) + ADAM_WD * w), m_new, v_new


def _sum_parts(p_ref):
    g = p_ref[0].astype(F32)
    for d in range(1, N_DEV):
        g = g + p_ref[d].astype(F32)
    return g


def _reduce_adamw(parts, w, m, v, name):
    R, C = w.shape
    tr = _row_tile(R)

    def body(p_ref, w_ref, m_ref, v_ref, g_ref, d_ref, mo_ref, vo_ref):
        g = _sum_parts(p_ref)
        g_ref[...] = g
        d_ref[...], mo_ref[...], vo_ref[...] = _adamw(g, w_ref[...], m_ref[...], v_ref[...])

    row = pl.BlockSpec((tr, C), lambda i: (i, 0))
    return pl.pallas_call(
        body, name=name, grid=(R // tr,),
        in_specs=[pl.BlockSpec((N_DEV, tr, C), lambda i: (0, i, 0)), row, row, row],
        out_specs=[row] * 4, out_shape=[jax.ShapeDtypeStruct((R, C), F32)] * 4,
        compiler_params=_params(("parallel",)),
    )(parts, w, m, v)


def _reduce_adamw_small(parts, ws, ms, vs):
    sizes = [a.shape[1] for a in ws]
    k = len(sizes)
    offs = np.cumsum([0] + [-(-n // LANES) * LANES for n in sizes])

    def body(*refs):
        p_ref, w_refs, m_refs, v_refs = refs[0], refs[1:1 + k], refs[1 + k:1 + 2 * k], refs[1 + 2 * k:1 + 3 * k]
        outs, loss_ref = refs[1 + 3 * k:-1], refs[-1]
        g_all = _sum_parts(p_ref)
        for j, n in enumerate(sizes):
            g = g_all[:, offs[j]:offs[j] + LANES * (-(-n // LANES))][:, :n]
            outs[4 * j][...] = g
            outs[4 * j + 1][...], outs[4 * j + 2][...], outs[4 * j + 3][...] = _adamw(g, w_refs[j][...], m_refs[j][...], v_refs[j][...])
        loss_ref[...] = g_all[:, offs[k]:offs[k] + LANES]

    vm = pl.BlockSpec(memory_space=pltpu.VMEM)
    out_shape = [jax.ShapeDtypeStruct((1, n), F32) for n in sizes for _ in range(4)] + [jax.ShapeDtypeStruct((1, LANES), F32)]
    res = pl.pallas_call(
        body, name="reduce_adamw_replicated", in_specs=[vm] * (1 + 3 * k), out_specs=[vm] * len(out_shape), out_shape=out_shape,
        compiler_params=_params(),
    )(parts, *ws, *ms, *vs)
    return [res[4 * j:4 * j + 4] for j in range(k)], res[-1]


TRANSPOSED = ("w_in", "w_up")
COL_SHARDED = ("w_branch_a", "w_branch_b", "conv_w")
ROW_SHARDED = TRANSPOSED + ("w_out", "w_down")
SMALL = ("b_gate", "sinks", "ln1_g", "ln1_b", "conv_b", "ln2_g", "ln2_b")
ORDER = ("w_in", "b_gate", "sinks", "w_branch_a", "w_branch_b", "w_out", "ln1_g", "ln1_b", "w_up", "conv_w", "conv_b", "w_down", "ln2_g", "ln2_b")


def _pad_lanes(a):
    pad = (-a.shape[-1]) % LANES
    return a if pad == 0 else jnp.pad(a, ((0, 0), (0, pad)))


def kernel(x, positions, w_in, b_gate, sinks, w_branch_a, w_branch_b, w_out, ln1_g, ln1_b, w_up, conv_w, conv_b, w_down, ln2_g, ln2_b, loss_target, m_w_in, m_b_gate, m_sinks, m_w_branch_a, m_w_branch_b, m_w_out, m_ln1_g, m_ln1_b, m_w_up, m_conv_w, m_conv_b, m_w_down, m_ln2_g, m_ln2_b, v_w_in, v_b_gate, v_sinks, v_w_branch_a, v_w_branch_b, v_w_out, v_ln1_g, v_ln1_b, v_w_up, v_conv_w, v_conv_b, v_w_down, v_ln2_g, v_ln2_b):
    args = dict(locals())
    sharded = COL_SHARDED + ROW_SHARDED

    def shard(name, a):
        return a if name not in sharded else a[0].T if name in TRANSPOSED else a[0]

    w = {n: shard(n, args[n]) for n in ORDER}
    m = {n: shard(n, args["m_" + n]) for n in ORDER}
    v = {n: shard(n, args["v_" + n]) for n in ORDER}

    travel = {n: (w[n] if n == "conv_w" else w[n].astype(BF16)) for n in sharded}
    (g_in,) = _all_gather([travel["w_in"]], "all_gather_w_in")
    w_in_full = g_in.reshape(-1, g_in.shape[-1])
    later = ("w_branch_a", "w_branch_b", "w_out", "w_up", "conv_w", "w_down")

    def join(name, slabs):
        return _join_cols(slabs, "join_" + name) if name in COL_SHARDED else slabs.reshape(-1, slabs.shape[-1])

    def split(name, grad):
        if name in COL_SHARDED:
            return _split_cols(grad if isinstance(grad, tuple) else (grad,), "split_d" + name)
        return grad.reshape((N_DEV, -1, grad.shape[-1]))

    def early_exchange(grads):
        return _Exchange([split(n, grads[n]) for n in later], ["scatter"] * len(later))

    def tail_exchange(grads, loss):
        small_pack = jnp.concatenate(
            [_pad_lanes(p) for n in SMALL for p in (grads[n] if isinstance(grads[n], tuple) else (grads[n],))] + [loss], axis=1)
        return _Exchange([split("w_in", grads["w_in"]), small_pack], ["scatter", "gather"])

    gather_later = _Exchange([travel[n] for n in later], ["gather"] * len(later))
    _, grad_x, _, early_out, (recv_w_in, small_parts) = _local_step(
        x[0], positions[0], w_in_full, w["b_gate"], w["sinks"][0], w["ln1_g"], w["ln1_b"], w["conv_b"], w["ln2_g"], w["ln2_b"], loss_target[0],
        (gather_later, lambda arrived: [join(n, a) for n, a in zip(later, arrived)]), early_exchange, tail_exchange)
    recv = dict(zip(later, early_out), w_in=recv_w_in)

    res = {n: _reduce_adamw(recv[n], w[n], m[n], v[n], "reduce_adamw_" + n) for n in sharded}
    small_res, loss_sum = _reduce_adamw_small(small_parts, [w[n] for n in SMALL], [m[n] for n in SMALL], [v[n] for n in SMALL])
    res.update(zip(SMALL, small_res))
    out = [loss_sum[0, 0], grad_x[None]]
    for k in range(4):
        out += [res[n][k].T[None] if n in TRANSPOSED else res[n][k][None] if n in sharded else res[n][k] for n in ORDER]
    return tuple(out)
```

```python
import functools

import jax
import jax.numpy as jnp
import numpy as np
from jax import lax
from jax.experimental import pallas as pl
from jax.experimental.pallas import tpu as pltpu

D_MODEL = 1024
HEAD_DIM = 64
SWA_Q_HEADS = 8
SWA_KV_HEADS = 2
SB_HEADS = 8
WINDOW = 128
ROPE_THETA = 10000.0
D_FF = 2816
LN_EPS = 1e-5
DEPTH = 1
ALPHA = (2.0 * DEPTH) ** 0.25
SWA_Q_WIDTH = SWA_Q_HEADS * HEAD_DIM
SWA_KV_WIDTH = SWA_KV_HEADS * HEAD_DIM
SB_WIDTH = SB_HEADS * HEAD_DIM
GATE_WIDTH = 2 * D_MODEL
IN_WIDTHS = (SWA_Q_WIDTH, SWA_KV_WIDTH, SWA_KV_WIDTH, SB_WIDTH, SB_WIDTH, SB_WIDTH, GATE_WIDTH)
IN_TOTAL = sum(IN_WIDTHS)
ATTN_SCALE = HEAD_DIM ** -0.5

ADAM_LR = 0.001
ADAM_B1 = 0.9
ADAM_B2 = 0.999
ADAM_EPS = 1e-08
ADAM_WD = 0.01
ADAM_STEP = 10

N_DEV = 8
LANES = 128
SB_BLOCK = 256
VMEM_LIMIT = 56 * 1024 * 1024

F32 = jnp.float32
BF16 = jnp.bfloat16
ACT_DTYPE = BF16
MESH = pl.DeviceIdType.MESH


def _params(sem=None):
    return pltpu.CompilerParams(dimension_semantics=sem, vmem_limit_bytes=VMEM_LIMIT)


def _dot(a, b):
    return jnp.dot(a, b, preferred_element_type=F32)


def _dot_nt(a, b):
    return lax.dot_general(a, b, (((1,), (1,)), ((), ())), preferred_element_type=F32)


def _dot_tn(a, b):
    return lax.dot_general(a, b, (((0,), (0,)), ((), ())), preferred_element_type=F32)


def _split_bf16(v):
    hi = v.astype(BF16)
    lo = (v - hi.astype(F32)).astype(BF16)
    return hi, lo


def _matmul(a, b, *, kind, out_shape, grid, a_spec, b_spec, out_spec, name, add=None, add_spec=None, add_scale=1.0):
    dot = {"nn": _dot, "nt": _dot_nt, "tn": _dot_tn}[kind]

    def body(*refs):
        if add is None:
            a_ref, b_ref, o_ref = refs
        else:
            a_ref, b_ref, add_ref, o_ref = refs
        r = dot(a_ref[...].astype(BF16), b_ref[...].astype(BF16))
        if add is not None:
            r = r + add_scale * add_ref[...]
        o_ref[...] = r.astype(o_ref.dtype)

    ins = [a, b] + ([] if add is None else [add])
    specs = [a_spec, b_spec] + ([] if add is None else [add_spec])
    return pl.pallas_call(
        body, name=name, grid=grid, in_specs=specs, out_specs=out_spec, out_shape=out_shape,
        compiler_params=_params(("parallel",) * len(grid)),
    )(*ins)


def _rope_tables(pos_col, inv_freq_lanes):
    T = pos_col.shape[0]
    tm = min(512, T)

    def body(pos_ref, f_ref, cos_ref, sin_ref):
        ang = pos_ref[...].astype(F32) * f_ref[...]
        cos_ref[...] = jnp.cos(ang)
        sin_ref[...] = jnp.sin(ang)

    return pl.pallas_call(
        body, name="rope_tables", grid=(T // tm,),
        in_specs=[pl.BlockSpec((tm, 1), lambda i: (i, 0)), pl.BlockSpec((1, LANES), lambda i: (0, 0))],
        out_specs=[pl.BlockSpec((tm, LANES), lambda i: (i, 0))] * 2,
        out_shape=[jax.ShapeDtypeStruct((T, LANES), F32)] * 2,
        compiler_params=_params(("parallel",)),
    )(pos_col, inv_freq_lanes)


def _lane_iota(shape):
    return lax.broadcasted_iota(jnp.int32, shape, len(shape) - 1)


def _rot_half(t):
    first = (_lane_iota(t.shape) % HEAD_DIM) < (HEAD_DIM // 2)
    return jnp.where(first, -pltpu.roll(t, LANES - HEAD_DIM // 2, axis=1), pltpu.roll(t, HEAD_DIM // 2, axis=1))


def _rope(t, cos, sin):
    return t * cos + _rot_half(t) * sin


def _rope_transpose(d, cos, sin):
    return d * cos - _rot_half(d * sin)


_IN_DTYPES = (F32, F32, BF16, BF16, BF16, BF16, F32)


def _in_proj(x, w_in_t):
    T = x.shape[0]
    tm = min(512, T)
    offs = np.cumsum((0,) + IN_WIDTHS)

    def body(x_ref, w_ref, xb_ref, *outs):
        xb = x_ref[...].astype(BF16)
        xb_ref[...] = xb
        for o_ref, a, b in zip(outs, offs[:-1], offs[1:]):
            o_ref[...] = _dot_nt(xb, w_ref[a:b, :]).astype(o_ref.dtype)

    row = lambda n: pl.BlockSpec((tm, n), lambda i: (i, 0))
    return pl.pallas_call(
        body, name="in_proj", grid=(T // tm,),
        in_specs=[row(D_MODEL), pl.BlockSpec((IN_TOTAL, D_MODEL), lambda i: (0, 0))],
        out_specs=[row(D_MODEL)] + [row(n) for n in IN_WIDTHS],
        out_shape=[jax.ShapeDtypeStruct((T, D_MODEL), BF16)] + [jax.ShapeDtypeStruct((T, n), dt) for n, dt in zip(IN_WIDTHS, _IN_DTYPES)],
        compiler_params=_params(("parallel",)),
    )(x, w_in_t)


def _swa_specs(T):
    blk = WINDOW
    cur = lambda n: pl.BlockSpec((blk, n), lambda i: (i, 0))
    prev = lambda n: pl.BlockSpec((blk, n), lambda i: (jnp.maximum(i - 1, 0), 0))
    return blk, cur, prev


SWA_GROUP = SWA_Q_HEADS // SWA_KV_HEADS


def _swa_stack(pairs):
    lane = _lane_iota(pairs[0].shape)
    zero = jnp.zeros((), pairs[0].dtype)
    rows = []
    for h in range(SWA_Q_HEADS):
        hh, g = h % 2, h // SWA_GROUP
        x = jnp.where((lane >= hh * HEAD_DIM) & (lane < (hh + 1) * HEAD_DIM), pairs[h // 2], zero)
        rows.append(x if hh == g else pltpu.roll(x, HEAD_DIM, axis=1))
    return jnp.concatenate(rows, axis=0)


def _swa_unstack(stacked, blk):
    low = _lane_iota((blk, LANES)) < HEAD_DIM
    pairs = []
    for pp in range(SWA_Q_HEADS // 2):
        halves = []
        for hh in range(2):
            h = 2 * pp + hh
            x = stacked[h * blk:(h + 1) * blk]
            halves.append(x if hh == h // SWA_GROUP else pltpu.roll(x, HEAD_DIM, axis=1))
        pairs.append(jnp.where(low, halves[0], halves[1]))
    return pairs


def _swa_probs(i, q_stack, kwin, sink_ref, blk):
    r = lax.broadcasted_iota(jnp.int32, (blk, 2 * blk), 0)
    c = lax.broadcasted_iota(jnp.int32, (blk, 2 * blk), 1)
    rel = blk + r - c
    valid = (rel >= 0) & (rel < WINDOW) & ((c >= blk) | (i > 0))
    bias = jnp.concatenate([jnp.where(valid, 0.0, -1e30)] * SWA_Q_HEADS, axis=0)
    head = lax.broadcasted_iota(jnp.int32, (SWA_Q_HEADS * blk, 1), 0) // blk
    sink = jnp.zeros((SWA_Q_HEADS * blk, 1), F32)
    for h in range(SWA_Q_HEADS):
        sink = jnp.where(head == h, sink_ref[h], sink)
    s = _dot_nt(q_stack, kwin) * ATTN_SCALE + bias
    m = jnp.maximum(jnp.max(s, axis=1, keepdims=True), sink)
    p = jnp.exp(s - m)
    es = jnp.exp(sink - m)
    den = jnp.sum(p, axis=1, keepdims=True) + es
    return p / den, es / den


def _swa_inputs(q_ref, kp_ref, kc_ref, vp_ref, vc_ref, cp_ref, cc_ref, sp_ref, sc_ref):
    cc, sc = cc_ref[...], sc_ref[...]
    kwin = jnp.concatenate([_rope(kp_ref[...], cp_ref[...], sp_ref[...]), _rope(kc_ref[...], cc, sc)], axis=0).astype(BF16)
    vwin = jnp.concatenate([vp_ref[...], vc_ref[...]], axis=0)
    q_stack = _swa_stack([_rope(q_ref[:, pp * LANES:(pp + 1) * LANES], cc, sc) for pp in range(SWA_Q_HEADS // 2)]).astype(BF16)
    return q_stack, kwin, vwin


def _swa_fwd(qa, ka, va, cos, sin, sinks):
    T = qa.shape[0]
    blk, cur, prev = _swa_specs(T)

    def body(sink_ref, q_ref, kp_ref, kc_ref, vp_ref, vc_ref, cp_ref, cc_ref, sp_ref, sc_ref, o_ref):
        q_stack, kwin, vwin = _swa_inputs(q_ref, kp_ref, kc_ref, vp_ref, vc_ref, cp_ref, cc_ref, sp_ref, sc_ref)
        probs, _ = _swa_probs(pl.program_id(0), q_stack, kwin, sink_ref, blk)
        for pp, tile in enumerate(_swa_unstack(_dot(probs.astype(BF16), vwin), blk)):
            o_ref[:, pp * LANES:(pp + 1) * LANES] = tile.astype(o_ref.dtype)

    return pl.pallas_call(
        body, name="swa_fwd", grid=(T // blk,),
        in_specs=[pl.BlockSpec(memory_space=pltpu.SMEM), cur(SWA_Q_WIDTH), prev(LANES), cur(LANES), prev(LANES), cur(LANES),
                  prev(LANES), cur(LANES), prev(LANES), cur(LANES)],
        out_specs=cur(SWA_Q_WIDTH),
        out_shape=jax.ShapeDtypeStruct((T, SWA_Q_WIDTH), BF16),
        compiler_params=_params(("parallel",)),
    )(sinks, qa, ka, ka, va, va, cos, cos, sin, sin)


def _swa_bwd(qa, ka, va, cos, sin, sinks, dya):
    T = qa.shape[0]
    blk, cur, prev = _swa_specs(T)
    full = lambda n: pl.BlockSpec((T, n), lambda i: (0, 0))

    def body(sink_ref, q_ref, kp_ref, kc_ref, vp_ref, vc_ref, cp_ref, cc_ref, sp_ref, sc_ref, do_ref,
             dq_ref, dk_out, dv_out, dsink_ref, dk_ref, dv_ref):
        i = pl.program_id(0)

        @pl.when(i == 0)
        def _():
            dk_ref[...] = jnp.zeros_like(dk_ref)
            dv_ref[...] = jnp.zeros_like(dv_ref)
            dsink_ref[...] = jnp.zeros_like(dsink_ref)

        cp, cc, sp, sc = cp_ref[...], cc_ref[...], sp_ref[...], sc_ref[...]
        q_stack, kwin, vwin = _swa_inputs(q_ref, kp_ref, kc_ref, vp_ref, vc_ref, cp_ref, cc_ref, sp_ref, sc_ref)
        probs, psink = _swa_probs(i, q_stack, kwin, sink_ref, blk)
        do_stack = _swa_stack([do_ref[:, pp * LANES:(pp + 1) * LANES] for pp in range(SWA_Q_HEADS // 2)])
        dp = _dot_nt(do_stack, vwin)
        dsum = jnp.sum(probs * dp, axis=1, keepdims=True)
        ds = (probs * (dp - dsum) * ATTN_SCALE).astype(BF16)
        for pp, tile in enumerate(_swa_unstack(_dot(ds, kwin), blk)):
            dq_ref[:, pp * LANES:(pp + 1) * LANES] = _rope_transpose(tile, cc, sc).astype(dq_ref.dtype)
        dkw = _dot_tn(ds, q_stack)
        dvw = _dot_tn(probs.astype(BF16), do_stack)
        lane1 = _lane_iota((1, LANES))
        sink_share = psink * dsum
        dsink = jnp.zeros((1, LANES), F32)
        for h in range(SWA_Q_HEADS):
            dsink = dsink + jnp.where(lane1 == h, -jnp.sum(sink_share[h * blk:(h + 1) * blk]), 0.0)
        dsink_ref[...] += dsink
        ip = jnp.maximum(i - 1, 0)
        rows_p = pl.ds(pl.multiple_of(ip * blk, blk), blk)
        rows_c = pl.ds(pl.multiple_of(i * blk, blk), blk)
        dk_ref[rows_p, :] += _rope_transpose(dkw[:blk], cp, sp)
        dv_ref[rows_p, :] += dvw[:blk]
        dk_ref[rows_c, :] += _rope_transpose(dkw[blk:], cc, sc)
        dv_ref[rows_c, :] += dvw[blk:]

        @pl.when(i == T // blk - 1)
        def _():
            dk_out[...] = dk_ref[...].astype(BF16)
            dv_out[...] = dv_ref[...].astype(BF16)

    return pl.pallas_call(
        body, name="swa_bwd", grid=(T // blk,),
        in_specs=[pl.BlockSpec(memory_space=pltpu.SMEM), cur(SWA_Q_WIDTH), prev(LANES), cur(LANES), prev(LANES), cur(LANES),
                  prev(LANES), cur(LANES), prev(LANES), cur(LANES), cur(SWA_Q_WIDTH)],
        out_specs=[cur(SWA_Q_WIDTH), full(LANES), full(LANES), pl.BlockSpec((1, LANES), lambda i: (0, 0))],
        out_shape=[jax.ShapeDtypeStruct((T, SWA_Q_WIDTH), BF16), jax.ShapeDtypeStruct((T, LANES), BF16),
                   jax.ShapeDtypeStruct((T, LANES), BF16), jax.ShapeDtypeStruct((1, LANES), F32)],
        scratch_shapes=[pltpu.VMEM((T, LANES), F32)] * 2,
        compiler_params=_params(("arbitrary",)),
    )(sinks, qa, ka, ka, va, va, cos, cos, sin, sin, dya)


class _Exchange:
    FLIPS = [(fx, fy, fc) for fx in (1, 0) for fy in (1, 0) for fc in (0, 1) if (fx, fy, fc) != (0, 0, 0)]

    def __init__(self, arrays, kinds):
        self.arrays, self.kinds, self.n = list(arrays), list(kinds), len(arrays)

    def out_shape(self):
        return [jax.ShapeDtypeStruct(a.shape if k == "scatter" else (N_DEV,) + a.shape, a.dtype) for a, k in zip(self.arrays, self.kinds)]

    def scratch(self):
        return [pltpu.SemaphoreType.DMA((self.n, 7)), pltpu.SemaphoreType.DMA((self.n, 7)), pltpu.SemaphoreType.DMA((self.n,))]

    def bind(self, ins, outs, send_sems, recv_sems, local_sems):
        x, y, c = lax.axis_index("x"), lax.axis_index("y"), lax.axis_index("c")
        me = 4 * x + 2 * y + c
        local, remote = [], []
        for a, kind in enumerate(self.kinds):
            mine = ins[a].at[me] if kind == "scatter" else ins[a]
            local.append(pltpu.make_async_copy(mine, outs[a].at[me], local_sems.at[a]))
            for k, (fx, fy, fc) in enumerate(self.FLIPS):
                peer = (x ^ fx, y ^ fy, c ^ fc)
                peer_slot = 4 * peer[0] + 2 * peer[1] + peer[2]
                src = ins[a].at[peer_slot] if kind == "scatter" else ins[a]
                sems = dict(send_sem=send_sems.at[a, k], recv_sem=recv_sems.at[a, k], device_id=peer, device_id_type=MESH)
                remote.append((pltpu.make_async_remote_copy(src_ref=src, dst_ref=outs[a].at[me], **sems),
                               pltpu.make_async_remote_copy(src_ref=src, dst_ref=outs[a].at[peer_slot], **sems)))

        def start():
            for cp in local:
                cp.start()
            for send, _ in remote:
                send.start()

        def wait():
            for send, arrival in remote:
                arrival.wait_recv()
                send.wait_send()
            for cp in local:
                cp.wait()

        return start, wait


def _hosted_call(body, name, grid, exchange, *, in_specs, out_specs, out_shape, semantics, args, scratch=(), prefetch=()):
    n = 0 if exchange is None else exchange.n
    n_pre, n_in, n_out, n_scratch = len(prefetch), len(in_specs), len(out_specs), len(scratch)

    def hosted(*refs):
        pre, rest = refs[:n_pre], refs[n_pre:]
        ins, rest = rest[:n_in], rest[n_in:]
        ex_ins, rest = rest[:n], rest[n:]
        outs, rest = rest[:n_out], rest[n_out:]
        ex_outs, rest = rest[:n], rest[n:]
        own, sems = rest[:n_scratch], rest[n_scratch:]
        if exchange is None:
            return body(*pre, *ins, *outs, *own)
        start, wait = exchange.bind(ex_ins, ex_outs, *sems)
        ids = [pl.program_id(d) for d in range(len(grid))]
        first = functools.reduce(jnp.logical_and, [i == 0 for i in ids])
        last = functools.reduce(jnp.logical_and, [i == g - 1 for i, g in zip(ids, grid)])
        pl.when(first)(start)
        body(*pre, *ins, *outs, *own)
        pl.when(last)(wait)

    grid_spec = pltpu.PrefetchScalarGridSpec(
        num_scalar_prefetch=n_pre, grid=grid, in_specs=list(in_specs) + [ANY] * n, out_specs=list(out_specs) + [ANY] * n,
        scratch_shapes=list(scratch) + ([] if exchange is None else exchange.scratch()))
    res = pl.pallas_call(
        hosted, name=name, grid_spec=grid_spec, out_shape=list(out_shape) + ([] if exchange is None else exchange.out_shape()),
        compiler_params=_params(semantics if exchange is None else ("arbitrary",) * len(grid)),
    )(*prefetch, *args, *([] if exchange is None else exchange.arrays))
    return res[:n_out], res[n_out:]


SOFTPLUS_LINEAR_FROM = 30.0


def _sb_scores(qm, k, valid):
    z = _dot_nt(qm, k)
    sp = jnp.where(z > SOFTPLUS_LINEAR_FROM, z, jnp.log(1.0 + jnp.exp(z)))
    log_beta = z - sp
    if valid is not None:
        sp = jnp.where(valid, sp, 0.0)
    return sp, log_beta


def _tri2(B, cmp):
    r = lax.broadcasted_iota(jnp.int32, (2 * B, B), 0) % B
    c = lax.broadcasted_iota(jnp.int32, (2 * B, B), 1)
    return cmp(r, c).astype(BF16)


def _tri_sum(v, tri2):
    hi, lo = _split_bf16(v)
    return _dot(jnp.concatenate([hi, lo], axis=1), tri2)


def _head_masks(x):
    low = _lane_iota(x.shape) < HEAD_DIM
    zero = jnp.zeros((), x.dtype)
    return jnp.where(low, x, zero), jnp.where(low, zero, x)


def _strictly_below(B):
    r = lax.broadcasted_iota(jnp.int32, (B, B), 0)
    c = lax.broadcasted_iota(jnp.int32, (B, B), 1)
    return c < r


def _sb_grid(T, descending):
    B = min(SB_BLOCK, T)
    n = T // B
    pairs = [(i, j) for i in range(n) for j in (range(i, -1, -1) if descending else range(i + 1))]
    return B, jnp.asarray([p[0] for p in pairs], jnp.int32), jnp.asarray([p[1] for p in pairs], jnp.int32)


N_PAIRS = SB_HEADS // 2
PAIR_COLS = [slice(p * LANES, (p + 1) * LANES) for p in range(N_PAIRS)]


def _sb_fwd(qb, kb, vb, exchange=None):
    T = qb.shape[0]
    B, i_tab, j_tab = _sb_grid(T, descending=True)
    n = T // B

    def body(i_ref, j_ref, q_ref, k_ref, v_ref, o_ref, a_ref, b_ref, acc_ref, c_ref, tri_ref):
        s = pl.program_id(0)
        i, j = i_ref[s], j_ref[s]

        @pl.when(s == 0)
        def _():
            tri_ref[...] = _tri2(B, lambda r, c: r > c)

        @pl.when(j == i)
        def _():
            acc_ref[...] = jnp.zeros_like(acc_ref)
            c_ref[...] = jnp.zeros_like(c_ref)

        def block(valid):
            for p, cols in enumerate(PAIR_COLS):
                qms = _head_masks(q_ref[:, cols] * ATTN_SCALE)
                k = k_ref[:, cols]
                probs = []
                for hh in range(2):
                    h = 2 * p + hh
                    sp, lb = _sb_scores(qms[hh], k, valid)
                    c = c_ref[h]
                    a = jnp.exp(lb - (c + _tri_sum(sp, tri_ref[...])))
                    beta = jnp.exp(lb)
                    if valid is not None:
                        a = jnp.where(valid, a, 0.0)
                        beta = jnp.where(valid, beta, 0.0)
                    probs.append(a.astype(BF16))
                    a_ref[h] = probs[-1]
                    b_ref[h] = beta.astype(BF16)
                    c_ref[h] = c + jnp.sum(sp, axis=1, keepdims=True)
                acc_ref[:, cols] += _dot(jnp.concatenate(probs, axis=1), jnp.concatenate(_head_masks(v_ref[:, cols]), axis=0))

        pl.when(j == i)(lambda: block(_strictly_below(B)))
        pl.when(j != i)(lambda: block(None))

        @pl.when(j == 0)
        def _():
            o_ref[...] = acc_ref[...].astype(o_ref.dtype)

    q_spec = pl.BlockSpec((B, SB_WIDTH), lambda s, i_ref, j_ref: (i_ref[s], 0))
    k_spec = pl.BlockSpec((B, SB_WIDTH), lambda s, i_ref, j_ref: (j_ref[s], 0))
    tile = pl.BlockSpec((None, None, SB_HEADS, B, B), lambda s, i_ref, j_ref: (i_ref[s], j_ref[s], 0, 0, 0))
    saved = jax.ShapeDtypeStruct((n, n, SB_HEADS, B, B), BF16)
    return _hosted_call(
        body, "sb_fwd", (int(i_tab.shape[0]),), exchange, prefetch=(i_tab, j_tab),
        in_specs=[q_spec, k_spec, k_spec], out_specs=[q_spec, tile, tile],
        out_shape=[jax.ShapeDtypeStruct((T, SB_WIDTH), BF16), saved, saved],
        scratch=[pltpu.VMEM((B, SB_WIDTH), F32), pltpu.VMEM((SB_HEADS, B, 1), F32), pltpu.VMEM((2 * B, B), BF16)],
        semantics=("arbitrary",), args=(qb, kb, vb))


def _sb_bwd(qb, kb, vb, probs, betas, dyb, exchange=None):
    T = qb.shape[0]
    B, i_tab, j_tab = _sb_grid(T, descending=False)
    n_steps = int(i_tab.shape[0])

    def block_diag_t(x):
        xt = x.T
        top = lax.broadcasted_iota(jnp.int32, xt.shape, 0) < HEAD_DIM
        zero = jnp.zeros((), x.dtype)
        return jnp.concatenate([jnp.where(top, xt, zero), jnp.where(top, zero, xt)], axis=1)

    def body(i_ref, j_ref, q_ref, k_ref, v_ref, a_ref, b_ref, do_ref, dq_ref, dk_out, dv_out,
             dq_acc, cg_ref, dkt_ref, dvt_ref, tri_ref, qt_ref, dot_ref):
        s = pl.program_id(0)
        i, j = i_ref[s], j_ref[s]

        @pl.when(s == 0)
        def _():
            dkt_ref[...] = jnp.zeros_like(dkt_ref)
            dvt_ref[...] = jnp.zeros_like(dvt_ref)
            tri_ref[...] = _tri2(B, lambda r, c: r < c)[:B]

        @pl.when(j == 0)
        def _():
            dq_acc[...] = jnp.zeros_like(dq_acc)
            cg_ref[...] = jnp.zeros_like(cg_ref)
            for p, cols in enumerate(PAIR_COLS):
                qt_ref[p] = block_diag_t(q_ref[:, cols] * ATTN_SCALE)
                dot_ref[p] = block_diag_t(do_ref[:, cols])

        for p, cols in enumerate(PAIR_COLS):
            doms = _head_masks(do_ref[:, cols])
            k, v = k_ref[:, cols], v_ref[:, cols]
            dzs = []
            for hh in range(2):
                h = 2 * p + hh
                g = a_ref[h].astype(F32) * _dot_nt(doms[hh], v)
                cg = cg_ref[h]
                gsum = g + (cg + _dot(g.astype(BF16), tri_ref[...]))
                dzs.append((g - b_ref[h].astype(F32) * gsum).astype(BF16))
                cg_ref[h] = cg + jnp.sum(g, axis=1, keepdims=True)
            dq_acc[:, cols] += _dot(jnp.concatenate(dzs, axis=1), jnp.concatenate(_head_masks(k), axis=0))
            dkt_ref[j, cols, :] += _dot(qt_ref[p], jnp.concatenate(dzs, axis=0))
            dvt_ref[j, cols, :] += _dot(dot_ref[p], jnp.concatenate([a_ref[2 * p], a_ref[2 * p + 1]], axis=0))

        @pl.when(j == i)
        def _():
            dq_ref[...] = (dq_acc[...] * ATTN_SCALE).astype(dq_ref.dtype)

        @pl.when(s == n_steps - 1)
        def _():
            for jb in range(T // B):
                dk_out[jb * B:(jb + 1) * B, :] = dkt_ref[jb].T.astype(BF16)
                dv_out[jb * B:(jb + 1) * B, :] = dvt_ref[jb].T.astype(BF16)

    q_spec = pl.BlockSpec((B, SB_WIDTH), lambda s, i_ref, j_ref: (i_ref[s], 0))
    k_spec = pl.BlockSpec((B, SB_WIDTH), lambda s, i_ref, j_ref: (j_ref[s], 0))
    tile = pl.BlockSpec((None, None, SB_HEADS, B, B), lambda s, i_ref, j_ref: (i_ref[s], j_ref[s], 0, 0, 0))
    full = pl.BlockSpec((T, SB_WIDTH), lambda s, i_ref, j_ref: (0, 0))
    return _hosted_call(
        body, "sb_bwd", (n_steps,), exchange, prefetch=(i_tab, j_tab),
        in_specs=[q_spec, k_spec, k_spec, tile, tile, q_spec], out_specs=[q_spec, full, full],
        out_shape=[jax.ShapeDtypeStruct((T, SB_WIDTH), BF16)] * 3,
        scratch=[pltpu.VMEM((B, SB_WIDTH), F32), pltpu.VMEM((SB_HEADS, B, 1), F32), pltpu.VMEM((T // B, SB_WIDTH, B), F32),
                 pltpu.VMEM((T // B, SB_WIDTH, B), F32), pltpu.VMEM((B, B), BF16), pltpu.VMEM((N_PAIRS, LANES, 2 * B), BF16),
                 pltpu.VMEM((N_PAIRS, LANES, 2 * B), BF16)],
        semantics=("arbitrary",), args=(qb, kb, vb, probs, betas, dyb))


def _ln_stats(u):
    mu = jnp.mean(u, axis=-1, keepdims=True)
    xc = u - mu
    var = jnp.mean(xc * xc, axis=-1, keepdims=True)
    rstd = lax.rsqrt(var + LN_EPS)
    return xc * rstd, rstd


def _ln_bwd(dy, xhat, rstd, g):
    dxh = dy * g
    return rstd * (dxh - jnp.mean(dxh, axis=-1, keepdims=True) - xhat * jnp.mean(dxh * xhat, axis=-1, keepdims=True))


def _gates(gl_ref, bg_ref):
    ga = jax.nn.sigmoid(gl_ref[:, :D_MODEL] + bg_ref[:, :D_MODEL])
    gb = jax.nn.sigmoid(gl_ref[:, D_MODEL:] + bg_ref[:, D_MODEL:])
    return ga, gb


def _mix_fwd(ya, yb, gl, x, wa, wb, wo, b_gate, ln1_g, ln1_b):
    T = x.shape[0]
    tm = min(512, T)

    def body(ya_ref, yb_ref, gl_ref, x_ref, wa_ref, wb_ref, wo_ref, bg_ref, g_ref, b_ref, h_ref, u_ref, x1_ref):
        ga, gb = _gates(gl_ref, bg_ref)
        h = (ga * _dot(ya_ref[...], wa_ref[...]) + gb * _dot(yb_ref[...], wb_ref[...])).astype(BF16)
        h_ref[...] = h
        u = ALPHA * x_ref[...] + _dot(h, wo_ref[...])
        u_ref[...] = u
        xhat, _ = _ln_stats(u)
        x1_ref[...] = (xhat * g_ref[...] + b_ref[...]).astype(BF16)

    row = lambda n: pl.BlockSpec((tm, n), lambda i: (i, 0))
    const = lambda r, n: pl.BlockSpec((r, n), lambda i: (0, 0))
    return pl.pallas_call(
        body, name="mix_fwd", grid=(T // tm,),
        in_specs=[row(SWA_Q_WIDTH), row(SB_WIDTH), row(GATE_WIDTH), row(D_MODEL), const(SWA_Q_WIDTH, D_MODEL), const(SB_WIDTH, D_MODEL),
                  const(D_MODEL, D_MODEL), const(1, GATE_WIDTH), const(1, D_MODEL), const(1, D_MODEL)],
        out_specs=[row(D_MODEL)] * 3,
        out_shape=[jax.ShapeDtypeStruct((T, D_MODEL), BF16), jax.ShapeDtypeStruct((T, D_MODEL), F32), jax.ShapeDtypeStruct((T, D_MODEL), BF16)],
        compiler_params=_params(("parallel",)),
    )(ya, yb, gl, x, wa, wb, wo, b_gate, ln1_g, ln1_b)


def _mix_bwd(du1, ya, yb, gl, wa, wb, wo, b_gate):
    T = du1.shape[0]
    tm = min(512, T)

    def body(du_ref, ya_ref, yb_ref, gl_ref, wa_ref, wb_ref, wo_ref, bg_ref, dya_ref, dyb_ref, dgl_ref, dta_ref, dtb_ref, dbg_ref):
        @pl.when(pl.program_id(0) == 0)
        def _():
            dbg_ref[...] = jnp.zeros_like(dbg_ref)

        dh = _dot_nt(du_ref[...].astype(BF16), wo_ref[...])
        ga, gb = _gates(gl_ref, bg_ref)
        for gate, y_ref, w_ref, dy_ref, dt_ref, lo in ((ga, ya_ref, wa_ref, dya_ref, dta_ref, 0), (gb, yb_ref, wb_ref, dyb_ref, dtb_ref, D_MODEL)):
            t = _dot(y_ref[...], w_ref[...])
            dlogit = dh * t * gate * (1.0 - gate)
            dgl_ref[:, lo:lo + D_MODEL] = dlogit.astype(BF16)
            dbg_ref[:, lo:lo + D_MODEL] += jnp.sum(dlogit, axis=0, keepdims=True)
            dt = (dh * gate).astype(BF16)
            dt_ref[...] = dt
            dy_ref[...] = _dot_nt(dt, w_ref[...]).astype(BF16)

    row = lambda n: pl.BlockSpec((tm, n), lambda i: (i, 0))
    const = lambda r, n: pl.BlockSpec((r, n), lambda i: (0, 0))
    sds = lambda n, dt: jax.ShapeDtypeStruct((T, n), dt)
    return pl.pallas_call(
        body, name="mix_bwd", grid=(T // tm,),
        in_specs=[row(D_MODEL), row(SWA_Q_WIDTH), row(SB_WIDTH), row(GATE_WIDTH), const(SWA_Q_WIDTH, D_MODEL), const(SB_WIDTH, D_MODEL),
                  const(D_MODEL, D_MODEL), const(1, GATE_WIDTH)],
        out_specs=[row(SWA_Q_WIDTH), row(SB_WIDTH), row(GATE_WIDTH), row(D_MODEL), row(D_MODEL), const(1, GATE_WIDTH)],
        out_shape=[sds(SWA_Q_WIDTH, BF16), sds(SB_WIDTH, BF16), sds(GATE_WIDTH, BF16), sds(D_MODEL, BF16), sds(D_MODEL, BF16),
                   jax.ShapeDtypeStruct((1, GATE_WIDTH), F32)],
        compiler_params=_params(("arbitrary",)),
    )(du1, ya, yb, gl, wa, wb, wo, b_gate)


CONV_COLS = LANES


CONV_CHUNK = 64
CONV_CHUNK_FWD = 256
HALO = 8


def _taps(ref, r0, rows, lead):
    return [ref[pl.ds(r0 + lead + k, rows), :] for k in ((-2, -1, 0) if lead else (0, 1, 2))]


def _chunks(T, rows, step, init=None):
    def body(c, carry):
        out = step(pl.multiple_of(c * rows, rows), *(() if init is None else (carry,)))
        return carry if init is None else out
    return lax.fori_loop(0, T // rows, body, 0 if init is None else init)


def _conv_chunk(taps, w_ref, b_ref):
    return w_ref[0:1, :] * taps[0] + w_ref[1:2, :] * taps[1] + w_ref[2:3, :] * taps[2] + b_ref[...]


def _fold(x):
    return jnp.sum(x.reshape(x.shape[0] // 8, 8, x.shape[1]), axis=0)


def _conv_specs(T):
    nb = D_FF // CONV_COLS
    pair = pl.BlockSpec((2, T, CONV_COLS), lambda j: (0, 0, j))
    gate = lambda r: pl.BlockSpec((r, CONV_COLS), lambda j: (0, j))
    up = lambda r: pl.BlockSpec((r, CONV_COLS), lambda j: (0, j + nb))
    return nb, pair, gate, up


def _conv_glu_fwd(p3, conv_w, conv_b):
    T = p3.shape[1]
    nb, pair, gate, up = _conv_specs(T)

    R = min(CONV_CHUNK_FWD, T)

    def body(p_ref, wg_ref, wu_ref, bg_ref, bu_ref, s_ref, pg_s, pu_s):
        for half, scr in enumerate((pg_s, pu_s)):
            scr[0:HALO, :] = jnp.zeros((HALO, CONV_COLS), F32)
            scr[HALO:HALO + T, :] = p_ref[half].astype(F32)
        def step(r0):
            ag = _conv_chunk(_taps(pg_s, r0, R, HALO), wg_ref, bg_ref)
            au = _conv_chunk(_taps(pu_s, r0, R, HALO), wu_ref, bu_ref)
            s_ref[pl.ds(r0, R), :] = (ag * jax.nn.sigmoid(ag) * au).astype(BF16)

        _chunks(T, R, step)

    return pl.pallas_call(
        body, name="conv_glu_fwd", grid=(nb,),
        in_specs=[pair, gate(3), up(3), gate(1), up(1)],
        out_specs=pl.BlockSpec((T, CONV_COLS), lambda j: (0, j)),
        out_shape=jax.ShapeDtypeStruct((T, D_FF), BF16),
        scratch_shapes=[pltpu.VMEM((T + HALO, CONV_COLS), F32)] * 2,
        compiler_params=_params(("parallel",)),
    )(p3, conv_w, conv_w, conv_b, conv_b)


def _conv_glu_bwd(p3, ds, conv_w, conv_b):
    T = p3.shape[1]
    nb, pair, gate, up = _conv_specs(T)

    R = min(CONV_CHUNK, T)

    def body(p_ref, ds_ref, wg_ref, wu_ref, bg_ref, bu_ref, dp_ref, dwg_ref, dwu_ref, dbg_ref, dbu_ref, pg_s, pu_s, dag_s, dau_s):
        for half, scr in enumerate((pg_s, pu_s)):
            scr[0:HALO, :] = jnp.zeros((HALO, CONV_COLS), F32)
            scr[HALO:HALO + T, :] = p_ref[half].astype(F32)
        for scr in (dag_s, dau_s):
            scr[T:T + HALO, :] = jnp.zeros((HALO, CONV_COLS), F32)
        halves = ((pg_s, dag_s, wg_ref, dwg_ref, dbg_ref), (pu_s, dau_s, wu_ref, dwu_ref, dbu_ref))

        def step(r0, sums):
            taps = [_taps(p_s, r0, R, HALO) for p_s, *_ in halves]
            ag = _conv_chunk(taps[0], wg_ref, bg_ref)
            au = _conv_chunk(taps[1], wu_ref, bu_ref)
            sg = jax.nn.sigmoid(ag)
            d = ds_ref[pl.ds(r0, R), :].astype(F32)
            das = (d * au * (sg * (1.0 + ag * (1.0 - sg))), d * ag * sg)
            out = []
            for half, (_, da_s, *_) in enumerate(halves):
                da_s[pl.ds(r0, R), :] = das[half]
                out.append(tuple(sums[half][k] + _fold(das[half] * taps[half][k]) for k in range(3)) + (sums[half][3] + _fold(das[half]),))
            return tuple(out)

        sums = _chunks(T, R, step, ((jnp.zeros((8, CONV_COLS), F32),) * 4,) * 2)
        for half, (_, da_s, w_ref, dw_ref, db_ref) in enumerate(halves):
            for k in range(3):
                dw_ref[k:k + 1, :] = jnp.sum(sums[half][k], axis=0, keepdims=True)
            db_ref[...] = jnp.sum(sums[half][3], axis=0, keepdims=True)

            def transposed(r0, da_s=da_s, w_ref=w_ref, half=half):
                da0, da1, da2 = _taps(da_s, r0, R, 0)
                dp_ref[half, pl.ds(r0, R), :] = (w_ref[2:3, :] * da0 + w_ref[1:2, :] * da1 + w_ref[0:1, :] * da2).astype(BF16)

            _chunks(T, R, transposed)

    col = lambda r: pl.BlockSpec((r, CONV_COLS), lambda j: (0, j))
    return pl.pallas_call(
        body, name="conv_glu_bwd", grid=(nb,),
        in_specs=[pair, col(T), gate(3), up(3), gate(1), up(1)],
        out_specs=[pair, col(3), col(3), col(1), col(1)],
        out_shape=[jax.ShapeDtypeStruct((2, T, D_FF), BF16), jax.ShapeDtypeStruct((3, D_FF), F32), jax.ShapeDtypeStruct((3, D_FF), F32),
                   jax.ShapeDtypeStruct((1, D_FF), F32), jax.ShapeDtypeStruct((1, D_FF), F32)],
        scratch_shapes=[pltpu.VMEM((T + HALO, CONV_COLS), F32)] * 4,
        compiler_params=_params(("parallel",)),
    )(p3, ds, conv_w, conv_w, conv_b, conv_b)


def _ffn_down_loss(s, w_down, u1, ln1_g, ln1_b, ln2_g, ln2_b, target):
    T = u1.shape[0]
    tm = min(512, T)

    def body(s_ref, w_ref, u1_ref, g1_ref, b1_ref, g2_ref, b2_ref, t_ref, du_ref, dub_ref, dg_ref, db_ref, loss_ref):
        @pl.when(pl.program_id(0) == 0)
        def _():
            dg_ref[...] = jnp.zeros_like(dg_ref)
            db_ref[...] = jnp.zeros_like(db_ref)
            loss_ref[...] = jnp.zeros_like(loss_ref)

        xh1, _ = _ln_stats(u1_ref[...])
        x1 = xh1 * g1_ref[...] + b1_ref[...]
        u2 = ALPHA * x1 + _dot(s_ref[...], w_ref[...])
        xh2, rstd2 = _ln_stats(u2)
        err = xh2 * g2_ref[...] + b2_ref[...] - t_ref[...]
        per_token = jnp.mean(err * err, axis=-1, keepdims=True)
        loss_ref[...] += 0.5 * jnp.sum(per_token, axis=0, keepdims=True)
        dy = err * (1.0 / D_MODEL)
        dg_ref[...] += jnp.sum(dy * xh2, axis=0, keepdims=True)
        db_ref[...] += jnp.sum(dy, axis=0, keepdims=True)
        du2 = _ln_bwd(dy, xh2, rstd2, g2_ref[...])
        du_ref[...] = du2
        dub_ref[...] = du2.astype(BF16)

    row = lambda n: pl.BlockSpec((tm, n), lambda i: (i, 0))
    const = lambda r, n: pl.BlockSpec((r, n), lambda i: (0, 0))
    vec = const(1, D_MODEL)
    return pl.pallas_call(
        body, name="ffn_down_loss", grid=(T // tm,),
        in_specs=[row(D_FF), const(D_FF, D_MODEL), row(D_MODEL), vec, vec, vec, vec, row(D_MODEL)],
        out_specs=[row(D_MODEL), row(D_MODEL), vec, vec, const(1, LANES)],
        out_shape=[jax.ShapeDtypeStruct((T, D_MODEL), F32), jax.ShapeDtypeStruct((T, D_MODEL), BF16), jax.ShapeDtypeStruct((1, D_MODEL), F32),
                   jax.ShapeDtypeStruct((1, D_MODEL), F32), jax.ShapeDtypeStruct((1, LANES), F32)],
        compiler_params=_params(("arbitrary",)),
    )(s, w_down, u1, ln1_g, ln1_b, ln2_g, ln2_b, target)


def _ffn_up_bwd_ln1(dp3, w_up, du2, u1, ln1_g):
    T = u1.shape[0]
    tm = min(256, T)

    def body(dp_ref, w_ref, du2_ref, u1_ref, g_ref, du_ref, dub_ref, dg_ref, db_ref):
        @pl.when(pl.program_id(0) == 0)
        def _():
            dg_ref[...] = jnp.zeros_like(dg_ref)
            db_ref[...] = jnp.zeros_like(db_ref)

        dx1 = _dot(dp_ref[0], w_ref[:D_FF, :]) + _dot(dp_ref[1], w_ref[D_FF:, :]) + ALPHA * du2_ref[...]
        xh, rstd = _ln_stats(u1_ref[...])
        dg_ref[...] += jnp.sum(dx1 * xh, axis=0, keepdims=True)
        db_ref[...] += jnp.sum(dx1, axis=0, keepdims=True)
        du1 = _ln_bwd(dx1, xh, rstd, g_ref[...])
        du_ref[...] = du1
        dub_ref[...] = du1.astype(BF16)

    row = lambda n: pl.BlockSpec((tm, n), lambda i: (i, 0))
    const = lambda r, n: pl.BlockSpec((r, n), lambda i: (0, 0))
    vec = const(1, D_MODEL)
    return pl.pallas_call(
        body, name="ffn_up_bwd_ln1", grid=(T // tm,),
        in_specs=[pl.BlockSpec((2, tm, D_FF), lambda i: (0, i, 0)), const(2 * D_FF, D_MODEL), row(D_MODEL), row(D_MODEL), vec],
        out_specs=[row(D_MODEL), row(D_MODEL), vec, vec],
        out_shape=[jax.ShapeDtypeStruct((T, D_MODEL), F32), jax.ShapeDtypeStruct((T, D_MODEL), BF16), jax.ShapeDtypeStruct((1, D_MODEL), F32),
                   jax.ShapeDtypeStruct((1, D_MODEL), F32)],
        compiler_params=_params(("arbitrary",)),
    )(dp3, w_up, du2, u1, ln1_g)


def _local_step(x, positions, w_in, b_gate, sinks, ln1_g, ln1_b, conv_b, ln2_g, ln2_b, target, later_weights,
                early_exchange=None, tail_exchange=None):
    T = x.shape[0]
    inv_freq = 1.0 / (ROPE_THETA ** (jnp.arange(0, HEAD_DIM, 2, dtype=F32) / HEAD_DIM))
    cos, sin = _rope_tables(positions.reshape(T, 1), jnp.tile(inv_freq, LANES // (HEAD_DIM // 2)).reshape(1, LANES))

    xb, qa, ka, va, qb, kb, vb, gl = _in_proj(x, w_in)
    ya = _swa_fwd(qa, ka, va, cos, sin, sinks)
    if isinstance(later_weights, tuple):
        exchange, finish = later_weights
        (yb, probs, betas), arrived = _sb_fwd(qb, kb, vb, exchange)
        later_weights = finish(arrived)
    else:
        (yb, probs, betas), _ = _sb_fwd(qb, kb, vb)
    wa, wb, wo, w_up, conv_w, w_down = later_weights
    h, u1, x1 = _mix_fwd(ya, yb, gl, x, wa, wb, wo, b_gate, ln1_g, ln1_b)

    ff_tn = D_FF // 2
    nff = D_FF // ff_tn
    tm = min(1024, T)
    p3 = _matmul(x1, w_up, kind="nt", name="ffn_up", grid=(T // tm, 2 * nff),
                 a_spec=pl.BlockSpec((tm, D_MODEL), lambda i, j: (i, 0)), b_spec=pl.BlockSpec((ff_tn, D_MODEL), lambda i, j: (j, 0)),
                 out_spec=pl.BlockSpec((None, tm, ff_tn), lambda i, j: (j // nff, i, j % nff)),
                 out_shape=jax.ShapeDtypeStruct((2, T, D_FF), ACT_DTYPE))
    s = _conv_glu_fwd(p3, conv_w, conv_b)
    du2, du2b, dln2_g, dln2_b, loss = _ffn_down_loss(s, w_down, u1, ln1_g, ln1_b, ln2_g, ln2_b, target)

    ds = _matmul(du2b, w_down, kind="nt", name="ffn_down_bwd", grid=(T // tm, nff),
                 a_spec=pl.BlockSpec((tm, D_MODEL), lambda i, j: (i, 0)), b_spec=pl.BlockSpec((ff_tn, D_MODEL), lambda i, j: (j, 0)),
                 out_spec=pl.BlockSpec((tm, ff_tn), lambda i, j: (i, j)), out_shape=jax.ShapeDtypeStruct((T, D_FF), ACT_DTYPE))
    dp3, dcw_g, dcw_u, dcb_g, dcb_u = _conv_glu_bwd(p3, ds, conv_w, conv_b)
    tk = 256
    dw_down = _matmul(s, du2b, kind="tn", name="dw_down", grid=(D_FF // tk,),
                      a_spec=pl.BlockSpec((T, tk), lambda i: (0, i)), b_spec=pl.BlockSpec((T, D_MODEL), lambda i: (0, 0)),
                      out_spec=pl.BlockSpec((tk, D_MODEL), lambda i: (i, 0)), out_shape=jax.ShapeDtypeStruct((D_FF, D_MODEL), BF16))
    dw_up = _matmul(dp3, x1, kind="tn", name="dw_up", grid=(2 * nff,),
                    a_spec=pl.BlockSpec((None, T, ff_tn), lambda j: (j // nff, 0, j % nff)), b_spec=pl.BlockSpec((T, D_MODEL), lambda j: (0, 0)),
                    out_spec=pl.BlockSpec((ff_tn, D_MODEL), lambda j: (j, 0)), out_shape=jax.ShapeDtypeStruct((2 * D_FF, D_MODEL), BF16))
    du1, du1b, dln1_g, dln1_b = _ffn_up_bwd_ln1(dp3, w_up, du2, u1, ln1_g)
    dya, dyb, dgl, dta, dtb, db_gate = _mix_bwd(du1, ya, yb, gl, wa, wb, wo, b_gate)

    def dw_tn(a, g, name):
        rows, cols = a.shape[1], g.shape[1]
        tn = min(512, cols)
        return _matmul(a, g, kind="tn", name=name, grid=(rows // 512, cols // tn),
                       a_spec=pl.BlockSpec((T, 512), lambda i, j: (0, i)), b_spec=pl.BlockSpec((T, tn), lambda i, j: (0, j)),
                       out_spec=pl.BlockSpec((512, tn), lambda i, j: (i, j)), out_shape=jax.ShapeDtypeStruct((rows, cols), BF16))

    dwa = dw_tn(ya, dta, "dw_branch_a")
    dwb = dw_tn(yb, dtb, "dw_branch_b")
    dwo = dw_tn(h, du1b, "dw_out")

    grads = dict(
        b_gate=db_gate, w_branch_a=dwa, w_branch_b=dwb, w_out=dwo, ln1_g=dln1_g, ln1_b=dln1_b,
        w_up=dw_up, conv_w=jnp.concatenate([dcw_g, dcw_u], axis=1), conv_b=(dcb_g, dcb_u), w_down=dw_down, ln2_g=dln2_g, ln2_b=dln2_b)
    (dqb, dkb, dvb), early_out = _sb_bwd(qb, kb, vb, probs, betas, dyb, early_exchange(grads) if early_exchange else None)
    dqa, dka, dva, grads["sinks"] = _swa_bwd(qa, ka, va, cos, sin, sinks, dya)
    dproj = (dqa, dka, dva, dqb, dkb, dvb, dgl)
    grads["w_in"] = _dw_in(xb, dproj)
    grad_x, tail_out = _grad_x(dproj, w_in, du1, tail_exchange(grads, loss) if tail_exchange else None)
    return loss, grad_x, grads, early_out, tail_out


def _dw_in(xb, dproj):
    T = xb.shape[0]
    tn = 2 * LANES
    groups, start, k = [], 0, 0
    while k < len(IN_WIDTHS):
        if IN_WIDTHS[k] >= tn:
            groups.append((start, IN_WIDTHS[k] // tn, [(k, 0, tn)]))
            k += 1
        else:
            members, off = [], 0
            while off < tn:
                members.append((k, off, IN_WIDTHS[k]))
                off += IN_WIDTHS[k]
                k += 1
            groups.append((start, 1, members))
        start += groups[-1][1]

    def body(x_ref, *refs):
        pieces, o_ref = refs[:-1], refs[-1]
        j = pl.program_id(0)
        for first, steps, members in groups:
            @pl.when((j >= first) & (j < first + steps))
            def _(members=members):
                for k, off, width in members:
                    o_ref[off:off + width, :] = _dot_tn(pieces[k][...], x_ref[...]).astype(o_ref.dtype)

    specs = [None] * len(IN_WIDTHS)
    for first, steps, members in groups:
        for k, _, width in members:
            specs[k] = pl.BlockSpec((T, width), lambda j, first=first, steps=steps: (0, jnp.clip(j - first, 0, steps - 1)))
    return pl.pallas_call(
        body, name="dw_in", grid=(IN_TOTAL // tn,),
        in_specs=[pl.BlockSpec((T, D_MODEL), lambda j: (0, 0))] + specs, out_specs=pl.BlockSpec((tn, D_MODEL), lambda j: (j, 0)),
        out_shape=jax.ShapeDtypeStruct((IN_TOTAL, D_MODEL), BF16), compiler_params=_params(("arbitrary",)),
    )(xb, *dproj)


def _grad_x(dproj, w_in, du1, exchange=None):
    T = du1.shape[0]
    tm = min(512, T)
    offs = np.cumsum((0,) + IN_WIDTHS)

    def body(*refs):
        pieces, (w_ref, du_ref, o_ref) = refs[:len(IN_WIDTHS)], refs[len(IN_WIDTHS):]
        acc = ALPHA * du_ref[...]
        for p_ref, a, b in zip(pieces, offs[:-1], offs[1:]):
            acc = acc + _dot(p_ref[...].astype(BF16), w_ref[a:b, :])
        o_ref[...] = acc

    row = lambda n: pl.BlockSpec((tm, n), lambda i: (i, 0))
    (grad_x,), arrived = _hosted_call(
        body, "grad_x", (T // tm,), exchange,
        in_specs=[row(n) for n in IN_WIDTHS] + [pl.BlockSpec((IN_TOTAL, D_MODEL), lambda i: (0, 0)), row(D_MODEL)],
        out_specs=[row(D_MODEL)], out_shape=[jax.ShapeDtypeStruct((T, D_MODEL), F32)], semantics=("parallel",),
        args=(*dproj, w_in, du1))
    return grad_x, arrived


ANY = pl.BlockSpec(memory_space=pl.ANY)


def _all_gather(slabs, name):
    n = len(slabs)

    def body(*refs):
        ins, outs = refs[:n], refs[n:2 * n]
        send_sems, recv_sems, local_sems = refs[2 * n:]
        x, y, c = lax.axis_index("x"), lax.axis_index("y"), lax.axis_index("c")
        me, sibling = (x, y, c), (x, y, 1 - c)
        chips = [(1 - x, y), (x, 1 - y), (1 - x, 1 - y)]

        def slot(pos):
            return 4 * pos[0] + 2 * pos[1] + pos[2]

        def copy(a, k, block, to, from_input=False):
            return pltpu.make_async_remote_copy(
                src_ref=ins[a] if from_input else outs[a].at[slot(block)], dst_ref=outs[a].at[slot(block)],
                send_sem=send_sems.at[a, k], recv_sem=recv_sems.at[a, k], device_id=to, device_id_type=MESH)

        mine = [pltpu.make_async_copy(ins[a], outs[a].at[slot(me)], local_sems.at[a]) for a in range(n)]
        for cp in mine:
            cp.start()
        first = []
        for a in range(n):
            first.append(copy(a, 0, me, sibling, from_input=True))
            first += [copy(a, 1 + j, me, (*chip, c), from_input=True) for j, chip in enumerate(chips)]
        for cp in first:
            cp.start()
        passed = []
        for j, chip in enumerate(chips):
            for a in range(n):
                copy(a, 1 + j, (*chip, c), me).wait_recv()
                fwd = copy(a, 4 + j, (*chip, c), sibling)
                fwd.start()
                passed.append(fwd)
        for a in range(n):
            copy(a, 0, sibling, me).wait_recv()
            for j, chip in enumerate(chips):
                copy(a, 4 + j, (*chip, 1 - c), me).wait_recv()
        for cp in first + passed:
            cp.wait_send()
        for cp in mine:
            cp.wait()

    return pl.pallas_call(
        body, name=name,
        in_specs=[ANY] * n, out_specs=[ANY] * n,
        out_shape=[jax.ShapeDtypeStruct((N_DEV,) + s.shape, s.dtype) for s in slabs],
        scratch_shapes=[pltpu.SemaphoreType.DMA((n, 7)), pltpu.SemaphoreType.DMA((n, 7)), pltpu.SemaphoreType.DMA((n,))],
    )(*slabs)


def _row_tile(rows):
    for cand in range(256, 7, -8):
        if rows % cand == 0:
            return cand
    return rows


def _window(w):
    wp = max(-(-((w * r) % LANES + w) // LANES) for r in range(N_DEV)) * LANES
    assert all((w * r) // LANES * LANES + wp <= N_DEV * w for r in range(N_DEV))
    return wp


def _join_cols(slabs, name):
    _, R, w = slabs.shape
    tr = _row_tile(R)
    wp = _window(w)

    def body(g_ref, o_ref, pad_ref):
        if w % LANES == 0:
            for r in range(N_DEV):
                o_ref[:, w * r:w * (r + 1)] = g_ref[r]
            return
        o_ref[...] = jnp.zeros_like(o_ref)
        pad_ref[...] = jnp.zeros_like(pad_ref)
        for r in range(N_DEV):
            q, s = divmod(w * r, LANES)
            pad_ref[:, :w] = g_ref[r]
            y = pad_ref[...]
            if s:
                y = pltpu.roll(y, s, axis=1)
            o_ref[:, LANES * q:LANES * q + wp] += y

    return pl.pallas_call(
        body, name=name, grid=(R // tr,),
        in_specs=[pl.BlockSpec((N_DEV, tr, w), lambda i: (0, i, 0))], out_specs=pl.BlockSpec((tr, N_DEV * w), lambda i: (i, 0)),
        out_shape=jax.ShapeDtypeStruct((R, N_DEV * w), slabs.dtype), scratch_shapes=[pltpu.VMEM((tr, wp), slabs.dtype)],
        compiler_params=_params(("parallel",)),
    )(slabs)


def _split_cols(pieces, name):
    R = pieces[0].shape[0]
    widths = [p.shape[1] for p in pieces]
    total = sum(widths)
    w = total // N_DEV
    tr = _row_tile(R)
    wp = _window(w)
    offs = np.cumsum([0] + widths)
    dtype = pieces[0].dtype

    def body(*refs):
        ins, (o_ref, full_ref) = refs[:len(pieces)], refs[len(pieces):]
        for p_ref, a, b in zip(ins, offs[:-1], offs[1:]):
            full_ref[:, a:b] = p_ref[...].astype(dtype)
        for r in range(N_DEV):
            q, s = divmod(w * r, LANES)
            y = full_ref[:, LANES * q:LANES * q + wp]
            if s:
                y = pltpu.roll(y, wp - s, axis=1)
            o_ref[r] = y[:, :w]

    return pl.pallas_call(
        body, name=name, grid=(R // tr,),
        in_specs=[pl.BlockSpec((tr, n), lambda i: (i, 0)) for n in widths], out_specs=pl.BlockSpec((N_DEV, tr, w), lambda i: (0, i, 0)),
        out_shape=jax.ShapeDtypeStruct((N_DEV, R, w), dtype), scratch_shapes=[pltpu.VMEM((tr, total), dtype)],
        compiler_params=_params(("parallel",)),
    )(*pieces)


def _adamw(g, w, m, v):
    m_new = ADAM_B1 * m + (1.0 - ADAM_B1) * g
    v_new = ADAM_B2 * v + (1.0 - ADAM_B2) * jnp.square(g)
    m_hat = m_new / (1.0 - ADAM_B1 ** ADAM_STEP)
    v_hat = v_new / (1.0 - ADAM_B2 ** ADAM_STEP)
    return -ADAM_LR * (m_hat / (jnp.sqrt(v_hat) + ADAM_EPS) + ADAM_WD * w), m_new, v_new


def _sum_parts(p_ref):
    g = p_ref[0].astype(F32)
    for d in range(1, N_DEV):
        g = g + p_ref[d].astype(F32)
    return g


def _reduce_adamw(parts, w, m, v, name):
    R, C = w.shape
    tr = _row_tile(R)

    def body(p_ref, w_ref, m_ref, v_ref, g_ref, d_ref, mo_ref, vo_ref):
        g = _sum_parts(p_ref)
        g_ref[...] = g
        d_ref[...], mo_ref[...], vo_ref[...] = _adamw(g, w_ref[...], m_ref[...], v_ref[...])

    row = pl.BlockSpec((tr, C), lambda i: (i, 0))
    return pl.pallas_call(
        body, name=name, grid=(R // tr,),
        in_specs=[pl.BlockSpec((N_DEV, tr, C), lambda i: (0, i, 0)), row, row, row],
        out_specs=[row] * 4, out_shape=[jax.ShapeDtypeStruct((R, C), F32)] * 4,
        compiler_params=_params(("parallel",)),
    )(parts, w, m, v)


def _reduce_adamw_small(parts, ws, ms, vs):
    sizes = [a.shape[1] for a in ws]
    k = len(sizes)
    offs = np.cumsum([0] + [-(-n // LANES) * LANES for n in sizes])

    def body(*refs):
        p_ref, w_refs, m_refs, v_refs = refs[0], refs[1:1 + k], refs[1 + k:1 + 2 * k], refs[1 + 2 * k:1 + 3 * k]
        outs, loss_ref = refs[1 + 3 * k:-1], refs[-1]
        g_all = _sum_parts(p_ref)
        for j, n in enumerate(sizes):
            g = g_all[:, offs[j]:offs[j] + LANES * (-(-n // LANES))][:, :n]
            outs[4 * j][...] = g
            outs[4 * j + 1][...], outs[4 * j + 2][...], outs[4 * j + 3][...] = _adamw(g, w_refs[j][...], m_refs[j][...], v_refs[j][...])
        loss_ref[...] = g_all[:, offs[k]:offs[k] + LANES]

    vm = pl.BlockSpec(memory_space=pltpu.VMEM)
    out_shape = [jax.ShapeDtypeStruct((1, n), F32) for n in sizes for _ in range(4)] + [jax.ShapeDtypeStruct((1, LANES), F32)]
    res = pl.pallas_call(
        body, name="reduce_adamw_replicated", in_specs=[vm] * (1 + 3 * k), out_specs=[vm] * len(out_shape), out_shape=out_shape,
        compiler_params=_params(),
    )(parts, *ws, *ms, *vs)
    return [res[4 * j:4 * j + 4] for j in range(k)], res[-1]


TRANSPOSED = ("w_in", "w_up")
COL_SHARDED = ("w_branch_a", "w_branch_b", "conv_w")
ROW_SHARDED = TRANSPOSED + ("w_out", "w_down")
SMALL = ("b_gate", "sinks", "ln1_g", "ln1_b", "conv_b", "ln2_g", "ln2_b")
ORDER = ("w_in", "b_gate", "sinks", "w_branch_a", "w_branch_b", "w_out", "ln1_g", "ln1_b", "w_up", "conv_w", "conv_b", "w_down", "ln2_g", "ln2_b")


def _pad_lanes(a):
    pad = (-a.shape[-1]) % LANES
    return a if pad == 0 else jnp.pad(a, ((0, 0), (0, pad)))


def kernel(x, positions, w_in, b_gate, sinks, w_branch_a, w_branch_b, w_out, ln1_g, ln1_b, w_up, conv_w, conv_b, w_down, ln2_g, ln2_b, loss_target, m_w_in, m_b_gate, m_sinks, m_w_branch_a, m_w_branch_b, m_w_out, m_ln1_g, m_ln1_b, m_w_up, m_conv_w, m_conv_b, m_w_down, m_ln2_g, m_ln2_b, v_w_in, v_b_gate, v_sinks, v_w_branch_a, v_w_branch_b, v_w_out, v_ln1_g, v_ln1_b, v_w_up, v_conv_w, v_conv_b, v_w_down, v_ln2_g, v_ln2_b):
    args = dict(locals())
    sharded = COL_SHARDED + ROW_SHARDED

    def shard(name, a):
        return a if name not in sharded else a[0].T if name in TRANSPOSED else a[0]

    w = {n: shard(n, args[n]) for n in ORDER}
    m = {n: shard(n, args["m_" + n]) for n in ORDER}
    v = {n: shard(n, args["v_" + n]) for n in ORDER}

    travel = {n: (w[n] if n == "conv_w" else w[n].astype(BF16)) for n in sharded}
    (g_in,) = _all_gather([travel["w_in"]], "all_gather_w_in")
    w_in_full = g_in.reshape(-1, g_in.shape[-1])
    later = ("w_branch_a", "w_branch_b", "w_out", "w_up", "conv_w", "w_down")

    def join(name, slabs):
        return _join_cols(slabs, "join_" + name) if name in COL_SHARDED else slabs.reshape(-1, slabs.shape[-1])

    def split(name, grad):
        if name in COL_SHARDED:
            return _split_cols(grad if isinstance(grad, tuple) else (grad,), "split_d" + name)
        return grad.reshape((N_DEV, -1, grad.shape[-1]))

    def early_exchange(grads):
        return _Exchange([split(n, grads[n]) for n in later], ["scatter"] * len(later))

    def tail_exchange(grads, loss):
        small_pack = jnp.concatenate(
            [_pad_lanes(p) for n in SMALL for p in (grads[n] if isinstance(grads[n], tuple) else (grads[n],))] + [loss], axis=1)
        return _Exchange([split("w_in", grads["w_in"]), small_pack], ["scatter", "gather"])

    gather_later = _Exchange([travel[n] for n in later], ["gather"] * len(later))
    _, grad_x, _, early_out, (recv_w_in, small_parts) = _local_step(
        x[0], positions[0], w_in_full, w["b_gate"], w["sinks"][0], w["ln1_g"], w["ln1_b"], w["conv_b"], w["ln2_g"], w["ln2_b"], loss_target[0],
        (gather_later, lambda arrived: [join(n, a) for n, a in zip(later, arrived)]), early_exchange, tail_exchange)
    recv = dict(zip(later, early_out), w_in=recv_w_in)

    res = {n: _reduce_adamw(recv[n], w[n], m[n], v[n], "reduce_adamw_" + n) for n in sharded}
    small_res, loss_sum = _reduce_adamw_small(small_parts, [w[n] for n in SMALL], [m[n] for n in SMALL], [v[n] for n in SMALL])
    res.update(zip(SMALL, small_res))
    out = [loss_sum[0, 0], grad_x[None]]
    for k in range(4):
        out += [res[n][k].T[None] if n in TRANSPOSED else res[n][k][None] if n in sharded else res[n][k] for n in ORDER]
    return tuple(out)
```

```python
import functools

import jax
import jax.numpy as jnp
import numpy as np
from jax import lax
from jax.experimental import pallas as pl
from jax.experimental.pallas import tpu as pltpu

D_MODEL = 1024
HEAD_DIM = 64
SWA_Q_HEADS = 8
SWA_KV_HEADS = 2
SB_HEADS = 8
WINDOW = 128
ROPE_THETA = 10000.0
D_FF = 2816
LN_EPS = 1e-5
DEPTH = 1
ALPHA = (2.0 * DEPTH) ** 0.25
SWA_Q_WIDTH = SWA_Q_HEADS * HEAD_DIM
SWA_KV_WIDTH = SWA_KV_HEADS * HEAD_DIM
SB_WIDTH = SB_HEADS * HEAD_DIM
GATE_WIDTH = 2 * D_MODEL
IN_WIDTHS = (SWA_Q_WIDTH, SWA_KV_WIDTH, SWA_KV_WIDTH, SB_WIDTH, SB_WIDTH, SB_WIDTH, GATE_WIDTH)
IN_TOTAL = sum(IN_WIDTHS)
ATTN_SCALE = HEAD_DIM ** -0.5

ADAM_LR = 0.001
ADAM_B1 = 0.9
ADAM_B2 = 0.999
ADAM_EPS = 1e-08
ADAM_WD = 0.01
ADAM_STEP = 10

N_DEV = 8
LANES = 128
SB_BLOCK = 256
VMEM_LIMIT = 56 * 1024 * 1024

F32 = jnp.float32
BF16 = jnp.bfloat16
ACT_DTYPE = BF16
MESH = pl.DeviceIdType.MESH


def _params(sem=None):
    return pltpu.CompilerParams(dimension_semantics=sem, vmem_limit_bytes=VMEM_LIMIT)


def _dot(a, b):
    return jnp.dot(a, b, preferred_element_type=F32)


def _dot_nt(a, b):
    return lax.dot_general(a, b, (((1,), (1,)), ((), ())), preferred_element_type=F32)


def _dot_tn(a, b):
    return lax.dot_general(a, b, (((0,), (0,)), ((), ())), preferred_element_type=F32)


def _split_bf16(v):
    hi = v.astype(BF16)
    lo = (v - hi.astype(F32)).astype(BF16)
    return hi, lo


def _matmul(a, b, *, kind, out_shape, grid, a_spec, b_spec, out_spec, name, add=None, add_spec=None, add_scale=1.0):
    dot = {"nn": _dot, "nt": _dot_nt, "tn": _dot_tn}[kind]

    def body(*refs):
        if add is None:
            a_ref, b_ref, o_ref = refs
        else:
            a_ref, b_ref, add_ref, o_ref = refs
        r = dot(a_ref[...].astype(BF16), b_ref[...].astype(BF16))
        if add is not None:
            r = r + add_scale * add_ref[...]
        o_ref[...] = r.astype(o_ref.dtype)

    ins = [a, b] + ([] if add is None else [add])
    specs = [a_spec, b_spec] + ([] if add is None else [add_spec])
    return pl.pallas_call(
        body, name=name, grid=grid, in_specs=specs, out_specs=out_spec, out_shape=out_shape,
        compiler_params=_params(("parallel",) * len(grid)),
    )(*ins)


def _rope_tables(pos_col, inv_freq_lanes):
    T = pos_col.shape[0]
    tm = min(512, T)

    def body(pos_ref, f_ref, cos_ref, sin_ref):
        ang = pos_ref[...].astype(F32) * f_ref[...]
        cos_ref[...] = jnp.cos(ang)
        sin_ref[...] = jnp.sin(ang)

    return pl.pallas_call(
        body, name="rope_tables", grid=(T // tm,),
        in_specs=[pl.BlockSpec((tm, 1), lambda i: (i, 0)), pl.BlockSpec((1, LANES), lambda i: (0, 0))],
        out_specs=[pl.BlockSpec((tm, LANES), lambda i: (i, 0))] * 2,
        out_shape=[jax.ShapeDtypeStruct((T, LANES), F32)] * 2,
        compiler_params=_params(("parallel",)),
    )(pos_col, inv_freq_lanes)


def _lane_iota(shape):
    return lax.broadcasted_iota(jnp.int32, shape, len(shape) - 1)


def _rot_half(t):
    first = (_lane_iota(t.shape) % HEAD_DIM) < (HEAD_DIM // 2)
    return jnp.where(first, -pltpu.roll(t, LANES - HEAD_DIM // 2, axis=1), pltpu.roll(t, HEAD_DIM // 2, axis=1))


def _rope(t, cos, sin):
    return t * cos + _rot_half(t) * sin


def _rope_transpose(d, cos, sin):
    return d * cos - _rot_half(d * sin)


_IN_DTYPES = (F32, F32, BF16, BF16, BF16, BF16, F32)


def _in_proj(x, w_in_t):
    T = x.shape[0]
    tm = min(512, T)
    offs = np.cumsum((0,) + IN_WIDTHS)

    def body(x_ref, w_ref, xb_ref, *outs):
        xb = x_ref[...].astype(BF16)
        xb_ref[...] = xb
        for o_ref, a, b in zip(outs, offs[:-1], offs[1:]):
            o_ref[...] = _dot_nt(xb, w_ref[a:b, :]).astype(o_ref.dtype)

    row = lambda n: pl.BlockSpec((tm, n), lambda i: (i, 0))
    return pl.pallas_call(
        body, name="in_proj", grid=(T // tm,),
        in_specs=[row(D_MODEL), pl.BlockSpec((IN_TOTAL, D_MODEL), lambda i: (0, 0))],
        out_specs=[row(D_MODEL)] + [row(n) for n in IN_WIDTHS],
        out_shape=[jax.ShapeDtypeStruct((T, D_MODEL), BF16)] + [jax.ShapeDtypeStruct((T, n), dt) for n, dt in zip(IN_WIDTHS, _IN_DTYPES)],
        compiler_params=_params(("parallel",)),
    )(x, w_in_t)


def _swa_specs(T):
    blk = WINDOW
    cur = lambda n: pl.BlockSpec((blk, n), lambda i: (i, 0))
    prev = lambda n: pl.BlockSpec((blk, n), lambda i: (jnp.maximum(i - 1, 0), 0))
    return blk, cur, prev


SWA_GROUP = SWA_Q_HEADS // SWA_KV_HEADS


def _swa_stack(pairs):
    lane = _lane_iota(pairs[0].shape)
    zero = jnp.zeros((), pairs[0].dtype)
    rows = []
    for h in range(SWA_Q_HEADS):
        hh, g = h % 2, h // SWA_GROUP
        x = jnp.where((lane >= hh * HEAD_DIM) & (lane < (hh + 1) * HEAD_DIM), pairs[h // 2], zero)
        rows.append(x if hh == g else pltpu.roll(x, HEAD_DIM, axis=1))
    return jnp.concatenate(rows, axis=0)


def _swa_unstack(stacked, blk):
    low = _lane_iota((blk, LANES)) < HEAD_DIM
    pairs = []
    for pp in range(SWA_Q_HEADS // 2):
        halves = []
        for hh in range(2):
            h = 2 * pp + hh
            x = stacked[h * blk:(h + 1) * blk]
            halves.append(x if hh == h // SWA_GROUP else pltpu.roll(x, HEAD_DIM, axis=1))
        pairs.append(jnp.where(low, halves[0], halves[1]))
    return pairs


def _swa_probs(i, q_stack, kwin, sink_ref, blk):
    r = lax.broadcasted_iota(jnp.int32, (blk, 2 * blk), 0)
    c = lax.broadcasted_iota(jnp.int32, (blk, 2 * blk), 1)
    rel = blk + r - c
    valid = (rel >= 0) & (rel < WINDOW) & ((c >= blk) | (i > 0))
    bias = jnp.concatenate([jnp.where(valid, 0.0, -1e30)] * SWA_Q_HEADS, axis=0)
    head = lax.broadcasted_iota(jnp.int32, (SWA_Q_HEADS * blk, 1), 0) // blk
    sink = jnp.zeros((SWA_Q_HEADS * blk, 1), F32)
    for h in range(SWA_Q_HEADS):
        sink = jnp.where(head == h, sink_ref[h], sink)
    s = _dot_nt(q_stack, kwin) * ATTN_SCALE + bias
    m = jnp.maximum(jnp.max(s, axis=1, keepdims=True), sink)
    p = jnp.exp(s - m)
    es = jnp.exp(sink - m)
    den = jnp.sum(p, axis=1, keepdims=True) + es
    return p / den, es / den


def _swa_inputs(q_ref, kp_ref, kc_ref, vp_ref, vc_ref, cp_ref, cc_ref, sp_ref, sc_ref):
    cc, sc = cc_ref[...], sc_ref[...]
    kwin = jnp.concatenate([_rope(kp_ref[...], cp_ref[...], sp_ref[...]), _rope(kc_ref[...], cc, sc)], axis=0).astype(BF16)
    vwin = jnp.concatenate([vp_ref[...], vc_ref[...]], axis=0)
    q_stack = _swa_stack([_rope(q_ref[:, pp * LANES:(pp + 1) * LANES], cc, sc) for pp in range(SWA_Q_HEADS // 2)]).astype(BF16)
    return q_stack, kwin, vwin


def _swa_fwd(qa, ka, va, cos, sin, sinks):
    T = qa.shape[0]
    blk, cur, prev = _swa_specs(T)

    def body(sink_ref, q_ref, kp_ref, kc_ref, vp_ref, vc_ref, cp_ref, cc_ref, sp_ref, sc_ref, o_ref):
        q_stack, kwin, vwin = _swa_inputs(q_ref, kp_ref, kc_ref, vp_ref, vc_ref, cp_ref, cc_ref, sp_ref, sc_ref)
        probs, _ = _swa_probs(pl.program_id(0), q_stack, kwin, sink_ref, blk)
        for pp, tile in enumerate(_swa_unstack(_dot(probs.astype(BF16), vwin), blk)):
            o_ref[:, pp * LANES:(pp + 1) * LANES] = tile.astype(o_ref.dtype)

    return pl.pallas_call(
        body, name="swa_fwd", grid=(T // blk,),
        in_specs=[pl.BlockSpec(memory_space=pltpu.SMEM), cur(SWA_Q_WIDTH), prev(LANES), cur(LANES), prev(LANES), cur(LANES),
                  prev(LANES), cur(LANES), prev(LANES), cur(LANES)],
        out_specs=cur(SWA_Q_WIDTH),
        out_shape=jax.ShapeDtypeStruct((T, SWA_Q_WIDTH), BF16),
        compiler_params=_params(("parallel",)),
    )(sinks, qa, ka, ka, va, va, cos, cos, sin, sin)


def _swa_bwd(qa, ka, va, cos, sin, sinks, dya):
    T = qa.shape[0]
    blk, cur, prev = _swa_specs(T)
    full = lambda n: pl.BlockSpec((T, n), lambda i: (0, 0))

    def body(sink_ref, q_ref, kp_ref, kc_ref, vp_ref, vc_ref, cp_ref, cc_ref, sp_ref, sc_ref, do_ref,
             dq_ref, dk_out, dv_out, dsink_ref, dk_ref, dv_ref):
        i = pl.program_id(0)

        @pl.when(i == 0)
        def _():
            dk_ref[...] = jnp.zeros_like(dk_ref)
            dv_ref[...] = jnp.zeros_like(dv_ref)
            dsink_ref[...] = jnp.zeros_like(dsink_ref)

        cp, cc, sp, sc = cp_ref[...], cc_ref[...], sp_ref[...], sc_ref[...]
        q_stack, kwin, vwin = _swa_inputs(q_ref, kp_ref, kc_ref, vp_ref, vc_ref, cp_ref, cc_ref, sp_ref, sc_ref)
        probs, psink = _swa_probs(i, q_stack, kwin, sink_ref, blk)
        do_stack = _swa_stack([do_ref[:, pp * LANES:(pp + 1) * LANES] for pp in range(SWA_Q_HEADS // 2)])
        dp = _dot_nt(do_stack, vwin)
        dsum = jnp.sum(probs * dp, axis=1, keepdims=True)
        ds = (probs * (dp - dsum) * ATTN_SCALE).astype(BF16)
        for pp, tile in enumerate(_swa_unstack(_dot(ds, kwin), blk)):
            dq_ref[:, pp * LANES:(pp + 1) * LANES] = _rope_transpose(tile, cc, sc).astype(dq_ref.dtype)
        dkw = _dot_tn(ds, q_stack)
        dvw = _dot_tn(probs.astype(BF16), do_stack)
        lane1 = _lane_iota((1, LANES))
        sink_share = psink * dsum
        dsink = jnp.zeros((1, LANES), F32)
        for h in range(SWA_Q_HEADS):
            dsink = dsink + jnp.where(lane1 == h, -jnp.sum(sink_share[h * blk:(h + 1) * blk]), 0.0)
        dsink_ref[...] += dsink
        ip = jnp.maximum(i - 1, 0)
        rows_p = pl.ds(pl.multiple_of(ip * blk, blk), blk)
        rows_c = pl.ds(pl.multiple_of(i * blk, blk), blk)
        dk_ref[rows_p, :] += _rope_transpose(dkw[:blk], cp, sp)
        dv_ref[rows_p, :] += dvw[:blk]
        dk_ref[rows_c, :] += _rope_transpose(dkw[blk:], cc, sc)
        dv_ref[rows_c, :] += dvw[blk:]

        @pl.when(i == T // blk - 1)
        def _():
            dk_out[...] = dk_ref[...].astype(BF16)
            dv_out[...] = dv_ref[...].astype(BF16)

    return pl.pallas_call(
        body, name="swa_bwd", grid=(T // blk,),
        in_specs=[pl.BlockSpec(memory_space=pltpu.SMEM), cur(SWA_Q_WIDTH), prev(LANES), cur(LANES), prev(LANES), cur(LANES),
                  prev(LANES), cur(LANES), prev(LANES), cur(LANES), cur(SWA_Q_WIDTH)],
        out_specs=[cur(SWA_Q_WIDTH), full(LANES), full(LANES), pl.BlockSpec((1, LANES), lambda i: (0, 0))],
        out_shape=[jax.ShapeDtypeStruct((T, SWA_Q_WIDTH), BF16), jax.ShapeDtypeStruct((T, LANES), BF16),
                   jax.ShapeDtypeStruct((T, LANES), BF16), jax.ShapeDtypeStruct((1, LANES), F32)],
        scratch_shapes=[pltpu.VMEM((T, LANES), F32)] * 2,
        compiler_params=_params(("arbitrary",)),
    )(sinks, qa, ka, ka, va, va, cos, cos, sin, sin, dya)


class _Exchange:
    FLIPS = [(fx, fy, fc) for fx in (0, 1) for fy in (0, 1) for fc in (0, 1) if (fx, fy, fc) != (0, 0, 0)]

    def __init__(self, arrays, kinds):
        self.arrays, self.kinds, self.n = list(arrays), list(kinds), len(arrays)

    def out_shape(self):
        return [jax.ShapeDtypeStruct(a.shape if k == "scatter" else (N_DEV,) + a.shape, a.dtype) for a, k in zip(self.arrays, self.kinds)]

    def scratch(self):
        return [pltpu.SemaphoreType.DMA((self.n, 7)), pltpu.SemaphoreType.DMA((self.n, 7)), pltpu.SemaphoreType.DMA((self.n,))]

    def bind(self, ins, outs, send_sems, recv_sems, local_sems):
        x, y, c = lax.axis_index("x"), lax.axis_index("y"), lax.axis_index("c")
        me = 4 * x + 2 * y + c
        local, remote = [], []
        for a, kind in enumerate(self.kinds):
            mine = ins[a].at[me] if kind == "scatter" else ins[a]
            local.append(pltpu.make_async_copy(mine, outs[a].at[me], local_sems.at[a]))
            for k, (fx, fy, fc) in enumerate(self.FLIPS):
                peer = (x ^ fx, y ^ fy, c ^ fc)
                peer_slot = 4 * peer[0] + 2 * peer[1] + peer[2]
                src = ins[a].at[peer_slot] if kind == "scatter" else ins[a]
                sems = dict(send_sem=send_sems.at[a, k], recv_sem=recv_sems.at[a, k], device_id=peer, device_id_type=MESH)
                remote.append((pltpu.make_async_remote_copy(src_ref=src, dst_ref=outs[a].at[me], **sems),
                               pltpu.make_async_remote_copy(src_ref=src, dst_ref=outs[a].at[peer_slot], **sems)))

        def start():
            for cp in local:
                cp.start()
            for send, _ in remote:
                send.start()

        def wait():
            for send, arrival in remote:
                arrival.wait_recv()
                send.wait_send()
            for cp in local:
                cp.wait()

        return start, wait


def _hosted_call(body, name, grid, exchange, *, in_specs, out_specs, out_shape, semantics, args, scratch=(), prefetch=()):
    n = 0 if exchange is None else exchange.n
    n_pre, n_in, n_out, n_scratch = len(prefetch), len(in_specs), len(out_specs), len(scratch)

    def hosted(*refs):
        pre, rest = refs[:n_pre], refs[n_pre:]
        ins, rest = rest[:n_in], rest[n_in:]
        ex_ins, rest = rest[:n], rest[n:]
        outs, rest = rest[:n_out], rest[n_out:]
        ex_outs, rest = rest[:n], rest[n:]
        own, sems = rest[:n_scratch], rest[n_scratch:]
        if exchange is None:
            return body(*pre, *ins, *outs, *own)
        start, wait = exchange.bind(ex_ins, ex_outs, *sems)
        ids = [pl.program_id(d) for d in range(len(grid))]
        first = functools.reduce(jnp.logical_and, [i == 0 for i in ids])
        last = functools.reduce(jnp.logical_and, [i == g - 1 for i, g in zip(ids, grid)])
        pl.when(first)(start)
        body(*pre, *ins, *outs, *own)
        pl.when(last)(wait)

    grid_spec = pltpu.PrefetchScalarGridSpec(
        num_scalar_prefetch=n_pre, grid=grid, in_specs=list(in_specs) + [ANY] * n, out_specs=list(out_specs) + [ANY] * n,
        scratch_shapes=list(scratch) + ([] if exchange is None else exchange.scratch()))
    res = pl.pallas_call(
        hosted, name=name, grid_spec=grid_spec, out_shape=list(out_shape) + ([] if exchange is None else exchange.out_shape()),
        compiler_params=_params(semantics if exchange is None else ("arbitrary",) * len(grid)),
    )(*prefetch, *args, *([] if exchange is None else exchange.arrays))
    return res[:n_out], res[n_out:]


SOFTPLUS_LINEAR_FROM = 30.0


def _sb_scores(qm, k, valid):
    z = _dot_nt(qm, k)
    sp = jnp.where(z > SOFTPLUS_LINEAR_FROM, z, jnp.log(1.0 + jnp.exp(z)))
    log_beta = z - sp
    if valid is not None:
        sp = jnp.where(valid, sp, 0.0)
    return sp, log_beta


def _tri2(B, cmp):
    r = lax.broadcasted_iota(jnp.int32, (2 * B, B), 0) % B
    c = lax.broadcasted_iota(jnp.int32, (2 * B, B), 1)
    return cmp(r, c).astype(BF16)


def _tri_sum(v, tri2):
    hi, lo = _split_bf16(v)
    return _dot(jnp.concatenate([hi, lo], axis=1), tri2)


def _head_masks(x):
    low = _lane_iota(x.shape) < HEAD_DIM
    zero = jnp.zeros((), x.dtype)
    return jnp.where(low, x, zero), jnp.where(low, zero, x)


def _strictly_below(B):
    r = lax.broadcasted_iota(jnp.int32, (B, B), 0)
    c = lax.broadcasted_iota(jnp.int32, (B, B), 1)
    return c < r


def _sb_grid(T, descending):
    B = min(SB_BLOCK, T)
    n = T // B
    pairs = [(i, j) for i in range(n) for j in (range(i, -1, -1) if descending else range(i + 1))]
    return B, jnp.asarray([p[0] for p in pairs], jnp.int32), jnp.asarray([p[1] for p in pairs], jnp.int32)


N_PAIRS = SB_HEADS // 2
PAIR_COLS = [slice(p * LANES, (p + 1) * LANES) for p in range(N_PAIRS)]


def _sb_fwd(qb, kb, vb, exchange=None):
    T = qb.shape[0]
    B, i_tab, j_tab = _sb_grid(T, descending=True)
    n = T // B

    def body(i_ref, j_ref, q_ref, k_ref, v_ref, o_ref, a_ref, b_ref, acc_ref, c_ref, tri_ref):
        s = pl.program_id(0)
        i, j = i_ref[s], j_ref[s]

        @pl.when(s == 0)
        def _():
            tri_ref[...] = _tri2(B, lambda r, c: r > c)

        @pl.when(j == i)
        def _():
            acc_ref[...] = jnp.zeros_like(acc_ref)
            c_ref[...] = jnp.zeros_like(c_ref)

        def block(valid):
            for p, cols in enumerate(PAIR_COLS):
                qms = _head_masks(q_ref[:, cols] * ATTN_SCALE)
                k = k_ref[:, cols]
                probs = []
                for hh in range(2):
                    h = 2 * p + hh
                    sp, lb = _sb_scores(qms[hh], k, valid)
                    c = c_ref[h]
                    a = jnp.exp(lb - (c + _tri_sum(sp, tri_ref[...])))
                    beta = jnp.exp(lb)
                    if valid is not None:
                        a = jnp.where(valid, a, 0.0)
                        beta = jnp.where(valid, beta, 0.0)
                    probs.append(a.astype(BF16))
                    a_ref[h] = probs[-1]
                    b_ref[h] = beta.astype(BF16)
                    c_ref[h] = c + jnp.sum(sp, axis=1, keepdims=True)
                acc_ref[:, cols] += _dot(jnp.concatenate(probs, axis=1), jnp.concatenate(_head_masks(v_ref[:, cols]), axis=0))

        pl.when(j == i)(lambda: block(_strictly_below(B)))
        pl.when(j != i)(lambda: block(None))

        @pl.when(j == 0)
        def _():
            o_ref[...] = acc_ref[...].astype(o_ref.dtype)

    q_spec = pl.BlockSpec((B, SB_WIDTH), lambda s, i_ref, j_ref: (i_ref[s], 0))
    k_spec = pl.BlockSpec((B, SB_WIDTH), lambda s, i_ref, j_ref: (j_ref[s], 0))
    tile = pl.BlockSpec((None, None, SB_HEADS, B, B), lambda s, i_ref, j_ref: (i_ref[s], j_ref[s], 0, 0, 0))
    saved = jax.ShapeDtypeStruct((n, n, SB_HEADS, B, B), BF16)
    return _hosted_call(
        body, "sb_fwd", (int(i_tab.shape[0]),), exchange, prefetch=(i_tab, j_tab),
        in_specs=[q_spec, k_spec, k_spec], out_specs=[q_spec, tile, tile],
        out_shape=[jax.ShapeDtypeStruct((T, SB_WIDTH), BF16), saved, saved],
        scratch=[pltpu.VMEM((B, SB_WIDTH), F32), pltpu.VMEM((SB_HEADS, B, 1), F32), pltpu.VMEM((2 * B, B), BF16)],
        semantics=("arbitrary",), args=(qb, kb, vb))


def _sb_bwd(qb, kb, vb, probs, betas, dyb, exchange=None):
    T = qb.shape[0]
    B, i_tab, j_tab = _sb_grid(T, descending=False)
    n_steps = int(i_tab.shape[0])

    def block_diag_t(x):
        xt = x.T
        top = lax.broadcasted_iota(jnp.int32, xt.shape, 0) < HEAD_DIM
        zero = jnp.zeros((), x.dtype)
        return jnp.concatenate([jnp.where(top, xt, zero), jnp.where(top, zero, xt)], axis=1)

    def body(i_ref, j_ref, q_ref, k_ref, v_ref, a_ref, b_ref, do_ref, dq_ref, dk_out, dv_out,
             dq_acc, cg_ref, dkt_ref, dvt_ref, tri_ref, qt_ref, dot_ref):
        s = pl.program_id(0)
        i, j = i_ref[s], j_ref[s]

        @pl.when(s == 0)
        def _():
            dkt_ref[...] = jnp.zeros_like(dkt_ref)
            dvt_ref[...] = jnp.zeros_like(dvt_ref)
            tri_ref[...] = _tri2(B, lambda r, c: r < c)[:B]

        @pl.when(j == 0)
        def _():
            dq_acc[...] = jnp.zeros_like(dq_acc)
            cg_ref[...] = jnp.zeros_like(cg_ref)
            for p, cols in enumerate(PAIR_COLS):
                qt_ref[p] = block_diag_t(q_ref[:, cols] * ATTN_SCALE)
                dot_ref[p] = block_diag_t(do_ref[:, cols])

        for p, cols in enumerate(PAIR_COLS):
            doms = _head_masks(do_ref[:, cols])
            k, v = k_ref[:, cols], v_ref[:, cols]
            dzs = []
            for hh in range(2):
                h = 2 * p + hh
                g = a_ref[h].astype(F32) * _dot_nt(doms[hh], v)
                cg = cg_ref[h]
                gsum = g + (cg + _dot(g.astype(BF16), tri_ref[...]))
                dzs.append((g - b_ref[h].astype(F32) * gsum).astype(BF16))
                cg_ref[h] = cg + jnp.sum(g, axis=1, keepdims=True)
            dq_acc[:, cols] += _dot(jnp.concatenate(dzs, axis=1), jnp.concatenate(_head_masks(k), axis=0))
            dkt_ref[j, cols, :] += _dot(qt_ref[p], jnp.concatenate(dzs, axis=0))
            dvt_ref[j, cols, :] += _dot(dot_ref[p], jnp.concatenate([a_ref[2 * p], a_ref[2 * p + 1]], axis=0))

        @pl.when(j == i)
        def _():
            dq_ref[...] = (dq_acc[...] * ATTN_SCALE).astype(dq_ref.dtype)

        @pl.when(s == n_steps - 1)
        def _():
            for jb in range(T // B):
                dk_out[jb * B:(jb + 1) * B, :] = dkt_ref[jb].T.astype(BF16)
                dv_out[jb * B:(jb + 1) * B, :] = dvt_ref[jb].T.astype(BF16)

    q_spec = pl.BlockSpec((B, SB_WIDTH), lambda s, i_ref, j_ref: (i_ref[s], 0))
    k_spec = pl.BlockSpec((B, SB_WIDTH), lambda s, i_ref, j_ref: (j_ref[s], 0))
    tile = pl.BlockSpec((None, None, SB_HEADS, B, B), lambda s, i_ref, j_ref: (i_ref[s], j_ref[s], 0, 0, 0))
    full = pl.BlockSpec((T, SB_WIDTH), lambda s, i_ref, j_ref: (0, 0))
    return _hosted_call(
        body, "sb_bwd", (n_steps,), exchange, prefetch=(i_tab, j_tab),
        in_specs=[q_spec, k_spec, k_spec, tile, tile, q_spec], out_specs=[q_spec, full, full],
        out_shape=[jax.ShapeDtypeStruct((T, SB_WIDTH), BF16)] * 3,
        scratch=[pltpu.VMEM((B, SB_WIDTH), F32), pltpu.VMEM((SB_HEADS, B, 1), F32), pltpu.VMEM((T // B, SB_WIDTH, B), F32),
                 pltpu.VMEM((T // B, SB_WIDTH, B), F32), pltpu.VMEM((B, B), BF16), pltpu.VMEM((N_PAIRS, LANES, 2 * B), BF16),
                 pltpu.VMEM((N_PAIRS, LANES, 2 * B), BF16)],
        semantics=("arbitrary",), args=(qb, kb, vb, probs, betas, dyb))


def _ln_stats(u):
    mu = jnp.mean(u, axis=-1, keepdims=True)
    xc = u - mu
    var = jnp.mean(xc * xc, axis=-1, keepdims=True)
    rstd = lax.rsqrt(var + LN_EPS)
    return xc * rstd, rstd


def _ln_bwd(dy, xhat, rstd, g):
    dxh = dy * g
    return rstd * (dxh - jnp.mean(dxh, axis=-1, keepdims=True) - xhat * jnp.mean(dxh * xhat, axis=-1, keepdims=True))


def _gates(gl_ref, bg_ref):
    ga = jax.nn.sigmoid(gl_ref[:, :D_MODEL] + bg_ref[:, :D_MODEL])
    gb = jax.nn.sigmoid(gl_ref[:, D_MODEL:] + bg_ref[:, D_MODEL:])
    return ga, gb


def _mix_fwd(ya, yb, gl, x, wa, wb, wo, b_gate, ln1_g, ln1_b):
    T = x.shape[0]
    tm = min(512, T)

    def body(ya_ref, yb_ref, gl_ref, x_ref, wa_ref, wb_ref, wo_ref, bg_ref, g_ref, b_ref, h_ref, u_ref, x1_ref):
        ga, gb = _gates(gl_ref, bg_ref)
        h = (ga * _dot(ya_ref[...], wa_ref[...]) + gb * _dot(yb_ref[...], wb_ref[...])).astype(BF16)
        h_ref[...] = h
        u = ALPHA * x_ref[...] + _dot(h, wo_ref[...])
        u_ref[...] = u
        xhat, _ = _ln_stats(u)
        x1_ref[...] = (xhat * g_ref[...] + b_ref[...]).astype(BF16)

    row = lambda n: pl.BlockSpec((tm, n), lambda i: (i, 0))
    const = lambda r, n: pl.BlockSpec((r, n), lambda i: (0, 0))
    return pl.pallas_call(
        body, name="mix_fwd", grid=(T // tm,),
        in_specs=[row(SWA_Q_WIDTH), row(SB_WIDTH), row(GATE_WIDTH), row(D_MODEL), const(SWA_Q_WIDTH, D_MODEL), const(SB_WIDTH, D_MODEL),
                  const(D_MODEL, D_MODEL), const(1, GATE_WIDTH), const(1, D_MODEL), const(1, D_MODEL)],
        out_specs=[row(D_MODEL)] * 3,
        out_shape=[jax.ShapeDtypeStruct((T, D_MODEL), BF16), jax.ShapeDtypeStruct((T, D_MODEL), F32), jax.ShapeDtypeStruct((T, D_MODEL), BF16)],
        compiler_params=_params(("parallel",)),
    )(ya, yb, gl, x, wa, wb, wo, b_gate, ln1_g, ln1_b)


def _mix_bwd(du1, ya, yb, gl, wa, wb, wo, b_gate):
    T = du1.shape[0]
    tm = min(512, T)

    def body(du_ref, ya_ref, yb_ref, gl_ref, wa_ref, wb_ref, wo_ref, bg_ref, dya_ref, dyb_ref, dgl_ref, dta_ref, dtb_ref, dbg_ref):
        @pl.when(pl.program_id(0) == 0)
        def _():
            dbg_ref[...] = jnp.zeros_like(dbg_ref)

        dh = _dot_nt(du_ref[...].astype(BF16), wo_ref[...])
        ga, gb = _gates(gl_ref, bg_ref)
        for gate, y_ref, w_ref, dy_ref, dt_ref, lo in ((ga, ya_ref, wa_ref, dya_ref, dta_ref, 0), (gb, yb_ref, wb_ref, dyb_ref, dtb_ref, D_MODEL)):
            t = _dot(y_ref[...], w_ref[...])
            dlogit = dh * t * gate * (1.0 - gate)
            dgl_ref[:, lo:lo + D_MODEL] = dlogit.astype(BF16)
            dbg_ref[:, lo:lo + D_MODEL] += jnp.sum(dlogit, axis=0, keepdims=True)
            dt = (dh * gate).astype(BF16)
            dt_ref[...] = dt
            dy_ref[...] = _dot_nt(dt, w_ref[...]).astype(BF16)

    row = lambda n: pl.BlockSpec((tm, n), lambda i: (i, 0))
    const = lambda r, n: pl.BlockSpec((r, n), lambda i: (0, 0))
    sds = lambda n, dt: jax.ShapeDtypeStruct((T, n), dt)
    return pl.pallas_call(
        body, name="mix_bwd", grid=(T // tm,),
        in_specs=[row(D_MODEL), row(SWA_Q_WIDTH), row(SB_WIDTH), row(GATE_WIDTH), const(SWA_Q_WIDTH, D_MODEL), const(SB_WIDTH, D_MODEL),
                  const(D_MODEL, D_MODEL), const(1, GATE_WIDTH)],
        out_specs=[row(SWA_Q_WIDTH), row(SB_WIDTH), row(GATE_WIDTH), row(D_MODEL), row(D_MODEL), const(1, GATE_WIDTH)],
        out_shape=[sds(SWA_Q_WIDTH, BF16), sds(SB_WIDTH, BF16), sds(GATE_WIDTH, BF16), sds(D_MODEL, BF16), sds(D_MODEL, BF16),
                   jax.ShapeDtypeStruct((1, GATE_WIDTH), F32)],
        compiler_params=_params(("arbitrary",)),
    )(du1, ya, yb, gl, wa, wb, wo, b_gate)


CONV_COLS = LANES


CONV_CHUNK = 64
CONV_CHUNK_FWD = 256
HALO = 8


def _taps(ref, r0, rows, lead):
    return [ref[pl.ds(r0 + lead + k, rows), :] for k in ((-2, -1, 0) if lead else (0, 1, 2))]


def _chunks(T, rows, step, init=None, reverse=False):
    def body(c, carry):
        c = T // rows - 1 - c if reverse else c
        out = step(pl.multiple_of(c * rows, rows), *(() if init is None else (carry,)))
        return carry if init is None else out
    return lax.fori_loop(0, T // rows, body, 0 if init is None else init)


def _conv_chunk(taps, w_ref, b_ref):
    return w_ref[0:1, :] * taps[0] + w_ref[1:2, :] * taps[1] + w_ref[2:3, :] * taps[2] + b_ref[...]


def _fold(x):
    return jnp.sum(x.reshape(x.shape[0] // 8, 8, x.shape[1]), axis=0)


def _conv_specs(T):
    nb = D_FF // CONV_COLS
    pair = pl.BlockSpec((2, T, CONV_COLS), lambda j: (0, 0, j))
    gate = lambda r: pl.BlockSpec((r, CONV_COLS), lambda j: (0, j))
    up = lambda r: pl.BlockSpec((r, CONV_COLS), lambda j: (0, j + nb))
    return nb, pair, gate, up


def _conv_glu_fwd(p3, conv_w, conv_b):
    T = p3.shape[1]
    nb, pair, gate, up = _conv_specs(T)

    R = min(CONV_CHUNK_FWD, T)

    def body(p_ref, wg_ref, wu_ref, bg_ref, bu_ref, s_ref, pg_s, pu_s):
        for half, scr in enumerate((pg_s, pu_s)):
            scr[0:HALO, :] = jnp.zeros((HALO, CONV_COLS), F32)
            scr[HALO:HALO + T, :] = p_ref[half].astype(F32)
        def step(r0):
            ag = _conv_chunk(_taps(pg_s, r0, R, HALO), wg_ref, bg_ref)
            au = _conv_chunk(_taps(pu_s, r0, R, HALO), wu_ref, bu_ref)
            s_ref[pl.ds(r0, R), :] = (ag * jax.nn.sigmoid(ag) * au).astype(BF16)

        _chunks(T, R, step)

    return pl.pallas_call(
        body, name="conv_glu_fwd", grid=(nb,),
        in_specs=[pair, gate(3), up(3), gate(1), up(1)],
        out_specs=pl.BlockSpec((T, CONV_COLS), lambda j: (0, j)),
        out_shape=jax.ShapeDtypeStruct((T, D_FF), BF16),
        scratch_shapes=[pltpu.VMEM((T + HALO, CONV_COLS), F32)] * 2,
        compiler_params=_params(("parallel",)),
    )(p3, conv_w, conv_w, conv_b, conv_b)


def _conv_glu_bwd(p3, ds, conv_w, conv_b):
    T = p3.shape[1]
    nb, pair, gate, up = _conv_specs(T)

    R = min(CONV_CHUNK, T)

    def body(p_ref, ds_ref, wg_ref, wu_ref, bg_ref, bu_ref, dp_ref, dwg_ref, dwu_ref, dbg_ref, dbu_ref, pg_s, pu_s, dag_s, dau_s):
        for half, scr in enumerate((pg_s, pu_s)):
            scr[0:HALO, :] = jnp.zeros((HALO, CONV_COLS), F32)
            scr[HALO:HALO + T, :] = p_ref[half].astype(F32)
        for scr in (dag_s, dau_s):
            scr[T:T + HALO, :] = jnp.zeros((HALO, CONV_COLS), F32)
        halves = ((pg_s, dag_s, wg_ref, dwg_ref, dbg_ref), (pu_s, dau_s, wu_ref, dwu_ref, dbu_ref))

        def step(r0, sums):
            taps = [_taps(p_s, r0, R, HALO) for p_s, *_ in halves]
            ag = _conv_chunk(taps[0], wg_ref, bg_ref)
            au = _conv_chunk(taps[1], wu_ref, bu_ref)
            sg = jax.nn.sigmoid(ag)
            d = ds_ref[pl.ds(r0, R), :].astype(F32)
            das = (d * au * (sg * (1.0 + ag * (1.0 - sg))), d * ag * sg)
            out = []
            for half, (_, da_s, w_ref, *_) in enumerate(halves):
                da_s[pl.ds(r0, R), :] = das[half]
                out.append(tuple(sums[half][k] + _fold(das[half] * taps[half][k]) for k in range(3)) + (sums[half][3] + _fold(das[half]),))
                _, da1, da2 = _taps(da_s, r0, R, 0)
                dp_ref[half, pl.ds(r0, R), :] = (w_ref[2:3, :] * das[half] + w_ref[1:2, :] * da1 + w_ref[0:1, :] * da2).astype(BF16)
            return tuple(out)

        sums = _chunks(T, R, step, ((jnp.zeros((8, CONV_COLS), F32),) * 4,) * 2, reverse=True)
        for half, (_, da_s, w_ref, dw_ref, db_ref) in enumerate(halves):
            for k in range(3):
                dw_ref[k:k + 1, :] = jnp.sum(sums[half][k], axis=0, keepdims=True)
            db_ref[...] = jnp.sum(sums[half][3], axis=0, keepdims=True)

    col = lambda r: pl.BlockSpec((r, CONV_COLS), lambda j: (0, j))
    return pl.pallas_call(
        body, name="conv_glu_bwd", grid=(nb,),
        in_specs=[pair, col(T), gate(3), up(3), gate(1), up(1)],
        out_specs=[pair, col(3), col(3), col(1), col(1)],
        out_shape=[jax.ShapeDtypeStruct((2, T, D_FF), BF16), jax.ShapeDtypeStruct((3, D_FF), F32), jax.ShapeDtypeStruct((3, D_FF), F32),
                   jax.ShapeDtypeStruct((1, D_FF), F32), jax.ShapeDtypeStruct((1, D_FF), F32)],
        scratch_shapes=[pltpu.VMEM((T + HALO, CONV_COLS), F32)] * 4,
        compiler_params=_params(("parallel",)),
    )(p3, ds, conv_w, conv_w, conv_b, conv_b)


def _ffn_down_loss(s, w_down, u1, ln1_g, ln1_b, ln2_g, ln2_b, target):
    T = u1.shape[0]
    tm = min(512, T)

    def body(s_ref, w_ref, u1_ref, g1_ref, b1_ref, g2_ref, b2_ref, t_ref, du_ref, dub_ref, dg_ref, db_ref, loss_ref):
        @pl.when(pl.program_id(0) == 0)
        def _():
            dg_ref[...] = jnp.zeros_like(dg_ref)
            db_ref[...] = jnp.zeros_like(db_ref)
            loss_ref[...] = jnp.zeros_like(loss_ref)

        xh1, _ = _ln_stats(u1_ref[...])
        x1 = xh1 * g1_ref[...] + b1_ref[...]
        u2 = ALPHA * x1 + _dot(s_ref[...], w_ref[...])
        xh2, rstd2 = _ln_stats(u2)
        err = xh2 * g2_ref[...] + b2_ref[...] - t_ref[...]
        per_token = jnp.mean(err * err, axis=-1, keepdims=True)
        loss_ref[...] += 0.5 * jnp.sum(per_token, axis=0, keepdims=True)
        dy = err * (1.0 / D_MODEL)
        dg_ref[...] += jnp.sum(dy * xh2, axis=0, keepdims=True)
        db_ref[...] += jnp.sum(dy, axis=0, keepdims=True)
        du2 = _ln_bwd(dy, xh2, rstd2, g2_ref[...])
        du_ref[...] = du2
        dub_ref[...] = du2.astype(BF16)

    row = lambda n: pl.BlockSpec((tm, n), lambda i: (i, 0))
    const = lambda r, n: pl.BlockSpec((r, n), lambda i: (0, 0))
    vec = const(1, D_MODEL)
    return pl.pallas_call(
        body, name="ffn_down_loss", grid=(T // tm,),
        in_specs=[row(D_FF), const(D_FF, D_MODEL), row(D_MODEL), vec, vec, vec, vec, row(D_MODEL)],
        out_specs=[row(D_MODEL), row(D_MODEL), vec, vec, const(1, LANES)],
        out_shape=[jax.ShapeDtypeStruct((T, D_MODEL), F32), jax.ShapeDtypeStruct((T, D_MODEL), BF16), jax.ShapeDtypeStruct((1, D_MODEL), F32),
                   jax.ShapeDtypeStruct((1, D_MODEL), F32), jax.ShapeDtypeStruct((1, LANES), F32)],
        compiler_params=_params(("arbitrary",)),
    )(s, w_down, u1, ln1_g, ln1_b, ln2_g, ln2_b, target)


def _ffn_up_bwd_ln1(dp3, w_up, du2, u1, ln1_g):
    T = u1.shape[0]
    tm = min(512, T)

    def body(dp_ref, w_ref, du2_ref, u1_ref, g_ref, du_ref, dub_ref, dg_ref, db_ref):
        @pl.when(pl.program_id(0) == 0)
        def _():
            dg_ref[...] = jnp.zeros_like(dg_ref)
            db_ref[...] = jnp.zeros_like(db_ref)

        dx1 = _dot(dp_ref[0], w_ref[:D_FF, :]) + _dot(dp_ref[1], w_ref[D_FF:, :]) + ALPHA * du2_ref[...]
        xh, rstd = _ln_stats(u1_ref[...])
        dg_ref[...] += jnp.sum(dx1 * xh, axis=0, keepdims=True)
        db_ref[...] += jnp.sum(dx1, axis=0, keepdims=True)
        du1 = _ln_bwd(dx1, xh, rstd, g_ref[...])
        du_ref[...] = du1
        dub_ref[...] = du1.astype(BF16)

    row = lambda n: pl.BlockSpec((tm, n), lambda i: (i, 0))
    const = lambda r, n: pl.BlockSpec((r, n), lambda i: (0, 0))
    vec = const(1, D_MODEL)
    return pl.pallas_call(
        body, name="ffn_up_bwd_ln1", grid=(T // tm,),
        in_specs=[pl.BlockSpec((2, tm, D_FF), lambda i: (0, i, 0)), const(2 * D_FF, D_MODEL), row(D_MODEL), row(D_MODEL), vec],
        out_specs=[row(D_MODEL), row(D_MODEL), vec, vec],
        out_shape=[jax.ShapeDtypeStruct((T, D_MODEL), F32), jax.ShapeDtypeStruct((T, D_MODEL), BF16), jax.ShapeDtypeStruct((1, D_MODEL), F32),
                   jax.ShapeDtypeStruct((1, D_MODEL), F32)],
        compiler_params=_params(("arbitrary",)),
    )(dp3, w_up, du2, u1, ln1_g)


def _local_step(x, positions, w_in, b_gate, sinks, ln1_g, ln1_b, conv_b, ln2_g, ln2_b, target, later_weights,
                early_exchange=None, tail_exchange=None):
    T = x.shape[0]
    inv_freq = 1.0 / (ROPE_THETA ** (jnp.arange(0, HEAD_DIM, 2, dtype=F32) / HEAD_DIM))
    cos, sin = _rope_tables(positions.reshape(T, 1), jnp.tile(inv_freq, LANES // (HEAD_DIM // 2)).reshape(1, LANES))

    xb, qa, ka, va, qb, kb, vb, gl = _in_proj(x, w_in)
    ya = _swa_fwd(qa, ka, va, cos, sin, sinks)
    if isinstance(later_weights, tuple):
        exchange, finish = later_weights
        (yb, probs, betas), arrived = _sb_fwd(qb, kb, vb, exchange)
        later_weights = finish(arrived)
    else:
        (yb, probs, betas), _ = _sb_fwd(qb, kb, vb)
    wa, wb, wo, w_up, conv_w, w_down = later_weights
    h, u1, x1 = _mix_fwd(ya, yb, gl, x, wa, wb, wo, b_gate, ln1_g, ln1_b)

    ff_tn = D_FF // 2
    nff = D_FF // ff_tn
    tm = min(1024, T)
    p3 = _matmul(x1, w_up, kind="nt", name="ffn_up", grid=(T // tm, 2 * nff),
                 a_spec=pl.BlockSpec((tm, D_MODEL), lambda i, j: (i, 0)), b_spec=pl.BlockSpec((ff_tn, D_MODEL), lambda i, j: (j, 0)),
                 out_spec=pl.BlockSpec((None, tm, ff_tn), lambda i, j: (j // nff, i, j % nff)),
                 out_shape=jax.ShapeDtypeStruct((2, T, D_FF), ACT_DTYPE))
    s = _conv_glu_fwd(p3, conv_w, conv_b)
    du2, du2b, dln2_g, dln2_b, loss = _ffn_down_loss(s, w_down, u1, ln1_g, ln1_b, ln2_g, ln2_b, target)

    ds = _matmul(du2b, w_down, kind="nt", name="ffn_down_bwd", grid=(T // tm, nff),
                 a_spec=pl.BlockSpec((tm, D_MODEL), lambda i, j: (i, 0)), b_spec=pl.BlockSpec((ff_tn, D_MODEL), lambda i, j: (j, 0)),
                 out_spec=pl.BlockSpec((tm, ff_tn), lambda i, j: (i, j)), out_shape=jax.ShapeDtypeStruct((T, D_FF), ACT_DTYPE))
    dp3, dcw_g, dcw_u, dcb_g, dcb_u = _conv_glu_bwd(p3, ds, conv_w, conv_b)
    tk = 256
    dw_down = _matmul(s, du2b, kind="tn", name="dw_down", grid=(D_FF // tk,),
                      a_spec=pl.BlockSpec((T, tk), lambda i: (0, i)), b_spec=pl.BlockSpec((T, D_MODEL), lambda i: (0, 0)),
                      out_spec=pl.BlockSpec((tk, D_MODEL), lambda i: (i, 0)), out_shape=jax.ShapeDtypeStruct((D_FF, D_MODEL), BF16))
    dw_up = _matmul(dp3, x1, kind="tn", name="dw_up", grid=(2 * nff,),
                    a_spec=pl.BlockSpec((None, T, ff_tn), lambda j: (j // nff, 0, j % nff)), b_spec=pl.BlockSpec((T, D_MODEL), lambda j: (0, 0)),
                    out_spec=pl.BlockSpec((ff_tn, D_MODEL), lambda j: (j, 0)), out_shape=jax.ShapeDtypeStruct((2 * D_FF, D_MODEL), BF16))
    du1, du1b, dln1_g, dln1_b = _ffn_up_bwd_ln1(dp3, w_up, du2, u1, ln1_g)
    dya, dyb, dgl, dta, dtb, db_gate = _mix_bwd(du1, ya, yb, gl, wa, wb, wo, b_gate)

    def dw_tn(a, g, name):
        rows, cols = a.shape[1], g.shape[1]
        tn = min(512, cols)
        return _matmul(a, g, kind="tn", name=name, grid=(rows // 512, cols // tn),
                       a_spec=pl.BlockSpec((T, 512), lambda i, j: (0, i)), b_spec=pl.BlockSpec((T, tn), lambda i, j: (0, j)),
                       out_spec=pl.BlockSpec((512, tn), lambda i, j: (i, j)), out_shape=jax.ShapeDtypeStruct((rows, cols), BF16))

    dwa = dw_tn(ya, dta, "dw_branch_a")
    dwb = dw_tn(yb, dtb, "dw_branch_b")
    dwo = dw_tn(h, du1b, "dw_out")

    grads = dict(
        b_gate=db_gate, w_branch_a=dwa, w_branch_b=dwb, w_out=dwo, ln1_g=dln1_g, ln1_b=dln1_b,
        w_up=dw_up, conv_w=jnp.concatenate([dcw_g, dcw_u], axis=1), conv_b=(dcb_g, dcb_u), w_down=dw_down, ln2_g=dln2_g, ln2_b=dln2_b)
    (dqb, dkb, dvb), early_out = _sb_bwd(qb, kb, vb, probs, betas, dyb, early_exchange(grads) if early_exchange else None)
    dqa, dka, dva, grads["sinks"] = _swa_bwd(qa, ka, va, cos, sin, sinks, dya)
    dproj = (dqa, dka, dva, dqb, dkb, dvb, dgl)
    grads["w_in"] = _dw_in(xb, dproj)
    grad_x, tail_out = _grad_x(dproj, w_in, du1, tail_exchange(grads, loss) if tail_exchange else None)
    return loss, grad_x, grads, early_out, tail_out


def _dw_in(xb, dproj):
    T = xb.shape[0]
    tn = 2 * LANES
    groups, start, k = [], 0, 0
    while k < len(IN_WIDTHS):
        if IN_WIDTHS[k] >= tn:
            groups.append((start, IN_WIDTHS[k] // tn, [(k, 0, tn)]))
            k += 1
        else:
            members, off = [], 0
            while off < tn:
                members.append((k, off, IN_WIDTHS[k]))
                off += IN_WIDTHS[k]
                k += 1
            groups.append((start, 1, members))
        start += groups[-1][1]

    def body(x_ref, *refs):
        pieces, o_ref = refs[:-1], refs[-1]
        j = pl.program_id(0)
        for first, steps, members in groups:
            @pl.when((j >= first) & (j < first + steps))
            def _(members=members):
                for k, off, width in members:
                    o_ref[off:off + width, :] = _dot_tn(pieces[k][...], x_ref[...]).astype(o_ref.dtype)

    specs = [None] * len(IN_WIDTHS)
    for first, steps, members in groups:
        for k, _, width in members:
            specs[k] = pl.BlockSpec((T, width), lambda j, first=first, steps=steps: (0, jnp.clip(j - first, 0, steps - 1)))
    return pl.pallas_call(
        body, name="dw_in", grid=(IN_TOTAL // tn,),
        in_specs=[pl.BlockSpec((T, D_MODEL), lambda j: (0, 0))] + specs, out_specs=pl.BlockSpec((tn, D_MODEL), lambda j: (j, 0)),
        out_shape=jax.ShapeDtypeStruct((IN_TOTAL, D_MODEL), BF16), compiler_params=_params(("arbitrary",)),
    )(xb, *dproj)


def _grad_x(dproj, w_in, du1, exchange=None):
    T = du1.shape[0]
    tm = min(512, T)
    offs = np.cumsum((0,) + IN_WIDTHS)

    def body(*refs):
        pieces, (w_ref, du_ref, o_ref) = refs[:len(IN_WIDTHS)], refs[len(IN_WIDTHS):]
        acc = ALPHA * du_ref[...]
        for p_ref, a, b in zip(pieces, offs[:-1], offs[1:]):
            acc = acc + _dot(p_ref[...].astype(BF16), w_ref[a:b, :])
        o_ref[...] = acc

    row = lambda n: pl.BlockSpec((tm, n), lambda i: (i, 0))
    (grad_x,), arrived = _hosted_call(
        body, "grad_x", (T // tm,), exchange,
        in_specs=[row(n) for n in IN_WIDTHS] + [pl.BlockSpec((IN_TOTAL, D_MODEL), lambda i: (0, 0)), row(D_MODEL)],
        out_specs=[row(D_MODEL)], out_shape=[jax.ShapeDtypeStruct((T, D_MODEL), F32)], semantics=("parallel",),
        args=(*dproj, w_in, du1))
    return grad_x, arrived


ANY = pl.BlockSpec(memory_space=pl.ANY)


def _all_gather(slabs, name):
    n = len(slabs)

    def body(*refs):
        ins, outs = refs[:n], refs[n:2 * n]
        send_sems, recv_sems, local_sems = refs[2 * n:]
        x, y, c = lax.axis_index("x"), lax.axis_index("y"), lax.axis_index("c")
        me, sibling = (x, y, c), (x, y, 1 - c)
        chips = [(1 - x, y), (x, 1 - y), (1 - x, 1 - y)]

        def slot(pos):
            return 4 * pos[0] + 2 * pos[1] + pos[2]

        def copy(a, k, block, to, from_input=False):
            return pltpu.make_async_remote_copy(
                src_ref=ins[a] if from_input else outs[a].at[slot(block)], dst_ref=outs[a].at[slot(block)],
                send_sem=send_sems.at[a, k], recv_sem=recv_sems.at[a, k], device_id=to, device_id_type=MESH)

        mine = [pltpu.make_async_copy(ins[a], outs[a].at[slot(me)], local_sems.at[a]) for a in range(n)]
        for cp in mine:
            cp.start()
        first = []
        for a in range(n):
            first.append(copy(a, 0, me, sibling, from_input=True))
            first += [copy(a, 1 + j, me, (*chip, c), from_input=True) for j, chip in enumerate(chips)]
        for cp in first:
            cp.start()
        passed = []
        for j, chip in enumerate(chips):
            for a in range(n):
                copy(a, 1 + j, (*chip, c), me).wait_recv()
                fwd = copy(a, 4 + j, (*chip, c), sibling)
                fwd.start()
                passed.append(fwd)
        for a in range(n):
            copy(a, 0, sibling, me).wait_recv()
            for j, chip in enumerate(chips):
                copy(a, 4 + j, (*chip, 1 - c), me).wait_recv()
        for cp in first + passed:
            cp.wait_send()
        for cp in mine:
            cp.wait()

    return pl.pallas_call(
        body, name=name,
        in_specs=[ANY] * n, out_specs=[ANY] * n,
        out_shape=[jax.ShapeDtypeStruct((N_DEV,) + s.shape, s.dtype) for s in slabs],
        scratch_shapes=[pltpu.SemaphoreType.DMA((n, 7)), pltpu.SemaphoreType.DMA((n, 7)), pltpu.SemaphoreType.DMA((n,))],
    )(*slabs)


def _row_tile(rows):
    for cand in range(256, 7, -8):
        if rows % cand == 0:
            return cand
    return rows


def _window(w):
    wp = max(-(-((w * r) % LANES + w) // LANES) for r in range(N_DEV)) * LANES
    assert all((w * r) // LANES * LANES + wp <= N_DEV * w for r in range(N_DEV))
    return wp


def _join_cols(slabs, name):
    _, R, w = slabs.shape
    tr = _row_tile(R)
    wp = _window(w)

    def body(g_ref, o_ref, pad_ref):
        if w % LANES == 0:
            for r in range(N_DEV):
                o_ref[:, w * r:w * (r + 1)] = g_ref[r]
            return
        o_ref[...] = jnp.zeros_like(o_ref)
        pad_ref[...] = jnp.zeros_like(pad_ref)
        for r in range(N_DEV):
            q, s = divmod(w * r, LANES)
            pad_ref[:, :w] = g_ref[r]
            y = pad_ref[...]
            if s:
                y = pltpu.roll(y, s, axis=1)
            o_ref[:, LANES * q:LANES * q + wp] += y

    return pl.pallas_call(
        body, name=name, grid=(R // tr,),
        in_specs=[pl.BlockSpec((N_DEV, tr, w), lambda i: (0, i, 0))], out_specs=pl.BlockSpec((tr, N_DEV * w), lambda i: (i, 0)),
        out_shape=jax.ShapeDtypeStruct((R, N_DEV * w), slabs.dtype), scratch_shapes=[pltpu.VMEM((tr, wp), slabs.dtype)],
        compiler_params=_params(("parallel",)),
    )(slabs)


def _split_cols(pieces, name):
    R = pieces[0].shape[0]
    widths = [p.shape[1] for p in pieces]
    total = sum(widths)
    w = total // N_DEV
    tr = _row_tile(R)
    wp = _window(w)
    offs = np.cumsum([0] + widths)
    dtype = pieces[0].dtype

    def body(*refs):
        ins, (o_ref, full_ref) = refs[:len(pieces)], refs[len(pieces):]
        for p_ref, a, b in zip(ins, offs[:-1], offs[1:]):
            full_ref[:, a:b] = p_ref[...].astype(dtype)
        for r in range(N_DEV):
            q, s = divmod(w * r, LANES)
            y = full_ref[:, LANES * q:LANES * q + wp]
            if s:
                y = pltpu.roll(y, wp - s, axis=1)
            o_ref[r] = y[:, :w]

    return pl.pallas_call(
        body, name=name, grid=(R // tr,),
        in_specs=[pl.BlockSpec((tr, n), lambda i: (i, 0)) for n in widths], out_specs=pl.BlockSpec((N_DEV, tr, w), lambda i: (0, i, 0)),
        out_shape=jax.ShapeDtypeStruct((N_DEV, R, w), dtype), scratch_shapes=[pltpu.VMEM((tr, total), dtype)],
        compiler_params=_params(("parallel",)),
    )(*pieces)


def _adamw(g, w, m, v):
    m_new = ADAM_B1 * m + (1.0 - ADAM_B1) * g
    v_new = ADAM_B2 * v + (1.0 - ADAM_B2) * jnp.square(g)
    m_hat = m_new / (1.0 - ADAM_B1 ** ADAM_STEP)
    v_hat = v_new / (1.0 - ADAM_B2 ** ADAM_STEP)
    return -ADAM_LR * (m_hat / (jnp.sqrt(v_hat) + ADAM_EPS) + ADAM_WD * w), m_new, v_new


def _sum_parts(p_ref):
    g = p_ref[0].astype(F32)
    for d in range(1, N_DEV):
        g = g + p_ref[d].astype(F32)
    return g


def _reduce_adamw(parts, w, m, v, name):
    R, C = w.shape
    tr = _row_tile(R)

    def body(p_ref, w_ref, m_ref, v_ref, g_ref, d_ref, mo_ref, vo_ref):
        g = _sum_parts(p_ref)
        g_ref[...] = g
        d_ref[...], mo_ref[...], vo_ref[...] = _adamw(g, w_ref[...], m_ref[...], v_ref[...])

    row = pl.BlockSpec((tr, C), lambda i: (i, 0))
    return pl.pallas_call(
        body, name=name, grid=(R // tr,),
        in_specs=[pl.BlockSpec((N_DEV, tr, C), lambda i: (0, i, 0)), row, row, row],
        out_specs=[row] * 4, out_shape=[jax.ShapeDtypeStruct((R, C), F32)] * 4,
        compiler_params=_params(("parallel",)),
    )(parts, w, m, v)


def _reduce_adamw_small(parts, ws, ms, vs):
    sizes = [a.shape[1] for a in ws]
    k = len(sizes)
    offs = np.cumsum([0] + [-(-n // LANES) * LANES for n in sizes])

    def body(*refs):
        p_ref, w_refs, m_refs, v_refs = refs[0], refs[1:1 + k], refs[1 + k:1 + 2 * k], refs[1 + 2 * k:1 + 3 * k]
        outs, loss_ref = refs[1 + 3 * k:-1], refs[-1]
        g_all = _sum_parts(p_ref)
        for j, n in enumerate(sizes):
            g = g_all[:, offs[j]:offs[j] + LANES * (-(-n // LANES))][:, :n]
            outs[4 * j][...] = g
            outs[4 * j + 1][...], outs[4 * j + 2][...], outs[4 * j + 3][...] = _adamw(g, w_refs[j][...], m_refs[j][...], v_refs[j][...])
        loss_ref[...] = g_all[:, offs[k]:offs[k] + LANES]

    vm = pl.BlockSpec(memory_space=pltpu.VMEM)
    out_shape = [jax.ShapeDtypeStruct((1, n), F32) for n in sizes for _ in range(4)] + [jax.ShapeDtypeStruct((1, LANES), F32)]
    res = pl.pallas_call(
        body, name="reduce_adamw_replicated", in_specs=[vm] * (1 + 3 * k), out_specs=[vm] * len(out_shape), out_shape=out_shape,
        compiler_params=_params(),
    )(parts, *ws, *ms, *vs)
    return [res[4 * j:4 * j + 4] for j in range(k)], res[-1]


TRANSPOSED = ("w_in", "w_up")
COL_SHARDED = ("w_branch_a", "w_branch_b", "conv_w")
ROW_SHARDED = TRANSPOSED + ("w_out", "w_down")
SMALL = ("b_gate", "sinks", "ln1_g", "ln1_b", "conv_b", "ln2_g", "ln2_b")
ORDER = ("w_in", "b_gate", "sinks", "w_branch_a", "w_branch_b", "w_out", "ln1_g", "ln1_b", "w_up", "conv_w", "conv_b", "w_down", "ln2_g", "ln2_b")


def _pad_lanes(a):
    pad = (-a.shape[-1]) % LANES
    return a if pad == 0 else jnp.pad(a, ((0, 0), (0, pad)))


def kernel(x, positions, w_in, b_gate, sinks, w_branch_a, w_branch_b, w_out, ln1_g, ln1_b, w_up, conv_w, conv_b, w_down, ln2_g, ln2_b, loss_target, m_w_in, m_b_gate, m_sinks, m_w_branch_a, m_w_branch_b, m_w_out, m_ln1_g, m_ln1_b, m_w_up, m_conv_w, m_conv_b, m_w_down, m_ln2_g, m_ln2_b, v_w_in, v_b_gate, v_sinks, v_w_branch_a, v_w_branch_b, v_w_out, v_ln1_g, v_ln1_b, v_w_up, v_conv_w, v_conv_b, v_w_down, v_ln2_g, v_ln2_b):
    args = dict(locals())
    sharded = COL_SHARDED + ROW_SHARDED

    def shard(name, a):
        return a if name not in sharded else a[0].T if name in TRANSPOSED else a[0]

    w = {n: shard(n, args[n]) for n in ORDER}
    m = {n: shard(n, args["m_" + n]) for n in ORDER}
    v = {n: shard(n, args["v_" + n]) for n in ORDER}

    travel = {n: (w[n] if n == "conv_w" else w[n].astype(BF16)) for n in sharded}
    (g_in,) = _all_gather([travel["w_in"]], "all_gather_w_in")
    w_in_full = g_in.reshape(-1, g_in.shape[-1])
    later = ("w_branch_a", "w_branch_b", "w_out", "w_up", "conv_w", "w_down")

    def join(name, slabs):
        return _join_cols(slabs, "join_" + name) if name in COL_SHARDED else slabs.reshape(-1, slabs.shape[-1])

    def split(name, grad):
        if name in COL_SHARDED:
            return _split_cols(grad if isinstance(grad, tuple) else (grad,), "split_d" + name)
        return grad.reshape((N_DEV, -1, grad.shape[-1]))

    def early_exchange(grads):
        return _Exchange([split(n, grads[n]) for n in later], ["scatter"] * len(later))

    def tail_exchange(grads, loss):
        small_pack = jnp.concatenate(
            [_pad_lanes(p) for n in SMALL for p in (grads[n] if isinstance(grads[n], tuple) else (grads[n],))] + [loss], axis=1)
        return _Exchange([split("w_in", grads["w_in"]), small_pack], ["scatter", "gather"])

    gather_later = _Exchange([travel[n] for n in later], ["gather"] * len(later))
    _, grad_x, _, early_out, (recv_w_in, small_parts) = _local_step(
        x[0], positions[0], w_in_full, w["b_gate"], w["sinks"][0], w["ln1_g"], w["ln1_b"], w["conv_b"], w["ln2_g"], w["ln2_b"], loss_target[0],
        (gather_later, lambda arrived: [join(n, a) for n, a in zip(later, arrived)]), early_exchange, tail_exchange)
    recv = dict(zip(later, early_out), w_in=recv_w_in)

    res = {n: _reduce_adamw(recv[n], w[n], m[n], v[n], "reduce_adamw_" + n) for n in sharded}
    small_res, loss_sum = _reduce_adamw_small(small_parts, [w[n] for n in SMALL], [m[n] for n in SMALL], [v[n] for n in SMALL])
    res.update(zip(SMALL, small_res))
    out = [loss_sum[0, 0], grad_x[None]]
    for k in range(4):
        out += [res[n][k].T[None] if n in TRANSPOSED else res[n][k][None] if n in sharded else res[n][k] for n in ORDER]
    return tuple(out)
```

```python
import functools

import jax
import jax.numpy as jnp
import numpy as np
from jax import lax
from jax.experimental import pallas as pl
from jax.experimental.pallas import tpu as pltpu

D_MODEL = 1024
HEAD_DIM = 64
SWA_Q_HEADS = 8
SWA_KV_HEADS = 2
SB_HEADS = 8
WINDOW = 128
ROPE_THETA = 10000.0
D_FF = 2816
LN_EPS = 1e-5
DEPTH = 1
ALPHA = (2.0 * DEPTH) ** 0.25
SWA_Q_WIDTH = SWA_Q_HEADS * HEAD_DIM
SWA_KV_WIDTH = SWA_KV_HEADS * HEAD_DIM
SB_WIDTH = SB_HEADS * HEAD_DIM
GATE_WIDTH = 2 * D_MODEL
IN_WIDTHS = (SWA_Q_WIDTH, SWA_KV_WIDTH, SWA_KV_WIDTH, SB_WIDTH, SB_WIDTH, SB_WIDTH, GATE_WIDTH)
IN_TOTAL = sum(IN_WIDTHS)
ATTN_SCALE = HEAD_DIM ** -0.5

ADAM_LR = 0.001
ADAM_B1 = 0.9
ADAM_B2 = 0.999
ADAM_EPS = 1e-08
ADAM_WD = 0.01
ADAM_STEP = 10

N_DEV = 8
LANES = 128
SB_BLOCK = 256
VMEM_LIMIT = 56 * 1024 * 1024

F32 = jnp.float32
BF16 = jnp.bfloat16
ACT_DTYPE = BF16
MESH = pl.DeviceIdType.MESH


def _params(sem=None):
    return pltpu.CompilerParams(dimension_semantics=sem, vmem_limit_bytes=VMEM_LIMIT)


def _dot(a, b):
    return jnp.dot(a, b, preferred_element_type=F32)


def _dot_nt(a, b):
    return lax.dot_general(a, b, (((1,), (1,)), ((), ())), preferred_element_type=F32)


def _dot_tn(a, b):
    return lax.dot_general(a, b, (((0,), (0,)), ((), ())), preferred_element_type=F32)


def _split_bf16(v):
    hi = v.astype(BF16)
    lo = (v - hi.astype(F32)).astype(BF16)
    return hi, lo


def _matmul(a, b, *, kind, out_shape, grid, a_spec, b_spec, out_spec, name, add=None, add_spec=None, add_scale=1.0):
    dot = {"nn": _dot, "nt": _dot_nt, "tn": _dot_tn}[kind]

    def body(*refs):
        if add is None:
            a_ref, b_ref, o_ref = refs
        else:
            a_ref, b_ref, add_ref, o_ref = refs
        r = dot(a_ref[...].astype(BF16), b_ref[...].astype(BF16))
        if add is not None:
            r = r + add_scale * add_ref[...]
        o_ref[...] = r.astype(o_ref.dtype)

    ins = [a, b] + ([] if add is None else [add])
    specs = [a_spec, b_spec] + ([] if add is None else [add_spec])
    return pl.pallas_call(
        body, name=name, grid=grid, in_specs=specs, out_specs=out_spec, out_shape=out_shape,
        compiler_params=_params(("parallel",) * len(grid)),
    )(*ins)


def _rope_tables(pos_col, inv_freq_lanes):
    T = pos_col.shape[0]
    tm = min(512, T)

    def body(pos_ref, f_ref, cos_ref, sin_ref):
        ang = pos_ref[...].astype(F32) * f_ref[...]
        cos_ref[...] = jnp.cos(ang)
        sin_ref[...] = jnp.sin(ang)

    return pl.pallas_call(
        body, name="rope_tables", grid=(T // tm,),
        in_specs=[pl.BlockSpec((tm, 1), lambda i: (i, 0)), pl.BlockSpec((1, LANES), lambda i: (0, 0))],
        out_specs=[pl.BlockSpec((tm, LANES), lambda i: (i, 0))] * 2,
        out_shape=[jax.ShapeDtypeStruct((T, LANES), F32)] * 2,
        compiler_params=_params(("parallel",)),
    )(pos_col, inv_freq_lanes)


def _lane_iota(shape):
    return lax.broadcasted_iota(jnp.int32, shape, len(shape) - 1)


def _rot_half(t):
    first = (_lane_iota(t.shape) % HEAD_DIM) < (HEAD_DIM // 2)
    return jnp.where(first, -pltpu.roll(t, LANES - HEAD_DIM // 2, axis=1), pltpu.roll(t, HEAD_DIM // 2, axis=1))


def _rope(t, cos, sin):
    return t * cos + _rot_half(t) * sin


def _rope_transpose(d, cos, sin):
    return d * cos - _rot_half(d * sin)


_IN_DTYPES = (F32, F32, BF16, BF16, BF16, BF16, F32)


def _in_proj(x, w_in_t):
    T = x.shape[0]
    tm = min(512, T)
    offs = np.cumsum((0,) + IN_WIDTHS)

    def body(x_ref, w_ref, xb_ref, *outs):
        xb = x_ref[...].astype(BF16)
        xb_ref[...] = xb
        for o_ref, a, b in zip(outs, offs[:-1], offs[1:]):
            o_ref[...] = _dot_nt(xb, w_ref[a:b, :]).astype(o_ref.dtype)

    row = lambda n: pl.BlockSpec((tm, n), lambda i: (i, 0))
    return pl.pallas_call(
        body, name="in_proj", grid=(T // tm,),
        in_specs=[row(D_MODEL), pl.BlockSpec((IN_TOTAL, D_MODEL), lambda i: (0, 0))],
        out_specs=[row(D_MODEL)] + [row(n) for n in IN_WIDTHS],
        out_shape=[jax.ShapeDtypeStruct((T, D_MODEL), BF16)] + [jax.ShapeDtypeStruct((T, n), dt) for n, dt in zip(IN_WIDTHS, _IN_DTYPES)],
        compiler_params=_params(("parallel",)),
    )(x, w_in_t)


def _swa_specs(T):
    blk = WINDOW
    cur = lambda n: pl.BlockSpec((blk, n), lambda i: (i, 0))
    prev = lambda n: pl.BlockSpec((blk, n), lambda i: (jnp.maximum(i - 1, 0), 0))
    return blk, cur, prev


SWA_GROUP = SWA_Q_HEADS // SWA_KV_HEADS


def _swa_stack(pairs):
    lane = _lane_iota(pairs[0].shape)
    zero = jnp.zeros((), pairs[0].dtype)
    rows = []
    for h in range(SWA_Q_HEADS):
        hh, g = h % 2, h // SWA_GROUP
        x = jnp.where((lane >= hh * HEAD_DIM) & (lane < (hh + 1) * HEAD_DIM), pairs[h // 2], zero)
        rows.append(x if hh == g else pltpu.roll(x, HEAD_DIM, axis=1))
    return jnp.concatenate(rows, axis=0)


def _swa_unstack(stacked, blk):
    low = _lane_iota((blk, LANES)) < HEAD_DIM
    pairs = []
    for pp in range(SWA_Q_HEADS // 2):
        halves = []
        for hh in range(2):
            h = 2 * pp + hh
            x = stacked[h * blk:(h + 1) * blk]
            halves.append(x if hh == h // SWA_GROUP else pltpu.roll(x, HEAD_DIM, axis=1))
        pairs.append(jnp.where(low, halves[0], halves[1]))
    return pairs


def _swa_probs(i, q_stack, kwin, sink_ref, blk):
    r = lax.broadcasted_iota(jnp.int32, (blk, 2 * blk), 0)
    c = lax.broadcasted_iota(jnp.int32, (blk, 2 * blk), 1)
    rel = blk + r - c
    valid = (rel >= 0) & (rel < WINDOW) & ((c >= blk) | (i > 0))
    bias = jnp.concatenate([jnp.where(valid, 0.0, -1e30)] * SWA_Q_HEADS, axis=0)
    head = lax.broadcasted_iota(jnp.int32, (SWA_Q_HEADS * blk, 1), 0) // blk
    sink = jnp.zeros((SWA_Q_HEADS * blk, 1), F32)
    for h in range(SWA_Q_HEADS):
        sink = jnp.where(head == h, sink_ref[h], sink)
    s = _dot_nt(q_stack, kwin) * ATTN_SCALE + bias
    m = jnp.maximum(jnp.max(s, axis=1, keepdims=True), sink)
    p = jnp.exp(s - m)
    es = jnp.exp(sink - m)
    den = jnp.sum(p, axis=1, keepdims=True) + es
    return p / den, es / den


def _swa_inputs(q_ref, kp_ref, kc_ref, vp_ref, vc_ref, cp_ref, cc_ref, sp_ref, sc_ref):
    cc, sc = cc_ref[...], sc_ref[...]
    kwin = jnp.concatenate([_rope(kp_ref[...], cp_ref[...], sp_ref[...]), _rope(kc_ref[...], cc, sc)], axis=0).astype(BF16)
    vwin = jnp.concatenate([vp_ref[...], vc_ref[...]], axis=0)
    q_stack = _swa_stack([_rope(q_ref[:, pp * LANES:(pp + 1) * LANES], cc, sc) for pp in range(SWA_Q_HEADS // 2)]).astype(BF16)
    return q_stack, kwin, vwin


def _swa_fwd(qa, ka, va, cos, sin, sinks):
    T = qa.shape[0]
    blk, cur, prev = _swa_specs(T)

    def body(sink_ref, q_ref, kp_ref, kc_ref, vp_ref, vc_ref, cp_ref, cc_ref, sp_ref, sc_ref, o_ref):
        q_stack, kwin, vwin = _swa_inputs(q_ref, kp_ref, kc_ref, vp_ref, vc_ref, cp_ref, cc_ref, sp_ref, sc_ref)
        probs, _ = _swa_probs(pl.program_id(0), q_stack, kwin, sink_ref, blk)
        for pp, tile in enumerate(_swa_unstack(_dot(probs.astype(BF16), vwin), blk)):
            o_ref[:, pp * LANES:(pp + 1) * LANES] = tile.astype(o_ref.dtype)

    return pl.pallas_call(
        body, name="swa_fwd", grid=(T // blk,),
        in_specs=[pl.BlockSpec(memory_space=pltpu.SMEM), cur(SWA_Q_WIDTH), prev(LANES), cur(LANES), prev(LANES), cur(LANES),
                  prev(LANES), cur(LANES), prev(LANES), cur(LANES)],
        out_specs=cur(SWA_Q_WIDTH),
        out_shape=jax.ShapeDtypeStruct((T, SWA_Q_WIDTH), BF16),
        compiler_params=_params(("parallel",)),
    )(sinks, qa, ka, ka, va, va, cos, cos, sin, sin)


def _swa_bwd(qa, ka, va, cos, sin, sinks, dya):
    T = qa.shape[0]
    blk, cur, prev = _swa_specs(T)
    full = lambda n: pl.BlockSpec((T, n), lambda i: (0, 0))

    def body(sink_ref, q_ref, kp_ref, kc_ref, vp_ref, vc_ref, cp_ref, cc_ref, sp_ref, sc_ref, do_ref,
             dq_ref, dk_out, dv_out, dsink_ref, dk_ref, dv_ref):
        i = pl.program_id(0)

        @pl.when(i == 0)
        def _():
            dk_ref[...] = jnp.zeros_like(dk_ref)
            dv_ref[...] = jnp.zeros_like(dv_ref)
            dsink_ref[...] = jnp.zeros_like(dsink_ref)

        cp, cc, sp, sc = cp_ref[...], cc_ref[...], sp_ref[...], sc_ref[...]
        q_stack, kwin, vwin = _swa_inputs(q_ref, kp_ref, kc_ref, vp_ref, vc_ref, cp_ref, cc_ref, sp_ref, sc_ref)
        probs, psink = _swa_probs(i, q_stack, kwin, sink_ref, blk)
        do_stack = _swa_stack([do_ref[:, pp * LANES:(pp + 1) * LANES] for pp in range(SWA_Q_HEADS // 2)])
        dp = _dot_nt(do_stack, vwin)
        dsum = jnp.sum(probs * dp, axis=1, keepdims=True)
        ds = (probs * (dp - dsum) * ATTN_SCALE).astype(BF16)
        for pp, tile in enumerate(_swa_unstack(_dot(ds, kwin), blk)):
            dq_ref[:, pp * LANES:(pp + 1) * LANES] = _rope_transpose(tile, cc, sc).astype(dq_ref.dtype)
        dkw = _dot_tn(ds, q_stack)
        dvw = _dot_tn(probs.astype(BF16), do_stack)
        lane1 = _lane_iota((1, LANES))
        sink_share = psink * dsum
        dsink = jnp.zeros((1, LANES), F32)
        for h in range(SWA_Q_HEADS):
            dsink = dsink + jnp.where(lane1 == h, -jnp.sum(sink_share[h * blk:(h + 1) * blk]), 0.0)
        dsink_ref[...] += dsink
        ip = jnp.maximum(i - 1, 0)
        rows_p = pl.ds(pl.multiple_of(ip * blk, blk), blk)
        rows_c = pl.ds(pl.multiple_of(i * blk, blk), blk)
        dk_ref[rows_p, :] += _rope_transpose(dkw[:blk], cp, sp)
        dv_ref[rows_p, :] += dvw[:blk]
        dk_ref[rows_c, :] += _rope_transpose(dkw[blk:], cc, sc)
        dv_ref[rows_c, :] += dvw[blk:]

        @pl.when(i == T // blk - 1)
        def _():
            dk_out[...] = dk_ref[...].astype(BF16)
            dv_out[...] = dv_ref[...].astype(BF16)

    return pl.pallas_call(
        body, name="swa_bwd", grid=(T // blk,),
        in_specs=[pl.BlockSpec(memory_space=pltpu.SMEM), cur(SWA_Q_WIDTH), prev(LANES), cur(LANES), prev(LANES), cur(LANES),
                  prev(LANES), cur(LANES), prev(LANES), cur(LANES), cur(SWA_Q_WIDTH)],
        out_specs=[cur(SWA_Q_WIDTH), full(LANES), full(LANES), pl.BlockSpec((1, LANES), lambda i: (0, 0))],
        out_shape=[jax.ShapeDtypeStruct((T, SWA_Q_WIDTH), BF16), jax.ShapeDtypeStruct((T, LANES), BF16),
                   jax.ShapeDtypeStruct((T, LANES), BF16), jax.ShapeDtypeStruct((1, LANES), F32)],
        scratch_shapes=[pltpu.VMEM((T, LANES), F32)] * 2,
        compiler_params=_params(("arbitrary",)),
    )(sinks, qa, ka, ka, va, va, cos, cos, sin, sin, dya)


class _Exchange:
    FLIPS = [(fx, fy, fc) for fx in (0, 1) for fy in (0, 1) for fc in (0, 1) if (fx, fy, fc) != (0, 0, 0)]

    def __init__(self, arrays, kinds):
        self.arrays, self.kinds, self.n = list(arrays), list(kinds), len(arrays)

    def out_shape(self):
        return [jax.ShapeDtypeStruct(a.shape if k == "scatter" else (N_DEV,) + a.shape, a.dtype) for a, k in zip(self.arrays, self.kinds)]

    def scratch(self):
        return [pltpu.SemaphoreType.DMA((self.n, 7)), pltpu.SemaphoreType.DMA((self.n, 7)), pltpu.SemaphoreType.DMA((self.n,))]

    def bind(self, ins, outs, send_sems, recv_sems, local_sems):
        x, y, c = lax.axis_index("x"), lax.axis_index("y"), lax.axis_index("c")
        me = 4 * x + 2 * y + c
        local, remote = [], []
        for a, kind in enumerate(self.kinds):
            mine = ins[a].at[me] if kind == "scatter" else ins[a]
            local.append(pltpu.make_async_copy(mine, outs[a].at[me], local_sems.at[a]))
            for k, (fx, fy, fc) in enumerate(self.FLIPS):
                peer = (x ^ fx, y ^ fy, c ^ fc)
                peer_slot = 4 * peer[0] + 2 * peer[1] + peer[2]
                src = ins[a].at[peer_slot] if kind == "scatter" else ins[a]
                sems = dict(send_sem=send_sems.at[a, k], recv_sem=recv_sems.at[a, k], device_id=peer, device_id_type=MESH)
                remote.append((pltpu.make_async_remote_copy(src_ref=src, dst_ref=outs[a].at[me], **sems),
                               pltpu.make_async_remote_copy(src_ref=src, dst_ref=outs[a].at[peer_slot], **sems)))

        def start():
            for cp in local:
                cp.start()
            for send, _ in remote:
                send.start()

        def wait():
            for send, arrival in remote:
                arrival.wait_recv()
                send.wait_send()
            for cp in local:
                cp.wait()

        return start, wait


def _hosted_call(body, name, grid, exchange, *, in_specs, out_specs, out_shape, semantics, args, scratch=(), prefetch=()):
    n = 0 if exchange is None else exchange.n
    n_pre, n_in, n_out, n_scratch = len(prefetch), len(in_specs), len(out_specs), len(scratch)

    def hosted(*refs):
        pre, rest = refs[:n_pre], refs[n_pre:]
        ins, rest = rest[:n_in], rest[n_in:]
        ex_ins, rest = rest[:n], rest[n:]
        outs, rest = rest[:n_out], rest[n_out:]
        ex_outs, rest = rest[:n], rest[n:]
        own, sems = rest[:n_scratch], rest[n_scratch:]
        if exchange is None:
            return body(*pre, *ins, *outs, *own)
        start, wait = exchange.bind(ex_ins, ex_outs, *sems)
        ids = [pl.program_id(d) for d in range(len(grid))]
        first = functools.reduce(jnp.logical_and, [i == 0 for i in ids])
        last = functools.reduce(jnp.logical_and, [i == g - 1 for i, g in zip(ids, grid)])
        pl.when(first)(start)
        body(*pre, *ins, *outs, *own)
        pl.when(last)(wait)

    grid_spec = pltpu.PrefetchScalarGridSpec(
        num_scalar_prefetch=n_pre, grid=grid, in_specs=list(in_specs) + [ANY] * n, out_specs=list(out_specs) + [ANY] * n,
        scratch_shapes=list(scratch) + ([] if exchange is None else exchange.scratch()))
    res = pl.pallas_call(
        hosted, name=name, grid_spec=grid_spec, out_shape=list(out_shape) + ([] if exchange is None else exchange.out_shape()),
        compiler_params=_params(semantics if exchange is None else ("arbitrary",) * len(grid)),
    )(*prefetch, *args, *([] if exchange is None else exchange.arrays))
    return res[:n_out], res[n_out:]


SOFTPLUS_LINEAR_FROM = 30.0


def _sb_scores(qm, k, valid):
    z = _dot_nt(qm, k)
    sp = jnp.where(z > SOFTPLUS_LINEAR_FROM, z, jnp.log(1.0 + jnp.exp(z)))
    log_beta = z - sp
    if valid is not None:
        sp = jnp.where(valid, sp, 0.0)
    return sp, log_beta


def _tri2(B, cmp):
    r = lax.broadcasted_iota(jnp.int32, (2 * B, B), 0) % B
    c = lax.broadcasted_iota(jnp.int32, (2 * B, B), 1)
    return cmp(r, c).astype(BF16)


def _tri_sum(v, tri2):
    hi, lo = _split_bf16(v)
    return _dot(jnp.concatenate([hi, lo], axis=1), tri2)


def _head_masks(x):
    low = _lane_iota(x.shape) < HEAD_DIM
    zero = jnp.zeros((), x.dtype)
    return jnp.where(low, x, zero), jnp.where(low, zero, x)


def _strictly_below(B):
    r = lax.broadcasted_iota(jnp.int32, (B, B), 0)
    c = lax.broadcasted_iota(jnp.int32, (B, B), 1)
    return c < r


def _sb_grid(T, descending):
    B = min(SB_BLOCK, T)
    n = T // B
    pairs = [(i, j) for i in range(n) for j in (range(i, -1, -1) if descending else range(i + 1))]
    return B, jnp.asarray([p[0] for p in pairs], jnp.int32), jnp.asarray([p[1] for p in pairs], jnp.int32)


N_PAIRS = SB_HEADS // 2
PAIR_COLS = [slice(p * LANES, (p + 1) * LANES) for p in range(N_PAIRS)]


def _sb_fwd(qb, kb, vb, exchange=None):
    T = qb.shape[0]
    B, i_tab, j_tab = _sb_grid(T, descending=True)
    n = T // B

    def body(i_ref, j_ref, q_ref, k_ref, v_ref, o_ref, a_ref, b_ref, acc_ref, c_ref, tri_ref):
        s = pl.program_id(0)
        i, j = i_ref[s], j_ref[s]

        @pl.when(s == 0)
        def _():
            tri_ref[...] = _tri2(B, lambda r, c: r > c)

        @pl.when(j == i)
        def _():
            acc_ref[...] = jnp.zeros_like(acc_ref)
            c_ref[...] = jnp.zeros_like(c_ref)

        def block(valid):
            for p, cols in enumerate(PAIR_COLS):
                qms = _head_masks(q_ref[:, cols] * ATTN_SCALE)
                k = k_ref[:, cols]
                probs = []
                for hh in range(2):
                    h = 2 * p + hh
                    sp, lb = _sb_scores(qms[hh], k, valid)
                    c = c_ref[h]
                    a = jnp.exp(lb - (c + _tri_sum(sp, tri_ref[...])))
                    beta = jnp.exp(lb)
                    if valid is not None:
                        a = jnp.where(valid, a, 0.0)
                        beta = jnp.where(valid, beta, 0.0)
                    probs.append(a.astype(BF16))
                    a_ref[h] = probs[-1]
                    b_ref[h] = beta.astype(BF16)
                    c_ref[h] = c + jnp.sum(sp, axis=1, keepdims=True)
                acc_ref[:, cols] += _dot(jnp.concatenate(probs, axis=1), jnp.concatenate(_head_masks(v_ref[:, cols]), axis=0))

        pl.when(j == i)(lambda: block(_strictly_below(B)))
        pl.when(j != i)(lambda: block(None))

        @pl.when(j == 0)
        def _():
            o_ref[...] = acc_ref[...].astype(o_ref.dtype)

    q_spec = pl.BlockSpec((B, SB_WIDTH), lambda s, i_ref, j_ref: (i_ref[s], 0))
    k_spec = pl.BlockSpec((B, SB_WIDTH), lambda s, i_ref, j_ref: (j_ref[s], 0))
    tile = pl.BlockSpec((None, None, SB_HEADS, B, B), lambda s, i_ref, j_ref: (i_ref[s], j_ref[s], 0, 0, 0))
    saved = jax.ShapeDtypeStruct((n, n, SB_HEADS, B, B), BF16)
    return _hosted_call(
        body, "sb_fwd", (int(i_tab.shape[0]),), exchange, prefetch=(i_tab, j_tab),
        in_specs=[q_spec, k_spec, k_spec], out_specs=[q_spec, tile, tile],
        out_shape=[jax.ShapeDtypeStruct((T, SB_WIDTH), BF16), saved, saved],
        scratch=[pltpu.VMEM((B, SB_WIDTH), F32), pltpu.VMEM((SB_HEADS, B, 1), F32), pltpu.VMEM((2 * B, B), BF16)],
        semantics=("arbitrary",), args=(qb, kb, vb))


def _sb_bwd(qb, kb, vb, probs, betas, dyb, exchange=None):
    T = qb.shape[0]
    B, i_tab, j_tab = _sb_grid(T, descending=False)
    n_steps = int(i_tab.shape[0])

    def block_diag_t(x):
        xt = x.T
        top = lax.broadcasted_iota(jnp.int32, xt.shape, 0) < HEAD_DIM
        zero = jnp.zeros((), x.dtype)
        return jnp.concatenate([jnp.where(top, xt, zero), jnp.where(top, zero, xt)], axis=1)

    def body(i_ref, j_ref, q_ref, k_ref, v_ref, a_ref, b_ref, do_ref, dq_ref, dk_out, dv_out,
             dq_acc, cg_ref, dkt_ref, dvt_ref, tri_ref, qt_ref, dot_ref):
        s = pl.program_id(0)
        i, j = i_ref[s], j_ref[s]

        @pl.when(s == 0)
        def _():
            dkt_ref[...] = jnp.zeros_like(dkt_ref)
            dvt_ref[...] = jnp.zeros_like(dvt_ref)
            tri_ref[...] = _tri2(B, lambda r, c: r < c)[:B]

        @pl.when(j == 0)
        def _():
            dq_acc[...] = jnp.zeros_like(dq_acc)
            cg_ref[...] = jnp.zeros_like(cg_ref)
            for p, cols in enumerate(PAIR_COLS):
                qt_ref[p] = block_diag_t(q_ref[:, cols] * ATTN_SCALE)
                dot_ref[p] = block_diag_t(do_ref[:, cols])

        for p, cols in enumerate(PAIR_COLS):
            doms = _head_masks(do_ref[:, cols])
            k, v = k_ref[:, cols], v_ref[:, cols]
            dzs = []
            for hh in range(2):
                h = 2 * p + hh
                g = a_ref[h].astype(F32) * _dot_nt(doms[hh], v)
                cg = cg_ref[h]
                gsum = g + (cg + _dot(g.astype(BF16), tri_ref[...]))
                dzs.append((g - b_ref[h].astype(F32) * gsum).astype(BF16))
                cg_ref[h] = cg + jnp.sum(g, axis=1, keepdims=True)
            dq_acc[:, cols] += _dot(jnp.concatenate(dzs, axis=1), jnp.concatenate(_head_masks(k), axis=0))
            dkt_ref[j, cols, :] += _dot(qt_ref[p], jnp.concatenate(dzs, axis=0))
            dvt_ref[j, cols, :] += _dot(dot_ref[p], jnp.concatenate([a_ref[2 * p], a_ref[2 * p + 1]], axis=0))

        @pl.when(j == i)
        def _():
            dq_ref[...] = (dq_acc[...] * ATTN_SCALE).astype(dq_ref.dtype)

        @pl.when(s == n_steps - 1)
        def _():
            for jb in range(T // B):
                dk_out[jb * B:(jb + 1) * B, :] = dkt_ref[jb].T.astype(BF16)
                dv_out[jb * B:(jb + 1) * B, :] = dvt_ref[jb].T.astype(BF16)

    q_spec = pl.BlockSpec((B, SB_WIDTH), lambda s, i_ref, j_ref: (i_ref[s], 0))
    k_spec = pl.BlockSpec((B, SB_WIDTH), lambda s, i_ref, j_ref: (j_ref[s], 0))
    tile = pl.BlockSpec((None, None, SB_HEADS, B, B), lambda s, i_ref, j_ref: (i_ref[s], j_ref[s], 0, 0, 0))
    full = pl.BlockSpec((T, SB_WIDTH), lambda s, i_ref, j_ref: (0, 0))
    return _hosted_call(
        body, "sb_bwd", (n_steps,), exchange, prefetch=(i_tab, j_tab),
        in_specs=[q_spec, k_spec, k_spec, tile, tile, q_spec], out_specs=[q_spec, full, full],
        out_shape=[jax.ShapeDtypeStruct((T, SB_WIDTH), BF16)] * 3,
        scratch=[pltpu.VMEM((B, SB_WIDTH), F32), pltpu.VMEM((SB_HEADS, B, 1), F32), pltpu.VMEM((T // B, SB_WIDTH, B), F32),
                 pltpu.VMEM((T // B, SB_WIDTH, B), F32), pltpu.VMEM((B, B), BF16), pltpu.VMEM((N_PAIRS, LANES, 2 * B), BF16),
                 pltpu.VMEM((N_PAIRS, LANES, 2 * B), BF16)],
        semantics=("arbitrary",), args=(qb, kb, vb, probs, betas, dyb))


def _ln_stats(u):
    mu = jnp.mean(u, axis=-1, keepdims=True)
    xc = u - mu
    var = jnp.mean(xc * xc, axis=-1, keepdims=True)
    rstd = lax.rsqrt(var + LN_EPS)
    return xc * rstd, rstd


def _ln_bwd(dy, xhat, rstd, g):
    dxh = dy * g
    return rstd * (dxh - jnp.mean(dxh, axis=-1, keepdims=True) - xhat * jnp.mean(dxh * xhat, axis=-1, keepdims=True))


def _gates(gl_ref, bg_ref):
    ga = jax.nn.sigmoid(gl_ref[:, :D_MODEL] + bg_ref[:, :D_MODEL])
    gb = jax.nn.sigmoid(gl_ref[:, D_MODEL:] + bg_ref[:, D_MODEL:])
    return ga, gb


def _mix_fwd(ya, yb, gl, x, wa, wb, wo, b_gate, ln1_g, ln1_b):
    T = x.shape[0]
    tm = min(512, T)

    def body(ya_ref, yb_ref, gl_ref, x_ref, wa_ref, wb_ref, wo_ref, bg_ref, g_ref, b_ref, h_ref, u_ref, x1_ref):
        ga, gb = _gates(gl_ref, bg_ref)
        h = (ga * _dot(ya_ref[...], wa_ref[...]) + gb * _dot(yb_ref[...], wb_ref[...])).astype(BF16)
        h_ref[...] = h
        u = ALPHA * x_ref[...] + _dot(h, wo_ref[...])
        u_ref[...] = u
        xhat, _ = _ln_stats(u)
        x1_ref[...] = (xhat * g_ref[...] + b_ref[...]).astype(BF16)

    row = lambda n: pl.BlockSpec((tm, n), lambda i: (i, 0))
    const = lambda r, n: pl.BlockSpec((r, n), lambda i: (0, 0))
    return pl.pallas_call(
        body, name="mix_fwd", grid=(T // tm,),
        in_specs=[row(SWA_Q_WIDTH), row(SB_WIDTH), row(GATE_WIDTH), row(D_MODEL), const(SWA_Q_WIDTH, D_MODEL), const(SB_WIDTH, D_MODEL),
                  const(D_MODEL, D_MODEL), const(1, GATE_WIDTH), const(1, D_MODEL), const(1, D_MODEL)],
        out_specs=[row(D_MODEL)] * 3,
        out_shape=[jax.ShapeDtypeStruct((T, D_MODEL), BF16), jax.ShapeDtypeStruct((T, D_MODEL), F32), jax.ShapeDtypeStruct((T, D_MODEL), BF16)],
        compiler_params=_params(("parallel",)),
    )(ya, yb, gl, x, wa, wb, wo, b_gate, ln1_g, ln1_b)


def _mix_bwd(du1, ya, yb, gl, wa, wb, wo, b_gate):
    T = du1.shape[0]
    tm = min(512, T)

    def body(du_ref, ya_ref, yb_ref, gl_ref, wa_ref, wb_ref, wo_ref, bg_ref, dya_ref, dyb_ref, dgl_ref, dta_ref, dtb_ref, dbg_ref):
        @pl.when(pl.program_id(0) == 0)
        def _():
            dbg_ref[...] = jnp.zeros_like(dbg_ref)

        dh = _dot_nt(du_ref[...].astype(BF16), wo_ref[...])
        ga, gb = _gates(gl_ref, bg_ref)
        for gate, y_ref, w_ref, dy_ref, dt_ref, lo in ((ga, ya_ref, wa_ref, dya_ref, dta_ref, 0), (gb, yb_ref, wb_ref, dyb_ref, dtb_ref, D_MODEL)):
            t = _dot(y_ref[...], w_ref[...])
            dlogit = dh * t * gate * (1.0 - gate)
            dgl_ref[:, lo:lo + D_MODEL] = dlogit.astype(BF16)
            dbg_ref[:, lo:lo + D_MODEL] += jnp.sum(dlogit, axis=0, keepdims=True)
            dt = (dh * gate).astype(BF16)
            dt_ref[...] = dt
            dy_ref[...] = _dot_nt(dt, w_ref[...]).astype(BF16)

    row = lambda n: pl.BlockSpec((tm, n), lambda i: (i, 0))
    const = lambda r, n: pl.BlockSpec((r, n), lambda i: (0, 0))
    sds = lambda n, dt: jax.ShapeDtypeStruct((T, n), dt)
    return pl.pallas_call(
        body, name="mix_bwd", grid=(T // tm,),
        in_specs=[row(D_MODEL), row(SWA_Q_WIDTH), row(SB_WIDTH), row(GATE_WIDTH), const(SWA_Q_WIDTH, D_MODEL), const(SB_WIDTH, D_MODEL),
                  const(D_MODEL, D_MODEL), const(1, GATE_WIDTH)],
        out_specs=[row(SWA_Q_WIDTH), row(SB_WIDTH), row(GATE_WIDTH), row(D_MODEL), row(D_MODEL), const(1, GATE_WIDTH)],
        out_shape=[sds(SWA_Q_WIDTH, BF16), sds(SB_WIDTH, BF16), sds(GATE_WIDTH, BF16), sds(D_MODEL, BF16), sds(D_MODEL, BF16),
                   jax.ShapeDtypeStruct((1, GATE_WIDTH), F32)],
        compiler_params=_params(("arbitrary",)),
    )(du1, ya, yb, gl, wa, wb, wo, b_gate)


CONV_COLS = LANES


CONV_CHUNK = 64
CONV_CHUNK_FWD = 256
HALO = 8


def _taps(ref, r0, rows, lead):
    return [ref[pl.ds(r0 + lead + k, rows), :] for k in ((-2, -1, 0) if lead else (0, 1, 2))]


def _chunks(T, rows, step, init=None, reverse=False):
    def body(c, carry):
        c = T // rows - 1 - c if reverse else c
        out = step(pl.multiple_of(c * rows, rows), *(() if init is None else (carry,)))
        return carry if init is None else out
    return lax.fori_loop(0, T // rows, body, 0 if init is None else init)


def _conv_chunk(taps, w_ref, b_ref):
    return w_ref[0:1, :] * taps[0] + w_ref[1:2, :] * taps[1] + w_ref[2:3, :] * taps[2] + b_ref[...]


def _fold(x):
    return jnp.sum(x.reshape(x.shape[0] // 8, 8, x.shape[1]), axis=0)


def _conv_specs(T):
    nb = D_FF // CONV_COLS
    pair = pl.BlockSpec((2, T, CONV_COLS), lambda j: (0, 0, j))
    gate = lambda r: pl.BlockSpec((r, CONV_COLS), lambda j: (0, j))
    up = lambda r: pl.BlockSpec((r, CONV_COLS), lambda j: (0, j + nb))
    return nb, pair, gate, up


def _conv_glu_fwd(p3, conv_w, conv_b):
    T = p3.shape[1]
    nb, pair, gate, up = _conv_specs(T)

    R = min(CONV_CHUNK_FWD, T)

    def body(p_ref, wg_ref, wu_ref, bg_ref, bu_ref, s_ref, pg_s, pu_s):
        for half, scr in enumerate((pg_s, pu_s)):
            scr[0:HALO, :] = jnp.zeros((HALO, CONV_COLS), F32)
            scr[HALO:HALO + T, :] = p_ref[half].astype(F32)
        def step(r0):
            ag = _conv_chunk(_taps(pg_s, r0, R, HALO), wg_ref, bg_ref)
            au = _conv_chunk(_taps(pu_s, r0, R, HALO), wu_ref, bu_ref)
            s_ref[pl.ds(r0, R), :] = (ag * jax.nn.sigmoid(ag) * au).astype(BF16)

        _chunks(T, R, step)

    return pl.pallas_call(
        body, name="conv_glu_fwd", grid=(nb,),
        in_specs=[pair, gate(3), up(3), gate(1), up(1)],
        out_specs=pl.BlockSpec((T, CONV_COLS), lambda j: (0, j)),
        out_shape=jax.ShapeDtypeStruct((T, D_FF), BF16),
        scratch_shapes=[pltpu.VMEM((T + HALO, CONV_COLS), F32)] * 2,
        compiler_params=_params(("parallel",)),
    )(p3, conv_w, conv_w, conv_b, conv_b)


def _conv_glu_bwd(p3, ds, conv_w, conv_b):
    T = p3.shape[1]
    nb, pair, gate, up = _conv_specs(T)

    R = min(CONV_CHUNK, T)

    def body(p_ref, ds_ref, wg_ref, wu_ref, bg_ref, bu_ref, dp_ref, dwg_ref, dwu_ref, dbg_ref, dbu_ref, pg_s, pu_s, dag_s, dau_s):
        for half, scr in enumerate((pg_s, pu_s)):
            scr[0:HALO, :] = jnp.zeros((HALO, CONV_COLS), F32)
            scr[HALO:HALO + T, :] = p_ref[half].astype(F32)
        for scr in (dag_s, dau_s):
            scr[T:T + HALO, :] = jnp.zeros((HALO, CONV_COLS), F32)
        halves = ((pg_s, dag_s, wg_ref, dwg_ref, dbg_ref), (pu_s, dau_s, wu_ref, dwu_ref, dbu_ref))

        def step(r0, sums):
            taps = [_taps(p_s, r0, R, HALO) for p_s, *_ in halves]
            ag = _conv_chunk(taps[0], wg_ref, bg_ref)
            au = _conv_chunk(taps[1], wu_ref, bu_ref)
            sg = jax.nn.sigmoid(ag)
            d = ds_ref[pl.ds(r0, R), :].astype(F32)
            das = (d * au * (sg * (1.0 + ag * (1.0 - sg))), d * ag * sg)
            out = []
            for half, (_, da_s, w_ref, *_) in enumerate(halves):
                da_s[pl.ds(r0, R), :] = das[half]
                out.append(tuple(sums[half][k] + _fold(das[half] * taps[half][k]) for k in range(3)) + (sums[half][3] + _fold(das[half]),))
                _, da1, da2 = _taps(da_s, r0, R, 0)
                dp_ref[half, pl.ds(r0, R), :] = (w_ref[2:3, :] * das[half] + w_ref[1:2, :] * da1 + w_ref[0:1, :] * da2).astype(BF16)
            return tuple(out)

        sums = _chunks(T, R, step, ((jnp.zeros((8, CONV_COLS), F32),) * 4,) * 2, reverse=True)
        for half, (_, da_s, w_ref, dw_ref, db_ref) in enumerate(halves):
            for k in range(3):
                dw_ref[k:k + 1, :] = jnp.sum(sums[half][k], axis=0, keepdims=True)
            db_ref[...] = jnp.sum(sums[half][3], axis=0, keepdims=True)

    col = lambda r: pl.BlockSpec((r, CONV_COLS), lambda j: (0, j))
    return pl.pallas_call(
        body, name="conv_glu_bwd", grid=(nb,),
        in_specs=[pair, col(T), gate(3), up(3), gate(1), up(1)],
        out_specs=[pair, col(3), col(3), col(1), col(1)],
        out_shape=[jax.ShapeDtypeStruct((2, T, D_FF), BF16), jax.ShapeDtypeStruct((3, D_FF), F32), jax.ShapeDtypeStruct((3, D_FF), F32),
                   jax.ShapeDtypeStruct((1, D_FF), F32), jax.ShapeDtypeStruct((1, D_FF), F32)],
        scratch_shapes=[pltpu.VMEM((T + HALO, CONV_COLS), F32)] * 4,
        compiler_params=_params(("parallel",)),
    )(p3, ds, conv_w, conv_w, conv_b, conv_b)


def _ffn_down_loss(s, w_down, u1, ln1_g, ln1_b, ln2_g, ln2_b, target):
    T = u1.shape[0]
    tm = min(512, T)

    def body(s_ref, w_ref, u1_ref, g1_ref, b1_ref, g2_ref, b2_ref, t_ref, du_ref, dub_ref, dg_ref, db_ref, loss_ref):
        @pl.when(pl.program_id(0) == 0)
        def _():
            dg_ref[...] = jnp.zeros_like(dg_ref)
            db_ref[...] = jnp.zeros_like(db_ref)
            loss_ref[...] = jnp.zeros_like(loss_ref)

        xh1, _ = _ln_stats(u1_ref[...])
        x1 = xh1 * g1_ref[...] + b1_ref[...]
        u2 = ALPHA * x1 + _dot(s_ref[...], w_ref[...])
        xh2, rstd2 = _ln_stats(u2)
        err = xh2 * g2_ref[...] + b2_ref[...] - t_ref[...]
        per_token = jnp.mean(err * err, axis=-1, keepdims=True)
        loss_ref[...] += 0.5 * jnp.sum(per_token, axis=0, keepdims=True)
        dy = err * (1.0 / D_MODEL)
        dg_ref[...] += jnp.sum(dy * xh2, axis=0, keepdims=True)
        db_ref[...] += jnp.sum(dy, axis=0, keepdims=True)
        du2 = _ln_bwd(dy, xh2, rstd2, g2_ref[...])
        du_ref[...] = du2
        dub_ref[...] = du2.astype(BF16)

    row = lambda n: pl.BlockSpec((tm, n), lambda i: (i, 0))
    const = lambda r, n: pl.BlockSpec((r, n), lambda i: (0, 0))
    vec = const(1, D_MODEL)
    return pl.pallas_call(
        body, name="ffn_down_loss", grid=(T // tm,),
        in_specs=[row(D_FF), const(D_FF, D_MODEL), row(D_MODEL), vec, vec, vec, vec, row(D_MODEL)],
        out_specs=[row(D_MODEL), row(D_MODEL), vec, vec, const(1, LANES)],
        out_shape=[jax.ShapeDtypeStruct((T, D_MODEL), F32), jax.ShapeDtypeStruct((T, D_MODEL), BF16), jax.ShapeDtypeStruct((1, D_MODEL), F32),
                   jax.ShapeDtypeStruct((1, D_MODEL), F32), jax.ShapeDtypeStruct((1, LANES), F32)],
        compiler_params=_params(("arbitrary",)),
    )(s, w_down, u1, ln1_g, ln1_b, ln2_g, ln2_b, target)


def _ffn_up_bwd_ln1(dp3, w_up, du2, u1, ln1_g):
    T = u1.shape[0]
    tm = min(512, T)

    def body(dp_ref, w_ref, du2_ref, u1_ref, g_ref, du_ref, dub_ref, dg_ref, db_ref):
        @pl.when(pl.program_id(0) == 0)
        def _():
            dg_ref[...] = jnp.zeros_like(dg_ref)
            db_ref[...] = jnp.zeros_like(db_ref)

        dx1 = _dot(dp_ref[0], w_ref[:D_FF, :]) + _dot(dp_ref[1], w_ref[D_FF:, :]) + ALPHA * du2_ref[...]
        xh, rstd = _ln_stats(u1_ref[...])
        dg_ref[...] += jnp.sum(dx1 * xh, axis=0, keepdims=True)
        db_ref[...] += jnp.sum(dx1, axis=0, keepdims=True)
        du1 = _ln_bwd(dx1, xh, rstd, g_ref[...])
        du_ref[...] = du1
        dub_ref[...] = du1.astype(BF16)

    row = lambda n: pl.BlockSpec((tm, n), lambda i: (i, 0))
    const = lambda r, n: pl.BlockSpec((r, n), lambda i: (0, 0))
    vec = const(1, D_MODEL)
    return pl.pallas_call(
        body, name="ffn_up_bwd_ln1", grid=(T // tm,),
        in_specs=[pl.BlockSpec((2, tm, D_FF), lambda i: (0, i, 0)), const(2 * D_FF, D_MODEL), row(D_MODEL), row(D_MODEL), vec],
        out_specs=[row(D_MODEL), row(D_MODEL), vec, vec],
        out_shape=[jax.ShapeDtypeStruct((T, D_MODEL), F32), jax.ShapeDtypeStruct((T, D_MODEL), BF16), jax.ShapeDtypeStruct((1, D_MODEL), F32),
                   jax.ShapeDtypeStruct((1, D_MODEL), F32)],
        compiler_params=_params(("arbitrary",)),
    )(dp3, w_up, du2, u1, ln1_g)


def _local_step(x, positions, w_in, b_gate, sinks, ln1_g, ln1_b, conv_b, ln2_g, ln2_b, target, later_weights,
                early_exchange=None, tail_exchange=None):
    T = x.shape[0]
    inv_freq = 1.0 / (ROPE_THETA ** (jnp.arange(0, HEAD_DIM, 2, dtype=F32) / HEAD_DIM))
    cos, sin = _rope_tables(positions.reshape(T, 1), jnp.tile(inv_freq, LANES // (HEAD_DIM // 2)).reshape(1, LANES))

    xb, qa, ka, va, qb, kb, vb, gl = _in_proj(x, w_in)
    ya = _swa_fwd(qa, ka, va, cos, sin, sinks)
    if isinstance(later_weights, tuple):
        exchange, finish = later_weights
        (yb, probs, betas), arrived = _sb_fwd(qb, kb, vb, exchange)
        later_weights = finish(arrived)
    else:
        (yb, probs, betas), _ = _sb_fwd(qb, kb, vb)
    wa, wb, wo, w_up, conv_w, w_down = later_weights
    h, u1, x1 = _mix_fwd(ya, yb, gl, x, wa, wb, wo, b_gate, ln1_g, ln1_b)

    ff_tn = D_FF // 2
    nff = D_FF // ff_tn
    tm = min(1024, T)
    p3 = _matmul(x1, w_up, kind="nt", name="ffn_up", grid=(T // tm, 2 * nff),
                 a_spec=pl.BlockSpec((tm, D_MODEL), lambda i, j: (i, 0)), b_spec=pl.BlockSpec((ff_tn, D_MODEL), lambda i, j: (j, 0)),
                 out_spec=pl.BlockSpec((None, tm, ff_tn), lambda i, j: (j // nff, i, j % nff)),
                 out_shape=jax.ShapeDtypeStruct((2, T, D_FF), ACT_DTYPE))
    s = _conv_glu_fwd(p3, conv_w, conv_b)
    du2, du2b, dln2_g, dln2_b, loss = _ffn_down_loss(s, w_down, u1, ln1_g, ln1_b, ln2_g, ln2_b, target)

    ds = _matmul(du2b, w_down, kind="nt", name="ffn_down_bwd", grid=(T // tm, nff),
                 a_spec=pl.BlockSpec((tm, D_MODEL), lambda i, j: (i, 0)), b_spec=pl.BlockSpec((ff_tn, D_MODEL), lambda i, j: (j, 0)),
                 out_spec=pl.BlockSpec((tm, ff_tn), lambda i, j: (i, j)), out_shape=jax.ShapeDtypeStruct((T, D_FF), ACT_DTYPE))
    dp3, dcw_g, dcw_u, dcb_g, dcb_u = _conv_glu_bwd(p3, ds, conv_w, conv_b)
    tk = 256
    dw_down = _matmul(s, du2b, kind="tn", name="dw_down", grid=(D_FF // tk,),
                      a_spec=pl.BlockSpec((T, tk), lambda i: (0, i)), b_spec=pl.BlockSpec((T, D_MODEL), lambda i: (0, 0)),
                      out_spec=pl.BlockSpec((tk, D_MODEL), lambda i: (i, 0)), out_shape=jax.ShapeDtypeStruct((D_FF, D_MODEL), BF16))
    dw_up = _matmul(dp3, x1, kind="tn", name="dw_up", grid=(2 * nff,),
                    a_spec=pl.BlockSpec((None, T, ff_tn), lambda j: (j // nff, 0, j % nff)), b_spec=pl.BlockSpec((T, D_MODEL), lambda j: (0, 0)),
                    out_spec=pl.BlockSpec((ff_tn, D_MODEL), lambda j: (j, 0)), out_shape=jax.ShapeDtypeStruct((2 * D_FF, D_MODEL), BF16))
    du1, du1b, dln1_g, dln1_b = _ffn_up_bwd_ln1(dp3, w_up, du2, u1, ln1_g)
    dya, dyb, dgl, dta, dtb, db_gate = _mix_bwd(du1, ya, yb, gl, wa, wb, wo, b_gate)

    def dw_tn(a, g, name):
        rows, cols = a.shape[1], g.shape[1]
        tn = min(1024, cols)
        return _matmul(a, g, kind="tn", name=name, grid=(rows // 512, cols // tn),
                       a_spec=pl.BlockSpec((T, 512), lambda i, j: (0, i)), b_spec=pl.BlockSpec((T, tn), lambda i, j: (0, j)),
                       out_spec=pl.BlockSpec((512, tn), lambda i, j: (i, j)), out_shape=jax.ShapeDtypeStruct((rows, cols), BF16))

    dwa = dw_tn(ya, dta, "dw_branch_a")
    dwb = dw_tn(yb, dtb, "dw_branch_b")
    dwo = dw_tn(h, du1b, "dw_out")

    grads = dict(
        b_gate=db_gate, w_branch_a=dwa, w_branch_b=dwb, w_out=dwo, ln1_g=dln1_g, ln1_b=dln1_b,
        w_up=dw_up, conv_w=jnp.concatenate([dcw_g, dcw_u], axis=1), conv_b=(dcb_g, dcb_u), w_down=dw_down, ln2_g=dln2_g, ln2_b=dln2_b)
    (dqb, dkb, dvb), early_out = _sb_bwd(qb, kb, vb, probs, betas, dyb, early_exchange(grads) if early_exchange else None)
    dqa, dka, dva, grads["sinks"] = _swa_bwd(qa, ka, va, cos, sin, sinks, dya)
    dproj = (dqa, dka, dva, dqb, dkb, dvb, dgl)
    grads["w_in"] = _dw_in(xb, dproj)
    grad_x, tail_out = _grad_x(dproj, w_in, du1, tail_exchange(grads, loss) if tail_exchange else None)
    return loss, grad_x, grads, early_out, tail_out


def _dw_in(xb, dproj):
    T = xb.shape[0]
    tn = 2 * LANES
    groups, start, k = [], 0, 0
    while k < len(IN_WIDTHS):
        if IN_WIDTHS[k] >= tn:
            groups.append((start, IN_WIDTHS[k] // tn, [(k, 0, tn)]))
            k += 1
        else:
            members, off = [], 0
            while off < tn:
                members.append((k, off, IN_WIDTHS[k]))
                off += IN_WIDTHS[k]
                k += 1
            groups.append((start, 1, members))
        start += groups[-1][1]

    def body(x_ref, *refs):
        pieces, o_ref = refs[:-1], refs[-1]
        j = pl.program_id(0)
        for first, steps, members in groups:
            @pl.when((j >= first) & (j < first + steps))
            def _(members=members):
                for k, off, width in members:
                    o_ref[off:off + width, :] = _dot_tn(pieces[k][...], x_ref[...]).astype(o_ref.dtype)

    specs = [None] * len(IN_WIDTHS)
    for first, steps, members in groups:
        for k, _, width in members:
            specs[k] = pl.BlockSpec((T, width), lambda j, first=first, steps=steps: (0, jnp.clip(j - first, 0, steps - 1)))
    return pl.pallas_call(
        body, name="dw_in", grid=(IN_TOTAL // tn,),
        in_specs=[pl.BlockSpec((T, D_MODEL), lambda j: (0, 0))] + specs, out_specs=pl.BlockSpec((tn, D_MODEL), lambda j: (j, 0)),
        out_shape=jax.ShapeDtypeStruct((IN_TOTAL, D_MODEL), BF16), compiler_params=_params(("arbitrary",)),
    )(xb, *dproj)


def _grad_x(dproj, w_in, du1, exchange=None):
    T = du1.shape[0]
    tm = min(512, T)
    offs = np.cumsum((0,) + IN_WIDTHS)

    def body(*refs):
        pieces, (w_ref, du_ref, o_ref) = refs[:len(IN_WIDTHS)], refs[len(IN_WIDTHS):]
        acc = ALPHA * du_ref[...]
        for p_ref, a, b in zip(pieces, offs[:-1], offs[1:]):
            acc = acc + _dot(p_ref[...].astype(BF16), w_ref[a:b, :])
        o_ref[...] = acc

    row = lambda n: pl.BlockSpec((tm, n), lambda i: (i, 0))
    (grad_x,), arrived = _hosted_call(
        body, "grad_x", (T // tm,), exchange,
        in_specs=[row(n) for n in IN_WIDTHS] + [pl.BlockSpec((IN_TOTAL, D_MODEL), lambda i: (0, 0)), row(D_MODEL)],
        out_specs=[row(D_MODEL)], out_shape=[jax.ShapeDtypeStruct((T, D_MODEL), F32)], semantics=("parallel",),
        args=(*dproj, w_in, du1))
    return grad_x, arrived


ANY = pl.BlockSpec(memory_space=pl.ANY)


def _all_gather(slabs, name):
    n = len(slabs)

    def body(*refs):
        ins, outs = refs[:n], refs[n:2 * n]
        send_sems, recv_sems, local_sems = refs[2 * n:]
        x, y, c = lax.axis_index("x"), lax.axis_index("y"), lax.axis_index("c")
        me, sibling = (x, y, c), (x, y, 1 - c)
        chips = [(1 - x, y), (x, 1 - y), (1 - x, 1 - y)]

        def slot(pos):
            return 4 * pos[0] + 2 * pos[1] + pos[2]

        def copy(a, k, block, to, from_input=False):
            return pltpu.make_async_remote_copy(
                src_ref=ins[a] if from_input else outs[a].at[slot(block)], dst_ref=outs[a].at[slot(block)],
                send_sem=send_sems.at[a, k], recv_sem=recv_sems.at[a, k], device_id=to, device_id_type=MESH)

        mine = [pltpu.make_async_copy(ins[a], outs[a].at[slot(me)], local_sems.at[a]) for a in range(n)]
        for cp in mine:
            cp.start()
        first = []
        for a in range(n):
            first.append(copy(a, 0, me, sibling, from_input=True))
            first += [copy(a, 1 + j, me, (*chip, c), from_input=True) for j, chip in enumerate(chips)]
        for cp in first:
            cp.start()
        passed = []
        for j, chip in enumerate(chips):
            for a in range(n):
                copy(a, 1 + j, (*chip, c), me).wait_recv()
                fwd = copy(a, 4 + j, (*chip, c), sibling)
                fwd.start()
                passed.append(fwd)
        for a in range(n):
            copy(a, 0, sibling, me).wait_recv()
            for j, chip in enumerate(chips):
                copy(a, 4 + j, (*chip, 1 - c), me).wait_recv()
        for cp in first + passed:
            cp.wait_send()
        for cp in mine:
            cp.wait()

    return pl.pallas_call(
        body, name=name,
        in_specs=[ANY] * n, out_specs=[ANY] * n,
        out_shape=[jax.ShapeDtypeStruct((N_DEV,) + s.shape, s.dtype) for s in slabs],
        scratch_shapes=[pltpu.SemaphoreType.DMA((n, 7)), pltpu.SemaphoreType.DMA((n, 7)), pltpu.SemaphoreType.DMA((n,))],
    )(*slabs)


def _row_tile(rows):
    for cand in range(256, 7, -8):
        if rows % cand == 0:
            return cand
    return rows


def _window(w):
    wp = max(-(-((w * r) % LANES + w) // LANES) for r in range(N_DEV)) * LANES
    assert all((w * r) // LANES * LANES + wp <= N_DEV * w for r in range(N_DEV))
    return wp


def _join_cols(slabs, name):
    _, R, w = slabs.shape
    tr = _row_tile(R)
    wp = _window(w)

    def body(g_ref, o_ref, pad_ref):
        if w % LANES == 0:
            for r in range(N_DEV):
                o_ref[:, w * r:w * (r + 1)] = g_ref[r]
            return
        o_ref[...] = jnp.zeros_like(o_ref)
        pad_ref[...] = jnp.zeros_like(pad_ref)
        for r in range(N_DEV):
            q, s = divmod(w * r, LANES)
            pad_ref[:, :w] = g_ref[r]
            y = pad_ref[...]
            if s:
                y = pltpu.roll(y, s, axis=1)
            o_ref[:, LANES * q:LANES * q + wp] += y

    return pl.pallas_call(
        body, name=name, grid=(R // tr,),
        in_specs=[pl.BlockSpec((N_DEV, tr, w), lambda i: (0, i, 0))], out_specs=pl.BlockSpec((tr, N_DEV * w), lambda i: (i, 0)),
        out_shape=jax.ShapeDtypeStruct((R, N_DEV * w), slabs.dtype), scratch_shapes=[pltpu.VMEM((tr, wp), slabs.dtype)],
        compiler_params=_params(("parallel",)),
    )(slabs)


def _split_cols(pieces, name):
    R = pieces[0].shape[0]
    widths = [p.shape[1] for p in pieces]
    total = sum(widths)
    w = total // N_DEV
    tr = _row_tile(R)
    wp = _window(w)
    offs = np.cumsum([0] + widths)
    dtype = pieces[0].dtype

    def body(*refs):
        ins, (o_ref, full_ref) = refs[:len(pieces)], refs[len(pieces):]
        for p_ref, a, b in zip(ins, offs[:-1], offs[1:]):
            full_ref[:, a:b] = p_ref[...].astype(dtype)
        for r in range(N_DEV):
            q, s = divmod(w * r, LANES)
            y = full_ref[:, LANES * q:LANES * q + wp]
            if s:
                y = pltpu.roll(y, wp - s, axis=1)
            o_ref[r] = y[:, :w]

    return pl.pallas_call(
        body, name=name, grid=(R // tr,),
        in_specs=[pl.BlockSpec((tr, n), lambda i: (i, 0)) for n in widths], out_specs=pl.BlockSpec((N_DEV, tr, w), lambda i: (0, i, 0)),
        out_shape=jax.ShapeDtypeStruct((N_DEV, R, w), dtype), scratch_shapes=[pltpu.VMEM((tr, total), dtype)],
        compiler_params=_params(("parallel",)),
    )(*pieces)


def _adamw(g, w, m, v):
    m_new = ADAM_B1 * m + (1.0 - ADAM_B1) * g
    v_new = ADAM_B2 * v + (1.0 - ADAM_B2) * jnp.square(g)
    m_hat = m_new / (1.0 - ADAM_B1 ** ADAM_STEP)
    v_hat = v_new / (1.0 - ADAM_B2 ** ADAM_STEP)
    return -ADAM_LR * (m_hat / (jnp.sqrt(v_hat) + ADAM_EPS) + ADAM_WD * w), m_new, v_new


def _sum_parts(p_ref):
    g = p_ref[0].astype(F32)
    for d in range(1, N_DEV):
        g = g + p_ref[d].astype(F32)
    return g


def _reduce_adamw(parts, w, m, v, name):
    R, C = w.shape
    tr = _row_tile(R)

    def body(p_ref, w_ref, m_ref, v_ref, g_ref, d_ref, mo_ref, vo_ref):
        g = _sum_parts(p_ref)
        g_ref[...] = g
        d_ref[...], mo_ref[...], vo_ref[...] = _adamw(g, w_ref[...], m_ref[...], v_ref[...])

    row = pl.BlockSpec((tr, C), lambda i: (i, 0))
    return pl.pallas_call(
        body, name=name, grid=(R // tr,),
        in_specs=[pl.BlockSpec((N_DEV, tr, C), lambda i: (0, i, 0)), row, row, row],
        out_specs=[row] * 4, out_shape=[jax.ShapeDtypeStruct((R, C), F32)] * 4,
        compiler_params=_params(("parallel",)),
    )(parts, w, m, v)


def _reduce_adamw_small(parts, ws, ms, vs):
    sizes = [a.shape[1] for a in ws]
    k = len(sizes)
    offs = np.cumsum([0] + [-(-n // LANES) * LANES for n in sizes])

    def body(*refs):
        p_ref, w_refs, m_refs, v_refs = refs[0], refs[1:1 + k], refs[1 + k:1 + 2 * k], refs[1 + 2 * k:1 + 3 * k]
        outs, loss_ref = refs[1 + 3 * k:-1], refs[-1]
        g_all = _sum_parts(p_ref)
        for j, n in enumerate(sizes):
            g = g_all[:, offs[j]:offs[j] + LANES * (-(-n // LANES))][:, :n]
            outs[4 * j][...] = g
            outs[4 * j + 1][...], outs[4 * j + 2][...], outs[4 * j + 3][...] = _adamw(g, w_refs[j][...], m_refs[j][...], v_refs[j][...])
        loss_ref[...] = g_all[:, offs[k]:offs[k] + LANES]

    vm = pl.BlockSpec(memory_space=pltpu.VMEM)
    out_shape = [jax.ShapeDtypeStruct((1, n), F32) for n in sizes for _ in range(4)] + [jax.ShapeDtypeStruct((1, LANES), F32)]
    res = pl.pallas_call(
        body, name="reduce_adamw_replicated", in_specs=[vm] * (1 + 3 * k), out_specs=[vm] * len(out_shape), out_shape=out_shape,
        compiler_params=_params(),
    )(parts, *ws, *ms, *vs)
    return [res[4 * j:4 * j + 4] for j in range(k)], res[-1]


TRANSPOSED = ("w_in", "w_up")
COL_SHARDED = ("w_branch_a", "w_branch_b", "conv_w")
ROW_SHARDED = TRANSPOSED + ("w_out", "w_down")
SMALL = ("b_gate", "sinks", "ln1_g", "ln1_b", "conv_b", "ln2_g", "ln2_b")
ORDER = ("w_in", "b_gate", "sinks", "w_branch_a", "w_branch_b", "w_out", "ln1_g", "ln1_b", "w_up", "conv_w", "conv_b", "w_down", "ln2_g", "ln2_b")


def _pad_lanes(a):
    pad = (-a.shape[-1]) % LANES
    return a if pad == 0 else jnp.pad(a, ((0, 0), (0, pad)))


def kernel(x, positions, w_in, b_gate, sinks, w_branch_a, w_branch_b, w_out, ln1_g, ln1_b, w_up, conv_w, conv_b, w_down, ln2_g, ln2_b, loss_target, m_w_in, m_b_gate, m_sinks, m_w_branch_a, m_w_branch_b, m_w_out, m_ln1_g, m_ln1_b, m_w_up, m_conv_w, m_conv_b, m_w_down, m_ln2_g, m_ln2_b, v_w_in, v_b_gate, v_sinks, v_w_branch_a, v_w_branch_b, v_w_out, v_ln1_g, v_ln1_b, v_w_up, v_conv_w, v_conv_b, v_w_down, v_ln2_g, v_ln2_b):
    args = dict(locals())
    sharded = COL_SHARDED + ROW_SHARDED

    def shard(name, a):
        return a if name not in sharded else a[0].T if name in TRANSPOSED else a[0]

    w = {n: shard(n, args[n]) for n in ORDER}
    m = {n: shard(n, args["m_" + n]) for n in ORDER}
    v = {n: shard(n, args["v_" + n]) for n in ORDER}

    travel = {n: (w[n] if n == "conv_w" else w[n].astype(BF16)) for n in sharded}
    (g_in,) = _all_gather([travel["w_in"]], "all_gather_w_in")
    w_in_full = g_in.reshape(-1, g_in.shape[-1])
    later = ("w_branch_a", "w_branch_b", "w_out", "w_up", "conv_w", "w_down")

    def join(name, slabs):
        return _join_cols(slabs, "join_" + name) if name in COL_SHARDED else slabs.reshape(-1, slabs.shape[-1])

    def split(name, grad):
        if name in COL_SHARDED:
            return _split_cols(grad if isinstance(grad, tuple) else (grad,), "split_d" + name)
        return grad.reshape((N_DEV, -1, grad.shape[-1]))

    def early_exchange(grads):
        return _Exchange([split(n, grads[n]) for n in later], ["scatter"] * len(later))

    def tail_exchange(grads, loss):
        small_pack = jnp.concatenate(
            [_pad_lanes(p) for n in SMALL for p in (grads[n] if isinstance(grads[n], tuple) else (grads[n],))] + [loss], axis=1)
        return _Exchange([split("w_in", grads["w_in"]), small_pack], ["scatter", "gather"])

    gather_later = _Exchange([travel[n] for n in later], ["gather"] * len(later))
    _, grad_x, _, early_out, (recv_w_in, small_parts) = _local_step(
        x[0], positions[0], w_in_full, w["b_gate"], w["sinks"][0], w["ln1_g"], w["ln1_b"], w["conv_b"], w["ln2_g"], w["ln2_b"], loss_target[0],
        (gather_later, lambda arrived: [join(n, a) for n, a in zip(later, arrived)]), early_exchange, tail_exchange)
    recv = dict(zip(later, early_out), w_in=recv_w_in)

    res = {n: _reduce_adamw(recv[n], w[n], m[n], v[n], "reduce_adamw_" + n) for n in sharded}
    small_res, loss_sum = _reduce_adamw_small(small_parts, [w[n] for n in SMALL], [m[n] for n in SMALL], [v[n] for n in SMALL])
    res.update(zip(SMALL, small_res))
    out = [loss_sum[0, 0], grad_x[None]]
    for k in range(4):
        out += [res[n][k].T[None] if n in TRANSPOSED else res[n][k][None] if n in sharded else res[n][k] for n in ORDER]
    return tuple(out)
```

```python
import functools

import jax
import jax.numpy as jnp
import numpy as np
from jax import lax
from jax.experimental import pallas as pl
from jax.experimental.pallas import tpu as pltpu

D_MODEL = 1024
HEAD_DIM = 64
SWA_Q_HEADS = 8
SWA_KV_HEADS = 2
SB_HEADS = 8
WINDOW = 128
ROPE_THETA = 10000.0
D_FF = 2816
LN_EPS = 1e-5
DEPTH = 1
ALPHA = (2.0 * DEPTH) ** 0.25
SWA_Q_WIDTH = SWA_Q_HEADS * HEAD_DIM
SWA_KV_WIDTH = SWA_KV_HEADS * HEAD_DIM
SB_WIDTH = SB_HEADS * HEAD_DIM
GATE_WIDTH = 2 * D_MODEL
IN_WIDTHS = (SWA_Q_WIDTH, SWA_KV_WIDTH, SWA_KV_WIDTH, SB_WIDTH, SB_WIDTH, SB_WIDTH, GATE_WIDTH)
IN_TOTAL = sum(IN_WIDTHS)
ATTN_SCALE = HEAD_DIM ** -0.5

ADAM_LR = 0.001
ADAM_B1 = 0.9
ADAM_B2 = 0.999
ADAM_EPS = 1e-08
ADAM_WD = 0.01
ADAM_STEP = 10

N_DEV = 8
LANES = 128
SB_BLOCK = 256
VMEM_LIMIT = 56 * 1024 * 1024

F32 = jnp.float32
BF16 = jnp.bfloat16
ACT_DTYPE = BF16
MESH = pl.DeviceIdType.MESH


def _params(sem=None):
    return pltpu.CompilerParams(dimension_semantics=sem, vmem_limit_bytes=VMEM_LIMIT)


def _dot(a, b):
    return jnp.dot(a, b, preferred_element_type=F32)


def _dot_nt(a, b):
    return lax.dot_general(a, b, (((1,), (1,)), ((), ())), preferred_element_type=F32)


def _dot_tn(a, b):
    return lax.dot_general(a, b, (((0,), (0,)), ((), ())), preferred_element_type=F32)


def _split_bf16(v):
    hi = v.astype(BF16)
    lo = (v - hi.astype(F32)).astype(BF16)
    return hi, lo


def _matmul(a, b, *, kind, out_shape, grid, a_spec, b_spec, out_spec, name, add=None, add_spec=None, add_scale=1.0):
    dot = {"nn": _dot, "nt": _dot_nt, "tn": _dot_tn}[kind]

    def body(*refs):
        if add is None:
            a_ref, b_ref, o_ref = refs
        else:
            a_ref, b_ref, add_ref, o_ref = refs
        r = dot(a_ref[...].astype(BF16), b_ref[...].astype(BF16))
        if add is not None:
            r = r + add_scale * add_ref[...]
        o_ref[...] = r.astype(o_ref.dtype)

    ins = [a, b] + ([] if add is None else [add])
    specs = [a_spec, b_spec] + ([] if add is None else [add_spec])
    return pl.pallas_call(
        body, name=name, grid=grid, in_specs=specs, out_specs=out_spec, out_shape=out_shape,
        compiler_params=_params(("parallel",) * len(grid)),
    )(*ins)


def _rope_tables(pos_col, inv_freq_lanes):
    T = pos_col.shape[0]
    tm = min(512, T)

    def body(pos_ref, f_ref, cos_ref, sin_ref):
        ang = pos_ref[...].astype(F32) * f_ref[...]
        cos_ref[...] = jnp.cos(ang)
        sin_ref[...] = jnp.sin(ang)

    return pl.pallas_call(
        body, name="rope_tables", grid=(T // tm,),
        in_specs=[pl.BlockSpec((tm, 1), lambda i: (i, 0)), pl.BlockSpec((1, LANES), lambda i: (0, 0))],
        out_specs=[pl.BlockSpec((tm, LANES), lambda i: (i, 0))] * 2,
        out_shape=[jax.ShapeDtypeStruct((T, LANES), F32)] * 2,
        compiler_params=_params(("parallel",)),
    )(pos_col, inv_freq_lanes)


def _lane_iota(shape):
    return lax.broadcasted_iota(jnp.int32, shape, len(shape) - 1)


def _rot_half(t):
    first = (_lane_iota(t.shape) % HEAD_DIM) < (HEAD_DIM // 2)
    return jnp.where(first, -pltpu.roll(t, LANES - HEAD_DIM // 2, axis=1), pltpu.roll(t, HEAD_DIM // 2, axis=1))


def _rope(t, cos, sin):
    return t * cos + _rot_half(t) * sin


def _rope_transpose(d, cos, sin):
    return d * cos - _rot_half(d * sin)


_IN_DTYPES = (F32, F32, BF16, BF16, BF16, BF16, F32)


def _in_proj(x, w_in_t):
    T = x.shape[0]
    tm = min(512, T)
    offs = np.cumsum((0,) + IN_WIDTHS)

    def body(x_ref, w_ref, xb_ref, *outs):
        xb = x_ref[...].astype(BF16)
        xb_ref[...] = xb
        for o_ref, a, b in zip(outs, offs[:-1], offs[1:]):
            o_ref[...] = _dot_nt(xb, w_ref[a:b, :]).astype(o_ref.dtype)

    row = lambda n: pl.BlockSpec((tm, n), lambda i: (i, 0))
    return pl.pallas_call(
        body, name="in_proj", grid=(T // tm,),
        in_specs=[row(D_MODEL), pl.BlockSpec((IN_TOTAL, D_MODEL), lambda i: (0, 0))],
        out_specs=[row(D_MODEL)] + [row(n) for n in IN_WIDTHS],
        out_shape=[jax.ShapeDtypeStruct((T, D_MODEL), BF16)] + [jax.ShapeDtypeStruct((T, n), dt) for n, dt in zip(IN_WIDTHS, _IN_DTYPES)],
        compiler_params=_params(("parallel",)),
    )(x, w_in_t)


def _swa_specs(T):
    blk = WINDOW
    cur = lambda n: pl.BlockSpec((blk, n), lambda i: (i, 0))
    prev = lambda n: pl.BlockSpec((blk, n), lambda i: (jnp.maximum(i - 1, 0), 0))
    return blk, cur, prev


SWA_GROUP = SWA_Q_HEADS // SWA_KV_HEADS


def _swa_stack(pairs):
    lane = _lane_iota(pairs[0].shape)
    zero = jnp.zeros((), pairs[0].dtype)
    rows = []
    for h in range(SWA_Q_HEADS):
        hh, g = h % 2, h // SWA_GROUP
        x = jnp.where((lane >= hh * HEAD_DIM) & (lane < (hh + 1) * HEAD_DIM), pairs[h // 2], zero)
        rows.append(x if hh == g else pltpu.roll(x, HEAD_DIM, axis=1))
    return jnp.concatenate(rows, axis=0)


def _swa_unstack(stacked, blk):
    low = _lane_iota((blk, LANES)) < HEAD_DIM
    pairs = []
    for pp in range(SWA_Q_HEADS // 2):
        halves = []
        for hh in range(2):
            h = 2 * pp + hh
            x = stacked[h * blk:(h + 1) * blk]
            halves.append(x if hh == h // SWA_GROUP else pltpu.roll(x, HEAD_DIM, axis=1))
        pairs.append(jnp.where(low, halves[0], halves[1]))
    return pairs


def _swa_probs(i, q_stack, kwin, sink_ref, blk):
    r = lax.broadcasted_iota(jnp.int32, (blk, 2 * blk), 0)
    c = lax.broadcasted_iota(jnp.int32, (blk, 2 * blk), 1)
    rel = blk + r - c
    valid = (rel >= 0) & (rel < WINDOW) & ((c >= blk) | (i > 0))
    bias = jnp.concatenate([jnp.where(valid, 0.0, -1e30)] * SWA_Q_HEADS, axis=0)
    head = lax.broadcasted_iota(jnp.int32, (SWA_Q_HEADS * blk, 1), 0) // blk
    sink = jnp.zeros((SWA_Q_HEADS * blk, 1), F32)
    for h in range(SWA_Q_HEADS):
        sink = jnp.where(head == h, sink_ref[h], sink)
    s = _dot_nt(q_stack, kwin) * ATTN_SCALE + bias
    m = jnp.maximum(jnp.max(s, axis=1, keepdims=True), sink)
    p = jnp.exp(s - m)
    es = jnp.exp(sink - m)
    den = jnp.sum(p, axis=1, keepdims=True) + es
    return p / den, es / den


def _swa_inputs(q_ref, kp_ref, kc_ref, vp_ref, vc_ref, cp_ref, cc_ref, sp_ref, sc_ref):
    cc, sc = cc_ref[...], sc_ref[...]
    kwin = jnp.concatenate([_rope(kp_ref[...], cp_ref[...], sp_ref[...]), _rope(kc_ref[...], cc, sc)], axis=0).astype(BF16)
    vwin = jnp.concatenate([vp_ref[...], vc_ref[...]], axis=0)
    q_stack = _swa_stack([_rope(q_ref[:, pp * LANES:(pp + 1) * LANES], cc, sc) for pp in range(SWA_Q_HEADS // 2)]).astype(BF16)
    return q_stack, kwin, vwin


def _swa_fwd(qa, ka, va, cos, sin, sinks):
    T = qa.shape[0]
    blk, cur, prev = _swa_specs(T)

    def body(sink_ref, q_ref, kp_ref, kc_ref, vp_ref, vc_ref, cp_ref, cc_ref, sp_ref, sc_ref, o_ref):
        q_stack, kwin, vwin = _swa_inputs(q_ref, kp_ref, kc_ref, vp_ref, vc_ref, cp_ref, cc_ref, sp_ref, sc_ref)
        probs, _ = _swa_probs(pl.program_id(0), q_stack, kwin, sink_ref, blk)
        for pp, tile in enumerate(_swa_unstack(_dot(probs.astype(BF16), vwin), blk)):
            o_ref[:, pp * LANES:(pp + 1) * LANES] = tile.astype(o_ref.dtype)

    return pl.pallas_call(
        body, name="swa_fwd", grid=(T // blk,),
        in_specs=[pl.BlockSpec(memory_space=pltpu.SMEM), cur(SWA_Q_WIDTH), prev(LANES), cur(LANES), prev(LANES), cur(LANES),
                  prev(LANES), cur(LANES), prev(LANES), cur(LANES)],
        out_specs=cur(SWA_Q_WIDTH),
        out_shape=jax.ShapeDtypeStruct((T, SWA_Q_WIDTH), BF16),
        compiler_params=_params(("parallel",)),
    )(sinks, qa, ka, ka, va, va, cos, cos, sin, sin)


def _swa_bwd(qa, ka, va, cos, sin, sinks, dya):
    T = qa.shape[0]
    blk, cur, prev = _swa_specs(T)
    full = lambda n: pl.BlockSpec((T, n), lambda i: (0, 0))

    def body(sink_ref, q_ref, kp_ref, kc_ref, vp_ref, vc_ref, cp_ref, cc_ref, sp_ref, sc_ref, do_ref,
             dq_ref, dk_out, dv_out, dsink_ref, dk_ref, dv_ref):
        i = pl.program_id(0)

        @pl.when(i == 0)
        def _():
            dk_ref[...] = jnp.zeros_like(dk_ref)
            dv_ref[...] = jnp.zeros_like(dv_ref)
            dsink_ref[...] = jnp.zeros_like(dsink_ref)

        cp, cc, sp, sc = cp_ref[...], cc_ref[...], sp_ref[...], sc_ref[...]
        q_stack, kwin, vwin = _swa_inputs(q_ref, kp_ref, kc_ref, vp_ref, vc_ref, cp_ref, cc_ref, sp_ref, sc_ref)
        probs, psink = _swa_probs(i, q_stack, kwin, sink_ref, blk)
        do_stack = _swa_stack([do_ref[:, pp * LANES:(pp + 1) * LANES] for pp in range(SWA_Q_HEADS // 2)])
        dp = _dot_nt(do_stack, vwin)
        dsum = jnp.sum(probs * dp, axis=1, keepdims=True)
        ds = (probs * (dp - dsum) * ATTN_SCALE).astype(BF16)
        for pp, tile in enumerate(_swa_unstack(_dot(ds, kwin), blk)):
            dq_ref[:, pp * LANES:(pp + 1) * LANES] = _rope_transpose(tile, cc, sc).astype(dq_ref.dtype)
        dkw = _dot_tn(ds, q_stack)
        dvw = _dot_tn(probs.astype(BF16), do_stack)
        lane1 = _lane_iota((1, LANES))
        sink_share = psink * dsum
        dsink = jnp.zeros((1, LANES), F32)
        for h in range(SWA_Q_HEADS):
            dsink = dsink + jnp.where(lane1 == h, -jnp.sum(sink_share[h * blk:(h + 1) * blk]), 0.0)
        dsink_ref[...] += dsink
        ip = jnp.maximum(i - 1, 0)
        rows_p = pl.ds(pl.multiple_of(ip * blk, blk), blk)
        rows_c = pl.ds(pl.multiple_of(i * blk, blk), blk)
        dk_ref[rows_p, :] += _rope_transpose(dkw[:blk], cp, sp)
        dv_ref[rows_p, :] += dvw[:blk]
        dk_ref[rows_c, :] += _rope_transpose(dkw[blk:], cc, sc)
        dv_ref[rows_c, :] += dvw[blk:]

        @pl.when(i == T // blk - 1)
        def _():
            dk_out[...] = dk_ref[...].astype(BF16)
            dv_out[...] = dv_ref[...].astype(BF16)

    return pl.pallas_call(
        body, name="swa_bwd", grid=(T // blk,),
        in_specs=[pl.BlockSpec(memory_space=pltpu.SMEM), cur(SWA_Q_WIDTH), prev(LANES), cur(LANES), prev(LANES), cur(LANES),
                  prev(LANES), cur(LANES), prev(LANES), cur(LANES), cur(SWA_Q_WIDTH)],
        out_specs=[cur(SWA_Q_WIDTH), full(LANES), full(LANES), pl.BlockSpec((1, LANES), lambda i: (0, 0))],
        out_shape=[jax.ShapeDtypeStruct((T, SWA_Q_WIDTH), BF16), jax.ShapeDtypeStruct((T, LANES), BF16),
                   jax.ShapeDtypeStruct((T, LANES), BF16), jax.ShapeDtypeStruct((1, LANES), F32)],
        scratch_shapes=[pltpu.VMEM((T, LANES), F32)] * 2,
        compiler_params=_params(("arbitrary",)),
    )(sinks, qa, ka, ka, va, va, cos, cos, sin, sin, dya)


class _Exchange:
    FLIPS = [(fx, fy, fc) for fx in (0, 1) for fy in (0, 1) for fc in (0, 1) if (fx, fy, fc) != (0, 0, 0)]

    def __init__(self, arrays, kinds):
        self.arrays, self.kinds, self.n = list(arrays), list(kinds), len(arrays)

    def out_shape(self):
        return [jax.ShapeDtypeStruct((N_DEV,) + a.shape if k == "gather" else a.shape, a.dtype) for a, k in zip(self.arrays, self.kinds)]

    def scratch(self):
        return [pltpu.SemaphoreType.DMA((self.n, 7)), pltpu.SemaphoreType.DMA((self.n, 7)), pltpu.SemaphoreType.DMA((self.n,))]

    def bind(self, ins, outs, send_sems, recv_sems, local_sems):
        x, y, c = lax.axis_index("x"), lax.axis_index("y"), lax.axis_index("c")
        local, remote = [], []
        for a, kind in enumerate(self.kinds):
            me = 2 * x + y if kind == "chips" else 4 * x + 2 * y + c
            mine = ins[a] if kind == "gather" else ins[a].at[me]
            local.append(pltpu.make_async_copy(mine, outs[a].at[me], local_sems.at[a]))
            for k, (fx, fy, fc) in enumerate(self.FLIPS):
                if kind == "chips" and (fc or not (fx or fy)):
                    continue
                peer = (x ^ fx, y ^ fy, c ^ fc)
                peer_slot = 2 * peer[0] + peer[1] if kind == "chips" else 4 * peer[0] + 2 * peer[1] + peer[2]
                src = ins[a] if kind == "gather" else ins[a].at[peer_slot]
                sems = dict(send_sem=send_sems.at[a, k], recv_sem=recv_sems.at[a, k], device_id=peer, device_id_type=MESH)
                remote.append((pltpu.make_async_remote_copy(src_ref=src, dst_ref=outs[a].at[me], **sems),
                               pltpu.make_async_remote_copy(src_ref=src, dst_ref=outs[a].at[peer_slot], **sems)))

        def start():
            for cp in local:
                cp.start()
            for send, _ in remote:
                send.start()

        def wait():
            for send, arrival in remote:
                arrival.wait_recv()
                send.wait_send()
            for cp in local:
                cp.wait()

        return start, wait


def _hosted_call(body, name, grid, exchange, *, in_specs, out_specs, out_shape, semantics, args, scratch=(), prefetch=()):
    n = 0 if exchange is None else exchange.n
    n_pre, n_in, n_out, n_scratch = len(prefetch), len(in_specs), len(out_specs), len(scratch)

    def hosted(*refs):
        pre, rest = refs[:n_pre], refs[n_pre:]
        ins, rest = rest[:n_in], rest[n_in:]
        ex_ins, rest = rest[:n], rest[n:]
        outs, rest = rest[:n_out], rest[n_out:]
        ex_outs, rest = rest[:n], rest[n:]
        own, sems = rest[:n_scratch], rest[n_scratch:]
        if exchange is None:
            return body(*pre, *ins, *outs, *own)
        start, wait = exchange.bind(ex_ins, ex_outs, *sems)
        ids = [pl.program_id(d) for d in range(len(grid))]
        first = functools.reduce(jnp.logical_and, [i == 0 for i in ids])
        last = functools.reduce(jnp.logical_and, [i == g - 1 for i, g in zip(ids, grid)])
        pl.when(first)(start)
        body(*pre, *ins, *outs, *own)
        pl.when(last)(wait)

    grid_spec = pltpu.PrefetchScalarGridSpec(
        num_scalar_prefetch=n_pre, grid=grid, in_specs=list(in_specs) + [ANY] * n, out_specs=list(out_specs) + [ANY] * n,
        scratch_shapes=list(scratch) + ([] if exchange is None else exchange.scratch()))
    res = pl.pallas_call(
        hosted, name=name, grid_spec=grid_spec, out_shape=list(out_shape) + ([] if exchange is None else exchange.out_shape()),
        compiler_params=_params(semantics if exchange is None else ("arbitrary",) * len(grid)),
    )(*prefetch, *args, *([] if exchange is None else exchange.arrays))
    return res[:n_out], res[n_out:]


SOFTPLUS_LINEAR_FROM = 30.0


def _sb_scores(qm, k, valid):
    z = _dot_nt(qm, k)
    sp = jnp.where(z > SOFTPLUS_LINEAR_FROM, z, jnp.log(1.0 + jnp.exp(z)))
    log_beta = z - sp
    if valid is not None:
        sp = jnp.where(valid, sp, 0.0)
    return sp, log_beta


def _tri2(B, cmp):
    r = lax.broadcasted_iota(jnp.int32, (2 * B, B), 0) % B
    c = lax.broadcasted_iota(jnp.int32, (2 * B, B), 1)
    return cmp(r, c).astype(BF16)


def _tri_sum(v, tri2):
    hi, lo = _split_bf16(v)
    return _dot(jnp.concatenate([hi, lo], axis=1), tri2)


def _head_masks(x):
    low = _lane_iota(x.shape) < HEAD_DIM
    zero = jnp.zeros((), x.dtype)
    return jnp.where(low, x, zero), jnp.where(low, zero, x)


def _strictly_below(B):
    r = lax.broadcasted_iota(jnp.int32, (B, B), 0)
    c = lax.broadcasted_iota(jnp.int32, (B, B), 1)
    return c < r


def _sb_grid(T, descending):
    B = min(SB_BLOCK, T)
    n = T // B
    pairs = [(i, j) for i in range(n) for j in (range(i, -1, -1) if descending else range(i + 1))]
    return B, jnp.asarray([p[0] for p in pairs], jnp.int32), jnp.asarray([p[1] for p in pairs], jnp.int32)


N_PAIRS = SB_HEADS // 2
PAIR_COLS = [slice(p * LANES, (p + 1) * LANES) for p in range(N_PAIRS)]


def _sb_fwd(qb, kb, vb, exchange=None):
    T = qb.shape[0]
    B, i_tab, j_tab = _sb_grid(T, descending=True)
    n = T // B

    def body(i_ref, j_ref, q_ref, k_ref, v_ref, o_ref, a_ref, b_ref, acc_ref, c_ref, tri_ref):
        s = pl.program_id(0)
        i, j = i_ref[s], j_ref[s]

        @pl.when(s == 0)
        def _():
            tri_ref[...] = _tri2(B, lambda r, c: r > c)

        @pl.when(j == i)
        def _():
            acc_ref[...] = jnp.zeros_like(acc_ref)
            c_ref[...] = jnp.zeros_like(c_ref)

        def block(valid):
            for p, cols in enumerate(PAIR_COLS):
                qms = _head_masks(q_ref[:, cols] * ATTN_SCALE)
                k = k_ref[:, cols]
                probs = []
                for hh in range(2):
                    h = 2 * p + hh
                    sp, lb = _sb_scores(qms[hh], k, valid)
                    c = c_ref[h]
                    a = jnp.exp(lb - (c + _tri_sum(sp, tri_ref[...])))
                    beta = jnp.exp(lb)
                    if valid is not None:
                        a = jnp.where(valid, a, 0.0)
                        beta = jnp.where(valid, beta, 0.0)
                    probs.append(a.astype(BF16))
                    a_ref[h] = probs[-1]
                    b_ref[h] = beta.astype(BF16)
                    c_ref[h] = c + jnp.sum(sp, axis=1, keepdims=True)
                acc_ref[:, cols] += _dot(jnp.concatenate(probs, axis=1), jnp.concatenate(_head_masks(v_ref[:, cols]), axis=0))

        pl.when(j == i)(lambda: block(_strictly_below(B)))
        pl.when(j != i)(lambda: block(None))

        @pl.when(j == 0)
        def _():
            o_ref[...] = acc_ref[...].astype(o_ref.dtype)

    q_spec = pl.BlockSpec((B, SB_WIDTH), lambda s, i_ref, j_ref: (i_ref[s], 0))
    k_spec = pl.BlockSpec((B, SB_WIDTH), lambda s, i_ref, j_ref: (j_ref[s], 0))
    tile = pl.BlockSpec((None, None, SB_HEADS, B, B), lambda s, i_ref, j_ref: (i_ref[s], j_ref[s], 0, 0, 0))
    saved = jax.ShapeDtypeStruct((n, n, SB_HEADS, B, B), BF16)
    return _hosted_call(
        body, "sb_fwd", (int(i_tab.shape[0]),), exchange, prefetch=(i_tab, j_tab),
        in_specs=[q_spec, k_spec, k_spec], out_specs=[q_spec, tile, tile],
        out_shape=[jax.ShapeDtypeStruct((T, SB_WIDTH), BF16), saved, saved],
        scratch=[pltpu.VMEM((B, SB_WIDTH), F32), pltpu.VMEM((SB_HEADS, B, 1), F32), pltpu.VMEM((2 * B, B), BF16)],
        semantics=("arbitrary",), args=(qb, kb, vb))


def _sb_bwd(qb, kb, vb, probs, betas, dyb, exchange=None):
    T = qb.shape[0]
    B, i_tab, j_tab = _sb_grid(T, descending=False)
    n_steps = int(i_tab.shape[0])

    def block_diag_t(x):
        xt = x.T
        top = lax.broadcasted_iota(jnp.int32, xt.shape, 0) < HEAD_DIM
        zero = jnp.zeros((), x.dtype)
        return jnp.concatenate([jnp.where(top, xt, zero), jnp.where(top, zero, xt)], axis=1)

    def body(i_ref, j_ref, q_ref, k_ref, v_ref, a_ref, b_ref, do_ref, dq_ref, dk_out, dv_out,
             dq_acc, cg_ref, dkt_ref, dvt_ref, tri_ref, qt_ref, dot_ref):
        s = pl.program_id(0)
        i, j = i_ref[s], j_ref[s]

        @pl.when(s == 0)
        def _():
            dkt_ref[...] = jnp.zeros_like(dkt_ref)
            dvt_ref[...] = jnp.zeros_like(dvt_ref)
            tri_ref[...] = _tri2(B, lambda r, c: r < c)[:B]

        @pl.when(j == 0)
        def _():
            dq_acc[...] = jnp.zeros_like(dq_acc)
            cg_ref[...] = jnp.zeros_like(cg_ref)
            for p, cols in enumerate(PAIR_COLS):
                qt_ref[p] = block_diag_t(q_ref[:, cols] * ATTN_SCALE)
                dot_ref[p] = block_diag_t(do_ref[:, cols])

        for p, cols in enumerate(PAIR_COLS):
            doms = _head_masks(do_ref[:, cols])
            k, v = k_ref[:, cols], v_ref[:, cols]
            dzs = []
            for hh in range(2):
                h = 2 * p + hh
                g = a_ref[h].astype(F32) * _dot_nt(doms[hh], v)
                cg = cg_ref[h]
                gsum = g + (cg + _dot(g.astype(BF16), tri_ref[...]))
                dzs.append((g - b_ref[h].astype(F32) * gsum).astype(BF16))
                cg_ref[h] = cg + jnp.sum(g, axis=1, keepdims=True)
            dq_acc[:, cols] += _dot(jnp.concatenate(dzs, axis=1), jnp.concatenate(_head_masks(k), axis=0))
            dkt_ref[j, cols, :] += _dot(qt_ref[p], jnp.concatenate(dzs, axis=0))
            dvt_ref[j, cols, :] += _dot(dot_ref[p], jnp.concatenate([a_ref[2 * p], a_ref[2 * p + 1]], axis=0))

        @pl.when(j == i)
        def _():
            dq_ref[...] = (dq_acc[...] * ATTN_SCALE).astype(dq_ref.dtype)

        @pl.when(s == n_steps - 1)
        def _():
            for jb in range(T // B):
                dk_out[jb * B:(jb + 1) * B, :] = dkt_ref[jb].T.astype(BF16)
                dv_out[jb * B:(jb + 1) * B, :] = dvt_ref[jb].T.astype(BF16)

    q_spec = pl.BlockSpec((B, SB_WIDTH), lambda s, i_ref, j_ref: (i_ref[s], 0))
    k_spec = pl.BlockSpec((B, SB_WIDTH), lambda s, i_ref, j_ref: (j_ref[s], 0))
    tile = pl.BlockSpec((None, None, SB_HEADS, B, B), lambda s, i_ref, j_ref: (i_ref[s], j_ref[s], 0, 0, 0))
    full = pl.BlockSpec((T, SB_WIDTH), lambda s, i_ref, j_ref: (0, 0))
    return _hosted_call(
        body, "sb_bwd", (n_steps,), exchange, prefetch=(i_tab, j_tab),
        in_specs=[q_spec, k_spec, k_spec, tile, tile, q_spec], out_specs=[q_spec, full, full],
        out_shape=[jax.ShapeDtypeStruct((T, SB_WIDTH), BF16)] * 3,
        scratch=[pltpu.VMEM((B, SB_WIDTH), F32), pltpu.VMEM((SB_HEADS, B, 1), F32), pltpu.VMEM((T // B, SB_WIDTH, B), F32),
                 pltpu.VMEM((T // B, SB_WIDTH, B), F32), pltpu.VMEM((B, B), BF16), pltpu.VMEM((N_PAIRS, LANES, 2 * B), BF16),
                 pltpu.VMEM((N_PAIRS, LANES, 2 * B), BF16)],
        semantics=("arbitrary",), args=(qb, kb, vb, probs, betas, dyb))


def _ln_stats(u):
    mu = jnp.mean(u, axis=-1, keepdims=True)
    xc = u - mu
    var = jnp.mean(xc * xc, axis=-1, keepdims=True)
    rstd = lax.rsqrt(var + LN_EPS)
    return xc * rstd, rstd


def _ln_bwd(dy, xhat, rstd, g):
    dxh = dy * g
    return rstd * (dxh - jnp.mean(dxh, axis=-1, keepdims=True) - xhat * jnp.mean(dxh * xhat, axis=-1, keepdims=True))


def _gates(gl_ref, bg_ref):
    ga = jax.nn.sigmoid(gl_ref[:, :D_MODEL] + bg_ref[:, :D_MODEL])
    gb = jax.nn.sigmoid(gl_ref[:, D_MODEL:] + bg_ref[:, D_MODEL:])
    return ga, gb


def _mix_fwd(ya, yb, gl, x, wa, wb, wo, b_gate, ln1_g, ln1_b):
    T = x.shape[0]
    tm = min(512, T)

    def body(ya_ref, yb_ref, gl_ref, x_ref, wa_ref, wb_ref, wo_ref, bg_ref, g_ref, b_ref, h_ref, u_ref, x1_ref):
        ga, gb = _gates(gl_ref, bg_ref)
        h = (ga * _dot(ya_ref[...], wa_ref[...]) + gb * _dot(yb_ref[...], wb_ref[...])).astype(BF16)
        h_ref[...] = h
        u = ALPHA * x_ref[...] + _dot(h, wo_ref[...])
        u_ref[...] = u
        xhat, _ = _ln_stats(u)
        x1_ref[...] = (xhat * g_ref[...] + b_ref[...]).astype(BF16)

    row = lambda n: pl.BlockSpec((tm, n), lambda i: (i, 0))
    const = lambda r, n: pl.BlockSpec((r, n), lambda i: (0, 0))
    return pl.pallas_call(
        body, name="mix_fwd", grid=(T // tm,),
        in_specs=[row(SWA_Q_WIDTH), row(SB_WIDTH), row(GATE_WIDTH), row(D_MODEL), const(SWA_Q_WIDTH, D_MODEL), const(SB_WIDTH, D_MODEL),
                  const(D_MODEL, D_MODEL), const(1, GATE_WIDTH), const(1, D_MODEL), const(1, D_MODEL)],
        out_specs=[row(D_MODEL)] * 3,
        out_shape=[jax.ShapeDtypeStruct((T, D_MODEL), BF16), jax.ShapeDtypeStruct((T, D_MODEL), F32), jax.ShapeDtypeStruct((T, D_MODEL), BF16)],
        compiler_params=_params(("parallel",)),
    )(ya, yb, gl, x, wa, wb, wo, b_gate, ln1_g, ln1_b)


def _mix_bwd(du1, ya, yb, gl, wa, wb, wo, b_gate):
    T = du1.shape[0]
    tm = min(512, T)

    def body(du_ref, ya_ref, yb_ref, gl_ref, wa_ref, wb_ref, wo_ref, bg_ref, dya_ref, dyb_ref, dgl_ref, dta_ref, dtb_ref, dbg_ref):
        @pl.when(pl.program_id(0) == 0)
        def _():
            dbg_ref[...] = jnp.zeros_like(dbg_ref)

        dh = _dot_nt(du_ref[...].astype(BF16), wo_ref[...])
        ga, gb = _gates(gl_ref, bg_ref)
        for gate, y_ref, w_ref, dy_ref, dt_ref, lo in ((ga, ya_ref, wa_ref, dya_ref, dta_ref, 0), (gb, yb_ref, wb_ref, dyb_ref, dtb_ref, D_MODEL)):
            t = _dot(y_ref[...], w_ref[...])
            dlogit = dh * t * gate * (1.0 - gate)
            dgl_ref[:, lo:lo + D_MODEL] = dlogit.astype(BF16)
            dbg_ref[:, lo:lo + D_MODEL] += jnp.sum(dlogit, axis=0, keepdims=True)
            dt = (dh * gate).astype(BF16)
            dt_ref[...] = dt
            dy_ref[...] = _dot_nt(dt, w_ref[...]).astype(BF16)

    row = lambda n: pl.BlockSpec((tm, n), lambda i: (i, 0))
    const = lambda r, n: pl.BlockSpec((r, n), lambda i: (0, 0))
    sds = lambda n, dt: jax.ShapeDtypeStruct((T, n), dt)
    return pl.pallas_call(
        body, name="mix_bwd", grid=(T // tm,),
        in_specs=[row(D_MODEL), row(SWA_Q_WIDTH), row(SB_WIDTH), row(GATE_WIDTH), const(SWA_Q_WIDTH, D_MODEL), const(SB_WIDTH, D_MODEL),
                  const(D_MODEL, D_MODEL), const(1, GATE_WIDTH)],
        out_specs=[row(SWA_Q_WIDTH), row(SB_WIDTH), row(GATE_WIDTH), row(D_MODEL), row(D_MODEL), const(1, GATE_WIDTH)],
        out_shape=[sds(SWA_Q_WIDTH, BF16), sds(SB_WIDTH, BF16), sds(GATE_WIDTH, BF16), sds(D_MODEL, BF16), sds(D_MODEL, BF16),
                   jax.ShapeDtypeStruct((1, GATE_WIDTH), F32)],
        compiler_params=_params(("arbitrary",)),
    )(du1, ya, yb, gl, wa, wb, wo, b_gate)


CONV_COLS = LANES


CONV_CHUNK = 64
CONV_CHUNK_FWD = 256
HALO = 8


def _taps(ref, r0, rows, lead):
    return [ref[pl.ds(r0 + lead + k, rows), :] for k in ((-2, -1, 0) if lead else (0, 1, 2))]


def _chunks(T, rows, step, init=None, reverse=False):
    def body(c, carry):
        c = T // rows - 1 - c if reverse else c
        out = step(pl.multiple_of(c * rows, rows), *(() if init is None else (carry,)))
        return carry if init is None else out
    return lax.fori_loop(0, T // rows, body, 0 if init is None else init)


def _conv_chunk(taps, w_ref, b_ref):
    return w_ref[0:1, :] * taps[0] + w_ref[1:2, :] * taps[1] + w_ref[2:3, :] * taps[2] + b_ref[...]


def _fold(x):
    return jnp.sum(x.reshape(x.shape[0] // 8, 8, x.shape[1]), axis=0)


def _conv_specs(T):
    nb = D_FF // CONV_COLS
    pair = pl.BlockSpec((2, T, CONV_COLS), lambda j: (0, 0, j))
    gate = lambda r: pl.BlockSpec((r, CONV_COLS), lambda j: (0, j))
    up = lambda r: pl.BlockSpec((r, CONV_COLS), lambda j: (0, j + nb))
    return nb, pair, gate, up


def _conv_glu_fwd(p3, conv_w, conv_b):
    T = p3.shape[1]
    nb, pair, gate, up = _conv_specs(T)

    R = min(CONV_CHUNK_FWD, T)

    def body(p_ref, wg_ref, wu_ref, bg_ref, bu_ref, s_ref, pg_s, pu_s):
        for half, scr in enumerate((pg_s, pu_s)):
            scr[0:HALO, :] = jnp.zeros((HALO, CONV_COLS), F32)
            scr[HALO:HALO + T, :] = p_ref[half].astype(F32)
        def step(r0):
            ag = _conv_chunk(_taps(pg_s, r0, R, HALO), wg_ref, bg_ref)
            au = _conv_chunk(_taps(pu_s, r0, R, HALO), wu_ref, bu_ref)
            s_ref[pl.ds(r0, R), :] = (ag * jax.nn.sigmoid(ag) * au).astype(BF16)

        _chunks(T, R, step)

    return pl.pallas_call(
        body, name="conv_glu_fwd", grid=(nb,),
        in_specs=[pair, gate(3), up(3), gate(1), up(1)],
        out_specs=pl.BlockSpec((T, CONV_COLS), lambda j: (0, j)),
        out_shape=jax.ShapeDtypeStruct((T, D_FF), BF16),
        scratch_shapes=[pltpu.VMEM((T + HALO, CONV_COLS), F32)] * 2,
        compiler_params=_params(("parallel",)),
    )(p3, conv_w, conv_w, conv_b, conv_b)


def _conv_glu_bwd(p3, ds, conv_w, conv_b):
    T = p3.shape[1]
    nb, pair, gate, up = _conv_specs(T)

    R = min(CONV_CHUNK, T)

    def body(p_ref, ds_ref, wg_ref, wu_ref, bg_ref, bu_ref, dp_ref, dwg_ref, dwu_ref, dbg_ref, dbu_ref, pg_s, pu_s, dag_s, dau_s):
        for half, scr in enumerate((pg_s, pu_s)):
            scr[0:HALO, :] = jnp.zeros((HALO, CONV_COLS), F32)
            scr[HALO:HALO + T, :] = p_ref[half].astype(F32)
        for scr in (dag_s, dau_s):
            scr[T:T + HALO, :] = jnp.zeros((HALO, CONV_COLS), F32)
        halves = ((pg_s, dag_s, wg_ref, dwg_ref, dbg_ref), (pu_s, dau_s, wu_ref, dwu_ref, dbu_ref))

        def step(r0, sums):
            taps = [_taps(p_s, r0, R, HALO) for p_s, *_ in halves]
            ag = _conv_chunk(taps[0], wg_ref, bg_ref)
            au = _conv_chunk(taps[1], wu_ref, bu_ref)
            sg = jax.nn.sigmoid(ag)
            d = ds_ref[pl.ds(r0, R), :].astype(F32)
            das = (d * au * (sg * (1.0 + ag * (1.0 - sg))), d * ag * sg)
            out = []
            for half, (_, da_s, w_ref, *_) in enumerate(halves):
                da_s[pl.ds(r0, R), :] = das[half]
                out.append(tuple(sums[half][k] + _fold(das[half] * taps[half][k]) for k in range(3)) + (sums[half][3] + _fold(das[half]),))
                _, da1, da2 = _taps(da_s, r0, R, 0)
                dp_ref[half, pl.ds(r0, R), :] = (w_ref[2:3, :] * das[half] + w_ref[1:2, :] * da1 + w_ref[0:1, :] * da2).astype(BF16)
            return tuple(out)

        sums = _chunks(T, R, step, ((jnp.zeros((8, CONV_COLS), F32),) * 4,) * 2, reverse=True)
        for half, (_, da_s, w_ref, dw_ref, db_ref) in enumerate(halves):
            for k in range(3):
                dw_ref[k:k + 1, :] = jnp.sum(sums[half][k], axis=0, keepdims=True)
            db_ref[...] = jnp.sum(sums[half][3], axis=0, keepdims=True)

    col = lambda r: pl.BlockSpec((r, CONV_COLS), lambda j: (0, j))
    return pl.pallas_call(
        body, name="conv_glu_bwd", grid=(nb,),
        in_specs=[pair, col(T), gate(3), up(3), gate(1), up(1)],
        out_specs=[pair, col(3), col(3), col(1), col(1)],
        out_shape=[jax.ShapeDtypeStruct((2, T, D_FF), BF16), jax.ShapeDtypeStruct((3, D_FF), F32), jax.ShapeDtypeStruct((3, D_FF), F32),
                   jax.ShapeDtypeStruct((1, D_FF), F32), jax.ShapeDtypeStruct((1, D_FF), F32)],
        scratch_shapes=[pltpu.VMEM((T + HALO, CONV_COLS), F32)] * 4,
        compiler_params=_params(("parallel",)),
    )(p3, ds, conv_w, conv_w, conv_b, conv_b)


def _ffn_down_loss(s, w_down, u1, ln1_g, ln1_b, ln2_g, ln2_b, target):
    T = u1.shape[0]
    tm = min(512, T)

    def body(s_ref, w_ref, u1_ref, g1_ref, b1_ref, g2_ref, b2_ref, t_ref, du_ref, dub_ref, dg_ref, db_ref, loss_ref):
        @pl.when(pl.program_id(0) == 0)
        def _():
            dg_ref[...] = jnp.zeros_like(dg_ref)
            db_ref[...] = jnp.zeros_like(db_ref)
            loss_ref[...] = jnp.zeros_like(loss_ref)

        xh1, _ = _ln_stats(u1_ref[...])
        x1 = xh1 * g1_ref[...] + b1_ref[...]
        u2 = ALPHA * x1 + _dot(s_ref[...], w_ref[...])
        xh2, rstd2 = _ln_stats(u2)
        err = xh2 * g2_ref[...] + b2_ref[...] - t_ref[...]
        per_token = jnp.mean(err * err, axis=-1, keepdims=True)
        loss_ref[...] += 0.5 * jnp.sum(per_token, axis=0, keepdims=True)
        dy = err * (1.0 / D_MODEL)
        dg_ref[...] += jnp.sum(dy * xh2, axis=0, keepdims=True)
        db_ref[...] += jnp.sum(dy, axis=0, keepdims=True)
        du2 = _ln_bwd(dy, xh2, rstd2, g2_ref[...])
        du_ref[...] = du2
        dub_ref[...] = du2.astype(BF16)

    row = lambda n: pl.BlockSpec((tm, n), lambda i: (i, 0))
    const = lambda r, n: pl.BlockSpec((r, n), lambda i: (0, 0))
    vec = const(1, D_MODEL)
    return pl.pallas_call(
        body, name="ffn_down_loss", grid=(T // tm,),
        in_specs=[row(D_FF), const(D_FF, D_MODEL), row(D_MODEL), vec, vec, vec, vec, row(D_MODEL)],
        out_specs=[row(D_MODEL), row(D_MODEL), vec, vec, const(1, LANES)],
        out_shape=[jax.ShapeDtypeStruct((T, D_MODEL), F32), jax.ShapeDtypeStruct((T, D_MODEL), BF16), jax.ShapeDtypeStruct((1, D_MODEL), F32),
                   jax.ShapeDtypeStruct((1, D_MODEL), F32), jax.ShapeDtypeStruct((1, LANES), F32)],
        compiler_params=_params(("arbitrary",)),
    )(s, w_down, u1, ln1_g, ln1_b, ln2_g, ln2_b, target)


def _ffn_up_bwd_ln1(dp3, w_up, du2, u1, ln1_g):
    T = u1.shape[0]
    tm = min(512, T)

    def body(dp_ref, w_ref, du2_ref, u1_ref, g_ref, du_ref, dub_ref, dg_ref, db_ref):
        @pl.when(pl.program_id(0) == 0)
        def _():
            dg_ref[...] = jnp.zeros_like(dg_ref)
            db_ref[...] = jnp.zeros_like(db_ref)

        dx1 = _dot(dp_ref[0], w_ref[:D_FF, :]) + _dot(dp_ref[1], w_ref[D_FF:, :]) + ALPHA * du2_ref[...]
        xh, rstd = _ln_stats(u1_ref[...])
        dg_ref[...] += jnp.sum(dx1 * xh, axis=0, keepdims=True)
        db_ref[...] += jnp.sum(dx1, axis=0, keepdims=True)
        du1 = _ln_bwd(dx1, xh, rstd, g_ref[...])
        du_ref[...] = du1
        dub_ref[...] = du1.astype(BF16)

    row = lambda n: pl.BlockSpec((tm, n), lambda i: (i, 0))
    const = lambda r, n: pl.BlockSpec((r, n), lambda i: (0, 0))
    vec = const(1, D_MODEL)
    return pl.pallas_call(
        body, name="ffn_up_bwd_ln1", grid=(T // tm,),
        in_specs=[pl.BlockSpec((2, tm, D_FF), lambda i: (0, i, 0)), const(2 * D_FF, D_MODEL), row(D_MODEL), row(D_MODEL), vec],
        out_specs=[row(D_MODEL), row(D_MODEL), vec, vec],
        out_shape=[jax.ShapeDtypeStruct((T, D_MODEL), F32), jax.ShapeDtypeStruct((T, D_MODEL), BF16), jax.ShapeDtypeStruct((1, D_MODEL), F32),
                   jax.ShapeDtypeStruct((1, D_MODEL), F32)],
        compiler_params=_params(("arbitrary",)),
    )(dp3, w_up, du2, u1, ln1_g)


def _local_step(x, positions, w_in, b_gate, sinks, ln1_g, ln1_b, conv_b, ln2_g, ln2_b, target, later_weights,
                early_exchange=None, tail_exchange=None):
    T = x.shape[0]
    inv_freq = 1.0 / (ROPE_THETA ** (jnp.arange(0, HEAD_DIM, 2, dtype=F32) / HEAD_DIM))
    cos, sin = _rope_tables(positions.reshape(T, 1), jnp.tile(inv_freq, LANES // (HEAD_DIM // 2)).reshape(1, LANES))

    xb, qa, ka, va, qb, kb, vb, gl = _in_proj(x, w_in)
    ya = _swa_fwd(qa, ka, va, cos, sin, sinks)
    if isinstance(later_weights, tuple):
        exchange, finish = later_weights
        (yb, probs, betas), arrived = _sb_fwd(qb, kb, vb, exchange)
        later_weights = finish(arrived)
    else:
        (yb, probs, betas), _ = _sb_fwd(qb, kb, vb)
    wa, wb, wo, w_up, conv_w, w_down = later_weights
    h, u1, x1 = _mix_fwd(ya, yb, gl, x, wa, wb, wo, b_gate, ln1_g, ln1_b)

    ff_tn = D_FF // 2
    nff = D_FF // ff_tn
    tm = min(1024, T)
    p3 = _matmul(x1, w_up, kind="nt", name="ffn_up", grid=(T // tm, 2 * nff),
                 a_spec=pl.BlockSpec((tm, D_MODEL), lambda i, j: (i, 0)), b_spec=pl.BlockSpec((ff_tn, D_MODEL), lambda i, j: (j, 0)),
                 out_spec=pl.BlockSpec((None, tm, ff_tn), lambda i, j: (j // nff, i, j % nff)),
                 out_shape=jax.ShapeDtypeStruct((2, T, D_FF), ACT_DTYPE))
    s = _conv_glu_fwd(p3, conv_w, conv_b)
    du2, du2b, dln2_g, dln2_b, loss = _ffn_down_loss(s, w_down, u1, ln1_g, ln1_b, ln2_g, ln2_b, target)

    ds = _matmul(du2b, w_down, kind="nt", name="ffn_down_bwd", grid=(T // tm, nff),
                 a_spec=pl.BlockSpec((tm, D_MODEL), lambda i, j: (i, 0)), b_spec=pl.BlockSpec((ff_tn, D_MODEL), lambda i, j: (j, 0)),
                 out_spec=pl.BlockSpec((tm, ff_tn), lambda i, j: (i, j)), out_shape=jax.ShapeDtypeStruct((T, D_FF), ACT_DTYPE))
    dp3, dcw_g, dcw_u, dcb_g, dcb_u = _conv_glu_bwd(p3, ds, conv_w, conv_b)
    tk = 256
    dw_down = _matmul(s, du2b, kind="tn", name="dw_down", grid=(D_FF // tk,),
                      a_spec=pl.BlockSpec((T, tk), lambda i: (0, i)), b_spec=pl.BlockSpec((T, D_MODEL), lambda i: (0, 0)),
                      out_spec=pl.BlockSpec((tk, D_MODEL), lambda i: (i, 0)), out_shape=jax.ShapeDtypeStruct((D_FF, D_MODEL), BF16))
    dw_up = _matmul(dp3, x1, kind="tn", name="dw_up", grid=(2 * nff,),
                    a_spec=pl.BlockSpec((None, T, ff_tn), lambda j: (j // nff, 0, j % nff)), b_spec=pl.BlockSpec((T, D_MODEL), lambda j: (0, 0)),
                    out_spec=pl.BlockSpec((ff_tn, D_MODEL), lambda j: (j, 0)), out_shape=jax.ShapeDtypeStruct((2 * D_FF, D_MODEL), BF16))
    du1, du1b, dln1_g, dln1_b = _ffn_up_bwd_ln1(dp3, w_up, du2, u1, ln1_g)
    dya, dyb, dgl, dta, dtb, db_gate = _mix_bwd(du1, ya, yb, gl, wa, wb, wo, b_gate)

    def dw_tn(a, g, name):
        rows, cols = a.shape[1], g.shape[1]
        tn = min(512, cols)
        return _matmul(a, g, kind="tn", name=name, grid=(rows // 512, cols // tn),
                       a_spec=pl.BlockSpec((T, 512), lambda i, j: (0, i)), b_spec=pl.BlockSpec((T, tn), lambda i, j: (0, j)),
                       out_spec=pl.BlockSpec((512, tn), lambda i, j: (i, j)), out_shape=jax.ShapeDtypeStruct((rows, cols), BF16))

    dwa = dw_tn(ya, dta, "dw_branch_a")
    dwb = dw_tn(yb, dtb, "dw_branch_b")
    dwo = dw_tn(h, du1b, "dw_out")

    grads = dict(
        b_gate=db_gate, w_branch_a=dwa, w_branch_b=dwb, w_out=dwo, ln1_g=dln1_g, ln1_b=dln1_b,
        w_up=dw_up, conv_w=jnp.concatenate([dcw_g, dcw_u], axis=1), conv_b=(dcb_g, dcb_u), w_down=dw_down, ln2_g=dln2_g, ln2_b=dln2_b)
    (dqb, dkb, dvb), early_out = _sb_bwd(qb, kb, vb, probs, betas, dyb, early_exchange(grads) if early_exchange else None)
    dqa, dka, dva, grads["sinks"] = _swa_bwd(qa, ka, va, cos, sin, sinks, dya)
    dproj = (dqa, dka, dva, dqb, dkb, dvb, dgl)
    grads["w_in"] = _dw_in(xb, dproj)
    grad_x, tail_out = _grad_x(dproj, w_in, du1, tail_exchange(grads, loss) if tail_exchange else None)
    return loss, grad_x, grads, early_out, tail_out


def _dw_in(xb, dproj):
    T = xb.shape[0]
    tn = 2 * LANES
    groups, start, k = [], 0, 0
    while k < len(IN_WIDTHS):
        if IN_WIDTHS[k] >= tn:
            groups.append((start, IN_WIDTHS[k] // tn, [(k, 0, tn)]))
            k += 1
        else:
            members, off = [], 0
            while off < tn:
                members.append((k, off, IN_WIDTHS[k]))
                off += IN_WIDTHS[k]
                k += 1
            groups.append((start, 1, members))
        start += groups[-1][1]

    def body(x_ref, *refs):
        pieces, o_ref = refs[:-1], refs[-1]
        j = pl.program_id(0)
        for first, steps, members in groups:
            @pl.when((j >= first) & (j < first + steps))
            def _(members=members):
                for k, off, width in members:
                    o_ref[off:off + width, :] = _dot_tn(pieces[k][...], x_ref[...]).astype(o_ref.dtype)

    specs = [None] * len(IN_WIDTHS)
    for first, steps, members in groups:
        for k, _, width in members:
            specs[k] = pl.BlockSpec((T, width), lambda j, first=first, steps=steps: (0, jnp.clip(j - first, 0, steps - 1)))
    return pl.pallas_call(
        body, name="dw_in", grid=(IN_TOTAL // tn,),
        in_specs=[pl.BlockSpec((T, D_MODEL), lambda j: (0, 0))] + specs, out_specs=pl.BlockSpec((tn, D_MODEL), lambda j: (j, 0)),
        out_shape=jax.ShapeDtypeStruct((IN_TOTAL, D_MODEL), BF16), compiler_params=_params(("arbitrary",)),
    )(xb, *dproj)


def _grad_x(dproj, w_in, du1, exchange=None):
    T = du1.shape[0]
    tm = min(512, T)
    offs = np.cumsum((0,) + IN_WIDTHS)

    def body(*refs):
        pieces, (w_ref, du_ref, o_ref) = refs[:len(IN_WIDTHS)], refs[len(IN_WIDTHS):]
        acc = ALPHA * du_ref[...]
        for p_ref, a, b in zip(pieces, offs[:-1], offs[1:]):
            acc = acc + _dot(p_ref[...].astype(BF16), w_ref[a:b, :])
        o_ref[...] = acc

    row = lambda n: pl.BlockSpec((tm, n), lambda i: (i, 0))
    (grad_x,), arrived = _hosted_call(
        body, "grad_x", (T // tm,), exchange,
        in_specs=[row(n) for n in IN_WIDTHS] + [pl.BlockSpec((IN_TOTAL, D_MODEL), lambda i: (0, 0)), row(D_MODEL)],
        out_specs=[row(D_MODEL)], out_shape=[jax.ShapeDtypeStruct((T, D_MODEL), F32)], semantics=("parallel",),
        args=(*dproj, w_in, du1))
    return grad_x, arrived


ANY = pl.BlockSpec(memory_space=pl.ANY)


def _all_gather(slabs, name):
    n = len(slabs)

    def body(*refs):
        ins, outs = refs[:n], refs[n:2 * n]
        send_sems, recv_sems, local_sems = refs[2 * n:]
        x, y, c = lax.axis_index("x"), lax.axis_index("y"), lax.axis_index("c")
        me, sibling = (x, y, c), (x, y, 1 - c)
        chips = [(1 - x, y), (x, 1 - y), (1 - x, 1 - y)]

        def slot(pos):
            return 4 * pos[0] + 2 * pos[1] + pos[2]

        def copy(a, k, block, to, from_input=False):
            return pltpu.make_async_remote_copy(
                src_ref=ins[a] if from_input else outs[a].at[slot(block)], dst_ref=outs[a].at[slot(block)],
                send_sem=send_sems.at[a, k], recv_sem=recv_sems.at[a, k], device_id=to, device_id_type=MESH)

        mine = [pltpu.make_async_copy(ins[a], outs[a].at[slot(me)], local_sems.at[a]) for a in range(n)]
        for cp in mine:
            cp.start()
        first = []
        for a in range(n):
            first.append(copy(a, 0, me, sibling, from_input=True))
            first += [copy(a, 1 + j, me, (*chip, c), from_input=True) for j, chip in enumerate(chips)]
        for cp in first:
            cp.start()
        passed = []
        for j, chip in enumerate(chips):
            for a in range(n):
                copy(a, 1 + j, (*chip, c), me).wait_recv()
                fwd = copy(a, 4 + j, (*chip, c), sibling)
                fwd.start()
                passed.append(fwd)
        for a in range(n):
            copy(a, 0, sibling, me).wait_recv()
            for j, chip in enumerate(chips):
                copy(a, 4 + j, (*chip, 1 - c), me).wait_recv()
        for cp in first + passed:
            cp.wait_send()
        for cp in mine:
            cp.wait()

    return pl.pallas_call(
        body, name=name,
        in_specs=[ANY] * n, out_specs=[ANY] * n,
        out_shape=[jax.ShapeDtypeStruct((N_DEV,) + s.shape, s.dtype) for s in slabs],
        scratch_shapes=[pltpu.SemaphoreType.DMA((n, 7)), pltpu.SemaphoreType.DMA((n, 7)), pltpu.SemaphoreType.DMA((n,))],
    )(*slabs)


def _row_tile(rows):
    for cand in range(256, 7, -8):
        if rows % cand == 0:
            return cand
    return rows


def _window(w):
    wp = max(-(-((w * r) % LANES + w) // LANES) for r in range(N_DEV)) * LANES
    assert all((w * r) // LANES * LANES + wp <= N_DEV * w for r in range(N_DEV))
    return wp


def _join_cols(slabs, name):
    _, R, w = slabs.shape
    tr = _row_tile(R)
    wp = _window(w)

    def body(g_ref, o_ref, pad_ref):
        if w % LANES == 0:
            for r in range(N_DEV):
                o_ref[:, w * r:w * (r + 1)] = g_ref[r]
            return
        o_ref[...] = jnp.zeros_like(o_ref)
        pad_ref[...] = jnp.zeros_like(pad_ref)
        for r in range(N_DEV):
            q, s = divmod(w * r, LANES)
            pad_ref[:, :w] = g_ref[r]
            y = pad_ref[...]
            if s:
                y = pltpu.roll(y, s, axis=1)
            o_ref[:, LANES * q:LANES * q + wp] += y

    return pl.pallas_call(
        body, name=name, grid=(R // tr,),
        in_specs=[pl.BlockSpec((N_DEV, tr, w), lambda i: (0, i, 0))], out_specs=pl.BlockSpec((tr, N_DEV * w), lambda i: (i, 0)),
        out_shape=jax.ShapeDtypeStruct((R, N_DEV * w), slabs.dtype), scratch_shapes=[pltpu.VMEM((tr, wp), slabs.dtype)],
        compiler_params=_params(("parallel",)),
    )(slabs)


def _split_cols(pieces, name):
    R = pieces[0].shape[0]
    widths = [p.shape[1] for p in pieces]
    total = sum(widths)
    w = total // N_DEV
    tr = _row_tile(R)
    wp = _window(w)
    offs = np.cumsum([0] + widths)
    dtype = pieces[0].dtype

    def body(*refs):
        ins, (o_ref, full_ref) = refs[:len(pieces)], refs[len(pieces):]
        for p_ref, a, b in zip(ins, offs[:-1], offs[1:]):
            full_ref[:, a:b] = p_ref[...].astype(dtype)
        for r in range(N_DEV):
            q, s = divmod(w * r, LANES)
            y = full_ref[:, LANES * q:LANES * q + wp]
            if s:
                y = pltpu.roll(y, wp - s, axis=1)
            o_ref[r] = y[:, :w]

    return pl.pallas_call(
        body, name=name, grid=(R // tr,),
        in_specs=[pl.BlockSpec((tr, n), lambda i: (i, 0)) for n in widths], out_specs=pl.BlockSpec((N_DEV, tr, w), lambda i: (0, i, 0)),
        out_shape=jax.ShapeDtypeStruct((N_DEV, R, w), dtype), scratch_shapes=[pltpu.VMEM((tr, total), dtype)],
        compiler_params=_params(("parallel",)),
    )(*pieces)


def _adamw(g, w, m, v):
    m_new = ADAM_B1 * m + (1.0 - ADAM_B1) * g
    v_new = ADAM_B2 * v + (1.0 - ADAM_B2) * jnp.square(g)
    m_hat = m_new / (1.0 - ADAM_B1 ** ADAM_STEP)
    v_hat = v_new / (1.0 - ADAM_B2 ** ADAM_STEP)
    return -ADAM_LR * (m_hat / (jnp.sqrt(v_hat) + ADAM_EPS) + ADAM_WD * w), m_new, v_new


def _sum_parts(p_ref):
    g = p_ref[0].astype(F32)
    for d in range(1, p_ref.shape[0]):
        g = g + p_ref[d].astype(F32)
    return g


def _pair_swap(slabs, name):
    def body(in_ref, out_ref, send_sems, recv_sems):
        x, y, c = lax.axis_index("x"), lax.axis_index("y"), lax.axis_index("c")
        copies = [pltpu.make_async_remote_copy(src_ref=in_ref.at[2 * q + (1 - c)], dst_ref=out_ref.at[q], send_sem=send_sems.at[q],
                                               recv_sem=recv_sems.at[q], device_id=(x, y, 1 - c), device_id_type=MESH) for q in range(4)]
        for cp in copies:
            cp.start()
        for cp in copies:
            cp.wait()

    return pl.pallas_call(
        body, name=name, in_specs=[ANY], out_specs=ANY, out_shape=jax.ShapeDtypeStruct((4,) + slabs.shape[1:], slabs.dtype),
        scratch_shapes=[pltpu.SemaphoreType.DMA((4,)), pltpu.SemaphoreType.DMA((4,))],
    )(slabs)


def _pair_add(slabs, received, core, name):
    R, C = slabs.shape[1:]
    tr = _row_tile(R)

    def body(c_ref, a_ref, b_ref, o_ref):
        o_ref[...] = (a_ref[...].astype(F32) + b_ref[...].astype(F32)).astype(o_ref.dtype)

    same = pl.BlockSpec((None, tr, C), lambda q, i, c_ref: (q, i, 0))
    grid_spec = pltpu.PrefetchScalarGridSpec(
        num_scalar_prefetch=1, grid=(4, R // tr),
        in_specs=[pl.BlockSpec((None, tr, C), lambda q, i, c_ref: (2 * q + c_ref[0], i, 0)), same], out_specs=same)
    return pl.pallas_call(body, name=name, grid_spec=grid_spec, out_shape=jax.ShapeDtypeStruct((4, R, C), slabs.dtype),
                          compiler_params=_params(("parallel", "parallel")))(core, slabs, received)


def _reduce_adamw(parts, w, m, v, name):
    R, C = w.shape
    tr = _row_tile(R)

    def body(p_ref, w_ref, m_ref, v_ref, g_ref, d_ref, mo_ref, vo_ref):
        g = _sum_parts(p_ref)
        g_ref[...] = g
        d_ref[...], mo_ref[...], vo_ref[...] = _adamw(g, w_ref[...], m_ref[...], v_ref[...])

    row = pl.BlockSpec((tr, C), lambda i: (i, 0))
    return pl.pallas_call(
        body, name=name, grid=(R // tr,),
        in_specs=[pl.BlockSpec((parts.shape[0], tr, C), lambda i: (0, i, 0)), row, row, row],
        out_specs=[row] * 4, out_shape=[jax.ShapeDtypeStruct((R, C), F32)] * 4,
        compiler_params=_params(("parallel",)),
    )(parts, w, m, v)


def _reduce_adamw_small(parts, ws, ms, vs):
    sizes = [a.shape[1] for a in ws]
    k = len(sizes)
    offs = np.cumsum([0] + [-(-n // LANES) * LANES for n in sizes])

    def body(*refs):
        p_ref, w_refs, m_refs, v_refs = refs[0], refs[1:1 + k], refs[1 + k:1 + 2 * k], refs[1 + 2 * k:1 + 3 * k]
        outs, loss_ref = refs[1 + 3 * k:-1], refs[-1]
        g_all = _sum_parts(p_ref)
        for j, n in enumerate(sizes):
            g = g_all[:, offs[j]:offs[j] + LANES * (-(-n // LANES))][:, :n]
            outs[4 * j][...] = g
            outs[4 * j + 1][...], outs[4 * j + 2][...], outs[4 * j + 3][...] = _adamw(g, w_refs[j][...], m_refs[j][...], v_refs[j][...])
        loss_ref[...] = g_all[:, offs[k]:offs[k] + LANES]

    vm = pl.BlockSpec(memory_space=pltpu.VMEM)
    out_shape = [jax.ShapeDtypeStruct((1, n), F32) for n in sizes for _ in range(4)] + [jax.ShapeDtypeStruct((1, LANES), F32)]
    res = pl.pallas_call(
        body, name="reduce_adamw_replicated", in_specs=[vm] * (1 + 3 * k), out_specs=[vm] * len(out_shape), out_shape=out_shape,
        compiler_params=_params(),
    )(parts, *ws, *ms, *vs)
    return [res[4 * j:4 * j + 4] for j in range(k)], res[-1]


TRANSPOSED = ("w_in", "w_up")
COL_SHARDED = ("w_branch_a", "w_branch_b", "conv_w")
ROW_SHARDED = TRANSPOSED + ("w_out", "w_down")
SMALL = ("b_gate", "sinks", "ln1_g", "ln1_b", "conv_b", "ln2_g", "ln2_b")
ORDER = ("w_in", "b_gate", "sinks", "w_branch_a", "w_branch_b", "w_out", "ln1_g", "ln1_b", "w_up", "conv_w", "conv_b", "w_down", "ln2_g", "ln2_b")


def _pad_lanes(a):
    pad = (-a.shape[-1]) % LANES
    return a if pad == 0 else jnp.pad(a, ((0, 0), (0, pad)))


def kernel(x, positions, w_in, b_gate, sinks, w_branch_a, w_branch_b, w_out, ln1_g, ln1_b, w_up, conv_w, conv_b, w_down, ln2_g, ln2_b, loss_target, m_w_in, m_b_gate, m_sinks, m_w_branch_a, m_w_branch_b, m_w_out, m_ln1_g, m_ln1_b, m_w_up, m_conv_w, m_conv_b, m_w_down, m_ln2_g, m_ln2_b, v_w_in, v_b_gate, v_sinks, v_w_branch_a, v_w_branch_b, v_w_out, v_ln1_g, v_ln1_b, v_w_up, v_conv_w, v_conv_b, v_w_down, v_ln2_g, v_ln2_b):
    args = dict(locals())
    sharded = COL_SHARDED + ROW_SHARDED

    def shard(name, a):
        return a if name not in sharded else a[0].T if name in TRANSPOSED else a[0]

    w = {n: shard(n, args[n]) for n in ORDER}
    m = {n: shard(n, args["m_" + n]) for n in ORDER}
    v = {n: shard(n, args["v_" + n]) for n in ORDER}

    travel = {n: (w[n] if n == "conv_w" else w[n].astype(BF16)) for n in sharded}
    (g_in,) = _all_gather([travel["w_in"]], "all_gather_w_in")
    w_in_full = g_in.reshape(-1, g_in.shape[-1])
    later = ("w_branch_a", "w_branch_b", "w_out", "w_up", "conv_w", "w_down")

    def join(name, slabs):
        return _join_cols(slabs, "join_" + name) if name in COL_SHARDED else slabs.reshape(-1, slabs.shape[-1])

    def split(name, grad):
        if name in COL_SHARDED:
            return _split_cols(grad if isinstance(grad, tuple) else (grad,), "split_d" + name)
        return grad.reshape((N_DEV, -1, grad.shape[-1]))

    def early_exchange(grads):
        return _Exchange([split(n, grads[n]) for n in later], ["scatter"] * len(later))

    def tail_exchange(grads, loss):
        small_pack = jnp.concatenate(
            [_pad_lanes(p) for n in SMALL for p in (grads[n] if isinstance(grads[n], tuple) else (grads[n],))] + [loss], axis=1)
        slabs = split("w_in", grads["w_in"])
        core = lax.axis_index("c").astype(jnp.int32).reshape(1)
        chip_sums = _pair_add(slabs, _pair_swap(slabs, "pair_swap_dw_in"), core, "pair_add_dw_in")
        return _Exchange([chip_sums, small_pack], ["chips", "gather"])

    gather_later = _Exchange([travel[n] for n in later], ["gather"] * len(later))
    _, grad_x, _, early_out, (recv_w_in, small_parts) = _local_step(
        x[0], positions[0], w_in_full, w["b_gate"], w["sinks"][0], w["ln1_g"], w["ln1_b"], w["conv_b"], w["ln2_g"], w["ln2_b"], loss_target[0],
        (gather_later, lambda arrived: [join(n, a) for n, a in zip(later, arrived)]), early_exchange, tail_exchange)
    recv = dict(zip(later, early_out), w_in=recv_w_in)

    res = {n: _reduce_adamw(recv[n], w[n], m[n], v[n], "reduce_adamw_" + n) for n in sharded}
    small_res, loss_sum = _reduce_adamw_small(small_parts, [w[n] for n in SMALL], [m[n] for n in SMALL], [v[n] for n in SMALL])
    res.update(zip(SMALL, small_res))
    out = [loss_sum[0, 0], grad_x[None]]
    for k in range(4):
        out += [res[n][k].T[None] if n in TRANSPOSED else res[n][k][None] if n in sharded else res[n][k] for n in ORDER]
    return tuple(out)
```

```python
import functools

import jax
import jax.numpy as jnp
import numpy as np
from jax import lax
from jax.experimental import pallas as pl
from jax.experimental.pallas import tpu as pltpu

D_MODEL = 1024
HEAD_DIM = 64
SWA_Q_HEADS = 8
SWA_KV_HEADS = 2
SB_HEADS = 8
WINDOW = 128
ROPE_THETA = 10000.0
D_FF = 2816
LN_EPS = 1e-5
DEPTH = 1
ALPHA = (2.0 * DEPTH) ** 0.25
SWA_Q_WIDTH = SWA_Q_HEADS * HEAD_DIM
SWA_KV_WIDTH = SWA_KV_HEADS * HEAD_DIM
SB_WIDTH = SB_HEADS * HEAD_DIM
GATE_WIDTH = 2 * D_MODEL
IN_WIDTHS = (SWA_Q_WIDTH, SWA_KV_WIDTH, SWA_KV_WIDTH, SB_WIDTH, SB_WIDTH, SB_WIDTH, GATE_WIDTH)
IN_TOTAL = sum(IN_WIDTHS)
ATTN_SCALE = HEAD_DIM ** -0.5

ADAM_LR = 0.001
ADAM_B1 = 0.9
ADAM_B2 = 0.999
ADAM_EPS = 1e-08
ADAM_WD = 0.01
ADAM_STEP = 10

N_DEV = 8
LANES = 128
SB_BLOCK = 256
VMEM_LIMIT = 56 * 1024 * 1024

F32 = jnp.float32
BF16 = jnp.bfloat16
ACT_DTYPE = BF16
MESH = pl.DeviceIdType.MESH


def _params(sem=None):
    return pltpu.CompilerParams(dimension_semantics=sem, vmem_limit_bytes=VMEM_LIMIT)


def _dot(a, b):
    return jnp.dot(a, b, preferred_element_type=F32)


def _dot_nt(a, b):
    return lax.dot_general(a, b, (((1,), (1,)), ((), ())), preferred_element_type=F32)


def _dot_tn(a, b):
    return lax.dot_general(a, b, (((0,), (0,)), ((), ())), preferred_element_type=F32)


def _split_bf16(v):
    hi = v.astype(BF16)
    lo = (v - hi.astype(F32)).astype(BF16)
    return hi, lo


def _matmul(a, b, *, kind, out_shape, grid, a_spec, b_spec, out_spec, name, add=None, add_spec=None, add_scale=1.0):
    dot = {"nn": _dot, "nt": _dot_nt, "tn": _dot_tn}[kind]

    def body(*refs):
        if add is None:
            a_ref, b_ref, o_ref = refs
        else:
            a_ref, b_ref, add_ref, o_ref = refs
        r = dot(a_ref[...].astype(BF16), b_ref[...].astype(BF16))
        if add is not None:
            r = r + add_scale * add_ref[...]
        o_ref[...] = r.astype(o_ref.dtype)

    ins = [a, b] + ([] if add is None else [add])
    specs = [a_spec, b_spec] + ([] if add is None else [add_spec])
    return pl.pallas_call(
        body, name=name, grid=grid, in_specs=specs, out_specs=out_spec, out_shape=out_shape,
        compiler_params=_params(("parallel",) * len(grid)),
    )(*ins)


def _rope_tables(pos_col, inv_freq_lanes):
    T = pos_col.shape[0]
    tm = min(512, T)

    def body(pos_ref, f_ref, cos_ref, sin_ref):
        ang = pos_ref[...].astype(F32) * f_ref[...]
        cos_ref[...] = jnp.cos(ang)
        sin_ref[...] = jnp.sin(ang)

    return pl.pallas_call(
        body, name="rope_tables", grid=(T // tm,),
        in_specs=[pl.BlockSpec((tm, 1), lambda i: (i, 0)), pl.BlockSpec((1, LANES), lambda i: (0, 0))],
        out_specs=[pl.BlockSpec((tm, LANES), lambda i: (i, 0))] * 2,
        out_shape=[jax.ShapeDtypeStruct((T, LANES), F32)] * 2,
        compiler_params=_params(("parallel",)),
    )(pos_col, inv_freq_lanes)


def _lane_iota(shape):
    return lax.broadcasted_iota(jnp.int32, shape, len(shape) - 1)


def _rot_half(t):
    first = (_lane_iota(t.shape) % HEAD_DIM) < (HEAD_DIM // 2)
    return jnp.where(first, -pltpu.roll(t, LANES - HEAD_DIM // 2, axis=1), pltpu.roll(t, HEAD_DIM // 2, axis=1))


def _rope(t, cos, sin):
    return t * cos + _rot_half(t) * sin


def _rope_transpose(d, cos, sin):
    return d * cos - _rot_half(d * sin)


_IN_DTYPES = (F32, F32, BF16, BF16, BF16, BF16, F32)


def _in_proj(x, w_in_t):
    T = x.shape[0]
    tm = min(512, T)
    offs = np.cumsum((0,) + IN_WIDTHS)

    def body(x_ref, w_ref, xb_ref, *outs):
        xb = x_ref[...].astype(BF16)
        xb_ref[...] = xb
        for o_ref, a, b in zip(outs, offs[:-1], offs[1:]):
            o_ref[...] = _dot_nt(xb, w_ref[a:b, :]).astype(o_ref.dtype)

    row = lambda n: pl.BlockSpec((tm, n), lambda i: (i, 0))
    return pl.pallas_call(
        body, name="in_proj", grid=(T // tm,),
        in_specs=[row(D_MODEL), pl.BlockSpec((IN_TOTAL, D_MODEL), lambda i: (0, 0))],
        out_specs=[row(D_MODEL)] + [row(n) for n in IN_WIDTHS],
        out_shape=[jax.ShapeDtypeStruct((T, D_MODEL), BF16)] + [jax.ShapeDtypeStruct((T, n), dt) for n, dt in zip(IN_WIDTHS, _IN_DTYPES)],
        compiler_params=_params(("parallel",)),
    )(x, w_in_t)


def _swa_specs(T):
    blk = WINDOW
    cur = lambda n: pl.BlockSpec((blk, n), lambda i: (i, 0))
    prev = lambda n: pl.BlockSpec((blk, n), lambda i: (jnp.maximum(i - 1, 0), 0))
    return blk, cur, prev


SWA_GROUP = SWA_Q_HEADS // SWA_KV_HEADS


def _swa_stack(pairs):
    lane = _lane_iota(pairs[0].shape)
    zero = jnp.zeros((), pairs[0].dtype)
    rows = []
    for h in range(SWA_Q_HEADS):
        hh, g = h % 2, h // SWA_GROUP
        x = jnp.where((lane >= hh * HEAD_DIM) & (lane < (hh + 1) * HEAD_DIM), pairs[h // 2], zero)
        rows.append(x if hh == g else pltpu.roll(x, HEAD_DIM, axis=1))
    return jnp.concatenate(rows, axis=0)


def _swa_unstack(stacked, blk):
    low = _lane_iota((blk, LANES)) < HEAD_DIM
    pairs = []
    for pp in range(SWA_Q_HEADS // 2):
        halves = []
        for hh in range(2):
            h = 2 * pp + hh
            x = stacked[h * blk:(h + 1) * blk]
            halves.append(x if hh == h // SWA_GROUP else pltpu.roll(x, HEAD_DIM, axis=1))
        pairs.append(jnp.where(low, halves[0], halves[1]))
    return pairs


def _swa_probs(i, q_stack, kwin, sink_ref, blk):
    r = lax.broadcasted_iota(jnp.int32, (blk, 2 * blk), 0)
    c = lax.broadcasted_iota(jnp.int32, (blk, 2 * blk), 1)
    rel = blk + r - c
    valid = (rel >= 0) & (rel < WINDOW) & ((c >= blk) | (i > 0))
    bias = jnp.concatenate([jnp.where(valid, 0.0, -1e30)] * SWA_Q_HEADS, axis=0)
    head = lax.broadcasted_iota(jnp.int32, (SWA_Q_HEADS * blk, 1), 0) // blk
    sink = jnp.zeros((SWA_Q_HEADS * blk, 1), F32)
    for h in range(SWA_Q_HEADS):
        sink = jnp.where(head == h, sink_ref[h], sink)
    s = _dot_nt(q_stack, kwin) * ATTN_SCALE + bias
    m = jnp.maximum(jnp.max(s, axis=1, keepdims=True), sink)
    p = jnp.exp(s - m)
    es = jnp.exp(sink - m)
    den = jnp.sum(p, axis=1, keepdims=True) + es
    return p / den, es / den


def _swa_inputs(q_ref, kp_ref, kc_ref, vp_ref, vc_ref, cp_ref, cc_ref, sp_ref, sc_ref):
    cc, sc = cc_ref[...], sc_ref[...]
    kwin = jnp.concatenate([_rope(kp_ref[...], cp_ref[...], sp_ref[...]), _rope(kc_ref[...], cc, sc)], axis=0).astype(BF16)
    vwin = jnp.concatenate([vp_ref[...], vc_ref[...]], axis=0)
    q_stack = _swa_stack([_rope(q_ref[:, pp * LANES:(pp + 1) * LANES], cc, sc) for pp in range(SWA_Q_HEADS // 2)]).astype(BF16)
    return q_stack, kwin, vwin


def _swa_fwd(qa, ka, va, cos, sin, sinks):
    T = qa.shape[0]
    blk, cur, prev = _swa_specs(T)

    def body(sink_ref, q_ref, kp_ref, kc_ref, vp_ref, vc_ref, cp_ref, cc_ref, sp_ref, sc_ref, o_ref):
        q_stack, kwin, vwin = _swa_inputs(q_ref, kp_ref, kc_ref, vp_ref, vc_ref, cp_ref, cc_ref, sp_ref, sc_ref)
        probs, _ = _swa_probs(pl.program_id(0), q_stack, kwin, sink_ref, blk)
        for pp, tile in enumerate(_swa_unstack(_dot(probs.astype(BF16), vwin), blk)):
            o_ref[:, pp * LANES:(pp + 1) * LANES] = tile.astype(o_ref.dtype)

    return pl.pallas_call(
        body, name="swa_fwd", grid=(T // blk,),
        in_specs=[pl.BlockSpec(memory_space=pltpu.SMEM), cur(SWA_Q_WIDTH), prev(LANES), cur(LANES), prev(LANES), cur(LANES),
                  prev(LANES), cur(LANES), prev(LANES), cur(LANES)],
        out_specs=cur(SWA_Q_WIDTH),
        out_shape=jax.ShapeDtypeStruct((T, SWA_Q_WIDTH), BF16),
        compiler_params=_params(("parallel",)),
    )(sinks, qa, ka, ka, va, va, cos, cos, sin, sin)


def _swa_bwd(qa, ka, va, cos, sin, sinks, dya):
    T = qa.shape[0]
    blk, cur, prev = _swa_specs(T)
    full = lambda n: pl.BlockSpec((T, n), lambda i: (0, 0))

    def body(sink_ref, q_ref, kp_ref, kc_ref, vp_ref, vc_ref, cp_ref, cc_ref, sp_ref, sc_ref, do_ref,
             dq_ref, dk_out, dv_out, dsink_ref, dk_ref, dv_ref):
        i = pl.program_id(0)

        @pl.when(i == 0)
        def _():
            dk_ref[...] = jnp.zeros_like(dk_ref)
            dv_ref[...] = jnp.zeros_like(dv_ref)
            dsink_ref[...] = jnp.zeros_like(dsink_ref)

        cp, cc, sp, sc = cp_ref[...], cc_ref[...], sp_ref[...], sc_ref[...]
        q_stack, kwin, vwin = _swa_inputs(q_ref, kp_ref, kc_ref, vp_ref, vc_ref, cp_ref, cc_ref, sp_ref, sc_ref)
        probs, psink = _swa_probs(i, q_stack, kwin, sink_ref, blk)
        do_stack = _swa_stack([do_ref[:, pp * LANES:(pp + 1) * LANES] for pp in range(SWA_Q_HEADS // 2)])
        dp = _dot_nt(do_stack, vwin)
        dsum = jnp.sum(probs * dp, axis=1, keepdims=True)
        ds = (probs * (dp - dsum) * ATTN_SCALE).astype(BF16)
        for pp, tile in enumerate(_swa_unstack(_dot(ds, kwin), blk)):
            dq_ref[:, pp * LANES:(pp + 1) * LANES] = _rope_transpose(tile, cc, sc).astype(dq_ref.dtype)
        dkw = _dot_tn(ds, q_stack)
        dvw = _dot_tn(probs.astype(BF16), do_stack)
        lane1 = _lane_iota((1, LANES))
        sink_share = psink * dsum
        dsink = jnp.zeros((1, LANES), F32)
        for h in range(SWA_Q_HEADS):
            dsink = dsink + jnp.where(lane1 == h, -jnp.sum(sink_share[h * blk:(h + 1) * blk]), 0.0)
        dsink_ref[...] += dsink
        ip = jnp.maximum(i - 1, 0)
        rows_p = pl.ds(pl.multiple_of(ip * blk, blk), blk)
        rows_c = pl.ds(pl.multiple_of(i * blk, blk), blk)
        dk_ref[rows_p, :] += _rope_transpose(dkw[:blk], cp, sp)
        dv_ref[rows_p, :] += dvw[:blk]
        dk_ref[rows_c, :] += _rope_transpose(dkw[blk:], cc, sc)
        dv_ref[rows_c, :] += dvw[blk:]

        @pl.when(i == T // blk - 1)
        def _():
            dk_out[...] = dk_ref[...].astype(BF16)
            dv_out[...] = dv_ref[...].astype(BF16)

    return pl.pallas_call(
        body, name="swa_bwd", grid=(T // blk,),
        in_specs=[pl.BlockSpec(memory_space=pltpu.SMEM), cur(SWA_Q_WIDTH), prev(LANES), cur(LANES), prev(LANES), cur(LANES),
                  prev(LANES), cur(LANES), prev(LANES), cur(LANES), cur(SWA_Q_WIDTH)],
        out_specs=[cur(SWA_Q_WIDTH), full(LANES), full(LANES), pl.BlockSpec((1, LANES), lambda i: (0, 0))],
        out_shape=[jax.ShapeDtypeStruct((T, SWA_Q_WIDTH), BF16), jax.ShapeDtypeStruct((T, LANES), BF16),
                   jax.ShapeDtypeStruct((T, LANES), BF16), jax.ShapeDtypeStruct((1, LANES), F32)],
        scratch_shapes=[pltpu.VMEM((T, LANES), F32)] * 2,
        compiler_params=_params(("arbitrary",)),
    )(sinks, qa, ka, ka, va, va, cos, cos, sin, sin, dya)


class _Exchange:
    FLIPS = [(fx, fy, fc) for fx in (0, 1) for fy in (0, 1) for fc in (0, 1) if (fx, fy, fc) != (0, 0, 0)]

    def __init__(self, arrays, kinds):
        self.arrays, self.kinds, self.n = list(arrays), list(kinds), len(arrays)

    def out_shape(self):
        return [jax.ShapeDtypeStruct((N_DEV,) + a.shape if k == "gather" else a.shape, a.dtype) for a, k in zip(self.arrays, self.kinds)]

    def scratch(self):
        return [pltpu.SemaphoreType.DMA((self.n, 7)), pltpu.SemaphoreType.DMA((self.n, 7)), pltpu.SemaphoreType.DMA((self.n,))]

    def bind(self, ins, outs, send_sems, recv_sems, local_sems):
        x, y, c = lax.axis_index("x"), lax.axis_index("y"), lax.axis_index("c")
        local, remote = [], []
        for a, kind in enumerate(self.kinds):
            me = 2 * x + y if kind == "chips" else 4 * x + 2 * y + c
            mine = ins[a] if kind == "gather" else ins[a].at[me]
            local.append(pltpu.make_async_copy(mine, outs[a].at[me], local_sems.at[a]))
            for k, (fx, fy, fc) in enumerate(self.FLIPS):
                if kind == "chips" and (fc or not (fx or fy)):
                    continue
                peer = (x ^ fx, y ^ fy, c ^ fc)
                peer_slot = 2 * peer[0] + peer[1] if kind == "chips" else 4 * peer[0] + 2 * peer[1] + peer[2]
                src = ins[a] if kind == "gather" else ins[a].at[peer_slot]
                sems = dict(send_sem=send_sems.at[a, k], recv_sem=recv_sems.at[a, k], device_id=peer, device_id_type=MESH)
                remote.append((pltpu.make_async_remote_copy(src_ref=src, dst_ref=outs[a].at[me], **sems),
                               pltpu.make_async_remote_copy(src_ref=src, dst_ref=outs[a].at[peer_slot], **sems)))

        def start():
            for cp in local:
                cp.start()
            for send, _ in remote:
                send.start()

        def wait():
            for send, arrival in remote:
                arrival.wait_recv()
                send.wait_send()
            for cp in local:
                cp.wait()

        return start, wait


def _hosted_call(body, name, grid, exchange, *, in_specs, out_specs, out_shape, semantics, args, scratch=(), prefetch=()):
    n = 0 if exchange is None else exchange.n
    n_pre, n_in, n_out, n_scratch = len(prefetch), len(in_specs), len(out_specs), len(scratch)

    def hosted(*refs):
        pre, rest = refs[:n_pre], refs[n_pre:]
        ins, rest = rest[:n_in], rest[n_in:]
        ex_ins, rest = rest[:n], rest[n:]
        outs, rest = rest[:n_out], rest[n_out:]
        ex_outs, rest = rest[:n], rest[n:]
        own, sems = rest[:n_scratch], rest[n_scratch:]
        if exchange is None:
            return body(*pre, *ins, *outs, *own)
        start, wait = exchange.bind(ex_ins, ex_outs, *sems)
        ids = [pl.program_id(d) for d in range(len(grid))]
        first = functools.reduce(jnp.logical_and, [i == 0 for i in ids])
        last = functools.reduce(jnp.logical_and, [i == g - 1 for i, g in zip(ids, grid)])
        pl.when(first)(start)
        body(*pre, *ins, *outs, *own)
        pl.when(last)(wait)

    grid_spec = pltpu.PrefetchScalarGridSpec(
        num_scalar_prefetch=n_pre, grid=grid, in_specs=list(in_specs) + [ANY] * n, out_specs=list(out_specs) + [ANY] * n,
        scratch_shapes=list(scratch) + ([] if exchange is None else exchange.scratch()))
    res = pl.pallas_call(
        hosted, name=name, grid_spec=grid_spec, out_shape=list(out_shape) + ([] if exchange is None else exchange.out_shape()),
        compiler_params=_params(semantics if exchange is None else ("arbitrary",) * len(grid)),
    )(*prefetch, *args, *([] if exchange is None else exchange.arrays))
    return res[:n_out], res[n_out:]


SOFTPLUS_LINEAR_FROM = 30.0


def _sb_scores(qm, k, valid):
    z = _dot_nt(qm, k)
    sp = jnp.where(z > SOFTPLUS_LINEAR_FROM, z, jnp.log(1.0 + jnp.exp(z)))
    log_beta = z - sp
    if valid is not None:
        sp = jnp.where(valid, sp, 0.0)
    return sp, log_beta


def _tri2(B, cmp):
    r = lax.broadcasted_iota(jnp.int32, (2 * B, B), 0) % B
    c = lax.broadcasted_iota(jnp.int32, (2 * B, B), 1)
    return cmp(r, c).astype(BF16)


def _tri_sum(v, tri2):
    hi, lo = _split_bf16(v)
    return _dot(jnp.concatenate([hi, lo], axis=1), tri2)


def _head_masks(x):
    low = _lane_iota(x.shape) < HEAD_DIM
    zero = jnp.zeros((), x.dtype)
    return jnp.where(low, x, zero), jnp.where(low, zero, x)


def _strictly_below(B):
    r = lax.broadcasted_iota(jnp.int32, (B, B), 0)
    c = lax.broadcasted_iota(jnp.int32, (B, B), 1)
    return c < r


def _sb_grid(T, descending):
    B = min(SB_BLOCK, T)
    n = T // B
    pairs = [(i, j) for i in range(n) for j in (range(i, -1, -1) if descending else range(i + 1))]
    return B, jnp.asarray([p[0] for p in pairs], jnp.int32), jnp.asarray([p[1] for p in pairs], jnp.int32)


N_PAIRS = SB_HEADS // 2
PAIR_COLS = [slice(p * LANES, (p + 1) * LANES) for p in range(N_PAIRS)]


def _sb_fwd(qb, kb, vb, exchange=None):
    T = qb.shape[0]
    B, i_tab, j_tab = _sb_grid(T, descending=True)
    n = T // B

    def body(i_ref, j_ref, q_ref, k_ref, v_ref, o_ref, a_ref, b_ref, acc_ref, c_ref, tri_ref):
        s = pl.program_id(0)
        i, j = i_ref[s], j_ref[s]

        @pl.when(s == 0)
        def _():
            tri_ref[...] = _tri2(B, lambda r, c: r > c)

        @pl.when(j == i)
        def _():
            acc_ref[...] = jnp.zeros_like(acc_ref)
            c_ref[...] = jnp.zeros_like(c_ref)

        def block(valid):
            for p, cols in enumerate(PAIR_COLS):
                qms = _head_masks(q_ref[:, cols] * ATTN_SCALE)
                k = k_ref[:, cols]
                probs = []
                for hh in range(2):
                    h = 2 * p + hh
                    sp, lb = _sb_scores(qms[hh], k, valid)
                    c = c_ref[h]
                    a = jnp.exp(lb - (c + _tri_sum(sp, tri_ref[...])))
                    beta = jnp.exp(lb)
                    if valid is not None:
                        a = jnp.where(valid, a, 0.0)
                        beta = jnp.where(valid, beta, 0.0)
                    probs.append(a.astype(BF16))
                    a_ref[h] = probs[-1]
                    b_ref[h] = beta.astype(BF16)
                    c_ref[h] = c + jnp.sum(sp, axis=1, keepdims=True)
                acc_ref[:, cols] += _dot(jnp.concatenate(probs, axis=1), jnp.concatenate(_head_masks(v_ref[:, cols]), axis=0))

        pl.when(j == i)(lambda: block(_strictly_below(B)))
        pl.when(j != i)(lambda: block(None))

        @pl.when(j == 0)
        def _():
            o_ref[...] = acc_ref[...].astype(o_ref.dtype)

    q_spec = pl.BlockSpec((B, SB_WIDTH), lambda s, i_ref, j_ref: (i_ref[s], 0))
    k_spec = pl.BlockSpec((B, SB_WIDTH), lambda s, i_ref, j_ref: (j_ref[s], 0))
    tile = pl.BlockSpec((None, None, SB_HEADS, B, B), lambda s, i_ref, j_ref: (i_ref[s], j_ref[s], 0, 0, 0))
    saved = jax.ShapeDtypeStruct((n, n, SB_HEADS, B, B), BF16)
    return _hosted_call(
        body, "sb_fwd", (int(i_tab.shape[0]),), exchange, prefetch=(i_tab, j_tab),
        in_specs=[q_spec, k_spec, k_spec], out_specs=[q_spec, tile, tile],
        out_shape=[jax.ShapeDtypeStruct((T, SB_WIDTH), BF16), saved, saved],
        scratch=[pltpu.VMEM((B, SB_WIDTH), F32), pltpu.VMEM((SB_HEADS, B, 1), F32), pltpu.VMEM((2 * B, B), BF16)],
        semantics=("arbitrary",), args=(qb, kb, vb))


def _sb_bwd(qb, kb, vb, probs, betas, dyb, exchange=None):
    T = qb.shape[0]
    B, i_tab, j_tab = _sb_grid(T, descending=False)
    n_steps = int(i_tab.shape[0])

    def block_diag_t(x):
        xt = x.T
        top = lax.broadcasted_iota(jnp.int32, xt.shape, 0) < HEAD_DIM
        zero = jnp.zeros((), x.dtype)
        return jnp.concatenate([jnp.where(top, xt, zero), jnp.where(top, zero, xt)], axis=1)

    def body(i_ref, j_ref, q_ref, k_ref, v_ref, a_ref, b_ref, do_ref, dq_ref, dk_out, dv_out,
             dq_acc, cg_ref, dkt_ref, dvt_ref, tri_ref, qt_ref, dot_ref):
        s = pl.program_id(0)
        i, j = i_ref[s], j_ref[s]

        @pl.when(s == 0)
        def _():
            dkt_ref[...] = jnp.zeros_like(dkt_ref)
            dvt_ref[...] = jnp.zeros_like(dvt_ref)
            tri_ref[...] = _tri2(B, lambda r, c: r < c)[:B]

        @pl.when(j == 0)
        def _():
            dq_acc[...] = jnp.zeros_like(dq_acc)
            cg_ref[...] = jnp.zeros_like(cg_ref)
            for p, cols in enumerate(PAIR_COLS):
                qt_ref[p] = block_diag_t(q_ref[:, cols] * ATTN_SCALE)
                dot_ref[p] = block_diag_t(do_ref[:, cols])

        for p, cols in enumerate(PAIR_COLS):
            doms = _head_masks(do_ref[:, cols])
            k, v = k_ref[:, cols], v_ref[:, cols]
            dzs = []
            for hh in range(2):
                h = 2 * p + hh
                g = a_ref[h].astype(F32) * _dot_nt(doms[hh], v)
                cg = cg_ref[h]
                gsum = g + (cg + _dot(g.astype(BF16), tri_ref[...]))
                dzs.append((g - b_ref[h].astype(F32) * gsum).astype(BF16))
                cg_ref[h] = cg + jnp.sum(g, axis=1, keepdims=True)
            dq_acc[:, cols] += _dot(jnp.concatenate(dzs, axis=1), jnp.concatenate(_head_masks(k), axis=0))
            dkt_ref[j, cols, :] += _dot(qt_ref[p], jnp.concatenate(dzs, axis=0))
            dvt_ref[j, cols, :] += _dot(dot_ref[p], jnp.concatenate([a_ref[2 * p], a_ref[2 * p + 1]], axis=0))

        @pl.when(j == i)
        def _():
            dq_ref[...] = (dq_acc[...] * ATTN_SCALE).astype(dq_ref.dtype)

        @pl.when(s == n_steps - 1)
        def _():
            for jb in range(T // B):
                dk_out[jb * B:(jb + 1) * B, :] = dkt_ref[jb].T.astype(BF16)
                dv_out[jb * B:(jb + 1) * B, :] = dvt_ref[jb].T.astype(BF16)

    q_spec = pl.BlockSpec((B, SB_WIDTH), lambda s, i_ref, j_ref: (i_ref[s], 0))
    k_spec = pl.BlockSpec((B, SB_WIDTH), lambda s, i_ref, j_ref: (j_ref[s], 0))
    tile = pl.BlockSpec((None, None, SB_HEADS, B, B), lambda s, i_ref, j_ref: (i_ref[s], j_ref[s], 0, 0, 0))
    full = pl.BlockSpec((T, SB_WIDTH), lambda s, i_ref, j_ref: (0, 0))
    return _hosted_call(
        body, "sb_bwd", (n_steps,), exchange, prefetch=(i_tab, j_tab),
        in_specs=[q_spec, k_spec, k_spec, tile, tile, q_spec], out_specs=[q_spec, full, full],
        out_shape=[jax.ShapeDtypeStruct((T, SB_WIDTH), BF16)] * 3,
        scratch=[pltpu.VMEM((B, SB_WIDTH), F32), pltpu.VMEM((SB_HEADS, B, 1), F32), pltpu.VMEM((T // B, SB_WIDTH, B), F32),
                 pltpu.VMEM((T // B, SB_WIDTH, B), F32), pltpu.VMEM((B, B), BF16), pltpu.VMEM((N_PAIRS, LANES, 2 * B), BF16),
                 pltpu.VMEM((N_PAIRS, LANES, 2 * B), BF16)],
        semantics=("arbitrary",), args=(qb, kb, vb, probs, betas, dyb))


def _ln_stats(u):
    mu = jnp.mean(u, axis=-1, keepdims=True)
    xc = u - mu
    var = jnp.mean(xc * xc, axis=-1, keepdims=True)
    rstd = lax.rsqrt(var + LN_EPS)
    return xc * rstd, rstd


def _ln_bwd(dy, xhat, rstd, g):
    dxh = dy * g
    return rstd * (dxh - jnp.mean(dxh, axis=-1, keepdims=True) - xhat * jnp.mean(dxh * xhat, axis=-1, keepdims=True))


def _gates(gl_ref, bg_ref):
    ga = jax.nn.sigmoid(gl_ref[:, :D_MODEL] + bg_ref[:, :D_MODEL])
    gb = jax.nn.sigmoid(gl_ref[:, D_MODEL:] + bg_ref[:, D_MODEL:])
    return ga, gb


def _mix_fwd(ya, yb, gl, x, wa, wb, wo, b_gate, ln1_g, ln1_b):
    T = x.shape[0]
    tm = min(512, T)

    def body(ya_ref, yb_ref, gl_ref, x_ref, wa_ref, wb_ref, wo_ref, bg_ref, g_ref, b_ref, h_ref, u_ref, x1_ref):
        ga, gb = _gates(gl_ref, bg_ref)
        h = (ga * _dot(ya_ref[...], wa_ref[...]) + gb * _dot(yb_ref[...], wb_ref[...])).astype(BF16)
        h_ref[...] = h
        u = ALPHA * x_ref[...] + _dot(h, wo_ref[...])
        u_ref[...] = u
        xhat, _ = _ln_stats(u)
        x1_ref[...] = (xhat * g_ref[...] + b_ref[...]).astype(BF16)

    row = lambda n: pl.BlockSpec((tm, n), lambda i: (i, 0))
    const = lambda r, n: pl.BlockSpec((r, n), lambda i: (0, 0))
    return pl.pallas_call(
        body, name="mix_fwd", grid=(T // tm,),
        in_specs=[row(SWA_Q_WIDTH), row(SB_WIDTH), row(GATE_WIDTH), row(D_MODEL), const(SWA_Q_WIDTH, D_MODEL), const(SB_WIDTH, D_MODEL),
                  const(D_MODEL, D_MODEL), const(1, GATE_WIDTH), const(1, D_MODEL), const(1, D_MODEL)],
        out_specs=[row(D_MODEL)] * 3,
        out_shape=[jax.ShapeDtypeStruct((T, D_MODEL), BF16), jax.ShapeDtypeStruct((T, D_MODEL), F32), jax.ShapeDtypeStruct((T, D_MODEL), BF16)],
        compiler_params=_params(("parallel",)),
    )(ya, yb, gl, x, wa, wb, wo, b_gate, ln1_g, ln1_b)


def _mix_bwd(du1, ya, yb, gl, wa, wb, wo, b_gate):
    T = du1.shape[0]
    tm = min(512, T)

    def body(du_ref, ya_ref, yb_ref, gl_ref, wa_ref, wb_ref, wo_ref, bg_ref, dya_ref, dyb_ref, dgl_ref, dta_ref, dtb_ref, dbg_ref):
        @pl.when(pl.program_id(0) == 0)
        def _():
            dbg_ref[...] = jnp.zeros_like(dbg_ref)

        dh = _dot_nt(du_ref[...].astype(BF16), wo_ref[...])
        ga, gb = _gates(gl_ref, bg_ref)
        for gate, y_ref, w_ref, dy_ref, dt_ref, lo in ((ga, ya_ref, wa_ref, dya_ref, dta_ref, 0), (gb, yb_ref, wb_ref, dyb_ref, dtb_ref, D_MODEL)):
            t = _dot(y_ref[...], w_ref[...])
            dlogit = dh * t * gate * (1.0 - gate)
            dgl_ref[:, lo:lo + D_MODEL] = dlogit.astype(BF16)
            dbg_ref[:, lo:lo + D_MODEL] += jnp.sum(dlogit, axis=0, keepdims=True)
            dt = (dh * gate).astype(BF16)
            dt_ref[...] = dt
            dy_ref[...] = _dot_nt(dt, w_ref[...]).astype(BF16)

    row = lambda n: pl.BlockSpec((tm, n), lambda i: (i, 0))
    const = lambda r, n: pl.BlockSpec((r, n), lambda i: (0, 0))
    sds = lambda n, dt: jax.ShapeDtypeStruct((T, n), dt)
    return pl.pallas_call(
        body, name="mix_bwd", grid=(T // tm,),
        in_specs=[row(D_MODEL), row(SWA_Q_WIDTH), row(SB_WIDTH), row(GATE_WIDTH), const(SWA_Q_WIDTH, D_MODEL), const(SB_WIDTH, D_MODEL),
                  const(D_MODEL, D_MODEL), const(1, GATE_WIDTH)],
        out_specs=[row(SWA_Q_WIDTH), row(SB_WIDTH), row(GATE_WIDTH), row(D_MODEL), row(D_MODEL), const(1, GATE_WIDTH)],
        out_shape=[sds(SWA_Q_WIDTH, BF16), sds(SB_WIDTH, BF16), sds(GATE_WIDTH, BF16), sds(D_MODEL, BF16), sds(D_MODEL, BF16),
                   jax.ShapeDtypeStruct((1, GATE_WIDTH), F32)],
        compiler_params=_params(("arbitrary",)),
    )(du1, ya, yb, gl, wa, wb, wo, b_gate)


CONV_COLS = LANES


CONV_CHUNK = 64
CONV_CHUNK_FWD = 256
HALO = 8


def _taps(ref, r0, rows, lead):
    return [ref[pl.ds(r0 + lead + k, rows), :] for k in ((-2, -1, 0) if lead else (0, 1, 2))]


def _chunks(T, rows, step, init=None, reverse=False):
    def body(c, carry):
        c = T // rows - 1 - c if reverse else c
        out = step(pl.multiple_of(c * rows, rows), *(() if init is None else (carry,)))
        return carry if init is None else out
    return lax.fori_loop(0, T // rows, body, 0 if init is None else init)


def _conv_chunk(taps, w_ref, b_ref):
    return w_ref[0:1, :] * taps[0] + w_ref[1:2, :] * taps[1] + w_ref[2:3, :] * taps[2] + b_ref[...]


def _fold(x):
    return jnp.sum(x.reshape(x.shape[0] // 8, 8, x.shape[1]), axis=0)


def _conv_specs(T):
    nb = D_FF // CONV_COLS
    pair = pl.BlockSpec((2, T, CONV_COLS), lambda j: (0, 0, j))
    gate = lambda r: pl.BlockSpec((r, CONV_COLS), lambda j: (0, j))
    up = lambda r: pl.BlockSpec((r, CONV_COLS), lambda j: (0, j + nb))
    return nb, pair, gate, up


def _conv_glu_fwd(p3, conv_w, conv_b):
    T = p3.shape[1]
    nb, pair, gate, up = _conv_specs(T)

    R = min(CONV_CHUNK_FWD, T)

    def body(p_ref, wg_ref, wu_ref, bg_ref, bu_ref, s_ref, pg_s, pu_s):
        for half, scr in enumerate((pg_s, pu_s)):
            scr[0:HALO, :] = jnp.zeros((HALO, CONV_COLS), F32)
            scr[HALO:HALO + T, :] = p_ref[half].astype(F32)
        def step(r0):
            ag = _conv_chunk(_taps(pg_s, r0, R, HALO), wg_ref, bg_ref)
            au = _conv_chunk(_taps(pu_s, r0, R, HALO), wu_ref, bu_ref)
            s_ref[pl.ds(r0, R), :] = (ag * jax.nn.sigmoid(ag) * au).astype(BF16)

        _chunks(T, R, step)

    return pl.pallas_call(
        body, name="conv_glu_fwd", grid=(nb,),
        in_specs=[pair, gate(3), up(3), gate(1), up(1)],
        out_specs=pl.BlockSpec((T, CONV_COLS), lambda j: (0, j)),
        out_shape=jax.ShapeDtypeStruct((T, D_FF), BF16),
        scratch_shapes=[pltpu.VMEM((T + HALO, CONV_COLS), F32)] * 2,
        compiler_params=_params(("parallel",)),
    )(p3, conv_w, conv_w, conv_b, conv_b)


def _conv_glu_bwd(p3, ds, conv_w, conv_b):
    T = p3.shape[1]
    nb, pair, gate, up = _conv_specs(T)

    R = min(CONV_CHUNK, T)

    def body(p_ref, ds_ref, wg_ref, wu_ref, bg_ref, bu_ref, dp_ref, dwg_ref, dwu_ref, dbg_ref, dbu_ref, pg_s, pu_s, dag_s, dau_s):
        for half, scr in enumerate((pg_s, pu_s)):
            scr[0:HALO, :] = jnp.zeros((HALO, CONV_COLS), F32)
            scr[HALO:HALO + T, :] = p_ref[half].astype(F32)
        for scr in (dag_s, dau_s):
            scr[T:T + HALO, :] = jnp.zeros((HALO, CONV_COLS), F32)
        halves = ((pg_s, dag_s, wg_ref, dwg_ref, dbg_ref), (pu_s, dau_s, wu_ref, dwu_ref, dbu_ref))

        def step(r0, sums):
            taps = [_taps(p_s, r0, R, HALO) for p_s, *_ in halves]
            ag = _conv_chunk(taps[0], wg_ref, bg_ref)
            au = _conv_chunk(taps[1], wu_ref, bu_ref)
            sg = jax.nn.sigmoid(ag)
            d = ds_ref[pl.ds(r0, R), :].astype(F32)
            das = (d * au * (sg * (1.0 + ag * (1.0 - sg))), d * ag * sg)
            out = []
            for half, (_, da_s, w_ref, *_) in enumerate(halves):
                da_s[pl.ds(r0, R), :] = das[half]
                out.append(tuple(sums[half][k] + _fold(das[half] * taps[half][k]) for k in range(3)) + (sums[half][3] + _fold(das[half]),))
                _, da1, da2 = _taps(da_s, r0, R, 0)
                dp_ref[half, pl.ds(r0, R), :] = (w_ref[2:3, :] * das[half] + w_ref[1:2, :] * da1 + w_ref[0:1, :] * da2).astype(BF16)
            return tuple(out)

        sums = _chunks(T, R, step, ((jnp.zeros((8, CONV_COLS), F32),) * 4,) * 2, reverse=True)
        for half, (_, da_s, w_ref, dw_ref, db_ref) in enumerate(halves):
            for k in range(3):
                dw_ref[k:k + 1, :] = jnp.sum(sums[half][k], axis=0, keepdims=True)
            db_ref[...] = jnp.sum(sums[half][3], axis=0, keepdims=True)

    col = lambda r: pl.BlockSpec((r, CONV_COLS), lambda j: (0, j))
    return pl.pallas_call(
        body, name="conv_glu_bwd", grid=(nb,),
        in_specs=[pair, col(T), gate(3), up(3), gate(1), up(1)],
        out_specs=[pair, col(3), col(3), col(1), col(1)],
        out_shape=[jax.ShapeDtypeStruct((2, T, D_FF), BF16), jax.ShapeDtypeStruct((3, D_FF), F32), jax.ShapeDtypeStruct((3, D_FF), F32),
                   jax.ShapeDtypeStruct((1, D_FF), F32), jax.ShapeDtypeStruct((1, D_FF), F32)],
        scratch_shapes=[pltpu.VMEM((T + HALO, CONV_COLS), F32)] * 4,
        compiler_params=_params(("parallel",)),
    )(p3, ds, conv_w, conv_w, conv_b, conv_b)


def _ffn_down_loss(s, w_down, u1, ln1_g, ln1_b, ln2_g, ln2_b, target):
    T = u1.shape[0]
    tm = min(512, T)

    def body(s_ref, w_ref, u1_ref, g1_ref, b1_ref, g2_ref, b2_ref, t_ref, du_ref, dub_ref, dg_ref, db_ref, loss_ref):
        @pl.when(pl.program_id(0) == 0)
        def _():
            dg_ref[...] = jnp.zeros_like(dg_ref)
            db_ref[...] = jnp.zeros_like(db_ref)
            loss_ref[...] = jnp.zeros_like(loss_ref)

        xh1, _ = _ln_stats(u1_ref[...])
        x1 = xh1 * g1_ref[...] + b1_ref[...]
        u2 = ALPHA * x1 + _dot(s_ref[...], w_ref[...])
        xh2, rstd2 = _ln_stats(u2)
        err = xh2 * g2_ref[...] + b2_ref[...] - t_ref[...]
        per_token = jnp.mean(err * err, axis=-1, keepdims=True)
        loss_ref[...] += 0.5 * jnp.sum(per_token, axis=0, keepdims=True)
        dy = err * (1.0 / D_MODEL)
        dg_ref[...] += jnp.sum(dy * xh2, axis=0, keepdims=True)
        db_ref[...] += jnp.sum(dy, axis=0, keepdims=True)
        du2 = _ln_bwd(dy, xh2, rstd2, g2_ref[...])
        du_ref[...] = du2
        dub_ref[...] = du2.astype(BF16)

    row = lambda n: pl.BlockSpec((tm, n), lambda i: (i, 0))
    const = lambda r, n: pl.BlockSpec((r, n), lambda i: (0, 0))
    vec = const(1, D_MODEL)
    return pl.pallas_call(
        body, name="ffn_down_loss", grid=(T // tm,),
        in_specs=[row(D_FF), const(D_FF, D_MODEL), row(D_MODEL), vec, vec, vec, vec, row(D_MODEL)],
        out_specs=[row(D_MODEL), row(D_MODEL), vec, vec, const(1, LANES)],
        out_shape=[jax.ShapeDtypeStruct((T, D_MODEL), F32), jax.ShapeDtypeStruct((T, D_MODEL), BF16), jax.ShapeDtypeStruct((1, D_MODEL), F32),
                   jax.ShapeDtypeStruct((1, D_MODEL), F32), jax.ShapeDtypeStruct((1, LANES), F32)],
        compiler_params=_params(("arbitrary",)),
    )(s, w_down, u1, ln1_g, ln1_b, ln2_g, ln2_b, target)


def _ffn_up_bwd_ln1(dp3, w_up, du2, u1, ln1_g):
    T = u1.shape[0]
    tm = min(512, T)

    def body(dp_ref, w_ref, du2_ref, u1_ref, g_ref, du_ref, dub_ref, dg_ref, db_ref):
        @pl.when(pl.program_id(0) == 0)
        def _():
            dg_ref[...] = jnp.zeros_like(dg_ref)
            db_ref[...] = jnp.zeros_like(db_ref)

        dx1 = _dot(dp_ref[0], w_ref[:D_FF, :]) + _dot(dp_ref[1], w_ref[D_FF:, :]) + ALPHA * du2_ref[...]
        xh, rstd = _ln_stats(u1_ref[...])
        dg_ref[...] += jnp.sum(dx1 * xh, axis=0, keepdims=True)
        db_ref[...] += jnp.sum(dx1, axis=0, keepdims=True)
        du1 = _ln_bwd(dx1, xh, rstd, g_ref[...])
        du_ref[...] = du1
        dub_ref[...] = du1.astype(BF16)

    row = lambda n: pl.BlockSpec((tm, n), lambda i: (i, 0))
    const = lambda r, n: pl.BlockSpec((r, n), lambda i: (0, 0))
    vec = const(1, D_MODEL)
    return pl.pallas_call(
        body, name="ffn_up_bwd_ln1", grid=(T // tm,),
        in_specs=[pl.BlockSpec((2, tm, D_FF), lambda i: (0, i, 0)), const(2 * D_FF, D_MODEL), row(D_MODEL), row(D_MODEL), vec],
        out_specs=[row(D_MODEL), row(D_MODEL), vec, vec],
        out_shape=[jax.ShapeDtypeStruct((T, D_MODEL), F32), jax.ShapeDtypeStruct((T, D_MODEL), BF16), jax.ShapeDtypeStruct((1, D_MODEL), F32),
                   jax.ShapeDtypeStruct((1, D_MODEL), F32)],
        compiler_params=_params(("arbitrary",)),
    )(dp3, w_up, du2, u1, ln1_g)


def _local_step(x, positions, w_in, b_gate, sinks, ln1_g, ln1_b, conv_b, ln2_g, ln2_b, target, later_weights,
                early_exchange=None, tail_exchange=None):
    T = x.shape[0]
    inv_freq = 1.0 / (ROPE_THETA ** (jnp.arange(0, HEAD_DIM, 2, dtype=F32) / HEAD_DIM))
    cos, sin = _rope_tables(positions.reshape(T, 1), jnp.tile(inv_freq, LANES // (HEAD_DIM // 2)).reshape(1, LANES))

    xb, qa, ka, va, qb, kb, vb, gl = _in_proj(x, w_in)
    ya = _swa_fwd(qa, ka, va, cos, sin, sinks)
    if isinstance(later_weights, tuple):
        exchange, finish = later_weights
        (yb, probs, betas), arrived = _sb_fwd(qb, kb, vb, exchange)
        later_weights = finish(arrived)
    else:
        (yb, probs, betas), _ = _sb_fwd(qb, kb, vb)
    wa, wb, wo, w_up, conv_w, w_down = later_weights
    h, u1, x1 = _mix_fwd(ya, yb, gl, x, wa, wb, wo, b_gate, ln1_g, ln1_b)

    ff_tn = D_FF // 2
    nff = D_FF // ff_tn
    tm = min(1024, T)
    p3 = _matmul(x1, w_up, kind="nt", name="ffn_up", grid=(T // tm, 2 * nff),
                 a_spec=pl.BlockSpec((tm, D_MODEL), lambda i, j: (i, 0)), b_spec=pl.BlockSpec((ff_tn, D_MODEL), lambda i, j: (j, 0)),
                 out_spec=pl.BlockSpec((None, tm, ff_tn), lambda i, j: (j // nff, i, j % nff)),
                 out_shape=jax.ShapeDtypeStruct((2, T, D_FF), ACT_DTYPE))
    s = _conv_glu_fwd(p3, conv_w, conv_b)
    du2, du2b, dln2_g, dln2_b, loss = _ffn_down_loss(s, w_down, u1, ln1_g, ln1_b, ln2_g, ln2_b, target)

    ds = _matmul(du2b, w_down, kind="nt", name="ffn_down_bwd", grid=(T // tm, nff),
                 a_spec=pl.BlockSpec((tm, D_MODEL), lambda i, j: (i, 0)), b_spec=pl.BlockSpec((ff_tn, D_MODEL), lambda i, j: (j, 0)),
                 out_spec=pl.BlockSpec((tm, ff_tn), lambda i, j: (i, j)), out_shape=jax.ShapeDtypeStruct((T, D_FF), ACT_DTYPE))
    dp3, dcw_g, dcw_u, dcb_g, dcb_u = _conv_glu_bwd(p3, ds, conv_w, conv_b)
    tk = 256
    dw_down = _matmul(s, du2b, kind="tn", name="dw_down", grid=(D_FF // tk,),
                      a_spec=pl.BlockSpec((T, tk), lambda i: (0, i)), b_spec=pl.BlockSpec((T, D_MODEL), lambda i: (0, 0)),
                      out_spec=pl.BlockSpec((tk, D_MODEL), lambda i: (i, 0)), out_shape=jax.ShapeDtypeStruct((D_FF, D_MODEL), BF16))
    dw_up = _matmul(dp3, x1, kind="tn", name="dw_up", grid=(2 * nff,),
                    a_spec=pl.BlockSpec((None, T, ff_tn), lambda j: (j // nff, 0, j % nff)), b_spec=pl.BlockSpec((T, D_MODEL), lambda j: (0, 0)),
                    out_spec=pl.BlockSpec((ff_tn, D_MODEL), lambda j: (j, 0)), out_shape=jax.ShapeDtypeStruct((2 * D_FF, D_MODEL), BF16))
    du1, du1b, dln1_g, dln1_b = _ffn_up_bwd_ln1(dp3, w_up, du2, u1, ln1_g)
    dya, dyb, dgl, dta, dtb, db_gate = _mix_bwd(du1, ya, yb, gl, wa, wb, wo, b_gate)

    def dw_tn(a, g, name):
        rows, cols = a.shape[1], g.shape[1]
        tn = min(512, cols)
        return _matmul(a, g, kind="tn", name=name, grid=(rows // 512, cols // tn),
                       a_spec=pl.BlockSpec((T, 512), lambda i, j: (0, i)), b_spec=pl.BlockSpec((T, tn), lambda i, j: (0, j)),
                       out_spec=pl.BlockSpec((512, tn), lambda i, j: (i, j)), out_shape=jax.ShapeDtypeStruct((rows, cols), BF16))

    dwa = dw_tn(ya, dta, "dw_branch_a")
    dwb = dw_tn(yb, dtb, "dw_branch_b")
    dwo = dw_tn(h, du1b, "dw_out")

    grads = dict(
        b_gate=db_gate, w_branch_a=dwa, w_branch_b=dwb, w_out=dwo, ln1_g=dln1_g, ln1_b=dln1_b,
        w_up=dw_up, conv_w=jnp.concatenate([dcw_g, dcw_u], axis=1), conv_b=(dcb_g, dcb_u), w_down=dw_down, ln2_g=dln2_g, ln2_b=dln2_b)
    (dqb, dkb, dvb), early_out = _sb_bwd(qb, kb, vb, probs, betas, dyb, early_exchange(grads) if early_exchange else None)
    dqa, dka, dva, grads["sinks"] = _swa_bwd(qa, ka, va, cos, sin, sinks, dya)
    dproj = (dqa, dka, dva, dqb, dkb, dvb, dgl)
    grads["w_in"] = _dw_in(xb, dproj)
    grad_x, tail_out = _grad_x(dproj, w_in, du1, tail_exchange(grads, loss) if tail_exchange else None)
    return loss, grad_x, grads, early_out, tail_out


def _dw_in(xb, dproj):
    T = xb.shape[0]
    tn = 2 * LANES
    groups, start, k = [], 0, 0
    while k < len(IN_WIDTHS):
        if IN_WIDTHS[k] >= tn:
            groups.append((start, IN_WIDTHS[k] // tn, [(k, 0, tn)]))
            k += 1
        else:
            members, off = [], 0
            while off < tn:
                members.append((k, off, IN_WIDTHS[k]))
                off += IN_WIDTHS[k]
                k += 1
            groups.append((start, 1, members))
        start += groups[-1][1]

    def body(x_ref, *refs):
        pieces, o_ref = refs[:-1], refs[-1]
        j = pl.program_id(0)
        for first, steps, members in groups:
            @pl.when((j >= first) & (j < first + steps))
            def _(members=members):
                for k, off, width in members:
                    o_ref[off:off + width, :] = _dot_tn(pieces[k][...], x_ref[...]).astype(o_ref.dtype)

    specs = [None] * len(IN_WIDTHS)
    for first, steps, members in groups:
        for k, _, width in members:
            specs[k] = pl.BlockSpec((T, width), lambda j, first=first, steps=steps: (0, jnp.clip(j - first, 0, steps - 1)))
    return pl.pallas_call(
        body, name="dw_in", grid=(IN_TOTAL // tn,),
        in_specs=[pl.BlockSpec((T, D_MODEL), lambda j: (0, 0))] + specs, out_specs=pl.BlockSpec((tn, D_MODEL), lambda j: (j, 0)),
        out_shape=jax.ShapeDtypeStruct((IN_TOTAL, D_MODEL), BF16), compiler_params=_params(("arbitrary",)),
    )(xb, *dproj)


def _grad_x(dproj, w_in, du1, exchange=None):
    T = du1.shape[0]
    tm = min(512, T)
    offs = np.cumsum((0,) + IN_WIDTHS)

    def body(*refs):
        pieces, (w_ref, du_ref, o_ref) = refs[:len(IN_WIDTHS)], refs[len(IN_WIDTHS):]
        acc = ALPHA * du_ref[...]
        for p_ref, a, b in zip(pieces, offs[:-1], offs[1:]):
            acc = acc + _dot(p_ref[...].astype(BF16), w_ref[a:b, :])
        o_ref[...] = acc

    row = lambda n: pl.BlockSpec((tm, n), lambda i: (i, 0))
    (grad_x,), arrived = _hosted_call(
        body, "grad_x", (T // tm,), exchange,
        in_specs=[row(n) for n in IN_WIDTHS] + [pl.BlockSpec((IN_TOTAL, D_MODEL), lambda i: (0, 0)), row(D_MODEL)],
        out_specs=[row(D_MODEL)], out_shape=[jax.ShapeDtypeStruct((T, D_MODEL), F32)], semantics=("parallel",),
        args=(*dproj, w_in, du1))
    return grad_x, arrived


ANY = pl.BlockSpec(memory_space=pl.ANY)


def _all_gather(slabs, name):
    n = len(slabs)

    def body(*refs):
        ins, outs = refs[:n], refs[n:2 * n]
        send_sems, recv_sems, local_sems = refs[2 * n:]
        x, y, c = lax.axis_index("x"), lax.axis_index("y"), lax.axis_index("c")
        me, sibling = (x, y, c), (x, y, 1 - c)
        chips = [(1 - x, y), (x, 1 - y), (1 - x, 1 - y)]

        def slot(pos):
            return 4 * pos[0] + 2 * pos[1] + pos[2]

        def copy(a, k, block, to, from_input=False):
            return pltpu.make_async_remote_copy(
                src_ref=ins[a] if from_input else outs[a].at[slot(block)], dst_ref=outs[a].at[slot(block)],
                send_sem=send_sems.at[a, k], recv_sem=recv_sems.at[a, k], device_id=to, device_id_type=MESH)

        mine = [pltpu.make_async_copy(ins[a], outs[a].at[slot(me)], local_sems.at[a]) for a in range(n)]
        for cp in mine:
            cp.start()
        first = []
        for a in range(n):
            first.append(copy(a, 0, me, sibling, from_input=True))
            first += [copy(a, 1 + j, me, (*chip, c), from_input=True) for j, chip in enumerate(chips)]
        for cp in first:
            cp.start()
        passed = []
        for j, chip in enumerate(chips):
            for a in range(n):
                copy(a, 1 + j, (*chip, c), me).wait_recv()
                fwd = copy(a, 4 + j, (*chip, c), sibling)
                fwd.start()
                passed.append(fwd)
        for a in range(n):
            copy(a, 0, sibling, me).wait_recv()
            for j, chip in enumerate(chips):
                copy(a, 4 + j, (*chip, 1 - c), me).wait_recv()
        for cp in first + passed:
            cp.wait_send()
        for cp in mine:
            cp.wait()

    return pl.pallas_call(
        body, name=name,
        in_specs=[ANY] * n, out_specs=[ANY] * n,
        out_shape=[jax.ShapeDtypeStruct((N_DEV,) + s.shape, s.dtype) for s in slabs],
        scratch_shapes=[pltpu.SemaphoreType.DMA((n, 7)), pltpu.SemaphoreType.DMA((n, 7)), pltpu.SemaphoreType.DMA((n,))],
    )(*slabs)


def _row_tile(rows):
    for cand in range(256, 7, -8):
        if rows % cand == 0:
            return cand
    return rows


def _window(w):
    wp = max(-(-((w * r) % LANES + w) // LANES) for r in range(N_DEV)) * LANES
    assert all((w * r) // LANES * LANES + wp <= N_DEV * w for r in range(N_DEV))
    return wp


def _join_cols(slabs, name):
    _, R, w = slabs.shape
    tr = _row_tile(R)
    wp = _window(w)

    def body(g_ref, o_ref, pad_ref):
        if w % LANES == 0:
            for r in range(N_DEV):
                o_ref[:, w * r:w * (r + 1)] = g_ref[r]
            return
        o_ref[...] = jnp.zeros_like(o_ref)
        pad_ref[...] = jnp.zeros_like(pad_ref)
        for r in range(N_DEV):
            q, s = divmod(w * r, LANES)
            pad_ref[:, :w] = g_ref[r]
            y = pad_ref[...]
            if s:
                y = pltpu.roll(y, s, axis=1)
            o_ref[:, LANES * q:LANES * q + wp] += y

    return pl.pallas_call(
        body, name=name, grid=(R // tr,),
        in_specs=[pl.BlockSpec((N_DEV, tr, w), lambda i: (0, i, 0))], out_specs=pl.BlockSpec((tr, N_DEV * w), lambda i: (i, 0)),
        out_shape=jax.ShapeDtypeStruct((R, N_DEV * w), slabs.dtype), scratch_shapes=[pltpu.VMEM((tr, wp), slabs.dtype)],
        compiler_params=_params(("parallel",)),
    )(slabs)


def _split_cols(pieces, name):
    R = pieces[0].shape[0]
    widths = [p.shape[1] for p in pieces]
    total = sum(widths)
    w = total // N_DEV
    tr = _row_tile(R)
    wp = _window(w)
    offs = np.cumsum([0] + widths)
    dtype = pieces[0].dtype

    def body(*refs):
        ins, (o_ref, full_ref) = refs[:len(pieces)], refs[len(pieces):]
        for p_ref, a, b in zip(ins, offs[:-1], offs[1:]):
            full_ref[:, a:b] = p_ref[...].astype(dtype)
        for r in range(N_DEV):
            q, s = divmod(w * r, LANES)
            y = full_ref[:, LANES * q:LANES * q + wp]
            if s:
                y = pltpu.roll(y, wp - s, axis=1)
            o_ref[r] = y[:, :w]

    return pl.pallas_call(
        body, name=name, grid=(R // tr,),
        in_specs=[pl.BlockSpec((tr, n), lambda i: (i, 0)) for n in widths], out_specs=pl.BlockSpec((N_DEV, tr, w), lambda i: (0, i, 0)),
        out_shape=jax.ShapeDtypeStruct((N_DEV, R, w), dtype), scratch_shapes=[pltpu.VMEM((tr, total), dtype)],
        compiler_params=_params(("parallel",)),
    )(*pieces)


def _adamw(g, w, m, v):
    m_new = ADAM_B1 * m + (1.0 - ADAM_B1) * g
    v_new = ADAM_B2 * v + (1.0 - ADAM_B2) * jnp.square(g)
    m_hat = m_new / (1.0 - ADAM_B1 ** ADAM_STEP)
    v_hat = v_new / (1.0 - ADAM_B2 ** ADAM_STEP)
    return -ADAM_LR * (m_hat / (jnp.sqrt(v_hat) + ADAM_EPS) + ADAM_WD * w), m_new, v_new


def _sum_parts(p_ref):
    g = p_ref[0].astype(F32)
    for d in range(1, p_ref.shape[0]):
        g = g + p_ref[d].astype(F32)
    return g


def _pair_swap(slabs, name):
    def body(in_ref, out_ref, send_sems, recv_sems):
        x, y, c = lax.axis_index("x"), lax.axis_index("y"), lax.axis_index("c")
        copies = [pltpu.make_async_remote_copy(src_ref=in_ref.at[2 * q + (1 - c)], dst_ref=out_ref.at[q], send_sem=send_sems.at[q],
                                               recv_sem=recv_sems.at[q], device_id=(x, y, 1 - c), device_id_type=MESH) for q in range(4)]
        for cp in copies:
            cp.start()
        for cp in copies:
            cp.wait()

    return pl.pallas_call(
        body, name=name, in_specs=[ANY], out_specs=ANY, out_shape=jax.ShapeDtypeStruct((4,) + slabs.shape[1:], slabs.dtype),
        scratch_shapes=[pltpu.SemaphoreType.DMA((4,)), pltpu.SemaphoreType.DMA((4,))],
    )(slabs)


def _pair_add(slabs, received, core, name):
    R, C = slabs.shape[1:]
    tr = R

    def body(c_ref, a_ref, b_ref, o_ref):
        o_ref[...] = (a_ref[...].astype(F32) + b_ref[...].astype(F32)).astype(o_ref.dtype)

    same = pl.BlockSpec((None, tr, C), lambda q, i, c_ref: (q, i, 0))
    grid_spec = pltpu.PrefetchScalarGridSpec(
        num_scalar_prefetch=1, grid=(4, R // tr),
        in_specs=[pl.BlockSpec((None, tr, C), lambda q, i, c_ref: (2 * q + c_ref[0], i, 0)), same], out_specs=same)
    return pl.pallas_call(body, name=name, grid_spec=grid_spec, out_shape=jax.ShapeDtypeStruct((4, R, C), slabs.dtype),
                          compiler_params=_params(("parallel", "parallel")))(core, slabs, received)


def _reduce_adamw(parts, w, m, v, name):
    R, C = w.shape
    tr = _row_tile(R)

    def body(p_ref, w_ref, m_ref, v_ref, g_ref, d_ref, mo_ref, vo_ref):
        g = _sum_parts(p_ref)
        g_ref[...] = g
        d_ref[...], mo_ref[...], vo_ref[...] = _adamw(g, w_ref[...], m_ref[...], v_ref[...])

    row = pl.BlockSpec((tr, C), lambda i: (i, 0))
    return pl.pallas_call(
        body, name=name, grid=(R // tr,),
        in_specs=[pl.BlockSpec((parts.shape[0], tr, C), lambda i: (0, i, 0)), row, row, row],
        out_specs=[row] * 4, out_shape=[jax.ShapeDtypeStruct((R, C), F32)] * 4,
        compiler_params=_params(("parallel",)),
    )(parts, w, m, v)


def _reduce_adamw_small(parts, ws, ms, vs):
    sizes = [a.shape[1] for a in ws]
    k = len(sizes)
    offs = np.cumsum([0] + [-(-n // LANES) * LANES for n in sizes])

    def body(*refs):
        p_ref, w_refs, m_refs, v_refs = refs[0], refs[1:1 + k], refs[1 + k:1 + 2 * k], refs[1 + 2 * k:1 + 3 * k]
        outs, loss_ref = refs[1 + 3 * k:-1], refs[-1]
        g_all = _sum_parts(p_ref)
        for j, n in enumerate(sizes):
            g = g_all[:, offs[j]:offs[j] + LANES * (-(-n // LANES))][:, :n]
            outs[4 * j][...] = g
            outs[4 * j + 1][...], outs[4 * j + 2][...], outs[4 * j + 3][...] = _adamw(g, w_refs[j][...], m_refs[j][...], v_refs[j][...])
        loss_ref[...] = g_all[:, offs[k]:offs[k] + LANES]

    vm = pl.BlockSpec(memory_space=pltpu.VMEM)
    out_shape = [jax.ShapeDtypeStruct((1, n), F32) for n in sizes for _ in range(4)] + [jax.ShapeDtypeStruct((1, LANES), F32)]
    res = pl.pallas_call(
        body, name="reduce_adamw_replicated", in_specs=[vm] * (1 + 3 * k), out_specs=[vm] * len(out_shape), out_shape=out_shape,
        compiler_params=_params(),
    )(parts, *ws, *ms, *vs)
    return [res[4 * j:4 * j + 4] for j in range(k)], res[-1]


TRANSPOSED = ("w_in", "w_up")
COL_SHARDED = ("w_branch_a", "w_branch_b", "conv_w")
ROW_SHARDED = TRANSPOSED + ("w_out", "w_down")
SMALL = ("b_gate", "sinks", "ln1_g", "ln1_b", "conv_b", "ln2_g", "ln2_b")
ORDER = ("w_in", "b_gate", "sinks", "w_branch_a", "w_branch_b", "w_out", "ln1_g", "ln1_b", "w_up", "conv_w", "conv_b", "w_down", "ln2_g", "ln2_b")


def _pad_lanes(a):
    pad = (-a.shape[-1]) % LANES
    return a if pad == 0 else jnp.pad(a, ((0, 0), (0, pad)))


def kernel(x, positions, w_in, b_gate, sinks, w_branch_a, w_branch_b, w_out, ln1_g, ln1_b, w_up, conv_w, conv_b, w_down, ln2_g, ln2_b, loss_target, m_w_in, m_b_gate, m_sinks, m_w_branch_a, m_w_branch_b, m_w_out, m_ln1_g, m_ln1_b, m_w_up, m_conv_w, m_conv_b, m_w_down, m_ln2_g, m_ln2_b, v_w_in, v_b_gate, v_sinks, v_w_branch_a, v_w_branch_b, v_w_out, v_ln1_g, v_ln1_b, v_w_up, v_conv_w, v_conv_b, v_w_down, v_ln2_g, v_ln2_b):
    args = dict(locals())
    sharded = COL_SHARDED + ROW_SHARDED

    def shard(name, a):
        return a if name not in sharded else a[0].T if name in TRANSPOSED else a[0]

    w = {n: shard(n, args[n]) for n in ORDER}
    m = {n: shard(n, args["m_" + n]) for n in ORDER}
    v = {n: shard(n, args["v_" + n]) for n in ORDER}

    travel = {n: (w[n] if n == "conv_w" else w[n].astype(BF16)) for n in sharded}
    (g_in,) = _all_gather([travel["w_in"]], "all_gather_w_in")
    w_in_full = g_in.reshape(-1, g_in.shape[-1])
    later = ("w_branch_a", "w_branch_b", "w_out", "w_up", "conv_w", "w_down")

    def join(name, slabs):
        return _join_cols(slabs, "join_" + name) if name in COL_SHARDED else slabs.reshape(-1, slabs.shape[-1])

    def split(name, grad):
        if name in COL_SHARDED:
            return _split_cols(grad if isinstance(grad, tuple) else (grad,), "split_d" + name)
        return grad.reshape((N_DEV, -1, grad.shape[-1]))

    def early_exchange(grads):
        return _Exchange([split(n, grads[n]) for n in later], ["scatter"] * len(later))

    def tail_exchange(grads, loss):
        small_pack = jnp.concatenate(
            [_pad_lanes(p) for n in SMALL for p in (grads[n] if isinstance(grads[n], tuple) else (grads[n],))] + [loss], axis=1)
        slabs = split("w_in", grads["w_in"])
        core = lax.axis_index("c").astype(jnp.int32).reshape(1)
        chip_sums = _pair_add(slabs, _pair_swap(slabs, "pair_swap_dw_in"), core, "pair_add_dw_in")
        return _Exchange([chip_sums, small_pack], ["chips", "gather"])

    gather_later = _Exchange([travel[n] for n in later], ["gather"] * len(later))
    _, grad_x, _, early_out, (recv_w_in, small_parts) = _local_step(
        x[0], positions[0], w_in_full, w["b_gate"], w["sinks"][0], w["ln1_g"], w["ln1_b"], w["conv_b"], w["ln2_g"], w["ln2_b"], loss_target[0],
        (gather_later, lambda arrived: [join(n, a) for n, a in zip(later, arrived)]), early_exchange, tail_exchange)
    recv = dict(zip(later, early_out), w_in=recv_w_in)

    res = {n: _reduce_adamw(recv[n], w[n], m[n], v[n], "reduce_adamw_" + n) for n in sharded}
    small_res, loss_sum = _reduce_adamw_small(small_parts, [w[n] for n in SMALL], [m[n] for n in SMALL], [v[n] for n in SMALL])
    res.update(zip(SMALL, small_res))
    out = [loss_sum[0, 0], grad_x[None]]
    for k in range(4):
        out += [res[n][k].T[None] if n in TRANSPOSED else res[n][k][None] if n in sharded else res[n][k] for n in ORDER]
    return tuple(out)
```
